```python
import math
import jax, jax.numpy as jnp
from jax import lax
import numpy as np

D_MODEL = 2048
BATCH = 8
SEQ = 2048
DEPTH = 1

POOL_WIDTH = D_MODEL // 2
POOL_WINDOWS = (2, 4, 8, 16)
POOL_GROUPS = len(POOL_WINDOWS)
POOL_GROUP_DIM = POOL_WIDTH // POOL_GROUPS
SB_HEAD_DIM = 128
SB_HEADS = (D_MODEL // 2) // SB_HEAD_DIM
SB_WIDTH = SB_HEADS * SB_HEAD_DIM
N_BRANCHES = 2
IN_WIDTH = POOL_WIDTH + 3 * SB_WIDTH + N_BRANCHES * D_MODEL
D_FF = 4 * D_MODEL
Q_BLOCK = 128
N_MOD = 6
EPS = 1e-6

kernel_name = "hybrid_pool_stickbreak_gated_block"


def rms_norm(x, w):
    xf = x.astype(jnp.float32)
    y = xf * lax.rsqrt(jnp.mean(jnp.square(xf), axis=-1, keepdims=True) + EPS)
    return (y * w.astype(jnp.float32)).astype(x.dtype)


def multiscale_pool(u, w_pool, pool_scale):
    B, S, _ = u.shape
    uf = u.astype(jnp.float32).reshape(B, S, POOL_GROUPS, POOL_GROUP_DIM)
    cs = jnp.cumsum(uf, axis=1)
    pos = jnp.arange(S, dtype=jnp.int32)
    outs = []
    for g, w in enumerate(POOL_WINDOWS):
        csg = cs[:, :, g]
        lag = jnp.pad(csg, ((0, 0), (w, 0), (0, 0)))[:, :S]
        count = jnp.minimum(pos + 1, w).astype(jnp.float32)[None, :, None]
        outs.append((csg - lag) / count - uf[:, :, g])
    pooled = jnp.stack(outs, axis=2)
    mixed = jnp.einsum('bsgc,gce->bsge', pooled, w_pool.astype(jnp.float32))
    y = mixed.reshape(B, S, POOL_WIDTH) * pool_scale.astype(jnp.float32)
    return y.astype(u.dtype)


def stick_breaking_attention(q, k, v):
    B, S, H, Dh = q.shape
    nb = S // Q_BLOCK
    scale = 1.0 / math.sqrt(Dh)
    kh = k.transpose(0, 2, 1, 3)
    vh = v.transpose(0, 2, 1, 3)
    qb = q.transpose(0, 2, 1, 3).reshape(B, H, nb, Q_BLOCK, Dh).transpose(2, 0, 1, 3, 4)
    starts = jnp.arange(nb, dtype=jnp.int32) * Q_BLOCK
    key_pos = jnp.arange(S, dtype=jnp.int32)

    def block(args):
        q_blk, t0 = args
        z = jnp.einsum('bhqd,bhkd->bhqk', q_blk, kh).astype(jnp.float32) * scale
        q_pos = t0 + jnp.arange(Q_BLOCK, dtype=jnp.int32)
        mask = key_pos[None, :] < q_pos[:, None]
        log_beta = jax.nn.log_sigmoid(z)
        log_1m_beta = log_beta - z
        l = jnp.where(mask, log_1m_beta, 0.0)
        suffix = lax.cumsum(l, axis=3, reverse=True) - l
        a = jnp.where(mask, jnp.exp(log_beta + suffix), 0.0)
        return jnp.einsum('bhqk,bhkd->bhqd', a.astype(vh.dtype), vh)

    out = lax.map(block, (qb, starts))
    return out.transpose(1, 0, 3, 2, 4).reshape(B, S, H * Dh)


def _fwd_setup_inputs(seed: int = 0) -> dict:
    key = jax.random.key(seed)
    ks = jax.random.split(key, 20)
    f32 = jnp.float32
    L = DEPTH

    def nrm(k, shape, fan_in, gain=1.0):
        return jax.random.normal(k, shape, f32) * (gain * fan_in ** -0.5)

    return {
        "x": jax.random.normal(ks[0], (BATCH, SEQ, D_MODEL), f32),
        "c": jax.random.normal(ks[1], (BATCH, D_MODEL), f32),
        "w_ada": nrm(ks[2], (L, D_MODEL, N_MOD * D_MODEL), D_MODEL, 0.5),
        "b_ada": 0.02 * jax.random.normal(ks[3], (L, N_MOD * D_MODEL), f32),
        "norm1_w": 1.0 + 0.05 * jax.random.normal(ks[4], (L, D_MODEL), f32),
        "w_in": nrm(ks[5], (L, D_MODEL, IN_WIDTH), D_MODEL),
        "q_norm_w": 1.0 + 0.05 * jax.random.normal(ks[6], (L, SB_HEAD_DIM), f32),
        "k_norm_w": 1.0 + 0.05 * jax.random.normal(ks[7], (L, SB_HEAD_DIM), f32),
        "w_pool": nrm(ks[8], (L, POOL_GROUPS, POOL_GROUP_DIM, POOL_GROUP_DIM), POOL_GROUP_DIM),
        "pool_scale": 1.0 + 0.1 * jax.random.normal(ks[9], (L, POOL_WIDTH), f32),
        "w_a_up": nrm(ks[10], (L, POOL_WIDTH, D_MODEL), POOL_WIDTH),
        "w_b_up": nrm(ks[11], (L, SB_WIDTH, D_MODEL), SB_WIDTH),
        "w_o": nrm(ks[12], (L, D_MODEL, D_MODEL), D_MODEL),
        "norm2_w": 1.0 + 0.05 * jax.random.normal(ks[13], (L, D_MODEL), f32),
        "w_ff1": nrm(ks[14], (L, D_MODEL, D_FF), D_MODEL),
        "w_ff2": nrm(ks[15], (L, D_FF, D_MODEL), D_FF),
    }


def _fwd_reference(x, c, w_ada, b_ada, norm1_w, w_in, q_norm_w, k_norm_w, w_pool, pool_scale,
              w_a_up, w_b_up, w_o, norm2_w, w_ff1, w_ff2):
    B, S, D = x.shape
    split_at = [POOL_WIDTH, POOL_WIDTH + SB_WIDTH, POOL_WIDTH + 2 * SB_WIDTH,
                POOL_WIDTH + 3 * SB_WIDTH, POOL_WIDTH + 3 * SB_WIDTH + D_MODEL]
    for l in range(DEPTH):
        mod = jax.nn.silu(c) @ w_ada[l] + b_ada[l]
        shift1, scale1, gate1, shift2, scale2, gate2 = jnp.split(mod, N_MOD, axis=-1)

        h = rms_norm(x, norm1_w[l]) * (1.0 + scale1[:, None]) + shift1[:, None]
        proj = h @ w_in[l]
        u_pool, q, k, v, g_a, g_b = jnp.split(proj, split_at, axis=-1)

        y_a = multiscale_pool(u_pool, w_pool[l], pool_scale[l]) @ w_a_up[l]

        q = rms_norm(q.reshape(B, S, SB_HEADS, SB_HEAD_DIM), q_norm_w[l])
        k = rms_norm(k.reshape(B, S, SB_HEADS, SB_HEAD_DIM), k_norm_w[l])
        v = v.reshape(B, S, SB_HEADS, SB_HEAD_DIM)
        y_b = stick_breaking_attention(q, k, v) @ w_b_up[l]

        merged = jax.nn.sigmoid(g_a) * y_a + jax.nn.sigmoid(g_b) * y_b
        x = x + gate1[:, None] * (merged @ w_o[l])

        h2 = rms_norm(x, norm2_w[l]) * (1.0 + scale2[:, None]) + shift2[:, None]
        f = jnp.square(jax.nn.relu(h2 @ w_ff1[l])) @ w_ff2[l]
        x = x + gate2[:, None] * f
    return x


import jax as _jax
import jax.numpy as _jnp

TWIN_FORMAT = 'train_step'
FWD_PARAMS = ['x', 'c', 'w_ada', 'b_ada', 'norm1_w', 'w_in', 'q_norm_w', 'k_norm_w', 'w_pool', 'pool_scale', 'w_a_up', 'w_b_up', 'w_o', 'norm2_w', 'w_ff1', 'w_ff2']
TWIN_WEIGHTS = ['w_ada', 'b_ada', 'norm1_w', 'w_in', 'q_norm_w', 'k_norm_w', 'w_pool', 'pool_scale', 'w_a_up', 'w_b_up', 'w_o', 'norm2_w', 'w_ff1', 'w_ff2']
TWIN_DIFF_INPUT = 'x'
TWIN_INPUTS = ['x', 'c', 'w_ada', 'b_ada', 'norm1_w', 'w_in', 'q_norm_w', 'k_norm_w', 'w_pool', 'pool_scale', 'w_a_up', 'w_b_up', 'w_o', 'norm2_w', 'w_ff1', 'w_ff2', 'loss_target', 'm_w_ada', 'm_b_ada', 'm_norm1_w', 'm_w_in', 'm_q_norm_w', 'm_k_norm_w', 'm_w_pool', 'm_pool_scale', 'm_w_a_up', 'm_w_b_up', 'm_w_o', 'm_norm2_w', 'm_w_ff1', 'm_w_ff2', 'v_w_ada', 'v_b_ada', 'v_norm1_w', 'v_w_in', 'v_q_norm_w', 'v_k_norm_w', 'v_w_pool', 'v_pool_scale', 'v_w_a_up', 'v_w_b_up', 'v_w_o', 'v_norm2_w', 'v_w_ff1', 'v_w_ff2']
TWIN_OUTPUTS = ['loss', 'grad_x', 'grad_w_ada', 'grad_b_ada', 'grad_norm1_w', 'grad_w_in', 'grad_q_norm_w', 'grad_k_norm_w', 'grad_w_pool', 'grad_pool_scale', 'grad_w_a_up', 'grad_w_b_up', 'grad_w_o', 'grad_norm2_w', 'grad_w_ff1', 'grad_w_ff2', 'delta_w_ada', 'delta_b_ada', 'delta_norm1_w', 'delta_w_in', 'delta_q_norm_w', 'delta_k_norm_w', 'delta_w_pool', 'delta_pool_scale', 'delta_w_a_up', 'delta_w_b_up', 'delta_w_o', 'delta_norm2_w', 'delta_w_ff1', 'delta_w_ff2', 'new_m_w_ada', 'new_m_b_ada', 'new_m_norm1_w', 'new_m_w_in', 'new_m_q_norm_w', 'new_m_k_norm_w', 'new_m_w_pool', 'new_m_pool_scale', 'new_m_w_a_up', 'new_m_w_b_up', 'new_m_w_o', 'new_m_norm2_w', 'new_m_w_ff1', 'new_m_w_ff2', 'new_v_w_ada', 'new_v_b_ada', 'new_v_norm1_w', 'new_v_w_in', 'new_v_q_norm_w', 'new_v_k_norm_w', 'new_v_w_pool', 'new_v_pool_scale', 'new_v_w_a_up', 'new_v_w_b_up', 'new_v_w_o', 'new_v_norm2_w', 'new_v_w_ff1', 'new_v_w_ff2']
TWIN_LEAF_KINDS = {'loss': 'loss', 'grad_x': 'grad_x', 'grad_w_ada': 'grad_w', 'grad_b_ada': 'grad_w', 'grad_norm1_w': 'grad_w', 'grad_w_in': 'grad_w', 'grad_q_norm_w': 'grad_w', 'grad_k_norm_w': 'grad_w', 'grad_w_pool': 'grad_w', 'grad_pool_scale': 'grad_w', 'grad_w_a_up': 'grad_w', 'grad_w_b_up': 'grad_w', 'grad_w_o': 'grad_w', 'grad_norm2_w': 'grad_w', 'grad_w_ff1': 'grad_w', 'grad_w_ff2': 'grad_w', 'delta_w_ada': 'delta_w', 'delta_b_ada': 'delta_w', 'delta_norm1_w': 'delta_w', 'delta_w_in': 'delta_w', 'delta_q_norm_w': 'delta_w', 'delta_k_norm_w': 'delta_w', 'delta_w_pool': 'delta_w', 'delta_pool_scale': 'delta_w', 'delta_w_a_up': 'delta_w', 'delta_w_b_up': 'delta_w', 'delta_w_o': 'delta_w', 'delta_norm2_w': 'delta_w', 'delta_w_ff1': 'delta_w', 'delta_w_ff2': 'delta_w', 'new_m_w_ada': 'new_m', 'new_m_b_ada': 'new_m', 'new_m_norm1_w': 'new_m', 'new_m_w_in': 'new_m', 'new_m_q_norm_w': 'new_m', 'new_m_k_norm_w': 'new_m', 'new_m_w_pool': 'new_m', 'new_m_pool_scale': 'new_m', 'new_m_w_a_up': 'new_m', 'new_m_w_b_up': 'new_m', 'new_m_w_o': 'new_m', 'new_m_norm2_w': 'new_m', 'new_m_w_ff1': 'new_m', 'new_m_w_ff2': 'new_m', 'new_v_w_ada': 'new_v', 'new_v_b_ada': 'new_v', 'new_v_norm1_w': 'new_v', 'new_v_w_in': 'new_v', 'new_v_q_norm_w': 'new_v', 'new_v_k_norm_w': 'new_v', 'new_v_w_pool': 'new_v', 'new_v_pool_scale': 'new_v', 'new_v_w_a_up': 'new_v', 'new_v_w_b_up': 'new_v', 'new_v_w_o': 'new_v', 'new_v_norm2_w': 'new_v', 'new_v_w_ff1': 'new_v', 'new_v_w_ff2': 'new_v'}


def _forward(args):
    return _fwd_reference(*[args[k] for k in FWD_PARAMS])


def _output_shape():
    out = _jax.eval_shape(lambda: _forward(_fwd_setup_inputs(0)))
    return out.shape, out.dtype

N_MICROBATCH = 1
ADAM_LR = 0.001
ADAM_B1 = 0.9
ADAM_B2 = 0.999
ADAM_EPS = 1e-08
ADAM_WD = 0.01
ADAM_STEP = 10
PER_EXAMPLE_BATCH_AXIS = {'x': 0, 'c': 0, 'loss_target': 0}
SHARED_INPUTS = []
_WEIGHT_DTYPES = {'w_ada': _jnp.float32, 'b_ada': _jnp.float32, 'norm1_w': _jnp.float32, 'w_in': _jnp.float32, 'q_norm_w': _jnp.float32, 'k_norm_w': _jnp.float32, 'w_pool': _jnp.float32, 'pool_scale': _jnp.float32, 'w_a_up': _jnp.float32, 'w_b_up': _jnp.float32, 'w_o': _jnp.float32, 'norm2_w': _jnp.float32, 'w_ff1': _jnp.float32, 'w_ff2': _jnp.float32}
MOMENT_SCALE = {'w_ada': 8.190386e-01, 'b_ada': 1.766148e+00, 'norm1_w': 3.273810e-01, 'w_in': 2.799125e-02, 'q_norm_w': 2.299808e-01, 'k_norm_w': 2.330990e-01, 'w_pool': 3.498743e-02, 'pool_scale': 3.759913e-01, 'w_a_up': 2.159024e-02, 'w_b_up': 5.046390e-02, 'w_o': 5.021914e-02, 'norm2_w': 3.052709e+00, 'w_ff1': 9.278782e-02, 'w_ff2': 3.585806e-01}


def _to_microbatches(a, axis):
    t = _jnp.moveaxis(a, axis, 0)
    t = t.reshape((N_MICROBATCH, t.shape[0] // N_MICROBATCH) + t.shape[1:])
    return _jnp.moveaxis(t, 1, axis + 1)


def setup_inputs(seed: int = 0) -> dict:
    inp = _fwd_setup_inputs(seed)
    key = _jax.random.fold_in(_jax.random.key(seed), 7919)
    shape, _ = _output_shape()
    out = dict(inp)
    out["loss_target"] = _jax.random.normal(_jax.random.fold_in(key, 0), shape, _jnp.float32)
    for i, name in enumerate(TWIN_WEIGHTS):
        w = inp[name].astype(_jnp.float32)
        if MOMENT_SCALE is None:
            s = _jnp.sqrt(_jnp.mean(_jnp.square(w)) + 1e-30)
        else:
            s = MOMENT_SCALE[name]
        km, kv = _jax.random.split(_jax.random.fold_in(key, i + 1))
        out[name] = w
        out["m_" + name] = s * _jax.random.normal(km, w.shape, _jnp.float32)
        out["v_" + name] = (s * s) * _jax.random.uniform(kv, w.shape, _jnp.float32, 0.5, 1.5)
    if N_MICROBATCH > 1:
        for name, axis in PER_EXAMPLE_BATCH_AXIS.items():
            out[name] = _to_microbatches(out[name], axis)
    return {'x': out['x'], 'c': out['c'], 'w_ada': out['w_ada'], 'b_ada': out['b_ada'], 'norm1_w': out['norm1_w'], 'w_in': out['w_in'], 'q_norm_w': out['q_norm_w'], 'k_norm_w': out['k_norm_w'], 'w_pool': out['w_pool'], 'pool_scale': out['pool_scale'], 'w_a_up': out['w_a_up'], 'w_b_up': out['w_b_up'], 'w_o': out['w_o'], 'norm2_w': out['norm2_w'], 'w_ff1': out['w_ff1'], 'w_ff2': out['w_ff2'], 'loss_target': out['loss_target'], 'm_w_ada': out['m_w_ada'], 'm_b_ada': out['m_b_ada'], 'm_norm1_w': out['m_norm1_w'], 'm_w_in': out['m_w_in'], 'm_q_norm_w': out['m_q_norm_w'], 'm_k_norm_w': out['m_k_norm_w'], 'm_w_pool': out['m_w_pool'], 'm_pool_scale': out['m_pool_scale'], 'm_w_a_up': out['m_w_a_up'], 'm_w_b_up': out['m_w_b_up'], 'm_w_o': out['m_w_o'], 'm_norm2_w': out['m_norm2_w'], 'm_w_ff1': out['m_w_ff1'], 'm_w_ff2': out['m_w_ff2'], 'v_w_ada': out['v_w_ada'], 'v_b_ada': out['v_b_ada'], 'v_norm1_w': out['v_norm1_w'], 'v_w_in': out['v_w_in'], 'v_q_norm_w': out['v_q_norm_w'], 'v_k_norm_w': out['v_k_norm_w'], 'v_w_pool': out['v_w_pool'], 'v_pool_scale': out['v_pool_scale'], 'v_w_a_up': out['v_w_a_up'], 'v_w_b_up': out['v_w_b_up'], 'v_w_o': out['v_w_o'], 'v_norm2_w': out['v_norm2_w'], 'v_w_ff1': out['v_w_ff1'], 'v_w_ff2': out['v_w_ff2']}


def _loss(weights, diff, rest, loss_target):
    with _jax.named_scope("forward"):
        args = {**rest, TWIN_DIFF_INPUT: diff, **{k: w.astype(_WEIGHT_DTYPES[k]) for k, w in weights.items()}}
        y = _forward(args)
    with _jax.named_scope("loss_head"):
        err = _jnp.square(y.astype(_jnp.float32) - loss_target)
        return 0.5 * _jnp.sum(_jnp.mean(err, axis=-1)) if err.ndim else 0.5 * err


def _adamw(w, g, m, v):
    m = ADAM_B1 * m + (1.0 - ADAM_B1) * g
    v = ADAM_B2 * v + (1.0 - ADAM_B2) * _jnp.square(g)
    m_hat = m / (1.0 - ADAM_B1 ** ADAM_STEP)
    v_hat = v / (1.0 - ADAM_B2 ** ADAM_STEP)
    delta = -ADAM_LR * (m_hat / (_jnp.sqrt(v_hat) + ADAM_EPS) + ADAM_WD * w)
    return delta, m, v


def reference(x, c, w_ada, b_ada, norm1_w, w_in, q_norm_w, k_norm_w, w_pool, pool_scale, w_a_up, w_b_up, w_o, norm2_w, w_ff1, w_ff2, loss_target, m_w_ada, m_b_ada, m_norm1_w, m_w_in, m_q_norm_w, m_k_norm_w, m_w_pool, m_pool_scale, m_w_a_up, m_w_b_up, m_w_o, m_norm2_w, m_w_ff1, m_w_ff2, v_w_ada, v_b_ada, v_norm1_w, v_w_in, v_q_norm_w, v_k_norm_w, v_w_pool, v_pool_scale, v_w_a_up, v_w_b_up, v_w_o, v_norm2_w, v_w_ff1, v_w_ff2):
    given = dict(x=x, c=c, w_ada=w_ada, b_ada=b_ada, norm1_w=norm1_w, w_in=w_in, q_norm_w=q_norm_w, k_norm_w=k_norm_w, w_pool=w_pool, pool_scale=pool_scale, w_a_up=w_a_up, w_b_up=w_b_up, w_o=w_o, norm2_w=norm2_w, w_ff1=w_ff1, w_ff2=w_ff2, loss_target=loss_target, m_w_ada=m_w_ada, m_b_ada=m_b_ada, m_norm1_w=m_norm1_w, m_w_in=m_w_in, m_q_norm_w=m_q_norm_w, m_k_norm_w=m_k_norm_w, m_w_pool=m_w_pool, m_pool_scale=m_pool_scale, m_w_a_up=m_w_a_up, m_w_b_up=m_w_b_up, m_w_o=m_w_o, m_norm2_w=m_norm2_w, m_w_ff1=m_w_ff1, m_w_ff2=m_w_ff2, v_w_ada=v_w_ada, v_b_ada=v_b_ada, v_norm1_w=v_norm1_w, v_w_in=v_w_in, v_q_norm_w=v_q_norm_w, v_k_norm_w=v_k_norm_w, v_w_pool=v_w_pool, v_pool_scale=v_pool_scale, v_w_a_up=v_w_a_up, v_w_b_up=v_w_b_up, v_w_o=v_w_o, v_norm2_w=v_norm2_w, v_w_ff1=v_w_ff1, v_w_ff2=v_w_ff2)
    weights = {n: given[n] for n in TWIN_WEIGHTS}
    shared = {n: given[n] for n in SHARED_INPUTS}
    per_example = {n: given[n] for n in ['x', 'c']}
    grad_fn = _jax.value_and_grad(_loss, argnums=(0, 1))

    def one_microbatch(ex, loss_target):
        ex = dict(ex)
        diff = ex.pop(TWIN_DIFF_INPUT)
        return grad_fn(weights, diff, {**shared, **ex}, loss_target)

    if N_MICROBATCH == 1:
        loss, (grad_w, grad_x) = one_microbatch(per_example, given["loss_target"])
    else:
        def body(carry, xs):
            loss_sum, grad_sum = carry
            l_k, (gw_k, gx_k) = one_microbatch(xs[0], xs[1])
            with _jax.named_scope("update"):
                return (loss_sum + l_k, _jax.tree.map(_jnp.add, grad_sum, gw_k)), gx_k

        init = (_jnp.zeros((), _jnp.float32), _jax.tree.map(_jnp.zeros_like, weights))
        (loss, grad_w), grad_x = _jax.lax.scan(body, init, (per_example, given["loss_target"]))
    with _jax.named_scope("update"):
        delta_w, new_m, new_v = {}, {}, {}
        for n in TWIN_WEIGHTS:
            delta_w[n], new_m[n], new_v[n] = _adamw(weights[n], grad_w[n], given["m_" + n], given["v_" + n])
    return (loss, grad_x, *[grad_w[n] for n in TWIN_WEIGHTS], *[delta_w[n] for n in TWIN_WEIGHTS],
            *[new_m[n] for n in TWIN_WEIGHTS], *[new_v[n] for n in TWIN_WEIGHTS])
```

```python
import math

import jax
import jax.numpy as jnp
from jax import lax
from jax.experimental import pallas as pl
from jax.experimental.pallas import tpu as pltpu

F32 = jnp.float32
BF16 = jnp.bfloat16
MESH_AXES = ("x", "y", "c")
N_DEV = 8
HEAD_DIM = 128
POOL_WINDOWS = (2, 4, 8, 16)
N_MOD = 6
NORM_EPS = 1e-6
LANES = 128
SUBLANES = 8
VMEM_LIMIT_BYTES = 56 * 1024 * 1024
Q_TILE = 256
K_TILE = 256
POOL_TILE = 256

ADAM_LR = 0.001
ADAM_B1 = 0.9
ADAM_B2 = 0.999
ADAM_EPS = 1e-08
ADAM_WD = 0.01
ADAM_STEP = 10

_NN = (((1,), (0,)), ((), ()))
_NT = (((1,), (1,)), ((), ()))
_TN = (((0,), (0,)), ((), ()))
_DIMS = {"nn": _NN, "nt": _NT, "tn": _TN}


def _dot(a, b, mode="nn"):
    return lax.dot_general(a, b, _DIMS[mode], preferred_element_type=F32)


def _params(*sem):
    return pltpu.CompilerParams(dimension_semantics=sem, vmem_limit_bytes=VMEM_LIMIT_BYTES)


def _tile(dim, pref, align=SUBLANES):
    for t in range(min(dim, pref), 0, -1):
        if dim % t == 0 and t % align == 0:
            return t
    return dim


def _group_index(axes):
    idx = 0
    for a in axes:
        idx = idx * 2 + lax.axis_index(a)
    return idx


def _peer_device(axes, k):
    coords = {a: lax.axis_index(a) for a in MESH_AXES}
    for pos, a in enumerate(axes):
        if (k >> (len(axes) - 1 - pos)) & 1:
            coords[a] = 1 - coords[a]
    return tuple(coords[a] for a in MESH_AXES)


def _exchange(name, srcs, axes, gather):
    n = 2 ** len(axes)
    na = len(srcs)
    pres = [s.shape[0] for s in srcs]
    max_pre = max(pres)

    def body(*refs):
        src, out = refs[:na], refs[na:2 * na]
        send_sems, recv_sems, local_sems = refs[2 * na:]
        me = _group_index(axes)
        copies = []
        for k in range(1, n):
            peer = _peer_device(axes, k)
            for a in range(na):
                for i in range(pres[a]):
                    piece = src[a].at[i] if gather else src[a].at[i, me ^ k]
                    cp = pltpu.make_async_remote_copy(
                        src_ref=piece, dst_ref=out[a].at[i, me],
                        send_sem=send_sems.at[a, i, k - 1], recv_sem=recv_sems.at[a, i, k - 1],
                        device_id=peer, device_id_type=pl.DeviceIdType.MESH)
                    cp.start()
                    copies.append(cp)
        for a in range(na):
            for i in range(pres[a]):
                piece = src[a].at[i] if gather else src[a].at[i, me]
                cp = pltpu.make_async_copy(piece, out[a].at[i, me], local_sems.at[a, i])
                cp.start()
                copies.append(cp)
        for cp in copies:
            cp.wait()

    out_shape = []
    for s in srcs:
        rc = s.shape[1:] if gather else s.shape[2:]
        out_shape.append(jax.ShapeDtypeStruct((s.shape[0], n) + tuple(rc), s.dtype))
    any_spec = pl.BlockSpec(memory_space=pl.ANY)
    return pl.pallas_call(
        body, name=name, out_shape=tuple(out_shape),
        in_specs=[any_spec] * na, out_specs=tuple([any_spec] * na),
        scratch_shapes=[pltpu.SemaphoreType.DMA((na, max_pre, n - 1)),
                        pltpu.SemaphoreType.DMA((na, max_pre, n - 1)),
                        pltpu.SemaphoreType.DMA((na, max_pre))],
    )(*srcs)


def _all_gather_2d(name, x):
    r, c = x.shape

    def body(x_ref, out_ref, send_sems, recv_sems):
        me = _group_index(MESH_AXES)
        out_ref[me] = x_ref[...]
        copies = []
        for k in range(1, N_DEV):
            cp = pltpu.make_async_remote_copy(
                src_ref=x_ref, dst_ref=out_ref.at[me],
                send_sem=send_sems.at[k - 1], recv_sem=recv_sems.at[k - 1],
                device_id=_peer_device(MESH_AXES, k), device_id_type=pl.DeviceIdType.MESH)
            cp.start()
            copies.append(cp)
        for cp in copies:
            cp.wait()

    vmem = pl.BlockSpec(memory_space=pltpu.VMEM)
    return pl.pallas_call(
        body, name=name, out_shape=jax.ShapeDtypeStruct((N_DEV, r, c), x.dtype),
        in_specs=[vmem], out_specs=vmem,
        scratch_shapes=[pltpu.SemaphoreType.DMA((N_DEV - 1,)), pltpu.SemaphoreType.DMA((N_DEV - 1,))],
    )(x)


def _sum_slots(name, buf, out_dtype):
    pre, n, r, c = buf.shape
    tr = _tile(r, max(SUBLANES * 2, (1 << 20) // c))

    def body(b_ref, o_ref):
        acc = b_ref[0].astype(F32)
        for q in range(1, n):
            acc = acc + b_ref[q].astype(F32)
        o_ref[...] = acc.astype(o_ref.dtype)

    return pl.pallas_call(
        body, name=name, grid=(pre, r // tr),
        out_shape=jax.ShapeDtypeStruct((pre, r, c), out_dtype),
        in_specs=[pl.BlockSpec((None, n, tr, c), lambda i, j: (i, 0, j, 0))],
        out_specs=pl.BlockSpec((None, tr, c), lambda i, j: (i, j, 0)),
        compiler_params=_params("parallel", "parallel"),
    )(buf)


def _all_gather_weights(shards):
    stage1 = _exchange("ag_ici", [s[None] for s in shards], ("x", "y"), gather=True)
    stage1 = [s.reshape(s.shape[1:]) for s in stage1]
    stage2 = _exchange("ag_d2d", stage1, ("c",), gather=True)
    return [s.reshape((N_DEV,) + s.shape[2:]) for s in stage2]


def _reduce_scatter_grads(bufs):
    stage1 = _exchange("rs_d2d", [b.reshape((4, 2) + b.shape[1:]) for b in bufs], ("c",), gather=False)
    half = [_sum_slots("rs_sum_d2d_%d" % a, s, BF16) for a, s in enumerate(stage1)]
    stage2 = _exchange("rs_ici", [h[None] for h in half], ("x", "y"), gather=False)
    return [_sum_slots("rs_sum_ici_%d" % a, s, F32)[0] for a, s in enumerate(stage2)]


def _matmul(name, mode, grid, a, a_spec, b, b_spec, out_shapes, out_specs, acc_shape,
            epilogue=None, extras=(), extra_specs=(), aliases=None):
    nk = grid[2]
    n_extra = len(extras)
    n_out = len(out_shapes)

    def finish(acc, extra_refs, out_refs):
        if epilogue is None:
            out_refs[0][...] = acc.astype(out_refs[0].dtype)
        else:
            epilogue(acc, extra_refs, out_refs)

    def body(*refs):
        a_ref, b_ref = refs[0], refs[1]
        extra_refs = refs[2:2 + n_extra]
        out_refs = refs[2 + n_extra:2 + n_extra + n_out]
        if nk == 1:
            finish(_dot(a_ref[...], b_ref[...], mode), extra_refs, out_refs)
            return
        acc_ref = refs[-1]
        k = pl.program_id(2)

        @pl.when(k == 0)
        def _():
            acc_ref[...] = jnp.zeros_like(acc_ref)

        acc_ref[...] += _dot(a_ref[...], b_ref[...], mode)

        @pl.when(k == nk - 1)
        def _():
            finish(acc_ref[...], extra_refs, out_refs)

    scratch = [] if nk == 1 else [pltpu.VMEM(acc_shape, F32)]
    return pl.pallas_call(
        body, name=name, grid=grid, out_shape=tuple(out_shapes),
        in_specs=[a_spec, b_spec] + list(extra_specs), out_specs=tuple(out_specs),
        scratch_shapes=scratch, input_output_aliases=aliases or {},
        compiler_params=_params("parallel", "parallel", "arbitrary"),
    )(a, b, *extras)


def _sds(shape, dtype):
    return jax.ShapeDtypeStruct(tuple(shape), dtype)


def _ada_forward(c_all, w_ada, b_shard):
    nb, d = c_all.shape
    w = w_ada.shape[1]
    tn = _tile(w, 512)

    def body(c_ref, w_ref, b_ref, o_ref):
        cv = c_ref[...]
        sc = cv * jax.nn.sigmoid(cv)
        o_ref[...] = jnp.dot(sc, w_ref[...], precision=lax.Precision.HIGHEST,
                             preferred_element_type=F32) + b_ref[...]

    return pl.pallas_call(
        body, name="ada_fwd", grid=(w // tn,), out_shape=_sds((nb, w), F32),
        in_specs=[pl.BlockSpec((nb, d), lambda j: (0, 0)), pl.BlockSpec((d, tn), lambda j: (0, j)),
                  pl.BlockSpec((1, tn), lambda j: (0, j))],
        out_specs=pl.BlockSpec((nb, tn), lambda j: (0, j)),
        compiler_params=_params("parallel"),
    )(c_all, w_ada, b_shard)


def _ada_weight_grad(c_all, dmod_cols):
    nb, d = c_all.shape
    w = dmod_cols.shape[1]
    tn = _tile(w, 512)

    def body(c_ref, g_ref, o_ref):
        cv = c_ref[...]
        sc = cv * jax.nn.sigmoid(cv)
        o_ref[...] = lax.dot_general(sc, g_ref[...], _TN, precision=lax.Precision.HIGHEST,
                                     preferred_element_type=F32)

    return pl.pallas_call(
        body, name="ada_wgrad", grid=(w // tn,), out_shape=_sds((d, w), F32),
        in_specs=[pl.BlockSpec((nb, d), lambda j: (0, 0)), pl.BlockSpec((nb, tn), lambda j: (0, j))],
        out_specs=pl.BlockSpec((d, tn), lambda j: (0, j)),
        compiler_params=_params("parallel"),
    )(c_all, dmod_cols)


def _norm_forward(name, x, norm_w, scale, shift):
    s, d = x.shape
    tm = _tile(s, 256)

    def body(x_ref, w_ref, sc_ref, sh_ref, h_ref):
        xv = x_ref[...]
        r = lax.rsqrt(jnp.mean(xv * xv, axis=-1, keepdims=True) + NORM_EPS)
        h = (xv * r * w_ref[...]) * (1.0 + sc_ref[...]) + sh_ref[...]
        h_ref[...] = h.astype(BF16)

    vec = pl.BlockSpec((1, d), lambda i: (0, 0))
    row = pl.BlockSpec((tm, d), lambda i: (i, 0))
    return pl.pallas_call(
        body, name=name, grid=(s // tm,), out_shape=_sds((s, d), BF16),
        in_specs=[row, vec, vec, vec], out_specs=row, compiler_params=_params("parallel"),
    )(x, norm_w, scale, shift)


def _norm_backward(name, dh, x, norm_w, scale, dres):
    s, d = x.shape
    tm = _tile(s, 256)

    def body(dh_ref, x_ref, w_ref, sc_ref, dres_ref, dx_ref, dshift_ref, dscale_ref, dw_ref):
        @pl.when(pl.program_id(0) == 0)
        def _():
            dshift_ref[...] = jnp.zeros_like(dshift_ref)
            dscale_ref[...] = jnp.zeros_like(dscale_ref)
            dw_ref[...] = jnp.zeros_like(dw_ref)

        xv = x_ref[...]
        g = dh_ref[...]
        r = lax.rsqrt(jnp.mean(xv * xv, axis=-1, keepdims=True) + NORM_EPS)
        n = xv * r
        gain = 1.0 + sc_ref[...]
        gn = g * n
        dshift_ref[...] += jnp.sum(g, axis=0, keepdims=True)
        dscale_ref[...] += jnp.sum(gn, axis=0, keepdims=True) * w_ref[...]
        dw_ref[...] += jnp.sum(gn, axis=0, keepdims=True) * gain
        dn = g * (w_ref[...] * gain)
        dx_ref[...] = dres_ref[...] + r * (dn - n * jnp.mean(dn * n, axis=-1, keepdims=True))

    vec = pl.BlockSpec((1, d), lambda i: (0, 0))
    row = pl.BlockSpec((tm, d), lambda i: (i, 0))
    return pl.pallas_call(
        body, name=name, grid=(s // tm,),
        out_shape=(_sds((s, d), F32), _sds((1, d), F32), _sds((1, d), F32), _sds((1, d), F32)),
        in_specs=[row, row, vec, vec, row], out_specs=(row, vec, vec, vec),
        compiler_params=_params("arbitrary"),
    )(dh, x, norm_w, scale, dres)


def _gate_backward(name, d, gate, other):
    s, w = d.shape
    tm = _tile(s, 256)

    def body(d_ref, g_ref, o_ref, dg_ref, dgate_ref):
        @pl.when(pl.program_id(0) == 0)
        def _():
            dgate_ref[...] = jnp.zeros_like(dgate_ref)

        dv = d_ref[...]
        dg_ref[...] = (dv * g_ref[...]).astype(BF16)
        dgate_ref[...] += jnp.sum(dv * o_ref[...].astype(F32), axis=0, keepdims=True)

    vec = pl.BlockSpec((1, w), lambda i: (0, 0))
    row = pl.BlockSpec((tm, w), lambda i: (i, 0))
    return pl.pallas_call(
        body, name=name, grid=(s // tm,), out_shape=(_sds((s, w), BF16), _sds((1, w), F32)),
        in_specs=[row, vec, row], out_specs=(row, vec), compiler_params=_params("arbitrary"),
    )(d, gate, other)


def _split_bf16(v):
    hi = v.astype(BF16)
    lo = (v - hi.astype(F32)).astype(BF16)
    return hi, lo


def _pool_forward(proj, w_pool, pool_scale):
    s = proj.shape[0]
    g_n, cg, _ = w_pool.shape
    t = POOL_TILE
    nt = s // t

    def body(cur_ref, prev_ref, wp_ref, sc_ref, pooled_ref, ya_ref):
        g = pl.program_id(0)
        ti = pl.program_id(1)
        win = jnp.left_shift(2, g)
        row = lax.broadcasted_iota(jnp.int32, (t, t), 0)
        col = lax.broadcasted_iota(jnp.int32, (t, t), 1)
        lag = row - col
        band_cur = ((lag >= 0) & (lag < win)).astype(BF16)
        band_prev = ((lag + t < win) & (ti > 0)).astype(BF16)
        u = cur_ref[...]
        u_hi, u_lo = _split_bf16(u)
        p_hi, p_lo = _split_bf16(prev_ref[...])
        wsum = (_dot(band_cur, u_hi) + _dot(band_cur, u_lo)
                + _dot(band_prev, p_hi) + _dot(band_prev, p_lo))
        tok = ti * t + lax.broadcasted_iota(jnp.int32, (t, 1), 0)
        count = jnp.minimum(tok + 1, win).astype(F32)
        pooled = (wsum / count - u).astype(BF16)
        pooled_ref[...] = pooled
        ya_ref[...] = (_dot(pooled, wp_ref[...]) * sc_ref[...]).astype(BF16)

    blk = pl.BlockSpec((t, cg), lambda g, i: (i, g))
    return pl.pallas_call(
        body, name="pool_fwd", grid=(g_n, nt),
        out_shape=(_sds((s, g_n * cg), BF16), _sds((s, g_n * cg), BF16)),
        in_specs=[blk, pl.BlockSpec((t, cg), lambda g, i: (jnp.maximum(i - 1, 0), g)),
                  pl.BlockSpec((None, cg, cg), lambda g, i: (g, 0, 0)),
                  pl.BlockSpec((1, cg), lambda g, i: (0, g))],
        out_specs=(blk, blk), compiler_params=_params("parallel", "parallel"),
    )(proj, proj, w_pool, pool_scale)


def _pool_backward(dya, pooled, w_pool, pool_scale, dproj):
    s = dya.shape[0]
    g_n, cg, _ = w_pool.shape
    t = POOL_TILE
    nt = s // t

    def body(dya_ref, dya_next_ref, pooled_ref, wp_ref, sc_ref, dproj_in, du_ref, gw_ref, gs_ref):
        del dproj_in
        g = pl.program_id(0)
        ti = pl.program_id(1)

        @pl.when(ti == 0)
        def _():
            gw_ref[...] = jnp.zeros_like(gw_ref)
            gs_ref[...] = jnp.zeros_like(gs_ref)

        win = jnp.left_shift(2, g)
        wp = wp_ref[...]
        sc = sc_ref[...]
        pooled_v = pooled_ref[...]
        dya_v = dya_ref[...].astype(F32)
        mixed = _dot(pooled_v, wp)
        gs_ref[...] += jnp.sum(dya_v * mixed, axis=0, keepdims=True)
        dmixed = (dya_v * sc).astype(BF16)
        gw_ref[...] += _dot(pooled_v, dmixed, "tn")
        dpooled = _dot(dmixed, wp, "nt")
        dmixed_next = (dya_next_ref[...].astype(F32) * sc).astype(BF16)
        dpooled_next = _dot(dmixed_next, wp, "nt")
        tok = ti * t + lax.broadcasted_iota(jnp.int32, (t, 1), 0)
        e_cur = dpooled / jnp.minimum(tok + 1, win).astype(F32)
        e_next = dpooled_next / jnp.minimum(tok + t + 1, win).astype(F32)
        row = lax.broadcasted_iota(jnp.int32, (t, t), 0)
        col = lax.broadcasted_iota(jnp.int32, (t, t), 1)
        lead = col - row
        band_cur = ((lead >= 0) & (lead < win)).astype(BF16)
        band_next = ((lead + t < win) & (ti < nt - 1)).astype(BF16)
        c_hi, c_lo = _split_bf16(e_cur)
        n_hi, n_lo = _split_bf16(e_next)
        du = (_dot(band_cur, c_hi) + _dot(band_cur, c_lo)
              + _dot(band_next, n_hi) + _dot(band_next, n_lo)) - dpooled
        du_ref[...] = du.astype(BF16)

    blk = pl.BlockSpec((t, cg), lambda g, i: (i, g))
    du, gw, gs = pl.pallas_call(
        body, name="pool_bwd", grid=(g_n, nt),
        out_shape=(_sds(dproj.shape, BF16), _sds((g_n, cg, cg), F32), _sds((1, g_n * cg), F32)),
        in_specs=[blk, pl.BlockSpec((t, cg), lambda g, i: (jnp.minimum(i + 1, nt - 1), g)), blk,
                  pl.BlockSpec((None, cg, cg), lambda g, i: (g, 0, 0)),
                  pl.BlockSpec((1, cg), lambda g, i: (0, g)),
                  pl.BlockSpec(memory_space=pl.ANY)],
        out_specs=(blk, pl.BlockSpec((None, cg, cg), lambda g, i: (g, 0, 0)),
                   pl.BlockSpec((1, cg), lambda g, i: (0, g))),
        input_output_aliases={5: 0}, compiler_params=_params("parallel", "arbitrary"),
    )(dya, dya, pooled, w_pool, pool_scale, dproj)
    return du, gw, gs


def _qkv_prepare(proj, q_norm_w, k_norm_w, width):
    s = proj.shape[0]
    tm = _tile(s, 256)
    heads = width // HEAD_DIM

    def body(q_ref, k_ref, v_ref, qw_ref, kw_ref, qn_ref, kn_ref, vb_ref):
        for h in range(heads):
            cols = slice(h * HEAD_DIM, (h + 1) * HEAD_DIM)
            for src, w_ref, dst in ((q_ref, qw_ref, qn_ref), (k_ref, kw_ref, kn_ref)):
                v = src[:, cols]
                r = lax.rsqrt(jnp.mean(v * v, axis=-1, keepdims=True) + NORM_EPS)
                dst[:, cols] = (v * r * w_ref[...]).astype(BF16)
        vb_ref[...] = v_ref[...].astype(BF16)

    vec = pl.BlockSpec((1, HEAD_DIM), lambda i: (0, 0))
    out_spec = pl.BlockSpec((tm, width), lambda i: (i, 0))
    return pl.pallas_call(
        body, name="qkv_prep", grid=(s // tm,),
        out_shape=(_sds((s, width), BF16),) * 3,
        in_specs=[pl.BlockSpec((tm, width), lambda i: (i, 1)), pl.BlockSpec((tm, width), lambda i: (i, 2)),
                  pl.BlockSpec((tm, width), lambda i: (i, 3)), vec, vec],
        out_specs=(out_spec,) * 3, compiler_params=_params("parallel"),
    )(proj, proj, proj, q_norm_w, k_norm_w)


def _qk_norm_backward(name, dn, proj, col_block, norm_w, dproj, width):
    s = proj.shape[0]
    tm = _tile(s, 256)
    heads = width // HEAD_DIM

    def body(dn_ref, q_ref, w_ref, dproj_in, dq_ref, gw_ref):
        del dproj_in

        @pl.when(pl.program_id(0) == 0)
        def _():
            gw_ref[...] = jnp.zeros_like(gw_ref)

        wv = w_ref[...]
        gw = jnp.zeros((1, HEAD_DIM), F32)
        for h in range(heads):
            cols = slice(h * HEAD_DIM, (h + 1) * HEAD_DIM)
            v = q_ref[:, cols]
            g = dn_ref[:, cols]
            r = lax.rsqrt(jnp.mean(v * v, axis=-1, keepdims=True) + NORM_EPS)
            n = v * r
            gw = gw + jnp.sum(g * n, axis=0, keepdims=True)
            gn = g * wv
            dq_ref[:, cols] = (r * (gn - n * jnp.mean(gn * n, axis=-1, keepdims=True))).astype(BF16)
        gw_ref[...] += gw

    blk = pl.BlockSpec((tm, width), lambda i: (i, col_block))
    return pl.pallas_call(
        body, name=name, grid=(s // tm,),
        out_shape=(_sds(dproj.shape, BF16), _sds((1, HEAD_DIM), F32)),
        in_specs=[pl.BlockSpec((tm, width), lambda i: (i, 0)), blk,
                  pl.BlockSpec((1, HEAD_DIM), lambda i: (0, 0)), pl.BlockSpec(memory_space=pl.ANY)],
        out_specs=(blk, pl.BlockSpec((1, HEAD_DIM), lambda i: (0, 0))),
        input_output_aliases={3: 0}, compiler_params=_params("arbitrary"),
    )(dn, proj, norm_w, dproj)


def _strict_upper(n):
    row = lax.broadcasted_iota(jnp.int32, (n, n), 0)
    col = lax.broadcasted_iota(jnp.int32, (n, n), 1)
    return (row > col).astype(BF16)


def _strict_lower(n):
    row = lax.broadcasted_iota(jnp.int32, (n, n), 0)
    col = lax.broadcasted_iota(jnp.int32, (n, n), 1)
    return (row < col).astype(BF16)


def _cumulate(v, tri):
    hi, lo = _split_bf16(v)
    return _dot(hi, tri) + _dot(lo, tri)


def _log_sigmoid(z):
    return jnp.minimum(z, 0.0) - jnp.log(1.0 + jnp.exp(-jnp.abs(z)))


def _attention_forward(qn, kn, vb):
    s, width = qn.shape
    heads = width // HEAD_DIM
    tq, tk = Q_TILE, K_TILE
    assert tq == tk and s % tq == 0
    scale = 1.0 / math.sqrt(HEAD_DIM)

    def body(q_ref, k_ref, v_ref, o_ref):
        qi = pl.program_id(1)
        q = q_ref[...]
        upper = _strict_upper(tk)
        causal = lax.broadcasted_iota(jnp.int32, (tq, tk), 1) < lax.broadcasted_iota(jnp.int32, (tq, tk), 0)

        def chunk(kb, carry, masked):
            acc, later = carry
            rows = pl.ds(pl.multiple_of(kb * tk, tk), tk)
            z = _dot(q, k_ref[rows, :], "nt") * scale
            log_beta = _log_sigmoid(z)
            l = log_beta - z
            if masked:
                l = jnp.where(causal, l, 0.0)
            w = log_beta + _cumulate(l, upper) + later
            a = jnp.exp(w)
            if masked:
                a = jnp.where(causal, a, 0.0)
            acc = acc + _dot(a.astype(BF16), v_ref[rows, :])
            return acc, later + jnp.sum(l, axis=1, keepdims=True)

        carry = (jnp.zeros((tq, HEAD_DIM), F32), jnp.zeros((tq, 1), F32))
        carry = chunk(qi, carry, True)
        acc, _ = lax.fori_loop(0, qi, lambda i, c: chunk(qi - 1 - i, c, False), carry)
        o_ref[...] = acc.astype(BF16)

    full = pl.BlockSpec((s, HEAD_DIM), lambda h, i: (0, h))
    blk = pl.BlockSpec((tq, HEAD_DIM), lambda h, i: (i, h))
    return pl.pallas_call(
        body, name="attn_fwd", grid=(heads, s // tq), out_shape=_sds((s, width), BF16),
        in_specs=[blk, full, full], out_specs=blk, compiler_params=_params("parallel", "parallel"),
    )(qn, kn, vb)


def _attention_backward(qn, kn, vb, dout, dproj, v_col_block):
    s, width = qn.shape
    heads = width // HEAD_DIM
    tq, tk = Q_TILE, K_TILE
    nq = s // tq
    scale = 1.0 / math.sqrt(HEAD_DIM)
    v_blocks_per_head = v_col_block * heads

    def body(q_ref, k_ref, v_ref, do_ref, dproj_in, dq_ref, dk_ref, dv_ref, a_scr, dk_scr, dv_scr):
        del dproj_in
        qi = pl.program_id(1)

        @pl.when(qi == 0)
        def _():
            dk_scr[...] = jnp.zeros_like(dk_scr)
            dv_scr[...] = jnp.zeros_like(dv_scr)

        q = q_ref[...]
        dout_v = do_ref[...]
        upper = _strict_upper(tk)
        lower = _strict_lower(tk)
        causal = lax.broadcasted_iota(jnp.int32, (tq, tk), 1) < lax.broadcasted_iota(jnp.int32, (tq, tk), 0)

        def weights(kb, later, masked):
            rows = pl.ds(pl.multiple_of(kb * tk, tk), tk)
            z = _dot(q, k_ref[rows, :], "nt") * scale
            log_beta = _log_sigmoid(z)
            l = log_beta - z
            if masked:
                l = jnp.where(causal, l, 0.0)
            a = jnp.exp(log_beta + _cumulate(l, upper) + later)
            if masked:
                a = jnp.where(causal, a, 0.0)
            a_scr[:, rows] = a
            return later + jnp.sum(l, axis=1, keepdims=True)

        later = weights(qi, jnp.zeros((tq, 1), F32), True)
        lax.fori_loop(0, qi, lambda i, c: weights(qi - 1 - i, c, False), later)

        def grads(kb, carry, masked):
            dq, before = carry
            rows = pl.ds(pl.multiple_of(kb * tk, tk), tk)
            k_blk = k_ref[rows, :]
            z = _dot(q, k_blk, "nt") * scale
            beta = jax.nn.sigmoid(z)
            a = a_scr[:, rows]
            g = a * _dot(dout_v, v_ref[rows, :], "nt")
            p = _cumulate(g, lower) + before
            dz = g * (1.0 - beta) - p * beta
            if masked:
                dz = jnp.where(causal, dz, 0.0)
            dz = (dz * scale).astype(BF16)
            dk_scr[rows, :] += _dot(dz, q, "tn")
            dv_scr[rows, :] += _dot(a.astype(BF16), dout_v, "tn")
            return dq + _dot(dz, k_blk), before + jnp.sum(g, axis=1, keepdims=True)

        carry = (jnp.zeros((tq, HEAD_DIM), F32), jnp.zeros((tq, 1), F32))
        carry = lax.fori_loop(0, qi, lambda i, c: grads(i, c, False), carry)
        dq, _ = grads(qi, carry, True)
        dq_ref[...] = dq

        @pl.when(qi == nq - 1)
        def _():
            dk_ref[...] = dk_scr[...]
            dv_ref[...] = dv_scr[...].astype(BF16)

    full = pl.BlockSpec((s, HEAD_DIM), lambda h, i: (0, h))
    blk = pl.BlockSpec((tq, HEAD_DIM), lambda h, i: (i, h))
    return pl.pallas_call(
        body, name="attn_bwd", grid=(heads, nq),
        out_shape=(_sds((s, width), F32), _sds((s, width), F32), _sds(dproj.shape, BF16)),
        in_specs=[blk, full, full, blk, pl.BlockSpec(memory_space=pl.ANY)],
        out_specs=(blk, full, pl.BlockSpec((s, HEAD_DIM), lambda h, i: (0, v_blocks_per_head + h))),
        scratch_shapes=[pltpu.VMEM((tq, s), F32), pltpu.VMEM((s, HEAD_DIM), F32), pltpu.VMEM((s, HEAD_DIM), F32)],
        input_output_aliases={4: 2}, compiler_params=_params("parallel", "arbitrary"),
    )(qn, kn, vb, dout, dproj)


def _place_columns(name, src, dst, col_block):
    s, w = src.shape
    tm = _tile(s, 512)

    def body(src_ref, dst_in, out_ref):
        del dst_in
        out_ref[...] = src_ref[...]

    return pl.pallas_call(
        body, name=name, grid=(s // tm,), out_shape=_sds(dst.shape, dst.dtype),
        in_specs=[pl.BlockSpec((tm, w), lambda i: (i, 0)), pl.BlockSpec(memory_space=pl.ANY)],
        out_specs=pl.BlockSpec((tm, w), lambda i: (i, col_block)),
        input_output_aliases={1: 0}, compiler_params=_params("parallel"),
    )(src, dst)


def _cast_bf16(name, x):
    r, c = x.shape
    tr = _tile(r, max(SUBLANES * 2, (1 << 20) // c))

    def body(x_ref, o_ref):
        o_ref[...] = x_ref[...].astype(BF16)

    blk = pl.BlockSpec((tr, c), lambda i: (i, 0))
    return pl.pallas_call(
        body, name=name, grid=(r // tr,), out_shape=_sds((r, c), BF16),
        in_specs=[blk], out_specs=blk, compiler_params=_params("parallel"),
    )(x)


def _adamw(name, w, g, m, v):
    r, c = w.shape
    tr = _tile(r, max(SUBLANES, (1 << 19) // c))
    c1 = 1.0 - ADAM_B1 ** ADAM_STEP
    c2 = 1.0 - ADAM_B2 ** ADAM_STEP

    def body(w_ref, g_ref, m_ref, v_ref, d_ref, nm_ref, nv_ref):
        gv = g_ref[...]
        nm = ADAM_B1 * m_ref[...] + (1.0 - ADAM_B1) * gv
        nv = ADAM_B2 * v_ref[...] + (1.0 - ADAM_B2) * (gv * gv)
        d_ref[...] = -ADAM_LR * ((nm / c1) / (jnp.sqrt(nv / c2) + ADAM_EPS) + ADAM_WD * w_ref[...])
        nm_ref[...] = nm
        nv_ref[...] = nv

    blk = pl.BlockSpec((tr, c), lambda i: (i, 0))
    return pl.pallas_call(
        body, name=name, grid=(r // tr,), out_shape=(_sds((r, c), F32),) * 3,
        in_specs=[blk] * 4, out_specs=(blk,) * 3, compiler_params=_params("parallel"),
    )(w, g, m, v)


def _rows_of_lanes(v):
    rows = v.shape[1] // LANES
    out = v.reshape(rows, LANES)
    pad = (-rows) % SUBLANES
    if pad:
        out = jnp.pad(out, ((0, pad), (0, 0)))
    return out


def kernel(x, c, w_ada, b_ada, norm1_w, w_in, q_norm_w, k_norm_w, w_pool, pool_scale, w_a_up, w_b_up, w_o, norm2_w, w_ff1, w_ff2, loss_target, m_w_ada, m_b_ada, m_norm1_w, m_w_in, m_q_norm_w, m_k_norm_w, m_w_pool, m_pool_scale, m_w_a_up, m_w_b_up, m_w_o, m_norm2_w, m_w_ff1, m_w_ff2, v_w_ada, v_b_ada, v_norm1_w, v_w_in, v_q_norm_w, v_k_norm_w, v_w_pool, v_pool_scale, v_w_a_up, v_w_b_up, v_w_o, v_norm2_w, v_w_ff1, v_w_ff2):
    _, s, d = x.shape
    half = d // 2
    d8 = d // N_DEV
    n_groups = len(POOL_WINDOWS)
    cg = half // n_groups
    me = _group_index(MESH_AXES)

    x2 = x[0]
    target = loss_target[0]

    c_all = _all_gather_2d("ag_c", c.reshape(d // LANES, LANES)).reshape(N_DEV, d)
    wa = w_ada.shape[2]
    b_shard = lax.dynamic_slice_in_dim(b_ada, me * wa, wa, axis=1)
    mod_part = _ada_forward(c_all, w_ada[0], b_shard)
    mod_all = _all_gather_2d("ag_mod", mod_part.reshape(N_DEV * wa // LANES, LANES))
    mod_all = mod_all.reshape(N_DEV, N_DEV, wa)
    mod = lax.dynamic_slice_in_dim(mod_all, me, 1, axis=1).reshape(1, N_MOD * d)
    shift1, scale1, gate1, shift2, scale2, gate2 = [mod[:, i * d:(i + 1) * d] for i in range(N_MOD)]

    shards = [w_in[0], w_pool[0].reshape(-1, cg), w_a_up[0], w_b_up[0], w_o[0], w_ff1[0], w_ff2[0]]
    shards = [_cast_bf16("cast_w%d" % i, t) for i, t in enumerate(shards)]
    w_in_f, w_pool_f, w_a_f, w_b_f, w_o_f, w_ff1_f, w_ff2_f = _all_gather_weights(shards)
    rows_pool = cg // N_DEV
    w_pool_f = w_pool_f.reshape(N_DEV, n_groups, rows_pool, cg).transpose(1, 0, 2, 3).reshape(n_groups, cg, cg)
    w_o_f = w_o_f.reshape(d, d)
    w_ff2_f = w_ff2_f.reshape(4 * d, d)

    tm = _tile(s, 512)
    tk = _tile(d, 1024)

    h = _norm_forward("norm1_fwd", x2, norm1_w, scale1, shift1)
    proj = _matmul(
        "proj", "nn", (s // tm, N_DEV, d // tk), h, pl.BlockSpec((tm, tk), lambda i, j, k: (i, k)),
        w_in_f, pl.BlockSpec((None, tk, half), lambda i, j, k: (j, k, 0)),
        [_sds((s, 4 * d), F32)], [pl.BlockSpec((tm, half), lambda i, j, k: (i, j))], (tm, half))[0]

    pooled, ya_in = _pool_forward(proj, w_pool_f, pool_scale)
    qn, kn, vb = _qkv_prepare(proj, q_norm_w, k_norm_w, half)
    attn = _attention_forward(qn, kn, vb)

    def merge_epilogue(ga_ref, gb_ref, ya, yb, out_refs):
        merged_ref, ya_ref, yb_ref = out_refs
        merged = jax.nn.sigmoid(ga_ref[...]) * ya + jax.nn.sigmoid(gb_ref[...]) * yb
        merged_ref[...] = merged.astype(BF16)
        ya_ref[...] = ya.astype(BF16)
        yb_ref[...] = yb.astype(BF16)

    def up_body(a1_ref, b1_ref, a2_ref, b2_ref, ga_ref, gb_ref, *out_refs):
        merge_epilogue(ga_ref, gb_ref, _dot(a1_ref[...], b1_ref[...]), _dot(a2_ref[...], b2_ref[...]), out_refs)

    ga_blk0 = 2 * d // d8
    gb_blk0 = 3 * d // d8
    a_spec = pl.BlockSpec((tm, half), lambda i, j: (i, 0))
    wup_spec = pl.BlockSpec((None, half, d8), lambda i, j: (j, 0, 0))
    o_blk = pl.BlockSpec((tm, d8), lambda i, j: (i, j))
    merged, y_a, y_b = pl.pallas_call(
        up_body, name="up_merge", grid=(s // tm, N_DEV), out_shape=(_sds((s, d), BF16),) * 3,
        in_specs=[a_spec, wup_spec, a_spec, wup_spec,
                  pl.BlockSpec((tm, d8), lambda i, j: (i, ga_blk0 + j)),
                  pl.BlockSpec((tm, d8), lambda i, j: (i, gb_blk0 + j))],
        out_specs=(o_blk,) * 3, compiler_params=_params("parallel", "parallel"),
    )(ya_in, w_a_f, attn, w_b_f, proj, proj)

    tn = _tile(d, 1024)

    def oproj_epilogue(acc, extra_refs, out_refs):
        x_ref, g_ref = extra_refs
        x1_ref, o_ref = out_refs
        x1_ref[...] = x_ref[...] + g_ref[...] * acc
        o_ref[...] = acc.astype(BF16)

    mn_blk = pl.BlockSpec((tm, tn), lambda i, j, k: (i, j))
    n_vec = pl.BlockSpec((1, tn), lambda i, j, k: (0, j))
    x1, o_act = _matmul(
        "oproj", "nn", (s // tm, d // tn, d // tk), merged, pl.BlockSpec((tm, tk), lambda i, j, k: (i, k)),
        w_o_f, pl.BlockSpec((tk, tn), lambda i, j, k: (k, j)),
        [_sds((s, d), F32), _sds((s, d), BF16)], [mn_blk, mn_blk], (tm, tn),
        epilogue=oproj_epilogue, extras=(x2, gate1), extra_specs=(mn_blk, n_vec))

    h2 = _norm_forward("norm2_fwd", x1, norm2_w, scale2, shift2)

    def ff1_epilogue(acc, extra_refs, out_refs):
        r = jnp.maximum(acc, 0.0)
        out_refs[0][...] = r.astype(BF16)
        out_refs[1][...] = (r * r).astype(BF16)

    ff_blk = pl.BlockSpec((tm, half), lambda i, j, k: (i, j))
    relu, act = _matmul(
        "ff1", "nn", (s // tm, N_DEV, d // tk), h2, pl.BlockSpec((tm, tk), lambda i, j, k: (i, k)),
        w_ff1_f, pl.BlockSpec((None, tk, half), lambda i, j, k: (j, k, 0)),
        [_sds((s, 4 * d), BF16)] * 2, [ff_blk, ff_blk], (tm, half), epilogue=ff1_epilogue)

    def ff2_epilogue(acc, extra_refs, out_refs):
        x1_ref, g_ref, t_ref = extra_refs
        f_ref, dy_ref, sq_ref = out_refs
        err = x1_ref[...] + g_ref[...] * acc - t_ref[...]
        f_ref[...] = acc.astype(BF16)
        dy_ref[...] = err * (1.0 / d)
        sq_ref[...] = jnp.full(sq_ref.shape, jnp.sum(err * err), F32)

    f_act, dy, sq = _matmul(
        "ff2", "nn", (s // tm, d // tn, 4 * d // tk), act, pl.BlockSpec((tm, tk), lambda i, j, k: (i, k)),
        w_ff2_f, pl.BlockSpec((tk, tn), lambda i, j, k: (k, j)),
        [_sds((s, d), BF16), _sds((s, d), F32), _sds((s // tm * SUBLANES, d // tn * LANES), F32)],
        [mn_blk, mn_blk, pl.BlockSpec((SUBLANES, LANES), lambda i, j, k: (i, j))], (tm, tn),
        epilogue=ff2_epilogue, extras=(x1, gate2, target), extra_specs=(mn_blk, n_vec, mn_blk))
    loss_local = (0.5 / d) * jnp.sum(sq[::SUBLANES, ::LANES])
    loss = lax.psum(loss_local, MESH_AXES)

    df, dgate2 = _gate_backward("gate2_bwd", dy, gate2, f_act)
    tok_k = _tile(s, 1024)
    g_ff2 = _matmul(
        "g_ff2", "tn", (4 * d // tk, d // tn, s // tok_k), act, pl.BlockSpec((tok_k, tk), lambda i, j, k: (k, i)),
        df, pl.BlockSpec((tok_k, tn), lambda i, j, k: (k, j)),
        [_sds((4 * d, d), BF16)], [pl.BlockSpec((tk, tn), lambda i, j, k: (i, j))], (tk, tn))[0]

    def da_epilogue(acc, extra_refs, out_refs):
        out_refs[0][...] = (acc * (2.0 * extra_refs[0][...].astype(F32))).astype(BF16)

    big_blk = pl.BlockSpec((tm, tn), lambda i, j, k: (i, j))
    df1 = _matmul(
        "da_ff", "nt", (s // tm, 4 * d // tn, d // tk), df, pl.BlockSpec((tm, tk), lambda i, j, k: (i, k)),
        w_ff2_f, pl.BlockSpec((tn, tk), lambda i, j, k: (j, k)),
        [_sds((s, 4 * d), BF16)], [big_blk], (tm, tn),
        epilogue=da_epilogue, extras=(relu,), extra_specs=(big_blk,))[0]

    g_ff1 = _matmul(
        "g_ff1", "tn", (d // tk, N_DEV, s // tok_k), h2, pl.BlockSpec((tok_k, tk), lambda i, j, k: (k, i)),
        df1, pl.BlockSpec((tok_k, half), lambda i, j, k: (k, j)),
        [_sds((N_DEV, d, half), BF16)], [pl.BlockSpec((None, tk, half), lambda i, j, k: (j, i, 0))], (tk, half))[0]

    dh2 = _matmul(
        "dh2", "nt", (s // tm, d // tn, N_DEV), df1, pl.BlockSpec((tm, half), lambda i, j, k: (i, k)),
        w_ff1_f, pl.BlockSpec((None, tn, half), lambda i, j, k: (k, j, 0)),
        [_sds((s, d), F32)], [mn_blk], (tm, tn))[0]

    dx1, dshift2, dscale2, g_norm2 = _norm_backward("norm2_bwd", dh2, x1, norm2_w, scale2, dy)

    do, dgate1 = _gate_backward("gate1_bwd", dx1, gate1, o_act)
    g_o = _matmul(
        "g_o", "tn", (d // tk, d // tn, s // tok_k), merged, pl.BlockSpec((tok_k, tk), lambda i, j, k: (k, i)),
        do, pl.BlockSpec((tok_k, tn), lambda i, j, k: (k, j)),
        [_sds((d, d), BF16)], [pl.BlockSpec((tk, tn), lambda i, j, k: (i, j))], (tk, tn))[0]

    def merge_bwd_epilogue(acc, extra_refs, out_refs):
        ga_ref, gb_ref, ya_ref, yb_ref = extra_refs
        dya_ref, dyb_ref, dga_ref, dgb_ref = out_refs
        sa = jax.nn.sigmoid(ga_ref[...])
        sb = jax.nn.sigmoid(gb_ref[...])
        dya_ref[...] = (acc * sa).astype(BF16)
        dyb_ref[...] = (acc * sb).astype(BF16)
        dga_ref[...] = (acc * ya_ref[...].astype(F32) * (sa * (1.0 - sa))).astype(BF16)
        dgb_ref[...] = (acc * yb_ref[...].astype(F32) * (sb * (1.0 - sb))).astype(BF16)

    nb = d // tn
    dy_a, dy_b, dproj, dg_b = _matmul(
        "dmerged", "nt", (s // tm, nb, d // tk), do, pl.BlockSpec((tm, tk), lambda i, j, k: (i, k)),
        w_o_f, pl.BlockSpec((tn, tk), lambda i, j, k: (j, k)),
        [_sds((s, d), BF16), _sds((s, d), BF16), _sds((s, 4 * d), BF16), _sds((s, d), BF16)],
        [mn_blk, mn_blk, pl.BlockSpec((tm, tn), lambda i, j, k: (i, 2 * nb + j)), mn_blk], (tm, tn),
        epilogue=merge_bwd_epilogue, extras=(proj, proj, y_a, y_b),
        extra_specs=(pl.BlockSpec((tm, tn), lambda i, j, k: (i, 2 * nb + j)),
                     pl.BlockSpec((tm, tn), lambda i, j, k: (i, 3 * nb + j)), mn_blk, mn_blk))
    dproj = _place_columns("place_dgb", dg_b, dproj, 3)

    up_a = pl.BlockSpec((tok_k, half), lambda i, j, k: (k, 0))
    up_b = pl.BlockSpec((tok_k, d8), lambda i, j, k: (k, j))
    up_o = pl.BlockSpec((None, half, d8), lambda i, j, k: (j, 0, 0))
    g_a_up = _matmul("g_a_up", "tn", (1, N_DEV, s // tok_k), ya_in, up_a, dy_a, up_b,
                     [_sds((N_DEV, half, d8), BF16)], [up_o], (half, d8))[0]
    g_b_up = _matmul("g_b_up", "tn", (1, N_DEV, s // tok_k), attn, up_a, dy_b, up_b,
                     [_sds((N_DEV, half, d8), BF16)], [up_o], (half, d8))[0]
    dn_a = pl.BlockSpec((tm, d8), lambda i, j, k: (i, k))
    dn_b = pl.BlockSpec((None, half, d8), lambda i, j, k: (k, 0, 0))
    dn_o = pl.BlockSpec((tm, half), lambda i, j, k: (i, 0))
    dya_in = _matmul("d_ya_in", "nt", (s // tm, 1, N_DEV), dy_a, dn_a, w_a_f, dn_b,
                     [_sds((s, half), BF16)], [dn_o], (tm, half))[0]
    dattn = _matmul("d_attn", "nt", (s // tm, 1, N_DEV), dy_b, dn_a, w_b_f, dn_b,
                    [_sds((s, half), BF16)], [dn_o], (tm, half))[0]

    dproj, g_pool, g_pool_scale = _pool_backward(dya_in, pooled, w_pool_f, pool_scale, dproj)
    dqn, dkn, dproj = _attention_backward(qn, kn, vb, dattn, dproj, 3)
    dproj, g_qnorm = _qk_norm_backward("qnorm_bwd", dqn, proj, 1, q_norm_w, dproj, half)
    dproj, g_knorm = _qk_norm_backward("knorm_bwd", dkn, proj, 2, k_norm_w, dproj, half)

    g_in = _matmul(
        "g_in", "tn", (d // tk, N_DEV, s // tok_k), h, pl.BlockSpec((tok_k, tk), lambda i, j, k: (k, i)),
        dproj, pl.BlockSpec((tok_k, half), lambda i, j, k: (k, j)),
        [_sds((N_DEV, d, half), BF16)], [pl.BlockSpec((None, tk, half), lambda i, j, k: (j, i, 0))], (tk, half))[0]
    dh = _matmul(
        "dh", "nt", (s // tm, d // tn, N_DEV), dproj, pl.BlockSpec((tm, half), lambda i, j, k: (i, k)),
        w_in_f, pl.BlockSpec((None, tn, half), lambda i, j, k: (k, j, 0)),
        [_sds((s, d), F32)], [mn_blk], (tm, tn))[0]
    grad_x, dshift1, dscale1, g_norm1 = _norm_backward("norm1_bwd", dh, x2, norm1_w, scale1, dx1)

    dmod = jnp.concatenate([dshift1, dscale1, dgate1, dshift2, dscale2, dgate2], axis=1)
    pieces = [dmod, g_norm1, g_norm2, g_pool_scale, g_qnorm, g_knorm]
    packed_rows = [_rows_of_lanes(p) for p in pieces]
    offsets = [0]
    for p in packed_rows:
        offsets.append(offsets[-1] + p.shape[0])
    packed = jnp.concatenate(packed_rows, axis=0)
    small_all = _all_gather_2d("ag_small", packed)
    small_sum = _sum_slots("small_sum", small_all[None], F32)[0]

    def unpack(i, width):
        return small_sum[offsets[i]:offsets[i] + width // LANES].reshape(1, width)

    g_b_ada = unpack(0, N_MOD * d)
    g_norm1_w = unpack(1, d)
    g_norm2_w = unpack(2, d)
    g_pool_scale_w = unpack(3, half)
    g_q_norm_w = unpack(4, HEAD_DIM)
    g_k_norm_w = unpack(5, HEAD_DIM)
    dmod_all = small_all[:, :N_MOD * d // LANES].reshape(N_DEV, N_MOD * d)
    dmod_cols = lax.dynamic_slice_in_dim(dmod_all, me * wa, wa, axis=1)
    g_w_ada = _ada_weight_grad(c_all, dmod_cols)[None]

    g_pool_send = g_pool.astype(BF16).reshape(n_groups, N_DEV, rows_pool, cg).transpose(1, 0, 2, 3)
    g_pool_send = g_pool_send.reshape(N_DEV, n_groups * rows_pool, cg)
    sends = [g_in, g_pool_send, g_a_up, g_b_up, g_o.reshape(N_DEV, d8, d), g_ff1, g_ff2.reshape(N_DEV, half, d)]
    g_w_in, g_w_pool, g_w_a_up, g_w_b_up, g_w_o, g_w_ff1, g_w_ff2 = _reduce_scatter_grads(sends)

    grads = {
        "w_ada": g_w_ada, "b_ada": g_b_ada, "norm1_w": g_norm1_w, "w_in": g_w_in[None],
        "q_norm_w": g_q_norm_w, "k_norm_w": g_k_norm_w,
        "w_pool": g_w_pool.reshape(w_pool.shape), "pool_scale": g_pool_scale_w,
        "w_a_up": g_w_a_up[None], "w_b_up": g_w_b_up[None], "w_o": g_w_o[None],
        "norm2_w": g_norm2_w, "w_ff1": g_w_ff1[None], "w_ff2": g_w_ff2[None],
    }
    weights = {"w_ada": (w_ada, m_w_ada, v_w_ada), "b_ada": (b_ada, m_b_ada, v_b_ada),
               "norm1_w": (norm1_w, m_norm1_w, v_norm1_w), "w_in": (w_in, m_w_in, v_w_in),
               "q_norm_w": (q_norm_w, m_q_norm_w, v_q_norm_w), "k_norm_w": (k_norm_w, m_k_norm_w, v_k_norm_w),
               "w_pool": (w_pool, m_w_pool, v_w_pool), "pool_scale": (pool_scale, m_pool_scale, v_pool_scale),
               "w_a_up": (w_a_up, m_w_a_up, v_w_a_up), "w_b_up": (w_b_up, m_w_b_up, v_w_b_up),
               "w_o": (w_o, m_w_o, v_w_o), "norm2_w": (norm2_w, m_norm2_w, v_norm2_w),
               "w_ff1": (w_ff1, m_w_ff1, v_w_ff1), "w_ff2": (w_ff2, m_w_ff2, v_w_ff2)}
    order = list(weights)
    deltas, new_m, new_v = {}, {}, {}
    for name in order:
        wt, mt, vt = weights[name]
        shape = wt.shape
        flat = (-1, shape[-1])
        dl, nm, nv = _adamw("adamw_" + name, wt.reshape(flat), grads[name].reshape(flat),
                            mt.reshape(flat), vt.reshape(flat))
        deltas[name], new_m[name], new_v[name] = dl.reshape(shape), nm.reshape(shape), nv.reshape(shape)

    return (loss, grad_x[None], *[grads[n] for n in order], *[deltas[n] for n in order],
            *[new_m[n] for n in order], *[new_v[n] for n in order])
```

```python
import math

import jax
import jax.numpy as jnp
from jax import lax
from jax.experimental import pallas as pl
from jax.experimental.pallas import tpu as pltpu

F32 = jnp.float32
BF16 = jnp.bfloat16
MESH_AXES = ("x", "y", "c")
N_DEV = 8
HEAD_DIM = 128
POOL_WINDOWS = (2, 4, 8, 16)
N_MOD = 6
NORM_EPS = 1e-6
LANES = 128
SUBLANES = 8
VMEM_LIMIT_BYTES = 56 * 1024 * 1024
Q_TILE = 256
K_TILE = 256
POOL_TILE = 256

ADAM_LR = 0.001
ADAM_B1 = 0.9
ADAM_B2 = 0.999
ADAM_EPS = 1e-08
ADAM_WD = 0.01
ADAM_STEP = 10

_NN = (((1,), (0,)), ((), ()))
_NT = (((1,), (1,)), ((), ()))
_TN = (((0,), (0,)), ((), ()))
_DIMS = {"nn": _NN, "nt": _NT, "tn": _TN}


def _dot(a, b, mode="nn"):
    return lax.dot_general(a, b, _DIMS[mode], preferred_element_type=F32)


def _params(*sem):
    return pltpu.CompilerParams(dimension_semantics=sem, vmem_limit_bytes=VMEM_LIMIT_BYTES)


def _tile(dim, pref, align=SUBLANES):
    for t in range(min(dim, pref), 0, -1):
        if dim % t == 0 and t % align == 0:
            return t
    return dim


def _group_index(axes):
    idx = 0
    for a in axes:
        idx = idx * 2 + lax.axis_index(a)
    return idx


def _peer_device(axes, k):
    coords = {a: lax.axis_index(a) for a in MESH_AXES}
    for pos, a in enumerate(axes):
        if (k >> (len(axes) - 1 - pos)) & 1:
            coords[a] = 1 - coords[a]
    return tuple(coords[a] for a in MESH_AXES)


_AXIS_BIT = {"x": 4, "y": 2, "c": 1}
_ANY = pl.BlockSpec(memory_space=pl.ANY)


def _device_xor(mask):
    return tuple(1 - lax.axis_index(a) if mask & _AXIS_BIT[a] else lax.axis_index(a) for a in MESH_AXES)


def _remote(src, dst, send_sem, recv_sem, mask):
    return pltpu.make_async_remote_copy(src_ref=src, dst_ref=dst, send_sem=send_sem, recv_sem=recv_sem,
                                        device_id=_device_xor(mask), device_id_type=pl.DeviceIdType.MESH)


def _start_all(copies):
    for cp in copies:
        cp.start()


def _wait_all(copies):
    for cp in copies:
        cp.wait()


def _gather_ici(name, bufs):
    na = len(bufs)
    bx, by = _AXIS_BIT["x"], _AXIS_BIT["y"]

    def body(*refs):
        out = refs[na:2 * na]
        send_sems, recv_sems = refs[2 * na:]
        me = _group_index(MESH_AXES)

        def copy(a, n, slot, color, mask):
            half = out[a].shape[1] // 2
            piece = out[a].at[slot, pl.ds(color * half, half)]
            return _remote(piece, piece, send_sems.at[a, n], recv_sems.at[a, n], mask)

        first, second = [], []
        for a in range(na):
            first += [copy(a, 0, me, 0, bx), copy(a, 1, me, 1, by)]
            second += [copy(a, 2, me, 0, by), copy(a, 3, me ^ bx, 0, by),
                       copy(a, 4, me, 1, bx), copy(a, 5, me ^ by, 1, bx)]
        _start_all(first)
        _wait_all(first)
        _start_all(second)
        _wait_all(second)

    return pl.pallas_call(
        body, name=name, out_shape=tuple(jax.ShapeDtypeStruct(b.shape, b.dtype) for b in bufs),
        in_specs=[_ANY] * na, out_specs=tuple([_ANY] * na),
        input_output_aliases={a: a for a in range(na)},
        scratch_shapes=[pltpu.SemaphoreType.DMA((na, 6)), pltpu.SemaphoreType.DMA((na, 6))],
    )(*bufs)


def _gather_d2d(name, bufs):
    na = len(bufs)
    masks = (0, _AXIS_BIT["y"], _AXIS_BIT["x"], _AXIS_BIT["x"] | _AXIS_BIT["y"])

    def body(*refs):
        out = refs[na:2 * na]
        send_sems, recv_sems = refs[2 * na:]
        me = _group_index(MESH_AXES)
        copies = []
        for a in range(na):
            for n, m in enumerate(masks):
                piece = out[a].at[me ^ m]
                copies.append(_remote(piece, piece, send_sems.at[a, n], recv_sems.at[a, n], _AXIS_BIT["c"]))
        _start_all(copies)
        _wait_all(copies)

    return pl.pallas_call(
        body, name=name, out_shape=tuple(jax.ShapeDtypeStruct(b.shape, b.dtype) for b in bufs),
        in_specs=[_ANY] * na, out_specs=tuple([_ANY] * na),
        input_output_aliases={a: a for a in range(na)},
        scratch_shapes=[pltpu.SemaphoreType.DMA((na, 4)), pltpu.SemaphoreType.DMA((na, 4))],
    )(*bufs)


def _scatter(name, srcs, send_slots, masks):
    na = len(srcs)
    nm = len(masks)

    def body(*refs):
        src, out = refs[:na], refs[na:2 * na]
        send_sems, recv_sems = refs[2 * na:]
        slots = send_slots(_group_index(MESH_AXES))
        copies = []
        for n, m in enumerate(masks):
            for a in range(na):
                copies.append(_remote(src[a].at[slots[n]], out[a].at[n],
                                      send_sems.at[a, n], recv_sems.at[a, n], m))
        _start_all(copies)
        _wait_all(copies)

    return pl.pallas_call(
        body, name=name,
        out_shape=tuple(jax.ShapeDtypeStruct((nm,) + s.shape[1:], s.dtype) for s in srcs),
        in_specs=[_ANY] * na, out_specs=tuple([_ANY] * na),
        scratch_shapes=[pltpu.SemaphoreType.DMA((na, nm)), pltpu.SemaphoreType.DMA((na, nm))],
    )(*srcs)


def _add_received(name, own, own_slots, received, out_dtype):
    nj, r, c = received.shape
    tr = _tile(r, max(2 * SUBLANES, (1 << 20) // c), 2 * SUBLANES)

    def body(slots_ref, own_ref, rec_ref, o_ref):
        del slots_ref
        o_ref[...] = (own_ref[...].astype(F32) + rec_ref[...].astype(F32)).astype(o_ref.dtype)

    grid_spec = pltpu.PrefetchScalarGridSpec(
        num_scalar_prefetch=1, grid=(nj, r // tr),
        in_specs=[pl.BlockSpec((None, tr, c), lambda j, i, slots: (slots[j], i, 0)),
                  pl.BlockSpec((None, tr, c), lambda j, i, slots: (j, i, 0))],
        out_specs=pl.BlockSpec((None, tr, c), lambda j, i, slots: (j, i, 0)))
    return pl.pallas_call(
        body, name=name, grid_spec=grid_spec, out_shape=jax.ShapeDtypeStruct((nj, r, c), out_dtype),
        compiler_params=_params("parallel", "parallel"),
    )(own_slots, own, received)


def _add_final(name, own, received):
    nj, r, c = received.shape
    tr = _tile(r, max(2 * SUBLANES, (1 << 19) // c), 2 * SUBLANES)

    def body(own_ref, rec_ref, o_ref):
        acc = own_ref[...].astype(F32)
        for j in range(nj):
            acc = acc + rec_ref[j].astype(F32)
        o_ref[...] = acc

    return pl.pallas_call(
        body, name=name, grid=(r // tr,), out_shape=jax.ShapeDtypeStruct((r, c), F32),
        in_specs=[pl.BlockSpec((None, tr, c), lambda i: (0, i, 0)),
                  pl.BlockSpec((nj, tr, c), lambda i: (0, i, 0))],
        out_specs=pl.BlockSpec((tr, c), lambda i: (i, 0)),
        compiler_params=_params("parallel"),
    )(own, received)


def _all_gather_2d(name, x):
    r, c = x.shape

    def body(x_ref, out_ref, send_sems, recv_sems):
        me = _group_index(MESH_AXES)
        out_ref[me] = x_ref[...]
        copies = []
        for k in range(1, N_DEV):
            cp = pltpu.make_async_remote_copy(
                src_ref=x_ref, dst_ref=out_ref.at[me],
                send_sem=send_sems.at[k - 1], recv_sem=recv_sems.at[k - 1],
                device_id=_peer_device(MESH_AXES, k), device_id_type=pl.DeviceIdType.MESH)
            cp.start()
            copies.append(cp)
        for cp in copies:
            cp.wait()

    vmem = pl.BlockSpec(memory_space=pltpu.VMEM)
    return pl.pallas_call(
        body, name=name, out_shape=jax.ShapeDtypeStruct((N_DEV, r, c), x.dtype),
        in_specs=[vmem], out_specs=vmem,
        scratch_shapes=[pltpu.SemaphoreType.DMA((N_DEV - 1,)), pltpu.SemaphoreType.DMA((N_DEV - 1,))],
    )(x)


def _sum_slots(name, buf, out_dtype):
    pre, n, r, c = buf.shape
    tr = _tile(r, max(SUBLANES * 2, (1 << 20) // c))

    def body(b_ref, o_ref):
        acc = b_ref[0].astype(F32)
        for q in range(1, n):
            acc = acc + b_ref[q].astype(F32)
        o_ref[...] = acc.astype(o_ref.dtype)

    return pl.pallas_call(
        body, name=name, grid=(pre, r // tr),
        out_shape=jax.ShapeDtypeStruct((pre, r, c), out_dtype),
        in_specs=[pl.BlockSpec((None, n, tr, c), lambda i, j: (i, 0, j, 0))],
        out_specs=pl.BlockSpec((None, tr, c), lambda i, j: (i, j, 0)),
        compiler_params=_params("parallel", "parallel"),
    )(buf)


def _all_gather_weights(bufs):
    return _gather_d2d("ag_d2d", list(_gather_ici("ag_ici", bufs)))


_CHIP_MASKS = (0, _AXIS_BIT["y"], _AXIS_BIT["x"], _AXIS_BIT["x"] | _AXIS_BIT["y"])


def _reduce_scatter_grads(bufs):
    me = _group_index(MESH_AXES)
    bc = _AXIS_BIT["c"]
    r1 = _scatter("rs_d2d", bufs, lambda i: [i ^ bc ^ m for m in _CHIP_MASKS], (bc,) * len(_CHIP_MASKS))
    own_slots = jnp.stack([me ^ m for m in _CHIP_MASKS]).astype(jnp.int32)
    half = [_add_received("rs_add_d2d_%d" % a, b, own_slots, r, BF16) for a, (b, r) in enumerate(zip(bufs, r1))]
    r2 = _scatter("rs_ici", half, lambda i: [1, 2, 3], _CHIP_MASKS[1:])
    return [_add_final("rs_add_ici_%d" % a, h, r) for a, (h, r) in enumerate(zip(half, r2))]


def _matmul(name, mode, grid, a, a_spec, b, b_spec, out_shapes, out_specs, acc_shape,
            epilogue=None, extras=(), extra_specs=(), aliases=None):
    nk = grid[2]
    n_extra = len(extras)
    n_out = len(out_shapes)

    def finish(acc, extra_refs, out_refs):
        if epilogue is None:
            out_refs[0][...] = acc.astype(out_refs[0].dtype)
        else:
            epilogue(acc, extra_refs, out_refs)

    def body(*refs):
        a_ref, b_ref = refs[0], refs[1]
        extra_refs = refs[2:2 + n_extra]
        out_refs = refs[2 + n_extra:2 + n_extra + n_out]
        if nk == 1:
            finish(_dot(a_ref[...], b_ref[...], mode), extra_refs, out_refs)
            return
        acc_ref = refs[-1]
        k = pl.program_id(2)

        @pl.when(k == 0)
        def _():
            acc_ref[...] = jnp.zeros_like(acc_ref)

        acc_ref[...] += _dot(a_ref[...], b_ref[...], mode)

        @pl.when(k == nk - 1)
        def _():
            finish(acc_ref[...], extra_refs, out_refs)

    scratch = [] if nk == 1 else [pltpu.VMEM(acc_shape, F32)]
    return pl.pallas_call(
        body, name=name, grid=grid, out_shape=tuple(out_shapes),
        in_specs=[a_spec, b_spec] + list(extra_specs), out_specs=tuple(out_specs),
        scratch_shapes=scratch, input_output_aliases=aliases or {},
        compiler_params=_params("parallel", "parallel", "arbitrary"),
    )(a, b, *extras)


def _sds(shape, dtype):
    return jax.ShapeDtypeStruct(tuple(shape), dtype)


def _ada_forward(c_all, w_ada, b_shard):
    nb, d = c_all.shape
    w = w_ada.shape[1]
    tn = _tile(w, 512)

    def body(c_ref, w_ref, b_ref, o_ref):
        cv = c_ref[...]
        sc = cv * jax.nn.sigmoid(cv)
        o_ref[...] = jnp.dot(sc, w_ref[...], precision=lax.Precision.HIGHEST,
                             preferred_element_type=F32) + b_ref[...]

    return pl.pallas_call(
        body, name="ada_fwd", grid=(w // tn,), out_shape=_sds((nb, w), F32),
        in_specs=[pl.BlockSpec((nb, d), lambda j: (0, 0)), pl.BlockSpec((d, tn), lambda j: (0, j)),
                  pl.BlockSpec((1, tn), lambda j: (0, j))],
        out_specs=pl.BlockSpec((nb, tn), lambda j: (0, j)),
        compiler_params=_params("parallel"),
    )(c_all, w_ada, b_shard)


def _ada_weight_grad(c_all, dmod_cols):
    nb, d = c_all.shape
    w = dmod_cols.shape[1]
    tn = _tile(w, 512)

    def body(c_ref, g_ref, o_ref):
        cv = c_ref[...]
        sc = cv * jax.nn.sigmoid(cv)
        o_ref[...] = lax.dot_general(sc, g_ref[...], _TN, precision=lax.Precision.HIGHEST,
                                     preferred_element_type=F32)

    return pl.pallas_call(
        body, name="ada_wgrad", grid=(w // tn,), out_shape=_sds((d, w), F32),
        in_specs=[pl.BlockSpec((nb, d), lambda j: (0, 0)), pl.BlockSpec((nb, tn), lambda j: (0, j))],
        out_specs=pl.BlockSpec((d, tn), lambda j: (0, j)),
        compiler_params=_params("parallel"),
    )(c_all, dmod_cols)


def _norm_forward(name, x, norm_w, scale, shift):
    s, d = x.shape
    tm = _tile(s, 256)

    def body(x_ref, w_ref, sc_ref, sh_ref, h_ref):
        xv = x_ref[...]
        r = lax.rsqrt(jnp.mean(xv * xv, axis=-1, keepdims=True) + NORM_EPS)
        h = (xv * r * w_ref[...]) * (1.0 + sc_ref[...]) + sh_ref[...]
        h_ref[...] = h.astype(BF16)

    vec = pl.BlockSpec((1, d), lambda i: (0, 0))
    row = pl.BlockSpec((tm, d), lambda i: (i, 0))
    return pl.pallas_call(
        body, name=name, grid=(s // tm,), out_shape=_sds((s, d), BF16),
        in_specs=[row, vec, vec, vec], out_specs=row, compiler_params=_params("parallel"),
    )(x, norm_w, scale, shift)


def _norm_backward(name, dh, x, norm_w, scale, dres):
    s, d = x.shape
    tm = _tile(s, 256)

    def body(dh_ref, x_ref, w_ref, sc_ref, dres_ref, dx_ref, dshift_ref, dscale_ref, dw_ref):
        @pl.when(pl.program_id(0) == 0)
        def _():
            dshift_ref[...] = jnp.zeros_like(dshift_ref)
            dscale_ref[...] = jnp.zeros_like(dscale_ref)
            dw_ref[...] = jnp.zeros_like(dw_ref)

        xv = x_ref[...]
        g = dh_ref[...]
        r = lax.rsqrt(jnp.mean(xv * xv, axis=-1, keepdims=True) + NORM_EPS)
        n = xv * r
        gain = 1.0 + sc_ref[...]
        gn = g * n
        dshift_ref[...] += jnp.sum(g, axis=0, keepdims=True)
        dscale_ref[...] += jnp.sum(gn, axis=0, keepdims=True) * w_ref[...]
        dw_ref[...] += jnp.sum(gn, axis=0, keepdims=True) * gain
        dn = g * (w_ref[...] * gain)
        dx_ref[...] = dres_ref[...] + r * (dn - n * jnp.mean(dn * n, axis=-1, keepdims=True))

    vec = pl.BlockSpec((1, d), lambda i: (0, 0))
    row = pl.BlockSpec((tm, d), lambda i: (i, 0))
    return pl.pallas_call(
        body, name=name, grid=(s // tm,),
        out_shape=(_sds((s, d), F32), _sds((1, d), F32), _sds((1, d), F32), _sds((1, d), F32)),
        in_specs=[row, row, vec, vec, row], out_specs=(row, vec, vec, vec),
        compiler_params=_params("arbitrary"),
    )(dh, x, norm_w, scale, dres)


def _gate_backward(name, d, gate, other):
    s, w = d.shape
    tm = _tile(s, 256)

    def body(d_ref, g_ref, o_ref, dg_ref, dgate_ref):
        @pl.when(pl.program_id(0) == 0)
        def _():
            dgate_ref[...] = jnp.zeros_like(dgate_ref)

        dv = d_ref[...]
        dg_ref[...] = (dv * g_ref[...]).astype(BF16)
        dgate_ref[...] += jnp.sum(dv * o_ref[...].astype(F32), axis=0, keepdims=True)

    vec = pl.BlockSpec((1, w), lambda i: (0, 0))
    row = pl.BlockSpec((tm, w), lambda i: (i, 0))
    return pl.pallas_call(
        body, name=name, grid=(s // tm,), out_shape=(_sds((s, w), BF16), _sds((1, w), F32)),
        in_specs=[row, vec, row], out_specs=(row, vec), compiler_params=_params("arbitrary"),
    )(d, gate, other)


def _split_bf16(v):
    hi = v.astype(BF16)
    lo = (v - hi.astype(F32)).astype(BF16)
    return hi, lo


def _pool_forward(proj, w_pool, pool_scale):
    s = proj.shape[0]
    g_n, cg, _ = w_pool.shape
    t = POOL_TILE
    nt = s // t

    def body(cur_ref, prev_ref, wp_ref, sc_ref, pooled_ref, ya_ref):
        g = pl.program_id(0)
        ti = pl.program_id(1)
        win = jnp.left_shift(2, g)
        row = lax.broadcasted_iota(jnp.int32, (t, t), 0)
        col = lax.broadcasted_iota(jnp.int32, (t, t), 1)
        lag = row - col
        band_cur = ((lag >= 0) & (lag < win)).astype(BF16)
        band_prev = ((lag + t < win) & (ti > 0)).astype(BF16)
        u = cur_ref[...]
        u_hi, u_lo = _split_bf16(u)
        p_hi, p_lo = _split_bf16(prev_ref[...])
        wsum = (_dot(band_cur, u_hi) + _dot(band_cur, u_lo)
                + _dot(band_prev, p_hi) + _dot(band_prev, p_lo))
        tok = ti * t + lax.broadcasted_iota(jnp.int32, (t, 1), 0)
        count = jnp.minimum(tok + 1, win).astype(F32)
        pooled = (wsum / count - u).astype(BF16)
        pooled_ref[...] = pooled
        ya_ref[...] = (_dot(pooled, wp_ref[...]) * sc_ref[...]).astype(BF16)

    blk = pl.BlockSpec((t, cg), lambda g, i: (i, g))
    return pl.pallas_call(
        body, name="pool_fwd", grid=(g_n, nt),
        out_shape=(_sds((s, g_n * cg), BF16), _sds((s, g_n * cg), BF16)),
        in_specs=[blk, pl.BlockSpec((t, cg), lambda g, i: (jnp.maximum(i - 1, 0), g)),
                  pl.BlockSpec((None, cg, cg), lambda g, i: (g, 0, 0)),
                  pl.BlockSpec((1, cg), lambda g, i: (0, g))],
        out_specs=(blk, blk), compiler_params=_params("parallel", "parallel"),
    )(proj, proj, w_pool, pool_scale)


def _pool_backward(dya, pooled, w_pool, pool_scale, dproj):
    s = dya.shape[0]
    g_n, cg, _ = w_pool.shape
    t = POOL_TILE
    nt = s // t

    def body(dya_ref, dya_next_ref, pooled_ref, wp_ref, sc_ref, dproj_in, du_ref, gw_ref, gs_ref):
        del dproj_in
        g = pl.program_id(0)
        ti = pl.program_id(1)

        @pl.when(ti == 0)
        def _():
            gw_ref[...] = jnp.zeros_like(gw_ref)
            gs_ref[...] = jnp.zeros_like(gs_ref)

        win = jnp.left_shift(2, g)
        wp = wp_ref[...]
        sc = sc_ref[...]
        pooled_v = pooled_ref[...]
        dya_v = dya_ref[...].astype(F32)
        mixed = _dot(pooled_v, wp)
        gs_ref[...] += jnp.sum(dya_v * mixed, axis=0, keepdims=True)
        dmixed = (dya_v * sc).astype(BF16)
        gw_ref[...] += _dot(pooled_v, dmixed, "tn")
        dpooled = _dot(dmixed, wp, "nt")
        dmixed_next = (dya_next_ref[...].astype(F32) * sc).astype(BF16)
        dpooled_next = _dot(dmixed_next, wp, "nt")
        tok = ti * t + lax.broadcasted_iota(jnp.int32, (t, 1), 0)
        e_cur = dpooled / jnp.minimum(tok + 1, win).astype(F32)
        e_next = dpooled_next / jnp.minimum(tok + t + 1, win).astype(F32)
        row = lax.broadcasted_iota(jnp.int32, (t, t), 0)
        col = lax.broadcasted_iota(jnp.int32, (t, t), 1)
        lead = col - row
        band_cur = ((lead >= 0) & (lead < win)).astype(BF16)
        band_next = ((lead + t < win) & (ti < nt - 1)).astype(BF16)
        c_hi, c_lo = _split_bf16(e_cur)
        n_hi, n_lo = _split_bf16(e_next)
        du = (_dot(band_cur, c_hi) + _dot(band_cur, c_lo)
              + _dot(band_next, n_hi) + _dot(band_next, n_lo)) - dpooled
        du_ref[...] = du.astype(BF16)

    blk = pl.BlockSpec((t, cg), lambda g, i: (i, g))
    du, gw, gs = pl.pallas_call(
        body, name="pool_bwd", grid=(g_n, nt),
        out_shape=(_sds(dproj.shape, BF16), _sds((g_n, cg, cg), F32), _sds((1, g_n * cg), F32)),
        in_specs=[blk, pl.BlockSpec((t, cg), lambda g, i: (jnp.minimum(i + 1, nt - 1), g)), blk,
                  pl.BlockSpec((None, cg, cg), lambda g, i: (g, 0, 0)),
                  pl.BlockSpec((1, cg), lambda g, i: (0, g)),
                  pl.BlockSpec(memory_space=pl.ANY)],
        out_specs=(blk, pl.BlockSpec((None, cg, cg), lambda g, i: (g, 0, 0)),
                   pl.BlockSpec((1, cg), lambda g, i: (0, g))),
        input_output_aliases={5: 0}, compiler_params=_params("parallel", "arbitrary"),
    )(dya, dya, pooled, w_pool, pool_scale, dproj)
    return du, gw, gs


def _qkv_prepare(proj, q_norm_w, k_norm_w, width):
    s = proj.shape[0]
    tm = _tile(s, 256)
    heads = width // HEAD_DIM

    def body(q_ref, k_ref, v_ref, qw_ref, kw_ref, qn_ref, kn_ref, vb_ref):
        for h in range(heads):
            cols = slice(h * HEAD_DIM, (h + 1) * HEAD_DIM)
            for src, w_ref, dst in ((q_ref, qw_ref, qn_ref), (k_ref, kw_ref, kn_ref)):
                v = src[:, cols]
                r = lax.rsqrt(jnp.mean(v * v, axis=-1, keepdims=True) + NORM_EPS)
                dst[:, cols] = (v * r * w_ref[...]).astype(BF16)
        vb_ref[...] = v_ref[...].astype(BF16)

    vec = pl.BlockSpec((1, HEAD_DIM), lambda i: (0, 0))
    out_spec = pl.BlockSpec((tm, width), lambda i: (i, 0))
    return pl.pallas_call(
        body, name="qkv_prep", grid=(s // tm,),
        out_shape=(_sds((s, width), BF16),) * 3,
        in_specs=[pl.BlockSpec((tm, width), lambda i: (i, 1)), pl.BlockSpec((tm, width), lambda i: (i, 2)),
                  pl.BlockSpec((tm, width), lambda i: (i, 3)), vec, vec],
        out_specs=(out_spec,) * 3, compiler_params=_params("parallel"),
    )(proj, proj, proj, q_norm_w, k_norm_w)


def _qk_norm_backward(name, dn, proj, col_block, norm_w, dproj, width):
    s = proj.shape[0]
    tm = _tile(s, 256)
    heads = width // HEAD_DIM

    def body(dn_ref, q_ref, w_ref, dproj_in, dq_ref, gw_ref):
        del dproj_in

        @pl.when(pl.program_id(0) == 0)
        def _():
            gw_ref[...] = jnp.zeros_like(gw_ref)

        wv = w_ref[...]
        gw = jnp.zeros((1, HEAD_DIM), F32)
        for h in range(heads):
            cols = slice(h * HEAD_DIM, (h + 1) * HEAD_DIM)
            v = q_ref[:, cols]
            g = dn_ref[:, cols]
            r = lax.rsqrt(jnp.mean(v * v, axis=-1, keepdims=True) + NORM_EPS)
            n = v * r
            gw = gw + jnp.sum(g * n, axis=0, keepdims=True)
            gn = g * wv
            dq_ref[:, cols] = (r * (gn - n * jnp.mean(gn * n, axis=-1, keepdims=True))).astype(BF16)
        gw_ref[...] += gw

    blk = pl.BlockSpec((tm, width), lambda i: (i, col_block))
    return pl.pallas_call(
        body, name=name, grid=(s // tm,),
        out_shape=(_sds(dproj.shape, BF16), _sds((1, HEAD_DIM), F32)),
        in_specs=[pl.BlockSpec((tm, width), lambda i: (i, 0)), blk,
                  pl.BlockSpec((1, HEAD_DIM), lambda i: (0, 0)), pl.BlockSpec(memory_space=pl.ANY)],
        out_specs=(blk, pl.BlockSpec((1, HEAD_DIM), lambda i: (0, 0))),
        input_output_aliases={3: 0}, compiler_params=_params("arbitrary"),
    )(dn, proj, norm_w, dproj)


def _strict_upper(n):
    row = lax.broadcasted_iota(jnp.int32, (n, n), 0)
    col = lax.broadcasted_iota(jnp.int32, (n, n), 1)
    return (row > col).astype(BF16)


def _strict_lower(n):
    row = lax.broadcasted_iota(jnp.int32, (n, n), 0)
    col = lax.broadcasted_iota(jnp.int32, (n, n), 1)
    return (row < col).astype(BF16)


def _cumulate(v, tri):
    hi, lo = _split_bf16(v)
    return _dot(hi, tri) + _dot(lo, tri)


def _log_sigmoid(z):
    return jnp.minimum(z, 0.0) - jnp.log(1.0 + jnp.exp(-jnp.abs(z)))


def _attention_forward(qn, kn, vb):
    s, width = qn.shape
    heads = width // HEAD_DIM
    tq, tk = Q_TILE, K_TILE
    assert tq == tk and s % tq == 0
    scale = 1.0 / math.sqrt(HEAD_DIM)

    def body(q_ref, k_ref, v_ref, o_ref):
        qi = pl.program_id(1)
        q = q_ref[...]
        upper = _strict_upper(tk)
        causal = lax.broadcasted_iota(jnp.int32, (tq, tk), 1) < lax.broadcasted_iota(jnp.int32, (tq, tk), 0)

        def chunk(kb, carry, masked):
            acc, later = carry
            rows = pl.ds(pl.multiple_of(kb * tk, tk), tk)
            z = _dot(q, k_ref[rows, :], "nt") * scale
            log_beta = _log_sigmoid(z)
            l = log_beta - z
            if masked:
                l = jnp.where(causal, l, 0.0)
            w = log_beta + _cumulate(l, upper) + later
            a = jnp.exp(w)
            if masked:
                a = jnp.where(causal, a, 0.0)
            acc = acc + _dot(a.astype(BF16), v_ref[rows, :])
            return acc, later + jnp.sum(l, axis=1, keepdims=True)

        carry = (jnp.zeros((tq, HEAD_DIM), F32), jnp.zeros((tq, 1), F32))
        carry = chunk(qi, carry, True)
        acc, _ = lax.fori_loop(0, qi, lambda i, c: chunk(qi - 1 - i, c, False), carry)
        o_ref[...] = acc.astype(BF16)

    full = pl.BlockSpec((s, HEAD_DIM), lambda h, i: (0, h))
    blk = pl.BlockSpec((tq, HEAD_DIM), lambda h, i: (i, h))
    return pl.pallas_call(
        body, name="attn_fwd", grid=(heads, s // tq), out_shape=_sds((s, width), BF16),
        in_specs=[blk, full, full], out_specs=blk, compiler_params=_params("parallel", "parallel"),
    )(qn, kn, vb)


def _attention_backward(qn, kn, vb, dout, dproj, v_col_block):
    s, width = qn.shape
    heads = width // HEAD_DIM
    tq, tk = Q_TILE, K_TILE
    nq = s // tq
    scale = 1.0 / math.sqrt(HEAD_DIM)
    v_blocks_per_head = v_col_block * heads

    def body(q_ref, k_ref, v_ref, do_ref, dproj_in, dq_ref, dk_ref, dv_ref, a_scr, dk_scr, dv_scr):
        del dproj_in
        qi = pl.program_id(1)

        @pl.when(qi == 0)
        def _():
            dk_scr[...] = jnp.zeros_like(dk_scr)
            dv_scr[...] = jnp.zeros_like(dv_scr)

        q = q_ref[...]
        dout_v = do_ref[...]
        upper = _strict_upper(tk)
        lower = _strict_lower(tk)
        causal = lax.broadcasted_iota(jnp.int32, (tq, tk), 1) < lax.broadcasted_iota(jnp.int32, (tq, tk), 0)

        def weights(kb, later, masked):
            rows = pl.ds(pl.multiple_of(kb * tk, tk), tk)
            z = _dot(q, k_ref[rows, :], "nt") * scale
            log_beta = _log_sigmoid(z)
            l = log_beta - z
            if masked:
                l = jnp.where(causal, l, 0.0)
            a = jnp.exp(log_beta + _cumulate(l, upper) + later)
            if masked:
                a = jnp.where(causal, a, 0.0)
            a_scr[:, rows] = a
            return later + jnp.sum(l, axis=1, keepdims=True)

        later = weights(qi, jnp.zeros((tq, 1), F32), True)
        lax.fori_loop(0, qi, lambda i, c: weights(qi - 1 - i, c, False), later)

        def grads(kb, carry, masked):
            dq, before = carry
            rows = pl.ds(pl.multiple_of(kb * tk, tk), tk)
            k_blk = k_ref[rows, :]
            z = _dot(q, k_blk, "nt") * scale
            beta = jax.nn.sigmoid(z)
            a = a_scr[:, rows]
            g = a * _dot(dout_v, v_ref[rows, :], "nt")
            p = _cumulate(g, lower) + before
            dz = g * (1.0 - beta) - p * beta
            if masked:
                dz = jnp.where(causal, dz, 0.0)
            dz = (dz * scale).astype(BF16)
            dk_scr[rows, :] += _dot(dz, q, "tn")
            dv_scr[rows, :] += _dot(a.astype(BF16), dout_v, "tn")
            return dq + _dot(dz, k_blk), before + jnp.sum(g, axis=1, keepdims=True)

        carry = (jnp.zeros((tq, HEAD_DIM), F32), jnp.zeros((tq, 1), F32))
        carry = lax.fori_loop(0, qi, lambda i, c: grads(i, c, False), carry)
        dq, _ = grads(qi, carry, True)
        dq_ref[...] = dq

        @pl.when(qi == nq - 1)
        def _():
            dk_ref[...] = dk_scr[...]
            dv_ref[...] = dv_scr[...].astype(BF16)

    full = pl.BlockSpec((s, HEAD_DIM), lambda h, i: (0, h))
    blk = pl.BlockSpec((tq, HEAD_DIM), lambda h, i: (i, h))
    return pl.pallas_call(
        body, name="attn_bwd", grid=(heads, nq),
        out_shape=(_sds((s, width), F32), _sds((s, width), F32), _sds(dproj.shape, BF16)),
        in_specs=[blk, full, full, blk, pl.BlockSpec(memory_space=pl.ANY)],
        out_specs=(blk, full, pl.BlockSpec((s, HEAD_DIM), lambda h, i: (0, v_blocks_per_head + h))),
        scratch_shapes=[pltpu.VMEM((tq, s), F32), pltpu.VMEM((s, HEAD_DIM), F32), pltpu.VMEM((s, HEAD_DIM), F32)],
        input_output_aliases={4: 2}, compiler_params=_params("parallel", "arbitrary"),
    )(qn, kn, vb, dout, dproj)


def _place_columns(name, src, dst, col_block):
    s, w = src.shape
    tm = _tile(s, 512)

    def body(src_ref, dst_in, out_ref):
        del dst_in
        out_ref[...] = src_ref[...]

    return pl.pallas_call(
        body, name=name, grid=(s // tm,), out_shape=_sds(dst.shape, dst.dtype),
        in_specs=[pl.BlockSpec((tm, w), lambda i: (i, 0)), pl.BlockSpec(memory_space=pl.ANY)],
        out_specs=pl.BlockSpec((tm, w), lambda i: (i, col_block)),
        input_output_aliases={1: 0}, compiler_params=_params("parallel"),
    )(src, dst)


def _cast_into_slot(name, x, slot):
    r, c = x.shape
    tr = _tile(r, max(SUBLANES * 2, (1 << 20) // c), SUBLANES * 2)

    def body(slot_ref, x_ref, o_ref):
        del slot_ref
        o_ref[...] = x_ref[...].astype(BF16)

    grid_spec = pltpu.PrefetchScalarGridSpec(
        num_scalar_prefetch=1, grid=(r // tr,),
        in_specs=[pl.BlockSpec((tr, c), lambda i, slot_ref: (i, 0))],
        out_specs=pl.BlockSpec((None, tr, c), lambda i, slot_ref: (slot_ref[0], i, 0)))
    return pl.pallas_call(
        body, name=name, grid_spec=grid_spec, out_shape=_sds((N_DEV, r, c), BF16),
        compiler_params=_params("parallel"),
    )(slot, x)


def _adamw(name, w, g, m, v):
    r, c = w.shape
    tr = _tile(r, max(SUBLANES, (1 << 19) // c))
    c1 = 1.0 - ADAM_B1 ** ADAM_STEP
    c2 = 1.0 - ADAM_B2 ** ADAM_STEP

    def body(w_ref, g_ref, m_ref, v_ref, d_ref, nm_ref, nv_ref):
        gv = g_ref[...]
        nm = ADAM_B1 * m_ref[...] + (1.0 - ADAM_B1) * gv
        nv = ADAM_B2 * v_ref[...] + (1.0 - ADAM_B2) * (gv * gv)
        d_ref[...] = -ADAM_LR * ((nm / c1) / (jnp.sqrt(nv / c2) + ADAM_EPS) + ADAM_WD * w_ref[...])
        nm_ref[...] = nm
        nv_ref[...] = nv

    blk = pl.BlockSpec((tr, c), lambda i: (i, 0))
    return pl.pallas_call(
        body, name=name, grid=(r // tr,), out_shape=(_sds((r, c), F32),) * 3,
        in_specs=[blk] * 4, out_specs=(blk,) * 3, compiler_params=_params("parallel"),
    )(w, g, m, v)


def _rows_of_lanes(v):
    rows = v.shape[1] // LANES
    out = v.reshape(rows, LANES)
    pad = (-rows) % SUBLANES
    if pad:
        out = jnp.pad(out, ((0, pad), (0, 0)))
    return out


def kernel(x, c, w_ada, b_ada, norm1_w, w_in, q_norm_w, k_norm_w, w_pool, pool_scale, w_a_up, w_b_up, w_o, norm2_w, w_ff1, w_ff2, loss_target, m_w_ada, m_b_ada, m_norm1_w, m_w_in, m_q_norm_w, m_k_norm_w, m_w_pool, m_pool_scale, m_w_a_up, m_w_b_up, m_w_o, m_norm2_w, m_w_ff1, m_w_ff2, v_w_ada, v_b_ada, v_norm1_w, v_w_in, v_q_norm_w, v_k_norm_w, v_w_pool, v_pool_scale, v_w_a_up, v_w_b_up, v_w_o, v_norm2_w, v_w_ff1, v_w_ff2):
    _, s, d = x.shape
    half = d // 2
    d8 = d // N_DEV
    n_groups = len(POOL_WINDOWS)
    cg = half // n_groups
    me = _group_index(MESH_AXES)

    x2 = x[0]
    target = loss_target[0]

    c_all = _all_gather_2d("ag_c", c.reshape(d // LANES, LANES)).reshape(N_DEV, d)
    wa = w_ada.shape[2]
    b_shard = lax.dynamic_slice_in_dim(b_ada, me * wa, wa, axis=1)
    mod_part = _ada_forward(c_all, w_ada[0], b_shard)
    mod_all = _all_gather_2d("ag_mod", mod_part.reshape(N_DEV * wa // LANES, LANES))
    mod_all = mod_all.reshape(N_DEV, N_DEV, wa)
    mod = lax.dynamic_slice_in_dim(mod_all, me, 1, axis=1).reshape(1, N_MOD * d)
    shift1, scale1, gate1, shift2, scale2, gate2 = [mod[:, i * d:(i + 1) * d] for i in range(N_MOD)]

    shards = [w_in[0], w_pool[0].reshape(-1, cg), w_a_up[0], w_b_up[0], w_o[0], w_ff1[0], w_ff2[0]]
    my_slot = jnp.reshape(me, (1,)).astype(jnp.int32)
    shards = [_cast_into_slot("cast_w%d" % i, t, my_slot) for i, t in enumerate(shards)]
    w_in_f, w_pool_f, w_a_f, w_b_f, w_o_f, w_ff1_f, w_ff2_f = _all_gather_weights(shards)
    rows_pool = cg // N_DEV
    w_pool_f = w_pool_f.reshape(N_DEV, n_groups, rows_pool, cg).transpose(1, 0, 2, 3).reshape(n_groups, cg, cg)
    w_o_f = w_o_f.reshape(d, d)
    w_ff2_f = w_ff2_f.reshape(4 * d, d)

    tm = _tile(s, 512)
    tk = _tile(d, 1024)

    h = _norm_forward("norm1_fwd", x2, norm1_w, scale1, shift1)
    proj = _matmul(
        "proj", "nn", (s // tm, N_DEV, d // tk), h, pl.BlockSpec((tm, tk), lambda i, j, k: (i, k)),
        w_in_f, pl.BlockSpec((None, tk, half), lambda i, j, k: (j, k, 0)),
        [_sds((s, 4 * d), F32)], [pl.BlockSpec((tm, half), lambda i, j, k: (i, j))], (tm, half))[0]

    pooled, ya_in = _pool_forward(proj, w_pool_f, pool_scale)
    qn, kn, vb = _qkv_prepare(proj, q_norm_w, k_norm_w, half)
    attn = _attention_forward(qn, kn, vb)

    def merge_epilogue(ga_ref, gb_ref, ya, yb, out_refs):
        merged_ref, ya_ref, yb_ref = out_refs
        merged = jax.nn.sigmoid(ga_ref[...]) * ya + jax.nn.sigmoid(gb_ref[...]) * yb
        merged_ref[...] = merged.astype(BF16)
        ya_ref[...] = ya.astype(BF16)
        yb_ref[...] = yb.astype(BF16)

    def up_body(a1_ref, b1_ref, a2_ref, b2_ref, ga_ref, gb_ref, *out_refs):
        merge_epilogue(ga_ref, gb_ref, _dot(a1_ref[...], b1_ref[...]), _dot(a2_ref[...], b2_ref[...]), out_refs)

    ga_blk0 = 2 * d // d8
    gb_blk0 = 3 * d // d8
    a_spec = pl.BlockSpec((tm, half), lambda i, j: (i, 0))
    wup_spec = pl.BlockSpec((None, half, d8), lambda i, j: (j, 0, 0))
    o_blk = pl.BlockSpec((tm, d8), lambda i, j: (i, j))
    merged, y_a, y_b = pl.pallas_call(
        up_body, name="up_merge", grid=(s // tm, N_DEV), out_shape=(_sds((s, d), BF16),) * 3,
        in_specs=[a_spec, wup_spec, a_spec, wup_spec,
                  pl.BlockSpec((tm, d8), lambda i, j: (i, ga_blk0 + j)),
                  pl.BlockSpec((tm, d8), lambda i, j: (i, gb_blk0 + j))],
        out_specs=(o_blk,) * 3, compiler_params=_params("parallel", "parallel"),
    )(ya_in, w_a_f, attn, w_b_f, proj, proj)

    tn = _tile(d, 1024)

    def oproj_epilogue(acc, extra_refs, out_refs):
        x_ref, g_ref = extra_refs
        x1_ref, o_ref = out_refs
        x1_ref[...] = x_ref[...] + g_ref[...] * acc
        o_ref[...] = acc.astype(BF16)

    mn_blk = pl.BlockSpec((tm, tn), lambda i, j, k: (i, j))
    n_vec = pl.BlockSpec((1, tn), lambda i, j, k: (0, j))
    x1, o_act = _matmul(
        "oproj", "nn", (s // tm, d // tn, d // tk), merged, pl.BlockSpec((tm, tk), lambda i, j, k: (i, k)),
        w_o_f, pl.BlockSpec((tk, tn), lambda i, j, k: (k, j)),
        [_sds((s, d), F32), _sds((s, d), BF16)], [mn_blk, mn_blk], (tm, tn),
        epilogue=oproj_epilogue, extras=(x2, gate1), extra_specs=(mn_blk, n_vec))

    h2 = _norm_forward("norm2_fwd", x1, norm2_w, scale2, shift2)

    def ff1_epilogue(acc, extra_refs, out_refs):
        r = jnp.maximum(acc, 0.0)
        out_refs[0][...] = r.astype(BF16)
        out_refs[1][...] = (r * r).astype(BF16)

    ff_blk = pl.BlockSpec((tm, half), lambda i, j, k: (i, j))
    relu, act = _matmul(
        "ff1", "nn", (s // tm, N_DEV, d // tk), h2, pl.BlockSpec((tm, tk), lambda i, j, k: (i, k)),
        w_ff1_f, pl.BlockSpec((None, tk, half), lambda i, j, k: (j, k, 0)),
        [_sds((s, 4 * d), BF16)] * 2, [ff_blk, ff_blk], (tm, half), epilogue=ff1_epilogue)

    def ff2_epilogue(acc, extra_refs, out_refs):
        x1_ref, g_ref, t_ref = extra_refs
        f_ref, dy_ref, sq_ref = out_refs
        err = x1_ref[...] + g_ref[...] * acc - t_ref[...]
        f_ref[...] = acc.astype(BF16)
        dy_ref[...] = err * (1.0 / d)
        sq_ref[...] = jnp.full(sq_ref.shape, jnp.sum(err * err), F32)

    f_act, dy, sq = _matmul(
        "ff2", "nn", (s // tm, d // tn, 4 * d // tk), act, pl.BlockSpec((tm, tk), lambda i, j, k: (i, k)),
        w_ff2_f, pl.BlockSpec((tk, tn), lambda i, j, k: (k, j)),
        [_sds((s, d), BF16), _sds((s, d), F32), _sds((s // tm * SUBLANES, d // tn * LANES), F32)],
        [mn_blk, mn_blk, pl.BlockSpec((SUBLANES, LANES), lambda i, j, k: (i, j))], (tm, tn),
        epilogue=ff2_epilogue, extras=(x1, gate2, target), extra_specs=(mn_blk, n_vec, mn_blk))
    loss_local = (0.5 / d) * jnp.sum(sq[::SUBLANES, ::LANES])
    loss = lax.psum(loss_local, MESH_AXES)

    df, dgate2 = _gate_backward("gate2_bwd", dy, gate2, f_act)
    tok_k = _tile(s, 1024)
    g_ff2 = _matmul(
        "g_ff2", "tn", (4 * d // tk, d // tn, s // tok_k), act, pl.BlockSpec((tok_k, tk), lambda i, j, k: (k, i)),
        df, pl.BlockSpec((tok_k, tn), lambda i, j, k: (k, j)),
        [_sds((4 * d, d), BF16)], [pl.BlockSpec((tk, tn), lambda i, j, k: (i, j))], (tk, tn))[0]

    def da_epilogue(acc, extra_refs, out_refs):
        out_refs[0][...] = (acc * (2.0 * extra_refs[0][...].astype(F32))).astype(BF16)

    big_blk = pl.BlockSpec((tm, tn), lambda i, j, k: (i, j))
    df1 = _matmul(
        "da_ff", "nt", (s // tm, 4 * d // tn, d // tk), df, pl.BlockSpec((tm, tk), lambda i, j, k: (i, k)),
        w_ff2_f, pl.BlockSpec((tn, tk), lambda i, j, k: (j, k)),
        [_sds((s, 4 * d), BF16)], [big_blk], (tm, tn),
        epilogue=da_epilogue, extras=(relu,), extra_specs=(big_blk,))[0]

    g_ff1 = _matmul(
        "g_ff1", "tn", (d // tk, N_DEV, s // tok_k), h2, pl.BlockSpec((tok_k, tk), lambda i, j, k: (k, i)),
        df1, pl.BlockSpec((tok_k, half), lambda i, j, k: (k, j)),
        [_sds((N_DEV, d, half), BF16)], [pl.BlockSpec((None, tk, half), lambda i, j, k: (j, i, 0))], (tk, half))[0]

    dh2 = _matmul(
        "dh2", "nt", (s // tm, d // tn, N_DEV), df1, pl.BlockSpec((tm, half), lambda i, j, k: (i, k)),
        w_ff1_f, pl.BlockSpec((None, tn, half), lambda i, j, k: (k, j, 0)),
        [_sds((s, d), F32)], [mn_blk], (tm, tn))[0]

    dx1, dshift2, dscale2, g_norm2 = _norm_backward("norm2_bwd", dh2, x1, norm2_w, scale2, dy)

    do, dgate1 = _gate_backward("gate1_bwd", dx1, gate1, o_act)
    g_o = _matmul(
        "g_o", "tn", (d // tk, d // tn, s // tok_k), merged, pl.BlockSpec((tok_k, tk), lambda i, j, k: (k, i)),
        do, pl.BlockSpec((tok_k, tn), lambda i, j, k: (k, j)),
        [_sds((d, d), BF16)], [pl.BlockSpec((tk, tn), lambda i, j, k: (i, j))], (tk, tn))[0]

    def merge_bwd_epilogue(acc, extra_refs, out_refs):
        ga_ref, gb_ref, ya_ref, yb_ref = extra_refs
        dya_ref, dyb_ref, dga_ref, dgb_ref = out_refs
        sa = jax.nn.sigmoid(ga_ref[...])
        sb = jax.nn.sigmoid(gb_ref[...])
        dya_ref[...] = (acc * sa).astype(BF16)
        dyb_ref[...] = (acc * sb).astype(BF16)
        dga_ref[...] = (acc * ya_ref[...].astype(F32) * (sa * (1.0 - sa))).astype(BF16)
        dgb_ref[...] = (acc * yb_ref[...].astype(F32) * (sb * (1.0 - sb))).astype(BF16)

    nb = d // tn
    dy_a, dy_b, dproj, dg_b = _matmul(
        "dmerged", "nt", (s // tm, nb, d // tk), do, pl.BlockSpec((tm, tk), lambda i, j, k: (i, k)),
        w_o_f, pl.BlockSpec((tn, tk), lambda i, j, k: (j, k)),
        [_sds((s, d), BF16), _sds((s, d), BF16), _sds((s, 4 * d), BF16), _sds((s, d), BF16)],
        [mn_blk, mn_blk, pl.BlockSpec((tm, tn), lambda i, j, k: (i, 2 * nb + j)), mn_blk], (tm, tn),
        epilogue=merge_bwd_epilogue, extras=(proj, proj, y_a, y_b),
        extra_specs=(pl.BlockSpec((tm, tn), lambda i, j, k: (i, 2 * nb + j)),
                     pl.BlockSpec((tm, tn), lambda i, j, k: (i, 3 * nb + j)), mn_blk, mn_blk))
    dproj = _place_columns("place_dgb", dg_b, dproj, 3)

    up_a = pl.BlockSpec((tok_k, half), lambda i, j, k: (k, 0))
    up_b = pl.BlockSpec((tok_k, d8), lambda i, j, k: (k, j))
    up_o = pl.BlockSpec((None, half, d8), lambda i, j, k: (j, 0, 0))
    g_a_up = _matmul("g_a_up", "tn", (1, N_DEV, s // tok_k), ya_in, up_a, dy_a, up_b,
                     [_sds((N_DEV, half, d8), BF16)], [up_o], (half, d8))[0]
    g_b_up = _matmul("g_b_up", "tn", (1, N_DEV, s // tok_k), attn, up_a, dy_b, up_b,
                     [_sds((N_DEV, half, d8), BF16)], [up_o], (half, d8))[0]
    dn_a = pl.BlockSpec((tm, d8), lambda i, j, k: (i, k))
    dn_b = pl.BlockSpec((None, half, d8), lambda i, j, k: (k, 0, 0))
    dn_o = pl.BlockSpec((tm, half), lambda i, j, k: (i, 0))
    dya_in = _matmul("d_ya_in", "nt", (s // tm, 1, N_DEV), dy_a, dn_a, w_a_f, dn_b,
                     [_sds((s, half), BF16)], [dn_o], (tm, half))[0]
    dattn = _matmul("d_attn", "nt", (s // tm, 1, N_DEV), dy_b, dn_a, w_b_f, dn_b,
                    [_sds((s, half), BF16)], [dn_o], (tm, half))[0]

    dproj, g_pool, g_pool_scale = _pool_backward(dya_in, pooled, w_pool_f, pool_scale, dproj)
    dqn, dkn, dproj = _attention_backward(qn, kn, vb, dattn, dproj, 3)
    dproj, g_qnorm = _qk_norm_backward("qnorm_bwd", dqn, proj, 1, q_norm_w, dproj, half)
    dproj, g_knorm = _qk_norm_backward("knorm_bwd", dkn, proj, 2, k_norm_w, dproj, half)

    g_in = _matmul(
        "g_in", "tn", (d // tk, N_DEV, s // tok_k), h, pl.BlockSpec((tok_k, tk), lambda i, j, k: (k, i)),
        dproj, pl.BlockSpec((tok_k, half), lambda i, j, k: (k, j)),
        [_sds((N_DEV, d, half), BF16)], [pl.BlockSpec((None, tk, half), lambda i, j, k: (j, i, 0))], (tk, half))[0]
    dh = _matmul(
        "dh", "nt", (s // tm, d // tn, N_DEV), dproj, pl.BlockSpec((tm, half), lambda i, j, k: (i, k)),
        w_in_f, pl.BlockSpec((None, tn, half), lambda i, j, k: (k, j, 0)),
        [_sds((s, d), F32)], [mn_blk], (tm, tn))[0]
    grad_x, dshift1, dscale1, g_norm1 = _norm_backward("norm1_bwd", dh, x2, norm1_w, scale1, dx1)

    dmod = jnp.concatenate([dshift1, dscale1, dgate1, dshift2, dscale2, dgate2], axis=1)
    pieces = [dmod, g_norm1, g_norm2, g_pool_scale, g_qnorm, g_knorm]
    packed_rows = [_rows_of_lanes(p) for p in pieces]
    offsets = [0]
    for p in packed_rows:
        offsets.append(offsets[-1] + p.shape[0])
    packed = jnp.concatenate(packed_rows, axis=0)
    small_all = _all_gather_2d("ag_small", packed)
    small_sum = _sum_slots("small_sum", small_all[None], F32)[0]

    def unpack(i, width):
        return small_sum[offsets[i]:offsets[i] + width // LANES].reshape(1, width)

    g_b_ada = unpack(0, N_MOD * d)
    g_norm1_w = unpack(1, d)
    g_norm2_w = unpack(2, d)
    g_pool_scale_w = unpack(3, half)
    g_q_norm_w = unpack(4, HEAD_DIM)
    g_k_norm_w = unpack(5, HEAD_DIM)
    dmod_all = small_all[:, :N_MOD * d // LANES].reshape(N_DEV, N_MOD * d)
    dmod_cols = lax.dynamic_slice_in_dim(dmod_all, me * wa, wa, axis=1)
    g_w_ada = _ada_weight_grad(c_all, dmod_cols)[None]

    g_pool_send = g_pool.astype(BF16).reshape(n_groups, N_DEV, rows_pool, cg).transpose(1, 0, 2, 3)
    g_pool_send = g_pool_send.reshape(N_DEV, n_groups * rows_pool, cg)
    sends = [g_in, g_pool_send, g_a_up, g_b_up, g_o.reshape(N_DEV, d8, d), g_ff1, g_ff2.reshape(N_DEV, half, d)]
    g_w_in, g_w_pool, g_w_a_up, g_w_b_up, g_w_o, g_w_ff1, g_w_ff2 = _reduce_scatter_grads(sends)

    grads = {
        "w_ada": g_w_ada, "b_ada": g_b_ada, "norm1_w": g_norm1_w, "w_in": g_w_in[None],
        "q_norm_w": g_q_norm_w, "k_norm_w": g_k_norm_w,
        "w_pool": g_w_pool.reshape(w_pool.shape), "pool_scale": g_pool_scale_w,
        "w_a_up": g_w_a_up[None], "w_b_up": g_w_b_up[None], "w_o": g_w_o[None],
        "norm2_w": g_norm2_w, "w_ff1": g_w_ff1[None], "w_ff2": g_w_ff2[None],
    }
    weights = {"w_ada": (w_ada, m_w_ada, v_w_ada), "b_ada": (b_ada, m_b_ada, v_b_ada),
               "norm1_w": (norm1_w, m_norm1_w, v_norm1_w), "w_in": (w_in, m_w_in, v_w_in),
               "q_norm_w": (q_norm_w, m_q_norm_w, v_q_norm_w), "k_norm_w": (k_norm_w, m_k_norm_w, v_k_norm_w),
               "w_pool": (w_pool, m_w_pool, v_w_pool), "pool_scale": (pool_scale, m_pool_scale, v_pool_scale),
               "w_a_up": (w_a_up, m_w_a_up, v_w_a_up), "w_b_up": (w_b_up, m_w_b_up, v_w_b_up),
               "w_o": (w_o, m_w_o, v_w_o), "norm2_w": (norm2_w, m_norm2_w, v_norm2_w),
               "w_ff1": (w_ff1, m_w_ff1, v_w_ff1), "w_ff2": (w_ff2, m_w_ff2, v_w_ff2)}
    order = list(weights)
    deltas, new_m, new_v = {}, {}, {}
    for name in order:
        wt, mt, vt = weights[name]
        shape = wt.shape
        flat = (-1, shape[-1])
        dl, nm, nv = _adamw("adamw_" + name, wt.reshape(flat), grads[name].reshape(flat),
                            mt.reshape(flat), vt.reshape(flat))
        deltas[name], new_m[name], new_v[name] = dl.reshape(shape), nm.reshape(shape), nv.reshape(shape)

    return (loss, grad_x[None], *[grads[n] for n in order], *[deltas[n] for n in order],
            *[new_m[n] for n in order], *[new_v[n] for n in order])
```

```python
import math

import jax
import jax.numpy as jnp
from jax import lax
from jax.experimental import pallas as pl
from jax.experimental.pallas import tpu as pltpu

F32 = jnp.float32
BF16 = jnp.bfloat16
MESH_AXES = ("x", "y", "c")
N_DEV = 8
HEAD_DIM = 128
POOL_WINDOWS = (2, 4, 8, 16)
N_MOD = 6
NORM_EPS = 1e-6
LANES = 128
SUBLANES = 8
VMEM_LIMIT_BYTES = 56 * 1024 * 1024
Q_TILE = 256
K_TILE = 256
POOL_TILE = 256

ADAM_LR = 0.001
ADAM_B1 = 0.9
ADAM_B2 = 0.999
ADAM_EPS = 1e-08
ADAM_WD = 0.01
ADAM_STEP = 10

_NN = (((1,), (0,)), ((), ()))
_NT = (((1,), (1,)), ((), ()))
_TN = (((0,), (0,)), ((), ()))
_DIMS = {"nn": _NN, "nt": _NT, "tn": _TN}


def _dot(a, b, mode="nn"):
    return lax.dot_general(a, b, _DIMS[mode], preferred_element_type=F32)


def _params(*sem):
    return pltpu.CompilerParams(dimension_semantics=sem, vmem_limit_bytes=VMEM_LIMIT_BYTES)


def _tile(dim, pref, align=SUBLANES):
    for t in range(min(dim, pref), 0, -1):
        if dim % t == 0 and t % align == 0:
            return t
    return dim


def _group_index(axes):
    idx = 0
    for a in axes:
        idx = idx * 2 + lax.axis_index(a)
    return idx


def _peer_device(axes, k):
    coords = {a: lax.axis_index(a) for a in MESH_AXES}
    for pos, a in enumerate(axes):
        if (k >> (len(axes) - 1 - pos)) & 1:
            coords[a] = 1 - coords[a]
    return tuple(coords[a] for a in MESH_AXES)


_AXIS_BIT = {"x": 4, "y": 2, "c": 1}
_ANY = pl.BlockSpec(memory_space=pl.ANY)


def _device_xor(mask):
    return tuple(1 - lax.axis_index(a) if mask & _AXIS_BIT[a] else lax.axis_index(a) for a in MESH_AXES)


def _remote(src, dst, send_sem, recv_sem, mask):
    return pltpu.make_async_remote_copy(src_ref=src, dst_ref=dst, send_sem=send_sem, recv_sem=recv_sem,
                                        device_id=_device_xor(mask), device_id_type=pl.DeviceIdType.MESH)


def _start_all(copies):
    for cp in copies:
        cp.start()


def _wait_all(copies):
    for cp in copies:
        cp.wait()


def _gather_ici(name, bufs):
    na = len(bufs)
    bx, by = _AXIS_BIT["x"], _AXIS_BIT["y"]

    def body(*refs):
        out = refs[na:2 * na]
        send_sems, recv_sems = refs[2 * na:]
        me = _group_index(MESH_AXES)

        def copy(a, n, slot, color, mask):
            half = out[a].shape[1] // 2
            piece = out[a].at[slot, pl.ds(color * half, half)]
            return _remote(piece, piece, send_sems.at[a, n], recv_sems.at[a, n], mask)

        first, second = [], []
        for a in range(na):
            first += [copy(a, 0, me, 0, bx), copy(a, 1, me, 1, by)]
            second += [copy(a, 2, me, 0, by), copy(a, 3, me ^ bx, 0, by),
                       copy(a, 4, me, 1, bx), copy(a, 5, me ^ by, 1, bx)]
        _start_all(first)
        _wait_all(first)
        _start_all(second)
        _wait_all(second)

    return pl.pallas_call(
        body, name=name, out_shape=tuple(jax.ShapeDtypeStruct(b.shape, b.dtype) for b in bufs),
        in_specs=[_ANY] * na, out_specs=tuple([_ANY] * na),
        input_output_aliases={a: a for a in range(na)},
        scratch_shapes=[pltpu.SemaphoreType.DMA((na, 6)), pltpu.SemaphoreType.DMA((na, 6))],
    )(*bufs)


def _gather_d2d(name, bufs):
    na = len(bufs)
    masks = (0, _AXIS_BIT["y"], _AXIS_BIT["x"], _AXIS_BIT["x"] | _AXIS_BIT["y"])

    def body(*refs):
        out = refs[na:2 * na]
        send_sems, recv_sems = refs[2 * na:]
        me = _group_index(MESH_AXES)
        copies = []
        for a in range(na):
            for n, m in enumerate(masks):
                piece = out[a].at[me ^ m]
                copies.append(_remote(piece, piece, send_sems.at[a, n], recv_sems.at[a, n], _AXIS_BIT["c"]))
        _start_all(copies)
        _wait_all(copies)

    return pl.pallas_call(
        body, name=name, out_shape=tuple(jax.ShapeDtypeStruct(b.shape, b.dtype) for b in bufs),
        in_specs=[_ANY] * na, out_specs=tuple([_ANY] * na),
        input_output_aliases={a: a for a in range(na)},
        scratch_shapes=[pltpu.SemaphoreType.DMA((na, 4)), pltpu.SemaphoreType.DMA((na, 4))],
    )(*bufs)


def _scatter(name, srcs, send_slots, masks):
    na = len(srcs)
    nm = len(masks)

    def body(*refs):
        src, out = refs[:na], refs[na:2 * na]
        send_sems, recv_sems = refs[2 * na:]
        slots = send_slots(_group_index(MESH_AXES))
        copies = []
        for n, m in enumerate(masks):
            for a in range(na):
                copies.append(_remote(src[a].at[slots[n]], out[a].at[n],
                                      send_sems.at[a, n], recv_sems.at[a, n], m))
        _start_all(copies)
        _wait_all(copies)

    return pl.pallas_call(
        body, name=name,
        out_shape=tuple(jax.ShapeDtypeStruct((nm,) + s.shape[1:], s.dtype) for s in srcs),
        in_specs=[_ANY] * na, out_specs=tuple([_ANY] * na),
        scratch_shapes=[pltpu.SemaphoreType.DMA((na, nm)), pltpu.SemaphoreType.DMA((na, nm))],
    )(*srcs)


def _add_received(name, own, own_slots, received, out_dtype):
    nj, r, c = received.shape
    tr = _tile(r, max(2 * SUBLANES, (1 << 20) // c), 2 * SUBLANES)

    def body(slots_ref, own_ref, rec_ref, o_ref):
        del slots_ref
        o_ref[...] = (own_ref[...].astype(F32) + rec_ref[...].astype(F32)).astype(o_ref.dtype)

    grid_spec = pltpu.PrefetchScalarGridSpec(
        num_scalar_prefetch=1, grid=(nj, r // tr),
        in_specs=[pl.BlockSpec((None, tr, c), lambda j, i, slots: (slots[j], i, 0)),
                  pl.BlockSpec((None, tr, c), lambda j, i, slots: (j, i, 0))],
        out_specs=pl.BlockSpec((None, tr, c), lambda j, i, slots: (j, i, 0)))
    return pl.pallas_call(
        body, name=name, grid_spec=grid_spec, out_shape=jax.ShapeDtypeStruct((nj, r, c), out_dtype),
        compiler_params=_params("parallel", "parallel"),
    )(own_slots, own, received)


def _add_final(name, own, received):
    nj, r, c = received.shape
    tr = _tile(r, max(2 * SUBLANES, (1 << 19) // c), 2 * SUBLANES)

    def body(own_ref, rec_ref, o_ref):
        acc = own_ref[...].astype(F32)
        for j in range(nj):
            acc = acc + rec_ref[j].astype(F32)
        o_ref[...] = acc

    return pl.pallas_call(
        body, name=name, grid=(r // tr,), out_shape=jax.ShapeDtypeStruct((r, c), F32),
        in_specs=[pl.BlockSpec((None, tr, c), lambda i: (0, i, 0)),
                  pl.BlockSpec((nj, tr, c), lambda i: (0, i, 0))],
        out_specs=pl.BlockSpec((tr, c), lambda i: (i, 0)),
        compiler_params=_params("parallel"),
    )(own, received)


_HBM = pl.BlockSpec(memory_space=pltpu.HBM)
_SEM = pl.BlockSpec(memory_space=pltpu.SEMAPHORE)
_DATAFLOW = pltpu.SideEffectType.DATAFLOW_SIDE_EFFECTING


def _launch(name, bufs, plan, n_copies):
    nb = len(bufs)

    def body(*refs):
        ins = refs[:nb]
        send_sems, recv_sems = refs[nb], refs[nb + 1]
        token = refs[-1]
        copies = plan(ins, _group_index(MESH_AXES))
        assert len(copies) == n_copies
        for n, (src, dst, mask) in enumerate(copies):
            _remote(src, dst, send_sems.at[n], recv_sems.at[n], mask).start()
        token[...] = jnp.zeros_like(token)

    outs = pl.pallas_call(
        body, name=name,
        out_shape=(pltpu.SemaphoreType.DMA((n_copies,)), pltpu.SemaphoreType.DMA((n_copies,)),
                   *[pltpu.HBM(b.shape, b.dtype) for b in bufs], jax.ShapeDtypeStruct((SUBLANES, LANES), F32)),
        in_specs=[_HBM] * nb,
        out_specs=(_SEM, _SEM, *[_HBM] * nb, pl.BlockSpec(memory_space=pltpu.VMEM)),
        input_output_aliases={i: 2 + i for i in range(nb)},
        compiler_params=pltpu.CompilerParams(has_side_effects=_DATAFLOW),
    )(*[pltpu.with_memory_space_constraint(b, pltpu.HBM) for b in bufs])
    return (name, plan, n_copies, outs[0], outs[1], list(outs[2:2 + nb])), outs[-1]


def _land(flight, after):
    name, plan, n_copies, send_sems, recv_sems, bufs = flight
    nb = len(bufs)

    def body(*refs):
        ins = refs[:nb]
        s_sems, r_sems = refs[nb], refs[nb + 1]
        for n, (src, dst, mask) in enumerate(plan(ins, _group_index(MESH_AXES))):
            cp = _remote(src, dst, s_sems.at[n], r_sems.at[n], mask)
            cp.wait_send()
            cp.wait_recv()

    outs = pl.pallas_call(
        body, name=name + "_land",
        out_shape=tuple(pltpu.HBM(b.shape, b.dtype) for b in bufs),
        in_specs=[_HBM] * nb + [_SEM, _SEM, _ANY], out_specs=tuple([_HBM] * nb),
        input_output_aliases={i: i for i in range(nb)},
        compiler_params=pltpu.CompilerParams(has_side_effects=_DATAFLOW),
    )(*bufs, send_sems, recv_sems, after)
    return list(outs)


def _plan_gather_ici(phase):
    bx, by = _AXIS_BIT["x"], _AXIS_BIT["y"]

    def plan(refs, me):
        copies = []
        for ref in refs:
            half = ref.shape[1] // 2

            def piece(slot, color, mask, ref=ref, half=half):
                p = ref.at[slot, pl.ds(color * half, half)]
                return (p, p, mask)

            if phase == 0:
                copies += [piece(me, 0, bx), piece(me, 1, by)]
            else:
                copies += [piece(me, 0, by), piece(me ^ bx, 0, by), piece(me, 1, bx), piece(me ^ by, 1, bx)]
        return copies

    return plan


def _plan_gather_d2d(refs, me):
    copies = []
    for ref in refs:
        for m in _CHIP_MASKS:
            copies.append((ref.at[me ^ m], ref.at[me ^ m], _AXIS_BIT["c"]))
    return copies


def _plan_scatter_d2d(refs, me):
    na = len(refs) // 2
    copies = []
    for a in range(na):
        for j, m in enumerate(_CHIP_MASKS):
            copies.append((refs[a].at[me ^ _AXIS_BIT["c"] ^ m], refs[na + a].at[j], _AXIS_BIT["c"]))
    return copies


def _plan_scatter_ici(refs, me):
    del me
    na = len(refs) // 2
    copies = []
    for a in range(na):
        for n, m in enumerate(_CHIP_MASKS[1:]):
            copies.append((refs[a].at[n + 1], refs[na + a].at[n], m))
    return copies


def _after(x, *tokens):
    return lax.optimization_barrier((x,) + tokens)[0]


def _reduce_scatter_start(tag, grads):
    lands = [lax.empty((len(_CHIP_MASKS),) + g.shape[1:], g.dtype) for g in grads]
    return _launch("rs%s_d2d" % tag, list(grads) + lands, _plan_scatter_d2d, len(_CHIP_MASKS) * len(grads))


def _reduce_scatter_middle(tag, flight, after, me):
    bufs = _land(flight, after)
    na = len(bufs) // 2
    own_slots = jnp.stack([me ^ m for m in _CHIP_MASKS]).astype(jnp.int32)
    sums = [_add_received("rs%s_add_d2d_%d" % (tag, a), bufs[a], own_slots, bufs[na + a], BF16) for a in range(na)]
    lands = [lax.empty((len(_CHIP_MASKS) - 1,) + h.shape[1:], h.dtype) for h in sums]
    return _launch("rs%s_ici" % tag, sums + lands, _plan_scatter_ici, (len(_CHIP_MASKS) - 1) * na)


def _reduce_scatter_finish(tag, flight, after):
    bufs = _land(flight, after)
    na = len(bufs) // 2
    return [_add_final("rs%s_add_ici_%d" % (tag, a), bufs[a], bufs[na + a]) for a in range(na)]


def _all_gather_2d(name, x):
    r, c = x.shape

    def body(x_ref, out_ref, send_sems, recv_sems):
        me = _group_index(MESH_AXES)
        out_ref[me] = x_ref[...]
        copies = []
        for k in range(1, N_DEV):
            cp = pltpu.make_async_remote_copy(
                src_ref=x_ref, dst_ref=out_ref.at[me],
                send_sem=send_sems.at[k - 1], recv_sem=recv_sems.at[k - 1],
                device_id=_peer_device(MESH_AXES, k), device_id_type=pl.DeviceIdType.MESH)
            cp.start()
            copies.append(cp)
        for cp in copies:
            cp.wait()

    vmem = pl.BlockSpec(memory_space=pltpu.VMEM)
    return pl.pallas_call(
        body, name=name, out_shape=jax.ShapeDtypeStruct((N_DEV, r, c), x.dtype),
        in_specs=[vmem], out_specs=vmem,
        scratch_shapes=[pltpu.SemaphoreType.DMA((N_DEV - 1,)), pltpu.SemaphoreType.DMA((N_DEV - 1,))],
    )(x)


def _sum_slots(name, buf, out_dtype):
    pre, n, r, c = buf.shape
    tr = _tile(r, max(SUBLANES * 2, (1 << 20) // c))

    def body(b_ref, o_ref):
        acc = b_ref[0].astype(F32)
        for q in range(1, n):
            acc = acc + b_ref[q].astype(F32)
        o_ref[...] = acc.astype(o_ref.dtype)

    return pl.pallas_call(
        body, name=name, grid=(pre, r // tr),
        out_shape=jax.ShapeDtypeStruct((pre, r, c), out_dtype),
        in_specs=[pl.BlockSpec((None, n, tr, c), lambda i, j: (i, 0, j, 0))],
        out_specs=pl.BlockSpec((None, tr, c), lambda i, j: (i, j, 0)),
        compiler_params=_params("parallel", "parallel"),
    )(buf)


def _all_gather_weights(bufs):
    return _gather_d2d("ag_d2d", list(_gather_ici("ag_ici", bufs)))


_CHIP_MASKS = (0, _AXIS_BIT["y"], _AXIS_BIT["x"], _AXIS_BIT["x"] | _AXIS_BIT["y"])


def _reduce_scatter_grads(bufs):
    me = _group_index(MESH_AXES)
    bc = _AXIS_BIT["c"]
    r1 = _scatter("rs_d2d", bufs, lambda i: [i ^ bc ^ m for m in _CHIP_MASKS], (bc,) * len(_CHIP_MASKS))
    own_slots = jnp.stack([me ^ m for m in _CHIP_MASKS]).astype(jnp.int32)
    half = [_add_received("rs_add_d2d_%d" % a, b, own_slots, r, BF16) for a, (b, r) in enumerate(zip(bufs, r1))]
    r2 = _scatter("rs_ici", half, lambda i: [1, 2, 3], _CHIP_MASKS[1:])
    return [_add_final("rs_add_ici_%d" % a, h, r) for a, (h, r) in enumerate(zip(half, r2))]


def _matmul(name, mode, grid, a, a_spec, b, b_spec, out_shapes, out_specs, acc_shape,
            epilogue=None, extras=(), extra_specs=(), aliases=None):
    nk = grid[2]
    n_extra = len(extras)
    n_out = len(out_shapes)

    def finish(acc, extra_refs, out_refs):
        if epilogue is None:
            out_refs[0][...] = acc.astype(out_refs[0].dtype)
        else:
            epilogue(acc, extra_refs, out_refs)

    def body(*refs):
        a_ref, b_ref = refs[0], refs[1]
        extra_refs = refs[2:2 + n_extra]
        out_refs = refs[2 + n_extra:2 + n_extra + n_out]
        if nk == 1:
            finish(_dot(a_ref[...], b_ref[...], mode), extra_refs, out_refs)
            return
        acc_ref = refs[-1]
        k = pl.program_id(2)

        @pl.when(k == 0)
        def _():
            acc_ref[...] = jnp.zeros_like(acc_ref)

        acc_ref[...] += _dot(a_ref[...], b_ref[...], mode)

        @pl.when(k == nk - 1)
        def _():
            finish(acc_ref[...], extra_refs, out_refs)

    scratch = [] if nk == 1 else [pltpu.VMEM(acc_shape, F32)]
    return pl.pallas_call(
        body, name=name, grid=grid, out_shape=tuple(out_shapes),
        in_specs=[a_spec, b_spec] + list(extra_specs), out_specs=tuple(out_specs),
        scratch_shapes=scratch, input_output_aliases=aliases or {},
        compiler_params=_params("parallel", "parallel", "arbitrary"),
    )(a, b, *extras)


def _sds(shape, dtype):
    return jax.ShapeDtypeStruct(tuple(shape), dtype)


def _ada_forward(c_all, w_ada, b_shard):
    nb, d = c_all.shape
    w = w_ada.shape[1]
    tn = _tile(w, 512)

    def body(c_ref, w_ref, b_ref, o_ref):
        cv = c_ref[...]
        sc = cv * jax.nn.sigmoid(cv)
        o_ref[...] = jnp.dot(sc, w_ref[...], precision=lax.Precision.HIGHEST,
                             preferred_element_type=F32) + b_ref[...]

    return pl.pallas_call(
        body, name="ada_fwd", grid=(w // tn,), out_shape=_sds((nb, w), F32),
        in_specs=[pl.BlockSpec((nb, d), lambda j: (0, 0)), pl.BlockSpec((d, tn), lambda j: (0, j)),
                  pl.BlockSpec((1, tn), lambda j: (0, j))],
        out_specs=pl.BlockSpec((nb, tn), lambda j: (0, j)),
        compiler_params=_params("parallel"),
    )(c_all, w_ada, b_shard)


def _ada_weight_grad(c_all, dmod_cols):
    nb, d = c_all.shape
    w = dmod_cols.shape[1]
    tn = _tile(w, 512)

    def body(c_ref, g_ref, o_ref):
        cv = c_ref[...]
        sc = cv * jax.nn.sigmoid(cv)
        o_ref[...] = lax.dot_general(sc, g_ref[...], _TN, precision=lax.Precision.HIGHEST,
                                     preferred_element_type=F32)

    return pl.pallas_call(
        body, name="ada_wgrad", grid=(w // tn,), out_shape=_sds((d, w), F32),
        in_specs=[pl.BlockSpec((nb, d), lambda j: (0, 0)), pl.BlockSpec((nb, tn), lambda j: (0, j))],
        out_specs=pl.BlockSpec((d, tn), lambda j: (0, j)),
        compiler_params=_params("parallel"),
    )(c_all, dmod_cols)


def _norm_forward(name, x, norm_w, scale, shift):
    s, d = x.shape
    tm = _tile(s, 256)

    def body(x_ref, w_ref, sc_ref, sh_ref, h_ref):
        xv = x_ref[...]
        r = lax.rsqrt(jnp.mean(xv * xv, axis=-1, keepdims=True) + NORM_EPS)
        h = (xv * r * w_ref[...]) * (1.0 + sc_ref[...]) + sh_ref[...]
        h_ref[...] = h.astype(BF16)

    vec = pl.BlockSpec((1, d), lambda i: (0, 0))
    row = pl.BlockSpec((tm, d), lambda i: (i, 0))
    return pl.pallas_call(
        body, name=name, grid=(s // tm,), out_shape=_sds((s, d), BF16),
        in_specs=[row, vec, vec, vec], out_specs=row, compiler_params=_params("parallel"),
    )(x, norm_w, scale, shift)


def _norm_backward(name, dh, x, norm_w, scale, dres):
    s, d = x.shape
    tm = _tile(s, 256)

    def body(dh_ref, x_ref, w_ref, sc_ref, dres_ref, dx_ref, dshift_ref, dscale_ref, dw_ref):
        @pl.when(pl.program_id(0) == 0)
        def _():
            dshift_ref[...] = jnp.zeros_like(dshift_ref)
            dscale_ref[...] = jnp.zeros_like(dscale_ref)
            dw_ref[...] = jnp.zeros_like(dw_ref)

        xv = x_ref[...]
        g = dh_ref[...]
        r = lax.rsqrt(jnp.mean(xv * xv, axis=-1, keepdims=True) + NORM_EPS)
        n = xv * r
        gain = 1.0 + sc_ref[...]
        gn = g * n
        dshift_ref[...] += jnp.sum(g, axis=0, keepdims=True)
        dscale_ref[...] += jnp.sum(gn, axis=0, keepdims=True) * w_ref[...]
        dw_ref[...] += jnp.sum(gn, axis=0, keepdims=True) * gain
        dn = g * (w_ref[...] * gain)
        dx_ref[...] = dres_ref[...] + r * (dn - n * jnp.mean(dn * n, axis=-1, keepdims=True))

    vec = pl.BlockSpec((1, d), lambda i: (0, 0))
    row = pl.BlockSpec((tm, d), lambda i: (i, 0))
    return pl.pallas_call(
        body, name=name, grid=(s // tm,),
        out_shape=(_sds((s, d), F32), _sds((1, d), F32), _sds((1, d), F32), _sds((1, d), F32)),
        in_specs=[row, row, vec, vec, row], out_specs=(row, vec, vec, vec),
        compiler_params=_params("arbitrary"),
    )(dh, x, norm_w, scale, dres)


def _gate_backward(name, d, gate, other):
    s, w = d.shape
    tm = _tile(s, 256)

    def body(d_ref, g_ref, o_ref, dg_ref, dgate_ref):
        @pl.when(pl.program_id(0) == 0)
        def _():
            dgate_ref[...] = jnp.zeros_like(dgate_ref)

        dv = d_ref[...]
        dg_ref[...] = (dv * g_ref[...]).astype(BF16)
        dgate_ref[...] += jnp.sum(dv * o_ref[...].astype(F32), axis=0, keepdims=True)

    vec = pl.BlockSpec((1, w), lambda i: (0, 0))
    row = pl.BlockSpec((tm, w), lambda i: (i, 0))
    return pl.pallas_call(
        body, name=name, grid=(s // tm,), out_shape=(_sds((s, w), BF16), _sds((1, w), F32)),
        in_specs=[row, vec, row], out_specs=(row, vec), compiler_params=_params("arbitrary"),
    )(d, gate, other)


def _split_bf16(v):
    hi = v.astype(BF16)
    lo = (v - hi.astype(F32)).astype(BF16)
    return hi, lo


def _pool_forward(proj, w_pool, pool_scale):
    s = proj.shape[0]
    g_n, cg, _ = w_pool.shape
    t = POOL_TILE
    nt = s // t

    def body(cur_ref, prev_ref, wp_ref, sc_ref, pooled_ref, ya_ref):
        g = pl.program_id(0)
        ti = pl.program_id(1)
        win = jnp.left_shift(2, g)
        row = lax.broadcasted_iota(jnp.int32, (t, t), 0)
        col = lax.broadcasted_iota(jnp.int32, (t, t), 1)
        lag = row - col
        band_cur = ((lag >= 0) & (lag < win)).astype(BF16)
        band_prev = ((lag + t < win) & (ti > 0)).astype(BF16)
        u = cur_ref[...]
        u_hi, u_lo = _split_bf16(u)
        p_hi, p_lo = _split_bf16(prev_ref[...])
        wsum = (_dot(band_cur, u_hi) + _dot(band_cur, u_lo)
                + _dot(band_prev, p_hi) + _dot(band_prev, p_lo))
        tok = ti * t + lax.broadcasted_iota(jnp.int32, (t, 1), 0)
        count = jnp.minimum(tok + 1, win).astype(F32)
        pooled = (wsum / count - u).astype(BF16)
        pooled_ref[...] = pooled
        ya_ref[...] = (_dot(pooled, wp_ref[...]) * sc_ref[...]).astype(BF16)

    blk = pl.BlockSpec((t, cg), lambda g, i: (i, g))
    return pl.pallas_call(
        body, name="pool_fwd", grid=(g_n, nt),
        out_shape=(_sds((s, g_n * cg), BF16), _sds((s, g_n * cg), BF16)),
        in_specs=[blk, pl.BlockSpec((t, cg), lambda g, i: (jnp.maximum(i - 1, 0), g)),
                  pl.BlockSpec((None, cg, cg), lambda g, i: (g, 0, 0)),
                  pl.BlockSpec((1, cg), lambda g, i: (0, g))],
        out_specs=(blk, blk), compiler_params=_params("parallel", "parallel"),
    )(proj, proj, w_pool, pool_scale)


def _pool_backward(dya, pooled, w_pool, pool_scale, dproj):
    s = dya.shape[0]
    g_n, cg, _ = w_pool.shape
    t = POOL_TILE
    nt = s // t

    def body(dya_ref, dya_next_ref, pooled_ref, wp_ref, sc_ref, dproj_in, du_ref, gw_ref, gs_ref):
        del dproj_in
        g = pl.program_id(0)
        ti = pl.program_id(1)

        @pl.when(ti == 0)
        def _():
            gw_ref[...] = jnp.zeros_like(gw_ref)
            gs_ref[...] = jnp.zeros_like(gs_ref)

        win = jnp.left_shift(2, g)
        wp = wp_ref[...]
        sc = sc_ref[...]
        pooled_v = pooled_ref[...]
        dya_v = dya_ref[...].astype(F32)
        mixed = _dot(pooled_v, wp)
        gs_ref[...] += jnp.sum(dya_v * mixed, axis=0, keepdims=True)
        dmixed = (dya_v * sc).astype(BF16)
        gw_ref[...] += _dot(pooled_v, dmixed, "tn")
        dpooled = _dot(dmixed, wp, "nt")
        dmixed_next = (dya_next_ref[...].astype(F32) * sc).astype(BF16)
        dpooled_next = _dot(dmixed_next, wp, "nt")
        tok = ti * t + lax.broadcasted_iota(jnp.int32, (t, 1), 0)
        e_cur = dpooled / jnp.minimum(tok + 1, win).astype(F32)
        e_next = dpooled_next / jnp.minimum(tok + t + 1, win).astype(F32)
        row = lax.broadcasted_iota(jnp.int32, (t, t), 0)
        col = lax.broadcasted_iota(jnp.int32, (t, t), 1)
        lead = col - row
        band_cur = ((lead >= 0) & (lead < win)).astype(BF16)
        band_next = ((lead + t < win) & (ti < nt - 1)).astype(BF16)
        c_hi, c_lo = _split_bf16(e_cur)
        n_hi, n_lo = _split_bf16(e_next)
        du = (_dot(band_cur, c_hi) + _dot(band_cur, c_lo)
              + _dot(band_next, n_hi) + _dot(band_next, n_lo)) - dpooled
        du_ref[...] = du.astype(BF16)

    blk = pl.BlockSpec((t, cg), lambda g, i: (i, g))
    du, gw, gs = pl.pallas_call(
        body, name="pool_bwd", grid=(g_n, nt),
        out_shape=(_sds(dproj.shape, BF16), _sds((g_n, cg, cg), F32), _sds((1, g_n * cg), F32)),
        in_specs=[blk, pl.BlockSpec((t, cg), lambda g, i: (jnp.minimum(i + 1, nt - 1), g)), blk,
                  pl.BlockSpec((None, cg, cg), lambda g, i: (g, 0, 0)),
                  pl.BlockSpec((1, cg), lambda g, i: (0, g)),
                  pl.BlockSpec(memory_space=pl.ANY)],
        out_specs=(blk, pl.BlockSpec((None, cg, cg), lambda g, i: (g, 0, 0)),
                   pl.BlockSpec((1, cg), lambda g, i: (0, g))),
        input_output_aliases={5: 0}, compiler_params=_params("parallel", "arbitrary"),
    )(dya, dya, pooled, w_pool, pool_scale, dproj)
    return du, gw, gs


def _qkv_prepare(proj, q_norm_w, k_norm_w, width):
    s = proj.shape[0]
    tm = _tile(s, 256)
    heads = width // HEAD_DIM

    def body(q_ref, k_ref, v_ref, qw_ref, kw_ref, qn_ref, kn_ref, vb_ref):
        for h in range(heads):
            cols = slice(h * HEAD_DIM, (h + 1) * HEAD_DIM)
            for src, w_ref, dst in ((q_ref, qw_ref, qn_ref), (k_ref, kw_ref, kn_ref)):
                v = src[:, cols]
                r = lax.rsqrt(jnp.mean(v * v, axis=-1, keepdims=True) + NORM_EPS)
                dst[:, cols] = (v * r * w_ref[...]).astype(BF16)
        vb_ref[...] = v_ref[...].astype(BF16)

    vec = pl.BlockSpec((1, HEAD_DIM), lambda i: (0, 0))
    out_spec = pl.BlockSpec((tm, width), lambda i: (i, 0))
    return pl.pallas_call(
        body, name="qkv_prep", grid=(s // tm,),
        out_shape=(_sds((s, width), BF16),) * 3,
        in_specs=[pl.BlockSpec((tm, width), lambda i: (i, 1)), pl.BlockSpec((tm, width), lambda i: (i, 2)),
                  pl.BlockSpec((tm, width), lambda i: (i, 3)), vec, vec],
        out_specs=(out_spec,) * 3, compiler_params=_params("parallel"),
    )(proj, proj, proj, q_norm_w, k_norm_w)


def _qk_norm_backward(name, dn, proj, col_block, norm_w, dproj, width):
    s = proj.shape[0]
    tm = _tile(s, 256)
    heads = width // HEAD_DIM

    def body(dn_ref, q_ref, w_ref, dproj_in, dq_ref, gw_ref):
        del dproj_in

        @pl.when(pl.program_id(0) == 0)
        def _():
            gw_ref[...] = jnp.zeros_like(gw_ref)

        wv = w_ref[...]
        gw = jnp.zeros((1, HEAD_DIM), F32)
        for h in range(heads):
            cols = slice(h * HEAD_DIM, (h + 1) * HEAD_DIM)
            v = q_ref[:, cols]
            g = dn_ref[:, cols]
            r = lax.rsqrt(jnp.mean(v * v, axis=-1, keepdims=True) + NORM_EPS)
            n = v * r
            gw = gw + jnp.sum(g * n, axis=0, keepdims=True)
            gn = g * wv
            dq_ref[:, cols] = (r * (gn - n * jnp.mean(gn * n, axis=-1, keepdims=True))).astype(BF16)
        gw_ref[...] += gw

    blk = pl.BlockSpec((tm, width), lambda i: (i, col_block))
    return pl.pallas_call(
        body, name=name, grid=(s // tm,),
        out_shape=(_sds(dproj.shape, BF16), _sds((1, HEAD_DIM), F32)),
        in_specs=[pl.BlockSpec((tm, width), lambda i: (i, 0)), blk,
                  pl.BlockSpec((1, HEAD_DIM), lambda i: (0, 0)), pl.BlockSpec(memory_space=pl.ANY)],
        out_specs=(blk, pl.BlockSpec((1, HEAD_DIM), lambda i: (0, 0))),
        input_output_aliases={3: 0}, compiler_params=_params("arbitrary"),
    )(dn, proj, norm_w, dproj)


def _strict_upper(n):
    row = lax.broadcasted_iota(jnp.int32, (n, n), 0)
    col = lax.broadcasted_iota(jnp.int32, (n, n), 1)
    return (row > col).astype(BF16)


def _strict_lower(n):
    row = lax.broadcasted_iota(jnp.int32, (n, n), 0)
    col = lax.broadcasted_iota(jnp.int32, (n, n), 1)
    return (row < col).astype(BF16)


def _cumulate(v, tri):
    hi, lo = _split_bf16(v)
    return _dot(hi, tri) + _dot(lo, tri)


def _log_sigmoid(z):
    return jnp.minimum(z, 0.0) - jnp.log(1.0 + jnp.exp(-jnp.abs(z)))


def _attention_forward(qn, kn, vb):
    s, width = qn.shape
    heads = width // HEAD_DIM
    tq, tk = Q_TILE, K_TILE
    assert tq == tk and s % tq == 0
    scale = 1.0 / math.sqrt(HEAD_DIM)

    def body(q_ref, k_ref, v_ref, o_ref):
        qi = pl.program_id(1)
        q = q_ref[...]
        upper = _strict_upper(tk)
        causal = lax.broadcasted_iota(jnp.int32, (tq, tk), 1) < lax.broadcasted_iota(jnp.int32, (tq, tk), 0)

        def chunk(kb, carry, masked):
            acc, later = carry
            rows = pl.ds(pl.multiple_of(kb * tk, tk), tk)
            z = _dot(q, k_ref[rows, :], "nt") * scale
            log_beta = _log_sigmoid(z)
            l = log_beta - z
            if masked:
                l = jnp.where(causal, l, 0.0)
            w = log_beta + _cumulate(l, upper) + later
            a = jnp.exp(w)
            if masked:
                a = jnp.where(causal, a, 0.0)
            acc = acc + _dot(a.astype(BF16), v_ref[rows, :])
            return acc, later + jnp.sum(l, axis=1, keepdims=True)

        carry = (jnp.zeros((tq, HEAD_DIM), F32), jnp.zeros((tq, 1), F32))
        carry = chunk(qi, carry, True)
        acc, _ = lax.fori_loop(0, qi, lambda i, c: chunk(qi - 1 - i, c, False), carry)
        o_ref[...] = acc.astype(BF16)

    full = pl.BlockSpec((s, HEAD_DIM), lambda h, i: (0, h))
    blk = pl.BlockSpec((tq, HEAD_DIM), lambda h, i: (i, h))
    return pl.pallas_call(
        body, name="attn_fwd", grid=(heads, s // tq), out_shape=_sds((s, width), BF16),
        in_specs=[blk, full, full], out_specs=blk, compiler_params=_params("parallel", "parallel"),
    )(qn, kn, vb)


def _attention_backward(qn, kn, vb, dout, dproj, v_col_block):
    s, width = qn.shape
    heads = width // HEAD_DIM
    tq, tk = Q_TILE, K_TILE
    nq = s // tq
    scale = 1.0 / math.sqrt(HEAD_DIM)
    v_blocks_per_head = v_col_block * heads

    def body(q_ref, k_ref, v_ref, do_ref, dproj_in, dq_ref, dk_ref, dv_ref, a_scr, dk_scr, dv_scr):
        del dproj_in
        qi = pl.program_id(1)

        @pl.when(qi == 0)
        def _():
            dk_scr[...] = jnp.zeros_like(dk_scr)
            dv_scr[...] = jnp.zeros_like(dv_scr)

        q = q_ref[...]
        dout_v = do_ref[...]
        upper = _strict_upper(tk)
        lower = _strict_lower(tk)
        causal = lax.broadcasted_iota(jnp.int32, (tq, tk), 1) < lax.broadcasted_iota(jnp.int32, (tq, tk), 0)

        def weights(kb, later, masked):
            rows = pl.ds(pl.multiple_of(kb * tk, tk), tk)
            z = _dot(q, k_ref[rows, :], "nt") * scale
            log_beta = _log_sigmoid(z)
            l = log_beta - z
            if masked:
                l = jnp.where(causal, l, 0.0)
            a = jnp.exp(log_beta + _cumulate(l, upper) + later)
            if masked:
                a = jnp.where(causal, a, 0.0)
            a_scr[:, rows] = a
            return later + jnp.sum(l, axis=1, keepdims=True)

        later = weights(qi, jnp.zeros((tq, 1), F32), True)
        lax.fori_loop(0, qi, lambda i, c: weights(qi - 1 - i, c, False), later)

        def grads(kb, carry, masked):
            dq, before = carry
            rows = pl.ds(pl.multiple_of(kb * tk, tk), tk)
            k_blk = k_ref[rows, :]
            z = _dot(q, k_blk, "nt") * scale
            beta = jax.nn.sigmoid(z)
            a = a_scr[:, rows]
            g = a * _dot(dout_v, v_ref[rows, :], "nt")
            p = _cumulate(g, lower) + before
            dz = g * (1.0 - beta) - p * beta
            if masked:
                dz = jnp.where(causal, dz, 0.0)
            dz = (dz * scale).astype(BF16)
            dk_scr[rows, :] += _dot(dz, q, "tn")
            dv_scr[rows, :] += _dot(a.astype(BF16), dout_v, "tn")
            return dq + _dot(dz, k_blk), before + jnp.sum(g, axis=1, keepdims=True)

        carry = (jnp.zeros((tq, HEAD_DIM), F32), jnp.zeros((tq, 1), F32))
        carry = lax.fori_loop(0, qi, lambda i, c: grads(i, c, False), carry)
        dq, _ = grads(qi, carry, True)
        dq_ref[...] = dq

        @pl.when(qi == nq - 1)
        def _():
            dk_ref[...] = dk_scr[...]
            dv_ref[...] = dv_scr[...].astype(BF16)

    full = pl.BlockSpec((s, HEAD_DIM), lambda h, i: (0, h))
    blk = pl.BlockSpec((tq, HEAD_DIM), lambda h, i: (i, h))
    return pl.pallas_call(
        body, name="attn_bwd", grid=(heads, nq),
        out_shape=(_sds((s, width), F32), _sds((s, width), F32), _sds(dproj.shape, BF16)),
        in_specs=[blk, full, full, blk, pl.BlockSpec(memory_space=pl.ANY)],
        out_specs=(blk, full, pl.BlockSpec((s, HEAD_DIM), lambda h, i: (0, v_blocks_per_head + h))),
        scratch_shapes=[pltpu.VMEM((tq, s), F32), pltpu.VMEM((s, HEAD_DIM), F32), pltpu.VMEM((s, HEAD_DIM), F32)],
        input_output_aliases={4: 2}, compiler_params=_params("parallel", "arbitrary"),
    )(qn, kn, vb, dout, dproj)


def _place_columns(name, src, dst, col_block):
    s, w = src.shape
    tm = _tile(s, 512)

    def body(src_ref, dst_in, out_ref):
        del dst_in
        out_ref[...] = src_ref[...]

    return pl.pallas_call(
        body, name=name, grid=(s // tm,), out_shape=_sds(dst.shape, dst.dtype),
        in_specs=[pl.BlockSpec((tm, w), lambda i: (i, 0)), pl.BlockSpec(memory_space=pl.ANY)],
        out_specs=pl.BlockSpec((tm, w), lambda i: (i, col_block)),
        input_output_aliases={1: 0}, compiler_params=_params("parallel"),
    )(src, dst)


def _cast_into_slot(name, x, slot):
    r, c = x.shape
    tr = _tile(r, max(SUBLANES * 2, (1 << 20) // c), SUBLANES * 2)

    def body(slot_ref, x_ref, o_ref):
        del slot_ref
        o_ref[...] = x_ref[...].astype(BF16)

    grid_spec = pltpu.PrefetchScalarGridSpec(
        num_scalar_prefetch=1, grid=(r // tr,),
        in_specs=[pl.BlockSpec((tr, c), lambda i, slot_ref: (i, 0))],
        out_specs=pl.BlockSpec((None, tr, c), lambda i, slot_ref: (slot_ref[0], i, 0)))
    return pl.pallas_call(
        body, name=name, grid_spec=grid_spec, out_shape=_sds((N_DEV, r, c), BF16),
        compiler_params=_params("parallel"),
    )(slot, x)


def _adamw(name, w, g, m, v):
    r, c = w.shape
    tr = _tile(r, max(SUBLANES, (1 << 19) // c))
    c1 = 1.0 - ADAM_B1 ** ADAM_STEP
    c2 = 1.0 - ADAM_B2 ** ADAM_STEP

    def body(w_ref, g_ref, m_ref, v_ref, d_ref, nm_ref, nv_ref):
        gv = g_ref[...]
        nm = ADAM_B1 * m_ref[...] + (1.0 - ADAM_B1) * gv
        nv = ADAM_B2 * v_ref[...] + (1.0 - ADAM_B2) * (gv * gv)
        d_ref[...] = -ADAM_LR * ((nm / c1) / (jnp.sqrt(nv / c2) + ADAM_EPS) + ADAM_WD * w_ref[...])
        nm_ref[...] = nm
        nv_ref[...] = nv

    blk = pl.BlockSpec((tr, c), lambda i: (i, 0))
    return pl.pallas_call(
        body, name=name, grid=(r // tr,), out_shape=(_sds((r, c), F32),) * 3,
        in_specs=[blk] * 4, out_specs=(blk,) * 3, compiler_params=_params("parallel"),
    )(w, g, m, v)


def _rows_of_lanes(v):
    rows = v.shape[1] // LANES
    out = v.reshape(rows, LANES)
    pad = (-rows) % SUBLANES
    if pad:
        out = jnp.pad(out, ((0, pad), (0, 0)))
    return out


def kernel(x, c, w_ada, b_ada, norm1_w, w_in, q_norm_w, k_norm_w, w_pool, pool_scale, w_a_up, w_b_up, w_o, norm2_w, w_ff1, w_ff2, loss_target, m_w_ada, m_b_ada, m_norm1_w, m_w_in, m_q_norm_w, m_k_norm_w, m_w_pool, m_pool_scale, m_w_a_up, m_w_b_up, m_w_o, m_norm2_w, m_w_ff1, m_w_ff2, v_w_ada, v_b_ada, v_norm1_w, v_w_in, v_q_norm_w, v_k_norm_w, v_w_pool, v_pool_scale, v_w_a_up, v_w_b_up, v_w_o, v_norm2_w, v_w_ff1, v_w_ff2):
    _, s, d = x.shape
    half = d // 2
    d8 = d // N_DEV
    n_groups = len(POOL_WINDOWS)
    cg = half // n_groups
    me = _group_index(MESH_AXES)

    x2 = x[0]
    target = loss_target[0]

    my_slot = jnp.reshape(me, (1,)).astype(jnp.int32)

    def cast(i, t):
        return _cast_into_slot("cast_w%d" % i, t, my_slot)

    def gather_start(tag, bufs, phase):
        if phase < 2:
            return _launch("ag%s_ici%d" % (tag, phase), bufs, _plan_gather_ici(phase), (2, 4)[phase] * len(bufs))
        return _launch("ag%s_d2d" % tag, bufs, _plan_gather_d2d, len(_CHIP_MASKS) * len(bufs))

    buf_a = [cast(0, w_in[0])]
    fl_a, tok = gather_start("A", buf_a, 0)

    c_all = _all_gather_2d("ag_c", _after(c.reshape(d // LANES, LANES), tok)).reshape(N_DEV, d)
    wa = w_ada.shape[2]
    b_shard = lax.dynamic_slice_in_dim(b_ada, me * wa, wa, axis=1)
    mod_part = _ada_forward(c_all, w_ada[0], b_shard)
    mod_all = _all_gather_2d("ag_mod", mod_part.reshape(N_DEV * wa // LANES, LANES))
    mod_all = mod_all.reshape(N_DEV, N_DEV, wa)
    mod = lax.dynamic_slice_in_dim(mod_all, me, 1, axis=1).reshape(1, N_MOD * d)
    shift1, scale1, gate1, shift2, scale2, gate2 = [mod[:, i * d:(i + 1) * d] for i in range(N_MOD)]
    buf_b = [cast(1, w_pool[0].reshape(-1, cg)), cast(2, w_a_up[0]), cast(3, w_b_up[0]), cast(4, w_o[0])]
    buf_c = [cast(5, w_ff1[0]), cast(6, w_ff2[0])]

    buf_a = _land(fl_a, mod)
    fl_a, tok = gather_start("A", buf_a, 1)
    h = _norm_forward("norm1_fwd", _after(x2, tok), norm1_w, scale1, shift1)
    buf_a = _land(fl_a, h)
    fl_a, tok = gather_start("A", buf_a, 2)
    w_in_f, = _land(fl_a, tok)

    tm = _tile(s, 512)
    tk = _tile(d, 1024)

    fl_b, tok_b = gather_start("B", buf_b, 0)
    fl_c, tok_c = gather_start("C", buf_c, 0)
    proj = _matmul(
        "proj", "nn", (s // tm, N_DEV, d // tk), _after(h, tok_b, tok_c), pl.BlockSpec((tm, tk), lambda i, j, k: (i, k)),
        w_in_f, pl.BlockSpec((None, tk, half), lambda i, j, k: (j, k, 0)),
        [_sds((s, 4 * d), F32)], [pl.BlockSpec((tm, half), lambda i, j, k: (i, j))], (tm, half))[0]
    buf_b = _land(fl_b, proj)
    buf_c = _land(fl_c, proj)
    fl_b, tok_b = gather_start("B", buf_b, 1)
    fl_c, tok_c = gather_start("C", buf_c, 1)

    qn, kn, vb = _qkv_prepare(_after(proj, tok_b, tok_c), q_norm_w, k_norm_w, half)
    attn = _attention_forward(qn, kn, vb)
    buf_b = _land(fl_b, attn)
    buf_c = _land(fl_c, attn)
    fl_b, tok_b = gather_start("B", buf_b, 2)
    fl_c, tok_c = gather_start("C", buf_c, 2)
    w_pool_f, w_a_f, w_b_f, w_o_f = _land(fl_b, tok_b)
    rows_pool = cg // N_DEV
    w_pool_f = w_pool_f.reshape(N_DEV, n_groups, rows_pool, cg).transpose(1, 0, 2, 3).reshape(n_groups, cg, cg)
    w_o_f = w_o_f.reshape(d, d)
    pooled, ya_in = _pool_forward(_after(proj, tok_c), w_pool_f, pool_scale)

    def merge_epilogue(ga_ref, gb_ref, ya, yb, out_refs):
        merged_ref, ya_ref, yb_ref = out_refs
        merged = jax.nn.sigmoid(ga_ref[...]) * ya + jax.nn.sigmoid(gb_ref[...]) * yb
        merged_ref[...] = merged.astype(BF16)
        ya_ref[...] = ya.astype(BF16)
        yb_ref[...] = yb.astype(BF16)

    def up_body(a1_ref, b1_ref, a2_ref, b2_ref, ga_ref, gb_ref, *out_refs):
        merge_epilogue(ga_ref, gb_ref, _dot(a1_ref[...], b1_ref[...]), _dot(a2_ref[...], b2_ref[...]), out_refs)

    ga_blk0 = 2 * d // d8
    gb_blk0 = 3 * d // d8
    a_spec = pl.BlockSpec((tm, half), lambda i, j: (i, 0))
    wup_spec = pl.BlockSpec((None, half, d8), lambda i, j: (j, 0, 0))
    o_blk = pl.BlockSpec((tm, d8), lambda i, j: (i, j))
    merged, y_a, y_b = pl.pallas_call(
        up_body, name="up_merge", grid=(s // tm, N_DEV), out_shape=(_sds((s, d), BF16),) * 3,
        in_specs=[a_spec, wup_spec, a_spec, wup_spec,
                  pl.BlockSpec((tm, d8), lambda i, j: (i, ga_blk0 + j)),
                  pl.BlockSpec((tm, d8), lambda i, j: (i, gb_blk0 + j))],
        out_specs=(o_blk,) * 3, compiler_params=_params("parallel", "parallel"),
    )(ya_in, w_a_f, attn, w_b_f, proj, proj)

    tn = _tile(d, 1024)

    def oproj_epilogue(acc, extra_refs, out_refs):
        x_ref, g_ref = extra_refs
        x1_ref, o_ref = out_refs
        x1_ref[...] = x_ref[...] + g_ref[...] * acc
        o_ref[...] = acc.astype(BF16)

    mn_blk = pl.BlockSpec((tm, tn), lambda i, j, k: (i, j))
    n_vec = pl.BlockSpec((1, tn), lambda i, j, k: (0, j))
    x1, o_act = _matmul(
        "oproj", "nn", (s // tm, d // tn, d // tk), merged, pl.BlockSpec((tm, tk), lambda i, j, k: (i, k)),
        w_o_f, pl.BlockSpec((tk, tn), lambda i, j, k: (k, j)),
        [_sds((s, d), F32), _sds((s, d), BF16)], [mn_blk, mn_blk], (tm, tn),
        epilogue=oproj_epilogue, extras=(x2, gate1), extra_specs=(mn_blk, n_vec))

    h2 = _norm_forward("norm2_fwd", x1, norm2_w, scale2, shift2)
    w_ff1_f, w_ff2_f = _land(fl_c, h2)
    w_ff2_f = w_ff2_f.reshape(4 * d, d)

    def ff1_epilogue(acc, extra_refs, out_refs):
        r = jnp.maximum(acc, 0.0)
        out_refs[0][...] = r.astype(BF16)
        out_refs[1][...] = (r * r).astype(BF16)

    ff_blk = pl.BlockSpec((tm, half), lambda i, j, k: (i, j))
    relu, act = _matmul(
        "ff1", "nn", (s // tm, N_DEV, d // tk), h2, pl.BlockSpec((tm, tk), lambda i, j, k: (i, k)),
        w_ff1_f, pl.BlockSpec((None, tk, half), lambda i, j, k: (j, k, 0)),
        [_sds((s, 4 * d), BF16)] * 2, [ff_blk, ff_blk], (tm, half), epilogue=ff1_epilogue)

    def ff2_epilogue(acc, extra_refs, out_refs):
        x1_ref, g_ref, t_ref = extra_refs
        f_ref, dy_ref, sq_ref = out_refs
        err = x1_ref[...] + g_ref[...] * acc - t_ref[...]
        f_ref[...] = acc.astype(BF16)
        dy_ref[...] = err * (1.0 / d)
        sq_ref[...] = jnp.full(sq_ref.shape, jnp.sum(err * err), F32)

    f_act, dy, sq = _matmul(
        "ff2", "nn", (s // tm, d // tn, 4 * d // tk), act, pl.BlockSpec((tm, tk), lambda i, j, k: (i, k)),
        w_ff2_f, pl.BlockSpec((tk, tn), lambda i, j, k: (k, j)),
        [_sds((s, d), BF16), _sds((s, d), F32), _sds((s // tm * SUBLANES, d // tn * LANES), F32)],
        [mn_blk, mn_blk, pl.BlockSpec((SUBLANES, LANES), lambda i, j, k: (i, j))], (tm, tn),
        epilogue=ff2_epilogue, extras=(x1, gate2, target), extra_specs=(mn_blk, n_vec, mn_blk))
    loss_local = (0.5 / d) * jnp.sum(sq[::SUBLANES, ::LANES])
    loss = lax.psum(loss_local, MESH_AXES)

    df, dgate2 = _gate_backward("gate2_bwd", dy, gate2, f_act)
    tok_k = _tile(s, 1024)
    g_ff2 = _matmul(
        "g_ff2", "tn", (4 * d // tk, d // tn, s // tok_k), act, pl.BlockSpec((tok_k, tk), lambda i, j, k: (k, i)),
        df, pl.BlockSpec((tok_k, tn), lambda i, j, k: (k, j)),
        [_sds((4 * d, d), BF16)], [pl.BlockSpec((tk, tn), lambda i, j, k: (i, j))], (tk, tn))[0]

    def da_epilogue(acc, extra_refs, out_refs):
        out_refs[0][...] = (acc * (2.0 * extra_refs[0][...].astype(F32))).astype(BF16)

    big_blk = pl.BlockSpec((tm, tn), lambda i, j, k: (i, j))
    fl_f2, tok = _reduce_scatter_start("F2", [g_ff2.reshape(N_DEV, half, d)])
    df1 = _matmul(
        "da_ff", "nt", (s // tm, 4 * d // tn, d // tk), _after(df, tok), pl.BlockSpec((tm, tk), lambda i, j, k: (i, k)),
        w_ff2_f, pl.BlockSpec((tn, tk), lambda i, j, k: (j, k)),
        [_sds((s, 4 * d), BF16)], [big_blk], (tm, tn),
        epilogue=da_epilogue, extras=(relu,), extra_specs=(big_blk,))[0]

    fl_f2, tok = _reduce_scatter_middle("F2", fl_f2, df1, me)
    g_ff1 = _matmul(
        "g_ff1", "tn", (d // tk, N_DEV, s // tok_k), _after(h2, tok), pl.BlockSpec((tok_k, tk), lambda i, j, k: (k, i)),
        df1, pl.BlockSpec((tok_k, half), lambda i, j, k: (k, j)),
        [_sds((N_DEV, d, half), BF16)], [pl.BlockSpec((None, tk, half), lambda i, j, k: (j, i, 0))], (tk, half))[0]

    fl_f1, tok = _reduce_scatter_start("F1", [g_ff1])
    dh2 = _matmul(
        "dh2", "nt", (s // tm, d // tn, N_DEV), _after(df1, tok), pl.BlockSpec((tm, half), lambda i, j, k: (i, k)),
        w_ff1_f, pl.BlockSpec((None, tn, half), lambda i, j, k: (k, j, 0)),
        [_sds((s, d), F32)], [mn_blk], (tm, tn))[0]

    g_w_ff2, = _reduce_scatter_finish("F2", fl_f2, dh2)
    fl_f1, tok = _reduce_scatter_middle("F1", fl_f1, dh2, me)
    dx1, dshift2, dscale2, g_norm2 = _norm_backward("norm2_bwd", _after(dh2, tok), x1, norm2_w, scale2, dy)

    do, dgate1 = _gate_backward("gate1_bwd", dx1, gate1, o_act)
    g_o = _matmul(
        "g_o", "tn", (d // tk, d // tn, s // tok_k), merged, pl.BlockSpec((tok_k, tk), lambda i, j, k: (k, i)),
        do, pl.BlockSpec((tok_k, tn), lambda i, j, k: (k, j)),
        [_sds((d, d), BF16)], [pl.BlockSpec((tk, tn), lambda i, j, k: (i, j))], (tk, tn))[0]

    def merge_bwd_epilogue(acc, extra_refs, out_refs):
        ga_ref, gb_ref, ya_ref, yb_ref = extra_refs
        dya_ref, dyb_ref, dga_ref, dgb_ref = out_refs
        sa = jax.nn.sigmoid(ga_ref[...])
        sb = jax.nn.sigmoid(gb_ref[...])
        dya_ref[...] = (acc * sa).astype(BF16)
        dyb_ref[...] = (acc * sb).astype(BF16)
        dga_ref[...] = (acc * ya_ref[...].astype(F32) * (sa * (1.0 - sa))).astype(BF16)
        dgb_ref[...] = (acc * yb_ref[...].astype(F32) * (sb * (1.0 - sb))).astype(BF16)

    nb = d // tn
    dy_a, dy_b, dproj, dg_b = _matmul(
        "dmerged", "nt", (s // tm, nb, d // tk), do, pl.BlockSpec((tm, tk), lambda i, j, k: (i, k)),
        w_o_f, pl.BlockSpec((tn, tk), lambda i, j, k: (j, k)),
        [_sds((s, d), BF16), _sds((s, d), BF16), _sds((s, 4 * d), BF16), _sds((s, d), BF16)],
        [mn_blk, mn_blk, pl.BlockSpec((tm, tn), lambda i, j, k: (i, 2 * nb + j)), mn_blk], (tm, tn),
        epilogue=merge_bwd_epilogue, extras=(proj, proj, y_a, y_b),
        extra_specs=(pl.BlockSpec((tm, tn), lambda i, j, k: (i, 2 * nb + j)),
                     pl.BlockSpec((tm, tn), lambda i, j, k: (i, 3 * nb + j)), mn_blk, mn_blk))
    dproj = _place_columns("place_dgb", dg_b, dproj, 3)

    up_a = pl.BlockSpec((tok_k, half), lambda i, j, k: (k, 0))
    up_b = pl.BlockSpec((tok_k, d8), lambda i, j, k: (k, j))
    up_o = pl.BlockSpec((None, half, d8), lambda i, j, k: (j, 0, 0))
    g_a_up = _matmul("g_a_up", "tn", (1, N_DEV, s // tok_k), ya_in, up_a, dy_a, up_b,
                     [_sds((N_DEV, half, d8), BF16)], [up_o], (half, d8))[0]
    g_b_up = _matmul("g_b_up", "tn", (1, N_DEV, s // tok_k), attn, up_a, dy_b, up_b,
                     [_sds((N_DEV, half, d8), BF16)], [up_o], (half, d8))[0]
    dn_a = pl.BlockSpec((tm, d8), lambda i, j, k: (i, k))
    dn_b = pl.BlockSpec((None, half, d8), lambda i, j, k: (k, 0, 0))
    dn_o = pl.BlockSpec((tm, half), lambda i, j, k: (i, 0))
    dya_in = _matmul("d_ya_in", "nt", (s // tm, 1, N_DEV), dy_a, dn_a, w_a_f, dn_b,
                     [_sds((s, half), BF16)], [dn_o], (tm, half))[0]
    dattn = _matmul("d_attn", "nt", (s // tm, 1, N_DEV), dy_b, dn_a, w_b_f, dn_b,
                    [_sds((s, half), BF16)], [dn_o], (tm, half))[0]

    dproj, g_pool, g_pool_scale = _pool_backward(dya_in, pooled, w_pool_f, pool_scale, dproj)
    g_w_ff1, = _reduce_scatter_finish("F1", fl_f1, g_pool)
    g_pool_send = g_pool.astype(BF16).reshape(n_groups, N_DEV, rows_pool, cg).transpose(1, 0, 2, 3)
    g_pool_send = g_pool_send.reshape(N_DEV, n_groups * rows_pool, cg)
    fl_b, tok = _reduce_scatter_start("B", [g_pool_send, g_a_up, g_b_up, g_o.reshape(N_DEV, d8, d)])
    dqn, dkn, dproj = _attention_backward(qn, kn, vb, _after(dattn, tok), dproj, 3)
    fl_b, tok = _reduce_scatter_middle("B", fl_b, dqn, me)
    dproj, g_qnorm = _qk_norm_backward("qnorm_bwd", _after(dqn, tok), proj, 1, q_norm_w, dproj, half)
    dproj, g_knorm = _qk_norm_backward("knorm_bwd", dkn, proj, 2, k_norm_w, dproj, half)

    g_in = _matmul(
        "g_in", "tn", (d // tk, N_DEV, s // tok_k), h, pl.BlockSpec((tok_k, tk), lambda i, j, k: (k, i)),
        dproj, pl.BlockSpec((tok_k, half), lambda i, j, k: (k, j)),
        [_sds((N_DEV, d, half), BF16)], [pl.BlockSpec((None, tk, half), lambda i, j, k: (j, i, 0))], (tk, half))[0]
    fl_in, tok = _reduce_scatter_start("I", [g_in])
    dh = _matmul(
        "dh", "nt", (s // tm, d // tn, N_DEV), _after(dproj, tok), pl.BlockSpec((tm, half), lambda i, j, k: (i, k)),
        w_in_f, pl.BlockSpec((None, tn, half), lambda i, j, k: (k, j, 0)),
        [_sds((s, d), F32)], [mn_blk], (tm, tn))[0]
    g_w_pool, g_w_a_up, g_w_b_up, g_w_o = _reduce_scatter_finish("B", fl_b, dh)
    fl_in, tok = _reduce_scatter_middle("I", fl_in, dh, me)
    grad_x, dshift1, dscale1, g_norm1 = _norm_backward("norm1_bwd", _after(dh, tok), x2, norm1_w, scale1, dx1)

    dmod = jnp.concatenate([dshift1, dscale1, dgate1, dshift2, dscale2, dgate2], axis=1)
    pieces = [dmod, g_norm1, g_norm2, g_pool_scale, g_qnorm, g_knorm]
    packed_rows = [_rows_of_lanes(p) for p in pieces]
    offsets = [0]
    for p in packed_rows:
        offsets.append(offsets[-1] + p.shape[0])
    packed = jnp.concatenate(packed_rows, axis=0)
    small_all = _all_gather_2d("ag_small", packed)
    small_sum = _sum_slots("small_sum", small_all[None], F32)[0]

    def unpack(i, width):
        return small_sum[offsets[i]:offsets[i] + width // LANES].reshape(1, width)

    g_b_ada = unpack(0, N_MOD * d)
    g_norm1_w = unpack(1, d)
    g_norm2_w = unpack(2, d)
    g_pool_scale_w = unpack(3, half)
    g_q_norm_w = unpack(4, HEAD_DIM)
    g_k_norm_w = unpack(5, HEAD_DIM)
    dmod_all = small_all[:, :N_MOD * d // LANES].reshape(N_DEV, N_MOD * d)
    dmod_cols = lax.dynamic_slice_in_dim(dmod_all, me * wa, wa, axis=1)
    g_w_ada = _ada_weight_grad(c_all, dmod_cols)[None]

    g_w_in, = _reduce_scatter_finish("I", fl_in, g_w_ada)

    grads = {
        "w_ada": g_w_ada, "b_ada": g_b_ada, "norm1_w": g_norm1_w, "w_in": g_w_in[None],
        "q_norm_w": g_q_norm_w, "k_norm_w": g_k_norm_w,
        "w_pool": g_w_pool.reshape(w_pool.shape), "pool_scale": g_pool_scale_w,
        "w_a_up": g_w_a_up[None], "w_b_up": g_w_b_up[None], "w_o": g_w_o[None],
        "norm2_w": g_norm2_w, "w_ff1": g_w_ff1[None], "w_ff2": g_w_ff2[None],
    }
    weights = {"w_ada": (w_ada, m_w_ada, v_w_ada), "b_ada": (b_ada, m_b_ada, v_b_ada),
               "norm1_w": (norm1_w, m_norm1_w, v_norm1_w), "w_in": (w_in, m_w_in, v_w_in),
               "q_norm_w": (q_norm_w, m_q_norm_w, v_q_norm_w), "k_norm_w": (k_norm_w, m_k_norm_w, v_k_norm_w),
               "w_pool": (w_pool, m_w_pool, v_w_pool), "pool_scale": (pool_scale, m_pool_scale, v_pool_scale),
               "w_a_up": (w_a_up, m_w_a_up, v_w_a_up), "w_b_up": (w_b_up, m_w_b_up, v_w_b_up),
               "w_o": (w_o, m_w_o, v_w_o), "norm2_w": (norm2_w, m_norm2_w, v_norm2_w),
               "w_ff1": (w_ff1, m_w_ff1, v_w_ff1), "w_ff2": (w_ff2, m_w_ff2, v_w_ff2)}
    order = list(weights)
    deltas, new_m, new_v = {}, {}, {}
    for name in order:
        wt, mt, vt = weights[name]
        shape = wt.shape
        flat = (-1, shape[-1])
        dl, nm, nv = _adamw("adamw_" + name, wt.reshape(flat), grads[name].reshape(flat),
                            mt.reshape(flat), vt.reshape(flat))
        deltas[name], new_m[name], new_v[name] = dl.reshape(shape), nm.reshape(shape), nv.reshape(shape)

    return (loss, grad_x[None], *[grads[n] for n in order], *[deltas[n] for n in order],
            *[new_m[n] for n in order], *[new_v[n] for n in order])
```

```python
import math

import jax
import jax.numpy as jnp
from jax import lax
from jax.experimental import pallas as pl
from jax.experimental.pallas import tpu as pltpu

F32 = jnp.float32
BF16 = jnp.bfloat16
MESH_AXES = ("x", "y", "c")
N_DEV = 8
HEAD_DIM = 128
POOL_WINDOWS = (2, 4, 8, 16)
N_MOD = 6
NORM_EPS = 1e-6
LANES = 128
SUBLANES = 8
VMEM_LIMIT_BYTES = 56 * 1024 * 1024
Q_TILE = 256
K_TILE = 256
POOL_TILE = 256

ADAM_LR = 0.001
ADAM_B1 = 0.9
ADAM_B2 = 0.999
ADAM_EPS = 1e-08
ADAM_WD = 0.01
ADAM_STEP = 10

_NN = (((1,), (0,)), ((), ()))
_NT = (((1,), (1,)), ((), ()))
_TN = (((0,), (0,)), ((), ()))
_DIMS = {"nn": _NN, "nt": _NT, "tn": _TN}


def _dot(a, b, mode="nn"):
    return lax.dot_general(a, b, _DIMS[mode], preferred_element_type=F32)


def _params(*sem):
    return pltpu.CompilerParams(dimension_semantics=sem, vmem_limit_bytes=VMEM_LIMIT_BYTES)


def _tile(dim, pref, align=SUBLANES):
    for t in range(min(dim, pref), 0, -1):
        if dim % t == 0 and t % align == 0:
            return t
    return dim


def _group_index(axes):
    idx = 0
    for a in axes:
        idx = idx * 2 + lax.axis_index(a)
    return idx


def _peer_device(axes, k):
    coords = {a: lax.axis_index(a) for a in MESH_AXES}
    for pos, a in enumerate(axes):
        if (k >> (len(axes) - 1 - pos)) & 1:
            coords[a] = 1 - coords[a]
    return tuple(coords[a] for a in MESH_AXES)


_AXIS_BIT = {"x": 4, "y": 2, "c": 1}
_ANY = pl.BlockSpec(memory_space=pl.ANY)


def _device_xor(mask):
    return tuple(1 - lax.axis_index(a) if mask & _AXIS_BIT[a] else lax.axis_index(a) for a in MESH_AXES)


def _remote(src, dst, send_sem, recv_sem, mask):
    return pltpu.make_async_remote_copy(src_ref=src, dst_ref=dst, send_sem=send_sem, recv_sem=recv_sem,
                                        device_id=_device_xor(mask), device_id_type=pl.DeviceIdType.MESH)


def _start_all(copies):
    for cp in copies:
        cp.start()


def _wait_all(copies):
    for cp in copies:
        cp.wait()


def _gather_ici(name, bufs):
    na = len(bufs)
    bx, by = _AXIS_BIT["x"], _AXIS_BIT["y"]

    def body(*refs):
        out = refs[na:2 * na]
        send_sems, recv_sems = refs[2 * na:]
        me = _group_index(MESH_AXES)

        def copy(a, n, slot, color, mask):
            half = out[a].shape[1] // 2
            piece = out[a].at[slot, pl.ds(color * half, half)]
            return _remote(piece, piece, send_sems.at[a, n], recv_sems.at[a, n], mask)

        first, second = [], []
        for a in range(na):
            first += [copy(a, 0, me, 0, bx), copy(a, 1, me, 1, by)]
            second += [copy(a, 2, me, 0, by), copy(a, 3, me ^ bx, 0, by),
                       copy(a, 4, me, 1, bx), copy(a, 5, me ^ by, 1, bx)]
        _start_all(first)
        _wait_all(first)
        _start_all(second)
        _wait_all(second)

    return pl.pallas_call(
        body, name=name, out_shape=tuple(jax.ShapeDtypeStruct(b.shape, b.dtype) for b in bufs),
        in_specs=[_ANY] * na, out_specs=tuple([_ANY] * na),
        input_output_aliases={a: a for a in range(na)},
        scratch_shapes=[pltpu.SemaphoreType.DMA((na, 6)), pltpu.SemaphoreType.DMA((na, 6))],
    )(*bufs)


def _gather_d2d(name, bufs):
    na = len(bufs)
    masks = (0, _AXIS_BIT["y"], _AXIS_BIT["x"], _AXIS_BIT["x"] | _AXIS_BIT["y"])

    def body(*refs):
        out = refs[na:2 * na]
        send_sems, recv_sems = refs[2 * na:]
        me = _group_index(MESH_AXES)
        copies = []
        for a in range(na):
            for n, m in enumerate(masks):
                piece = out[a].at[me ^ m]
                copies.append(_remote(piece, piece, send_sems.at[a, n], recv_sems.at[a, n], _AXIS_BIT["c"]))
        _start_all(copies)
        _wait_all(copies)

    return pl.pallas_call(
        body, name=name, out_shape=tuple(jax.ShapeDtypeStruct(b.shape, b.dtype) for b in bufs),
        in_specs=[_ANY] * na, out_specs=tuple([_ANY] * na),
        input_output_aliases={a: a for a in range(na)},
        scratch_shapes=[pltpu.SemaphoreType.DMA((na, 4)), pltpu.SemaphoreType.DMA((na, 4))],
    )(*bufs)


def _scatter(name, srcs, send_slots, masks):
    na = len(srcs)
    nm = len(masks)

    def body(*refs):
        src, out = refs[:na], refs[na:2 * na]
        send_sems, recv_sems = refs[2 * na:]
        slots = send_slots(_group_index(MESH_AXES))
        copies = []
        for n, m in enumerate(masks):
            for a in range(na):
                copies.append(_remote(src[a].at[slots[n]], out[a].at[n],
                                      send_sems.at[a, n], recv_sems.at[a, n], m))
        _start_all(copies)
        _wait_all(copies)

    return pl.pallas_call(
        body, name=name,
        out_shape=tuple(jax.ShapeDtypeStruct((nm,) + s.shape[1:], s.dtype) for s in srcs),
        in_specs=[_ANY] * na, out_specs=tuple([_ANY] * na),
        scratch_shapes=[pltpu.SemaphoreType.DMA((na, nm)), pltpu.SemaphoreType.DMA((na, nm))],
    )(*srcs)


def _add_received(name, own, own_slots, received, out_dtype):
    nj, r, c = received.shape
    tr = _tile(r, max(2 * SUBLANES, (1 << 20) // c), 2 * SUBLANES)

    def body(slots_ref, own_ref, rec_ref, o_ref):
        del slots_ref
        o_ref[...] = (own_ref[...].astype(F32) + rec_ref[...].astype(F32)).astype(o_ref.dtype)

    grid_spec = pltpu.PrefetchScalarGridSpec(
        num_scalar_prefetch=1, grid=(nj, r // tr),
        in_specs=[pl.BlockSpec((None, tr, c), lambda j, i, slots: (slots[j], i, 0)),
                  pl.BlockSpec((None, tr, c), lambda j, i, slots: (j, i, 0))],
        out_specs=pl.BlockSpec((None, tr, c), lambda j, i, slots: (j, i, 0)))
    return pl.pallas_call(
        body, name=name, grid_spec=grid_spec, out_shape=jax.ShapeDtypeStruct((nj, r, c), out_dtype),
        compiler_params=_params("parallel", "parallel"),
    )(own_slots, own, received)


def _add_final(name, own, received):
    nj, r, c = received.shape
    tr = _tile(r, max(2 * SUBLANES, (1 << 19) // c), 2 * SUBLANES)

    def body(own_ref, rec_ref, o_ref):
        acc = own_ref[...].astype(F32)
        for j in range(nj):
            acc = acc + rec_ref[j].astype(F32)
        o_ref[...] = acc

    return pl.pallas_call(
        body, name=name, grid=(r // tr,), out_shape=jax.ShapeDtypeStruct((r, c), F32),
        in_specs=[pl.BlockSpec((None, tr, c), lambda i: (0, i, 0)),
                  pl.BlockSpec((nj, tr, c), lambda i: (0, i, 0))],
        out_specs=pl.BlockSpec((tr, c), lambda i: (i, 0)),
        compiler_params=_params("parallel"),
    )(own, received)


_HBM = pl.BlockSpec(memory_space=pltpu.HBM)
_SEM = pl.BlockSpec(memory_space=pltpu.SEMAPHORE)
_DATAFLOW = pltpu.SideEffectType.DATAFLOW_SIDE_EFFECTING


def _launch(name, bufs, plan, n_copies):
    nb = len(bufs)

    def body(*refs):
        ins = refs[:nb]
        send_sems, recv_sems = refs[nb], refs[nb + 1]
        token = refs[-1]
        copies = plan(ins, _group_index(MESH_AXES))
        assert len(copies) == n_copies
        for n, (src, dst, mask) in enumerate(copies):
            _remote(src, dst, send_sems.at[n], recv_sems.at[n], mask).start()
        token[...] = jnp.zeros_like(token)

    outs = pl.pallas_call(
        body, name=name,
        out_shape=(pltpu.SemaphoreType.DMA((n_copies,)), pltpu.SemaphoreType.DMA((n_copies,)),
                   *[pltpu.HBM(b.shape, b.dtype) for b in bufs], jax.ShapeDtypeStruct((SUBLANES, LANES), F32)),
        in_specs=[_HBM] * nb,
        out_specs=(_SEM, _SEM, *[_HBM] * nb, pl.BlockSpec(memory_space=pltpu.VMEM)),
        input_output_aliases={i: 2 + i for i in range(nb)},
        compiler_params=pltpu.CompilerParams(has_side_effects=_DATAFLOW),
    )(*[pltpu.with_memory_space_constraint(b, pltpu.HBM) for b in bufs])
    return (name, plan, n_copies, outs[0], outs[1], list(outs[2:2 + nb])), outs[-1]


def _land(flight, after):
    name, plan, n_copies, send_sems, recv_sems, bufs = flight
    nb = len(bufs)

    def body(*refs):
        ins = refs[:nb]
        s_sems, r_sems = refs[nb], refs[nb + 1]
        for n, (src, dst, mask) in enumerate(plan(ins, _group_index(MESH_AXES))):
            cp = _remote(src, dst, s_sems.at[n], r_sems.at[n], mask)
            cp.wait_send()
            cp.wait_recv()

    outs = pl.pallas_call(
        body, name=name + "_land",
        out_shape=tuple(pltpu.HBM(b.shape, b.dtype) for b in bufs),
        in_specs=[_HBM] * nb + [_SEM, _SEM, _ANY], out_specs=tuple([_HBM] * nb),
        input_output_aliases={i: i for i in range(nb)},
        compiler_params=pltpu.CompilerParams(has_side_effects=_DATAFLOW),
    )(*bufs, send_sems, recv_sems, after)
    return list(outs)


def _plan_gather_ici(phase):
    bx, by = _AXIS_BIT["x"], _AXIS_BIT["y"]

    def plan(refs, me):
        copies = []
        for ref in refs:
            half = ref.shape[1] // 2

            def piece(slot, color, mask, ref=ref, half=half):
                p = ref.at[slot, pl.ds(color * half, half)]
                return (p, p, mask)

            if phase == 0:
                copies += [piece(me, 0, bx), piece(me, 1, by)]
            else:
                copies += [piece(me, 0, by), piece(me ^ bx, 0, by), piece(me, 1, bx), piece(me ^ by, 1, bx)]
        return copies

    return plan


def _plan_gather_d2d(refs, me):
    copies = []
    for ref in refs:
        for m in _CHIP_MASKS:
            copies.append((ref.at[me ^ m], ref.at[me ^ m], _AXIS_BIT["c"]))
    return copies


def _plan_scatter_d2d(refs, me):
    na = len(refs) // 2
    copies = []
    for a in range(na):
        for j, m in enumerate(_CHIP_MASKS):
            copies.append((refs[a].at[me ^ _AXIS_BIT["c"] ^ m], refs[na + a].at[j], _AXIS_BIT["c"]))
    return copies


def _plan_scatter_ici(refs, me):
    del me
    na = len(refs) // 2
    copies = []
    for a in range(na):
        for n, m in enumerate(_CHIP_MASKS[1:]):
            copies.append((refs[a].at[n + 1], refs[na + a].at[n], m))
    return copies


def _with_deps(body, n_in, deps):
    if not deps:
        return body

    def wrapped(*refs):
        return body(*refs[:n_in], *refs[n_in + len(deps):])

    return wrapped


def _reduce_scatter_start(tag, grads):
    lands = [lax.empty((len(_CHIP_MASKS),) + g.shape[1:], g.dtype) for g in grads]
    return _launch("rs%s_d2d" % tag, list(grads) + lands, _plan_scatter_d2d, len(_CHIP_MASKS) * len(grads))


def _reduce_scatter_middle(tag, flight, after, me):
    bufs = _land(flight, after)
    na = len(bufs) // 2
    own_slots = jnp.stack([me ^ m for m in _CHIP_MASKS]).astype(jnp.int32)
    sums = [_add_received("rs%s_add_d2d_%d" % (tag, a), bufs[a], own_slots, bufs[na + a], BF16) for a in range(na)]
    lands = [lax.empty((len(_CHIP_MASKS) - 1,) + h.shape[1:], h.dtype) for h in sums]
    return _launch("rs%s_ici" % tag, sums + lands, _plan_scatter_ici, (len(_CHIP_MASKS) - 1) * na)


def _reduce_scatter_finish(tag, flight, after):
    bufs = _land(flight, after)
    na = len(bufs) // 2
    return [_add_final("rs%s_add_ici_%d" % (tag, a), bufs[a], bufs[na + a]) for a in range(na)]


def _all_gather_2d(name, x, deps=()):
    r, c = x.shape

    def body(x_ref, out_ref, send_sems, recv_sems):
        me = _group_index(MESH_AXES)
        out_ref[me] = x_ref[...]
        copies = []
        for k in range(1, N_DEV):
            cp = pltpu.make_async_remote_copy(
                src_ref=x_ref, dst_ref=out_ref.at[me],
                send_sem=send_sems.at[k - 1], recv_sem=recv_sems.at[k - 1],
                device_id=_peer_device(MESH_AXES, k), device_id_type=pl.DeviceIdType.MESH)
            cp.start()
            copies.append(cp)
        for cp in copies:
            cp.wait()

    vmem = pl.BlockSpec(memory_space=pltpu.VMEM)
    return pl.pallas_call(
        _with_deps(body, 1, deps), name=name, out_shape=jax.ShapeDtypeStruct((N_DEV, r, c), x.dtype),
        in_specs=[vmem] + [_ANY] * len(deps), out_specs=vmem,
        scratch_shapes=[pltpu.SemaphoreType.DMA((N_DEV - 1,)), pltpu.SemaphoreType.DMA((N_DEV - 1,))],
    )(x, *deps)


def _sum_slots(name, buf, out_dtype):
    pre, n, r, c = buf.shape
    tr = _tile(r, max(SUBLANES * 2, (1 << 20) // c))

    def body(b_ref, o_ref):
        acc = b_ref[0].astype(F32)
        for q in range(1, n):
            acc = acc + b_ref[q].astype(F32)
        o_ref[...] = acc.astype(o_ref.dtype)

    return pl.pallas_call(
        body, name=name, grid=(pre, r // tr),
        out_shape=jax.ShapeDtypeStruct((pre, r, c), out_dtype),
        in_specs=[pl.BlockSpec((None, n, tr, c), lambda i, j: (i, 0, j, 0))],
        out_specs=pl.BlockSpec((None, tr, c), lambda i, j: (i, j, 0)),
        compiler_params=_params("parallel", "parallel"),
    )(buf)


def _all_gather_weights(bufs):
    return _gather_d2d("ag_d2d", list(_gather_ici("ag_ici", bufs)))


_CHIP_MASKS = (0, _AXIS_BIT["y"], _AXIS_BIT["x"], _AXIS_BIT["x"] | _AXIS_BIT["y"])


def _reduce_scatter_grads(bufs):
    me = _group_index(MESH_AXES)
    bc = _AXIS_BIT["c"]
    r1 = _scatter("rs_d2d", bufs, lambda i: [i ^ bc ^ m for m in _CHIP_MASKS], (bc,) * len(_CHIP_MASKS))
    own_slots = jnp.stack([me ^ m for m in _CHIP_MASKS]).astype(jnp.int32)
    half = [_add_received("rs_add_d2d_%d" % a, b, own_slots, r, BF16) for a, (b, r) in enumerate(zip(bufs, r1))]
    r2 = _scatter("rs_ici", half, lambda i: [1, 2, 3], _CHIP_MASKS[1:])
    return [_add_final("rs_add_ici_%d" % a, h, r) for a, (h, r) in enumerate(zip(half, r2))]


def _matmul(name, mode, grid, a, a_spec, b, b_spec, out_shapes, out_specs, acc_shape,
            epilogue=None, extras=(), extra_specs=(), aliases=None, deps=()):
    nk = grid[2]
    n_extra = len(extras)
    n_out = len(out_shapes)

    def finish(acc, extra_refs, out_refs):
        if epilogue is None:
            out_refs[0][...] = acc.astype(out_refs[0].dtype)
        else:
            epilogue(acc, extra_refs, out_refs)

    def body(*refs):
        a_ref, b_ref = refs[0], refs[1]
        extra_refs = refs[2:2 + n_extra]
        out_refs = refs[2 + n_extra:2 + n_extra + n_out]
        if nk == 1:
            finish(_dot(a_ref[...], b_ref[...], mode), extra_refs, out_refs)
            return
        acc_ref = refs[-1]
        k = pl.program_id(2)

        @pl.when(k == 0)
        def _():
            acc_ref[...] = jnp.zeros_like(acc_ref)

        acc_ref[...] += _dot(a_ref[...], b_ref[...], mode)

        @pl.when(k == nk - 1)
        def _():
            finish(acc_ref[...], extra_refs, out_refs)

    scratch = [] if nk == 1 else [pltpu.VMEM(acc_shape, F32)]
    return pl.pallas_call(
        _with_deps(body, 2 + n_extra, deps), name=name, grid=grid, out_shape=tuple(out_shapes),
        in_specs=[a_spec, b_spec] + list(extra_specs) + [_ANY] * len(deps), out_specs=tuple(out_specs),
        scratch_shapes=scratch, input_output_aliases=aliases or {},
        compiler_params=_params("parallel", "parallel", "arbitrary"),
    )(a, b, *extras, *deps)


def _sds(shape, dtype):
    return jax.ShapeDtypeStruct(tuple(shape), dtype)


def _ada_forward(c_all, w_ada, b_shard):
    nb, d = c_all.shape
    w = w_ada.shape[1]
    tn = _tile(w, 512)

    def body(c_ref, w_ref, b_ref, o_ref):
        cv = c_ref[...]
        sc = cv * jax.nn.sigmoid(cv)
        o_ref[...] = jnp.dot(sc, w_ref[...], precision=lax.Precision.HIGHEST,
                             preferred_element_type=F32) + b_ref[...]

    return pl.pallas_call(
        body, name="ada_fwd", grid=(w // tn,), out_shape=_sds((nb, w), F32),
        in_specs=[pl.BlockSpec((nb, d), lambda j: (0, 0)), pl.BlockSpec((d, tn), lambda j: (0, j)),
                  pl.BlockSpec((1, tn), lambda j: (0, j))],
        out_specs=pl.BlockSpec((nb, tn), lambda j: (0, j)),
        compiler_params=_params("parallel"),
    )(c_all, w_ada, b_shard)


def _ada_weight_grad(c_all, dmod_cols):
    nb, d = c_all.shape
    w = dmod_cols.shape[1]
    tn = _tile(w, 512)

    def body(c_ref, g_ref, o_ref):
        cv = c_ref[...]
        sc = cv * jax.nn.sigmoid(cv)
        o_ref[...] = lax.dot_general(sc, g_ref[...], _TN, precision=lax.Precision.HIGHEST,
                                     preferred_element_type=F32)

    return pl.pallas_call(
        body, name="ada_wgrad", grid=(w // tn,), out_shape=_sds((d, w), F32),
        in_specs=[pl.BlockSpec((nb, d), lambda j: (0, 0)), pl.BlockSpec((nb, tn), lambda j: (0, j))],
        out_specs=pl.BlockSpec((d, tn), lambda j: (0, j)),
        compiler_params=_params("parallel"),
    )(c_all, dmod_cols)


def _norm_forward(name, x, norm_w, scale, shift, deps=()):
    s, d = x.shape
    tm = _tile(s, 256)

    def body(x_ref, w_ref, sc_ref, sh_ref, h_ref):
        xv = x_ref[...]
        r = lax.rsqrt(jnp.mean(xv * xv, axis=-1, keepdims=True) + NORM_EPS)
        h = (xv * r * w_ref[...]) * (1.0 + sc_ref[...]) + sh_ref[...]
        h_ref[...] = h.astype(BF16)

    vec = pl.BlockSpec((1, d), lambda i: (0, 0))
    row = pl.BlockSpec((tm, d), lambda i: (i, 0))
    return pl.pallas_call(
        _with_deps(body, 4, deps), name=name, grid=(s // tm,), out_shape=_sds((s, d), BF16),
        in_specs=[row, vec, vec, vec] + [_ANY] * len(deps), out_specs=row, compiler_params=_params("parallel"),
    )(x, norm_w, scale, shift, *deps)


def _norm_backward(name, dh, x, norm_w, scale, dres, deps=()):
    s, d = x.shape
    tm = _tile(s, 256)

    def body(dh_ref, x_ref, w_ref, sc_ref, dres_ref, dx_ref, dshift_ref, dscale_ref, dw_ref):
        @pl.when(pl.program_id(0) == 0)
        def _():
            dshift_ref[...] = jnp.zeros_like(dshift_ref)
            dscale_ref[...] = jnp.zeros_like(dscale_ref)
            dw_ref[...] = jnp.zeros_like(dw_ref)

        xv = x_ref[...]
        g = dh_ref[...]
        r = lax.rsqrt(jnp.mean(xv * xv, axis=-1, keepdims=True) + NORM_EPS)
        n = xv * r
        gain = 1.0 + sc_ref[...]
        gn = g * n
        dshift_ref[...] += jnp.sum(g, axis=0, keepdims=True)
        dscale_ref[...] += jnp.sum(gn, axis=0, keepdims=True) * w_ref[...]
        dw_ref[...] += jnp.sum(gn, axis=0, keepdims=True) * gain
        dn = g * (w_ref[...] * gain)
        dx_ref[...] = dres_ref[...] + r * (dn - n * jnp.mean(dn * n, axis=-1, keepdims=True))

    vec = pl.BlockSpec((1, d), lambda i: (0, 0))
    row = pl.BlockSpec((tm, d), lambda i: (i, 0))
    return pl.pallas_call(
        _with_deps(body, 5, deps), name=name, grid=(s // tm,),
        out_shape=(_sds((s, d), F32), _sds((1, d), F32), _sds((1, d), F32), _sds((1, d), F32)),
        in_specs=[row, row, vec, vec, row] + [_ANY] * len(deps), out_specs=(row, vec, vec, vec),
        compiler_params=_params("arbitrary"),
    )(dh, x, norm_w, scale, dres, *deps)


def _gate_backward(name, d, gate, other):
    s, w = d.shape
    tm = _tile(s, 256)

    def body(d_ref, g_ref, o_ref, dg_ref, dgate_ref):
        @pl.when(pl.program_id(0) == 0)
        def _():
            dgate_ref[...] = jnp.zeros_like(dgate_ref)

        dv = d_ref[...]
        dg_ref[...] = (dv * g_ref[...]).astype(BF16)
        dgate_ref[...] += jnp.sum(dv * o_ref[...].astype(F32), axis=0, keepdims=True)

    vec = pl.BlockSpec((1, w), lambda i: (0, 0))
    row = pl.BlockSpec((tm, w), lambda i: (i, 0))
    return pl.pallas_call(
        body, name=name, grid=(s // tm,), out_shape=(_sds((s, w), BF16), _sds((1, w), F32)),
        in_specs=[row, vec, row], out_specs=(row, vec), compiler_params=_params("arbitrary"),
    )(d, gate, other)


def _split_bf16(v):
    hi = v.astype(BF16)
    lo = (v - hi.astype(F32)).astype(BF16)
    return hi, lo


def _pool_forward(proj, w_pool, pool_scale, deps=()):
    s = proj.shape[0]
    g_n, cg, _ = w_pool.shape
    t = POOL_TILE
    nt = s // t

    def body(cur_ref, prev_ref, wp_ref, sc_ref, pooled_ref, ya_ref):
        g = pl.program_id(0)
        ti = pl.program_id(1)
        win = jnp.left_shift(2, g)
        row = lax.broadcasted_iota(jnp.int32, (t, t), 0)
        col = lax.broadcasted_iota(jnp.int32, (t, t), 1)
        lag = row - col
        band_cur = ((lag >= 0) & (lag < win)).astype(BF16)
        band_prev = ((lag + t < win) & (ti > 0)).astype(BF16)
        u = cur_ref[...]
        u_hi, u_lo = _split_bf16(u)
        p_hi, p_lo = _split_bf16(prev_ref[...])
        wsum = (_dot(band_cur, u_hi) + _dot(band_cur, u_lo)
                + _dot(band_prev, p_hi) + _dot(band_prev, p_lo))
        tok = ti * t + lax.broadcasted_iota(jnp.int32, (t, 1), 0)
        count = jnp.minimum(tok + 1, win).astype(F32)
        pooled = (wsum / count - u).astype(BF16)
        pooled_ref[...] = pooled
        ya_ref[...] = (_dot(pooled, wp_ref[...]) * sc_ref[...]).astype(BF16)

    blk = pl.BlockSpec((t, cg), lambda g, i: (i, g))
    return pl.pallas_call(
        _with_deps(body, 4, deps), name="pool_fwd", grid=(g_n, nt),
        out_shape=(_sds((s, g_n * cg), BF16), _sds((s, g_n * cg), BF16)),
        in_specs=[blk, pl.BlockSpec((t, cg), lambda g, i: (jnp.maximum(i - 1, 0), g)),
                  pl.BlockSpec((None, cg, cg), lambda g, i: (g, 0, 0)),
                  pl.BlockSpec((1, cg), lambda g, i: (0, g))] + [_ANY] * len(deps),
        out_specs=(blk, blk), compiler_params=_params("parallel", "parallel"),
    )(proj, proj, w_pool, pool_scale, *deps)


def _pool_backward(dya, pooled, w_pool, pool_scale, dproj):
    s = dya.shape[0]
    g_n, cg, _ = w_pool.shape
    t = POOL_TILE
    nt = s // t

    def body(dya_ref, dya_next_ref, pooled_ref, wp_ref, sc_ref, dproj_in, du_ref, gw_ref, gs_ref):
        del dproj_in
        g = pl.program_id(0)
        ti = pl.program_id(1)

        @pl.when(ti == 0)
        def _():
            gw_ref[...] = jnp.zeros_like(gw_ref)
            gs_ref[...] = jnp.zeros_like(gs_ref)

        win = jnp.left_shift(2, g)
        wp = wp_ref[...]
        sc = sc_ref[...]
        pooled_v = pooled_ref[...]
        dya_v = dya_ref[...].astype(F32)
        mixed = _dot(pooled_v, wp)
        gs_ref[...] += jnp.sum(dya_v * mixed, axis=0, keepdims=True)
        dmixed = (dya_v * sc).astype(BF16)
        gw_ref[...] += _dot(pooled_v, dmixed, "tn")
        dpooled = _dot(dmixed, wp, "nt")
        dmixed_next = (dya_next_ref[...].astype(F32) * sc).astype(BF16)
        dpooled_next = _dot(dmixed_next, wp, "nt")
        tok = ti * t + lax.broadcasted_iota(jnp.int32, (t, 1), 0)
        e_cur = dpooled / jnp.minimum(tok + 1, win).astype(F32)
        e_next = dpooled_next / jnp.minimum(tok + t + 1, win).astype(F32)
        row = lax.broadcasted_iota(jnp.int32, (t, t), 0)
        col = lax.broadcasted_iota(jnp.int32, (t, t), 1)
        lead = col - row
        band_cur = ((lead >= 0) & (lead < win)).astype(BF16)
        band_next = ((lead + t < win) & (ti < nt - 1)).astype(BF16)
        c_hi, c_lo = _split_bf16(e_cur)
        n_hi, n_lo = _split_bf16(e_next)
        du = (_dot(band_cur, c_hi) + _dot(band_cur, c_lo)
              + _dot(band_next, n_hi) + _dot(band_next, n_lo)) - dpooled
        du_ref[...] = du.astype(BF16)

    blk = pl.BlockSpec((t, cg), lambda g, i: (i, g))
    du, gw, gs = pl.pallas_call(
        body, name="pool_bwd", grid=(g_n, nt),
        out_shape=(_sds(dproj.shape, BF16), _sds((g_n, cg, cg), F32), _sds((1, g_n * cg), F32)),
        in_specs=[blk, pl.BlockSpec((t, cg), lambda g, i: (jnp.minimum(i + 1, nt - 1), g)), blk,
                  pl.BlockSpec((None, cg, cg), lambda g, i: (g, 0, 0)),
                  pl.BlockSpec((1, cg), lambda g, i: (0, g)),
                  pl.BlockSpec(memory_space=pl.ANY)],
        out_specs=(blk, pl.BlockSpec((None, cg, cg), lambda g, i: (g, 0, 0)),
                   pl.BlockSpec((1, cg), lambda g, i: (0, g))),
        input_output_aliases={5: 0}, compiler_params=_params("parallel", "arbitrary"),
    )(dya, dya, pooled, w_pool, pool_scale, dproj)
    return du, gw, gs


def _qkv_prepare(proj, q_norm_w, k_norm_w, width, deps=()):
    s = proj.shape[0]
    tm = _tile(s, 256)
    heads = width // HEAD_DIM

    def body(q_ref, k_ref, v_ref, qw_ref, kw_ref, qn_ref, kn_ref, vb_ref):
        for h in range(heads):
            cols = slice(h * HEAD_DIM, (h + 1) * HEAD_DIM)
            for src, w_ref, dst in ((q_ref, qw_ref, qn_ref), (k_ref, kw_ref, kn_ref)):
                v = src[:, cols]
                r = lax.rsqrt(jnp.mean(v * v, axis=-1, keepdims=True) + NORM_EPS)
                dst[:, cols] = (v * r * w_ref[...]).astype(BF16)
        vb_ref[...] = v_ref[...].astype(BF16)

    vec = pl.BlockSpec((1, HEAD_DIM), lambda i: (0, 0))
    out_spec = pl.BlockSpec((tm, width), lambda i: (i, 0))
    return pl.pallas_call(
        _with_deps(body, 5, deps), name="qkv_prep", grid=(s // tm,),
        out_shape=(_sds((s, width), BF16),) * 3,
        in_specs=[pl.BlockSpec((tm, width), lambda i: (i, 1)), pl.BlockSpec((tm, width), lambda i: (i, 2)),
                  pl.BlockSpec((tm, width), lambda i: (i, 3)), vec, vec] + [_ANY] * len(deps),
        out_specs=(out_spec,) * 3, compiler_params=_params("parallel"),
    )(proj, proj, proj, q_norm_w, k_norm_w, *deps)


def _qk_norm_backward(name, dn, proj, col_block, norm_w, dproj, width, deps=()):
    s = proj.shape[0]
    tm = _tile(s, 256)
    heads = width // HEAD_DIM

    def body(dn_ref, q_ref, w_ref, dproj_in, dq_ref, gw_ref):
        del dproj_in

        @pl.when(pl.program_id(0) == 0)
        def _():
            gw_ref[...] = jnp.zeros_like(gw_ref)

        wv = w_ref[...]
        gw = jnp.zeros((1, HEAD_DIM), F32)
        for h in range(heads):
            cols = slice(h * HEAD_DIM, (h + 1) * HEAD_DIM)
            v = q_ref[:, cols]
            g = dn_ref[:, cols]
            r = lax.rsqrt(jnp.mean(v * v, axis=-1, keepdims=True) + NORM_EPS)
            n = v * r
            gw = gw + jnp.sum(g * n, axis=0, keepdims=True)
            gn = g * wv
            dq_ref[:, cols] = (r * (gn - n * jnp.mean(gn * n, axis=-1, keepdims=True))).astype(BF16)
        gw_ref[...] += gw

    blk = pl.BlockSpec((tm, width), lambda i: (i, col_block))
    return pl.pallas_call(
        _with_deps(body, 4, deps), name=name, grid=(s // tm,),
        out_shape=(_sds(dproj.shape, BF16), _sds((1, HEAD_DIM), F32)),
        in_specs=[pl.BlockSpec((tm, width), lambda i: (i, 0)), blk,
                  pl.BlockSpec((1, HEAD_DIM), lambda i: (0, 0)), pl.BlockSpec(memory_space=pl.ANY)]
        + [_ANY] * len(deps),
        out_specs=(blk, pl.BlockSpec((1, HEAD_DIM), lambda i: (0, 0))),
        input_output_aliases={3: 0}, compiler_params=_params("arbitrary"),
    )(dn, proj, norm_w, dproj, *deps)


def _strict_upper(n):
    row = lax.broadcasted_iota(jnp.int32, (n, n), 0)
    col = lax.broadcasted_iota(jnp.int32, (n, n), 1)
    return (row > col).astype(BF16)


def _strict_lower(n):
    row = lax.broadcasted_iota(jnp.int32, (n, n), 0)
    col = lax.broadcasted_iota(jnp.int32, (n, n), 1)
    return (row < col).astype(BF16)


def _cumulate(v, tri):
    hi, lo = _split_bf16(v)
    return _dot(hi, tri) + _dot(lo, tri)


def _log_sigmoid(z):
    return jnp.minimum(z, 0.0) - jnp.log(1.0 + jnp.exp(-jnp.abs(z)))


def _attention_forward(qn, kn, vb):
    s, width = qn.shape
    heads = width // HEAD_DIM
    tq, tk = Q_TILE, K_TILE
    assert tq == tk and s % tq == 0
    scale = 1.0 / math.sqrt(HEAD_DIM)

    def body(q_ref, k_ref, v_ref, o_ref):
        qi = pl.program_id(1)
        q = q_ref[...]
        upper = _strict_upper(tk)
        causal = lax.broadcasted_iota(jnp.int32, (tq, tk), 1) < lax.broadcasted_iota(jnp.int32, (tq, tk), 0)

        def chunk(kb, carry, masked):
            acc, later = carry
            rows = pl.ds(pl.multiple_of(kb * tk, tk), tk)
            z = _dot(q, k_ref[rows, :], "nt") * scale
            log_beta = _log_sigmoid(z)
            l = log_beta - z
            if masked:
                l = jnp.where(causal, l, 0.0)
            w = log_beta + _cumulate(l, upper) + later
            a = jnp.exp(w)
            if masked:
                a = jnp.where(causal, a, 0.0)
            acc = acc + _dot(a.astype(BF16), v_ref[rows, :])
            return acc, later + jnp.sum(l, axis=1, keepdims=True)

        carry = (jnp.zeros((tq, HEAD_DIM), F32), jnp.zeros((tq, 1), F32))
        carry = chunk(qi, carry, True)
        acc, _ = lax.fori_loop(0, qi, lambda i, c: chunk(qi - 1 - i, c, False), carry)
        o_ref[...] = acc.astype(BF16)

    full = pl.BlockSpec((s, HEAD_DIM), lambda h, i: (0, h))
    blk = pl.BlockSpec((tq, HEAD_DIM), lambda h, i: (i, h))
    return pl.pallas_call(
        body, name="attn_fwd", grid=(heads, s // tq), out_shape=_sds((s, width), BF16),
        in_specs=[blk, full, full], out_specs=blk, compiler_params=_params("parallel", "parallel"),
    )(qn, kn, vb)


def _attention_backward(qn, kn, vb, dout, dproj, v_col_block, deps=()):
    s, width = qn.shape
    heads = width // HEAD_DIM
    tq, tk = Q_TILE, K_TILE
    nq = s // tq
    scale = 1.0 / math.sqrt(HEAD_DIM)
    v_blocks_per_head = v_col_block * heads

    def body(q_ref, k_ref, v_ref, do_ref, dproj_in, dq_ref, dk_ref, dv_ref, a_scr, dk_scr, dv_scr):
        del dproj_in
        qi = pl.program_id(1)

        @pl.when(qi == 0)
        def _():
            dk_scr[...] = jnp.zeros_like(dk_scr)
            dv_scr[...] = jnp.zeros_like(dv_scr)

        q = q_ref[...]
        dout_v = do_ref[...]
        upper = _strict_upper(tk)
        lower = _strict_lower(tk)
        causal = lax.broadcasted_iota(jnp.int32, (tq, tk), 1) < lax.broadcasted_iota(jnp.int32, (tq, tk), 0)

        def weights(kb, later, masked):
            rows = pl.ds(pl.multiple_of(kb * tk, tk), tk)
            z = _dot(q, k_ref[rows, :], "nt") * scale
            log_beta = _log_sigmoid(z)
            l = log_beta - z
            if masked:
                l = jnp.where(causal, l, 0.0)
            a = jnp.exp(log_beta + _cumulate(l, upper) + later)
            if masked:
                a = jnp.where(causal, a, 0.0)
            a_scr[:, rows] = a
            return later + jnp.sum(l, axis=1, keepdims=True)

        later = weights(qi, jnp.zeros((tq, 1), F32), True)
        lax.fori_loop(0, qi, lambda i, c: weights(qi - 1 - i, c, False), later)

        def grads(kb, carry, masked):
            dq, before = carry
            rows = pl.ds(pl.multiple_of(kb * tk, tk), tk)
            k_blk = k_ref[rows, :]
            z = _dot(q, k_blk, "nt") * scale
            beta = jax.nn.sigmoid(z)
            a = a_scr[:, rows]
            g = a * _dot(dout_v, v_ref[rows, :], "nt")
            p = _cumulate(g, lower) + before
            dz = g * (1.0 - beta) - p * beta
            if masked:
                dz = jnp.where(causal, dz, 0.0)
            dz = (dz * scale).astype(BF16)
            dk_scr[rows, :] += _dot(dz, q, "tn")
            dv_scr[rows, :] += _dot(a.astype(BF16), dout_v, "tn")
            return dq + _dot(dz, k_blk), before + jnp.sum(g, axis=1, keepdims=True)

        carry = (jnp.zeros((tq, HEAD_DIM), F32), jnp.zeros((tq, 1), F32))
        carry = lax.fori_loop(0, qi, lambda i, c: grads(i, c, False), carry)
        dq, _ = grads(qi, carry, True)
        dq_ref[...] = dq

        @pl.when(qi == nq - 1)
        def _():
            dk_ref[...] = dk_scr[...]
            dv_ref[...] = dv_scr[...].astype(BF16)

    full = pl.BlockSpec((s, HEAD_DIM), lambda h, i: (0, h))
    blk = pl.BlockSpec((tq, HEAD_DIM), lambda h, i: (i, h))
    return pl.pallas_call(
        _with_deps(body, 5, deps), name="attn_bwd", grid=(heads, nq),
        out_shape=(_sds((s, width), F32), _sds((s, width), F32), _sds(dproj.shape, BF16)),
        in_specs=[blk, full, full, blk, pl.BlockSpec(memory_space=pl.ANY)] + [_ANY] * len(deps),
        out_specs=(blk, full, pl.BlockSpec((s, HEAD_DIM), lambda h, i: (0, v_blocks_per_head + h))),
        scratch_shapes=[pltpu.VMEM((tq, s), F32), pltpu.VMEM((s, HEAD_DIM), F32), pltpu.VMEM((s, HEAD_DIM), F32)],
        input_output_aliases={4: 2}, compiler_params=_params("parallel", "arbitrary"),
    )(qn, kn, vb, dout, dproj, *deps)


def _place_columns(name, src, dst, col_block):
    s, w = src.shape
    tm = _tile(s, 512)

    def body(src_ref, dst_in, out_ref):
        del dst_in
        out_ref[...] = src_ref[...]

    return pl.pallas_call(
        body, name=name, grid=(s // tm,), out_shape=_sds(dst.shape, dst.dtype),
        in_specs=[pl.BlockSpec((tm, w), lambda i: (i, 0)), pl.BlockSpec(memory_space=pl.ANY)],
        out_specs=pl.BlockSpec((tm, w), lambda i: (i, col_block)),
        input_output_aliases={1: 0}, compiler_params=_params("parallel"),
    )(src, dst)


def _cast_into_slot(name, x, slot):
    r, c = x.shape
    tr = _tile(r, max(SUBLANES * 2, (1 << 20) // c), SUBLANES * 2)

    def body(slot_ref, x_ref, o_ref):
        del slot_ref
        o_ref[...] = x_ref[...].astype(BF16)

    grid_spec = pltpu.PrefetchScalarGridSpec(
        num_scalar_prefetch=1, grid=(r // tr,),
        in_specs=[pl.BlockSpec((tr, c), lambda i, slot_ref: (i, 0))],
        out_specs=pl.BlockSpec((None, tr, c), lambda i, slot_ref: (slot_ref[0], i, 0)))
    return pl.pallas_call(
        body, name=name, grid_spec=grid_spec, out_shape=_sds((N_DEV, r, c), BF16),
        compiler_params=_params("parallel"),
    )(slot, x)


def _adamw(name, w, g, m, v):
    r, c = w.shape
    tr = _tile(r, max(SUBLANES, (1 << 19) // c))
    c1 = 1.0 - ADAM_B1 ** ADAM_STEP
    c2 = 1.0 - ADAM_B2 ** ADAM_STEP

    def body(w_ref, g_ref, m_ref, v_ref, d_ref, nm_ref, nv_ref):
        gv = g_ref[...]
        nm = ADAM_B1 * m_ref[...] + (1.0 - ADAM_B1) * gv
        nv = ADAM_B2 * v_ref[...] + (1.0 - ADAM_B2) * (gv * gv)
        d_ref[...] = -ADAM_LR * ((nm / c1) / (jnp.sqrt(nv / c2) + ADAM_EPS) + ADAM_WD * w_ref[...])
        nm_ref[...] = nm
        nv_ref[...] = nv

    blk = pl.BlockSpec((tr, c), lambda i: (i, 0))
    return pl.pallas_call(
        body, name=name, grid=(r // tr,), out_shape=(_sds((r, c), F32),) * 3,
        in_specs=[blk] * 4, out_specs=(blk,) * 3, compiler_params=_params("parallel"),
    )(w, g, m, v)


def _rows_of_lanes(v):
    rows = v.shape[1] // LANES
    out = v.reshape(rows, LANES)
    pad = (-rows) % SUBLANES
    if pad:
        out = jnp.pad(out, ((0, pad), (0, 0)))
    return out


def kernel(x, c, w_ada, b_ada, norm1_w, w_in, q_norm_w, k_norm_w, w_pool, pool_scale, w_a_up, w_b_up, w_o, norm2_w, w_ff1, w_ff2, loss_target, m_w_ada, m_b_ada, m_norm1_w, m_w_in, m_q_norm_w, m_k_norm_w, m_w_pool, m_pool_scale, m_w_a_up, m_w_b_up, m_w_o, m_norm2_w, m_w_ff1, m_w_ff2, v_w_ada, v_b_ada, v_norm1_w, v_w_in, v_q_norm_w, v_k_norm_w, v_w_pool, v_pool_scale, v_w_a_up, v_w_b_up, v_w_o, v_norm2_w, v_w_ff1, v_w_ff2):
    _, s, d = x.shape
    half = d // 2
    d8 = d // N_DEV
    n_groups = len(POOL_WINDOWS)
    cg = half // n_groups
    me = _group_index(MESH_AXES)

    x2 = x[0]
    target = loss_target[0]

    my_slot = jnp.reshape(me, (1,)).astype(jnp.int32)

    def cast(i, t):
        return _cast_into_slot("cast_w%d" % i, t, my_slot)

    def gather_start(tag, bufs, phase):
        if phase < 2:
            return _launch("ag%s_ici%d" % (tag, phase), bufs, _plan_gather_ici(phase), (2, 4)[phase] * len(bufs))
        return _launch("ag%s_d2d" % tag, bufs, _plan_gather_d2d, len(_CHIP_MASKS) * len(bufs))

    buf_a = [cast(0, w_in[0])]
    fl_a, tok = gather_start("A", buf_a, 0)

    c_all = _all_gather_2d("ag_c", c.reshape(d // LANES, LANES), deps=(tok,)).reshape(N_DEV, d)
    wa = w_ada.shape[2]
    b_shard = lax.dynamic_slice_in_dim(b_ada, me * wa, wa, axis=1)
    mod_part = _ada_forward(c_all, w_ada[0], b_shard)
    mod_all = _all_gather_2d("ag_mod", mod_part.reshape(N_DEV * wa // LANES, LANES))
    mod_all = mod_all.reshape(N_DEV, N_DEV, wa)
    mod = lax.dynamic_slice_in_dim(mod_all, me, 1, axis=1).reshape(1, N_MOD * d)
    shift1, scale1, gate1, shift2, scale2, gate2 = [mod[:, i * d:(i + 1) * d] for i in range(N_MOD)]
    buf_b = [cast(1, w_pool[0].reshape(-1, cg)), cast(2, w_a_up[0]), cast(3, w_b_up[0]), cast(4, w_o[0])]
    buf_c = [cast(5, w_ff1[0]), cast(6, w_ff2[0])]

    buf_a = _land(fl_a, mod)
    fl_a, tok = gather_start("A", buf_a, 1)
    h = _norm_forward("norm1_fwd", x2, norm1_w, scale1, shift1, deps=(tok,))
    buf_a = _land(fl_a, h)
    fl_a, tok = gather_start("A", buf_a, 2)
    w_in_f, = _land(fl_a, tok)

    tm = _tile(s, 512)
    tk = _tile(d, 1024)

    fl_b, tok_b = gather_start("B", buf_b, 0)
    fl_c, tok_c = gather_start("C", buf_c, 0)
    proj = _matmul(
        "proj", "nn", (s // tm, N_DEV, d // tk), h, pl.BlockSpec((tm, tk), lambda i, j, k: (i, k)),
        w_in_f, pl.BlockSpec((None, tk, half), lambda i, j, k: (j, k, 0)),
        [_sds((s, 4 * d), F32)], [pl.BlockSpec((tm, half), lambda i, j, k: (i, j))], (tm, half),
        deps=(tok_b, tok_c))[0]
    buf_b = _land(fl_b, proj)
    buf_c = _land(fl_c, proj)
    fl_b, tok_b = gather_start("B", buf_b, 1)
    fl_c, tok_c = gather_start("C", buf_c, 1)

    qn, kn, vb = _qkv_prepare(proj, q_norm_w, k_norm_w, half, deps=(tok_b, tok_c))
    attn = _attention_forward(qn, kn, vb)
    buf_b = _land(fl_b, attn)
    buf_c = _land(fl_c, attn)
    fl_b, tok_b = gather_start("B", buf_b, 2)
    fl_c, tok_c = gather_start("C", buf_c, 2)
    w_pool_f, w_a_f, w_b_f, w_o_f = _land(fl_b, tok_b)
    rows_pool = cg // N_DEV
    w_pool_f = w_pool_f.reshape(N_DEV, n_groups, rows_pool, cg).transpose(1, 0, 2, 3).reshape(n_groups, cg, cg)
    w_o_f = w_o_f.reshape(d, d)
    pooled, ya_in = _pool_forward(proj, w_pool_f, pool_scale, deps=(tok_c,))

    def merge_epilogue(ga_ref, gb_ref, ya, yb, out_refs):
        merged_ref, ya_ref, yb_ref = out_refs
        merged = jax.nn.sigmoid(ga_ref[...]) * ya + jax.nn.sigmoid(gb_ref[...]) * yb
        merged_ref[...] = merged.astype(BF16)
        ya_ref[...] = ya.astype(BF16)
        yb_ref[...] = yb.astype(BF16)

    def up_body(a1_ref, b1_ref, a2_ref, b2_ref, ga_ref, gb_ref, *out_refs):
        merge_epilogue(ga_ref, gb_ref, _dot(a1_ref[...], b1_ref[...]), _dot(a2_ref[...], b2_ref[...]), out_refs)

    ga_blk0 = 2 * d // d8
    gb_blk0 = 3 * d // d8
    a_spec = pl.BlockSpec((tm, half), lambda i, j: (i, 0))
    wup_spec = pl.BlockSpec((None, half, d8), lambda i, j: (j, 0, 0))
    o_blk = pl.BlockSpec((tm, d8), lambda i, j: (i, j))
    merged, y_a, y_b = pl.pallas_call(
        up_body, name="up_merge", grid=(s // tm, N_DEV), out_shape=(_sds((s, d), BF16),) * 3,
        in_specs=[a_spec, wup_spec, a_spec, wup_spec,
                  pl.BlockSpec((tm, d8), lambda i, j: (i, ga_blk0 + j)),
                  pl.BlockSpec((tm, d8), lambda i, j: (i, gb_blk0 + j))],
        out_specs=(o_blk,) * 3, compiler_params=_params("parallel", "parallel"),
    )(ya_in, w_a_f, attn, w_b_f, proj, proj)

    tn = _tile(d, 1024)

    def oproj_epilogue(acc, extra_refs, out_refs):
        x_ref, g_ref = extra_refs
        x1_ref, o_ref = out_refs
        x1_ref[...] = x_ref[...] + g_ref[...] * acc
        o_ref[...] = acc.astype(BF16)

    mn_blk = pl.BlockSpec((tm, tn), lambda i, j, k: (i, j))
    n_vec = pl.BlockSpec((1, tn), lambda i, j, k: (0, j))
    x1, o_act = _matmul(
        "oproj", "nn", (s // tm, d // tn, d // tk), merged, pl.BlockSpec((tm, tk), lambda i, j, k: (i, k)),
        w_o_f, pl.BlockSpec((tk, tn), lambda i, j, k: (k, j)),
        [_sds((s, d), F32), _sds((s, d), BF16)], [mn_blk, mn_blk], (tm, tn),
        epilogue=oproj_epilogue, extras=(x2, gate1), extra_specs=(mn_blk, n_vec))

    h2 = _norm_forward("norm2_fwd", x1, norm2_w, scale2, shift2)
    w_ff1_f, w_ff2_f = _land(fl_c, h2)
    w_ff2_f = w_ff2_f.reshape(4 * d, d)

    def ff1_epilogue(acc, extra_refs, out_refs):
        r = jnp.maximum(acc, 0.0)
        out_refs[0][...] = r.astype(BF16)
        out_refs[1][...] = (r * r).astype(BF16)

    ff_blk = pl.BlockSpec((tm, half), lambda i, j, k: (i, j))
    relu, act = _matmul(
        "ff1", "nn", (s // tm, N_DEV, d // tk), h2, pl.BlockSpec((tm, tk), lambda i, j, k: (i, k)),
        w_ff1_f, pl.BlockSpec((None, tk, half), lambda i, j, k: (j, k, 0)),
        [_sds((s, 4 * d), BF16)] * 2, [ff_blk, ff_blk], (tm, half), epilogue=ff1_epilogue)

    def ff2_epilogue(acc, extra_refs, out_refs):
        x1_ref, g_ref, t_ref = extra_refs
        f_ref, dy_ref, sq_ref = out_refs
        err = x1_ref[...] + g_ref[...] * acc - t_ref[...]
        f_ref[...] = acc.astype(BF16)
        dy_ref[...] = err * (1.0 / d)
        sq_ref[...] = jnp.full(sq_ref.shape, jnp.sum(err * err), F32)

    f_act, dy, sq = _matmul(
        "ff2", "nn", (s // tm, d // tn, 4 * d // tk), act, pl.BlockSpec((tm, tk), lambda i, j, k: (i, k)),
        w_ff2_f, pl.BlockSpec((tk, tn), lambda i, j, k: (k, j)),
        [_sds((s, d), BF16), _sds((s, d), F32), _sds((s // tm * SUBLANES, d // tn * LANES), F32)],
        [mn_blk, mn_blk, pl.BlockSpec((SUBLANES, LANES), lambda i, j, k: (i, j))], (tm, tn),
        epilogue=ff2_epilogue, extras=(x1, gate2, target), extra_specs=(mn_blk, n_vec, mn_blk))
    loss_local = (0.5 / d) * jnp.sum(sq[::SUBLANES, ::LANES])
    loss = lax.psum(loss_local, MESH_AXES)

    df, dgate2 = _gate_backward("gate2_bwd", dy, gate2, f_act)
    tok_k = _tile(s, 1024)
    g_ff2 = _matmul(
        "g_ff2", "tn", (4 * d // tk, d // tn, s // tok_k), act, pl.BlockSpec((tok_k, tk), lambda i, j, k: (k, i)),
        df, pl.BlockSpec((tok_k, tn), lambda i, j, k: (k, j)),
        [_sds((4 * d, d), BF16)], [pl.BlockSpec((tk, tn), lambda i, j, k: (i, j))], (tk, tn))[0]

    def da_epilogue(acc, extra_refs, out_refs):
        out_refs[0][...] = (acc * (2.0 * extra_refs[0][...].astype(F32))).astype(BF16)

    big_blk = pl.BlockSpec((tm, tn), lambda i, j, k: (i, j))
    fl_f2, tok = _reduce_scatter_start("F2", [g_ff2.reshape(N_DEV, half, d)])
    df1 = _matmul(
        "da_ff", "nt", (s // tm, 4 * d // tn, d // tk), df, pl.BlockSpec((tm, tk), lambda i, j, k: (i, k)),
        w_ff2_f, pl.BlockSpec((tn, tk), lambda i, j, k: (j, k)),
        [_sds((s, 4 * d), BF16)], [big_blk], (tm, tn),
        epilogue=da_epilogue, extras=(relu,), extra_specs=(big_blk,), deps=(tok,))[0]

    fl_f2, tok = _reduce_scatter_middle("F2", fl_f2, df1, me)
    g_ff1 = _matmul(
        "g_ff1", "tn", (d // tk, N_DEV, s // tok_k), h2, pl.BlockSpec((tok_k, tk), lambda i, j, k: (k, i)),
        df1, pl.BlockSpec((tok_k, half), lambda i, j, k: (k, j)),
        [_sds((N_DEV, d, half), BF16)], [pl.BlockSpec((None, tk, half), lambda i, j, k: (j, i, 0))], (tk, half),
        deps=(tok,))[0]

    fl_f1, tok = _reduce_scatter_start("F1", [g_ff1])
    dh2 = _matmul(
        "dh2", "nt", (s // tm, d // tn, N_DEV), df1, pl.BlockSpec((tm, half), lambda i, j, k: (i, k)),
        w_ff1_f, pl.BlockSpec((None, tn, half), lambda i, j, k: (k, j, 0)),
        [_sds((s, d), F32)], [mn_blk], (tm, tn), deps=(tok,))[0]

    g_w_ff2, = _reduce_scatter_finish("F2", fl_f2, dh2)
    fl_f1, tok = _reduce_scatter_middle("F1", fl_f1, dh2, me)
    dx1, dshift2, dscale2, g_norm2 = _norm_backward("norm2_bwd", dh2, x1, norm2_w, scale2, dy, deps=(tok,))

    do, dgate1 = _gate_backward("gate1_bwd", dx1, gate1, o_act)
    g_o = _matmul(
        "g_o", "tn", (d // tk, d // tn, s // tok_k), merged, pl.BlockSpec((tok_k, tk), lambda i, j, k: (k, i)),
        do, pl.BlockSpec((tok_k, tn), lambda i, j, k: (k, j)),
        [_sds((d, d), BF16)], [pl.BlockSpec((tk, tn), lambda i, j, k: (i, j))], (tk, tn))[0]

    def merge_bwd_epilogue(acc, extra_refs, out_refs):
        ga_ref, gb_ref, ya_ref, yb_ref = extra_refs
        dya_ref, dyb_ref, dga_ref, dgb_ref = out_refs
        sa = jax.nn.sigmoid(ga_ref[...])
        sb = jax.nn.sigmoid(gb_ref[...])
        dya_ref[...] = (acc * sa).astype(BF16)
        dyb_ref[...] = (acc * sb).astype(BF16)
        dga_ref[...] = (acc * ya_ref[...].astype(F32) * (sa * (1.0 - sa))).astype(BF16)
        dgb_ref[...] = (acc * yb_ref[...].astype(F32) * (sb * (1.0 - sb))).astype(BF16)

    nb = d // tn
    dy_a, dy_b, dproj, dg_b = _matmul(
        "dmerged", "nt", (s // tm, nb, d // tk), do, pl.BlockSpec((tm, tk), lambda i, j, k: (i, k)),
        w_o_f, pl.BlockSpec((tn, tk), lambda i, j, k: (j, k)),
        [_sds((s, d), BF16), _sds((s, d), BF16), _sds((s, 4 * d), BF16), _sds((s, d), BF16)],
        [mn_blk, mn_blk, pl.BlockSpec((tm, tn), lambda i, j, k: (i, 2 * nb + j)), mn_blk], (tm, tn),
        epilogue=merge_bwd_epilogue, extras=(proj, proj, y_a, y_b),
        extra_specs=(pl.BlockSpec((tm, tn), lambda i, j, k: (i, 2 * nb + j)),
                     pl.BlockSpec((tm, tn), lambda i, j, k: (i, 3 * nb + j)), mn_blk, mn_blk))
    dproj = _place_columns("place_dgb", dg_b, dproj, 3)

    up_a = pl.BlockSpec((tok_k, half), lambda i, j, k: (k, 0))
    up_b = pl.BlockSpec((tok_k, d8), lambda i, j, k: (k, j))
    up_o = pl.BlockSpec((None, half, d8), lambda i, j, k: (j, 0, 0))
    g_a_up = _matmul("g_a_up", "tn", (1, N_DEV, s // tok_k), ya_in, up_a, dy_a, up_b,
                     [_sds((N_DEV, half, d8), BF16)], [up_o], (half, d8))[0]
    g_b_up = _matmul("g_b_up", "tn", (1, N_DEV, s // tok_k), attn, up_a, dy_b, up_b,
                     [_sds((N_DEV, half, d8), BF16)], [up_o], (half, d8))[0]
    dn_a = pl.BlockSpec((tm, d8), lambda i, j, k: (i, k))
    dn_b = pl.BlockSpec((None, half, d8), lambda i, j, k: (k, 0, 0))
    dn_o = pl.BlockSpec((tm, half), lambda i, j, k: (i, 0))
    dya_in = _matmul("d_ya_in", "nt", (s // tm, 1, N_DEV), dy_a, dn_a, w_a_f, dn_b,
                     [_sds((s, half), BF16)], [dn_o], (tm, half))[0]
    dattn = _matmul("d_attn", "nt", (s // tm, 1, N_DEV), dy_b, dn_a, w_b_f, dn_b,
                    [_sds((s, half), BF16)], [dn_o], (tm, half))[0]

    dproj, g_pool, g_pool_scale = _pool_backward(dya_in, pooled, w_pool_f, pool_scale, dproj)
    g_w_ff1, = _reduce_scatter_finish("F1", fl_f1, g_pool)
    g_pool_send = g_pool.astype(BF16).reshape(n_groups, N_DEV, rows_pool, cg).transpose(1, 0, 2, 3)
    g_pool_send = g_pool_send.reshape(N_DEV, n_groups * rows_pool, cg)
    fl_b, tok = _reduce_scatter_start("B", [g_pool_send, g_a_up, g_b_up, g_o.reshape(N_DEV, d8, d)])
    dqn, dkn, dproj = _attention_backward(qn, kn, vb, dattn, dproj, 3, deps=(tok,))
    fl_b, tok = _reduce_scatter_middle("B", fl_b, dqn, me)
    dproj, g_qnorm = _qk_norm_backward("qnorm_bwd", dqn, proj, 1, q_norm_w, dproj, half, deps=(tok,))
    dproj, g_knorm = _qk_norm_backward("knorm_bwd", dkn, proj, 2, k_norm_w, dproj, half)

    g_in = _matmul(
        "g_in", "tn", (d // tk, N_DEV, s // tok_k), h, pl.BlockSpec((tok_k, tk), lambda i, j, k: (k, i)),
        dproj, pl.BlockSpec((tok_k, half), lambda i, j, k: (k, j)),
        [_sds((N_DEV, d, half), BF16)], [pl.BlockSpec((None, tk, half), lambda i, j, k: (j, i, 0))], (tk, half))[0]
    fl_in, tok = _reduce_scatter_start("I", [g_in])
    dh = _matmul(
        "dh", "nt", (s // tm, d // tn, N_DEV), dproj, pl.BlockSpec((tm, half), lambda i, j, k: (i, k)),
        w_in_f, pl.BlockSpec((None, tn, half), lambda i, j, k: (k, j, 0)),
        [_sds((s, d), F32)], [mn_blk], (tm, tn), deps=(tok,))[0]
    g_w_pool, g_w_a_up, g_w_b_up, g_w_o = _reduce_scatter_finish("B", fl_b, dh)
    fl_in, tok = _reduce_scatter_middle("I", fl_in, dh, me)
    grad_x, dshift1, dscale1, g_norm1 = _norm_backward("norm1_bwd", dh, x2, norm1_w, scale1, dx1, deps=(tok,))

    dmod = jnp.concatenate([dshift1, dscale1, dgate1, dshift2, dscale2, dgate2], axis=1)
    pieces = [dmod, g_norm1, g_norm2, g_pool_scale, g_qnorm, g_knorm]
    packed_rows = [_rows_of_lanes(p) for p in pieces]
    offsets = [0]
    for p in packed_rows:
        offsets.append(offsets[-1] + p.shape[0])
    packed = jnp.concatenate(packed_rows, axis=0)
    small_all = _all_gather_2d("ag_small", packed)
    small_sum = _sum_slots("small_sum", small_all[None], F32)[0]

    def unpack(i, width):
        return small_sum[offsets[i]:offsets[i] + width // LANES].reshape(1, width)

    g_b_ada = unpack(0, N_MOD * d)
    g_norm1_w = unpack(1, d)
    g_norm2_w = unpack(2, d)
    g_pool_scale_w = unpack(3, half)
    g_q_norm_w = unpack(4, HEAD_DIM)
    g_k_norm_w = unpack(5, HEAD_DIM)
    dmod_all = small_all[:, :N_MOD * d // LANES].reshape(N_DEV, N_MOD * d)
    dmod_cols = lax.dynamic_slice_in_dim(dmod_all, me * wa, wa, axis=1)
    g_w_ada = _ada_weight_grad(c_all, dmod_cols)[None]

    g_w_in, = _reduce_scatter_finish("I", fl_in, g_w_ada)

    grads = {
        "w_ada": g_w_ada, "b_ada": g_b_ada, "norm1_w": g_norm1_w, "w_in": g_w_in[None],
        "q_norm_w": g_q_norm_w, "k_norm_w": g_k_norm_w,
        "w_pool": g_w_pool.reshape(w_pool.shape), "pool_scale": g_pool_scale_w,
        "w_a_up": g_w_a_up[None], "w_b_up": g_w_b_up[None], "w_o": g_w_o[None],
        "norm2_w": g_norm2_w, "w_ff1": g_w_ff1[None], "w_ff2": g_w_ff2[None],
    }
    weights = {"w_ada": (w_ada, m_w_ada, v_w_ada), "b_ada": (b_ada, m_b_ada, v_b_ada),
               "norm1_w": (norm1_w, m_norm1_w, v_norm1_w), "w_in": (w_in, m_w_in, v_w_in),
               "q_norm_w": (q_norm_w, m_q_norm_w, v_q_norm_w), "k_norm_w": (k_norm_w, m_k_norm_w, v_k_norm_w),
               "w_pool": (w_pool, m_w_pool, v_w_pool), "pool_scale": (pool_scale, m_pool_scale, v_pool_scale),
               "w_a_up": (w_a_up, m_w_a_up, v_w_a_up), "w_b_up": (w_b_up, m_w_b_up, v_w_b_up),
               "w_o": (w_o, m_w_o, v_w_o), "norm2_w": (norm2_w, m_norm2_w, v_norm2_w),
               "w_ff1": (w_ff1, m_w_ff1, v_w_ff1), "w_ff2": (w_ff2, m_w_ff2, v_w_ff2)}
    order = list(weights)
    deltas, new_m, new_v = {}, {}, {}
    for name in order:
        wt, mt, vt = weights[name]
        shape = wt.shape
        flat = (-1, shape[-1])
        dl, nm, nv = _adamw("adamw_" + name, wt.reshape(flat), grads[name].reshape(flat),
                            mt.reshape(flat), vt.reshape(flat))
        deltas[name], new_m[name], new_v[name] = dl.reshape(shape), nm.reshape(shape), nv.reshape(shape)

    return (loss, grad_x[None], *[grads[n] for n in order], *[deltas[n] for n in order],
            *[new_m[n] for n in order], *[new_v[n] for n in order])
```

```python
import math

import jax
import jax.numpy as jnp
from jax import lax
from jax.experimental import pallas as pl
from jax.experimental.pallas import tpu as pltpu

F32 = jnp.float32
BF16 = jnp.bfloat16
MESH_AXES = ("x", "y", "c")
N_DEV = 8
HEAD_DIM = 128
POOL_WINDOWS = (2, 4, 8, 16)
N_MOD = 6
NORM_EPS = 1e-6
LANES = 128
SUBLANES = 8
VMEM_LIMIT_BYTES = 56 * 1024 * 1024
Q_TILE = 256
K_TILE = 256
POOL_TILE = 256
HEADS_PER_STEP = 2

ADAM_LR = 0.001
ADAM_B1 = 0.9
ADAM_B2 = 0.999
ADAM_EPS = 1e-08
ADAM_WD = 0.01
ADAM_STEP = 10

_NN = (((1,), (0,)), ((), ()))
_NT = (((1,), (1,)), ((), ()))
_TN = (((0,), (0,)), ((), ()))
_DIMS = {"nn": _NN, "nt": _NT, "tn": _TN}


def _dot(a, b, mode="nn"):
    return lax.dot_general(a, b, _DIMS[mode], preferred_element_type=F32)


def _params(*sem):
    return pltpu.CompilerParams(dimension_semantics=sem, vmem_limit_bytes=VMEM_LIMIT_BYTES)


def _tile(dim, pref, align=SUBLANES):
    for t in range(min(dim, pref), 0, -1):
        if dim % t == 0 and t % align == 0:
            return t
    return dim


def _group_index(axes):
    idx = 0
    for a in axes:
        idx = idx * 2 + lax.axis_index(a)
    return idx


def _peer_device(axes, k):
    coords = {a: lax.axis_index(a) for a in MESH_AXES}
    for pos, a in enumerate(axes):
        if (k >> (len(axes) - 1 - pos)) & 1:
            coords[a] = 1 - coords[a]
    return tuple(coords[a] for a in MESH_AXES)


_AXIS_BIT = {"x": 4, "y": 2, "c": 1}
_ANY = pl.BlockSpec(memory_space=pl.ANY)


def _device_xor(mask):
    return tuple(1 - lax.axis_index(a) if mask & _AXIS_BIT[a] else lax.axis_index(a) for a in MESH_AXES)


def _remote(src, dst, send_sem, recv_sem, mask):
    return pltpu.make_async_remote_copy(src_ref=src, dst_ref=dst, send_sem=send_sem, recv_sem=recv_sem,
                                        device_id=_device_xor(mask), device_id_type=pl.DeviceIdType.MESH)


def _start_all(copies):
    for cp in copies:
        cp.start()


def _wait_all(copies):
    for cp in copies:
        cp.wait()


def _gather_ici(name, bufs):
    na = len(bufs)
    bx, by = _AXIS_BIT["x"], _AXIS_BIT["y"]

    def body(*refs):
        out = refs[na:2 * na]
        send_sems, recv_sems = refs[2 * na:]
        me = _group_index(MESH_AXES)

        def copy(a, n, slot, color, mask):
            half = out[a].shape[1] // 2
            piece = out[a].at[slot, pl.ds(color * half, half)]
            return _remote(piece, piece, send_sems.at[a, n], recv_sems.at[a, n], mask)

        first, second = [], []
        for a in range(na):
            first += [copy(a, 0, me, 0, bx), copy(a, 1, me, 1, by)]
            second += [copy(a, 2, me, 0, by), copy(a, 3, me ^ bx, 0, by),
                       copy(a, 4, me, 1, bx), copy(a, 5, me ^ by, 1, bx)]
        _start_all(first)
        _wait_all(first)
        _start_all(second)
        _wait_all(second)

    return pl.pallas_call(
        body, name=name, out_shape=tuple(jax.ShapeDtypeStruct(b.shape, b.dtype) for b in bufs),
        in_specs=[_ANY] * na, out_specs=tuple([_ANY] * na),
        input_output_aliases={a: a for a in range(na)},
        scratch_shapes=[pltpu.SemaphoreType.DMA((na, 6)), pltpu.SemaphoreType.DMA((na, 6))],
    )(*bufs)


def _gather_d2d(name, bufs):
    na = len(bufs)
    masks = (0, _AXIS_BIT["y"], _AXIS_BIT["x"], _AXIS_BIT["x"] | _AXIS_BIT["y"])

    def body(*refs):
        out = refs[na:2 * na]
        send_sems, recv_sems = refs[2 * na:]
        me = _group_index(MESH_AXES)
        copies = []
        for a in range(na):
            for n, m in enumerate(masks):
                piece = out[a].at[me ^ m]
                copies.append(_remote(piece, piece, send_sems.at[a, n], recv_sems.at[a, n], _AXIS_BIT["c"]))
        _start_all(copies)
        _wait_all(copies)

    return pl.pallas_call(
        body, name=name, out_shape=tuple(jax.ShapeDtypeStruct(b.shape, b.dtype) for b in bufs),
        in_specs=[_ANY] * na, out_specs=tuple([_ANY] * na),
        input_output_aliases={a: a for a in range(na)},
        scratch_shapes=[pltpu.SemaphoreType.DMA((na, 4)), pltpu.SemaphoreType.DMA((na, 4))],
    )(*bufs)


def _scatter(name, srcs, send_slots, masks):
    na = len(srcs)
    nm = len(masks)

    def body(*refs):
        src, out = refs[:na], refs[na:2 * na]
        send_sems, recv_sems = refs[2 * na:]
        slots = send_slots(_group_index(MESH_AXES))
        copies = []
        for n, m in enumerate(masks):
            for a in range(na):
                copies.append(_remote(src[a].at[slots[n]], out[a].at[n],
                                      send_sems.at[a, n], recv_sems.at[a, n], m))
        _start_all(copies)
        _wait_all(copies)

    return pl.pallas_call(
        body, name=name,
        out_shape=tuple(jax.ShapeDtypeStruct((nm,) + s.shape[1:], s.dtype) for s in srcs),
        in_specs=[_ANY] * na, out_specs=tuple([_ANY] * na),
        scratch_shapes=[pltpu.SemaphoreType.DMA((na, nm)), pltpu.SemaphoreType.DMA((na, nm))],
    )(*srcs)


def _add_received(name, own, own_slots, received, out_dtype):
    nj, r, c = received.shape
    tr = _tile(r, max(2 * SUBLANES, (1 << 20) // c), 2 * SUBLANES)

    def body(slots_ref, own_ref, rec_ref, o_ref):
        del slots_ref
        o_ref[...] = (own_ref[...].astype(F32) + rec_ref[...].astype(F32)).astype(o_ref.dtype)

    grid_spec = pltpu.PrefetchScalarGridSpec(
        num_scalar_prefetch=1, grid=(nj, r // tr),
        in_specs=[pl.BlockSpec((None, tr, c), lambda j, i, slots: (slots[j], i, 0)),
                  pl.BlockSpec((None, tr, c), lambda j, i, slots: (j, i, 0))],
        out_specs=pl.BlockSpec((None, tr, c), lambda j, i, slots: (j, i, 0)))
    return pl.pallas_call(
        body, name=name, grid_spec=grid_spec, out_shape=jax.ShapeDtypeStruct((nj, r, c), out_dtype),
        compiler_params=_params("parallel", "parallel"),
    )(own_slots, own, received)


def _add_final(name, own, received):
    nj, r, c = received.shape
    tr = _tile(r, max(2 * SUBLANES, (1 << 19) // c), 2 * SUBLANES)

    def body(own_ref, rec_ref, o_ref):
        acc = own_ref[...].astype(F32)
        for j in range(nj):
            acc = acc + rec_ref[j].astype(F32)
        o_ref[...] = acc

    return pl.pallas_call(
        body, name=name, grid=(r // tr,), out_shape=jax.ShapeDtypeStruct((r, c), F32),
        in_specs=[pl.BlockSpec((None, tr, c), lambda i: (0, i, 0)),
                  pl.BlockSpec((nj, tr, c), lambda i: (0, i, 0))],
        out_specs=pl.BlockSpec((tr, c), lambda i: (i, 0)),
        compiler_params=_params("parallel"),
    )(own, received)


_HBM = pl.BlockSpec(memory_space=pltpu.HBM)
_SEM = pl.BlockSpec(memory_space=pltpu.SEMAPHORE)
_DATAFLOW = pltpu.SideEffectType.DATAFLOW_SIDE_EFFECTING


def _launch(name, bufs, plan, n_copies):
    nb = len(bufs)

    def body(*refs):
        ins = refs[:nb]
        send_sems, recv_sems = refs[nb], refs[nb + 1]
        token = refs[-1]
        copies = plan(ins, _group_index(MESH_AXES))
        assert len(copies) == n_copies
        for n, (src, dst, mask) in enumerate(copies):
            _remote(src, dst, send_sems.at[n], recv_sems.at[n], mask).start()
        token[...] = jnp.zeros_like(token)

    outs = pl.pallas_call(
        body, name=name,
        out_shape=(pltpu.SemaphoreType.DMA((n_copies,)), pltpu.SemaphoreType.DMA((n_copies,)),
                   *[pltpu.HBM(b.shape, b.dtype) for b in bufs], jax.ShapeDtypeStruct((SUBLANES, LANES), F32)),
        in_specs=[_HBM] * nb,
        out_specs=(_SEM, _SEM, *[_HBM] * nb, pl.BlockSpec(memory_space=pltpu.VMEM)),
        input_output_aliases={i: 2 + i for i in range(nb)},
        compiler_params=pltpu.CompilerParams(has_side_effects=_DATAFLOW),
    )(*[pltpu.with_memory_space_constraint(b, pltpu.HBM) for b in bufs])
    return (name, plan, n_copies, outs[0], outs[1], list(outs[2:2 + nb])), outs[-1]


def _land(flight, after):
    name, plan, n_copies, send_sems, recv_sems, bufs = flight
    nb = len(bufs)
    after = list(after) if isinstance(after, (list, tuple)) else [after]

    def body(*refs):
        ins = refs[:nb]
        s_sems, r_sems = refs[nb], refs[nb + 1]
        for n, (src, dst, mask) in enumerate(plan(ins, _group_index(MESH_AXES))):
            cp = _remote(src, dst, s_sems.at[n], r_sems.at[n], mask)
            cp.wait_send()
            cp.wait_recv()

    outs = pl.pallas_call(
        body, name=name + "_land",
        out_shape=tuple(pltpu.HBM(b.shape, b.dtype) for b in bufs),
        in_specs=[_HBM] * nb + [_SEM, _SEM] + [_ANY] * len(after), out_specs=tuple([_HBM] * nb),
        input_output_aliases={i: i for i in range(nb)},
        compiler_params=pltpu.CompilerParams(has_side_effects=_DATAFLOW),
    )(*bufs, send_sems, recv_sems, *after)
    return list(outs)


def _plan_gather_ici(phase):
    bx, by = _AXIS_BIT["x"], _AXIS_BIT["y"]

    def plan(refs, me):
        copies = []
        for ref in refs:
            half = ref.shape[1] // 2

            def piece(slot, color, mask, ref=ref, half=half):
                p = ref.at[slot, pl.ds(color * half, half)]
                return (p, p, mask)

            if phase == 0:
                copies += [piece(me, 0, bx), piece(me, 1, by)]
            else:
                copies += [piece(me, 0, by), piece(me ^ bx, 0, by), piece(me, 1, bx), piece(me ^ by, 1, bx)]
        return copies

    return plan


def _plan_gather_d2d(refs, me):
    copies = []
    for ref in refs:
        for m in _CHIP_MASKS:
            copies.append((ref.at[me ^ m], ref.at[me ^ m], _AXIS_BIT["c"]))
    return copies


def _plan_scatter_d2d(refs, me):
    na = len(refs) // 2
    copies = []
    for a in range(na):
        for j, m in enumerate(_CHIP_MASKS):
            copies.append((refs[a].at[me ^ _AXIS_BIT["c"] ^ m], refs[na + a].at[j], _AXIS_BIT["c"]))
    return copies


def _plan_scatter_ici(refs, me):
    del me
    na = len(refs) // 2
    copies = []
    for a in range(na):
        for n, m in enumerate(_CHIP_MASKS[1:]):
            copies.append((refs[a].at[n + 1], refs[na + a].at[n], m))
    return copies


def _with_deps(body, n_in, deps):
    if not deps:
        return body

    def wrapped(*refs):
        return body(*refs[:n_in], *refs[n_in + len(deps):])

    return wrapped


def _reduce_scatter_start(tag, grads):
    lands = [lax.empty((len(_CHIP_MASKS),) + g.shape[1:], g.dtype) for g in grads]
    return _launch("rs%s_d2d" % tag, list(grads) + lands, _plan_scatter_d2d, len(_CHIP_MASKS) * len(grads))


def _reduce_scatter_middle(tag, flight, after, me):
    bufs = _land(flight, after)
    na = len(bufs) // 2
    own_slots = jnp.stack([me ^ m for m in _CHIP_MASKS]).astype(jnp.int32)
    sums = [_add_received("rs%s_add_d2d_%d" % (tag, a), bufs[a], own_slots, bufs[na + a], BF16) for a in range(na)]
    lands = [lax.empty((len(_CHIP_MASKS) - 1,) + h.shape[1:], h.dtype) for h in sums]
    return _launch("rs%s_ici" % tag, sums + lands, _plan_scatter_ici, (len(_CHIP_MASKS) - 1) * na)


def _reduce_scatter_finish(tag, flight, after):
    bufs = _land(flight, after)
    na = len(bufs) // 2
    return [_add_final("rs%s_add_ici_%d" % (tag, a), bufs[a], bufs[na + a]) for a in range(na)]


def _all_gather_2d(name, x, deps=()):
    r, c = x.shape

    def body(x_ref, out_ref, send_sems, recv_sems):
        me = _group_index(MESH_AXES)
        out_ref[me] = x_ref[...]
        copies = []
        for k in range(1, N_DEV):
            cp = pltpu.make_async_remote_copy(
                src_ref=x_ref, dst_ref=out_ref.at[me],
                send_sem=send_sems.at[k - 1], recv_sem=recv_sems.at[k - 1],
                device_id=_peer_device(MESH_AXES, k), device_id_type=pl.DeviceIdType.MESH)
            cp.start()
            copies.append(cp)
        for cp in copies:
            cp.wait()

    vmem = pl.BlockSpec(memory_space=pltpu.VMEM)
    return pl.pallas_call(
        _with_deps(body, 1, deps), name=name, out_shape=jax.ShapeDtypeStruct((N_DEV, r, c), x.dtype),
        in_specs=[vmem] + [_ANY] * len(deps), out_specs=vmem,
        scratch_shapes=[pltpu.SemaphoreType.DMA((N_DEV - 1,)), pltpu.SemaphoreType.DMA((N_DEV - 1,))],
    )(x, *deps)


def _sum_slots(name, buf, out_dtype):
    pre, n, r, c = buf.shape
    tr = _tile(r, max(SUBLANES * 2, (1 << 20) // c))

    def body(b_ref, o_ref):
        acc = b_ref[0].astype(F32)
        for q in range(1, n):
            acc = acc + b_ref[q].astype(F32)
        o_ref[...] = acc.astype(o_ref.dtype)

    return pl.pallas_call(
        body, name=name, grid=(pre, r // tr),
        out_shape=jax.ShapeDtypeStruct((pre, r, c), out_dtype),
        in_specs=[pl.BlockSpec((None, n, tr, c), lambda i, j: (i, 0, j, 0))],
        out_specs=pl.BlockSpec((None, tr, c), lambda i, j: (i, j, 0)),
        compiler_params=_params("parallel", "parallel"),
    )(buf)


def _all_gather_weights(bufs):
    return _gather_d2d("ag_d2d", list(_gather_ici("ag_ici", bufs)))


_CHIP_MASKS = (0, _AXIS_BIT["y"], _AXIS_BIT["x"], _AXIS_BIT["x"] | _AXIS_BIT["y"])


def _reduce_scatter_grads(bufs):
    me = _group_index(MESH_AXES)
    bc = _AXIS_BIT["c"]
    r1 = _scatter("rs_d2d", bufs, lambda i: [i ^ bc ^ m for m in _CHIP_MASKS], (bc,) * len(_CHIP_MASKS))
    own_slots = jnp.stack([me ^ m for m in _CHIP_MASKS]).astype(jnp.int32)
    half = [_add_received("rs_add_d2d_%d" % a, b, own_slots, r, BF16) for a, (b, r) in enumerate(zip(bufs, r1))]
    r2 = _scatter("rs_ici", half, lambda i: [1, 2, 3], _CHIP_MASKS[1:])
    return [_add_final("rs_add_ici_%d" % a, h, r) for a, (h, r) in enumerate(zip(half, r2))]


def _matmul(name, mode, grid, a, a_spec, b, b_spec, out_shapes, out_specs, acc_shape,
            epilogue=None, extras=(), extra_specs=(), aliases=None, deps=()):
    nk = grid[2]
    n_extra = len(extras)
    n_out = len(out_shapes)

    def finish(acc, extra_refs, out_refs):
        if epilogue is None:
            out_refs[0][...] = acc.astype(out_refs[0].dtype)
        else:
            epilogue(acc, extra_refs, out_refs)

    def body(*refs):
        a_ref, b_ref = refs[0], refs[1]
        extra_refs = refs[2:2 + n_extra]
        out_refs = refs[2 + n_extra:2 + n_extra + n_out]
        if nk == 1:
            finish(_dot(a_ref[...], b_ref[...], mode), extra_refs, out_refs)
            return
        acc_ref = refs[-1]
        k = pl.program_id(2)

        @pl.when(k == 0)
        def _():
            acc_ref[...] = jnp.zeros_like(acc_ref)

        acc_ref[...] += _dot(a_ref[...], b_ref[...], mode)

        @pl.when(k == nk - 1)
        def _():
            finish(acc_ref[...], extra_refs, out_refs)

    scratch = [] if nk == 1 else [pltpu.VMEM(acc_shape, F32)]
    return pl.pallas_call(
        _with_deps(body, 2 + n_extra, deps), name=name, grid=grid, out_shape=tuple(out_shapes),
        in_specs=[a_spec, b_spec] + list(extra_specs) + [_ANY] * len(deps), out_specs=tuple(out_specs),
        scratch_shapes=scratch, input_output_aliases=aliases or {},
        compiler_params=_params("parallel", "parallel", "arbitrary"),
    )(a, b, *extras, *deps)


def _sds(shape, dtype):
    return jax.ShapeDtypeStruct(tuple(shape), dtype)


def _ada_forward(c_all, w_ada, b_shard):
    nb, d = c_all.shape
    w = w_ada.shape[1]
    tn = _tile(w, 512)

    def body(c_ref, w_ref, b_ref, o_ref):
        cv = c_ref[...]
        sc = cv * jax.nn.sigmoid(cv)
        o_ref[...] = jnp.dot(sc, w_ref[...], precision=lax.Precision.HIGHEST,
                             preferred_element_type=F32) + b_ref[...]

    return pl.pallas_call(
        body, name="ada_fwd", grid=(w // tn,), out_shape=_sds((nb, w), F32),
        in_specs=[pl.BlockSpec((nb, d), lambda j: (0, 0)), pl.BlockSpec((d, tn), lambda j: (0, j)),
                  pl.BlockSpec((1, tn), lambda j: (0, j))],
        out_specs=pl.BlockSpec((nb, tn), lambda j: (0, j)),
        compiler_params=_params("parallel"),
    )(c_all, w_ada, b_shard)


def _ada_weight_grad(c_all, dmod_cols, deps=()):
    nb, d = c_all.shape
    w = dmod_cols.shape[1]
    tn = _tile(w, 512)

    def body(c_ref, g_ref, o_ref):
        cv = c_ref[...]
        sc = cv * jax.nn.sigmoid(cv)
        o_ref[...] = lax.dot_general(sc, g_ref[...], _TN, precision=lax.Precision.HIGHEST,
                                     preferred_element_type=F32)

    return pl.pallas_call(
        _with_deps(body, 2, deps), name="ada_wgrad", grid=(w // tn,), out_shape=_sds((d, w), F32),
        in_specs=[pl.BlockSpec((nb, d), lambda j: (0, 0)), pl.BlockSpec((nb, tn), lambda j: (0, j))]
        + [_ANY] * len(deps),
        out_specs=pl.BlockSpec((d, tn), lambda j: (0, j)),
        compiler_params=_params("parallel"),
    )(c_all, dmod_cols, *deps)


def _norm_forward(name, x, norm_w, scale, shift, deps=()):
    s, d = x.shape
    tm = _tile(s, 256)

    def body(x_ref, w_ref, sc_ref, sh_ref, h_ref):
        xv = x_ref[...]
        r = lax.rsqrt(jnp.mean(xv * xv, axis=-1, keepdims=True) + NORM_EPS)
        h = (xv * r * w_ref[...]) * (1.0 + sc_ref[...]) + sh_ref[...]
        h_ref[...] = h.astype(BF16)

    vec = pl.BlockSpec((1, d), lambda i: (0, 0))
    row = pl.BlockSpec((tm, d), lambda i: (i, 0))
    return pl.pallas_call(
        _with_deps(body, 4, deps), name=name, grid=(s // tm,), out_shape=_sds((s, d), BF16),
        in_specs=[row, vec, vec, vec] + [_ANY] * len(deps), out_specs=row, compiler_params=_params("parallel"),
    )(x, norm_w, scale, shift, *deps)


def _norm_backward(name, dh, x, norm_w, scale, dres, deps=()):
    s, d = x.shape
    tm = _tile(s, 256)

    def body(dh_ref, x_ref, w_ref, sc_ref, dres_ref, dx_ref, dshift_ref, dscale_ref, dw_ref):
        @pl.when(pl.program_id(0) == 0)
        def _():
            dshift_ref[...] = jnp.zeros_like(dshift_ref)
            dscale_ref[...] = jnp.zeros_like(dscale_ref)
            dw_ref[...] = jnp.zeros_like(dw_ref)

        xv = x_ref[...]
        g = dh_ref[...]
        r = lax.rsqrt(jnp.mean(xv * xv, axis=-1, keepdims=True) + NORM_EPS)
        n = xv * r
        gain = 1.0 + sc_ref[...]
        gn = g * n
        dshift_ref[...] += jnp.sum(g, axis=0, keepdims=True)
        dscale_ref[...] += jnp.sum(gn, axis=0, keepdims=True) * w_ref[...]
        dw_ref[...] += jnp.sum(gn, axis=0, keepdims=True) * gain
        dn = g * (w_ref[...] * gain)
        dx_ref[...] = dres_ref[...] + r * (dn - n * jnp.mean(dn * n, axis=-1, keepdims=True))

    vec = pl.BlockSpec((1, d), lambda i: (0, 0))
    row = pl.BlockSpec((tm, d), lambda i: (i, 0))
    return pl.pallas_call(
        _with_deps(body, 5, deps), name=name, grid=(s // tm,),
        out_shape=(_sds((s, d), F32), _sds((1, d), F32), _sds((1, d), F32), _sds((1, d), F32)),
        in_specs=[row, row, vec, vec, row] + [_ANY] * len(deps), out_specs=(row, vec, vec, vec),
        compiler_params=_params("arbitrary"),
    )(dh, x, norm_w, scale, dres, *deps)


def _gate_backward(name, d, gate, other):
    s, w = d.shape
    tm = _tile(s, 256)

    def body(d_ref, g_ref, o_ref, dg_ref, dgate_ref):
        @pl.when(pl.program_id(0) == 0)
        def _():
            dgate_ref[...] = jnp.zeros_like(dgate_ref)

        dv = d_ref[...]
        dg_ref[...] = (dv * g_ref[...]).astype(BF16)
        dgate_ref[...] += jnp.sum(dv * o_ref[...].astype(F32), axis=0, keepdims=True)

    vec = pl.BlockSpec((1, w), lambda i: (0, 0))
    row = pl.BlockSpec((tm, w), lambda i: (i, 0))
    return pl.pallas_call(
        body, name=name, grid=(s // tm,), out_shape=(_sds((s, w), BF16), _sds((1, w), F32)),
        in_specs=[row, vec, row], out_specs=(row, vec), compiler_params=_params("arbitrary"),
    )(d, gate, other)


def _split_bf16(v):
    hi = v.astype(BF16)
    lo = (v - hi.astype(F32)).astype(BF16)
    return hi, lo


def _pool_forward(proj, w_pool, pool_scale, deps=()):
    s = proj.shape[0]
    g_n, cg, _ = w_pool.shape
    t = POOL_TILE
    nt = s // t

    def body(cur_ref, prev_ref, wp_ref, sc_ref, pooled_ref, ya_ref):
        g = pl.program_id(0)
        ti = pl.program_id(1)
        win = jnp.left_shift(2, g)
        row = lax.broadcasted_iota(jnp.int32, (t, t), 0)
        col = lax.broadcasted_iota(jnp.int32, (t, t), 1)
        lag = row - col
        band_cur = ((lag >= 0) & (lag < win)).astype(BF16)
        band_prev = ((lag + t < win) & (ti > 0)).astype(BF16)
        u = cur_ref[...]
        u_hi, u_lo = _split_bf16(u)
        p_hi, p_lo = _split_bf16(prev_ref[...])
        wsum = (_dot(band_cur, u_hi) + _dot(band_cur, u_lo)
                + _dot(band_prev, p_hi) + _dot(band_prev, p_lo))
        tok = ti * t + lax.broadcasted_iota(jnp.int32, (t, 1), 0)
        count = jnp.minimum(tok + 1, win).astype(F32)
        pooled = (wsum / count - u).astype(BF16)
        pooled_ref[...] = pooled
        ya_ref[...] = (_dot(pooled, wp_ref[...]) * sc_ref[...]).astype(BF16)

    blk = pl.BlockSpec((t, cg), lambda g, i: (i, g))
    return pl.pallas_call(
        _with_deps(body, 4, deps), name="pool_fwd", grid=(g_n, nt),
        out_shape=(_sds((s, g_n * cg), BF16), _sds((s, g_n * cg), BF16)),
        in_specs=[blk, pl.BlockSpec((t, cg), lambda g, i: (jnp.maximum(i - 1, 0), g)),
                  pl.BlockSpec((None, cg, cg), lambda g, i: (g, 0, 0)),
                  pl.BlockSpec((1, cg), lambda g, i: (0, g))] + [_ANY] * len(deps),
        out_specs=(blk, blk), compiler_params=_params("parallel", "parallel"),
    )(proj, proj, w_pool, pool_scale, *deps)


def _pool_backward(dya, pooled, w_pool, pool_scale, dproj):
    s = dya.shape[0]
    g_n, cg, _ = w_pool.shape
    t = POOL_TILE
    nt = s // t

    def body(dya_ref, dya_next_ref, pooled_ref, wp_ref, sc_ref, dproj_in, du_ref, gw_ref, gs_ref):
        del dproj_in
        g = pl.program_id(0)
        ti = pl.program_id(1)

        @pl.when(ti == 0)
        def _():
            gw_ref[...] = jnp.zeros_like(gw_ref)
            gs_ref[...] = jnp.zeros_like(gs_ref)

        win = jnp.left_shift(2, g)
        wp = wp_ref[...]
        sc = sc_ref[...]
        pooled_v = pooled_ref[...]
        dya_v = dya_ref[...].astype(F32)
        mixed = _dot(pooled_v, wp)
        gs_ref[...] += jnp.sum(dya_v * mixed, axis=0, keepdims=True)
        dmixed = (dya_v * sc).astype(BF16)
        gw_ref[...] += _dot(pooled_v, dmixed, "tn")
        dpooled = _dot(dmixed, wp, "nt")
        dmixed_next = (dya_next_ref[...].astype(F32) * sc).astype(BF16)
        dpooled_next = _dot(dmixed_next, wp, "nt")
        tok = ti * t + lax.broadcasted_iota(jnp.int32, (t, 1), 0)
        e_cur = dpooled / jnp.minimum(tok + 1, win).astype(F32)
        e_next = dpooled_next / jnp.minimum(tok + t + 1, win).astype(F32)
        row = lax.broadcasted_iota(jnp.int32, (t, t), 0)
        col = lax.broadcasted_iota(jnp.int32, (t, t), 1)
        lead = col - row
        band_cur = ((lead >= 0) & (lead < win)).astype(BF16)
        band_next = ((lead + t < win) & (ti < nt - 1)).astype(BF16)
        c_hi, c_lo = _split_bf16(e_cur)
        n_hi, n_lo = _split_bf16(e_next)
        du = (_dot(band_cur, c_hi) + _dot(band_cur, c_lo)
              + _dot(band_next, n_hi) + _dot(band_next, n_lo)) - dpooled
        du_ref[...] = du.astype(BF16)

    blk = pl.BlockSpec((t, cg), lambda g, i: (i, g))
    du, gw, gs = pl.pallas_call(
        body, name="pool_bwd", grid=(g_n, nt),
        out_shape=(_sds(dproj.shape, BF16), _sds((g_n, cg, cg), F32), _sds((1, g_n * cg), F32)),
        in_specs=[blk, pl.BlockSpec((t, cg), lambda g, i: (jnp.minimum(i + 1, nt - 1), g)), blk,
                  pl.BlockSpec((None, cg, cg), lambda g, i: (g, 0, 0)),
                  pl.BlockSpec((1, cg), lambda g, i: (0, g)),
                  pl.BlockSpec(memory_space=pl.ANY)],
        out_specs=(blk, pl.BlockSpec((None, cg, cg), lambda g, i: (g, 0, 0)),
                   pl.BlockSpec((1, cg), lambda g, i: (0, g))),
        input_output_aliases={5: 0}, compiler_params=_params("parallel", "arbitrary"),
    )(dya, dya, pooled, w_pool, pool_scale, dproj)
    return du, gw, gs


def _qkv_prepare(proj, q_norm_w, k_norm_w, width, deps=()):
    s = proj.shape[0]
    tm = _tile(s, 256)
    heads = width // HEAD_DIM

    def body(q_ref, k_ref, v_ref, qw_ref, kw_ref, qn_ref, kn_ref, vb_ref):
        for h in range(heads):
            cols = slice(h * HEAD_DIM, (h + 1) * HEAD_DIM)
            for src, w_ref, dst in ((q_ref, qw_ref, qn_ref), (k_ref, kw_ref, kn_ref)):
                v = src[:, cols]
                r = lax.rsqrt(jnp.mean(v * v, axis=-1, keepdims=True) + NORM_EPS)
                dst[:, cols] = (v * r * w_ref[...]).astype(BF16)
        vb_ref[...] = v_ref[...].astype(BF16)

    vec = pl.BlockSpec((1, HEAD_DIM), lambda i: (0, 0))
    out_spec = pl.BlockSpec((tm, width), lambda i: (i, 0))
    return pl.pallas_call(
        _with_deps(body, 5, deps), name="qkv_prep", grid=(s // tm,),
        out_shape=(_sds((s, width), BF16),) * 3,
        in_specs=[pl.BlockSpec((tm, width), lambda i: (i, 1)), pl.BlockSpec((tm, width), lambda i: (i, 2)),
                  pl.BlockSpec((tm, width), lambda i: (i, 3)), vec, vec] + [_ANY] * len(deps),
        out_specs=(out_spec,) * 3, compiler_params=_params("parallel"),
    )(proj, proj, proj, q_norm_w, k_norm_w, *deps)


def _qk_norm_backward(name, dn, proj, col_block, norm_w, dproj, width, deps=()):
    s = proj.shape[0]
    tm = _tile(s, 256)
    heads = width // HEAD_DIM

    def body(dn_ref, q_ref, w_ref, dproj_in, dq_ref, gw_ref):
        del dproj_in

        @pl.when(pl.program_id(0) == 0)
        def _():
            gw_ref[...] = jnp.zeros_like(gw_ref)

        wv = w_ref[...]
        gw = jnp.zeros((1, HEAD_DIM), F32)
        for h in range(heads):
            cols = slice(h * HEAD_DIM, (h + 1) * HEAD_DIM)
            v = q_ref[:, cols]
            g = dn_ref[:, cols]
            r = lax.rsqrt(jnp.mean(v * v, axis=-1, keepdims=True) + NORM_EPS)
            n = v * r
            gw = gw + jnp.sum(g * n, axis=0, keepdims=True)
            gn = g * wv
            dq_ref[:, cols] = (r * (gn - n * jnp.mean(gn * n, axis=-1, keepdims=True))).astype(BF16)
        gw_ref[...] += gw

    blk = pl.BlockSpec((tm, width), lambda i: (i, col_block))
    return pl.pallas_call(
        _with_deps(body, 4, deps), name=name, grid=(s // tm,),
        out_shape=(_sds(dproj.shape, BF16), _sds((1, HEAD_DIM), F32)),
        in_specs=[pl.BlockSpec((tm, width), lambda i: (i, 0)), blk,
                  pl.BlockSpec((1, HEAD_DIM), lambda i: (0, 0)), pl.BlockSpec(memory_space=pl.ANY)]
        + [_ANY] * len(deps),
        out_specs=(blk, pl.BlockSpec((1, HEAD_DIM), lambda i: (0, 0))),
        input_output_aliases={3: 0}, compiler_params=_params("arbitrary"),
    )(dn, proj, norm_w, dproj, *deps)


def _strict_upper(n):
    row = lax.broadcasted_iota(jnp.int32, (n, n), 0)
    col = lax.broadcasted_iota(jnp.int32, (n, n), 1)
    return (row > col).astype(BF16)


def _strict_lower(n):
    row = lax.broadcasted_iota(jnp.int32, (n, n), 0)
    col = lax.broadcasted_iota(jnp.int32, (n, n), 1)
    return (row < col).astype(BF16)


def _cumulate(v, tri):
    hi, lo = _split_bf16(v)
    return _dot(hi, tri) + _dot(lo, tri)


def _log_sigmoid(z):
    return jnp.minimum(z, 0.0) - jnp.log(1.0 + jnp.exp(-jnp.abs(z)))


def _attention_forward(qn, kn, vb):
    s, width = qn.shape
    heads = width // HEAD_DIM
    tq, tk = Q_TILE, K_TILE
    hp = HEADS_PER_STEP
    assert tq == tk and s % tq == 0 and heads % hp == 0
    scale = 1.0 / math.sqrt(HEAD_DIM)

    def body(q_ref, k_ref, v_ref, o_ref):
        qi = pl.program_id(1)
        upper = _strict_upper(tk)
        causal = lax.broadcasted_iota(jnp.int32, (tq, tk), 1) < lax.broadcasted_iota(jnp.int32, (tq, tk), 0)

        def chunk(kb, carry, masked):
            rows = pl.ds(pl.multiple_of(kb * tk, tk), tk)
            out = []
            for u in range(hp):
                acc, later = carry[u]
                cols = slice(u * HEAD_DIM, (u + 1) * HEAD_DIM)
                z = _dot(q_ref[:, cols], k_ref[rows, cols], "nt") * scale
                log_beta = _log_sigmoid(z)
                l = log_beta - z
                if masked:
                    l = jnp.where(causal, l, 0.0)
                a = jnp.exp(log_beta + _cumulate(l, upper) + later)
                if masked:
                    a = jnp.where(causal, a, 0.0)
                acc = acc + _dot(a.astype(BF16), v_ref[rows, cols])
                out.append((acc, later + jnp.sum(l, axis=1, keepdims=True)))
            return tuple(out)

        carry = tuple((jnp.zeros((tq, HEAD_DIM), F32), jnp.zeros((tq, 1), F32)) for _ in range(hp))
        carry = chunk(qi, carry, True)
        carry = lax.fori_loop(0, qi, lambda i, c: chunk(qi - 1 - i, c, False), carry)
        for u in range(hp):
            o_ref[:, u * HEAD_DIM:(u + 1) * HEAD_DIM] = carry[u][0].astype(BF16)

    full = pl.BlockSpec((s, hp * HEAD_DIM), lambda h, i: (0, h))
    blk = pl.BlockSpec((tq, hp * HEAD_DIM), lambda h, i: (i, h))
    return pl.pallas_call(
        body, name="attn_fwd", grid=(heads // hp, s // tq), out_shape=_sds((s, width), BF16),
        in_specs=[blk, full, full], out_specs=blk, compiler_params=_params("parallel", "parallel"),
    )(qn, kn, vb)


def _attention_backward(qn, kn, vb, dout, dproj, v_col_block, deps=()):
    s, width = qn.shape
    heads = width // HEAD_DIM
    tq, tk = Q_TILE, K_TILE
    hp = HEADS_PER_STEP
    nq = s // tq
    scale = 1.0 / math.sqrt(HEAD_DIM)
    v_block0 = v_col_block * (heads // hp)

    def body(q_ref, k_ref, v_ref, do_ref, dproj_in, dq_ref, dk_ref, dv_ref, a_scr, dk_scr, dv_scr):
        del dproj_in
        qi = pl.program_id(1)

        @pl.when(qi == 0)
        def _():
            dk_scr[...] = jnp.zeros_like(dk_scr)
            dv_scr[...] = jnp.zeros_like(dv_scr)

        upper = _strict_upper(tk)
        lower = _strict_lower(tk)
        causal = lax.broadcasted_iota(jnp.int32, (tq, tk), 1) < lax.broadcasted_iota(jnp.int32, (tq, tk), 0)
        head_cols = [slice(u * HEAD_DIM, (u + 1) * HEAD_DIM) for u in range(hp)]

        def weights(kb, carry, masked):
            rows = pl.ds(pl.multiple_of(kb * tk, tk), tk)
            out = []
            for u, cols in enumerate(head_cols):
                later = carry[u]
                z = _dot(q_ref[:, cols], k_ref[rows, cols], "nt") * scale
                log_beta = _log_sigmoid(z)
                l = log_beta - z
                if masked:
                    l = jnp.where(causal, l, 0.0)
                a = jnp.exp(log_beta + _cumulate(l, upper) + later)
                if masked:
                    a = jnp.where(causal, a, 0.0)
                a_scr[u, :, rows] = a
                out.append(later + jnp.sum(l, axis=1, keepdims=True))
            return tuple(out)

        zeros = tuple(jnp.zeros((tq, 1), F32) for _ in range(hp))
        later = weights(qi, zeros, True)
        lax.fori_loop(0, qi, lambda i, c: weights(qi - 1 - i, c, False), later)

        def grads(kb, carry, masked):
            rows = pl.ds(pl.multiple_of(kb * tk, tk), tk)
            out = []
            for u, cols in enumerate(head_cols):
                dq, before = carry[u]
                q = q_ref[:, cols]
                dout_v = do_ref[:, cols]
                k_blk = k_ref[rows, cols]
                z = _dot(q, k_blk, "nt") * scale
                beta = jax.nn.sigmoid(z)
                a = a_scr[u, :, rows]
                g = a * _dot(dout_v, v_ref[rows, cols], "nt")
                p = _cumulate(g, lower) + before
                dz = g * (1.0 - beta) - p * beta
                if masked:
                    dz = jnp.where(causal, dz, 0.0)
                dz = (dz * scale).astype(BF16)
                dk_scr[rows, cols] += _dot(dz, q, "tn")
                dv_scr[rows, cols] += _dot(a.astype(BF16), dout_v, "tn")
                out.append((dq + _dot(dz, k_blk), before + jnp.sum(g, axis=1, keepdims=True)))
            return tuple(out)

        carry = tuple((jnp.zeros((tq, HEAD_DIM), F32), jnp.zeros((tq, 1), F32)) for _ in range(hp))
        carry = lax.fori_loop(0, qi, lambda i, c: grads(i, c, False), carry)
        carry = grads(qi, carry, True)
        for u, cols in enumerate(head_cols):
            dq_ref[:, cols] = carry[u][0]

        @pl.when(qi == nq - 1)
        def _():
            dk_ref[...] = dk_scr[...]
            dv_ref[...] = dv_scr[...].astype(BF16)

    wide = hp * HEAD_DIM
    full = pl.BlockSpec((s, wide), lambda h, i: (0, h))
    blk = pl.BlockSpec((tq, wide), lambda h, i: (i, h))
    return pl.pallas_call(
        _with_deps(body, 5, deps), name="attn_bwd", grid=(heads // hp, nq),
        out_shape=(_sds((s, width), F32), _sds((s, width), F32), _sds(dproj.shape, BF16)),
        in_specs=[blk, full, full, blk, pl.BlockSpec(memory_space=pl.ANY)] + [_ANY] * len(deps),
        out_specs=(blk, full, pl.BlockSpec((s, wide), lambda h, i: (0, v_block0 + h))),
        scratch_shapes=[pltpu.VMEM((hp, tq, s), F32), pltpu.VMEM((s, wide), F32), pltpu.VMEM((s, wide), F32)],
        input_output_aliases={4: 2}, compiler_params=_params("parallel", "arbitrary"),
    )(qn, kn, vb, dout, dproj, *deps)


def _place_columns(name, src, dst, col_block):
    s, w = src.shape
    tm = _tile(s, 512)

    def body(src_ref, dst_in, out_ref):
        del dst_in
        out_ref[...] = src_ref[...]

    return pl.pallas_call(
        body, name=name, grid=(s // tm,), out_shape=_sds(dst.shape, dst.dtype),
        in_specs=[pl.BlockSpec((tm, w), lambda i: (i, 0)), pl.BlockSpec(memory_space=pl.ANY)],
        out_specs=pl.BlockSpec((tm, w), lambda i: (i, col_block)),
        input_output_aliases={1: 0}, compiler_params=_params("parallel"),
    )(src, dst)


def _cast_into_slot(name, x, slot):
    r, c = x.shape
    tr = _tile(r, max(SUBLANES * 2, (1 << 20) // c), SUBLANES * 2)

    def body(slot_ref, x_ref, o_ref):
        del slot_ref
        o_ref[...] = x_ref[...].astype(BF16)

    grid_spec = pltpu.PrefetchScalarGridSpec(
        num_scalar_prefetch=1, grid=(r // tr,),
        in_specs=[pl.BlockSpec((tr, c), lambda i, slot_ref: (i, 0))],
        out_specs=pl.BlockSpec((None, tr, c), lambda i, slot_ref: (slot_ref[0], i, 0)))
    return pl.pallas_call(
        body, name=name, grid_spec=grid_spec, out_shape=_sds((N_DEV, r, c), BF16),
        compiler_params=_params("parallel"),
    )(slot, x)


def _adamw(name, w, g, m, v):
    r, c = w.shape
    tr = _tile(r, max(SUBLANES, (1 << 19) // c))
    c1 = 1.0 - ADAM_B1 ** ADAM_STEP
    c2 = 1.0 - ADAM_B2 ** ADAM_STEP

    def body(w_ref, g_ref, m_ref, v_ref, d_ref, nm_ref, nv_ref):
        gv = g_ref[...]
        nm = ADAM_B1 * m_ref[...] + (1.0 - ADAM_B1) * gv
        nv = ADAM_B2 * v_ref[...] + (1.0 - ADAM_B2) * (gv * gv)
        d_ref[...] = -ADAM_LR * ((nm / c1) / (jnp.sqrt(nv / c2) + ADAM_EPS) + ADAM_WD * w_ref[...])
        nm_ref[...] = nm
        nv_ref[...] = nv

    blk = pl.BlockSpec((tr, c), lambda i: (i, 0))
    return pl.pallas_call(
        body, name=name, grid=(r // tr,), out_shape=(_sds((r, c), F32),) * 3,
        in_specs=[blk] * 4, out_specs=(blk,) * 3, compiler_params=_params("parallel"),
    )(w, g, m, v)


def _rows_of_lanes(v):
    rows = v.shape[1] // LANES
    out = v.reshape(rows, LANES)
    pad = (-rows) % SUBLANES
    if pad:
        out = jnp.pad(out, ((0, pad), (0, 0)))
    return out


def kernel(x, c, w_ada, b_ada, norm1_w, w_in, q_norm_w, k_norm_w, w_pool, pool_scale, w_a_up, w_b_up, w_o, norm2_w, w_ff1, w_ff2, loss_target, m_w_ada, m_b_ada, m_norm1_w, m_w_in, m_q_norm_w, m_k_norm_w, m_w_pool, m_pool_scale, m_w_a_up, m_w_b_up, m_w_o, m_norm2_w, m_w_ff1, m_w_ff2, v_w_ada, v_b_ada, v_norm1_w, v_w_in, v_q_norm_w, v_k_norm_w, v_w_pool, v_pool_scale, v_w_a_up, v_w_b_up, v_w_o, v_norm2_w, v_w_ff1, v_w_ff2):
    _, s, d = x.shape
    half = d // 2
    d8 = d // N_DEV
    n_groups = len(POOL_WINDOWS)
    cg = half // n_groups
    me = _group_index(MESH_AXES)

    x2 = x[0]
    target = loss_target[0]

    my_slot = jnp.reshape(me, (1,)).astype(jnp.int32)

    def cast(i, t):
        return _cast_into_slot("cast_w%d" % i, t, my_slot)

    def gather_start(tag, bufs, phase):
        if phase < 2:
            return _launch("ag%s_ici%d" % (tag, phase), bufs, _plan_gather_ici(phase), (2, 4)[phase] * len(bufs))
        return _launch("ag%s_d2d" % tag, bufs, _plan_gather_d2d, len(_CHIP_MASKS) * len(bufs))

    buf_a = [cast(0, w_in[0])]
    fl_a, tok = gather_start("A", buf_a, 0)

    c_all = _all_gather_2d("ag_c", c.reshape(d // LANES, LANES), deps=(tok,)).reshape(N_DEV, d)
    wa = w_ada.shape[2]
    b_shard = lax.dynamic_slice_in_dim(b_ada, me * wa, wa, axis=1)
    mod_part = _ada_forward(c_all, w_ada[0], b_shard)
    mod_all = _all_gather_2d("ag_mod", mod_part.reshape(N_DEV * wa // LANES, LANES))
    mod_all = mod_all.reshape(N_DEV, N_DEV, wa)
    mod = lax.dynamic_slice_in_dim(mod_all, me, 1, axis=1).reshape(1, N_MOD * d)
    shift1, scale1, gate1, shift2, scale2, gate2 = [mod[:, i * d:(i + 1) * d] for i in range(N_MOD)]
    buf_b = [cast(1, w_pool[0].reshape(-1, cg)), cast(2, w_a_up[0]), cast(3, w_b_up[0]), cast(4, w_o[0])]
    buf_c = [cast(5, w_ff1[0]), cast(6, w_ff2[0])]

    buf_a = _land(fl_a, mod)
    fl_a, tok = gather_start("A", buf_a, 1)
    h = _norm_forward("norm1_fwd", x2, norm1_w, scale1, shift1, deps=(tok,))
    buf_a = _land(fl_a, [h] + buf_b + buf_c)
    fl_a, tok = gather_start("A", buf_a, 2)
    w_in_f, = _land(fl_a, tok)

    tm = _tile(s, 1024)
    tk = _tile(d, 2048)
    te = _tile(d, 512)

    fl_b, tok_b = gather_start("B", buf_b, 0)
    fl_c, tok_c = gather_start("C", buf_c, 0)
    proj = _matmul(
        "proj", "nn", (s // tm, N_DEV, d // tk), h, pl.BlockSpec((tm, tk), lambda i, j, k: (i, k)),
        w_in_f, pl.BlockSpec((None, tk, half), lambda i, j, k: (j, k, 0)),
        [_sds((s, 4 * d), F32)], [pl.BlockSpec((tm, half), lambda i, j, k: (i, j))], (tm, half),
        deps=(tok_b, tok_c))[0]
    buf_b = _land(fl_b, proj)
    buf_c = _land(fl_c, proj)
    fl_b, tok_b = gather_start("B", buf_b, 1)
    fl_c, tok_c = gather_start("C", buf_c, 1)

    qn, kn, vb = _qkv_prepare(proj, q_norm_w, k_norm_w, half, deps=(tok_b, tok_c))
    attn = _attention_forward(qn, kn, vb)
    buf_b = _land(fl_b, attn)
    buf_c = _land(fl_c, attn)
    fl_b, tok_b = gather_start("B", buf_b, 2)
    fl_c, tok_c = gather_start("C", buf_c, 2)
    w_pool_f, w_a_f, w_b_f, w_o_f = _land(fl_b, tok_b)
    rows_pool = cg // N_DEV
    w_pool_f = w_pool_f.reshape(N_DEV, n_groups, rows_pool, cg).transpose(1, 0, 2, 3).reshape(n_groups, cg, cg)
    w_o_f = w_o_f.reshape(d, d)
    pooled, ya_in = _pool_forward(proj, w_pool_f, pool_scale, deps=(tok_c,))

    def merge_epilogue(ga_ref, gb_ref, ya, yb, out_refs):
        merged_ref, ya_ref, yb_ref = out_refs
        merged = jax.nn.sigmoid(ga_ref[...]) * ya + jax.nn.sigmoid(gb_ref[...]) * yb
        merged_ref[...] = merged.astype(BF16)
        ya_ref[...] = ya.astype(BF16)
        yb_ref[...] = yb.astype(BF16)

    def up_body(a1_ref, b1_ref, a2_ref, b2_ref, ga_ref, gb_ref, *out_refs):
        merge_epilogue(ga_ref, gb_ref, _dot(a1_ref[...], b1_ref[...]), _dot(a2_ref[...], b2_ref[...]), out_refs)

    ga_blk0 = 2 * d // d8
    gb_blk0 = 3 * d // d8
    a_spec = pl.BlockSpec((tm, half), lambda i, j: (i, 0))
    wup_spec = pl.BlockSpec((None, half, d8), lambda i, j: (j, 0, 0))
    o_blk = pl.BlockSpec((tm, d8), lambda i, j: (i, j))
    merged, y_a, y_b = pl.pallas_call(
        up_body, name="up_merge", grid=(s // tm, N_DEV), out_shape=(_sds((s, d), BF16),) * 3,
        in_specs=[a_spec, wup_spec, a_spec, wup_spec,
                  pl.BlockSpec((tm, d8), lambda i, j: (i, ga_blk0 + j)),
                  pl.BlockSpec((tm, d8), lambda i, j: (i, gb_blk0 + j))],
        out_specs=(o_blk,) * 3, compiler_params=_params("parallel", "parallel"),
    )(ya_in, w_a_f, attn, w_b_f, proj, proj)

    tn = _tile(d, 1024)

    def oproj_epilogue(acc, extra_refs, out_refs):
        x_ref, g_ref = extra_refs
        x1_ref, o_ref = out_refs
        x1_ref[...] = x_ref[...] + g_ref[...] * acc
        o_ref[...] = acc.astype(BF16)

    mn_blk = pl.BlockSpec((tm, tn), lambda i, j, k: (i, j))
    e_blk = pl.BlockSpec((tm, te), lambda i, j, k: (i, j))
    e_vec = pl.BlockSpec((1, te), lambda i, j, k: (0, j))
    x1, o_act = _matmul(
        "oproj", "nn", (s // tm, d // te, d // tk), merged, pl.BlockSpec((tm, tk), lambda i, j, k: (i, k)),
        w_o_f, pl.BlockSpec((tk, te), lambda i, j, k: (k, j)),
        [_sds((s, d), F32), _sds((s, d), BF16)], [e_blk, e_blk], (tm, te),
        epilogue=oproj_epilogue, extras=(x2, gate1), extra_specs=(e_blk, e_vec))

    h2 = _norm_forward("norm2_fwd", x1, norm2_w, scale2, shift2)
    w_ff1_f, w_ff2_f = _land(fl_c, h2)
    w_ff2_f = w_ff2_f.reshape(4 * d, d)

    def ff1_epilogue(acc, extra_refs, out_refs):
        r = jnp.maximum(acc, 0.0)
        out_refs[0][...] = r.astype(BF16)
        out_refs[1][...] = (r * r).astype(BF16)

    ff_blk = pl.BlockSpec((tm, half), lambda i, j, k: (i, j))
    relu, act = _matmul(
        "ff1", "nn", (s // tm, N_DEV, d // tk), h2, pl.BlockSpec((tm, tk), lambda i, j, k: (i, k)),
        w_ff1_f, pl.BlockSpec((None, tk, half), lambda i, j, k: (j, k, 0)),
        [_sds((s, 4 * d), BF16)] * 2, [ff_blk, ff_blk], (tm, half), epilogue=ff1_epilogue)

    def ff2_epilogue(acc, extra_refs, out_refs):
        x1_ref, g_ref, t_ref = extra_refs
        f_ref, dy_ref, sq_ref = out_refs
        err = x1_ref[...] + g_ref[...] * acc - t_ref[...]
        f_ref[...] = acc.astype(BF16)
        dy_ref[...] = err * (1.0 / d)
        sq_ref[...] = jnp.full(sq_ref.shape, jnp.sum(err * err), F32)

    f_act, dy, sq = _matmul(
        "ff2", "nn", (s // tm, d // te, 4 * d // tk), act, pl.BlockSpec((tm, tk), lambda i, j, k: (i, k)),
        w_ff2_f, pl.BlockSpec((tk, te), lambda i, j, k: (k, j)),
        [_sds((s, d), BF16), _sds((s, d), F32), _sds((s // tm * SUBLANES, d // te * LANES), F32)],
        [e_blk, e_blk, pl.BlockSpec((SUBLANES, LANES), lambda i, j, k: (i, j))], (tm, te),
        epilogue=ff2_epilogue, extras=(x1, gate2, target), extra_specs=(e_blk, e_vec, e_blk))
    loss_local = (0.5 / d) * jnp.sum(sq[::SUBLANES, ::LANES])
    loss = lax.psum(loss_local, MESH_AXES)

    df, dgate2 = _gate_backward("gate2_bwd", dy, gate2, f_act)
    tok_k = _tile(s, 2048)
    tw = _tile(d, 1024)
    g_ff2 = _matmul(
        "g_ff2", "tn", (4 * d // tw, d // tn, s // tok_k), act, pl.BlockSpec((tok_k, tw), lambda i, j, k: (k, i)),
        df, pl.BlockSpec((tok_k, tn), lambda i, j, k: (k, j)),
        [_sds((4 * d, d), BF16)], [pl.BlockSpec((tw, tn), lambda i, j, k: (i, j))], (tw, tn))[0]

    def da_epilogue(acc, extra_refs, out_refs):
        out_refs[0][...] = (acc * (2.0 * extra_refs[0][...].astype(F32))).astype(BF16)

    big_blk = pl.BlockSpec((tm, tn), lambda i, j, k: (i, j))
    fl_f2, tok = _reduce_scatter_start("F2", [g_ff2.reshape(N_DEV, half, d)])
    df1 = _matmul(
        "da_ff", "nt", (s // tm, 4 * d // tn, d // tk), df, pl.BlockSpec((tm, tk), lambda i, j, k: (i, k)),
        w_ff2_f, pl.BlockSpec((tn, tk), lambda i, j, k: (j, k)),
        [_sds((s, 4 * d), BF16)], [big_blk], (tm, tn),
        epilogue=da_epilogue, extras=(relu,), extra_specs=(big_blk,), deps=(tok,))[0]

    fl_f2, tok = _reduce_scatter_middle("F2", fl_f2, df1, me)
    g_ff1 = _matmul(
        "g_ff1", "tn", (d // tw, N_DEV, s // tok_k), h2, pl.BlockSpec((tok_k, tw), lambda i, j, k: (k, i)),
        df1, pl.BlockSpec((tok_k, half), lambda i, j, k: (k, j)),
        [_sds((N_DEV, d, half), BF16)], [pl.BlockSpec((None, tw, half), lambda i, j, k: (j, i, 0))], (tw, half),
        deps=(tok,))[0]

    fl_f1, tok = _reduce_scatter_start("F1", [g_ff1])
    dh2 = _matmul(
        "dh2", "nt", (s // tm, d // tn, N_DEV), df1, pl.BlockSpec((tm, half), lambda i, j, k: (i, k)),
        w_ff1_f, pl.BlockSpec((None, tn, half), lambda i, j, k: (k, j, 0)),
        [_sds((s, d), F32)], [mn_blk], (tm, tn), deps=(tok,))[0]

    g_w_ff2, = _reduce_scatter_finish("F2", fl_f2, dh2)
    fl_f1, tok = _reduce_scatter_middle("F1", fl_f1, dh2, me)
    dx1, dshift2, dscale2, g_norm2 = _norm_backward("norm2_bwd", dh2, x1, norm2_w, scale2, dy, deps=(tok,))

    do, dgate1 = _gate_backward("gate1_bwd", dx1, gate1, o_act)
    g_o = _matmul(
        "g_o", "tn", (d // tw, d // tn, s // tok_k), merged, pl.BlockSpec((tok_k, tw), lambda i, j, k: (k, i)),
        do, pl.BlockSpec((tok_k, tn), lambda i, j, k: (k, j)),
        [_sds((d, d), BF16)], [pl.BlockSpec((tw, tn), lambda i, j, k: (i, j))], (tw, tn))[0]

    def merge_bwd_epilogue(acc, extra_refs, out_refs):
        ga_ref, gb_ref, ya_ref, yb_ref = extra_refs
        dya_ref, dyb_ref, dga_ref, dgb_ref = out_refs
        sa = jax.nn.sigmoid(ga_ref[...])
        sb = jax.nn.sigmoid(gb_ref[...])
        dya_ref[...] = (acc * sa).astype(BF16)
        dyb_ref[...] = (acc * sb).astype(BF16)
        dga_ref[...] = (acc * ya_ref[...].astype(F32) * (sa * (1.0 - sa))).astype(BF16)
        dgb_ref[...] = (acc * yb_ref[...].astype(F32) * (sb * (1.0 - sb))).astype(BF16)

    nb = d // te
    dy_a, dy_b, dproj, dg_b = _matmul(
        "dmerged", "nt", (s // tm, nb, d // tk), do, pl.BlockSpec((tm, tk), lambda i, j, k: (i, k)),
        w_o_f, pl.BlockSpec((te, tk), lambda i, j, k: (j, k)),
        [_sds((s, d), BF16), _sds((s, d), BF16), _sds((s, 4 * d), BF16), _sds((s, d), BF16)],
        [e_blk, e_blk, pl.BlockSpec((tm, te), lambda i, j, k: (i, 2 * nb + j)), e_blk], (tm, te),
        epilogue=merge_bwd_epilogue, extras=(proj, proj, y_a, y_b),
        extra_specs=(pl.BlockSpec((tm, te), lambda i, j, k: (i, 2 * nb + j)),
                     pl.BlockSpec((tm, te), lambda i, j, k: (i, 3 * nb + j)), e_blk, e_blk))
    dproj = _place_columns("place_dgb", dg_b, dproj, 3)

    up_a = pl.BlockSpec((tok_k, half), lambda i, j, k: (k, 0))
    up_b = pl.BlockSpec((tok_k, d8), lambda i, j, k: (k, j))
    up_o = pl.BlockSpec((None, half, d8), lambda i, j, k: (j, 0, 0))
    g_a_up = _matmul("g_a_up", "tn", (1, N_DEV, s // tok_k), ya_in, up_a, dy_a, up_b,
                     [_sds((N_DEV, half, d8), BF16)], [up_o], (half, d8))[0]
    g_b_up = _matmul("g_b_up", "tn", (1, N_DEV, s // tok_k), attn, up_a, dy_b, up_b,
                     [_sds((N_DEV, half, d8), BF16)], [up_o], (half, d8))[0]
    dn_a = pl.BlockSpec((tm, d8), lambda i, j, k: (i, k))
    dn_b = pl.BlockSpec((None, half, d8), lambda i, j, k: (k, 0, 0))
    dn_o = pl.BlockSpec((tm, half), lambda i, j, k: (i, 0))
    dya_in = _matmul("d_ya_in", "nt", (s // tm, 1, N_DEV), dy_a, dn_a, w_a_f, dn_b,
                     [_sds((s, half), BF16)], [dn_o], (tm, half))[0]
    dattn = _matmul("d_attn", "nt", (s // tm, 1, N_DEV), dy_b, dn_a, w_b_f, dn_b,
                    [_sds((s, half), BF16)], [dn_o], (tm, half))[0]

    dproj, g_pool, g_pool_scale = _pool_backward(dya_in, pooled, w_pool_f, pool_scale, dproj)
    g_w_ff1, = _reduce_scatter_finish("F1", fl_f1, g_pool)
    g_pool_send = g_pool.astype(BF16).reshape(n_groups, N_DEV, rows_pool, cg).transpose(1, 0, 2, 3)
    g_pool_send = g_pool_send.reshape(N_DEV, n_groups * rows_pool, cg)
    fl_b, tok = _reduce_scatter_start("B", [g_pool_send, g_a_up, g_b_up, g_o.reshape(N_DEV, d8, d)])
    dqn, dkn, dproj = _attention_backward(qn, kn, vb, dattn, dproj, 3, deps=(tok,))
    fl_b, tok = _reduce_scatter_middle("B", fl_b, dqn, me)
    dproj, g_qnorm = _qk_norm_backward("qnorm_bwd", dqn, proj, 1, q_norm_w, dproj, half, deps=(tok,))
    dproj, g_knorm = _qk_norm_backward("knorm_bwd", dkn, proj, 2, k_norm_w, dproj, half)

    g_in = _matmul(
        "g_in", "tn", (d // tw, N_DEV, s // tok_k), h, pl.BlockSpec((tok_k, tw), lambda i, j, k: (k, i)),
        dproj, pl.BlockSpec((tok_k, half), lambda i, j, k: (k, j)),
        [_sds((N_DEV, d, half), BF16)], [pl.BlockSpec((None, tw, half), lambda i, j, k: (j, i, 0))], (tw, half))[0]
    fl_in, tok = _reduce_scatter_start("I", [g_in])
    dh = _matmul(
        "dh", "nt", (s // tm, d // tn, N_DEV), dproj, pl.BlockSpec((tm, half), lambda i, j, k: (i, k)),
        w_in_f, pl.BlockSpec((None, tn, half), lambda i, j, k: (k, j, 0)),
        [_sds((s, d), F32)], [mn_blk], (tm, tn), deps=(tok,))[0]
    g_w_pool, g_w_a_up, g_w_b_up, g_w_o = _reduce_scatter_finish("B", fl_b, dh)
    grad_x, dshift1, dscale1, g_norm1 = _norm_backward("norm1_bwd", dh, x2, norm1_w, scale1, dx1)

    dmod = jnp.concatenate([dshift1, dscale1, dgate1, dshift2, dscale2, dgate2], axis=1)
    pieces = [dmod, g_norm1, g_norm2, g_pool_scale, g_qnorm, g_knorm]
    packed_rows = [_rows_of_lanes(p) for p in pieces]
    offsets = [0]
    for p in packed_rows:
        offsets.append(offsets[-1] + p.shape[0])
    packed = jnp.concatenate(packed_rows, axis=0)
    small_all = _all_gather_2d("ag_small", packed)
    fl_in, tok = _reduce_scatter_middle("I", fl_in, small_all, me)
    small_sum = _sum_slots("small_sum", small_all[None], F32)[0]

    def unpack(i, width):
        return small_sum[offsets[i]:offsets[i] + width // LANES].reshape(1, width)

    g_b_ada = unpack(0, N_MOD * d)
    g_norm1_w = unpack(1, d)
    g_norm2_w = unpack(2, d)
    g_pool_scale_w = unpack(3, half)
    g_q_norm_w = unpack(4, HEAD_DIM)
    g_k_norm_w = unpack(5, HEAD_DIM)
    dmod_all = small_all[:, :N_MOD * d // LANES].reshape(N_DEV, N_MOD * d)
    dmod_cols = lax.dynamic_slice_in_dim(dmod_all, me * wa, wa, axis=1)
    g_w_ada = _ada_weight_grad(c_all, dmod_cols, deps=(tok,))[None]


    grads = {
        "w_ada": g_w_ada, "b_ada": g_b_ada, "norm1_w": g_norm1_w,
        "q_norm_w": g_q_norm_w, "k_norm_w": g_k_norm_w,
        "w_pool": g_w_pool.reshape(w_pool.shape), "pool_scale": g_pool_scale_w,
        "w_a_up": g_w_a_up[None], "w_b_up": g_w_b_up[None], "w_o": g_w_o[None],
        "norm2_w": g_norm2_w, "w_ff1": g_w_ff1[None], "w_ff2": g_w_ff2[None],
    }
    weights = {"w_ada": (w_ada, m_w_ada, v_w_ada), "b_ada": (b_ada, m_b_ada, v_b_ada),
               "norm1_w": (norm1_w, m_norm1_w, v_norm1_w), "w_in": (w_in, m_w_in, v_w_in),
               "q_norm_w": (q_norm_w, m_q_norm_w, v_q_norm_w), "k_norm_w": (k_norm_w, m_k_norm_w, v_k_norm_w),
               "w_pool": (w_pool, m_w_pool, v_w_pool), "pool_scale": (pool_scale, m_pool_scale, v_pool_scale),
               "w_a_up": (w_a_up, m_w_a_up, v_w_a_up), "w_b_up": (w_b_up, m_w_b_up, v_w_b_up),
               "w_o": (w_o, m_w_o, v_w_o), "norm2_w": (norm2_w, m_norm2_w, v_norm2_w),
               "w_ff1": (w_ff1, m_w_ff1, v_w_ff1), "w_ff2": (w_ff2, m_w_ff2, v_w_ff2)}
    order = list(weights)
    deltas, new_m, new_v = {}, {}, {}
    def adam(name):
        wt, mt, vt = weights[name]
        shape = wt.shape
        flat = (-1, shape[-1])
        dl, nm, nv = _adamw("adamw_" + name, wt.reshape(flat), grads[name].reshape(flat),
                            mt.reshape(flat), vt.reshape(flat))
        deltas[name], new_m[name], new_v[name] = dl.reshape(shape), nm.reshape(shape), nv.reshape(shape)

    others = [n for n in order if n != "w_in"]
    for name in others:
        adam(name)
    g_w_in, = _reduce_scatter_finish("I", fl_in, [deltas[n] for n in others])
    grads["w_in"] = g_w_in[None]
    adam("w_in")

    return (loss, grad_x[None], *[grads[n] for n in order], *[deltas[n] for n in order],
            *[new_m[n] for n in order], *[new_v[n] for n in order])
```

```python
import math

import jax
import jax.numpy as jnp
from jax import lax
from jax.experimental import pallas as pl
from jax.experimental.pallas import tpu as pltpu

F32 = jnp.float32
BF16 = jnp.bfloat16
MESH_AXES = ("x", "y", "c")
N_DEV = 8
HEAD_DIM = 128
POOL_WINDOWS = (2, 4, 8, 16)
N_MOD = 6
NORM_EPS = 1e-6
LANES = 128
SUBLANES = 8
VMEM_LIMIT_BYTES = 56 * 1024 * 1024
Q_TILE = 256
K_TILE = 256
POOL_TILE = 256
HEADS_PER_STEP = 4
HEADS_PER_STEP_BWD = 4

ADAM_LR = 0.001
ADAM_B1 = 0.9
ADAM_B2 = 0.999
ADAM_EPS = 1e-08
ADAM_WD = 0.01
ADAM_STEP = 10

_NN = (((1,), (0,)), ((), ()))
_NT = (((1,), (1,)), ((), ()))
_TN = (((0,), (0,)), ((), ()))
_DIMS = {"nn": _NN, "nt": _NT, "tn": _TN}


def _dot(a, b, mode="nn"):
    return lax.dot_general(a, b, _DIMS[mode], preferred_element_type=F32)


def _params(*sem):
    return pltpu.CompilerParams(dimension_semantics=sem, vmem_limit_bytes=VMEM_LIMIT_BYTES)


def _tile(dim, pref, align=SUBLANES):
    for t in range(min(dim, pref), 0, -1):
        if dim % t == 0 and t % align == 0:
            return t
    return dim


def _group_index(axes):
    idx = 0
    for a in axes:
        idx = idx * 2 + lax.axis_index(a)
    return idx


def _peer_device(axes, k):
    coords = {a: lax.axis_index(a) for a in MESH_AXES}
    for pos, a in enumerate(axes):
        if (k >> (len(axes) - 1 - pos)) & 1:
            coords[a] = 1 - coords[a]
    return tuple(coords[a] for a in MESH_AXES)


_AXIS_BIT = {"x": 4, "y": 2, "c": 1}
_ANY = pl.BlockSpec(memory_space=pl.ANY)


def _device_xor(mask):
    return tuple(1 - lax.axis_index(a) if mask & _AXIS_BIT[a] else lax.axis_index(a) for a in MESH_AXES)


def _remote(src, dst, send_sem, recv_sem, mask):
    return pltpu.make_async_remote_copy(src_ref=src, dst_ref=dst, send_sem=send_sem, recv_sem=recv_sem,
                                        device_id=_device_xor(mask), device_id_type=pl.DeviceIdType.MESH)


def _start_all(copies):
    for cp in copies:
        cp.start()


def _wait_all(copies):
    for cp in copies:
        cp.wait()


def _gather_ici(name, bufs):
    na = len(bufs)
    bx, by = _AXIS_BIT["x"], _AXIS_BIT["y"]

    def body(*refs):
        out = refs[na:2 * na]
        send_sems, recv_sems = refs[2 * na:]
        me = _group_index(MESH_AXES)

        def copy(a, n, slot, color, mask):
            half = out[a].shape[1] // 2
            piece = out[a].at[slot, pl.ds(color * half, half)]
            return _remote(piece, piece, send_sems.at[a, n], recv_sems.at[a, n], mask)

        first, second = [], []
        for a in range(na):
            first += [copy(a, 0, me, 0, bx), copy(a, 1, me, 1, by)]
            second += [copy(a, 2, me, 0, by), copy(a, 3, me ^ bx, 0, by),
                       copy(a, 4, me, 1, bx), copy(a, 5, me ^ by, 1, bx)]
        _start_all(first)
        _wait_all(first)
        _start_all(second)
        _wait_all(second)

    return pl.pallas_call(
        body, name=name, out_shape=tuple(jax.ShapeDtypeStruct(b.shape, b.dtype) for b in bufs),
        in_specs=[_ANY] * na, out_specs=tuple([_ANY] * na),
        input_output_aliases={a: a for a in range(na)},
        scratch_shapes=[pltpu.SemaphoreType.DMA((na, 6)), pltpu.SemaphoreType.DMA((na, 6))],
    )(*bufs)


def _gather_d2d(name, bufs):
    na = len(bufs)
    masks = (0, _AXIS_BIT["y"], _AXIS_BIT["x"], _AXIS_BIT["x"] | _AXIS_BIT["y"])

    def body(*refs):
        out = refs[na:2 * na]
        send_sems, recv_sems = refs[2 * na:]
        me = _group_index(MESH_AXES)
        copies = []
        for a in range(na):
            for n, m in enumerate(masks):
                piece = out[a].at[me ^ m]
                copies.append(_remote(piece, piece, send_sems.at[a, n], recv_sems.at[a, n], _AXIS_BIT["c"]))
        _start_all(copies)
        _wait_all(copies)

    return pl.pallas_call(
        body, name=name, out_shape=tuple(jax.ShapeDtypeStruct(b.shape, b.dtype) for b in bufs),
        in_specs=[_ANY] * na, out_specs=tuple([_ANY] * na),
        input_output_aliases={a: a for a in range(na)},
        scratch_shapes=[pltpu.SemaphoreType.DMA((na, 4)), pltpu.SemaphoreType.DMA((na, 4))],
    )(*bufs)


def _scatter(name, srcs, send_slots, masks):
    na = len(srcs)
    nm = len(masks)

    def body(*refs):
        src, out = refs[:na], refs[na:2 * na]
        send_sems, recv_sems = refs[2 * na:]
        slots = send_slots(_group_index(MESH_AXES))
        copies = []
        for n, m in enumerate(masks):
            for a in range(na):
                copies.append(_remote(src[a].at[slots[n]], out[a].at[n],
                                      send_sems.at[a, n], recv_sems.at[a, n], m))
        _start_all(copies)
        _wait_all(copies)

    return pl.pallas_call(
        body, name=name,
        out_shape=tuple(jax.ShapeDtypeStruct((nm,) + s.shape[1:], s.dtype) for s in srcs),
        in_specs=[_ANY] * na, out_specs=tuple([_ANY] * na),
        scratch_shapes=[pltpu.SemaphoreType.DMA((na, nm)), pltpu.SemaphoreType.DMA((na, nm))],
    )(*srcs)


def _add_received(name, own, own_slots, received, out_dtype):
    nj, r, c = received.shape
    tr = _tile(r, max(2 * SUBLANES, (1 << 20) // c), 2 * SUBLANES)

    def body(slots_ref, own_ref, rec_ref, o_ref):
        del slots_ref
        o_ref[...] = (own_ref[...].astype(F32) + rec_ref[...].astype(F32)).astype(o_ref.dtype)

    grid_spec = pltpu.PrefetchScalarGridSpec(
        num_scalar_prefetch=1, grid=(nj, r // tr),
        in_specs=[pl.BlockSpec((None, tr, c), lambda j, i, slots: (slots[j], i, 0)),
                  pl.BlockSpec((None, tr, c), lambda j, i, slots: (j, i, 0))],
        out_specs=pl.BlockSpec((None, tr, c), lambda j, i, slots: (j, i, 0)))
    return pl.pallas_call(
        body, name=name, grid_spec=grid_spec, out_shape=jax.ShapeDtypeStruct((nj, r, c), out_dtype),
        compiler_params=_params("parallel", "parallel"),
    )(own_slots, own, received)


def _add_final(name, own, received):
    nj, r, c = received.shape
    tr = _tile(r, max(2 * SUBLANES, (1 << 19) // c), 2 * SUBLANES)

    def body(own_ref, rec_ref, o_ref):
        acc = own_ref[...].astype(F32)
        for j in range(nj):
            acc = acc + rec_ref[j].astype(F32)
        o_ref[...] = acc

    return pl.pallas_call(
        body, name=name, grid=(r // tr,), out_shape=jax.ShapeDtypeStruct((r, c), F32),
        in_specs=[pl.BlockSpec((None, tr, c), lambda i: (0, i, 0)),
                  pl.BlockSpec((nj, tr, c), lambda i: (0, i, 0))],
        out_specs=pl.BlockSpec((tr, c), lambda i: (i, 0)),
        compiler_params=_params("parallel"),
    )(own, received)


_HBM = pl.BlockSpec(memory_space=pltpu.HBM)
_SEM = pl.BlockSpec(memory_space=pltpu.SEMAPHORE)
_DATAFLOW = pltpu.SideEffectType.DATAFLOW_SIDE_EFFECTING


def _launch(name, bufs, plan, n_copies):
    nb = len(bufs)

    def body(*refs):
        ins = refs[:nb]
        send_sems, recv_sems = refs[nb], refs[nb + 1]
        token = refs[-1]
        copies = plan(ins, _group_index(MESH_AXES))
        assert len(copies) == n_copies
        for n, (src, dst, mask) in enumerate(copies):
            _remote(src, dst, send_sems.at[n], recv_sems.at[n], mask).start()
        token[...] = jnp.zeros_like(token)

    outs = pl.pallas_call(
        body, name=name,
        out_shape=(pltpu.SemaphoreType.DMA((n_copies,)), pltpu.SemaphoreType.DMA((n_copies,)),
                   *[pltpu.HBM(b.shape, b.dtype) for b in bufs], jax.ShapeDtypeStruct((SUBLANES, LANES), F32)),
        in_specs=[_HBM] * nb,
        out_specs=(_SEM, _SEM, *[_HBM] * nb, pl.BlockSpec(memory_space=pltpu.VMEM)),
        input_output_aliases={i: 2 + i for i in range(nb)},
        compiler_params=pltpu.CompilerParams(has_side_effects=_DATAFLOW),
    )(*[pltpu.with_memory_space_constraint(b, pltpu.HBM) for b in bufs])
    return (name, plan, n_copies, outs[0], outs[1], list(outs[2:2 + nb])), outs[-1]


def _land(flight, after):
    name, plan, n_copies, send_sems, recv_sems, bufs = flight
    nb = len(bufs)
    after = list(after) if isinstance(after, (list, tuple)) else [after]

    def body(*refs):
        ins = refs[:nb]
        s_sems, r_sems = refs[nb], refs[nb + 1]
        for n, (src, dst, mask) in enumerate(plan(ins, _group_index(MESH_AXES))):
            cp = _remote(src, dst, s_sems.at[n], r_sems.at[n], mask)
            cp.wait_send()
            cp.wait_recv()

    outs = pl.pallas_call(
        body, name=name + "_land",
        out_shape=tuple(pltpu.HBM(b.shape, b.dtype) for b in bufs),
        in_specs=[_HBM] * nb + [_SEM, _SEM] + [_ANY] * len(after), out_specs=tuple([_HBM] * nb),
        input_output_aliases={i: i for i in range(nb)},
        compiler_params=pltpu.CompilerParams(has_side_effects=_DATAFLOW),
    )(*bufs, send_sems, recv_sems, *after)
    return list(outs)


def _plan_gather_ici(phase):
    bx, by = _AXIS_BIT["x"], _AXIS_BIT["y"]

    def plan(refs, me):
        copies = []
        for ref in refs:
            half = ref.shape[1] // 2

            def piece(slot, color, mask, ref=ref, half=half):
                p = ref.at[slot, pl.ds(color * half, half)]
                return (p, p, mask)

            if phase == 0:
                copies += [piece(me, 0, bx), piece(me, 1, by)]
            else:
                copies += [piece(me, 0, by), piece(me ^ bx, 0, by), piece(me, 1, bx), piece(me ^ by, 1, bx)]
        return copies

    return plan


def _plan_gather_d2d(refs, me):
    copies = []
    for ref in refs:
        for m in _CHIP_MASKS:
            copies.append((ref.at[me ^ m], ref.at[me ^ m], _AXIS_BIT["c"]))
    return copies


def _plan_scatter_d2d(refs, me):
    na = len(refs) // 2
    copies = []
    for a in range(na):
        for j, m in enumerate(_CHIP_MASKS):
            copies.append((refs[a].at[me ^ _AXIS_BIT["c"] ^ m], refs[na + a].at[j], _AXIS_BIT["c"]))
    return copies


def _plan_scatter_ici(refs, me):
    del me
    na = len(refs) // 2
    copies = []
    for a in range(na):
        for n, m in enumerate(_CHIP_MASKS[1:]):
            copies.append((refs[a].at[n + 1], refs[na + a].at[n], m))
    return copies


def _with_deps(body, n_in, deps):
    if not deps:
        return body

    def wrapped(*refs):
        return body(*refs[:n_in], *refs[n_in + len(deps):])

    return wrapped


def _reduce_scatter_start(tag, grads):
    lands = [lax.empty((len(_CHIP_MASKS),) + g.shape[1:], g.dtype) for g in grads]
    return _launch("rs%s_d2d" % tag, list(grads) + lands, _plan_scatter_d2d, len(_CHIP_MASKS) * len(grads))


def _reduce_scatter_middle(tag, flight, after, me):
    bufs = _land(flight, after)
    na = len(bufs) // 2
    own_slots = jnp.stack([me ^ m for m in _CHIP_MASKS]).astype(jnp.int32)
    sums = [_add_received("rs%s_add_d2d_%d" % (tag, a), bufs[a], own_slots, bufs[na + a], BF16) for a in range(na)]
    lands = [lax.empty((len(_CHIP_MASKS) - 1,) + h.shape[1:], h.dtype) for h in sums]
    return _launch("rs%s_ici" % tag, sums + lands, _plan_scatter_ici, (len(_CHIP_MASKS) - 1) * na)


def _reduce_scatter_finish(tag, flight, after):
    bufs = _land(flight, after)
    na = len(bufs) // 2
    return [_add_final("rs%s_add_ici_%d" % (tag, a), bufs[a], bufs[na + a]) for a in range(na)]


def _all_gather_2d(name, x, deps=()):
    r, c = x.shape

    def body(x_ref, out_ref, send_sems, recv_sems):
        me = _group_index(MESH_AXES)
        out_ref[me] = x_ref[...]
        copies = []
        for k in range(1, N_DEV):
            cp = pltpu.make_async_remote_copy(
                src_ref=x_ref, dst_ref=out_ref.at[me],
                send_sem=send_sems.at[k - 1], recv_sem=recv_sems.at[k - 1],
                device_id=_peer_device(MESH_AXES, k), device_id_type=pl.DeviceIdType.MESH)
            cp.start()
            copies.append(cp)
        for cp in copies:
            cp.wait()

    vmem = pl.BlockSpec(memory_space=pltpu.VMEM)
    return pl.pallas_call(
        _with_deps(body, 1, deps), name=name, out_shape=jax.ShapeDtypeStruct((N_DEV, r, c), x.dtype),
        in_specs=[vmem] + [_ANY] * len(deps), out_specs=vmem,
        scratch_shapes=[pltpu.SemaphoreType.DMA((N_DEV - 1,)), pltpu.SemaphoreType.DMA((N_DEV - 1,))],
    )(x, *deps)


def _sum_slots(name, buf, out_dtype):
    pre, n, r, c = buf.shape
    tr = _tile(r, max(SUBLANES * 2, (1 << 20) // c))

    def body(b_ref, o_ref):
        acc = b_ref[0].astype(F32)
        for q in range(1, n):
            acc = acc + b_ref[q].astype(F32)
        o_ref[...] = acc.astype(o_ref.dtype)

    return pl.pallas_call(
        body, name=name, grid=(pre, r // tr),
        out_shape=jax.ShapeDtypeStruct((pre, r, c), out_dtype),
        in_specs=[pl.BlockSpec((None, n, tr, c), lambda i, j: (i, 0, j, 0))],
        out_specs=pl.BlockSpec((None, tr, c), lambda i, j: (i, j, 0)),
        compiler_params=_params("parallel", "parallel"),
    )(buf)


def _all_gather_weights(bufs):
    return _gather_d2d("ag_d2d", list(_gather_ici("ag_ici", bufs)))


_CHIP_MASKS = (0, _AXIS_BIT["y"], _AXIS_BIT["x"], _AXIS_BIT["x"] | _AXIS_BIT["y"])


def _reduce_scatter_grads(bufs):
    me = _group_index(MESH_AXES)
    bc = _AXIS_BIT["c"]
    r1 = _scatter("rs_d2d", bufs, lambda i: [i ^ bc ^ m for m in _CHIP_MASKS], (bc,) * len(_CHIP_MASKS))
    own_slots = jnp.stack([me ^ m for m in _CHIP_MASKS]).astype(jnp.int32)
    half = [_add_received("rs_add_d2d_%d" % a, b, own_slots, r, BF16) for a, (b, r) in enumerate(zip(bufs, r1))]
    r2 = _scatter("rs_ici", half, lambda i: [1, 2, 3], _CHIP_MASKS[1:])
    return [_add_final("rs_add_ici_%d" % a, h, r) for a, (h, r) in enumerate(zip(half, r2))]


def _matmul(name, mode, grid, a, a_spec, b, b_spec, out_shapes, out_specs, acc_shape,
            epilogue=None, extras=(), extra_specs=(), aliases=None, deps=()):
    nk = grid[2]
    n_extra = len(extras)
    n_out = len(out_shapes)

    def finish(acc, extra_refs, out_refs):
        if epilogue is None:
            out_refs[0][...] = acc.astype(out_refs[0].dtype)
        else:
            epilogue(acc, extra_refs, out_refs)

    def body(*refs):
        a_ref, b_ref = refs[0], refs[1]
        extra_refs = refs[2:2 + n_extra]
        out_refs = refs[2 + n_extra:2 + n_extra + n_out]
        if nk == 1:
            finish(_dot(a_ref[...], b_ref[...], mode), extra_refs, out_refs)
            return
        acc_ref = refs[-1]
        k = pl.program_id(2)

        @pl.when(k == 0)
        def _():
            acc_ref[...] = jnp.zeros_like(acc_ref)

        acc_ref[...] += _dot(a_ref[...], b_ref[...], mode)

        @pl.when(k == nk - 1)
        def _():
            finish(acc_ref[...], extra_refs, out_refs)

    scratch = [] if nk == 1 else [pltpu.VMEM(acc_shape, F32)]
    return pl.pallas_call(
        _with_deps(body, 2 + n_extra, deps), name=name, grid=grid, out_shape=tuple(out_shapes),
        in_specs=[a_spec, b_spec] + list(extra_specs) + [_ANY] * len(deps), out_specs=tuple(out_specs),
        scratch_shapes=scratch, input_output_aliases=aliases or {},
        compiler_params=_params("parallel", "parallel", "arbitrary"),
    )(a, b, *extras, *deps)


def _sds(shape, dtype):
    return jax.ShapeDtypeStruct(tuple(shape), dtype)


def _ada_forward(c_all, w_ada, b_shard):
    nb, d = c_all.shape
    w = w_ada.shape[1]
    tn = _tile(w, 512)

    def body(c_ref, w_ref, b_ref, o_ref):
        cv = c_ref[...]
        sc = cv * jax.nn.sigmoid(cv)
        o_ref[...] = jnp.dot(sc, w_ref[...], precision=lax.Precision.HIGHEST,
                             preferred_element_type=F32) + b_ref[...]

    return pl.pallas_call(
        body, name="ada_fwd", grid=(w // tn,), out_shape=_sds((nb, w), F32),
        in_specs=[pl.BlockSpec((nb, d), lambda j: (0, 0)), pl.BlockSpec((d, tn), lambda j: (0, j)),
                  pl.BlockSpec((1, tn), lambda j: (0, j))],
        out_specs=pl.BlockSpec((nb, tn), lambda j: (0, j)),
        compiler_params=_params("parallel"),
    )(c_all, w_ada, b_shard)


def _ada_weight_grad(c_all, dmod_cols, deps=()):
    nb, d = c_all.shape
    w = dmod_cols.shape[1]
    tn = _tile(w, 512)

    def body(c_ref, g_ref, o_ref):
        cv = c_ref[...]
        sc = cv * jax.nn.sigmoid(cv)
        o_ref[...] = lax.dot_general(sc, g_ref[...], _TN, precision=lax.Precision.HIGHEST,
                                     preferred_element_type=F32)

    return pl.pallas_call(
        _with_deps(body, 2, deps), name="ada_wgrad", grid=(w // tn,), out_shape=_sds((d, w), F32),
        in_specs=[pl.BlockSpec((nb, d), lambda j: (0, 0)), pl.BlockSpec((nb, tn), lambda j: (0, j))]
        + [_ANY] * len(deps),
        out_specs=pl.BlockSpec((d, tn), lambda j: (0, j)),
        compiler_params=_params("parallel"),
    )(c_all, dmod_cols, *deps)


def _norm_forward(name, x, norm_w, scale, shift, deps=()):
    s, d = x.shape
    tm = _tile(s, 256)

    def body(x_ref, w_ref, sc_ref, sh_ref, h_ref):
        xv = x_ref[...]
        r = lax.rsqrt(jnp.mean(xv * xv, axis=-1, keepdims=True) + NORM_EPS)
        h = (xv * r * w_ref[...]) * (1.0 + sc_ref[...]) + sh_ref[...]
        h_ref[...] = h.astype(BF16)

    vec = pl.BlockSpec((1, d), lambda i: (0, 0))
    row = pl.BlockSpec((tm, d), lambda i: (i, 0))
    return pl.pallas_call(
        _with_deps(body, 4, deps), name=name, grid=(s // tm,), out_shape=_sds((s, d), BF16),
        in_specs=[row, vec, vec, vec] + [_ANY] * len(deps), out_specs=row, compiler_params=_params("parallel"),
    )(x, norm_w, scale, shift, *deps)


def _norm_backward(name, dh, x, norm_w, scale, dres, gated=None, deps=()):
    s, d = x.shape
    tm = _tile(s, 256)
    n_in = 7 if gated else 5

    def body(*refs):
        dh_ref, x_ref, w_ref, sc_ref, dres_ref = refs[:5]
        dx_ref, dshift_ref, dscale_ref, dw_ref = refs[n_in:n_in + 4]
        sums = (dshift_ref, dscale_ref, dw_ref) + ((refs[n_in + 5],) if gated else ())

        @pl.when(pl.program_id(0) == 0)
        def _():
            for ref in sums:
                ref[...] = jnp.zeros_like(ref)

        xv = x_ref[...]
        g = dh_ref[...]
        r = lax.rsqrt(jnp.mean(xv * xv, axis=-1, keepdims=True) + NORM_EPS)
        n = xv * r
        gain = 1.0 + sc_ref[...]
        gn = g * n
        dshift_ref[...] += jnp.sum(g, axis=0, keepdims=True)
        dscale_ref[...] += jnp.sum(gn, axis=0, keepdims=True) * w_ref[...]
        dw_ref[...] += jnp.sum(gn, axis=0, keepdims=True) * gain
        dn = g * (w_ref[...] * gain)
        dx = dres_ref[...] + r * (dn - n * jnp.mean(dn * n, axis=-1, keepdims=True))
        dx_ref[...] = dx
        if gated:
            gate_ref, other_ref = refs[5:7]
            refs[n_in + 4][...] = (dx * gate_ref[...]).astype(BF16)
            refs[n_in + 5][...] += jnp.sum(dx * other_ref[...].astype(F32), axis=0, keepdims=True)

    vec = pl.BlockSpec((1, d), lambda i: (0, 0))
    row = pl.BlockSpec((tm, d), lambda i: (i, 0))
    vec_out = _sds((1, d), F32)
    return pl.pallas_call(
        _with_deps(body, n_in, deps), name=name, grid=(s // tm,),
        out_shape=(_sds((s, d), F32), vec_out, vec_out, vec_out) + ((_sds((s, d), BF16), vec_out) if gated else ()),
        in_specs=[row, row, vec, vec, row] + ([vec, row] if gated else []) + [_ANY] * len(deps),
        out_specs=(row, vec, vec, vec) + ((row, vec) if gated else ()),
        compiler_params=_params("arbitrary"),
    )(dh, x, norm_w, scale, dres, *(gated or ()), *deps)


def _split_bf16(v):
    hi = v.astype(BF16)
    lo = (v - hi.astype(F32)).astype(BF16)
    return hi, lo


def _pool_forward(proj, w_pool, pool_scale, deps=()):
    s = proj.shape[0]
    g_n, cg, _ = w_pool.shape
    t = POOL_TILE
    nt = s // t

    def body(cur_ref, prev_ref, wp_ref, sc_ref, pooled_ref, ya_ref):
        g = pl.program_id(0)
        ti = pl.program_id(1)
        win = jnp.left_shift(2, g)
        row = lax.broadcasted_iota(jnp.int32, (t, t), 0)
        col = lax.broadcasted_iota(jnp.int32, (t, t), 1)
        lag = row - col
        band_cur = ((lag >= 0) & (lag < win)).astype(BF16)
        band_prev = ((lag + t < win) & (ti > 0)).astype(BF16)
        u = cur_ref[...]
        u_hi, u_lo = _split_bf16(u)
        p_hi, p_lo = _split_bf16(prev_ref[...])
        wsum = (_dot(band_cur, u_hi) + _dot(band_cur, u_lo)
                + _dot(band_prev, p_hi) + _dot(band_prev, p_lo))
        tok = ti * t + lax.broadcasted_iota(jnp.int32, (t, 1), 0)
        count = jnp.minimum(tok + 1, win).astype(F32)
        pooled = (wsum / count - u).astype(BF16)
        pooled_ref[...] = pooled
        ya_ref[...] = (_dot(pooled, wp_ref[...]) * sc_ref[...]).astype(BF16)

    blk = pl.BlockSpec((t, cg), lambda g, i: (i, g))
    return pl.pallas_call(
        _with_deps(body, 4, deps), name="pool_fwd", grid=(g_n, nt),
        out_shape=(_sds((s, g_n * cg), BF16), _sds((s, g_n * cg), BF16)),
        in_specs=[blk, pl.BlockSpec((t, cg), lambda g, i: (jnp.maximum(i - 1, 0), g)),
                  pl.BlockSpec((None, cg, cg), lambda g, i: (g, 0, 0)),
                  pl.BlockSpec((1, cg), lambda g, i: (0, g))] + [_ANY] * len(deps),
        out_specs=(blk, blk), compiler_params=_params("parallel", "parallel"),
    )(proj, proj, w_pool, pool_scale, *deps)


def _pool_backward(dya, pooled, w_pool, pool_scale, dproj):
    s = dya.shape[0]
    g_n, cg, _ = w_pool.shape
    t = POOL_TILE
    nt = s // t

    def body(dya_ref, dya_next_ref, pooled_ref, wp_ref, sc_ref, dproj_in, du_ref, gw_ref, gs_ref):
        del dproj_in
        g = pl.program_id(0)
        ti = pl.program_id(1)

        @pl.when(ti == 0)
        def _():
            gw_ref[...] = jnp.zeros_like(gw_ref)
            gs_ref[...] = jnp.zeros_like(gs_ref)

        win = jnp.left_shift(2, g)
        wp = wp_ref[...]
        sc = sc_ref[...]
        pooled_v = pooled_ref[...]
        dya_v = dya_ref[...].astype(F32)
        mixed = _dot(pooled_v, wp)
        gs_ref[...] += jnp.sum(dya_v * mixed, axis=0, keepdims=True)
        dmixed = (dya_v * sc).astype(BF16)
        gw_ref[...] += _dot(pooled_v, dmixed, "tn")
        dpooled = _dot(dmixed, wp, "nt")
        dmixed_next = (dya_next_ref[...].astype(F32) * sc).astype(BF16)
        dpooled_next = _dot(dmixed_next, wp, "nt")
        tok = ti * t + lax.broadcasted_iota(jnp.int32, (t, 1), 0)
        e_cur = dpooled / jnp.minimum(tok + 1, win).astype(F32)
        e_next = dpooled_next / jnp.minimum(tok + t + 1, win).astype(F32)
        row = lax.broadcasted_iota(jnp.int32, (t, t), 0)
        col = lax.broadcasted_iota(jnp.int32, (t, t), 1)
        lead = col - row
        band_cur = ((lead >= 0) & (lead < win)).astype(BF16)
        band_next = ((lead + t < win) & (ti < nt - 1)).astype(BF16)
        c_hi, c_lo = _split_bf16(e_cur)
        n_hi, n_lo = _split_bf16(e_next)
        du = (_dot(band_cur, c_hi) + _dot(band_cur, c_lo)
              + _dot(band_next, n_hi) + _dot(band_next, n_lo)) - dpooled
        du_ref[...] = du.astype(BF16)

    blk = pl.BlockSpec((t, cg), lambda g, i: (i, g))
    du, gw, gs = pl.pallas_call(
        body, name="pool_bwd", grid=(g_n, nt),
        out_shape=(_sds(dproj.shape, BF16), _sds((g_n, cg, cg), F32), _sds((1, g_n * cg), F32)),
        in_specs=[blk, pl.BlockSpec((t, cg), lambda g, i: (jnp.minimum(i + 1, nt - 1), g)), blk,
                  pl.BlockSpec((None, cg, cg), lambda g, i: (g, 0, 0)),
                  pl.BlockSpec((1, cg), lambda g, i: (0, g)),
                  pl.BlockSpec(memory_space=pl.ANY)],
        out_specs=(blk, pl.BlockSpec((None, cg, cg), lambda g, i: (g, 0, 0)),
                   pl.BlockSpec((1, cg), lambda g, i: (0, g))),
        input_output_aliases={5: 0}, compiler_params=_params("parallel", "arbitrary"),
    )(dya, dya, pooled, w_pool, pool_scale, dproj)
    return du, gw, gs


def _qkv_prepare(proj, q_norm_w, k_norm_w, width, deps=()):
    s = proj.shape[0]
    tm = _tile(s, 256)
    heads = width // HEAD_DIM

    def body(q_ref, k_ref, v_ref, qw_ref, kw_ref, qn_ref, kn_ref, vb_ref):
        for h in range(heads):
            cols = slice(h * HEAD_DIM, (h + 1) * HEAD_DIM)
            for src, w_ref, dst in ((q_ref, qw_ref, qn_ref), (k_ref, kw_ref, kn_ref)):
                v = src[:, cols]
                r = lax.rsqrt(jnp.mean(v * v, axis=-1, keepdims=True) + NORM_EPS)
                dst[:, cols] = (v * r * w_ref[...]).astype(BF16)
        vb_ref[...] = v_ref[...].astype(BF16)

    vec = pl.BlockSpec((1, HEAD_DIM), lambda i: (0, 0))
    out_spec = pl.BlockSpec((tm, width), lambda i: (i, 0))
    return pl.pallas_call(
        _with_deps(body, 5, deps), name="qkv_prep", grid=(s // tm,),
        out_shape=(_sds((s, width), BF16),) * 3,
        in_specs=[pl.BlockSpec((tm, width), lambda i: (i, 1)), pl.BlockSpec((tm, width), lambda i: (i, 2)),
                  pl.BlockSpec((tm, width), lambda i: (i, 3)), vec, vec] + [_ANY] * len(deps),
        out_specs=(out_spec,) * 3, compiler_params=_params("parallel"),
    )(proj, proj, proj, q_norm_w, k_norm_w, *deps)


def _qk_norm_backward(name, dn, proj, col_block, norm_w, dproj, width, deps=()):
    s = proj.shape[0]
    tm = _tile(s, 256)
    heads = width // HEAD_DIM

    def body(dn_ref, q_ref, w_ref, dproj_in, dq_ref, gw_ref):
        del dproj_in

        @pl.when(pl.program_id(0) == 0)
        def _():
            gw_ref[...] = jnp.zeros_like(gw_ref)

        wv = w_ref[...]
        gw = jnp.zeros((1, HEAD_DIM), F32)
        for h in range(heads):
            cols = slice(h * HEAD_DIM, (h + 1) * HEAD_DIM)
            v = q_ref[:, cols]
            g = dn_ref[:, cols]
            r = lax.rsqrt(jnp.mean(v * v, axis=-1, keepdims=True) + NORM_EPS)
            n = v * r
            gw = gw + jnp.sum(g * n, axis=0, keepdims=True)
            gn = g * wv
            dq_ref[:, cols] = (r * (gn - n * jnp.mean(gn * n, axis=-1, keepdims=True))).astype(BF16)
        gw_ref[...] += gw

    blk = pl.BlockSpec((tm, width), lambda i: (i, col_block))
    return pl.pallas_call(
        _with_deps(body, 4, deps), name=name, grid=(s // tm,),
        out_shape=(_sds(dproj.shape, BF16), _sds((1, HEAD_DIM), F32)),
        in_specs=[pl.BlockSpec((tm, width), lambda i: (i, 0)), blk,
                  pl.BlockSpec((1, HEAD_DIM), lambda i: (0, 0)), pl.BlockSpec(memory_space=pl.ANY)]
        + [_ANY] * len(deps),
        out_specs=(blk, pl.BlockSpec((1, HEAD_DIM), lambda i: (0, 0))),
        input_output_aliases={3: 0}, compiler_params=_params("arbitrary"),
    )(dn, proj, norm_w, dproj, *deps)


def _strict_upper(n):
    row = lax.broadcasted_iota(jnp.int32, (n, n), 0)
    col = lax.broadcasted_iota(jnp.int32, (n, n), 1)
    return (row > col).astype(BF16)


def _strict_lower(n):
    row = lax.broadcasted_iota(jnp.int32, (n, n), 0)
    col = lax.broadcasted_iota(jnp.int32, (n, n), 1)
    return (row < col).astype(BF16)


def _cumulate(v, tri):
    hi, lo = _split_bf16(v)
    return _dot(hi, tri) + _dot(lo, tri)


def _log_sigmoid(z):
    return jnp.minimum(z, 0.0) - jnp.log(1.0 + jnp.exp(-jnp.abs(z)))


def _attention_forward(qn, kn, vb):
    s, width = qn.shape
    heads = width // HEAD_DIM
    tq, tk = Q_TILE, K_TILE
    hp = HEADS_PER_STEP
    assert tq == tk and s % tq == 0 and heads % hp == 0
    scale = 1.0 / math.sqrt(HEAD_DIM)

    def body(q_ref, k_ref, v_ref, o_ref, a_scr):
        qi = pl.program_id(1)
        upper = _strict_upper(tk)
        causal = lax.broadcasted_iota(jnp.int32, (tq, tk), 1) < lax.broadcasted_iota(jnp.int32, (tq, tk), 0)
        head_cols = [slice(u * HEAD_DIM, (u + 1) * HEAD_DIM) for u in range(hp)]

        def weights(kb, carry, masked):
            rows = pl.ds(pl.multiple_of(kb * tk, tk), tk)
            out = []
            for u, cols in enumerate(head_cols):
                later = carry[u]
                z = _dot(q_ref[:, cols], k_ref[rows, cols], "nt") * scale
                log_beta = _log_sigmoid(z)
                l = log_beta - z
                if masked:
                    l = jnp.where(causal, l, 0.0)
                a = jnp.exp(log_beta + _cumulate(l, upper) + later)
                if masked:
                    a = jnp.where(causal, a, 0.0)
                a_scr[u, :, rows] = a.astype(BF16)
                out.append(later + jnp.sum(l, axis=1, keepdims=True))
            return tuple(out)

        later = weights(qi, tuple(jnp.zeros((tq, 1), F32) for _ in range(hp)), True)
        lax.fori_loop(0, qi, lambda i, c: weights(qi - 1 - i, c, False), later)

        def mix(kb, accs):
            rows = pl.ds(pl.multiple_of(kb * tk, tk), tk)
            return tuple(acc + _dot(a_scr[u, :, rows], v_ref[rows, cols])
                         for u, (acc, cols) in enumerate(zip(accs, head_cols)))

        accs = lax.fori_loop(0, qi + 1, mix, tuple(jnp.zeros((tq, HEAD_DIM), F32) for _ in range(hp)))
        for acc, cols in zip(accs, head_cols):
            o_ref[:, cols] = acc.astype(BF16)

    full = pl.BlockSpec((s, hp * HEAD_DIM), lambda h, i: (0, h))
    blk = pl.BlockSpec((tq, hp * HEAD_DIM), lambda h, i: (i, h))
    return pl.pallas_call(
        body, name="attn_fwd", grid=(heads // hp, s // tq), out_shape=_sds((s, width), BF16),
        in_specs=[blk, full, full], out_specs=blk, scratch_shapes=[pltpu.VMEM((hp, tq, s), BF16)],
        compiler_params=_params("parallel", "parallel"),
    )(qn, kn, vb)


def _attention_backward(qn, kn, vb, dout, dproj, v_col_block, deps=()):
    s, width = qn.shape
    heads = width // HEAD_DIM
    tq, tk = Q_TILE, K_TILE
    hp = HEADS_PER_STEP_BWD
    nq = s // tq
    scale = 1.0 / math.sqrt(HEAD_DIM)
    v_block0 = v_col_block * (heads // hp)

    def body(q_ref, k_ref, v_ref, do_ref, dproj_in, dq_ref, dk_ref, dv_ref, a_scr, lb_scr, dkt_scr, dvt_scr):
        del dproj_in
        qi = pl.program_id(1)

        @pl.when(qi == 0)
        def _():
            dkt_scr[...] = jnp.zeros_like(dkt_scr)
            dvt_scr[...] = jnp.zeros_like(dvt_scr)

        upper = _strict_upper(tk)
        lower = _strict_lower(tk)
        causal = lax.broadcasted_iota(jnp.int32, (tq, tk), 1) < lax.broadcasted_iota(jnp.int32, (tq, tk), 0)
        head_cols = [slice(u * HEAD_DIM, (u + 1) * HEAD_DIM) for u in range(hp)]

        def weights(kb, carry, masked):
            rows = pl.ds(pl.multiple_of(kb * tk, tk), tk)
            out = []
            for u, cols in enumerate(head_cols):
                later = carry[u]
                z = _dot(q_ref[:, cols], k_ref[rows, cols], "nt") * scale
                log_beta = _log_sigmoid(z)
                l = log_beta - z
                if masked:
                    l = jnp.where(causal, l, 0.0)
                a = jnp.exp(log_beta + _cumulate(l, upper) + later)
                if masked:
                    a = jnp.where(causal, a, 0.0)
                a_scr[u, :, rows] = a
                lb_scr[u, :, rows] = log_beta
                out.append(later + jnp.sum(l, axis=1, keepdims=True))
            return tuple(out)

        zeros = tuple(jnp.zeros((tq, 1), F32) for _ in range(hp))
        later = weights(qi, zeros, True)
        lax.fori_loop(0, qi, lambda i, c: weights(qi - 1 - i, c, False), later)

        q_t = [jnp.transpose(q_ref[:, cols].astype(F32)).astype(BF16) for cols in head_cols]
        do_t = [jnp.transpose(do_ref[:, cols].astype(F32)).astype(BF16) for cols in head_cols]

        def grads(kb, carry, masked):
            rows = pl.ds(pl.multiple_of(kb * tk, tk), tk)
            out = []
            for u, cols in enumerate(head_cols):
                dq, before = carry[u]
                k_blk = k_ref[rows, cols]
                a = a_scr[u, :, rows]
                beta = jnp.exp(lb_scr[u, :, rows])
                g = a * _dot(do_ref[:, cols], v_ref[rows, cols], "nt")
                p = _cumulate(g, lower) + before
                dz = g - (g + p) * beta
                if masked:
                    dz = jnp.where(causal, dz, 0.0)
                dz = (dz * scale).astype(BF16)
                dkt_scr[cols, rows] += _dot(q_t[u], dz)
                dvt_scr[cols, rows] += _dot(do_t[u], a.astype(BF16))
                out.append((dq + _dot(dz, k_blk), before + jnp.sum(g, axis=1, keepdims=True)))
            return tuple(out)

        carry = tuple((jnp.zeros((tq, HEAD_DIM), F32), jnp.zeros((tq, 1), F32)) for _ in range(hp))
        carry = lax.fori_loop(0, qi, lambda i, c: grads(i, c, False), carry)
        carry = grads(qi, carry, True)
        for u, cols in enumerate(head_cols):
            dq_ref[:, cols] = carry[u][0]

        @pl.when(qi == nq - 1)
        def _():
            dk_ref[...] = jnp.transpose(dkt_scr[...])
            dv_ref[...] = jnp.transpose(dvt_scr[...]).astype(BF16)

    wide = hp * HEAD_DIM
    full = pl.BlockSpec((s, wide), lambda h, i: (0, h))
    blk = pl.BlockSpec((tq, wide), lambda h, i: (i, h))
    return pl.pallas_call(
        _with_deps(body, 5, deps), name="attn_bwd", grid=(heads // hp, nq),
        out_shape=(_sds((s, width), F32), _sds((s, width), F32), _sds(dproj.shape, BF16)),
        in_specs=[blk, full, full, blk, pl.BlockSpec(memory_space=pl.ANY)] + [_ANY] * len(deps),
        out_specs=(blk, full, pl.BlockSpec((s, wide), lambda h, i: (0, v_block0 + h))),
        scratch_shapes=[pltpu.VMEM((hp, tq, s), F32), pltpu.VMEM((hp, tq, s), F32),
                        pltpu.VMEM((wide, s), F32), pltpu.VMEM((wide, s), F32)],
        input_output_aliases={4: 2}, compiler_params=_params("parallel", "arbitrary"),
    )(qn, kn, vb, dout, dproj, *deps)


def _place_columns(name, src, dst, col_block):
    s, w = src.shape
    tm = _tile(s, 512)

    def body(src_ref, dst_in, out_ref):
        del dst_in
        out_ref[...] = src_ref[...]

    return pl.pallas_call(
        body, name=name, grid=(s // tm,), out_shape=_sds(dst.shape, dst.dtype),
        in_specs=[pl.BlockSpec((tm, w), lambda i: (i, 0)), pl.BlockSpec(memory_space=pl.ANY)],
        out_specs=pl.BlockSpec((tm, w), lambda i: (i, col_block)),
        input_output_aliases={1: 0}, compiler_params=_params("parallel"),
    )(src, dst)


def _cast_into_slot(name, x, slot):
    r, c = x.shape
    tr = _tile(r, max(SUBLANES * 2, (1 << 20) // c), SUBLANES * 2)

    def body(slot_ref, x_ref, o_ref):
        del slot_ref
        o_ref[...] = x_ref[...].astype(BF16)

    grid_spec = pltpu.PrefetchScalarGridSpec(
        num_scalar_prefetch=1, grid=(r // tr,),
        in_specs=[pl.BlockSpec((tr, c), lambda i, slot_ref: (i, 0))],
        out_specs=pl.BlockSpec((None, tr, c), lambda i, slot_ref: (slot_ref[0], i, 0)))
    return pl.pallas_call(
        body, name=name, grid_spec=grid_spec, out_shape=_sds((N_DEV, r, c), BF16),
        compiler_params=_params("parallel"),
    )(slot, x)


def _adamw(name, w, g, m, v):
    r, c = w.shape
    tr = _tile(r, max(SUBLANES, (1 << 19) // c))
    c1 = 1.0 - ADAM_B1 ** ADAM_STEP
    c2 = 1.0 - ADAM_B2 ** ADAM_STEP

    def body(w_ref, g_ref, m_ref, v_ref, d_ref, nm_ref, nv_ref):
        gv = g_ref[...]
        nm = ADAM_B1 * m_ref[...] + (1.0 - ADAM_B1) * gv
        nv = ADAM_B2 * v_ref[...] + (1.0 - ADAM_B2) * (gv * gv)
        d_ref[...] = -ADAM_LR * ((nm / c1) / (jnp.sqrt(nv / c2) + ADAM_EPS) + ADAM_WD * w_ref[...])
        nm_ref[...] = nm
        nv_ref[...] = nv

    blk = pl.BlockSpec((tr, c), lambda i: (i, 0))
    return pl.pallas_call(
        body, name=name, grid=(r // tr,), out_shape=(_sds((r, c), F32),) * 3,
        in_specs=[blk] * 4, out_specs=(blk,) * 3, compiler_params=_params("parallel"),
    )(w, g, m, v)


def _rows_of_lanes(v):
    rows = v.shape[1] // LANES
    out = v.reshape(rows, LANES)
    pad = (-rows) % SUBLANES
    if pad:
        out = jnp.pad(out, ((0, pad), (0, 0)))
    return out


def kernel(x, c, w_ada, b_ada, norm1_w, w_in, q_norm_w, k_norm_w, w_pool, pool_scale, w_a_up, w_b_up, w_o, norm2_w, w_ff1, w_ff2, loss_target, m_w_ada, m_b_ada, m_norm1_w, m_w_in, m_q_norm_w, m_k_norm_w, m_w_pool, m_pool_scale, m_w_a_up, m_w_b_up, m_w_o, m_norm2_w, m_w_ff1, m_w_ff2, v_w_ada, v_b_ada, v_norm1_w, v_w_in, v_q_norm_w, v_k_norm_w, v_w_pool, v_pool_scale, v_w_a_up, v_w_b_up, v_w_o, v_norm2_w, v_w_ff1, v_w_ff2):
    _, s, d = x.shape
    half = d // 2
    d8 = d // N_DEV
    n_groups = len(POOL_WINDOWS)
    cg = half // n_groups
    me = _group_index(MESH_AXES)

    x2 = x[0]
    target = loss_target[0]

    my_slot = jnp.reshape(me, (1,)).astype(jnp.int32)

    def cast(i, t):
        return _cast_into_slot("cast_w%d" % i, t, my_slot)

    def gather_start(tag, bufs, phase):
        if phase < 2:
            return _launch("ag%s_ici%d" % (tag, phase), bufs, _plan_gather_ici(phase), (2, 4)[phase] * len(bufs))
        return _launch("ag%s_d2d" % tag, bufs, _plan_gather_d2d, len(_CHIP_MASKS) * len(bufs))

    buf_a = [cast(0, w_in[0])]
    fl_a, tok = gather_start("A", buf_a, 0)

    c_all = _all_gather_2d("ag_c", c.reshape(d // LANES, LANES), deps=(tok,)).reshape(N_DEV, d)
    wa = w_ada.shape[2]
    b_shard = lax.dynamic_slice_in_dim(b_ada, me * wa, wa, axis=1)
    mod_part = _ada_forward(c_all, w_ada[0], b_shard)
    mod_all = _all_gather_2d("ag_mod", mod_part.reshape(N_DEV * wa // LANES, LANES))
    mod_all = mod_all.reshape(N_DEV, N_DEV, wa)
    mod = lax.dynamic_slice_in_dim(mod_all, me, 1, axis=1).reshape(1, N_MOD * d)
    shift1, scale1, gate1, shift2, scale2, gate2 = [mod[:, i * d:(i + 1) * d] for i in range(N_MOD)]
    buf_b = [cast(1, w_pool[0].reshape(-1, cg)), cast(2, w_a_up[0]), cast(3, w_b_up[0]), cast(4, w_o[0])]
    buf_c = [cast(5, w_ff1[0])]
    buf_e = [cast(6, w_ff2[0])]

    buf_a = _land(fl_a, mod)
    fl_a, tok = gather_start("A", buf_a, 1)
    h = _norm_forward("norm1_fwd", x2, norm1_w, scale1, shift1, deps=(tok,))
    buf_a = _land(fl_a, [h] + buf_b + buf_c + buf_e)
    fl_b, tok_b = gather_start("B", buf_b, 0)
    fl_c, tok_c = gather_start("C", buf_c, 0)
    fl_e, tok_e = gather_start("E", buf_e, 0)
    fl_a, tok = gather_start("A", buf_a, 2)
    w_in_f, = _land(fl_a, [tok, tok_b, tok_c, tok_e])

    tm = _tile(s, 1024)
    tk = _tile(d, 2048)
    te = _tile(d, 512)

    proj = _matmul(
        "proj", "nn", (s // tm, N_DEV, d // tk), h, pl.BlockSpec((tm, tk), lambda i, j, k: (i, k)),
        w_in_f, pl.BlockSpec((None, tk, half), lambda i, j, k: (j, k, 0)),
        [_sds((s, 4 * d), F32)], [pl.BlockSpec((tm, half), lambda i, j, k: (i, j))], (tm, half))[0]
    buf_b = _land(fl_b, proj)
    buf_c = _land(fl_c, proj)
    fl_b, tok_b = gather_start("B", buf_b, 1)
    fl_c, tok_c = gather_start("C", buf_c, 1)

    qn, kn, vb = _qkv_prepare(proj, q_norm_w, k_norm_w, half, deps=(tok_b, tok_c))
    attn = _attention_forward(qn, kn, vb)
    buf_b = _land(fl_b, attn)
    buf_e = _land(fl_e, attn)
    fl_b, tok_b = gather_start("B", buf_b, 2)
    fl_e, tok_e = gather_start("E", buf_e, 1)
    w_pool_f, w_a_f, w_b_f, w_o_f = _land(fl_b, [tok_b, tok_e])
    rows_pool = cg // N_DEV
    w_pool_f = w_pool_f.reshape(N_DEV, n_groups, rows_pool, cg).transpose(1, 0, 2, 3).reshape(n_groups, cg, cg)
    w_o_f = w_o_f.reshape(d, d)
    pooled, ya_in = _pool_forward(proj, w_pool_f, pool_scale)

    def merge_epilogue(ga_ref, gb_ref, ya, yb, out_refs):
        merged_ref, ya_ref, yb_ref = out_refs
        merged = jax.nn.sigmoid(ga_ref[...]) * ya + jax.nn.sigmoid(gb_ref[...]) * yb
        merged_ref[...] = merged.astype(BF16)
        ya_ref[...] = ya.astype(BF16)
        yb_ref[...] = yb.astype(BF16)

    def up_body(a1_ref, b1_ref, a2_ref, b2_ref, ga_ref, gb_ref, *out_refs):
        merge_epilogue(ga_ref, gb_ref, _dot(a1_ref[...], b1_ref[...]), _dot(a2_ref[...], b2_ref[...]), out_refs)

    ga_blk0 = 2 * d // d8
    gb_blk0 = 3 * d // d8
    a_spec = pl.BlockSpec((tm, half), lambda i, j: (i, 0))
    wup_spec = pl.BlockSpec((None, half, d8), lambda i, j: (j, 0, 0))
    o_blk = pl.BlockSpec((tm, d8), lambda i, j: (i, j))
    merged, y_a, y_b = pl.pallas_call(
        up_body, name="up_merge", grid=(s // tm, N_DEV), out_shape=(_sds((s, d), BF16),) * 3,
        in_specs=[a_spec, wup_spec, a_spec, wup_spec,
                  pl.BlockSpec((tm, d8), lambda i, j: (i, ga_blk0 + j)),
                  pl.BlockSpec((tm, d8), lambda i, j: (i, gb_blk0 + j))],
        out_specs=(o_blk,) * 3, compiler_params=_params("parallel", "parallel"),
    )(ya_in, w_a_f, attn, w_b_f, proj, proj)
    buf_c = _land(fl_c, merged)
    fl_c, tok_c = gather_start("C", buf_c, 2)

    tn = _tile(d, 1024)

    def oproj_epilogue(acc, extra_refs, out_refs):
        x_ref, g_ref = extra_refs
        x1_ref, o_ref = out_refs
        x1_ref[...] = x_ref[...] + g_ref[...] * acc
        o_ref[...] = acc.astype(BF16)

    mn_blk = pl.BlockSpec((tm, tn), lambda i, j, k: (i, j))
    e_blk = pl.BlockSpec((tm, te), lambda i, j, k: (i, j))
    e_vec = pl.BlockSpec((1, te), lambda i, j, k: (0, j))
    x1, o_act = _matmul(
        "oproj", "nn", (s // tm, d // te, d // tk), merged, pl.BlockSpec((tm, tk), lambda i, j, k: (i, k)),
        w_o_f, pl.BlockSpec((tk, te), lambda i, j, k: (k, j)),
        [_sds((s, d), F32), _sds((s, d), BF16)], [e_blk, e_blk], (tm, te),
        epilogue=oproj_epilogue, extras=(x2, gate1), extra_specs=(e_blk, e_vec), deps=(tok_c,))

    h2 = _norm_forward("norm2_fwd", x1, norm2_w, scale2, shift2)
    buf_e = _land(fl_e, h2)
    fl_e, tok_e = gather_start("E", buf_e, 2)
    w_ff1_f, = _land(fl_c, [h2, tok_e])

    def ff1_epilogue(acc, extra_refs, out_refs):
        r = jnp.maximum(acc, 0.0)
        out_refs[0][...] = r.astype(BF16)
        out_refs[1][...] = (r * r).astype(BF16)

    ff_blk = pl.BlockSpec((tm, half), lambda i, j, k: (i, j))
    relu, act = _matmul(
        "ff1", "nn", (s // tm, N_DEV, d // tk), h2, pl.BlockSpec((tm, tk), lambda i, j, k: (i, k)),
        w_ff1_f, pl.BlockSpec((None, tk, half), lambda i, j, k: (j, k, 0)),
        [_sds((s, 4 * d), BF16)] * 2, [ff_blk, ff_blk], (tm, half), epilogue=ff1_epilogue)
    w_ff2_f, = _land(fl_e, act)
    w_ff2_f = w_ff2_f.reshape(4 * d, d)

    def ff2_epilogue(acc, extra_refs, out_refs):
        x1_ref, g_ref, t_ref = extra_refs
        df_ref, dy_ref, sq_ref, dgate_ref = out_refs
        gate = g_ref[...]
        err = x1_ref[...] + gate * acc - t_ref[...]
        dyv = err * (1.0 / d)
        dy_ref[...] = dyv
        df_ref[...] = (dyv * gate).astype(BF16)
        sq_ref[...] = jnp.full(sq_ref.shape, jnp.sum(err * err), F32)
        dgate_ref[...] = jnp.broadcast_to(jnp.sum(dyv * acc, axis=0, keepdims=True), dgate_ref.shape)

    df, dy, sq, dgate2_parts = _matmul(
        "ff2", "nn", (s // tm, d // te, 4 * d // tk), act, pl.BlockSpec((tm, tk), lambda i, j, k: (i, k)),
        w_ff2_f, pl.BlockSpec((tk, te), lambda i, j, k: (k, j)),
        [_sds((s, d), BF16), _sds((s, d), F32), _sds((s // tm * SUBLANES, d // te * LANES), F32),
         _sds((s // tm * SUBLANES, d), F32)],
        [e_blk, e_blk, pl.BlockSpec((SUBLANES, LANES), lambda i, j, k: (i, j)),
         pl.BlockSpec((SUBLANES, te), lambda i, j, k: (i, j))], (tm, te),
        epilogue=ff2_epilogue, extras=(x1, gate2, target), extra_specs=(e_blk, e_vec, e_blk))
    loss_local = (0.5 / d) * jnp.sum(sq[::SUBLANES, ::LANES])
    dgate2 = jnp.sum(dgate2_parts[::SUBLANES], axis=0, keepdims=True)

    tok_k = _tile(s, 2048)
    tw = _tile(d, 1024)
    g_ff2 = _matmul(
        "g_ff2", "tn", (4 * d // tw, d // tn, s // tok_k), act, pl.BlockSpec((tok_k, tw), lambda i, j, k: (k, i)),
        df, pl.BlockSpec((tok_k, tn), lambda i, j, k: (k, j)),
        [_sds((4 * d, d), BF16)], [pl.BlockSpec((tw, tn), lambda i, j, k: (i, j))], (tw, tn))[0]

    def da_epilogue(acc, extra_refs, out_refs):
        out_refs[0][...] = (acc * (2.0 * extra_refs[0][...].astype(F32))).astype(BF16)

    big_blk = pl.BlockSpec((tm, tn), lambda i, j, k: (i, j))
    fl_f2, tok = _reduce_scatter_start("F2", [g_ff2.reshape(N_DEV, half, d)])
    df1 = _matmul(
        "da_ff", "nt", (s // tm, 4 * d // tn, d // tk), df, pl.BlockSpec((tm, tk), lambda i, j, k: (i, k)),
        w_ff2_f, pl.BlockSpec((tn, tk), lambda i, j, k: (j, k)),
        [_sds((s, 4 * d), BF16)], [big_blk], (tm, tn),
        epilogue=da_epilogue, extras=(relu,), extra_specs=(big_blk,), deps=(tok,))[0]

    fl_f2, tok = _reduce_scatter_middle("F2", fl_f2, df1, me)
    g_ff1 = _matmul(
        "g_ff1", "tn", (d // tw, N_DEV, s // tok_k), h2, pl.BlockSpec((tok_k, tw), lambda i, j, k: (k, i)),
        df1, pl.BlockSpec((tok_k, half), lambda i, j, k: (k, j)),
        [_sds((N_DEV, d, half), BF16)], [pl.BlockSpec((None, tw, half), lambda i, j, k: (j, i, 0))], (tw, half),
        deps=(tok,))[0]

    fl_f1, tok = _reduce_scatter_start("F1", [g_ff1])
    dh2 = _matmul(
        "dh2", "nt", (s // tm, d // tn, N_DEV), df1, pl.BlockSpec((tm, half), lambda i, j, k: (i, k)),
        w_ff1_f, pl.BlockSpec((None, tn, half), lambda i, j, k: (k, j, 0)),
        [_sds((s, d), F32)], [mn_blk], (tm, tn), deps=(tok,))[0]

    g_w_ff2, = _reduce_scatter_finish("F2", fl_f2, dh2)
    fl_f1, tok = _reduce_scatter_middle("F1", fl_f1, dh2, me)
    dx1, dshift2, dscale2, g_norm2, do, dgate1 = _norm_backward(
        "norm2_bwd", dh2, x1, norm2_w, scale2, dy, gated=(gate1, o_act), deps=(tok,))

    g_o = _matmul(
        "g_o", "tn", (d // tw, d // tn, s // tok_k), merged, pl.BlockSpec((tok_k, tw), lambda i, j, k: (k, i)),
        do, pl.BlockSpec((tok_k, tn), lambda i, j, k: (k, j)),
        [_sds((d, d), BF16)], [pl.BlockSpec((tw, tn), lambda i, j, k: (i, j))], (tw, tn))[0]

    def merge_bwd_epilogue(acc, extra_refs, out_refs):
        ga_ref, gb_ref, ya_ref, yb_ref = extra_refs
        dya_ref, dyb_ref, dga_ref, dgb_ref = out_refs
        sa = jax.nn.sigmoid(ga_ref[...])
        sb = jax.nn.sigmoid(gb_ref[...])
        dya_ref[...] = (acc * sa).astype(BF16)
        dyb_ref[...] = (acc * sb).astype(BF16)
        dga_ref[...] = (acc * ya_ref[...].astype(F32) * (sa * (1.0 - sa))).astype(BF16)
        dgb_ref[...] = (acc * yb_ref[...].astype(F32) * (sb * (1.0 - sb))).astype(BF16)

    nb = d // te
    dy_a, dy_b, dproj, dg_b = _matmul(
        "dmerged", "nt", (s // tm, nb, d // tk), do, pl.BlockSpec((tm, tk), lambda i, j, k: (i, k)),
        w_o_f, pl.BlockSpec((te, tk), lambda i, j, k: (j, k)),
        [_sds((s, d), BF16), _sds((s, d), BF16), _sds((s, 4 * d), BF16), _sds((s, d), BF16)],
        [e_blk, e_blk, pl.BlockSpec((tm, te), lambda i, j, k: (i, 2 * nb + j)), e_blk], (tm, te),
        epilogue=merge_bwd_epilogue, extras=(proj, proj, y_a, y_b),
        extra_specs=(pl.BlockSpec((tm, te), lambda i, j, k: (i, 2 * nb + j)),
                     pl.BlockSpec((tm, te), lambda i, j, k: (i, 3 * nb + j)), e_blk, e_blk))
    dproj = _place_columns("place_dgb", dg_b, dproj, 3)

    up_a = pl.BlockSpec((tok_k, half), lambda i, j, k: (k, 0))
    up_b = pl.BlockSpec((tok_k, d8), lambda i, j, k: (k, j))
    up_o = pl.BlockSpec((None, half, d8), lambda i, j, k: (j, 0, 0))
    g_a_up = _matmul("g_a_up", "tn", (1, N_DEV, s // tok_k), ya_in, up_a, dy_a, up_b,
                     [_sds((N_DEV, half, d8), BF16)], [up_o], (half, d8))[0]
    g_b_up = _matmul("g_b_up", "tn", (1, N_DEV, s // tok_k), attn, up_a, dy_b, up_b,
                     [_sds((N_DEV, half, d8), BF16)], [up_o], (half, d8))[0]
    dn_a = pl.BlockSpec((tm, d8), lambda i, j, k: (i, k))
    dn_b = pl.BlockSpec((None, half, d8), lambda i, j, k: (k, 0, 0))
    dn_o = pl.BlockSpec((tm, half), lambda i, j, k: (i, 0))
    dya_in = _matmul("d_ya_in", "nt", (s // tm, 1, N_DEV), dy_a, dn_a, w_a_f, dn_b,
                     [_sds((s, half), BF16)], [dn_o], (tm, half))[0]
    dattn = _matmul("d_attn", "nt", (s // tm, 1, N_DEV), dy_b, dn_a, w_b_f, dn_b,
                    [_sds((s, half), BF16)], [dn_o], (tm, half))[0]

    dproj, g_pool, g_pool_scale = _pool_backward(dya_in, pooled, w_pool_f, pool_scale, dproj)
    g_w_ff1, = _reduce_scatter_finish("F1", fl_f1, g_pool)
    g_pool_send = g_pool.astype(BF16).reshape(n_groups, N_DEV, rows_pool, cg).transpose(1, 0, 2, 3)
    g_pool_send = g_pool_send.reshape(N_DEV, n_groups * rows_pool, cg)
    fl_b, tok = _reduce_scatter_start("B", [g_pool_send, g_a_up, g_b_up, g_o.reshape(N_DEV, d8, d)])
    dqn, dkn, dproj = _attention_backward(qn, kn, vb, dattn, dproj, 3, deps=(tok,))
    fl_b, tok = _reduce_scatter_middle("B", fl_b, dqn, me)
    dproj, g_qnorm = _qk_norm_backward("qnorm_bwd", dqn, proj, 1, q_norm_w, dproj, half, deps=(tok,))
    dproj, g_knorm = _qk_norm_backward("knorm_bwd", dkn, proj, 2, k_norm_w, dproj, half)

    g_in = _matmul(
        "g_in", "tn", (d // tw, N_DEV, s // tok_k), h, pl.BlockSpec((tok_k, tw), lambda i, j, k: (k, i)),
        dproj, pl.BlockSpec((tok_k, half), lambda i, j, k: (k, j)),
        [_sds((N_DEV, d, half), BF16)], [pl.BlockSpec((None, tw, half), lambda i, j, k: (j, i, 0))], (tw, half))[0]
    fl_in, tok = _reduce_scatter_start("I", [g_in])
    dh = _matmul(
        "dh", "nt", (s // tm, d // tn, N_DEV), dproj, pl.BlockSpec((tm, half), lambda i, j, k: (i, k)),
        w_in_f, pl.BlockSpec((None, tn, half), lambda i, j, k: (k, j, 0)),
        [_sds((s, d), F32)], [mn_blk], (tm, tn), deps=(tok,))[0]
    g_w_pool, g_w_a_up, g_w_b_up, g_w_o = _reduce_scatter_finish("B", fl_b, dh)
    grad_x, dshift1, dscale1, g_norm1 = _norm_backward("norm1_bwd", dh, x2, norm1_w, scale1, dx1)

    dmod = jnp.concatenate([dshift1, dscale1, dgate1, dshift2, dscale2, dgate2], axis=1)
    pieces = [dmod, g_norm1, g_norm2, g_pool_scale, g_qnorm, g_knorm, jnp.full((1, LANES), loss_local, F32)]
    packed_rows = [_rows_of_lanes(p) for p in pieces]
    offsets = [0]
    for p in packed_rows:
        offsets.append(offsets[-1] + p.shape[0])
    packed = jnp.concatenate(packed_rows, axis=0)
    small_all = _all_gather_2d("ag_small", packed)
    fl_in, tok = _reduce_scatter_middle("I", fl_in, small_all, me)
    small_sum = _sum_slots("small_sum", small_all[None], F32)[0]

    def unpack(i, width):
        return small_sum[offsets[i]:offsets[i] + width // LANES].reshape(1, width)

    g_b_ada = unpack(0, N_MOD * d)
    g_norm1_w = unpack(1, d)
    g_norm2_w = unpack(2, d)
    g_pool_scale_w = unpack(3, half)
    g_q_norm_w = unpack(4, HEAD_DIM)
    g_k_norm_w = unpack(5, HEAD_DIM)
    loss = unpack(6, LANES)[0, 0]
    dmod_all = small_all[:, :N_MOD * d // LANES].reshape(N_DEV, N_MOD * d)
    dmod_cols = lax.dynamic_slice_in_dim(dmod_all, me * wa, wa, axis=1)
    g_w_ada = _ada_weight_grad(c_all, dmod_cols, deps=(tok,))[None]


    grads = {
        "w_ada": g_w_ada, "b_ada": g_b_ada, "norm1_w": g_norm1_w,
        "q_norm_w": g_q_norm_w, "k_norm_w": g_k_norm_w,
        "w_pool": g_w_pool.reshape(w_pool.shape), "pool_scale": g_pool_scale_w,
        "w_a_up": g_w_a_up[None], "w_b_up": g_w_b_up[None], "w_o": g_w_o[None],
        "norm2_w": g_norm2_w, "w_ff1": g_w_ff1[None], "w_ff2": g_w_ff2[None],
    }
    weights = {"w_ada": (w_ada, m_w_ada, v_w_ada), "b_ada": (b_ada, m_b_ada, v_b_ada),
               "norm1_w": (norm1_w, m_norm1_w, v_norm1_w), "w_in": (w_in, m_w_in, v_w_in),
               "q_norm_w": (q_norm_w, m_q_norm_w, v_q_norm_w), "k_norm_w": (k_norm_w, m_k_norm_w, v_k_norm_w),
               "w_pool": (w_pool, m_w_pool, v_w_pool), "pool_scale": (pool_scale, m_pool_scale, v_pool_scale),
               "w_a_up": (w_a_up, m_w_a_up, v_w_a_up), "w_b_up": (w_b_up, m_w_b_up, v_w_b_up),
               "w_o": (w_o, m_w_o, v_w_o), "norm2_w": (norm2_w, m_norm2_w, v_norm2_w),
               "w_ff1": (w_ff1, m_w_ff1, v_w_ff1), "w_ff2": (w_ff2, m_w_ff2, v_w_ff2)}
    order = list(weights)
    deltas, new_m, new_v = {}, {}, {}
    def adam(name):
        wt, mt, vt = weights[name]
        shape = wt.shape
        flat = (-1, shape[-1])
        dl, nm, nv = _adamw("adamw_" + name, wt.reshape(flat), grads[name].reshape(flat),
                            mt.reshape(flat), vt.reshape(flat))
        deltas[name], new_m[name], new_v[name] = dl.reshape(shape), nm.reshape(shape), nv.reshape(shape)

    others = [n for n in order if n != "w_in"]
    for name in others:
        adam(name)
    g_w_in, = _reduce_scatter_finish("I", fl_in, [deltas[n] for n in others])
    grads["w_in"] = g_w_in[None]
    adam("w_in")

    return (loss, grad_x[None], *[grads[n] for n in order], *[deltas[n] for n in order],
            *[new_m[n] for n in order], *[new_v[n] for n in order])
```

```python
import math

import jax
import jax.numpy as jnp
from jax import lax
from jax.experimental import pallas as pl
from jax.experimental.pallas import tpu as pltpu

F32 = jnp.float32
BF16 = jnp.bfloat16
MESH_AXES = ("x", "y", "c")
N_DEV = 8
HEAD_DIM = 128
POOL_WINDOWS = (2, 4, 8, 16)
N_MOD = 6
NORM_EPS = 1e-6
LANES = 128
SUBLANES = 8
VMEM_LIMIT_BYTES = 56 * 1024 * 1024
Q_TILE = 256
K_TILE = 256
POOL_TILE = 256
HEADS_PER_STEP = 4
HEADS_PER_STEP_BWD = 4

ADAM_LR = 0.001
ADAM_B1 = 0.9
ADAM_B2 = 0.999
ADAM_EPS = 1e-08
ADAM_WD = 0.01
ADAM_STEP = 10

_NN = (((1,), (0,)), ((), ()))
_NT = (((1,), (1,)), ((), ()))
_TN = (((0,), (0,)), ((), ()))
_DIMS = {"nn": _NN, "nt": _NT, "tn": _TN}


def _dot(a, b, mode="nn"):
    return lax.dot_general(a, b, _DIMS[mode], preferred_element_type=F32)


def _params(*sem):
    return pltpu.CompilerParams(dimension_semantics=sem, vmem_limit_bytes=VMEM_LIMIT_BYTES)


def _tile(dim, pref, align=SUBLANES):
    for t in range(min(dim, pref), 0, -1):
        if dim % t == 0 and t % align == 0:
            return t
    return dim


def _group_index(axes):
    idx = 0
    for a in axes:
        idx = idx * 2 + lax.axis_index(a)
    return idx


def _peer_device(axes, k):
    coords = {a: lax.axis_index(a) for a in MESH_AXES}
    for pos, a in enumerate(axes):
        if (k >> (len(axes) - 1 - pos)) & 1:
            coords[a] = 1 - coords[a]
    return tuple(coords[a] for a in MESH_AXES)


_AXIS_BIT = {"x": 4, "y": 2, "c": 1}
_ANY = pl.BlockSpec(memory_space=pl.ANY)


def _device_xor(mask):
    return tuple(1 - lax.axis_index(a) if mask & _AXIS_BIT[a] else lax.axis_index(a) for a in MESH_AXES)


def _remote(src, dst, send_sem, recv_sem, mask):
    return pltpu.make_async_remote_copy(src_ref=src, dst_ref=dst, send_sem=send_sem, recv_sem=recv_sem,
                                        device_id=_device_xor(mask), device_id_type=pl.DeviceIdType.MESH)


def _start_all(copies):
    for cp in copies:
        cp.start()


def _wait_all(copies):
    for cp in copies:
        cp.wait()


def _gather_ici(name, bufs):
    na = len(bufs)
    bx, by = _AXIS_BIT["x"], _AXIS_BIT["y"]

    def body(*refs):
        out = refs[na:2 * na]
        send_sems, recv_sems = refs[2 * na:]
        me = _group_index(MESH_AXES)

        def copy(a, n, slot, color, mask):
            half = out[a].shape[1] // 2
            piece = out[a].at[slot, pl.ds(color * half, half)]
            return _remote(piece, piece, send_sems.at[a, n], recv_sems.at[a, n], mask)

        first, second = [], []
        for a in range(na):
            first += [copy(a, 0, me, 0, bx), copy(a, 1, me, 1, by)]
            second += [copy(a, 2, me, 0, by), copy(a, 3, me ^ bx, 0, by),
                       copy(a, 4, me, 1, bx), copy(a, 5, me ^ by, 1, bx)]
        _start_all(first)
        _wait_all(first)
        _start_all(second)
        _wait_all(second)

    return pl.pallas_call(
        body, name=name, out_shape=tuple(jax.ShapeDtypeStruct(b.shape, b.dtype) for b in bufs),
        in_specs=[_ANY] * na, out_specs=tuple([_ANY] * na),
        input_output_aliases={a: a for a in range(na)},
        scratch_shapes=[pltpu.SemaphoreType.DMA((na, 6)), pltpu.SemaphoreType.DMA((na, 6))],
    )(*bufs)


def _gather_d2d(name, bufs):
    na = len(bufs)
    masks = (0, _AXIS_BIT["y"], _AXIS_BIT["x"], _AXIS_BIT["x"] | _AXIS_BIT["y"])

    def body(*refs):
        out = refs[na:2 * na]
        send_sems, recv_sems = refs[2 * na:]
        me = _group_index(MESH_AXES)
        copies = []
        for a in range(na):
            for n, m in enumerate(masks):
                piece = out[a].at[me ^ m]
                copies.append(_remote(piece, piece, send_sems.at[a, n], recv_sems.at[a, n], _AXIS_BIT["c"]))
        _start_all(copies)
        _wait_all(copies)

    return pl.pallas_call(
        body, name=name, out_shape=tuple(jax.ShapeDtypeStruct(b.shape, b.dtype) for b in bufs),
        in_specs=[_ANY] * na, out_specs=tuple([_ANY] * na),
        input_output_aliases={a: a for a in range(na)},
        scratch_shapes=[pltpu.SemaphoreType.DMA((na, 4)), pltpu.SemaphoreType.DMA((na, 4))],
    )(*bufs)


def _scatter(name, srcs, send_slots, masks):
    na = len(srcs)
    nm = len(masks)

    def body(*refs):
        src, out = refs[:na], refs[na:2 * na]
        send_sems, recv_sems = refs[2 * na:]
        slots = send_slots(_group_index(MESH_AXES))
        copies = []
        for n, m in enumerate(masks):
            for a in range(na):
                copies.append(_remote(src[a].at[slots[n]], out[a].at[n],
                                      send_sems.at[a, n], recv_sems.at[a, n], m))
        _start_all(copies)
        _wait_all(copies)

    return pl.pallas_call(
        body, name=name,
        out_shape=tuple(jax.ShapeDtypeStruct((nm,) + s.shape[1:], s.dtype) for s in srcs),
        in_specs=[_ANY] * na, out_specs=tuple([_ANY] * na),
        scratch_shapes=[pltpu.SemaphoreType.DMA((na, nm)), pltpu.SemaphoreType.DMA((na, nm))],
    )(*srcs)


def _add_received(name, own, own_slots, received, out_dtype):
    nj, r, c = received.shape
    tr = _tile(r, max(2 * SUBLANES, (1 << 20) // c), 2 * SUBLANES)

    def body(slots_ref, own_ref, rec_ref, o_ref):
        del slots_ref
        o_ref[...] = (own_ref[...].astype(F32) + rec_ref[...].astype(F32)).astype(o_ref.dtype)

    grid_spec = pltpu.PrefetchScalarGridSpec(
        num_scalar_prefetch=1, grid=(nj, r // tr),
        in_specs=[pl.BlockSpec((None, tr, c), lambda j, i, slots: (slots[j], i, 0)),
                  pl.BlockSpec((None, tr, c), lambda j, i, slots: (j, i, 0))],
        out_specs=pl.BlockSpec((None, tr, c), lambda j, i, slots: (j, i, 0)))
    return pl.pallas_call(
        body, name=name, grid_spec=grid_spec, out_shape=jax.ShapeDtypeStruct((nj, r, c), out_dtype),
        compiler_params=_params("parallel", "parallel"),
    )(own_slots, own, received)


def _add_final(name, own, received):
    nj, r, c = received.shape
    tr = _tile(r, max(2 * SUBLANES, (1 << 19) // c), 2 * SUBLANES)

    def body(own_ref, rec_ref, o_ref):
        acc = own_ref[...].astype(F32)
        for j in range(nj):
            acc = acc + rec_ref[j].astype(F32)
        o_ref[...] = acc

    return pl.pallas_call(
        body, name=name, grid=(r // tr,), out_shape=jax.ShapeDtypeStruct((r, c), F32),
        in_specs=[pl.BlockSpec((None, tr, c), lambda i: (0, i, 0)),
                  pl.BlockSpec((nj, tr, c), lambda i: (0, i, 0))],
        out_specs=pl.BlockSpec((tr, c), lambda i: (i, 0)),
        compiler_params=_params("parallel"),
    )(own, received)


_HBM = pl.BlockSpec(memory_space=pltpu.HBM)
_SEM = pl.BlockSpec(memory_space=pltpu.SEMAPHORE)
_DATAFLOW = pltpu.SideEffectType.DATAFLOW_SIDE_EFFECTING


def _launch(name, bufs, plan, n_copies, deps=()):
    nb = len(bufs)

    def body(*refs):
        ins = refs[:nb]
        send_sems, recv_sems = refs[nb + len(deps)], refs[nb + len(deps) + 1]
        token = refs[-1]
        copies = plan(ins, _group_index(MESH_AXES))
        assert len(copies) == n_copies
        for n, (src, dst, mask) in enumerate(copies):
            _remote(src, dst, send_sems.at[n], recv_sems.at[n], mask).start()
        token[...] = jnp.zeros_like(token)

    outs = pl.pallas_call(
        body, name=name,
        out_shape=(pltpu.SemaphoreType.DMA((n_copies,)), pltpu.SemaphoreType.DMA((n_copies,)),
                   *[pltpu.HBM(b.shape, b.dtype) for b in bufs], jax.ShapeDtypeStruct((SUBLANES, LANES), F32)),
        in_specs=[_HBM] * nb + [_ANY] * len(deps),
        out_specs=(_SEM, _SEM, *[_HBM] * nb, pl.BlockSpec(memory_space=pltpu.VMEM)),
        input_output_aliases={i: 2 + i for i in range(nb)},
        compiler_params=pltpu.CompilerParams(has_side_effects=_DATAFLOW),
    )(*[pltpu.with_memory_space_constraint(b, pltpu.HBM) for b in bufs], *deps)
    return (name, plan, n_copies, outs[0], outs[1], list(outs[2:2 + nb])), outs[-1]


def _land(flight, after):
    name, plan, n_copies, send_sems, recv_sems, bufs = flight
    nb = len(bufs)
    after = list(after) if isinstance(after, (list, tuple)) else [after]

    def body(*refs):
        ins = refs[:nb]
        s_sems, r_sems = refs[nb], refs[nb + 1]
        for n, (src, dst, mask) in enumerate(plan(ins, _group_index(MESH_AXES))):
            cp = _remote(src, dst, s_sems.at[n], r_sems.at[n], mask)
            cp.wait_send()
            cp.wait_recv()

    outs = pl.pallas_call(
        body, name=name + "_land",
        out_shape=tuple(pltpu.HBM(b.shape, b.dtype) for b in bufs),
        in_specs=[_HBM] * nb + [_SEM, _SEM] + [_ANY] * len(after), out_specs=tuple([_HBM] * nb),
        input_output_aliases={i: i for i in range(nb)},
        compiler_params=pltpu.CompilerParams(has_side_effects=_DATAFLOW),
    )(*bufs, send_sems, recv_sems, *after)
    return list(outs)


def _plan_gather_ici(phase):
    bx, by = _AXIS_BIT["x"], _AXIS_BIT["y"]

    def plan(refs, me):
        copies = []
        for ref in refs:
            half = ref.shape[1] // 2

            def piece(slot, color, mask, ref=ref, half=half):
                p = ref.at[slot, pl.ds(color * half, half)]
                return (p, p, mask)

            if phase == 0:
                copies += [piece(me, 0, bx), piece(me, 1, by)]
            else:
                copies += [piece(me, 0, by), piece(me ^ bx, 0, by), piece(me, 1, bx), piece(me ^ by, 1, bx)]
        return copies

    return plan


def _plan_gather_d2d(refs, me):
    copies = []
    for ref in refs:
        for m in _CHIP_MASKS:
            copies.append((ref.at[me ^ m], ref.at[me ^ m], _AXIS_BIT["c"]))
    return copies


def _plan_scatter_d2d(refs, me):
    na = len(refs) // 2
    copies = []
    for a in range(na):
        for j, m in enumerate(_CHIP_MASKS):
            copies.append((refs[a].at[me ^ _AXIS_BIT["c"] ^ m], refs[na + a].at[j], _AXIS_BIT["c"]))
    return copies


def _plan_scatter_ici(refs, me):
    del me
    na = len(refs) // 2
    copies = []
    for a in range(na):
        for n, m in enumerate(_CHIP_MASKS[1:]):
            copies.append((refs[a].at[n + 1], refs[na + a].at[n], m))
    return copies


def _with_deps(body, n_in, deps):
    if not deps:
        return body

    def wrapped(*refs):
        return body(*refs[:n_in], *refs[n_in + len(deps):])

    return wrapped


def _reduce_scatter_start(tag, grads):
    lands = [lax.empty((len(_CHIP_MASKS),) + g.shape[1:], g.dtype) for g in grads]
    return _launch("rs%s_d2d" % tag, list(grads) + lands, _plan_scatter_d2d, len(_CHIP_MASKS) * len(grads))


def _reduce_scatter_middle(tag, flight, after, me):
    bufs = _land(flight, after)
    na = len(bufs) // 2
    own_slots = jnp.stack([me ^ m for m in _CHIP_MASKS]).astype(jnp.int32)
    sums = [_add_received("rs%s_add_d2d_%d" % (tag, a), bufs[a], own_slots, bufs[na + a], BF16) for a in range(na)]
    lands = [lax.empty((len(_CHIP_MASKS) - 1,) + h.shape[1:], h.dtype) for h in sums]
    return _launch("rs%s_ici" % tag, sums + lands, _plan_scatter_ici, (len(_CHIP_MASKS) - 1) * na)


def _reduce_scatter_finish(flight, after):
    bufs = _land(flight, after)
    na = len(bufs) // 2
    return [(bufs[a], bufs[na + a]) for a in range(na)]


def _all_gather_2d(name, x, deps=()):
    r, c = x.shape

    def body(x_ref, out_ref, send_sems, recv_sems):
        me = _group_index(MESH_AXES)
        out_ref[me] = x_ref[...]
        copies = []
        for k in range(1, N_DEV):
            cp = pltpu.make_async_remote_copy(
                src_ref=x_ref, dst_ref=out_ref.at[me],
                send_sem=send_sems.at[k - 1], recv_sem=recv_sems.at[k - 1],
                device_id=_peer_device(MESH_AXES, k), device_id_type=pl.DeviceIdType.MESH)
            cp.start()
            copies.append(cp)
        for cp in copies:
            cp.wait()

    vmem = pl.BlockSpec(memory_space=pltpu.VMEM)
    return pl.pallas_call(
        _with_deps(body, 1, deps), name=name, out_shape=jax.ShapeDtypeStruct((N_DEV, r, c), x.dtype),
        in_specs=[vmem] + [_ANY] * len(deps), out_specs=vmem,
        scratch_shapes=[pltpu.SemaphoreType.DMA((N_DEV - 1,)), pltpu.SemaphoreType.DMA((N_DEV - 1,))],
    )(x, *deps)


def _sum_slots(name, buf, out_dtype):
    pre, n, r, c = buf.shape
    tr = _tile(r, max(SUBLANES * 2, (1 << 20) // c))

    def body(b_ref, o_ref):
        acc = b_ref[0].astype(F32)
        for q in range(1, n):
            acc = acc + b_ref[q].astype(F32)
        o_ref[...] = acc.astype(o_ref.dtype)

    return pl.pallas_call(
        body, name=name, grid=(pre, r // tr),
        out_shape=jax.ShapeDtypeStruct((pre, r, c), out_dtype),
        in_specs=[pl.BlockSpec((None, n, tr, c), lambda i, j: (i, 0, j, 0))],
        out_specs=pl.BlockSpec((None, tr, c), lambda i, j: (i, j, 0)),
        compiler_params=_params("parallel", "parallel"),
    )(buf)


def _all_gather_weights(bufs):
    return _gather_d2d("ag_d2d", list(_gather_ici("ag_ici", bufs)))


_CHIP_MASKS = (0, _AXIS_BIT["y"], _AXIS_BIT["x"], _AXIS_BIT["x"] | _AXIS_BIT["y"])


def _reduce_scatter_grads(bufs):
    me = _group_index(MESH_AXES)
    bc = _AXIS_BIT["c"]
    r1 = _scatter("rs_d2d", bufs, lambda i: [i ^ bc ^ m for m in _CHIP_MASKS], (bc,) * len(_CHIP_MASKS))
    own_slots = jnp.stack([me ^ m for m in _CHIP_MASKS]).astype(jnp.int32)
    half = [_add_received("rs_add_d2d_%d" % a, b, own_slots, r, BF16) for a, (b, r) in enumerate(zip(bufs, r1))]
    r2 = _scatter("rs_ici", half, lambda i: [1, 2, 3], _CHIP_MASKS[1:])
    return [_add_final("rs_add_ici_%d" % a, h, r) for a, (h, r) in enumerate(zip(half, r2))]


def _matmul(name, mode, grid, a, a_spec, b, b_spec, out_shapes, out_specs, acc_shape,
            epilogue=None, extras=(), extra_specs=(), aliases=None, deps=()):
    nk = grid[2]
    n_extra = len(extras)
    n_out = len(out_shapes)

    def finish(acc, extra_refs, out_refs):
        if epilogue is None:
            out_refs[0][...] = acc.astype(out_refs[0].dtype)
        else:
            epilogue(acc, extra_refs, out_refs)

    def product(a_ref, b_ref):
        if len(b_ref.shape) == 2:
            return _dot(a_ref[...], b_ref[...], mode)
        width = a_ref.shape[1] // b_ref.shape[0]
        total = None
        for i in range(b_ref.shape[0]):
            part = _dot(a_ref[:, i * width:(i + 1) * width], b_ref[i], mode)
            total = part if total is None else total + part
        return total

    def body(*refs):
        a_ref, b_ref = refs[0], refs[1]
        extra_refs = refs[2:2 + n_extra]
        out_refs = refs[2 + n_extra:2 + n_extra + n_out]
        if nk == 1:
            finish(product(a_ref, b_ref), extra_refs, out_refs)
            return
        acc_ref = refs[-1]
        k = pl.program_id(2)

        @pl.when(k == 0)
        def _():
            acc_ref[...] = product(a_ref, b_ref)

        @pl.when((k > 0) & (k < nk - 1))
        def _():
            acc_ref[...] += product(a_ref, b_ref)

        @pl.when(k == nk - 1)
        def _():
            finish(acc_ref[...] + product(a_ref, b_ref), extra_refs, out_refs)

    scratch = [] if nk == 1 else [pltpu.VMEM(acc_shape, F32)]
    return pl.pallas_call(
        _with_deps(body, 2 + n_extra, deps), name=name, grid=grid, out_shape=tuple(out_shapes),
        in_specs=[a_spec, b_spec] + list(extra_specs) + [_ANY] * len(deps), out_specs=tuple(out_specs),
        scratch_shapes=scratch, input_output_aliases=aliases or {},
        compiler_params=_params("parallel", "parallel", "arbitrary"),
    )(a, b, *extras, *deps)


def _sds(shape, dtype):
    return jax.ShapeDtypeStruct(tuple(shape), dtype)


def _ada_forward(c_all, w_ada, b_shard):
    nb, d = c_all.shape
    w = w_ada.shape[1]
    tn = _tile(w, 512)

    def body(c_ref, w_ref, b_ref, o_ref):
        cv = c_ref[...]
        sc = cv * jax.nn.sigmoid(cv)
        o_ref[...] = jnp.dot(sc, w_ref[...], precision=lax.Precision.HIGHEST,
                             preferred_element_type=F32) + b_ref[...]

    return pl.pallas_call(
        body, name="ada_fwd", grid=(w // tn,), out_shape=_sds((nb, w), F32),
        in_specs=[pl.BlockSpec((nb, d), lambda j: (0, 0)), pl.BlockSpec((d, tn), lambda j: (0, j)),
                  pl.BlockSpec((1, tn), lambda j: (0, j))],
        out_specs=pl.BlockSpec((nb, tn), lambda j: (0, j)),
        compiler_params=_params("parallel"),
    )(c_all, w_ada, b_shard)


def _ada_weight_grad(c_all, dmod_cols, deps=()):
    nb, d = c_all.shape
    w = dmod_cols.shape[1]
    tn = _tile(w, 512)

    def body(c_ref, g_ref, o_ref):
        cv = c_ref[...]
        sc = cv * jax.nn.sigmoid(cv)
        o_ref[...] = lax.dot_general(sc, g_ref[...], _TN, precision=lax.Precision.HIGHEST,
                                     preferred_element_type=F32)

    return pl.pallas_call(
        _with_deps(body, 2, deps), name="ada_wgrad", grid=(w // tn,), out_shape=_sds((d, w), F32),
        in_specs=[pl.BlockSpec((nb, d), lambda j: (0, 0)), pl.BlockSpec((nb, tn), lambda j: (0, j))]
        + [_ANY] * len(deps),
        out_specs=pl.BlockSpec((d, tn), lambda j: (0, j)),
        compiler_params=_params("parallel"),
    )(c_all, dmod_cols, *deps)


def _norm_forward(name, x, norm_w, scale, shift, deps=()):
    s, d = x.shape
    tm = _tile(s, 256)

    def body(x_ref, w_ref, sc_ref, sh_ref, h_ref):
        xv = x_ref[...]
        r = lax.rsqrt(jnp.mean(xv * xv, axis=-1, keepdims=True) + NORM_EPS)
        h = (xv * r * w_ref[...]) * (1.0 + sc_ref[...]) + sh_ref[...]
        h_ref[...] = h.astype(BF16)

    vec = pl.BlockSpec((1, d), lambda i: (0, 0))
    row = pl.BlockSpec((tm, d), lambda i: (i, 0))
    return pl.pallas_call(
        _with_deps(body, 4, deps), name=name, grid=(s // tm,), out_shape=_sds((s, d), BF16),
        in_specs=[row, vec, vec, vec] + [_ANY] * len(deps), out_specs=row, compiler_params=_params("parallel"),
    )(x, norm_w, scale, shift, *deps)


def _norm_backward(name, dh, x, norm_w, scale, dres, gated=None, deps=()):
    s, d = x.shape
    tm = _tile(s, 256)
    n_in = 7 if gated else 5

    def body(*refs):
        dh_ref, x_ref, w_ref, sc_ref, dres_ref = refs[:5]
        dx_ref, dshift_ref, dscale_ref, dw_ref = refs[n_in:n_in + 4]
        sums = (dshift_ref, dscale_ref, dw_ref) + ((refs[n_in + 5],) if gated else ())

        @pl.when(pl.program_id(0) == 0)
        def _():
            for ref in sums:
                ref[...] = jnp.zeros_like(ref)

        xv = x_ref[...]
        g = dh_ref[...]
        r = lax.rsqrt(jnp.mean(xv * xv, axis=-1, keepdims=True) + NORM_EPS)
        n = xv * r
        gain = 1.0 + sc_ref[...]
        gn = g * n
        dshift_ref[...] += jnp.sum(g, axis=0, keepdims=True)
        dscale_ref[...] += jnp.sum(gn, axis=0, keepdims=True) * w_ref[...]
        dw_ref[...] += jnp.sum(gn, axis=0, keepdims=True) * gain
        dn = g * (w_ref[...] * gain)
        dx = dres_ref[...] + r * (dn - n * jnp.mean(dn * n, axis=-1, keepdims=True))
        dx_ref[...] = dx
        if gated:
            gate_ref, other_ref = refs[5:7]
            refs[n_in + 4][...] = (dx * gate_ref[...]).astype(BF16)
            refs[n_in + 5][...] += jnp.sum(dx * other_ref[...].astype(F32), axis=0, keepdims=True)

    vec = pl.BlockSpec((1, d), lambda i: (0, 0))
    row = pl.BlockSpec((tm, d), lambda i: (i, 0))
    vec_out = _sds((1, d), F32)
    return pl.pallas_call(
        _with_deps(body, n_in, deps), name=name, grid=(s // tm,),
        out_shape=(_sds((s, d), F32), vec_out, vec_out, vec_out) + ((_sds((s, d), BF16), vec_out) if gated else ()),
        in_specs=[row, row, vec, vec, row] + ([vec, row] if gated else []) + [_ANY] * len(deps),
        out_specs=(row, vec, vec, vec) + ((row, vec) if gated else ()),
        compiler_params=_params("arbitrary"),
    )(dh, x, norm_w, scale, dres, *(gated or ()), *deps)


def _split_bf16(v):
    hi = v.astype(BF16)
    lo = (v - hi.astype(F32)).astype(BF16)
    return hi, lo


def _pool_forward(proj, w_pool, pool_scale, deps=()):
    s = proj.shape[0]
    g_n, cg, _ = w_pool.shape
    t = POOL_TILE
    nt = s // t

    def body(cur_ref, prev_ref, wp_ref, sc_ref, pooled_ref, ya_ref):
        g = pl.program_id(0)
        ti = pl.program_id(1)
        win = jnp.left_shift(2, g)
        row = lax.broadcasted_iota(jnp.int32, (t, t), 0)
        col = lax.broadcasted_iota(jnp.int32, (t, t), 1)
        lag = row - col
        band_cur = ((lag >= 0) & (lag < win)).astype(BF16)
        band_prev = ((lag + t < win) & (ti > 0)).astype(BF16)
        u = cur_ref[...]
        u_hi, u_lo = _split_bf16(u)
        p_hi, p_lo = _split_bf16(prev_ref[...])
        wsum = (_dot(band_cur, u_hi) + _dot(band_cur, u_lo)
                + _dot(band_prev, p_hi) + _dot(band_prev, p_lo))
        tok = ti * t + lax.broadcasted_iota(jnp.int32, (t, 1), 0)
        count = jnp.minimum(tok + 1, win).astype(F32)
        pooled = (wsum / count - u).astype(BF16)
        pooled_ref[...] = pooled
        ya_ref[...] = (_dot(pooled, wp_ref[...]) * sc_ref[...]).astype(BF16)

    blk = pl.BlockSpec((t, cg), lambda g, i: (i, g))
    return pl.pallas_call(
        _with_deps(body, 4, deps), name="pool_fwd", grid=(g_n, nt),
        out_shape=(_sds((s, g_n * cg), BF16), _sds((s, g_n * cg), BF16)),
        in_specs=[blk, pl.BlockSpec((t, cg), lambda g, i: (jnp.maximum(i - 1, 0), g)),
                  pl.BlockSpec((None, cg, cg), lambda g, i: (g, 0, 0)),
                  pl.BlockSpec((1, cg), lambda g, i: (0, g))] + [_ANY] * len(deps),
        out_specs=(blk, blk), compiler_params=_params("parallel", "parallel"),
    )(proj, proj, w_pool, pool_scale, *deps)


def _pool_backward(dya, pooled, w_pool, pool_scale, dproj):
    s = dya.shape[0]
    g_n, cg, _ = w_pool.shape
    t = POOL_TILE
    nt = s // t

    def body(dya_ref, dya_next_ref, pooled_ref, wp_ref, sc_ref, dproj_in, du_ref, gw_ref, gs_ref):
        del dproj_in
        g = pl.program_id(0)
        ti = pl.program_id(1)

        @pl.when(ti == 0)
        def _():
            gw_ref[...] = jnp.zeros_like(gw_ref)
            gs_ref[...] = jnp.zeros_like(gs_ref)

        win = jnp.left_shift(2, g)
        wp = wp_ref[...]
        sc = sc_ref[...]
        pooled_v = pooled_ref[...]
        dya_v = dya_ref[...].astype(F32)
        mixed = _dot(pooled_v, wp)
        gs_ref[...] += jnp.sum(dya_v * mixed, axis=0, keepdims=True)
        dmixed = (dya_v * sc).astype(BF16)
        gw_ref[...] += _dot(pooled_v, dmixed, "tn")
        dpooled = _dot(dmixed, wp, "nt")
        dmixed_next = (dya_next_ref[...].astype(F32) * sc).astype(BF16)
        dpooled_next = _dot(dmixed_next, wp, "nt")
        tok = ti * t + lax.broadcasted_iota(jnp.int32, (t, 1), 0)
        e_cur = dpooled / jnp.minimum(tok + 1, win).astype(F32)
        e_next = dpooled_next / jnp.minimum(tok + t + 1, win).astype(F32)
        row = lax.broadcasted_iota(jnp.int32, (t, t), 0)
        col = lax.broadcasted_iota(jnp.int32, (t, t), 1)
        lead = col - row
        band_cur = ((lead >= 0) & (lead < win)).astype(BF16)
        band_next = ((lead + t < win) & (ti < nt - 1)).astype(BF16)
        c_hi, c_lo = _split_bf16(e_cur)
        n_hi, n_lo = _split_bf16(e_next)
        du = (_dot(band_cur, c_hi) + _dot(band_cur, c_lo)
              + _dot(band_next, n_hi) + _dot(band_next, n_lo)) - dpooled
        du_ref[...] = du.astype(BF16)

    blk = pl.BlockSpec((t, cg), lambda g, i: (i, g))
    du, gw, gs = pl.pallas_call(
        body, name="pool_bwd", grid=(g_n, nt),
        out_shape=(_sds(dproj.shape, BF16), _sds((g_n, cg, cg), F32), _sds((1, g_n * cg), F32)),
        in_specs=[blk, pl.BlockSpec((t, cg), lambda g, i: (jnp.minimum(i + 1, nt - 1), g)), blk,
                  pl.BlockSpec((None, cg, cg), lambda g, i: (g, 0, 0)),
                  pl.BlockSpec((1, cg), lambda g, i: (0, g)),
                  pl.BlockSpec(memory_space=pl.ANY)],
        out_specs=(blk, pl.BlockSpec((None, cg, cg), lambda g, i: (g, 0, 0)),
                   pl.BlockSpec((1, cg), lambda g, i: (0, g))),
        input_output_aliases={5: 0}, compiler_params=_params("parallel", "arbitrary"),
    )(dya, dya, pooled, w_pool, pool_scale, dproj)
    return du, gw, gs


def _qkv_prepare(proj, q_norm_w, k_norm_w, width, deps=()):
    s = proj.shape[0]
    tm = _tile(s, 256)
    heads = width // HEAD_DIM

    def body(q_ref, k_ref, v_ref, qw_ref, kw_ref, qn_ref, kn_ref, vb_ref):
        for h in range(heads):
            cols = slice(h * HEAD_DIM, (h + 1) * HEAD_DIM)
            for src, w_ref, dst in ((q_ref, qw_ref, qn_ref), (k_ref, kw_ref, kn_ref)):
                v = src[:, cols]
                r = lax.rsqrt(jnp.mean(v * v, axis=-1, keepdims=True) + NORM_EPS)
                dst[:, cols] = (v * r * w_ref[...]).astype(BF16)
        vb_ref[...] = v_ref[...].astype(BF16)

    vec = pl.BlockSpec((1, HEAD_DIM), lambda i: (0, 0))
    out_spec = pl.BlockSpec((tm, width), lambda i: (i, 0))
    return pl.pallas_call(
        _with_deps(body, 5, deps), name="qkv_prep", grid=(s // tm,),
        out_shape=(_sds((s, width), BF16),) * 3,
        in_specs=[pl.BlockSpec((tm, width), lambda i: (i, 1)), pl.BlockSpec((tm, width), lambda i: (i, 2)),
                  pl.BlockSpec((tm, width), lambda i: (i, 3)), vec, vec] + [_ANY] * len(deps),
        out_specs=(out_spec,) * 3, compiler_params=_params("parallel"),
    )(proj, proj, proj, q_norm_w, k_norm_w, *deps)


def _qk_norm_backward(name, dn, proj, col_block, norm_w, dproj, width, deps=()):
    s = proj.shape[0]
    tm = _tile(s, 256)
    heads = width // HEAD_DIM

    def body(dn_ref, q_ref, w_ref, dproj_in, dq_ref, gw_ref):
        del dproj_in

        @pl.when(pl.program_id(0) == 0)
        def _():
            gw_ref[...] = jnp.zeros_like(gw_ref)

        wv = w_ref[...]
        gw = jnp.zeros((1, HEAD_DIM), F32)
        for h in range(heads):
            cols = slice(h * HEAD_DIM, (h + 1) * HEAD_DIM)
            v = q_ref[:, cols]
            g = dn_ref[:, cols]
            r = lax.rsqrt(jnp.mean(v * v, axis=-1, keepdims=True) + NORM_EPS)
            n = v * r
            gw = gw + jnp.sum(g * n, axis=0, keepdims=True)
            gn = g * wv
            dq_ref[:, cols] = (r * (gn - n * jnp.mean(gn * n, axis=-1, keepdims=True))).astype(BF16)
        gw_ref[...] += gw

    blk = pl.BlockSpec((tm, width), lambda i: (i, col_block))
    return pl.pallas_call(
        _with_deps(body, 4, deps), name=name, grid=(s // tm,),
        out_shape=(_sds(dproj.shape, BF16), _sds((1, HEAD_DIM), F32)),
        in_specs=[pl.BlockSpec((tm, width), lambda i: (i, 0)), blk,
                  pl.BlockSpec((1, HEAD_DIM), lambda i: (0, 0)), pl.BlockSpec(memory_space=pl.ANY)]
        + [_ANY] * len(deps),
        out_specs=(blk, pl.BlockSpec((1, HEAD_DIM), lambda i: (0, 0))),
        input_output_aliases={3: 0}, compiler_params=_params("arbitrary"),
    )(dn, proj, norm_w, dproj, *deps)


def _strict_upper(n):
    row = lax.broadcasted_iota(jnp.int32, (n, n), 0)
    col = lax.broadcasted_iota(jnp.int32, (n, n), 1)
    return (row > col).astype(BF16)


def _strict_lower(n):
    row = lax.broadcasted_iota(jnp.int32, (n, n), 0)
    col = lax.broadcasted_iota(jnp.int32, (n, n), 1)
    return (row < col).astype(BF16)


def _cumulate(v, tri):
    hi, lo = _split_bf16(v)
    return _dot(hi, tri) + _dot(lo, tri)


def _log_sigmoid(z):
    return jnp.minimum(z, 0.0) - jnp.log(1.0 + jnp.exp(-jnp.abs(z)))


def _attention_forward(qn, kn, vb):
    s, width = qn.shape
    heads = width // HEAD_DIM
    tq, tk = Q_TILE, K_TILE
    hp = HEADS_PER_STEP
    assert tq == tk and s % tq == 0 and heads % hp == 0
    scale = 1.0 / math.sqrt(HEAD_DIM)

    def body(q_ref, k_ref, v_ref, o_ref, a_scr):
        qi = pl.program_id(1)
        upper = _strict_upper(tk)
        causal = lax.broadcasted_iota(jnp.int32, (tq, tk), 1) < lax.broadcasted_iota(jnp.int32, (tq, tk), 0)
        head_cols = [slice(u * HEAD_DIM, (u + 1) * HEAD_DIM) for u in range(hp)]

        def weights(kb, carry, masked):
            rows = pl.ds(pl.multiple_of(kb * tk, tk), tk)
            out = []
            for u, cols in enumerate(head_cols):
                later = carry[u]
                z = _dot(q_ref[:, cols], k_ref[rows, cols], "nt") * scale
                log_beta = _log_sigmoid(z)
                l = log_beta - z
                if masked:
                    l = jnp.where(causal, l, 0.0)
                a = jnp.exp(log_beta + _cumulate(l, upper) + later)
                if masked:
                    a = jnp.where(causal, a, 0.0)
                a_scr[u, :, rows] = a.astype(BF16)
                out.append(later + jnp.sum(l, axis=1, keepdims=True))
            return tuple(out)

        later = weights(qi, tuple(jnp.zeros((tq, 1), F32) for _ in range(hp)), True)
        lax.fori_loop(0, qi, lambda i, c: weights(qi - 1 - i, c, False), later)

        def mix(kb, accs):
            rows = pl.ds(pl.multiple_of(kb * tk, tk), tk)
            return tuple(acc + _dot(a_scr[u, :, rows], v_ref[rows, cols])
                         for u, (acc, cols) in enumerate(zip(accs, head_cols)))

        accs = lax.fori_loop(0, qi + 1, mix, tuple(jnp.zeros((tq, HEAD_DIM), F32) for _ in range(hp)))
        for acc, cols in zip(accs, head_cols):
            o_ref[:, cols] = acc.astype(BF16)

    full = pl.BlockSpec((s, hp * HEAD_DIM), lambda h, i: (0, h))
    blk = pl.BlockSpec((tq, hp * HEAD_DIM), lambda h, i: (i, h))
    return pl.pallas_call(
        body, name="attn_fwd", grid=(heads // hp, s // tq), out_shape=_sds((s, width), BF16),
        in_specs=[blk, full, full], out_specs=blk, scratch_shapes=[pltpu.VMEM((hp, tq, s), BF16)],
        compiler_params=_params("parallel", "parallel"),
    )(qn, kn, vb)


def _attention_backward(qn, kn, vb, dout, dproj, v_col_block, deps=()):
    s, width = qn.shape
    heads = width // HEAD_DIM
    tq, tk = Q_TILE, K_TILE
    hp = HEADS_PER_STEP_BWD
    nq = s // tq
    scale = 1.0 / math.sqrt(HEAD_DIM)
    v_block0 = v_col_block * (heads // hp)

    def body(q_ref, k_ref, v_ref, do_ref, dproj_in, dq_ref, dk_ref, dv_ref, a_scr, lb_scr, dkt_scr, dvt_scr):
        del dproj_in
        qi = pl.program_id(1)

        @pl.when(qi == 0)
        def _():
            dkt_scr[...] = jnp.zeros_like(dkt_scr)
            dvt_scr[...] = jnp.zeros_like(dvt_scr)

        upper = _strict_upper(tk)
        lower = _strict_lower(tk)
        causal = lax.broadcasted_iota(jnp.int32, (tq, tk), 1) < lax.broadcasted_iota(jnp.int32, (tq, tk), 0)
        head_cols = [slice(u * HEAD_DIM, (u + 1) * HEAD_DIM) for u in range(hp)]

        def weights(kb, carry, masked):
            rows = pl.ds(pl.multiple_of(kb * tk, tk), tk)
            out = []
            for u, cols in enumerate(head_cols):
                later = carry[u]
                z = _dot(q_ref[:, cols], k_ref[rows, cols], "nt") * scale
                log_beta = _log_sigmoid(z)
                l = log_beta - z
                if masked:
                    l = jnp.where(causal, l, 0.0)
                a = jnp.exp(log_beta + _cumulate(l, upper) + later)
                if masked:
                    a = jnp.where(causal, a, 0.0)
                a_scr[u, :, rows] = a
                lb_scr[u, :, rows] = log_beta
                out.append(later + jnp.sum(l, axis=1, keepdims=True))
            return tuple(out)

        zeros = tuple(jnp.zeros((tq, 1), F32) for _ in range(hp))
        later = weights(qi, zeros, True)
        lax.fori_loop(0, qi, lambda i, c: weights(qi - 1 - i, c, False), later)

        q_t = [jnp.transpose(q_ref[:, cols].astype(F32)).astype(BF16) for cols in head_cols]
        do_t = [jnp.transpose(do_ref[:, cols].astype(F32)).astype(BF16) for cols in head_cols]

        def grads(kb, carry, masked):
            rows = pl.ds(pl.multiple_of(kb * tk, tk), tk)
            out = []
            for u, cols in enumerate(head_cols):
                dq, before = carry[u]
                k_blk = k_ref[rows, cols]
                a = a_scr[u, :, rows]
                beta = jnp.exp(lb_scr[u, :, rows])
                g = a * _dot(do_ref[:, cols], v_ref[rows, cols], "nt")
                p = _cumulate(g, lower) + before
                dz = g - (g + p) * beta
                if masked:
                    dz = jnp.where(causal, dz, 0.0)
                dz = (dz * scale).astype(BF16)
                dkt_scr[cols, rows] += _dot(q_t[u], dz)
                dvt_scr[cols, rows] += _dot(do_t[u], a.astype(BF16))
                out.append((dq + _dot(dz, k_blk), before + jnp.sum(g, axis=1, keepdims=True)))
            return tuple(out)

        carry = tuple((jnp.zeros((tq, HEAD_DIM), F32), jnp.zeros((tq, 1), F32)) for _ in range(hp))
        carry = lax.fori_loop(0, qi, lambda i, c: grads(i, c, False), carry)
        carry = grads(qi, carry, True)
        for u, cols in enumerate(head_cols):
            dq_ref[:, cols] = carry[u][0]

        @pl.when(qi == nq - 1)
        def _():
            dk_ref[...] = jnp.transpose(dkt_scr[...])
            dv_ref[...] = jnp.transpose(dvt_scr[...]).astype(BF16)

    wide = hp * HEAD_DIM
    full = pl.BlockSpec((s, wide), lambda h, i: (0, h))
    blk = pl.BlockSpec((tq, wide), lambda h, i: (i, h))
    return pl.pallas_call(
        _with_deps(body, 5, deps), name="attn_bwd", grid=(heads // hp, nq),
        out_shape=(_sds((s, width), F32), _sds((s, width), F32), _sds(dproj.shape, BF16)),
        in_specs=[blk, full, full, blk, pl.BlockSpec(memory_space=pl.ANY)] + [_ANY] * len(deps),
        out_specs=(blk, full, pl.BlockSpec((s, wide), lambda h, i: (0, v_block0 + h))),
        scratch_shapes=[pltpu.VMEM((hp, tq, s), F32), pltpu.VMEM((hp, tq, s), F32),
                        pltpu.VMEM((wide, s), F32), pltpu.VMEM((wide, s), F32)],
        input_output_aliases={4: 2}, compiler_params=_params("parallel", "arbitrary"),
    )(qn, kn, vb, dout, dproj, *deps)


def _place_columns(name, src, dst, col_block):
    s, w = src.shape
    tm = _tile(s, 512)

    def body(src_ref, dst_in, out_ref):
        del dst_in
        out_ref[...] = src_ref[...]

    return pl.pallas_call(
        body, name=name, grid=(s // tm,), out_shape=_sds(dst.shape, dst.dtype),
        in_specs=[pl.BlockSpec((tm, w), lambda i: (i, 0)), pl.BlockSpec(memory_space=pl.ANY)],
        out_specs=pl.BlockSpec((tm, w), lambda i: (i, col_block)),
        input_output_aliases={1: 0}, compiler_params=_params("parallel"),
    )(src, dst)


def _cast_into_slot(name, x, slot):
    r, c = x.shape
    tr = _tile(r, max(SUBLANES * 2, (1 << 20) // c), SUBLANES * 2)

    def body(slot_ref, x_ref, o_ref):
        del slot_ref
        o_ref[...] = x_ref[...].astype(BF16)

    grid_spec = pltpu.PrefetchScalarGridSpec(
        num_scalar_prefetch=1, grid=(r // tr,),
        in_specs=[pl.BlockSpec((tr, c), lambda i, slot_ref: (i, 0))],
        out_specs=pl.BlockSpec((None, tr, c), lambda i, slot_ref: (slot_ref[0], i, 0)))
    return pl.pallas_call(
        body, name=name, grid_spec=grid_spec, out_shape=_sds((N_DEV, r, c), BF16),
        compiler_params=_params("parallel"),
    )(slot, x)


def _adamw_update(gv, w_ref, m_ref, v_ref, d_ref, nm_ref, nv_ref):
    c1 = 1.0 - ADAM_B1 ** ADAM_STEP
    c2 = 1.0 - ADAM_B2 ** ADAM_STEP
    nm = ADAM_B1 * m_ref[...] + (1.0 - ADAM_B1) * gv
    nv = ADAM_B2 * v_ref[...] + (1.0 - ADAM_B2) * (gv * gv)
    d_ref[...] = -ADAM_LR * ((nm / c1) / (jnp.sqrt(nv / c2) + ADAM_EPS) + ADAM_WD * w_ref[...])
    nm_ref[...] = nm
    nv_ref[...] = nv


def _adamw(name, w, g, m, v):
    r, c = w.shape
    tr = _tile(r, max(SUBLANES, (1 << 19) // c))

    def body(w_ref, g_ref, m_ref, v_ref, d_ref, nm_ref, nv_ref):
        _adamw_update(g_ref[...], w_ref, m_ref, v_ref, d_ref, nm_ref, nv_ref)

    blk = pl.BlockSpec((tr, c), lambda i: (i, 0))
    return pl.pallas_call(
        body, name=name, grid=(r // tr,), out_shape=(_sds((r, c), F32),) * 3,
        in_specs=[blk] * 4, out_specs=(blk,) * 3, compiler_params=_params("parallel"),
    )(w, g, m, v)


def _adamw_summed(name, w, own, received, m, v):
    r, c = w.shape
    nj = received.shape[0]
    tr = _tile(r, max(2 * SUBLANES, (1 << 19) // c), 2 * SUBLANES)

    def body(w_ref, own_ref, rec_ref, m_ref, v_ref, g_ref, d_ref, nm_ref, nv_ref):
        gv = own_ref[...].astype(F32)
        for j in range(nj):
            gv = gv + rec_ref[j].astype(F32)
        g_ref[...] = gv
        _adamw_update(gv, w_ref, m_ref, v_ref, d_ref, nm_ref, nv_ref)

    blk = pl.BlockSpec((tr, c), lambda i: (i, 0))
    return pl.pallas_call(
        body, name=name, grid=(r // tr,), out_shape=(_sds((r, c), F32),) * 4,
        in_specs=[blk, pl.BlockSpec((None, tr, c), lambda i: (0, i, 0)),
                  pl.BlockSpec((nj, tr, c), lambda i: (0, i, 0)), blk, blk],
        out_specs=(blk,) * 4, compiler_params=_params("parallel"),
    )(w, own, received, m, v)


def _rows_of_lanes(v):
    rows = v.shape[1] // LANES
    out = v.reshape(rows, LANES)
    pad = (-rows) % SUBLANES
    if pad:
        out = jnp.pad(out, ((0, pad), (0, 0)))
    return out


def kernel(x, c, w_ada, b_ada, norm1_w, w_in, q_norm_w, k_norm_w, w_pool, pool_scale, w_a_up, w_b_up, w_o, norm2_w, w_ff1, w_ff2, loss_target, m_w_ada, m_b_ada, m_norm1_w, m_w_in, m_q_norm_w, m_k_norm_w, m_w_pool, m_pool_scale, m_w_a_up, m_w_b_up, m_w_o, m_norm2_w, m_w_ff1, m_w_ff2, v_w_ada, v_b_ada, v_norm1_w, v_w_in, v_q_norm_w, v_k_norm_w, v_w_pool, v_pool_scale, v_w_a_up, v_w_b_up, v_w_o, v_norm2_w, v_w_ff1, v_w_ff2):
    _, s, d = x.shape
    half = d // 2
    d8 = d // N_DEV
    n_groups = len(POOL_WINDOWS)
    cg = half // n_groups
    me = _group_index(MESH_AXES)

    x2 = x[0]
    target = loss_target[0]

    my_slot = jnp.reshape(me, (1,)).astype(jnp.int32)

    def cast(i, t):
        return _cast_into_slot("cast_w%d" % i, t, my_slot)

    def gather_start(tag, bufs, phase):
        if phase < 2:
            return _launch("ag%s_ici%d" % (tag, phase), bufs, _plan_gather_ici(phase), (2, 4)[phase] * len(bufs))
        return _launch("ag%s_d2d" % tag, bufs, _plan_gather_d2d, len(_CHIP_MASKS) * len(bufs))

    buf_a = [cast(0, w_in[0])]
    fl_a, tok = gather_start("A", buf_a, 0)

    c_all = _all_gather_2d("ag_c", c.reshape(d // LANES, LANES), deps=(tok,)).reshape(N_DEV, d)
    wa = w_ada.shape[2]
    b_shard = lax.dynamic_slice_in_dim(b_ada, me * wa, wa, axis=1)
    mod_part = _ada_forward(c_all, w_ada[0], b_shard)
    mod_all = _all_gather_2d("ag_mod", mod_part.reshape(N_DEV * wa // LANES, LANES))
    mod_all = mod_all.reshape(N_DEV, N_DEV, wa)
    mod = lax.dynamic_slice_in_dim(mod_all, me, 1, axis=1).reshape(1, N_MOD * d)
    shift1, scale1, gate1, shift2, scale2, gate2 = [mod[:, i * d:(i + 1) * d] for i in range(N_MOD)]
    buf_b = [cast(1, w_pool[0].reshape(-1, cg)), cast(2, w_a_up[0]), cast(3, w_b_up[0]), cast(4, w_o[0])]
    buf_c = [cast(5, w_ff1[0])]
    buf_e = [cast(6, w_ff2[0])]

    buf_a = _land(fl_a, mod)
    fl_a, tok = gather_start("A", buf_a, 1)
    h = _norm_forward("norm1_fwd", x2, norm1_w, scale1, shift1, deps=(tok,))
    buf_a = _land(fl_a, [h] + buf_b + buf_c + buf_e)
    fl_b, tok_b = gather_start("B", buf_b, 0)
    fl_c, tok_c = gather_start("C", buf_c, 0)
    fl_e, tok_e = gather_start("E", buf_e, 0)
    fl_a, tok = gather_start("A", buf_a, 2)
    w_in_f, = _land(fl_a, [tok, tok_b, tok_c, tok_e])

    tm = _tile(s, 1024)
    tk = _tile(d, 2048)
    te = _tile(d, 512)

    proj = _matmul(
        "proj", "nn", (s // tm, N_DEV, d // tk), h, pl.BlockSpec((tm, tk), lambda i, j, k: (i, k)),
        w_in_f, pl.BlockSpec((None, tk, half), lambda i, j, k: (j, k, 0)),
        [_sds((s, 4 * d), F32)], [pl.BlockSpec((tm, half), lambda i, j, k: (i, j))], (tm, half))[0]
    buf_b = _land(fl_b, proj)
    buf_c = _land(fl_c, proj)
    fl_b, tok_b = gather_start("B", buf_b, 1)
    fl_c, tok_c = gather_start("C", buf_c, 1)

    qn, kn, vb = _qkv_prepare(proj, q_norm_w, k_norm_w, half, deps=(tok_b, tok_c))
    attn = _attention_forward(qn, kn, vb)
    buf_b = _land(fl_b, attn)
    buf_e = _land(fl_e, attn)
    fl_b, tok_b = gather_start("B", buf_b, 2)
    fl_e, tok_e = gather_start("E", buf_e, 1)
    w_pool_f, w_a_f, w_b_f, w_o_f = _land(fl_b, [tok_b, tok_e])
    rows_pool = cg // N_DEV
    w_pool_f = w_pool_f.reshape(N_DEV, n_groups, rows_pool, cg).transpose(1, 0, 2, 3).reshape(n_groups, cg, cg)
    w_o_f = w_o_f.reshape(d, d)
    pooled, ya_in = _pool_forward(proj, w_pool_f, pool_scale)

    def merge_epilogue(ga_ref, gb_ref, ya, yb, out_refs):
        merged_ref, ya_ref, yb_ref = out_refs
        merged = jax.nn.sigmoid(ga_ref[...]) * ya + jax.nn.sigmoid(gb_ref[...]) * yb
        merged_ref[...] = merged.astype(BF16)
        ya_ref[...] = ya.astype(BF16)
        yb_ref[...] = yb.astype(BF16)

    def up_body(a1_ref, b1_ref, a2_ref, b2_ref, ga_ref, gb_ref, *out_refs):
        merge_epilogue(ga_ref, gb_ref, _dot(a1_ref[...], b1_ref[...]), _dot(a2_ref[...], b2_ref[...]), out_refs)

    ga_blk0 = 2 * d // d8
    gb_blk0 = 3 * d // d8
    a_spec = pl.BlockSpec((tm, half), lambda i, j: (i, 0))
    wup_spec = pl.BlockSpec((None, half, d8), lambda i, j: (j, 0, 0))
    o_blk = pl.BlockSpec((tm, d8), lambda i, j: (i, j))
    merged, y_a, y_b = pl.pallas_call(
        up_body, name="up_merge", grid=(s // tm, N_DEV), out_shape=(_sds((s, d), BF16),) * 3,
        in_specs=[a_spec, wup_spec, a_spec, wup_spec,
                  pl.BlockSpec((tm, d8), lambda i, j: (i, ga_blk0 + j)),
                  pl.BlockSpec((tm, d8), lambda i, j: (i, gb_blk0 + j))],
        out_specs=(o_blk,) * 3, compiler_params=_params("parallel", "parallel"),
    )(ya_in, w_a_f, attn, w_b_f, proj, proj)
    buf_c = _land(fl_c, merged)
    fl_c, tok_c = gather_start("C", buf_c, 2)

    tn = _tile(d, 1024)

    def oproj_epilogue(acc, extra_refs, out_refs):
        x_ref, g_ref = extra_refs
        x1_ref, o_ref = out_refs
        x1_ref[...] = x_ref[...] + g_ref[...] * acc
        o_ref[...] = acc.astype(BF16)

    mn_blk = pl.BlockSpec((tm, tn), lambda i, j, k: (i, j))
    e_blk = pl.BlockSpec((tm, te), lambda i, j, k: (i, j))
    e_vec = pl.BlockSpec((1, te), lambda i, j, k: (0, j))
    x1, o_act = _matmul(
        "oproj", "nn", (s // tm, d // te, d // tk), merged, pl.BlockSpec((tm, tk), lambda i, j, k: (i, k)),
        w_o_f, pl.BlockSpec((tk, te), lambda i, j, k: (k, j)),
        [_sds((s, d), F32), _sds((s, d), BF16)], [e_blk, e_blk], (tm, te),
        epilogue=oproj_epilogue, extras=(x2, gate1), extra_specs=(e_blk, e_vec), deps=(tok_c,))

    h2 = _norm_forward("norm2_fwd", x1, norm2_w, scale2, shift2)
    buf_e = _land(fl_e, h2)
    fl_e, tok_e = gather_start("E", buf_e, 2)
    w_ff1_f, = _land(fl_c, [h2, tok_e])

    def ff1_epilogue(acc, extra_refs, out_refs):
        r = jnp.maximum(acc, 0.0)
        out_refs[0][...] = r.astype(BF16)
        out_refs[1][...] = (r * r).astype(BF16)

    ff_blk = pl.BlockSpec((tm, half), lambda i, j, k: (i, j))
    relu, act = _matmul(
        "ff1", "nn", (s // tm, N_DEV, d // tk), h2, pl.BlockSpec((tm, tk), lambda i, j, k: (i, k)),
        w_ff1_f, pl.BlockSpec((None, tk, half), lambda i, j, k: (j, k, 0)),
        [_sds((s, 4 * d), BF16)] * 2, [ff_blk, ff_blk], (tm, half), epilogue=ff1_epilogue)
    w_ff2_f, = _land(fl_e, act)
    w_ff2_f = w_ff2_f.reshape(4 * d, d)

    def ff2_epilogue(acc, extra_refs, out_refs):
        x1_ref, g_ref, t_ref = extra_refs
        df_ref, dy_ref, sq_ref, dgate_ref = out_refs
        gate = g_ref[...]
        err = x1_ref[...] + gate * acc - t_ref[...]
        dyv = err * (1.0 / d)
        dy_ref[...] = dyv
        df_ref[...] = (dyv * gate).astype(BF16)
        sq_ref[...] = jnp.full(sq_ref.shape, jnp.sum(err * err), F32)
        dgate_ref[...] = jnp.broadcast_to(jnp.sum(dyv * acc, axis=0, keepdims=True), dgate_ref.shape)

    df, dy, sq, dgate2_parts = _matmul(
        "ff2", "nn", (s // tm, d // te, 2 * d // tk), act, pl.BlockSpec((tm, 2 * tk), lambda i, j, k: (i, k)),
        w_ff2_f, pl.BlockSpec((2 * tk, te), lambda i, j, k: (k, j)),
        [_sds((s, d), BF16), _sds((s, d), F32), _sds((s // tm * SUBLANES, d // te * LANES), F32),
         _sds((s // tm * SUBLANES, d), F32)],
        [e_blk, e_blk, pl.BlockSpec((SUBLANES, LANES), lambda i, j, k: (i, j)),
         pl.BlockSpec((SUBLANES, te), lambda i, j, k: (i, j))], (tm, te),
        epilogue=ff2_epilogue, extras=(x1, gate2, target), extra_specs=(e_blk, e_vec, e_blk))
    loss_local = (0.5 / d) * jnp.sum(sq[::SUBLANES, ::LANES])
    dgate2 = jnp.sum(dgate2_parts[::SUBLANES], axis=0, keepdims=True)

    tok_k = _tile(s, 2048)
    tw = _tile(d, 1024)
    g_ff2 = _matmul(
        "g_ff2", "tn", (4 * d // tw, d // tn, s // tok_k), act, pl.BlockSpec((tok_k, tw), lambda i, j, k: (k, i)),
        df, pl.BlockSpec((tok_k, tn), lambda i, j, k: (k, j)),
        [_sds((4 * d, d), BF16)], [pl.BlockSpec((tw, tn), lambda i, j, k: (i, j))], (tw, tn))[0]

    def da_epilogue(acc, extra_refs, out_refs):
        out_refs[0][...] = (acc * (2.0 * extra_refs[0][...].astype(F32))).astype(BF16)

    big_blk = pl.BlockSpec((tm, tn), lambda i, j, k: (i, j))
    fl_f2, tok = _reduce_scatter_start("F2", [g_ff2.reshape(N_DEV, half, d)])
    df1 = _matmul(
        "da_ff", "nt", (s // tm, 4 * d // tn, d // tk), df, pl.BlockSpec((tm, tk), lambda i, j, k: (i, k)),
        w_ff2_f, pl.BlockSpec((tn, tk), lambda i, j, k: (j, k)),
        [_sds((s, 4 * d), BF16)], [big_blk], (tm, tn),
        epilogue=da_epilogue, extras=(relu,), extra_specs=(big_blk,), deps=(tok,))[0]

    fl_f2, tok = _reduce_scatter_middle("F2", fl_f2, df1, me)
    g_ff1 = _matmul(
        "g_ff1", "tn", (d // tw, N_DEV, s // tok_k), h2, pl.BlockSpec((tok_k, tw), lambda i, j, k: (k, i)),
        df1, pl.BlockSpec((tok_k, half), lambda i, j, k: (k, j)),
        [_sds((N_DEV, d, half), BF16)], [pl.BlockSpec((None, tw, half), lambda i, j, k: (j, i, 0))], (tw, half),
        deps=(tok,))[0]

    fl_f1, tok = _reduce_scatter_start("F1", [g_ff1])
    dh2 = _matmul(
        "dh2", "nt", (s // tm, d // tn, N_DEV // 2), df1, pl.BlockSpec((tm, 2 * half), lambda i, j, k: (i, k)),
        w_ff1_f, pl.BlockSpec((2, tn, half), lambda i, j, k: (k, j, 0)),
        [_sds((s, d), F32)], [mn_blk], (tm, tn), deps=(tok,))[0]

    sum_ff2, = _reduce_scatter_finish(fl_f2, dh2)
    fl_f1, tok = _reduce_scatter_middle("F1", fl_f1, dh2, me)
    dx1, dshift2, dscale2, g_norm2, do, dgate1 = _norm_backward(
        "norm2_bwd", dh2, x1, norm2_w, scale2, dy, gated=(gate1, o_act), deps=(tok,))

    g_o = _matmul(
        "g_o", "tn", (d // tw, d // tn, s // tok_k), merged, pl.BlockSpec((tok_k, tw), lambda i, j, k: (k, i)),
        do, pl.BlockSpec((tok_k, tn), lambda i, j, k: (k, j)),
        [_sds((d, d), BF16)], [pl.BlockSpec((tw, tn), lambda i, j, k: (i, j))], (tw, tn))[0]

    def merge_bwd_epilogue(acc, extra_refs, out_refs):
        ga_ref, gb_ref, ya_ref, yb_ref = extra_refs
        dya_ref, dyb_ref, dga_ref, dgb_ref = out_refs
        sa = jax.nn.sigmoid(ga_ref[...])
        sb = jax.nn.sigmoid(gb_ref[...])
        dya_ref[...] = (acc * sa).astype(BF16)
        dyb_ref[...] = (acc * sb).astype(BF16)
        dga_ref[...] = (acc * ya_ref[...].astype(F32) * (sa * (1.0 - sa))).astype(BF16)
        dgb_ref[...] = (acc * yb_ref[...].astype(F32) * (sb * (1.0 - sb))).astype(BF16)

    nb = d // te
    dy_a, dy_b, dproj, dg_b = _matmul(
        "dmerged", "nt", (s // tm, nb, d // tk), do, pl.BlockSpec((tm, tk), lambda i, j, k: (i, k)),
        w_o_f, pl.BlockSpec((te, tk), lambda i, j, k: (j, k)),
        [_sds((s, d), BF16), _sds((s, d), BF16), _sds((s, 4 * d), BF16), _sds((s, d), BF16)],
        [e_blk, e_blk, pl.BlockSpec((tm, te), lambda i, j, k: (i, 2 * nb + j)), e_blk], (tm, te),
        epilogue=merge_bwd_epilogue, extras=(proj, proj, y_a, y_b),
        extra_specs=(pl.BlockSpec((tm, te), lambda i, j, k: (i, 2 * nb + j)),
                     pl.BlockSpec((tm, te), lambda i, j, k: (i, 3 * nb + j)), e_blk, e_blk))
    dproj = _place_columns("place_dgb", dg_b, dproj, 3)

    up_a = pl.BlockSpec((tok_k, half), lambda i, j, k: (k, 0))
    up_b = pl.BlockSpec((tok_k, d8), lambda i, j, k: (k, j))
    up_o = pl.BlockSpec((None, half, d8), lambda i, j, k: (j, 0, 0))
    g_a_up = _matmul("g_a_up", "tn", (1, N_DEV, s // tok_k), ya_in, up_a, dy_a, up_b,
                     [_sds((N_DEV, half, d8), BF16)], [up_o], (half, d8))[0]
    g_b_up = _matmul("g_b_up", "tn", (1, N_DEV, s // tok_k), attn, up_a, dy_b, up_b,
                     [_sds((N_DEV, half, d8), BF16)], [up_o], (half, d8))[0]
    dn_a = pl.BlockSpec((tm, d8), lambda i, j, k: (i, k))
    dn_b = pl.BlockSpec((None, half, d8), lambda i, j, k: (k, 0, 0))
    dn_o = pl.BlockSpec((tm, half), lambda i, j, k: (i, 0))
    dya_in = _matmul("d_ya_in", "nt", (s // tm, 1, N_DEV), dy_a, dn_a, w_a_f, dn_b,
                     [_sds((s, half), BF16)], [dn_o], (tm, half))[0]
    dattn = _matmul("d_attn", "nt", (s // tm, 1, N_DEV), dy_b, dn_a, w_b_f, dn_b,
                    [_sds((s, half), BF16)], [dn_o], (tm, half))[0]

    dproj, g_pool, g_pool_scale = _pool_backward(dya_in, pooled, w_pool_f, pool_scale, dproj)
    sum_ff1, = _reduce_scatter_finish(fl_f1, g_pool)
    g_pool_send = g_pool.astype(BF16).reshape(n_groups, N_DEV, rows_pool, cg).transpose(1, 0, 2, 3)
    g_pool_send = g_pool_send.reshape(N_DEV, n_groups * rows_pool, cg)
    fl_b, tok = _reduce_scatter_start("B", [g_pool_send, g_a_up, g_b_up, g_o.reshape(N_DEV, d8, d)])
    dqn, dkn, dproj = _attention_backward(qn, kn, vb, dattn, dproj, 3, deps=(tok,))
    fl_b, tok = _reduce_scatter_middle("B", fl_b, dqn, me)
    dproj, g_qnorm = _qk_norm_backward("qnorm_bwd", dqn, proj, 1, q_norm_w, dproj, half, deps=(tok,))
    dproj, g_knorm = _qk_norm_backward("knorm_bwd", dkn, proj, 2, k_norm_w, dproj, half)

    g_in = _matmul(
        "g_in", "tn", (d // tw, N_DEV, s // tok_k), h, pl.BlockSpec((tok_k, tw), lambda i, j, k: (k, i)),
        dproj, pl.BlockSpec((tok_k, half), lambda i, j, k: (k, j)),
        [_sds((N_DEV, d, half), BF16)], [pl.BlockSpec((None, tw, half), lambda i, j, k: (j, i, 0))], (tw, half))[0]
    fl_in, tok = _reduce_scatter_start("I", [g_in])
    dh = _matmul(
        "dh", "nt", (s // tm, d // tn, N_DEV // 2), dproj, pl.BlockSpec((tm, 2 * half), lambda i, j, k: (i, k)),
        w_in_f, pl.BlockSpec((2, tn, half), lambda i, j, k: (k, j, 0)),
        [_sds((s, d), F32)], [mn_blk], (tm, tn), deps=(tok,))[0]
    sum_pool, sum_a_up, sum_b_up, sum_o = _reduce_scatter_finish(fl_b, dh)
    grad_x, dshift1, dscale1, g_norm1 = _norm_backward("norm1_bwd", dh, x2, norm1_w, scale1, dx1)

    dmod = jnp.concatenate([dshift1, dscale1, dgate1, dshift2, dscale2, dgate2], axis=1)
    pieces = [dmod, g_norm1, g_norm2, g_pool_scale, g_qnorm, g_knorm, jnp.full((1, LANES), loss_local, F32)]
    packed_rows = [_rows_of_lanes(p) for p in pieces]
    offsets = [0]
    for p in packed_rows:
        offsets.append(offsets[-1] + p.shape[0])
    packed = jnp.concatenate(packed_rows, axis=0)
    small_all = _all_gather_2d("ag_small", packed)
    fl_in, tok = _reduce_scatter_middle("I", fl_in, small_all, me)
    small_sum = _sum_slots("small_sum", small_all[None], F32)[0]

    def unpack(i, width):
        return small_sum[offsets[i]:offsets[i] + width // LANES].reshape(1, width)

    g_b_ada = unpack(0, N_MOD * d)
    g_norm1_w = unpack(1, d)
    g_norm2_w = unpack(2, d)
    g_pool_scale_w = unpack(3, half)
    g_q_norm_w = unpack(4, HEAD_DIM)
    g_k_norm_w = unpack(5, HEAD_DIM)
    loss = unpack(6, LANES)[0, 0]
    dmod_all = small_all[:, :N_MOD * d // LANES].reshape(N_DEV, N_MOD * d)
    dmod_cols = lax.dynamic_slice_in_dim(dmod_all, me * wa, wa, axis=1)
    g_w_ada = _ada_weight_grad(c_all, dmod_cols, deps=(tok,))[None]


    grads = {
        "w_ada": g_w_ada, "b_ada": g_b_ada, "norm1_w": g_norm1_w,
        "q_norm_w": g_q_norm_w, "k_norm_w": g_k_norm_w,
        "pool_scale": g_pool_scale_w, "norm2_w": g_norm2_w,
    }
    sums = {"w_pool": sum_pool, "w_a_up": sum_a_up, "w_b_up": sum_b_up, "w_o": sum_o,
            "w_ff1": sum_ff1, "w_ff2": sum_ff2}
    weights = {"w_ada": (w_ada, m_w_ada, v_w_ada), "b_ada": (b_ada, m_b_ada, v_b_ada),
               "norm1_w": (norm1_w, m_norm1_w, v_norm1_w), "w_in": (w_in, m_w_in, v_w_in),
               "q_norm_w": (q_norm_w, m_q_norm_w, v_q_norm_w), "k_norm_w": (k_norm_w, m_k_norm_w, v_k_norm_w),
               "w_pool": (w_pool, m_w_pool, v_w_pool), "pool_scale": (pool_scale, m_pool_scale, v_pool_scale),
               "w_a_up": (w_a_up, m_w_a_up, v_w_a_up), "w_b_up": (w_b_up, m_w_b_up, v_w_b_up),
               "w_o": (w_o, m_w_o, v_w_o), "norm2_w": (norm2_w, m_norm2_w, v_norm2_w),
               "w_ff1": (w_ff1, m_w_ff1, v_w_ff1), "w_ff2": (w_ff2, m_w_ff2, v_w_ff2)}
    order = list(weights)
    deltas, new_m, new_v = {}, {}, {}
    def adam(name):
        wt, mt, vt = weights[name]
        shape = wt.shape
        flat = (-1, shape[-1])
        if name in sums:
            own, received = sums[name]
            g, dl, nm, nv = _adamw_summed("adamw_" + name, wt.reshape(flat), own, received,
                                          mt.reshape(flat), vt.reshape(flat))
            grads[name] = g.reshape(shape)
        else:
            dl, nm, nv = _adamw("adamw_" + name, wt.reshape(flat), grads[name].reshape(flat),
                                mt.reshape(flat), vt.reshape(flat))
        deltas[name], new_m[name], new_v[name] = dl.reshape(shape), nm.reshape(shape), nv.reshape(shape)

    others = [n for n in order if n != "w_in"]
    for name in others:
        adam(name)
    sums["w_in"], = _reduce_scatter_finish(fl_in, [deltas[n] for n in others])
    adam("w_in")

    return (loss, grad_x[None], *[grads[n] for n in order], *[deltas[n] for n in order],
            *[new_m[n] for n in order], *[new_v[n] for n in order])
```

```python
import math

import jax
import jax.numpy as jnp
from jax import lax
from jax.experimental import pallas as pl
from jax.experimental.pallas import tpu as pltpu

F32 = jnp.float32
BF16 = jnp.bfloat16
MESH_AXES = ("x", "y", "c")
N_DEV = 8
HEAD_DIM = 128
POOL_WINDOWS = (2, 4, 8, 16)
N_MOD = 6
NORM_EPS = 1e-6
LANES = 128
SUBLANES = 8
VMEM_LIMIT_BYTES = 56 * 1024 * 1024
Q_TILE = 256
K_TILE = 256
POOL_TILE = 256
HEADS_PER_STEP = 4
HEADS_PER_STEP_BWD = 4

ADAM_LR = 0.001
ADAM_B1 = 0.9
ADAM_B2 = 0.999
ADAM_EPS = 1e-08
ADAM_WD = 0.01
ADAM_STEP = 10

_NN = (((1,), (0,)), ((), ()))
_NT = (((1,), (1,)), ((), ()))
_TN = (((0,), (0,)), ((), ()))
_DIMS = {"nn": _NN, "nt": _NT, "tn": _TN}


def _dot(a, b, mode="nn"):
    return lax.dot_general(a, b, _DIMS[mode], preferred_element_type=F32)


def _params(*sem):
    return pltpu.CompilerParams(dimension_semantics=sem, vmem_limit_bytes=VMEM_LIMIT_BYTES)


def _tile(dim, pref, align=SUBLANES):
    for t in range(min(dim, pref), 0, -1):
        if dim % t == 0 and t % align == 0:
            return t
    return dim


def _group_index(axes):
    idx = 0
    for a in axes:
        idx = idx * 2 + lax.axis_index(a)
    return idx


def _peer_device(axes, k):
    coords = {a: lax.axis_index(a) for a in MESH_AXES}
    for pos, a in enumerate(axes):
        if (k >> (len(axes) - 1 - pos)) & 1:
            coords[a] = 1 - coords[a]
    return tuple(coords[a] for a in MESH_AXES)


_AXIS_BIT = {"x": 4, "y": 2, "c": 1}
_ANY = pl.BlockSpec(memory_space=pl.ANY)


def _device_xor(mask):
    return tuple(1 - lax.axis_index(a) if mask & _AXIS_BIT[a] else lax.axis_index(a) for a in MESH_AXES)


def _remote(src, dst, send_sem, recv_sem, mask):
    return pltpu.make_async_remote_copy(src_ref=src, dst_ref=dst, send_sem=send_sem, recv_sem=recv_sem,
                                        device_id=_device_xor(mask), device_id_type=pl.DeviceIdType.MESH)


def _start_all(copies):
    for cp in copies:
        cp.start()


def _wait_all(copies):
    for cp in copies:
        cp.wait()


def _gather_ici(name, bufs):
    na = len(bufs)
    bx, by = _AXIS_BIT["x"], _AXIS_BIT["y"]

    def body(*refs):
        out = refs[na:2 * na]
        send_sems, recv_sems = refs[2 * na:]
        me = _group_index(MESH_AXES)

        def copy(a, n, slot, color, mask):
            half = out[a].shape[1] // 2
            piece = out[a].at[slot, pl.ds(color * half, half)]
            return _remote(piece, piece, send_sems.at[a, n], recv_sems.at[a, n], mask)

        first, second = [], []
        for a in range(na):
            first += [copy(a, 0, me, 0, bx), copy(a, 1, me, 1, by)]
            second += [copy(a, 2, me, 0, by), copy(a, 3, me ^ bx, 0, by),
                       copy(a, 4, me, 1, bx), copy(a, 5, me ^ by, 1, bx)]
        _start_all(first)
        _wait_all(first)
        _start_all(second)
        _wait_all(second)

    return pl.pallas_call(
        body, name=name, out_shape=tuple(jax.ShapeDtypeStruct(b.shape, b.dtype) for b in bufs),
        in_specs=[_ANY] * na, out_specs=tuple([_ANY] * na),
        input_output_aliases={a: a for a in range(na)},
        scratch_shapes=[pltpu.SemaphoreType.DMA((na, 6)), pltpu.SemaphoreType.DMA((na, 6))],
    )(*bufs)


def _gather_d2d(name, bufs):
    na = len(bufs)
    masks = (0, _AXIS_BIT["y"], _AXIS_BIT["x"], _AXIS_BIT["x"] | _AXIS_BIT["y"])

    def body(*refs):
        out = refs[na:2 * na]
        send_sems, recv_sems = refs[2 * na:]
        me = _group_index(MESH_AXES)
        copies = []
        for a in range(na):
            for n, m in enumerate(masks):
                piece = out[a].at[me ^ m]
                copies.append(_remote(piece, piece, send_sems.at[a, n], recv_sems.at[a, n], _AXIS_BIT["c"]))
        _start_all(copies)
        _wait_all(copies)

    return pl.pallas_call(
        body, name=name, out_shape=tuple(jax.ShapeDtypeStruct(b.shape, b.dtype) for b in bufs),
        in_specs=[_ANY] * na, out_specs=tuple([_ANY] * na),
        input_output_aliases={a: a for a in range(na)},
        scratch_shapes=[pltpu.SemaphoreType.DMA((na, 4)), pltpu.SemaphoreType.DMA((na, 4))],
    )(*bufs)


def _scatter(name, srcs, send_slots, masks):
    na = len(srcs)
    nm = len(masks)

    def body(*refs):
        src, out = refs[:na], refs[na:2 * na]
        send_sems, recv_sems = refs[2 * na:]
        slots = send_slots(_group_index(MESH_AXES))
        copies = []
        for n, m in enumerate(masks):
            for a in range(na):
                copies.append(_remote(src[a].at[slots[n]], out[a].at[n],
                                      send_sems.at[a, n], recv_sems.at[a, n], m))
        _start_all(copies)
        _wait_all(copies)

    return pl.pallas_call(
        body, name=name,
        out_shape=tuple(jax.ShapeDtypeStruct((nm,) + s.shape[1:], s.dtype) for s in srcs),
        in_specs=[_ANY] * na, out_specs=tuple([_ANY] * na),
        scratch_shapes=[pltpu.SemaphoreType.DMA((na, nm)), pltpu.SemaphoreType.DMA((na, nm))],
    )(*srcs)


def _add_received(name, own, own_slots, received, out_dtype):
    nj, r, c = received.shape
    tr = _tile(r, max(2 * SUBLANES, (1 << 20) // c), 2 * SUBLANES)

    def body(slots_ref, own_ref, rec_ref, o_ref):
        del slots_ref
        o_ref[...] = (own_ref[...].astype(F32) + rec_ref[...].astype(F32)).astype(o_ref.dtype)

    grid_spec = pltpu.PrefetchScalarGridSpec(
        num_scalar_prefetch=1, grid=(nj, r // tr),
        in_specs=[pl.BlockSpec((None, tr, c), lambda j, i, slots: (slots[j], i, 0)),
                  pl.BlockSpec((None, tr, c), lambda j, i, slots: (j, i, 0))],
        out_specs=pl.BlockSpec((None, tr, c), lambda j, i, slots: (j, i, 0)))
    return pl.pallas_call(
        body, name=name, grid_spec=grid_spec, out_shape=jax.ShapeDtypeStruct((nj, r, c), out_dtype),
        compiler_params=_params("parallel", "parallel"),
    )(own_slots, own, received)


def _add_final(name, own, received):
    nj, r, c = received.shape
    tr = _tile(r, max(2 * SUBLANES, (1 << 19) // c), 2 * SUBLANES)

    def body(own_ref, rec_ref, o_ref):
        acc = own_ref[...].astype(F32)
        for j in range(nj):
            acc = acc + rec_ref[j].astype(F32)
        o_ref[...] = acc

    return pl.pallas_call(
        body, name=name, grid=(r // tr,), out_shape=jax.ShapeDtypeStruct((r, c), F32),
        in_specs=[pl.BlockSpec((None, tr, c), lambda i: (0, i, 0)),
                  pl.BlockSpec((nj, tr, c), lambda i: (0, i, 0))],
        out_specs=pl.BlockSpec((tr, c), lambda i: (i, 0)),
        compiler_params=_params("parallel"),
    )(own, received)


_HBM = pl.BlockSpec(memory_space=pltpu.HBM)
_SEM = pl.BlockSpec(memory_space=pltpu.SEMAPHORE)
_DATAFLOW = pltpu.SideEffectType.DATAFLOW_SIDE_EFFECTING


def _launch_groups(name, groups, deps=()):
    bufs = [b for g_bufs, _ in groups for b in g_bufs]
    specs = [(len(g_bufs), spec) for g_bufs, g_plans in groups for spec in g_plans]
    nb, ns = len(bufs), len(specs)

    def body(*refs):
        sems = refs[nb + len(deps):nb + len(deps) + 2 * ns]
        me = _group_index(MESH_AXES)
        first, which = 0, 0
        for g_bufs, g_plans in groups:
            ins = refs[first:first + len(g_bufs)]
            for _, plan, n_copies in g_plans:
                copies = plan(ins, me)
                assert len(copies) == n_copies
                for n, (src, dst, mask) in enumerate(copies):
                    _remote(src, dst, sems[2 * which].at[n], sems[2 * which + 1].at[n], mask).start()
                which += 1
            first += len(g_bufs)
        refs[-1][...] = jnp.zeros_like(refs[-1])

    sem_shapes = [pltpu.SemaphoreType.DMA((n,)) for _, (_, _, n) in specs for _ in range(2)]
    outs = pl.pallas_call(
        body, name=name,
        out_shape=(*sem_shapes, *[pltpu.HBM(b.shape, b.dtype) for b in bufs],
                   jax.ShapeDtypeStruct((SUBLANES, LANES), F32)),
        in_specs=[_HBM] * nb + [_ANY] * len(deps),
        out_specs=(*[_SEM] * (2 * ns), *[_HBM] * nb, pl.BlockSpec(memory_space=pltpu.VMEM)),
        input_output_aliases={i: 2 * ns + i for i in range(nb)},
        compiler_params=pltpu.CompilerParams(has_side_effects=_DATAFLOW),
    )(*[pltpu.with_memory_space_constraint(b, pltpu.HBM) for b in bufs], *deps)
    flights, first, which = [], 0, 0
    for g_bufs, g_plans in groups:
        through = list(outs[2 * ns + first:2 * ns + first + len(g_bufs)])
        for land_name, plan, n_copies in g_plans:
            flights.append((land_name, plan, n_copies, outs[2 * which], outs[2 * which + 1], through))
            which += 1
        first += len(g_bufs)
    return flights, outs[-1]


def _launch(name, bufs, plan, n_copies, deps=()):
    (flight,), token = _launch_groups(name, [(bufs, [(name, plan, n_copies)])], deps)
    return flight, token


def _land(flight, after, bufs=None):
    name, plan, n_copies, send_sems, recv_sems, launched = flight
    bufs = launched if bufs is None else bufs
    nb = len(bufs)
    after = list(after) if isinstance(after, (list, tuple)) else [after]

    def body(*refs):
        ins = refs[:nb]
        s_sems, r_sems = refs[nb], refs[nb + 1]
        for n, (src, dst, mask) in enumerate(plan(ins, _group_index(MESH_AXES))):
            cp = _remote(src, dst, s_sems.at[n], r_sems.at[n], mask)
            cp.wait_send()
            cp.wait_recv()

    outs = pl.pallas_call(
        body, name=name + "_land",
        out_shape=tuple(pltpu.HBM(b.shape, b.dtype) for b in bufs),
        in_specs=[_HBM] * nb + [_SEM, _SEM] + [_ANY] * len(after), out_specs=tuple([_HBM] * nb),
        input_output_aliases={i: i for i in range(nb)},
        compiler_params=pltpu.CompilerParams(has_side_effects=_DATAFLOW),
    )(*bufs, send_sems, recv_sems, *after)
    return list(outs)


def _plan_gather_ici(phase):
    bx, by = _AXIS_BIT["x"], _AXIS_BIT["y"]

    def plan(refs, me):
        copies = []
        for ref in refs:
            half = ref.shape[1] // 2

            def piece(slot, color, mask, ref=ref, half=half):
                p = ref.at[slot, pl.ds(color * half, half)]
                return (p, p, mask)

            if phase == 0:
                copies += [piece(me, 0, bx), piece(me, 1, by)]
            else:
                copies += [piece(me, 0, by), piece(me ^ bx, 0, by), piece(me, 1, bx), piece(me ^ by, 1, bx)]
        return copies

    return plan


def _plan_d2d(masks):
    def plan(refs, me):
        return [(ref.at[me ^ m], ref.at[me ^ m], _AXIS_BIT["c"]) for ref in refs for m in masks]

    return plan


def _plan_gather_d2d(refs, me):
    return _plan_d2d(_CHIP_MASKS)(refs, me)


def _plan_neighbours(refs, me):
    return [(ref.at[me], ref.at[me], _AXIS_BIT[a]) for ref in refs for a in ("x", "y")]


def _plan_diagonal(refs, me):
    bx, by = _AXIS_BIT["x"], _AXIS_BIT["y"]
    copies = []
    for ref in refs:
        half = ref.shape[1] // 2
        lo = ref.at[me ^ bx, pl.ds(0, half)]
        hi = ref.at[me ^ by, pl.ds(half, half)]
        copies += [(lo, lo, by), (hi, hi, bx)]
    return copies


def _plan_scatter_d2d(refs, me):
    na = len(refs) // 2
    copies = []
    for a in range(na):
        for j, m in enumerate(_CHIP_MASKS):
            copies.append((refs[a].at[me ^ _AXIS_BIT["c"] ^ m], refs[na + a].at[j], _AXIS_BIT["c"]))
    return copies


def _plan_scatter_ici(refs, me):
    del me
    na = len(refs) // 2
    copies = []
    for a in range(na):
        for n, m in enumerate(_CHIP_MASKS[1:]):
            copies.append((refs[a].at[n + 1], refs[na + a].at[n], m))
    return copies


def _with_deps(body, n_in, deps):
    if not deps:
        return body

    def wrapped(*refs):
        return body(*refs[:n_in], *refs[n_in + len(deps):])

    return wrapped


def _reduce_scatter_start(tag, grads):
    lands = [lax.empty((len(_CHIP_MASKS),) + g.shape[1:], g.dtype) for g in grads]
    return _launch("rs%s_d2d" % tag, list(grads) + lands, _plan_scatter_d2d, len(_CHIP_MASKS) * len(grads))


def _reduce_scatter_middle(tag, flight, after, me):
    bufs = _land(flight, after)
    na = len(bufs) // 2
    own_slots = jnp.stack([me ^ m for m in _CHIP_MASKS]).astype(jnp.int32)
    sums = [_add_received("rs%s_add_d2d_%d" % (tag, a), bufs[a], own_slots, bufs[na + a], BF16) for a in range(na)]
    lands = [lax.empty((len(_CHIP_MASKS) - 1,) + h.shape[1:], h.dtype) for h in sums]
    return _launch("rs%s_ici" % tag, sums + lands, _plan_scatter_ici, (len(_CHIP_MASKS) - 1) * na)


def _reduce_scatter_finish(flight, after):
    bufs = _land(flight, after)
    na = len(bufs) // 2
    return [(bufs[a], bufs[na + a]) for a in range(na)]


def _all_gather_2d(name, x, deps=()):
    r, c = x.shape

    def body(x_ref, out_ref, send_sems, recv_sems):
        me = _group_index(MESH_AXES)
        out_ref[me] = x_ref[...]
        copies = []
        for k in range(1, N_DEV):
            cp = pltpu.make_async_remote_copy(
                src_ref=x_ref, dst_ref=out_ref.at[me],
                send_sem=send_sems.at[k - 1], recv_sem=recv_sems.at[k - 1],
                device_id=_peer_device(MESH_AXES, k), device_id_type=pl.DeviceIdType.MESH)
            cp.start()
            copies.append(cp)
        for cp in copies:
            cp.wait()

    vmem = pl.BlockSpec(memory_space=pltpu.VMEM)
    return pl.pallas_call(
        _with_deps(body, 1, deps), name=name, out_shape=jax.ShapeDtypeStruct((N_DEV, r, c), x.dtype),
        in_specs=[vmem] + [_ANY] * len(deps), out_specs=vmem,
        scratch_shapes=[pltpu.SemaphoreType.DMA((N_DEV - 1,)), pltpu.SemaphoreType.DMA((N_DEV - 1,))],
    )(x, *deps)


def _sum_slots(name, buf, out_dtype):
    pre, n, r, c = buf.shape
    tr = _tile(r, max(SUBLANES * 2, (1 << 20) // c))

    def body(b_ref, o_ref):
        acc = b_ref[0].astype(F32)
        for q in range(1, n):
            acc = acc + b_ref[q].astype(F32)
        o_ref[...] = acc.astype(o_ref.dtype)

    return pl.pallas_call(
        body, name=name, grid=(pre, r // tr),
        out_shape=jax.ShapeDtypeStruct((pre, r, c), out_dtype),
        in_specs=[pl.BlockSpec((None, n, tr, c), lambda i, j: (i, 0, j, 0))],
        out_specs=pl.BlockSpec((None, tr, c), lambda i, j: (i, j, 0)),
        compiler_params=_params("parallel", "parallel"),
    )(buf)


def _all_gather_weights(bufs):
    return _gather_d2d("ag_d2d", list(_gather_ici("ag_ici", bufs)))


_CHIP_MASKS = (0, _AXIS_BIT["y"], _AXIS_BIT["x"], _AXIS_BIT["x"] | _AXIS_BIT["y"])


def _reduce_scatter_grads(bufs):
    me = _group_index(MESH_AXES)
    bc = _AXIS_BIT["c"]
    r1 = _scatter("rs_d2d", bufs, lambda i: [i ^ bc ^ m for m in _CHIP_MASKS], (bc,) * len(_CHIP_MASKS))
    own_slots = jnp.stack([me ^ m for m in _CHIP_MASKS]).astype(jnp.int32)
    half = [_add_received("rs_add_d2d_%d" % a, b, own_slots, r, BF16) for a, (b, r) in enumerate(zip(bufs, r1))]
    r2 = _scatter("rs_ici", half, lambda i: [1, 2, 3], _CHIP_MASKS[1:])
    return [_add_final("rs_add_ici_%d" % a, h, r) for a, (h, r) in enumerate(zip(half, r2))]


def _matmul(name, mode, grid, a, a_spec, b, b_spec, out_shapes, out_specs, acc_shape,
            epilogue=None, extras=(), extra_specs=(), aliases=None, deps=()):
    nk = grid[2]
    n_extra = len(extras)
    n_out = len(out_shapes)

    def finish(acc, extra_refs, out_refs):
        if epilogue is None:
            out_refs[0][...] = acc.astype(out_refs[0].dtype)
        else:
            epilogue(acc, extra_refs, out_refs)

    def product(a_ref, b_ref):
        if len(b_ref.shape) == 2:
            return _dot(a_ref[...], b_ref[...], mode)
        width = a_ref.shape[1] // b_ref.shape[0]
        total = None
        for i in range(b_ref.shape[0]):
            part = _dot(a_ref[:, i * width:(i + 1) * width], b_ref[i], mode)
            total = part if total is None else total + part
        return total

    def body(*refs):
        a_ref, b_ref = refs[0], refs[1]
        extra_refs = refs[2:2 + n_extra]
        out_refs = refs[2 + n_extra:2 + n_extra + n_out]
        if nk == 1:
            finish(product(a_ref, b_ref), extra_refs, out_refs)
            return
        acc_ref = refs[-1]
        k = pl.program_id(2)

        @pl.when(k == 0)
        def _():
            acc_ref[...] = product(a_ref, b_ref)

        @pl.when((k > 0) & (k < nk - 1))
        def _():
            acc_ref[...] += product(a_ref, b_ref)

        @pl.when(k == nk - 1)
        def _():
            finish(acc_ref[...] + product(a_ref, b_ref), extra_refs, out_refs)

    scratch = [] if nk == 1 else [pltpu.VMEM(acc_shape, F32)]
    return pl.pallas_call(
        _with_deps(body, 2 + n_extra, deps), name=name, grid=grid, out_shape=tuple(out_shapes),
        in_specs=[a_spec, b_spec] + list(extra_specs) + [_ANY] * len(deps), out_specs=tuple(out_specs),
        scratch_shapes=scratch, input_output_aliases=aliases or {},
        compiler_params=_params("parallel", "parallel", "arbitrary"),
    )(a, b, *extras, *deps)


def _sds(shape, dtype):
    return jax.ShapeDtypeStruct(tuple(shape), dtype)


def _project_slots(name, h, w_full, slots, out_cols, proj_in=None, deps=()):
    s, d = h.shape
    _, _, wide = w_full.shape
    tm = _tile(s, 1024)
    n_in = 4 if proj_in is not None else 3

    def body(*refs):
        refs[-1][...] = _dot(refs[1][...], refs[2][...])

    grid_spec = pltpu.PrefetchScalarGridSpec(
        num_scalar_prefetch=1, grid=(s // tm, slots.shape[0]),
        in_specs=[pl.BlockSpec((tm, d), lambda i, j, sl: (i, 0)),
                  pl.BlockSpec((None, d, wide), lambda i, j, sl: (sl[j], 0, 0))]
        + [_ANY] * (n_in - 3 + len(deps)),
        out_specs=pl.BlockSpec((tm, wide), lambda i, j, sl: (i, sl[j])))
    extra = ([proj_in] if proj_in is not None else []) + list(deps)
    return pl.pallas_call(
        body, name=name, grid_spec=grid_spec, out_shape=_sds((s, out_cols), F32),
        input_output_aliases={3: 0} if proj_in is not None else {},
        compiler_params=_params("parallel", "arbitrary"),
    )(slots, h, w_full, *extra)


def _ada_forward(c_all, w_ada, b_shard):
    nb, d = c_all.shape
    w = w_ada.shape[1]
    tn = _tile(w, 512)

    def body(c_ref, w_ref, b_ref, o_ref):
        cv = c_ref[...]
        sc = cv * jax.nn.sigmoid(cv)
        o_ref[...] = jnp.dot(sc, w_ref[...], precision=lax.Precision.HIGHEST,
                             preferred_element_type=F32) + b_ref[...]

    return pl.pallas_call(
        body, name="ada_fwd", grid=(w // tn,), out_shape=_sds((nb, w), F32),
        in_specs=[pl.BlockSpec((nb, d), lambda j: (0, 0)), pl.BlockSpec((d, tn), lambda j: (0, j)),
                  pl.BlockSpec((1, tn), lambda j: (0, j))],
        out_specs=pl.BlockSpec((nb, tn), lambda j: (0, j)),
        compiler_params=_params("parallel"),
    )(c_all, w_ada, b_shard)


def _ada_weight_grad(c_all, dmod_cols, deps=()):
    nb, d = c_all.shape
    w = dmod_cols.shape[1]
    tn = _tile(w, 512)

    def body(c_ref, g_ref, o_ref):
        cv = c_ref[...]
        sc = cv * jax.nn.sigmoid(cv)
        o_ref[...] = lax.dot_general(sc, g_ref[...], _TN, precision=lax.Precision.HIGHEST,
                                     preferred_element_type=F32)

    return pl.pallas_call(
        _with_deps(body, 2, deps), name="ada_wgrad", grid=(w // tn,), out_shape=_sds((d, w), F32),
        in_specs=[pl.BlockSpec((nb, d), lambda j: (0, 0)), pl.BlockSpec((nb, tn), lambda j: (0, j))]
        + [_ANY] * len(deps),
        out_specs=pl.BlockSpec((d, tn), lambda j: (0, j)),
        compiler_params=_params("parallel"),
    )(c_all, dmod_cols, *deps)


def _norm_forward(name, x, norm_w, scale, shift, deps=()):
    s, d = x.shape
    tm = _tile(s, 256)

    def body(x_ref, w_ref, sc_ref, sh_ref, h_ref):
        xv = x_ref[...]
        r = lax.rsqrt(jnp.mean(xv * xv, axis=-1, keepdims=True) + NORM_EPS)
        h = (xv * r * w_ref[...]) * (1.0 + sc_ref[...]) + sh_ref[...]
        h_ref[...] = h.astype(BF16)

    vec = pl.BlockSpec((1, d), lambda i: (0, 0))
    row = pl.BlockSpec((tm, d), lambda i: (i, 0))
    return pl.pallas_call(
        _with_deps(body, 4, deps), name=name, grid=(s // tm,), out_shape=_sds((s, d), BF16),
        in_specs=[row, vec, vec, vec] + [_ANY] * len(deps), out_specs=row, compiler_params=_params("parallel"),
    )(x, norm_w, scale, shift, *deps)


def _norm_backward(name, dh, x, norm_w, scale, dres, gated=None, deps=()):
    s, d = x.shape
    tm = _tile(s, 256)
    n_in = 7 if gated else 5

    def body(*refs):
        dh_ref, x_ref, w_ref, sc_ref, dres_ref = refs[:5]
        dx_ref, dshift_ref, dscale_ref, dw_ref = refs[n_in:n_in + 4]
        sums = (dshift_ref, dscale_ref, dw_ref) + ((refs[n_in + 5],) if gated else ())

        @pl.when(pl.program_id(0) == 0)
        def _():
            for ref in sums:
                ref[...] = jnp.zeros_like(ref)

        xv = x_ref[...]
        g = dh_ref[...]
        r = lax.rsqrt(jnp.mean(xv * xv, axis=-1, keepdims=True) + NORM_EPS)
        n = xv * r
        gain = 1.0 + sc_ref[...]
        gn = g * n
        dshift_ref[...] += jnp.sum(g, axis=0, keepdims=True)
        dscale_ref[...] += jnp.sum(gn, axis=0, keepdims=True) * w_ref[...]
        dw_ref[...] += jnp.sum(gn, axis=0, keepdims=True) * gain
        dn = g * (w_ref[...] * gain)
        dx = dres_ref[...] + r * (dn - n * jnp.mean(dn * n, axis=-1, keepdims=True))
        dx_ref[...] = dx
        if gated:
            gate_ref, other_ref = refs[5:7]
            refs[n_in + 4][...] = (dx * gate_ref[...]).astype(BF16)
            refs[n_in + 5][...] += jnp.sum(dx * other_ref[...].astype(F32), axis=0, keepdims=True)

    vec = pl.BlockSpec((1, d), lambda i: (0, 0))
    row = pl.BlockSpec((tm, d), lambda i: (i, 0))
    vec_out = _sds((1, d), F32)
    return pl.pallas_call(
        _with_deps(body, n_in, deps), name=name, grid=(s // tm,),
        out_shape=(_sds((s, d), F32), vec_out, vec_out, vec_out) + ((_sds((s, d), BF16), vec_out) if gated else ()),
        in_specs=[row, row, vec, vec, row] + ([vec, row] if gated else []) + [_ANY] * len(deps),
        out_specs=(row, vec, vec, vec) + ((row, vec) if gated else ()),
        compiler_params=_params("arbitrary"),
    )(dh, x, norm_w, scale, dres, *(gated or ()), *deps)


def _split_bf16(v):
    hi = v.astype(BF16)
    lo = (v - hi.astype(F32)).astype(BF16)
    return hi, lo


def _pool_forward(proj, w_pool, pool_scale, deps=()):
    s = proj.shape[0]
    g_n, cg, _ = w_pool.shape
    t = POOL_TILE
    nt = s // t

    def body(cur_ref, prev_ref, wp_ref, sc_ref, pooled_ref, ya_ref):
        g = pl.program_id(0)
        ti = pl.program_id(1)
        win = jnp.left_shift(2, g)
        row = lax.broadcasted_iota(jnp.int32, (t, t), 0)
        col = lax.broadcasted_iota(jnp.int32, (t, t), 1)
        lag = row - col
        band_cur = ((lag >= 0) & (lag < win)).astype(BF16)
        band_prev = ((lag + t < win) & (ti > 0)).astype(BF16)
        u = cur_ref[...]
        u_hi, u_lo = _split_bf16(u)
        p_hi, p_lo = _split_bf16(prev_ref[...])
        wsum = (_dot(band_cur, u_hi) + _dot(band_cur, u_lo)
                + _dot(band_prev, p_hi) + _dot(band_prev, p_lo))
        tok = ti * t + lax.broadcasted_iota(jnp.int32, (t, 1), 0)
        count = jnp.minimum(tok + 1, win).astype(F32)
        pooled = (wsum / count - u).astype(BF16)
        pooled_ref[...] = pooled
        ya_ref[...] = (_dot(pooled, wp_ref[...]) * sc_ref[...]).astype(BF16)

    blk = pl.BlockSpec((t, cg), lambda g, i: (i, g))
    return pl.pallas_call(
        _with_deps(body, 4, deps), name="pool_fwd", grid=(g_n, nt),
        out_shape=(_sds((s, g_n * cg), BF16), _sds((s, g_n * cg), BF16)),
        in_specs=[blk, pl.BlockSpec((t, cg), lambda g, i: (jnp.maximum(i - 1, 0), g)),
                  pl.BlockSpec((None, cg, cg), lambda g, i: (g, 0, 0)),
                  pl.BlockSpec((1, cg), lambda g, i: (0, g))] + [_ANY] * len(deps),
        out_specs=(blk, blk), compiler_params=_params("parallel", "parallel"),
    )(proj, proj, w_pool, pool_scale, *deps)


def _pool_backward(dya, pooled, w_pool, pool_scale, dproj):
    s = dya.shape[0]
    g_n, cg, _ = w_pool.shape
    t = POOL_TILE
    nt = s // t

    def body(dya_ref, dya_next_ref, pooled_ref, wp_ref, sc_ref, dproj_in, du_ref, gw_ref, gs_ref):
        del dproj_in
        g = pl.program_id(0)
        ti = pl.program_id(1)

        @pl.when(ti == 0)
        def _():
            gw_ref[...] = jnp.zeros_like(gw_ref)
            gs_ref[...] = jnp.zeros_like(gs_ref)

        win = jnp.left_shift(2, g)
        wp = wp_ref[...]
        sc = sc_ref[...]
        pooled_v = pooled_ref[...]
        dya_v = dya_ref[...].astype(F32)
        mixed = _dot(pooled_v, wp)
        gs_ref[...] += jnp.sum(dya_v * mixed, axis=0, keepdims=True)
        dmixed = (dya_v * sc).astype(BF16)
        gw_ref[...] += _dot(pooled_v, dmixed, "tn")
        dpooled = _dot(dmixed, wp, "nt")
        dmixed_next = (dya_next_ref[...].astype(F32) * sc).astype(BF16)
        dpooled_next = _dot(dmixed_next, wp, "nt")
        tok = ti * t + lax.broadcasted_iota(jnp.int32, (t, 1), 0)
        e_cur = dpooled / jnp.minimum(tok + 1, win).astype(F32)
        e_next = dpooled_next / jnp.minimum(tok + t + 1, win).astype(F32)
        row = lax.broadcasted_iota(jnp.int32, (t, t), 0)
        col = lax.broadcasted_iota(jnp.int32, (t, t), 1)
        lead = col - row
        band_cur = ((lead >= 0) & (lead < win)).astype(BF16)
        band_next = ((lead + t < win) & (ti < nt - 1)).astype(BF16)
        c_hi, c_lo = _split_bf16(e_cur)
        n_hi, n_lo = _split_bf16(e_next)
        du = (_dot(band_cur, c_hi) + _dot(band_cur, c_lo)
              + _dot(band_next, n_hi) + _dot(band_next, n_lo)) - dpooled
        du_ref[...] = du.astype(BF16)

    blk = pl.BlockSpec((t, cg), lambda g, i: (i, g))
    du, gw, gs = pl.pallas_call(
        body, name="pool_bwd", grid=(g_n, nt),
        out_shape=(_sds(dproj.shape, BF16), _sds((g_n, cg, cg), F32), _sds((1, g_n * cg), F32)),
        in_specs=[blk, pl.BlockSpec((t, cg), lambda g, i: (jnp.minimum(i + 1, nt - 1), g)), blk,
                  pl.BlockSpec((None, cg, cg), lambda g, i: (g, 0, 0)),
                  pl.BlockSpec((1, cg), lambda g, i: (0, g)),
                  pl.BlockSpec(memory_space=pl.ANY)],
        out_specs=(blk, pl.BlockSpec((None, cg, cg), lambda g, i: (g, 0, 0)),
                   pl.BlockSpec((1, cg), lambda g, i: (0, g))),
        input_output_aliases={5: 0}, compiler_params=_params("parallel", "arbitrary"),
    )(dya, dya, pooled, w_pool, pool_scale, dproj)
    return du, gw, gs


def _qkv_prepare(proj, q_norm_w, k_norm_w, width, deps=()):
    s = proj.shape[0]
    tm = _tile(s, 256)
    heads = width // HEAD_DIM

    def body(q_ref, k_ref, v_ref, qw_ref, kw_ref, qn_ref, kn_ref, vb_ref):
        for h in range(heads):
            cols = slice(h * HEAD_DIM, (h + 1) * HEAD_DIM)
            for src, w_ref, dst in ((q_ref, qw_ref, qn_ref), (k_ref, kw_ref, kn_ref)):
                v = src[:, cols]
                r = lax.rsqrt(jnp.mean(v * v, axis=-1, keepdims=True) + NORM_EPS)
                dst[:, cols] = (v * r * w_ref[...]).astype(BF16)
        vb_ref[...] = v_ref[...].astype(BF16)

    vec = pl.BlockSpec((1, HEAD_DIM), lambda i: (0, 0))
    out_spec = pl.BlockSpec((tm, width), lambda i: (i, 0))
    return pl.pallas_call(
        _with_deps(body, 5, deps), name="qkv_prep", grid=(s // tm,),
        out_shape=(_sds((s, width), BF16),) * 3,
        in_specs=[pl.BlockSpec((tm, width), lambda i: (i, 1)), pl.BlockSpec((tm, width), lambda i: (i, 2)),
                  pl.BlockSpec((tm, width), lambda i: (i, 3)), vec, vec] + [_ANY] * len(deps),
        out_specs=(out_spec,) * 3, compiler_params=_params("parallel"),
    )(proj, proj, proj, q_norm_w, k_norm_w, *deps)


def _qk_norm_backward(name, dn, proj, col_block, norm_w, dproj, width, deps=()):
    s = proj.shape[0]
    tm = _tile(s, 256)
    heads = width // HEAD_DIM

    def body(dn_ref, q_ref, w_ref, dproj_in, dq_ref, gw_ref):
        del dproj_in

        @pl.when(pl.program_id(0) == 0)
        def _():
            gw_ref[...] = jnp.zeros_like(gw_ref)

        wv = w_ref[...]
        gw = jnp.zeros((1, HEAD_DIM), F32)
        for h in range(heads):
            cols = slice(h * HEAD_DIM, (h + 1) * HEAD_DIM)
            v = q_ref[:, cols]
            g = dn_ref[:, cols]
            r = lax.rsqrt(jnp.mean(v * v, axis=-1, keepdims=True) + NORM_EPS)
            n = v * r
            gw = gw + jnp.sum(g * n, axis=0, keepdims=True)
            gn = g * wv
            dq_ref[:, cols] = (r * (gn - n * jnp.mean(gn * n, axis=-1, keepdims=True))).astype(BF16)
        gw_ref[...] += gw

    blk = pl.BlockSpec((tm, width), lambda i: (i, col_block))
    return pl.pallas_call(
        _with_deps(body, 4, deps), name=name, grid=(s // tm,),
        out_shape=(_sds(dproj.shape, BF16), _sds((1, HEAD_DIM), F32)),
        in_specs=[pl.BlockSpec((tm, width), lambda i: (i, 0)), blk,
                  pl.BlockSpec((1, HEAD_DIM), lambda i: (0, 0)), pl.BlockSpec(memory_space=pl.ANY)]
        + [_ANY] * len(deps),
        out_specs=(blk, pl.BlockSpec((1, HEAD_DIM), lambda i: (0, 0))),
        input_output_aliases={3: 0}, compiler_params=_params("arbitrary"),
    )(dn, proj, norm_w, dproj, *deps)


def _strict_upper(n):
    row = lax.broadcasted_iota(jnp.int32, (n, n), 0)
    col = lax.broadcasted_iota(jnp.int32, (n, n), 1)
    return (row > col).astype(BF16)


def _strict_lower(n):
    row = lax.broadcasted_iota(jnp.int32, (n, n), 0)
    col = lax.broadcasted_iota(jnp.int32, (n, n), 1)
    return (row < col).astype(BF16)


def _cumulate(v, tri):
    hi, lo = _split_bf16(v)
    return _dot(hi, tri) + _dot(lo, tri)


def _log_sigmoid(z):
    return jnp.minimum(z, 0.0) - jnp.log(1.0 + jnp.exp(-jnp.abs(z)))


def _attention_forward(qn, kn, vb):
    s, width = qn.shape
    heads = width // HEAD_DIM
    tq, tk = Q_TILE, K_TILE
    hp = HEADS_PER_STEP
    assert tq == tk and s % tq == 0 and heads % hp == 0
    scale = 1.0 / math.sqrt(HEAD_DIM)

    def body(q_ref, k_ref, v_ref, o_ref, a_scr):
        qi = pl.program_id(1)
        upper = _strict_upper(tk)
        causal = lax.broadcasted_iota(jnp.int32, (tq, tk), 1) < lax.broadcasted_iota(jnp.int32, (tq, tk), 0)
        head_cols = [slice(u * HEAD_DIM, (u + 1) * HEAD_DIM) for u in range(hp)]

        def weights(kb, carry, masked):
            rows = pl.ds(pl.multiple_of(kb * tk, tk), tk)
            out = []
            for u, cols in enumerate(head_cols):
                later = carry[u]
                z = _dot(q_ref[:, cols], k_ref[rows, cols], "nt") * scale
                log_beta = _log_sigmoid(z)
                l = log_beta - z
                if masked:
                    l = jnp.where(causal, l, 0.0)
                a = jnp.exp(log_beta + _cumulate(l, upper) + later)
                if masked:
                    a = jnp.where(causal, a, 0.0)
                a_scr[u, :, rows] = a.astype(BF16)
                out.append(later + jnp.sum(l, axis=1, keepdims=True))
            return tuple(out)

        later = weights(qi, tuple(jnp.zeros((tq, 1), F32) for _ in range(hp)), True)
        lax.fori_loop(0, qi, lambda i, c: weights(qi - 1 - i, c, False), later)

        def mix(kb, accs):
            rows = pl.ds(pl.multiple_of(kb * tk, tk), tk)
            return tuple(acc + _dot(a_scr[u, :, rows], v_ref[rows, cols])
                         for u, (acc, cols) in enumerate(zip(accs, head_cols)))

        accs = lax.fori_loop(0, qi + 1, mix, tuple(jnp.zeros((tq, HEAD_DIM), F32) for _ in range(hp)))
        for acc, cols in zip(accs, head_cols):
            o_ref[:, cols] = acc.astype(BF16)

    full = pl.BlockSpec((s, hp * HEAD_DIM), lambda h, i: (0, h))
    blk = pl.BlockSpec((tq, hp * HEAD_DIM), lambda h, i: (i, h))
    return pl.pallas_call(
        body, name="attn_fwd", grid=(heads // hp, s // tq), out_shape=_sds((s, width), BF16),
        in_specs=[blk, full, full], out_specs=blk, scratch_shapes=[pltpu.VMEM((hp, tq, s), BF16)],
        compiler_params=_params("parallel", "parallel"),
    )(qn, kn, vb)


def _attention_backward(qn, kn, vb, dout, dproj, v_col_block, deps=()):
    s, width = qn.shape
    heads = width // HEAD_DIM
    tq, tk = Q_TILE, K_TILE
    hp = HEADS_PER_STEP_BWD
    nq = s // tq
    scale = 1.0 / math.sqrt(HEAD_DIM)
    v_block0 = v_col_block * (heads // hp)

    def body(q_ref, k_ref, v_ref, do_ref, dproj_in, dq_ref, dk_ref, dv_ref, a_scr, lb_scr, dkt_scr, dvt_scr):
        del dproj_in
        qi = pl.program_id(1)

        @pl.when(qi == 0)
        def _():
            dkt_scr[...] = jnp.zeros_like(dkt_scr)
            dvt_scr[...] = jnp.zeros_like(dvt_scr)

        upper = _strict_upper(tk)
        lower = _strict_lower(tk)
        causal = lax.broadcasted_iota(jnp.int32, (tq, tk), 1) < lax.broadcasted_iota(jnp.int32, (tq, tk), 0)
        head_cols = [slice(u * HEAD_DIM, (u + 1) * HEAD_DIM) for u in range(hp)]

        def weights(kb, carry, masked):
            rows = pl.ds(pl.multiple_of(kb * tk, tk), tk)
            out = []
            for u, cols in enumerate(head_cols):
                later = carry[u]
                z = _dot(q_ref[:, cols], k_ref[rows, cols], "nt") * scale
                log_beta = _log_sigmoid(z)
                l = log_beta - z
                if masked:
                    l = jnp.where(causal, l, 0.0)
                a = jnp.exp(log_beta + _cumulate(l, upper) + later)
                if masked:
                    a = jnp.where(causal, a, 0.0)
                a_scr[u, :, rows] = a
                lb_scr[u, :, rows] = log_beta
                out.append(later + jnp.sum(l, axis=1, keepdims=True))
            return tuple(out)

        zeros = tuple(jnp.zeros((tq, 1), F32) for _ in range(hp))
        later = weights(qi, zeros, True)
        lax.fori_loop(0, qi, lambda i, c: weights(qi - 1 - i, c, False), later)

        q_t = [jnp.transpose(q_ref[:, cols].astype(F32)).astype(BF16) for cols in head_cols]
        do_t = [jnp.transpose(do_ref[:, cols].astype(F32)).astype(BF16) for cols in head_cols]

        def grads(kb, carry, masked):
            rows = pl.ds(pl.multiple_of(kb * tk, tk), tk)
            out = []
            for u, cols in enumerate(head_cols):
                dq, before = carry[u]
                k_blk = k_ref[rows, cols]
                a = a_scr[u, :, rows]
                beta = jnp.exp(lb_scr[u, :, rows])
                g = a * _dot(do_ref[:, cols], v_ref[rows, cols], "nt")
                p = _cumulate(g, lower) + before
                dz = g - (g + p) * beta
                if masked:
                    dz = jnp.where(causal, dz, 0.0)
                dz = (dz * scale).astype(BF16)
                dkt_scr[cols, rows] += _dot(q_t[u], dz)
                dvt_scr[cols, rows] += _dot(do_t[u], a.astype(BF16))
                out.append((dq + _dot(dz, k_blk), before + jnp.sum(g, axis=1, keepdims=True)))
            return tuple(out)

        carry = tuple((jnp.zeros((tq, HEAD_DIM), F32), jnp.zeros((tq, 1), F32)) for _ in range(hp))
        carry = lax.fori_loop(0, qi, lambda i, c: grads(i, c, False), carry)
        carry = grads(qi, carry, True)
        for u, cols in enumerate(head_cols):
            dq_ref[:, cols] = carry[u][0]

        @pl.when(qi == nq - 1)
        def _():
            dk_ref[...] = jnp.transpose(dkt_scr[...])
            dv_ref[...] = jnp.transpose(dvt_scr[...]).astype(BF16)

    wide = hp * HEAD_DIM
    full = pl.BlockSpec((s, wide), lambda h, i: (0, h))
    blk = pl.BlockSpec((tq, wide), lambda h, i: (i, h))
    return pl.pallas_call(
        _with_deps(body, 5, deps), name="attn_bwd", grid=(heads // hp, nq),
        out_shape=(_sds((s, width), F32), _sds((s, width), F32), _sds(dproj.shape, BF16)),
        in_specs=[blk, full, full, blk, pl.BlockSpec(memory_space=pl.ANY)] + [_ANY] * len(deps),
        out_specs=(blk, full, pl.BlockSpec((s, wide), lambda h, i: (0, v_block0 + h))),
        scratch_shapes=[pltpu.VMEM((hp, tq, s), F32), pltpu.VMEM((hp, tq, s), F32),
                        pltpu.VMEM((wide, s), F32), pltpu.VMEM((wide, s), F32)],
        input_output_aliases={4: 2}, compiler_params=_params("parallel", "arbitrary"),
    )(qn, kn, vb, dout, dproj, *deps)


def _place_columns(name, src, dst, col_block):
    s, w = src.shape
    tm = _tile(s, 512)

    def body(src_ref, dst_in, out_ref):
        del dst_in
        out_ref[...] = src_ref[...]

    return pl.pallas_call(
        body, name=name, grid=(s // tm,), out_shape=_sds(dst.shape, dst.dtype),
        in_specs=[pl.BlockSpec((tm, w), lambda i: (i, 0)), pl.BlockSpec(memory_space=pl.ANY)],
        out_specs=pl.BlockSpec((tm, w), lambda i: (i, col_block)),
        input_output_aliases={1: 0}, compiler_params=_params("parallel"),
    )(src, dst)


def _cast_into_slot(name, x, slot):
    r, c = x.shape
    tr = _tile(r, max(SUBLANES * 2, (1 << 20) // c), SUBLANES * 2)

    def body(slot_ref, x_ref, o_ref):
        del slot_ref
        o_ref[...] = x_ref[...].astype(BF16)

    grid_spec = pltpu.PrefetchScalarGridSpec(
        num_scalar_prefetch=1, grid=(r // tr,),
        in_specs=[pl.BlockSpec((tr, c), lambda i, slot_ref: (i, 0))],
        out_specs=pl.BlockSpec((None, tr, c), lambda i, slot_ref: (slot_ref[0], i, 0)))
    return pl.pallas_call(
        body, name=name, grid_spec=grid_spec, out_shape=_sds((N_DEV, r, c), BF16),
        compiler_params=_params("parallel"),
    )(slot, x)


def _adamw_update(gv, w_ref, m_ref, v_ref, d_ref, nm_ref, nv_ref):
    c1 = 1.0 - ADAM_B1 ** ADAM_STEP
    c2 = 1.0 - ADAM_B2 ** ADAM_STEP
    nm = ADAM_B1 * m_ref[...] + (1.0 - ADAM_B1) * gv
    nv = ADAM_B2 * v_ref[...] + (1.0 - ADAM_B2) * (gv * gv)
    d_ref[...] = -ADAM_LR * ((nm / c1) / (jnp.sqrt(nv / c2) + ADAM_EPS) + ADAM_WD * w_ref[...])
    nm_ref[...] = nm
    nv_ref[...] = nv


def _adamw(name, w, g, m, v):
    r, c = w.shape
    tr = _tile(r, max(SUBLANES, (1 << 19) // c))

    def body(w_ref, g_ref, m_ref, v_ref, d_ref, nm_ref, nv_ref):
        _adamw_update(g_ref[...], w_ref, m_ref, v_ref, d_ref, nm_ref, nv_ref)

    blk = pl.BlockSpec((tr, c), lambda i: (i, 0))
    return pl.pallas_call(
        body, name=name, grid=(r // tr,), out_shape=(_sds((r, c), F32),) * 3,
        in_specs=[blk] * 4, out_specs=(blk,) * 3, compiler_params=_params("parallel"),
    )(w, g, m, v)


def _adamw_summed(name, w, own, received, m, v):
    r, c = w.shape
    nj = received.shape[0]
    tr = _tile(r, max(2 * SUBLANES, (1 << 19) // c), 2 * SUBLANES)

    def body(w_ref, own_ref, rec_ref, m_ref, v_ref, g_ref, d_ref, nm_ref, nv_ref):
        gv = own_ref[...].astype(F32)
        for j in range(nj):
            gv = gv + rec_ref[j].astype(F32)
        g_ref[...] = gv
        _adamw_update(gv, w_ref, m_ref, v_ref, d_ref, nm_ref, nv_ref)

    blk = pl.BlockSpec((tr, c), lambda i: (i, 0))
    return pl.pallas_call(
        body, name=name, grid=(r // tr,), out_shape=(_sds((r, c), F32),) * 4,
        in_specs=[blk, pl.BlockSpec((None, tr, c), lambda i: (0, i, 0)),
                  pl.BlockSpec((nj, tr, c), lambda i: (0, i, 0)), blk, blk],
        out_specs=(blk,) * 4, compiler_params=_params("parallel"),
    )(w, own, received, m, v)


def _rows_of_lanes(v):
    rows = v.shape[1] // LANES
    out = v.reshape(rows, LANES)
    pad = (-rows) % SUBLANES
    if pad:
        out = jnp.pad(out, ((0, pad), (0, 0)))
    return out


def kernel(x, c, w_ada, b_ada, norm1_w, w_in, q_norm_w, k_norm_w, w_pool, pool_scale, w_a_up, w_b_up, w_o, norm2_w, w_ff1, w_ff2, loss_target, m_w_ada, m_b_ada, m_norm1_w, m_w_in, m_q_norm_w, m_k_norm_w, m_w_pool, m_pool_scale, m_w_a_up, m_w_b_up, m_w_o, m_norm2_w, m_w_ff1, m_w_ff2, v_w_ada, v_b_ada, v_norm1_w, v_w_in, v_q_norm_w, v_k_norm_w, v_w_pool, v_pool_scale, v_w_a_up, v_w_b_up, v_w_o, v_norm2_w, v_w_ff1, v_w_ff2):
    _, s, d = x.shape
    half = d // 2
    d8 = d // N_DEV
    n_groups = len(POOL_WINDOWS)
    cg = half // n_groups
    me = _group_index(MESH_AXES)

    x2 = x[0]
    target = loss_target[0]

    my_slot = jnp.reshape(me, (1,)).astype(jnp.int32)

    def cast(i, t):
        return _cast_into_slot("cast_w%d" % i, t, my_slot)

    def gather_start(tag, bufs, phase):
        if phase < 2:
            return _launch("ag%s_ici%d" % (tag, phase), bufs, _plan_gather_ici(phase), (2, 4)[phase] * len(bufs))
        return _launch("ag%s_d2d" % tag, bufs, _plan_gather_d2d, len(_CHIP_MASKS) * len(bufs))

    def gather_plans(tag, n_bufs, phase):
        if phase < 2:
            return ("ag%s_ici%d" % (tag, phase), _plan_gather_ici(phase), (2, 4)[phase] * n_bufs)
        return ("ag%s_d2d" % tag, _plan_gather_d2d, len(_CHIP_MASKS) * n_bufs)

    bx, by, bc = _AXIS_BIT["x"], _AXIS_BIT["y"], _AXIS_BIT["c"]
    buf_a = [cast(0, w_in[0])]

    c_all = _all_gather_2d("ag_c", c.reshape(d // LANES, LANES), deps=tuple(buf_a)).reshape(N_DEV, d)
    wa = w_ada.shape[2]
    b_shard = lax.dynamic_slice_in_dim(b_ada, me * wa, wa, axis=1)
    mod_part = _ada_forward(c_all, w_ada[0], b_shard)
    mod_all = _all_gather_2d("ag_mod", mod_part.reshape(N_DEV * wa // LANES, LANES))
    mod_all = mod_all.reshape(N_DEV, N_DEV, wa)
    mod = lax.dynamic_slice_in_dim(mod_all, me, 1, axis=1).reshape(1, N_MOD * d)
    shift1, scale1, gate1, shift2, scale2, gate2 = [mod[:, i * d:(i + 1) * d] for i in range(N_MOD)]
    fl_a, tok = _launch("agA_near", buf_a, _plan_neighbours, 2, deps=(mod_all,))
    buf_b = [cast(1, w_pool[0].reshape(-1, cg)), cast(2, w_a_up[0]), cast(3, w_b_up[0]), cast(4, w_o[0])]
    buf_c = [cast(5, w_ff1[0])]
    buf_e = [cast(6, w_ff2[0])]

    h = _norm_forward("norm1_fwd", x2, norm1_w, scale1, shift1, deps=(tok,))
    buf_a = _land(fl_a, [h] + buf_b + buf_c + buf_e)
    near = (0, bx, by)
    (fl_far, fl_near), tok = _launch_groups(
        "agA_far", [(buf_a, [("agA_far", _plan_diagonal, 2), ("agA_near_d2d", _plan_d2d(near), len(near))])])
    (fl_b, fl_c, fl_e), tok = _launch_groups(
        "agBCE_ici0", [(buf_b, [gather_plans("B", len(buf_b), 0)]), (buf_c, [gather_plans("C", 1, 0)]),
                       (buf_e, [gather_plans("E", 1, 0)])])

    tm = _tile(s, 1024)
    tk = _tile(d, 2048)
    te = _tile(d, 512)

    own_slots = jnp.stack([me ^ m for m in near]).astype(jnp.int32)
    far_slots = jnp.stack([me ^ bx ^ by ^ f for f in (0, bc)]).astype(jnp.int32)
    buf_a = fl_near[-1]
    proj = _project_slots("proj_own", h, buf_a[0], own_slots, 4 * d, deps=(tok,))
    buf_a = _land(fl_near, proj, bufs=buf_a)
    proj = _project_slots("proj_sibling", h, buf_a[0], own_slots ^ bc, 4 * d, proj_in=proj)
    buf_a = _land(fl_far, proj, bufs=buf_a)
    fl_a, tok = _launch("agA_far_d2d", buf_a, _plan_d2d((bx | by,)), 1)
    w_in_f, = _land(fl_a, tok)
    proj = _project_slots("proj_far", h, w_in_f, far_slots, 4 * d, proj_in=proj)
    buf_b = _land(fl_b, proj)
    buf_c = _land(fl_c, proj)
    (fl_b, fl_c), tok = _launch_groups(
        "agBC_ici1", [(buf_b, [gather_plans("B", len(buf_b), 1)]), (buf_c, [gather_plans("C", 1, 1)])])

    qn, kn, vb = _qkv_prepare(proj, q_norm_w, k_norm_w, half, deps=(tok,))
    attn = _attention_forward(qn, kn, vb)
    buf_b = _land(fl_b, attn)
    buf_e = _land(fl_e, attn)
    (fl_b, fl_e), tok = _launch_groups(
        "agB_d2d_E_ici1", [(buf_b, [gather_plans("B", len(buf_b), 2)]), (buf_e, [gather_plans("E", 1, 1)])])
    w_pool_f, w_a_f, w_b_f, w_o_f = _land(fl_b, tok)
    rows_pool = cg // N_DEV
    w_pool_f = w_pool_f.reshape(N_DEV, n_groups, rows_pool, cg).transpose(1, 0, 2, 3).reshape(n_groups, cg, cg)
    w_o_f = w_o_f.reshape(d, d)
    pooled, ya_in = _pool_forward(proj, w_pool_f, pool_scale)

    def merge_epilogue(ga_ref, gb_ref, ya, yb, out_refs):
        merged_ref, ya_ref, yb_ref = out_refs
        merged = jax.nn.sigmoid(ga_ref[...]) * ya + jax.nn.sigmoid(gb_ref[...]) * yb
        merged_ref[...] = merged.astype(BF16)
        ya_ref[...] = ya.astype(BF16)
        yb_ref[...] = yb.astype(BF16)

    def up_body(a1_ref, b1_ref, a2_ref, b2_ref, ga_ref, gb_ref, *out_refs):
        merge_epilogue(ga_ref, gb_ref, _dot(a1_ref[...], b1_ref[...]), _dot(a2_ref[...], b2_ref[...]), out_refs)

    ga_blk0 = 2 * d // d8
    gb_blk0 = 3 * d // d8
    a_spec = pl.BlockSpec((tm, half), lambda i, j: (i, 0))
    wup_spec = pl.BlockSpec((None, half, d8), lambda i, j: (j, 0, 0))
    o_blk = pl.BlockSpec((tm, d8), lambda i, j: (i, j))
    merged, y_a, y_b = pl.pallas_call(
        up_body, name="up_merge", grid=(s // tm, N_DEV), out_shape=(_sds((s, d), BF16),) * 3,
        in_specs=[a_spec, wup_spec, a_spec, wup_spec,
                  pl.BlockSpec((tm, d8), lambda i, j: (i, ga_blk0 + j)),
                  pl.BlockSpec((tm, d8), lambda i, j: (i, gb_blk0 + j))],
        out_specs=(o_blk,) * 3, compiler_params=_params("parallel", "parallel"),
    )(ya_in, w_a_f, attn, w_b_f, proj, proj)
    buf_c = _land(fl_c, merged)
    fl_c, tok_c = gather_start("C", buf_c, 2)

    tn = _tile(d, 1024)

    def oproj_epilogue(acc, extra_refs, out_refs):
        x_ref, g_ref = extra_refs
        x1_ref, o_ref = out_refs
        x1_ref[...] = x_ref[...] + g_ref[...] * acc
        o_ref[...] = acc.astype(BF16)

    mn_blk = pl.BlockSpec((tm, tn), lambda i, j, k: (i, j))
    e_blk = pl.BlockSpec((tm, te), lambda i, j, k: (i, j))
    e_vec = pl.BlockSpec((1, te), lambda i, j, k: (0, j))
    x1, o_act = _matmul(
        "oproj", "nn", (s // tm, d // te, d // tk), merged, pl.BlockSpec((tm, tk), lambda i, j, k: (i, k)),
        w_o_f, pl.BlockSpec((tk, te), lambda i, j, k: (k, j)),
        [_sds((s, d), F32), _sds((s, d), BF16)], [e_blk, e_blk], (tm, te),
        epilogue=oproj_epilogue, extras=(x2, gate1), extra_specs=(e_blk, e_vec), deps=(tok_c,))

    h2 = _norm_forward("norm2_fwd", x1, norm2_w, scale2, shift2)
    buf_e = _land(fl_e, h2)
    fl_e, tok_e = gather_start("E", buf_e, 2)
    w_ff1_f, = _land(fl_c, [h2, tok_e])

    def ff1_epilogue(acc, extra_refs, out_refs):
        r = jnp.maximum(acc, 0.0)
        out_refs[0][...] = r.astype(BF16)
        out_refs[1][...] = (r * r).astype(BF16)

    ff_blk = pl.BlockSpec((tm, half), lambda i, j, k: (i, j))
    relu, act = _matmul(
        "ff1", "nn", (s // tm, N_DEV, d // tk), h2, pl.BlockSpec((tm, tk), lambda i, j, k: (i, k)),
        w_ff1_f, pl.BlockSpec((None, tk, half), lambda i, j, k: (j, k, 0)),
        [_sds((s, 4 * d), BF16)] * 2, [ff_blk, ff_blk], (tm, half), epilogue=ff1_epilogue)
    w_ff2_f, = _land(fl_e, act)
    w_ff2_f = w_ff2_f.reshape(4 * d, d)

    def ff2_epilogue(acc, extra_refs, out_refs):
        x1_ref, g_ref, t_ref = extra_refs
        df_ref, dy_ref, sq_ref, dgate_ref = out_refs
        gate = g_ref[...]
        err = x1_ref[...] + gate * acc - t_ref[...]
        dyv = err * (1.0 / d)
        dy_ref[...] = dyv
        df_ref[...] = (dyv * gate).astype(BF16)
        sq_ref[...] = jnp.full(sq_ref.shape, jnp.sum(err * err), F32)
        dgate_ref[...] = jnp.broadcast_to(jnp.sum(dyv * acc, axis=0, keepdims=True), dgate_ref.shape)

    df, dy, sq, dgate2_parts = _matmul(
        "ff2", "nn", (s // tm, d // te, 2 * d // tk), act, pl.BlockSpec((tm, 2 * tk), lambda i, j, k: (i, k)),
        w_ff2_f, pl.BlockSpec((2 * tk, te), lambda i, j, k: (k, j)),
        [_sds((s, d), BF16), _sds((s, d), F32), _sds((s // tm * SUBLANES, d // te * LANES), F32),
         _sds((s // tm * SUBLANES, d), F32)],
        [e_blk, e_blk, pl.BlockSpec((SUBLANES, LANES), lambda i, j, k: (i, j)),
         pl.BlockSpec((SUBLANES, te), lambda i, j, k: (i, j))], (tm, te),
        epilogue=ff2_epilogue, extras=(x1, gate2, target), extra_specs=(e_blk, e_vec, e_blk))
    loss_local = (0.5 / d) * jnp.sum(sq[::SUBLANES, ::LANES])
    dgate2 = jnp.sum(dgate2_parts[::SUBLANES], axis=0, keepdims=True)

    tok_k = _tile(s, 2048)
    tw = _tile(d, 1024)
    g_ff2 = _matmul(
        "g_ff2", "tn", (4 * d // tw, d // tn, s // tok_k), act, pl.BlockSpec((tok_k, tw), lambda i, j, k: (k, i)),
        df, pl.BlockSpec((tok_k, tn), lambda i, j, k: (k, j)),
        [_sds((4 * d, d), BF16)], [pl.BlockSpec((tw, tn), lambda i, j, k: (i, j))], (tw, tn))[0]

    def da_epilogue(acc, extra_refs, out_refs):
        out_refs[0][...] = (acc * (2.0 * extra_refs[0][...].astype(F32))).astype(BF16)

    big_blk = pl.BlockSpec((tm, tn), lambda i, j, k: (i, j))
    fl_f2, tok = _reduce_scatter_start("F2", [g_ff2.reshape(N_DEV, half, d)])
    df1 = _matmul(
        "da_ff", "nt", (s // tm, 4 * d // tn, d // tk), df, pl.BlockSpec((tm, tk), lambda i, j, k: (i, k)),
        w_ff2_f, pl.BlockSpec((tn, tk), lambda i, j, k: (j, k)),
        [_sds((s, 4 * d), BF16)], [big_blk], (tm, tn),
        epilogue=da_epilogue, extras=(relu,), extra_specs=(big_blk,), deps=(tok,))[0]

    fl_f2, tok = _reduce_scatter_middle("F2", fl_f2, df1, me)
    g_ff1 = _matmul(
        "g_ff1", "tn", (d // tw, N_DEV, s // tok_k), h2, pl.BlockSpec((tok_k, tw), lambda i, j, k: (k, i)),
        df1, pl.BlockSpec((tok_k, half), lambda i, j, k: (k, j)),
        [_sds((N_DEV, d, half), BF16)], [pl.BlockSpec((None, tw, half), lambda i, j, k: (j, i, 0))], (tw, half),
        deps=(tok,))[0]

    fl_f1, tok = _reduce_scatter_start("F1", [g_ff1])
    dh2 = _matmul(
        "dh2", "nt", (s // tm, d // tn, N_DEV // 2), df1, pl.BlockSpec((tm, 2 * half), lambda i, j, k: (i, k)),
        w_ff1_f, pl.BlockSpec((2, tn, half), lambda i, j, k: (k, j, 0)),
        [_sds((s, d), F32)], [mn_blk], (tm, tn), deps=(tok,))[0]

    sum_ff2, = _reduce_scatter_finish(fl_f2, dh2)
    fl_f1, tok = _reduce_scatter_middle("F1", fl_f1, dh2, me)
    dx1, dshift2, dscale2, g_norm2, do, dgate1 = _norm_backward(
        "norm2_bwd", dh2, x1, norm2_w, scale2, dy, gated=(gate1, o_act), deps=(tok,))

    g_o = _matmul(
        "g_o", "tn", (d // tw, d // tn, s // tok_k), merged, pl.BlockSpec((tok_k, tw), lambda i, j, k: (k, i)),
        do, pl.BlockSpec((tok_k, tn), lambda i, j, k: (k, j)),
        [_sds((d, d), BF16)], [pl.BlockSpec((tw, tn), lambda i, j, k: (i, j))], (tw, tn))[0]

    def merge_bwd_epilogue(acc, extra_refs, out_refs):
        ga_ref, gb_ref, ya_ref, yb_ref = extra_refs
        dya_ref, dyb_ref, dga_ref, dgb_ref = out_refs
        sa = jax.nn.sigmoid(ga_ref[...])
        sb = jax.nn.sigmoid(gb_ref[...])
        dya_ref[...] = (acc * sa).astype(BF16)
        dyb_ref[...] = (acc * sb).astype(BF16)
        dga_ref[...] = (acc * ya_ref[...].astype(F32) * (sa * (1.0 - sa))).astype(BF16)
        dgb_ref[...] = (acc * yb_ref[...].astype(F32) * (sb * (1.0 - sb))).astype(BF16)

    nb = d // te
    dy_a, dy_b, dproj, dg_b = _matmul(
        "dmerged", "nt", (s // tm, nb, d // tk), do, pl.BlockSpec((tm, tk), lambda i, j, k: (i, k)),
        w_o_f, pl.BlockSpec((te, tk), lambda i, j, k: (j, k)),
        [_sds((s, d), BF16), _sds((s, d), BF16), _sds((s, 4 * d), BF16), _sds((s, d), BF16)],
        [e_blk, e_blk, pl.BlockSpec((tm, te), lambda i, j, k: (i, 2 * nb + j)), e_blk], (tm, te),
        epilogue=merge_bwd_epilogue, extras=(proj, proj, y_a, y_b),
        extra_specs=(pl.BlockSpec((tm, te), lambda i, j, k: (i, 2 * nb + j)),
                     pl.BlockSpec((tm, te), lambda i, j, k: (i, 3 * nb + j)), e_blk, e_blk))
    dproj = _place_columns("place_dgb", dg_b, dproj, 3)

    up_a = pl.BlockSpec((tok_k, half), lambda i, j, k: (k, 0))
    up_b = pl.BlockSpec((tok_k, d8), lambda i, j, k: (k, j))
    up_o = pl.BlockSpec((None, half, d8), lambda i, j, k: (j, 0, 0))
    g_a_up = _matmul("g_a_up", "tn", (1, N_DEV, s // tok_k), ya_in, up_a, dy_a, up_b,
                     [_sds((N_DEV, half, d8), BF16)], [up_o], (half, d8))[0]
    g_b_up = _matmul("g_b_up", "tn", (1, N_DEV, s // tok_k), attn, up_a, dy_b, up_b,
                     [_sds((N_DEV, half, d8), BF16)], [up_o], (half, d8))[0]
    dn_a = pl.BlockSpec((tm, d8), lambda i, j, k: (i, k))
    dn_b = pl.BlockSpec((None, half, d8), lambda i, j, k: (k, 0, 0))
    dn_o = pl.BlockSpec((tm, half), lambda i, j, k: (i, 0))
    dya_in = _matmul("d_ya_in", "nt", (s // tm, 1, N_DEV), dy_a, dn_a, w_a_f, dn_b,
                     [_sds((s, half), BF16)], [dn_o], (tm, half))[0]
    dattn = _matmul("d_attn", "nt", (s // tm, 1, N_DEV), dy_b, dn_a, w_b_f, dn_b,
                    [_sds((s, half), BF16)], [dn_o], (tm, half))[0]

    dproj, g_pool, g_pool_scale = _pool_backward(dya_in, pooled, w_pool_f, pool_scale, dproj)
    sum_ff1, = _reduce_scatter_finish(fl_f1, g_pool)
    g_pool_send = g_pool.astype(BF16).reshape(n_groups, N_DEV, rows_pool, cg).transpose(1, 0, 2, 3)
    g_pool_send = g_pool_send.reshape(N_DEV, n_groups * rows_pool, cg)
    fl_b, tok = _reduce_scatter_start("B", [g_pool_send, g_a_up, g_b_up, g_o.reshape(N_DEV, d8, d)])
    dqn, dkn, dproj = _attention_backward(qn, kn, vb, dattn, dproj, 3, deps=(tok,))
    fl_b, tok = _reduce_scatter_middle("B", fl_b, dqn, me)
    dproj, g_qnorm = _qk_norm_backward("qnorm_bwd", dqn, proj, 1, q_norm_w, dproj, half, deps=(tok,))
    dproj, g_knorm = _qk_norm_backward("knorm_bwd", dkn, proj, 2, k_norm_w, dproj, half)

    g_in = _matmul(
        "g_in", "tn", (d // tw, N_DEV, s // tok_k), h, pl.BlockSpec((tok_k, tw), lambda i, j, k: (k, i)),
        dproj, pl.BlockSpec((tok_k, half), lambda i, j, k: (k, j)),
        [_sds((N_DEV, d, half), BF16)], [pl.BlockSpec((None, tw, half), lambda i, j, k: (j, i, 0))], (tw, half))[0]
    fl_in, tok = _reduce_scatter_start("I", [g_in])
    dh = _matmul(
        "dh", "nt", (s // tm, d // tn, N_DEV // 2), dproj, pl.BlockSpec((tm, 2 * half), lambda i, j, k: (i, k)),
        w_in_f, pl.BlockSpec((2, tn, half), lambda i, j, k: (k, j, 0)),
        [_sds((s, d), F32)], [mn_blk], (tm, tn), deps=(tok,))[0]
    sum_pool, sum_a_up, sum_b_up, sum_o = _reduce_scatter_finish(fl_b, dh)
    grad_x, dshift1, dscale1, g_norm1 = _norm_backward("norm1_bwd", dh, x2, norm1_w, scale1, dx1)

    dmod = jnp.concatenate([dshift1, dscale1, dgate1, dshift2, dscale2, dgate2], axis=1)
    pieces = [dmod, g_norm1, g_norm2, g_pool_scale, g_qnorm, g_knorm, jnp.full((1, LANES), loss_local, F32)]
    packed_rows = [_rows_of_lanes(p) for p in pieces]
    offsets = [0]
    for p in packed_rows:
        offsets.append(offsets[-1] + p.shape[0])
    packed = jnp.concatenate(packed_rows, axis=0)
    small_all = _all_gather_2d("ag_small", packed)
    fl_in, tok = _reduce_scatter_middle("I", fl_in, small_all, me)
    small_sum = _sum_slots("small_sum", small_all[None], F32)[0]

    def unpack(i, width):
        return small_sum[offsets[i]:offsets[i] + width // LANES].reshape(1, width)

    g_b_ada = unpack(0, N_MOD * d)
    g_norm1_w = unpack(1, d)
    g_norm2_w = unpack(2, d)
    g_pool_scale_w = unpack(3, half)
    g_q_norm_w = unpack(4, HEAD_DIM)
    g_k_norm_w = unpack(5, HEAD_DIM)
    loss = unpack(6, LANES)[0, 0]
    dmod_all = small_all[:, :N_MOD * d // LANES].reshape(N_DEV, N_MOD * d)
    dmod_cols = lax.dynamic_slice_in_dim(dmod_all, me * wa, wa, axis=1)
    g_w_ada = _ada_weight_grad(c_all, dmod_cols, deps=(tok,))[None]


    grads = {
        "w_ada": g_w_ada, "b_ada": g_b_ada, "norm1_w": g_norm1_w,
        "q_norm_w": g_q_norm_w, "k_norm_w": g_k_norm_w,
        "pool_scale": g_pool_scale_w, "norm2_w": g_norm2_w,
    }
    sums = {"w_pool": sum_pool, "w_a_up": sum_a_up, "w_b_up": sum_b_up, "w_o": sum_o,
            "w_ff1": sum_ff1, "w_ff2": sum_ff2}
    weights = {"w_ada": (w_ada, m_w_ada, v_w_ada), "b_ada": (b_ada, m_b_ada, v_b_ada),
               "norm1_w": (norm1_w, m_norm1_w, v_norm1_w), "w_in": (w_in, m_w_in, v_w_in),
               "q_norm_w": (q_norm_w, m_q_norm_w, v_q_norm_w), "k_norm_w": (k_norm_w, m_k_norm_w, v_k_norm_w),
               "w_pool": (w_pool, m_w_pool, v_w_pool), "pool_scale": (pool_scale, m_pool_scale, v_pool_scale),
               "w_a_up": (w_a_up, m_w_a_up, v_w_a_up), "w_b_up": (w_b_up, m_w_b_up, v_w_b_up),
               "w_o": (w_o, m_w_o, v_w_o), "norm2_w": (norm2_w, m_norm2_w, v_norm2_w),
               "w_ff1": (w_ff1, m_w_ff1, v_w_ff1), "w_ff2": (w_ff2, m_w_ff2, v_w_ff2)}
    order = list(weights)
    deltas, new_m, new_v = {}, {}, {}
    def adam(name):
        wt, mt, vt = weights[name]
        shape = wt.shape
        flat = (-1, shape[-1])
        if name in sums:
            own, received = sums[name]
            g, dl, nm, nv = _adamw_summed("adamw_" + name, wt.reshape(flat), own, received,
                                          mt.reshape(flat), vt.reshape(flat))
            grads[name] = g.reshape(shape)
        else:
            dl, nm, nv = _adamw("adamw_" + name, wt.reshape(flat), grads[name].reshape(flat),
                                mt.reshape(flat), vt.reshape(flat))
        deltas[name], new_m[name], new_v[name] = dl.reshape(shape), nm.reshape(shape), nv.reshape(shape)

    others = [n for n in order if n != "w_in"]
    for name in others:
        adam(name)
    sums["w_in"], = _reduce_scatter_finish(fl_in, [deltas[n] for n in others])
    adam("w_in")

    return (loss, grad_x[None], *[grads[n] for n in order], *[deltas[n] for n in order],
            *[new_m[n] for n in order], *[new_v[n] for n in order])
```

```python
import math

import jax
import jax.numpy as jnp
from jax import lax
from jax.experimental import pallas as pl
from jax.experimental.pallas import tpu as pltpu

F32 = jnp.float32
BF16 = jnp.bfloat16
MESH_AXES = ("x", "y", "c")
N_DEV = 8
HEAD_DIM = 128
POOL_WINDOWS = (2, 4, 8, 16)
N_MOD = 6
NORM_EPS = 1e-6
LANES = 128
SUBLANES = 8
VMEM_LIMIT_BYTES = 56 * 1024 * 1024
Q_TILE = 256
K_TILE = 256
POOL_TILE = 256
HEADS_PER_STEP = 4
HEADS_PER_STEP_BWD = 4

ADAM_LR = 0.001
ADAM_B1 = 0.9
ADAM_B2 = 0.999
ADAM_EPS = 1e-08
ADAM_WD = 0.01
ADAM_STEP = 10

_NN = (((1,), (0,)), ((), ()))
_NT = (((1,), (1,)), ((), ()))
_TN = (((0,), (0,)), ((), ()))
_DIMS = {"nn": _NN, "nt": _NT, "tn": _TN}


def _dot(a, b, mode="nn"):
    return lax.dot_general(a, b, _DIMS[mode], preferred_element_type=F32)


def _params(*sem):
    return pltpu.CompilerParams(dimension_semantics=sem, vmem_limit_bytes=VMEM_LIMIT_BYTES)


def _tile(dim, pref, align=SUBLANES):
    for t in range(min(dim, pref), 0, -1):
        if dim % t == 0 and t % align == 0:
            return t
    return dim


def _group_index(axes):
    idx = 0
    for a in axes:
        idx = idx * 2 + lax.axis_index(a)
    return idx


def _peer_device(axes, k):
    coords = {a: lax.axis_index(a) for a in MESH_AXES}
    for pos, a in enumerate(axes):
        if (k >> (len(axes) - 1 - pos)) & 1:
            coords[a] = 1 - coords[a]
    return tuple(coords[a] for a in MESH_AXES)


_AXIS_BIT = {"x": 4, "y": 2, "c": 1}
_ANY = pl.BlockSpec(memory_space=pl.ANY)


def _device_xor(mask):
    return tuple(1 - lax.axis_index(a) if mask & _AXIS_BIT[a] else lax.axis_index(a) for a in MESH_AXES)


def _remote(src, dst, send_sem, recv_sem, mask):
    return pltpu.make_async_remote_copy(src_ref=src, dst_ref=dst, send_sem=send_sem, recv_sem=recv_sem,
                                        device_id=_device_xor(mask), device_id_type=pl.DeviceIdType.MESH)


def _start_all(copies):
    for cp in copies:
        cp.start()


def _wait_all(copies):
    for cp in copies:
        cp.wait()


def _gather_ici(name, bufs):
    na = len(bufs)
    bx, by = _AXIS_BIT["x"], _AXIS_BIT["y"]

    def body(*refs):
        out = refs[na:2 * na]
        send_sems, recv_sems = refs[2 * na:]
        me = _group_index(MESH_AXES)

        def copy(a, n, slot, color, mask):
            half = out[a].shape[1] // 2
            piece = out[a].at[slot, pl.ds(color * half, half)]
            return _remote(piece, piece, send_sems.at[a, n], recv_sems.at[a, n], mask)

        first, second = [], []
        for a in range(na):
            first += [copy(a, 0, me, 0, bx), copy(a, 1, me, 1, by)]
            second += [copy(a, 2, me, 0, by), copy(a, 3, me ^ bx, 0, by),
                       copy(a, 4, me, 1, bx), copy(a, 5, me ^ by, 1, bx)]
        _start_all(first)
        _wait_all(first)
        _start_all(second)
        _wait_all(second)

    return pl.pallas_call(
        body, name=name, out_shape=tuple(jax.ShapeDtypeStruct(b.shape, b.dtype) for b in bufs),
        in_specs=[_ANY] * na, out_specs=tuple([_ANY] * na),
        input_output_aliases={a: a for a in range(na)},
        scratch_shapes=[pltpu.SemaphoreType.DMA((na, 6)), pltpu.SemaphoreType.DMA((na, 6))],
    )(*bufs)


def _gather_d2d(name, bufs):
    na = len(bufs)
    masks = (0, _AXIS_BIT["y"], _AXIS_BIT["x"], _AXIS_BIT["x"] | _AXIS_BIT["y"])

    def body(*refs):
        out = refs[na:2 * na]
        send_sems, recv_sems = refs[2 * na:]
        me = _group_index(MESH_AXES)
        copies = []
        for a in range(na):
            for n, m in enumerate(masks):
                piece = out[a].at[me ^ m]
                copies.append(_remote(piece, piece, send_sems.at[a, n], recv_sems.at[a, n], _AXIS_BIT["c"]))
        _start_all(copies)
        _wait_all(copies)

    return pl.pallas_call(
        body, name=name, out_shape=tuple(jax.ShapeDtypeStruct(b.shape, b.dtype) for b in bufs),
        in_specs=[_ANY] * na, out_specs=tuple([_ANY] * na),
        input_output_aliases={a: a for a in range(na)},
        scratch_shapes=[pltpu.SemaphoreType.DMA((na, 4)), pltpu.SemaphoreType.DMA((na, 4))],
    )(*bufs)


def _scatter(name, srcs, send_slots, masks):
    na = len(srcs)
    nm = len(masks)

    def body(*refs):
        src, out = refs[:na], refs[na:2 * na]
        send_sems, recv_sems = refs[2 * na:]
        slots = send_slots(_group_index(MESH_AXES))
        copies = []
        for n, m in enumerate(masks):
            for a in range(na):
                copies.append(_remote(src[a].at[slots[n]], out[a].at[n],
                                      send_sems.at[a, n], recv_sems.at[a, n], m))
        _start_all(copies)
        _wait_all(copies)

    return pl.pallas_call(
        body, name=name,
        out_shape=tuple(jax.ShapeDtypeStruct((nm,) + s.shape[1:], s.dtype) for s in srcs),
        in_specs=[_ANY] * na, out_specs=tuple([_ANY] * na),
        scratch_shapes=[pltpu.SemaphoreType.DMA((na, nm)), pltpu.SemaphoreType.DMA((na, nm))],
    )(*srcs)


def _add_received(name, own, own_slots, received, out_dtype):
    nj, r, c = received.shape
    tr = _tile(r, max(2 * SUBLANES, (1 << 20) // c), 2 * SUBLANES)

    def body(slots_ref, own_ref, rec_ref, o_ref):
        del slots_ref
        o_ref[...] = (own_ref[...].astype(F32) + rec_ref[...].astype(F32)).astype(o_ref.dtype)

    grid_spec = pltpu.PrefetchScalarGridSpec(
        num_scalar_prefetch=1, grid=(nj, r // tr),
        in_specs=[pl.BlockSpec((None, tr, c), lambda j, i, slots: (slots[j], i, 0)),
                  pl.BlockSpec((None, tr, c), lambda j, i, slots: (j, i, 0))],
        out_specs=pl.BlockSpec((None, tr, c), lambda j, i, slots: (j, i, 0)))
    return pl.pallas_call(
        body, name=name, grid_spec=grid_spec, out_shape=jax.ShapeDtypeStruct((nj, r, c), out_dtype),
        compiler_params=_params("parallel", "parallel"),
    )(own_slots, own, received)


def _add_final(name, own, received):
    nj, r, c = received.shape
    tr = _tile(r, max(2 * SUBLANES, (1 << 19) // c), 2 * SUBLANES)

    def body(own_ref, rec_ref, o_ref):
        acc = own_ref[...].astype(F32)
        for j in range(nj):
            acc = acc + rec_ref[j].astype(F32)
        o_ref[...] = acc

    return pl.pallas_call(
        body, name=name, grid=(r // tr,), out_shape=jax.ShapeDtypeStruct((r, c), F32),
        in_specs=[pl.BlockSpec((None, tr, c), lambda i: (0, i, 0)),
                  pl.BlockSpec((nj, tr, c), lambda i: (0, i, 0))],
        out_specs=pl.BlockSpec((tr, c), lambda i: (i, 0)),
        compiler_params=_params("parallel"),
    )(own, received)


_HBM = pl.BlockSpec(memory_space=pltpu.HBM)
_SEM = pl.BlockSpec(memory_space=pltpu.SEMAPHORE)
_DATAFLOW = pltpu.SideEffectType.DATAFLOW_SIDE_EFFECTING


def _launch_groups(name, groups, deps=()):
    bufs = [b for g_bufs, _ in groups for b in g_bufs]
    specs = [(len(g_bufs), spec) for g_bufs, g_plans in groups for spec in g_plans]
    nb, ns = len(bufs), len(specs)

    def body(*refs):
        sems = refs[nb + len(deps):nb + len(deps) + 2 * ns]
        me = _group_index(MESH_AXES)
        first, which = 0, 0
        for g_bufs, g_plans in groups:
            ins = refs[first:first + len(g_bufs)]
            for _, plan, n_copies in g_plans:
                copies = plan(ins, me)
                assert len(copies) == n_copies
                for n, (src, dst, mask) in enumerate(copies):
                    _remote(src, dst, sems[2 * which].at[n], sems[2 * which + 1].at[n], mask).start()
                which += 1
            first += len(g_bufs)
        refs[-1][...] = jnp.zeros_like(refs[-1])

    sem_shapes = [pltpu.SemaphoreType.DMA((n,)) for _, (_, _, n) in specs for _ in range(2)]
    outs = pl.pallas_call(
        body, name=name,
        out_shape=(*sem_shapes, *[pltpu.HBM(b.shape, b.dtype) for b in bufs],
                   jax.ShapeDtypeStruct((SUBLANES, LANES), F32)),
        in_specs=[_HBM] * nb + [_ANY] * len(deps),
        out_specs=(*[_SEM] * (2 * ns), *[_HBM] * nb, pl.BlockSpec(memory_space=pltpu.VMEM)),
        input_output_aliases={i: 2 * ns + i for i in range(nb)},
        compiler_params=pltpu.CompilerParams(has_side_effects=_DATAFLOW),
    )(*[pltpu.with_memory_space_constraint(b, pltpu.HBM) for b in bufs], *deps)
    flights, first, which = [], 0, 0
    for g_bufs, g_plans in groups:
        through = list(outs[2 * ns + first:2 * ns + first + len(g_bufs)])
        for land_name, plan, n_copies in g_plans:
            flights.append((land_name, plan, n_copies, outs[2 * which], outs[2 * which + 1], through))
            which += 1
        first += len(g_bufs)
    return flights, outs[-1]


def _launch(name, bufs, plan, n_copies, deps=()):
    (flight,), token = _launch_groups(name, [(bufs, [(name, plan, n_copies)])], deps)
    return flight, token


def _land(flight, after, bufs=None):
    name, plan, n_copies, send_sems, recv_sems, launched = flight
    bufs = launched if bufs is None else bufs
    nb = len(bufs)
    after = list(after) if isinstance(after, (list, tuple)) else [after]

    def body(*refs):
        ins = refs[:nb]
        s_sems, r_sems = refs[nb], refs[nb + 1]
        for n, (src, dst, mask) in enumerate(plan(ins, _group_index(MESH_AXES))):
            cp = _remote(src, dst, s_sems.at[n], r_sems.at[n], mask)
            cp.wait_send()
            cp.wait_recv()

    outs = pl.pallas_call(
        body, name=name + "_land",
        out_shape=tuple(pltpu.HBM(b.shape, b.dtype) for b in bufs),
        in_specs=[_HBM] * nb + [_SEM, _SEM] + [_ANY] * len(after), out_specs=tuple([_HBM] * nb),
        input_output_aliases={i: i for i in range(nb)},
        compiler_params=pltpu.CompilerParams(has_side_effects=_DATAFLOW),
    )(*bufs, send_sems, recv_sems, *after)
    return list(outs)


def _plan_gather_ici(phase):
    bx, by = _AXIS_BIT["x"], _AXIS_BIT["y"]

    def plan(refs, me):
        copies = []
        for ref in refs:
            half = ref.shape[1] // 2

            def piece(slot, color, mask, ref=ref, half=half):
                p = ref.at[slot, pl.ds(color * half, half)]
                return (p, p, mask)

            if phase == 0:
                copies += [piece(me, 0, bx), piece(me, 1, by)]
            else:
                copies += [piece(me, 0, by), piece(me ^ bx, 0, by), piece(me, 1, bx), piece(me ^ by, 1, bx)]
        return copies

    return plan


def _plan_d2d(masks):
    def plan(refs, me):
        return [(ref.at[me ^ m], ref.at[me ^ m], _AXIS_BIT["c"]) for ref in refs for m in masks]

    return plan


def _plan_gather_d2d(refs, me):
    return _plan_d2d(_CHIP_MASKS)(refs, me)


def _plan_neighbours(refs, me):
    return [(ref.at[me], ref.at[me], _AXIS_BIT[a]) for ref in refs for a in ("x", "y")]


def _plan_diagonal(refs, me):
    bx, by = _AXIS_BIT["x"], _AXIS_BIT["y"]
    copies = []
    for ref in refs:
        half = ref.shape[1] // 2
        lo = ref.at[me ^ bx, pl.ds(0, half)]
        hi = ref.at[me ^ by, pl.ds(half, half)]
        copies += [(lo, lo, by), (hi, hi, bx)]
    return copies


def _plan_scatter_d2d(refs, me):
    na = len(refs) // 2
    copies = []
    for a in range(na):
        for j, m in enumerate(_CHIP_MASKS):
            copies.append((refs[a].at[me ^ _AXIS_BIT["c"] ^ m], refs[na + a].at[j], _AXIS_BIT["c"]))
    return copies


def _plan_scatter_ici(refs, me):
    del me
    na = len(refs) // 2
    copies = []
    for a in range(na):
        for n, m in enumerate(_CHIP_MASKS[1:]):
            copies.append((refs[a].at[n + 1], refs[na + a].at[n], m))
    return copies


def _with_deps(body, n_in, deps):
    if not deps:
        return body

    def wrapped(*refs):
        return body(*refs[:n_in], *refs[n_in + len(deps):])

    return wrapped


def _reduce_scatter_start(tag, grads):
    lands = [lax.empty((len(_CHIP_MASKS),) + g.shape[1:], g.dtype) for g in grads]
    return _launch("rs%s_d2d" % tag, list(grads) + lands, _plan_scatter_d2d, len(_CHIP_MASKS) * len(grads))


def _reduce_scatter_middle(tag, flight, after, me):
    bufs = _land(flight, after)
    na = len(bufs) // 2
    own_slots = jnp.stack([me ^ m for m in _CHIP_MASKS]).astype(jnp.int32)
    sums = [_add_received("rs%s_add_d2d_%d" % (tag, a), bufs[a], own_slots, bufs[na + a], BF16) for a in range(na)]
    lands = [lax.empty((len(_CHIP_MASKS) - 1,) + h.shape[1:], h.dtype) for h in sums]
    return _launch("rs%s_ici" % tag, sums + lands, _plan_scatter_ici, (len(_CHIP_MASKS) - 1) * na)


def _reduce_scatter_finish(flight, after):
    bufs = _land(flight, after)
    na = len(bufs) // 2
    return [(bufs[a], bufs[na + a]) for a in range(na)]


def _all_gather_2d(name, x, deps=()):
    r, c = x.shape

    def body(x_ref, out_ref, send_sems, recv_sems):
        me = _group_index(MESH_AXES)
        out_ref[me] = x_ref[...]
        copies = []
        for k in range(1, N_DEV):
            cp = pltpu.make_async_remote_copy(
                src_ref=x_ref, dst_ref=out_ref.at[me],
                send_sem=send_sems.at[k - 1], recv_sem=recv_sems.at[k - 1],
                device_id=_peer_device(MESH_AXES, k), device_id_type=pl.DeviceIdType.MESH)
            cp.start()
            copies.append(cp)
        for cp in copies:
            cp.wait()

    vmem = pl.BlockSpec(memory_space=pltpu.VMEM)
    return pl.pallas_call(
        _with_deps(body, 1, deps), name=name, out_shape=jax.ShapeDtypeStruct((N_DEV, r, c), x.dtype),
        in_specs=[vmem] + [_ANY] * len(deps), out_specs=vmem,
        scratch_shapes=[pltpu.SemaphoreType.DMA((N_DEV - 1,)), pltpu.SemaphoreType.DMA((N_DEV - 1,))],
    )(x, *deps)


def _sum_slots(name, buf, out_dtype):
    pre, n, r, c = buf.shape
    tr = _tile(r, max(SUBLANES * 2, (1 << 20) // c))

    def body(b_ref, o_ref):
        acc = b_ref[0].astype(F32)
        for q in range(1, n):
            acc = acc + b_ref[q].astype(F32)
        o_ref[...] = acc.astype(o_ref.dtype)

    return pl.pallas_call(
        body, name=name, grid=(pre, r // tr),
        out_shape=jax.ShapeDtypeStruct((pre, r, c), out_dtype),
        in_specs=[pl.BlockSpec((None, n, tr, c), lambda i, j: (i, 0, j, 0))],
        out_specs=pl.BlockSpec((None, tr, c), lambda i, j: (i, j, 0)),
        compiler_params=_params("parallel", "parallel"),
    )(buf)


def _all_gather_weights(bufs):
    return _gather_d2d("ag_d2d", list(_gather_ici("ag_ici", bufs)))


_CHIP_MASKS = (0, _AXIS_BIT["y"], _AXIS_BIT["x"], _AXIS_BIT["x"] | _AXIS_BIT["y"])


def _reduce_scatter_grads(bufs):
    me = _group_index(MESH_AXES)
    bc = _AXIS_BIT["c"]
    r1 = _scatter("rs_d2d", bufs, lambda i: [i ^ bc ^ m for m in _CHIP_MASKS], (bc,) * len(_CHIP_MASKS))
    own_slots = jnp.stack([me ^ m for m in _CHIP_MASKS]).astype(jnp.int32)
    half = [_add_received("rs_add_d2d_%d" % a, b, own_slots, r, BF16) for a, (b, r) in enumerate(zip(bufs, r1))]
    r2 = _scatter("rs_ici", half, lambda i: [1, 2, 3], _CHIP_MASKS[1:])
    return [_add_final("rs_add_ici_%d" % a, h, r) for a, (h, r) in enumerate(zip(half, r2))]


def _matmul(name, mode, grid, a, a_spec, b, b_spec, out_shapes, out_specs, acc_shape,
            epilogue=None, extras=(), extra_specs=(), aliases=None, deps=()):
    nk = grid[2]
    n_extra = len(extras)
    n_out = len(out_shapes)

    def finish(acc, extra_refs, out_refs):
        if epilogue is None:
            out_refs[0][...] = acc.astype(out_refs[0].dtype)
        else:
            epilogue(acc, extra_refs, out_refs)

    def product(a_ref, b_ref):
        if len(b_ref.shape) == 2:
            return _dot(a_ref[...], b_ref[...], mode)
        width = a_ref.shape[1] // b_ref.shape[0]
        total = None
        for i in range(b_ref.shape[0]):
            part = _dot(a_ref[:, i * width:(i + 1) * width], b_ref[i], mode)
            total = part if total is None else total + part
        return total

    def body(*refs):
        a_ref, b_ref = refs[0], refs[1]
        extra_refs = refs[2:2 + n_extra]
        out_refs = refs[2 + n_extra:2 + n_extra + n_out]
        if nk == 1:
            finish(product(a_ref, b_ref), extra_refs, out_refs)
            return
        acc_ref = refs[-1]
        k = pl.program_id(2)

        @pl.when(k == 0)
        def _():
            acc_ref[...] = product(a_ref, b_ref)

        @pl.when((k > 0) & (k < nk - 1))
        def _():
            acc_ref[...] += product(a_ref, b_ref)

        @pl.when(k == nk - 1)
        def _():
            finish(acc_ref[...] + product(a_ref, b_ref), extra_refs, out_refs)

    scratch = [] if nk == 1 else [pltpu.VMEM(acc_shape, F32)]
    return pl.pallas_call(
        _with_deps(body, 2 + n_extra, deps), name=name, grid=grid, out_shape=tuple(out_shapes),
        in_specs=[a_spec, b_spec] + list(extra_specs) + [_ANY] * len(deps), out_specs=tuple(out_specs),
        scratch_shapes=scratch, input_output_aliases=aliases or {},
        compiler_params=_params("parallel", "parallel", "arbitrary"),
    )(a, b, *extras, *deps)


def _sds(shape, dtype):
    return jax.ShapeDtypeStruct(tuple(shape), dtype)


def _project_slots(name, h, w_full, slots, out_cols, proj_in=None, deps=()):
    s, d = h.shape
    _, _, wide = w_full.shape
    tm = _tile(s, 1024)
    n_in = 4 if proj_in is not None else 3

    def body(*refs):
        refs[-1][...] = _dot(refs[1][...], refs[2][...])

    grid_spec = pltpu.PrefetchScalarGridSpec(
        num_scalar_prefetch=1, grid=(s // tm, slots.shape[0]),
        in_specs=[pl.BlockSpec((tm, d), lambda i, j, sl: (i, 0)),
                  pl.BlockSpec((None, d, wide), lambda i, j, sl: (sl[j], 0, 0))]
        + [_ANY] * (n_in - 3 + len(deps)),
        out_specs=pl.BlockSpec((tm, wide), lambda i, j, sl: (i, sl[j])))
    extra = ([proj_in] if proj_in is not None else []) + list(deps)
    return pl.pallas_call(
        body, name=name, grid_spec=grid_spec, out_shape=_sds((s, out_cols), F32),
        input_output_aliases={3: 0} if proj_in is not None else {},
        compiler_params=_params("parallel", "arbitrary"),
    )(slots, h, w_full, *extra)


def _ada_forward(c_all, w_ada, b_shard):
    nb, d = c_all.shape
    w = w_ada.shape[1]
    tn = _tile(w, 512)

    def body(c_ref, w_ref, b_ref, o_ref):
        cv = c_ref[...]
        sc = cv * jax.nn.sigmoid(cv)
        o_ref[...] = jnp.dot(sc, w_ref[...], precision=lax.Precision.HIGHEST,
                             preferred_element_type=F32) + b_ref[...]

    return pl.pallas_call(
        body, name="ada_fwd", grid=(w // tn,), out_shape=_sds((nb, w), F32),
        in_specs=[pl.BlockSpec((nb, d), lambda j: (0, 0)), pl.BlockSpec((d, tn), lambda j: (0, j)),
                  pl.BlockSpec((1, tn), lambda j: (0, j))],
        out_specs=pl.BlockSpec((nb, tn), lambda j: (0, j)),
        compiler_params=_params("parallel"),
    )(c_all, w_ada, b_shard)


def _ada_weight_grad(c_all, dmod_cols, deps=()):
    nb, d = c_all.shape
    w = dmod_cols.shape[1]
    tn = _tile(w, 512)

    def body(c_ref, g_ref, o_ref):
        cv = c_ref[...]
        sc = cv * jax.nn.sigmoid(cv)
        o_ref[...] = lax.dot_general(sc, g_ref[...], _TN, precision=lax.Precision.HIGHEST,
                                     preferred_element_type=F32)

    return pl.pallas_call(
        _with_deps(body, 2, deps), name="ada_wgrad", grid=(w // tn,), out_shape=_sds((d, w), F32),
        in_specs=[pl.BlockSpec((nb, d), lambda j: (0, 0)), pl.BlockSpec((nb, tn), lambda j: (0, j))]
        + [_ANY] * len(deps),
        out_specs=pl.BlockSpec((d, tn), lambda j: (0, j)),
        compiler_params=_params("parallel"),
    )(c_all, dmod_cols, *deps)


def _norm_forward(name, x, norm_w, scale, shift, deps=()):
    s, d = x.shape
    tm = _tile(s, 256)

    def body(x_ref, w_ref, sc_ref, sh_ref, h_ref):
        xv = x_ref[...]
        r = lax.rsqrt(jnp.mean(xv * xv, axis=-1, keepdims=True) + NORM_EPS)
        h = (xv * r * w_ref[...]) * (1.0 + sc_ref[...]) + sh_ref[...]
        h_ref[...] = h.astype(BF16)

    vec = pl.BlockSpec((1, d), lambda i: (0, 0))
    row = pl.BlockSpec((tm, d), lambda i: (i, 0))
    return pl.pallas_call(
        _with_deps(body, 4, deps), name=name, grid=(s // tm,), out_shape=_sds((s, d), BF16),
        in_specs=[row, vec, vec, vec] + [_ANY] * len(deps), out_specs=row, compiler_params=_params("parallel"),
    )(x, norm_w, scale, shift, *deps)


def _norm_backward(name, dh, x, norm_w, scale, dres, gated=None, deps=()):
    s, d = x.shape
    tm = _tile(s, 256)
    n_in = 7 if gated else 5

    def body(*refs):
        dh_ref, x_ref, w_ref, sc_ref, dres_ref = refs[:5]
        dx_ref, dshift_ref, dscale_ref, dw_ref = refs[n_in:n_in + 4]
        sums = (dshift_ref, dscale_ref, dw_ref) + ((refs[n_in + 5],) if gated else ())

        @pl.when(pl.program_id(0) == 0)
        def _():
            for ref in sums:
                ref[...] = jnp.zeros_like(ref)

        xv = x_ref[...]
        g = dh_ref[...]
        r = lax.rsqrt(jnp.mean(xv * xv, axis=-1, keepdims=True) + NORM_EPS)
        n = xv * r
        gain = 1.0 + sc_ref[...]
        gn = g * n
        dshift_ref[...] += jnp.sum(g, axis=0, keepdims=True)
        dscale_ref[...] += jnp.sum(gn, axis=0, keepdims=True) * w_ref[...]
        dw_ref[...] += jnp.sum(gn, axis=0, keepdims=True) * gain
        dn = g * (w_ref[...] * gain)
        dx = dres_ref[...] + r * (dn - n * jnp.mean(dn * n, axis=-1, keepdims=True))
        dx_ref[...] = dx
        if gated:
            gate_ref, other_ref = refs[5:7]
            refs[n_in + 4][...] = (dx * gate_ref[...]).astype(BF16)
            refs[n_in + 5][...] += jnp.sum(dx * other_ref[...].astype(F32), axis=0, keepdims=True)

    vec = pl.BlockSpec((1, d), lambda i: (0, 0))
    row = pl.BlockSpec((tm, d), lambda i: (i, 0))
    vec_out = _sds((1, d), F32)
    return pl.pallas_call(
        _with_deps(body, n_in, deps), name=name, grid=(s // tm,),
        out_shape=(_sds((s, d), F32), vec_out, vec_out, vec_out) + ((_sds((s, d), BF16), vec_out) if gated else ()),
        in_specs=[row, row, vec, vec, row] + ([vec, row] if gated else []) + [_ANY] * len(deps),
        out_specs=(row, vec, vec, vec) + ((row, vec) if gated else ()),
        compiler_params=_params("arbitrary"),
    )(dh, x, norm_w, scale, dres, *(gated or ()), *deps)


def _split_bf16(v):
    hi = v.astype(BF16)
    lo = (v - hi.astype(F32)).astype(BF16)
    return hi, lo


def _pool_forward(proj, w_pool, pool_scale, deps=()):
    s = proj.shape[0]
    g_n, cg, _ = w_pool.shape
    t = POOL_TILE
    nt = s // t

    def body(cur_ref, prev_ref, wp_ref, sc_ref, pooled_ref, ya_ref):
        g = pl.program_id(0)
        ti = pl.program_id(1)
        win = jnp.left_shift(2, g)
        row = lax.broadcasted_iota(jnp.int32, (t, t), 0)
        col = lax.broadcasted_iota(jnp.int32, (t, t), 1)
        lag = row - col
        band_cur = ((lag >= 0) & (lag < win)).astype(BF16)
        band_prev = ((lag + t < win) & (ti > 0)).astype(BF16)
        u = cur_ref[...]
        u_hi, u_lo = _split_bf16(u)
        p_hi, p_lo = _split_bf16(prev_ref[...])
        wsum = (_dot(band_cur, u_hi) + _dot(band_cur, u_lo)
                + _dot(band_prev, p_hi) + _dot(band_prev, p_lo))
        tok = ti * t + lax.broadcasted_iota(jnp.int32, (t, 1), 0)
        count = jnp.minimum(tok + 1, win).astype(F32)
        pooled = (wsum / count - u).astype(BF16)
        pooled_ref[...] = pooled
        ya_ref[...] = (_dot(pooled, wp_ref[...]) * sc_ref[...]).astype(BF16)

    blk = pl.BlockSpec((t, cg), lambda g, i: (i, g))
    return pl.pallas_call(
        _with_deps(body, 4, deps), name="pool_fwd", grid=(g_n, nt),
        out_shape=(_sds((s, g_n * cg), BF16), _sds((s, g_n * cg), BF16)),
        in_specs=[blk, pl.BlockSpec((t, cg), lambda g, i: (jnp.maximum(i - 1, 0), g)),
                  pl.BlockSpec((None, cg, cg), lambda g, i: (g, 0, 0)),
                  pl.BlockSpec((1, cg), lambda g, i: (0, g))] + [_ANY] * len(deps),
        out_specs=(blk, blk), compiler_params=_params("parallel", "parallel"),
    )(proj, proj, w_pool, pool_scale, *deps)


def _pool_backward(dya, pooled, w_pool, pool_scale, dproj):
    s = dya.shape[0]
    g_n, cg, _ = w_pool.shape
    t = POOL_TILE
    nt = s // t

    def body(dya_ref, dya_next_ref, pooled_ref, wp_ref, sc_ref, dproj_in, du_ref, gw_ref, gs_ref):
        del dproj_in
        g = pl.program_id(0)
        ti = pl.program_id(1)

        @pl.when(ti == 0)
        def _():
            gw_ref[...] = jnp.zeros_like(gw_ref)
            gs_ref[...] = jnp.zeros_like(gs_ref)

        win = jnp.left_shift(2, g)
        wp = wp_ref[...]
        sc = sc_ref[...]
        pooled_v = pooled_ref[...]
        dya_v = dya_ref[...].astype(F32)
        mixed = _dot(pooled_v, wp)
        gs_ref[...] += jnp.sum(dya_v * mixed, axis=0, keepdims=True)
        dmixed = (dya_v * sc).astype(BF16)
        gw_ref[...] += _dot(pooled_v, dmixed, "tn")
        dpooled = _dot(dmixed, wp, "nt")
        dmixed_next = (dya_next_ref[...].astype(F32) * sc).astype(BF16)
        dpooled_next = _dot(dmixed_next, wp, "nt")
        tok = ti * t + lax.broadcasted_iota(jnp.int32, (t, 1), 0)
        e_cur = dpooled / jnp.minimum(tok + 1, win).astype(F32)
        e_next = dpooled_next / jnp.minimum(tok + t + 1, win).astype(F32)
        row = lax.broadcasted_iota(jnp.int32, (t, t), 0)
        col = lax.broadcasted_iota(jnp.int32, (t, t), 1)
        lead = col - row
        band_cur = ((lead >= 0) & (lead < win)).astype(BF16)
        band_next = ((lead + t < win) & (ti < nt - 1)).astype(BF16)
        c_hi, c_lo = _split_bf16(e_cur)
        n_hi, n_lo = _split_bf16(e_next)
        du = (_dot(band_cur, c_hi) + _dot(band_cur, c_lo)
              + _dot(band_next, n_hi) + _dot(band_next, n_lo)) - dpooled
        du_ref[...] = du.astype(BF16)

    blk = pl.BlockSpec((t, cg), lambda g, i: (i, g))
    du, gw, gs = pl.pallas_call(
        body, name="pool_bwd", grid=(g_n, nt),
        out_shape=(_sds(dproj.shape, BF16), _sds((g_n, cg, cg), F32), _sds((1, g_n * cg), F32)),
        in_specs=[blk, pl.BlockSpec((t, cg), lambda g, i: (jnp.minimum(i + 1, nt - 1), g)), blk,
                  pl.BlockSpec((None, cg, cg), lambda g, i: (g, 0, 0)),
                  pl.BlockSpec((1, cg), lambda g, i: (0, g)),
                  pl.BlockSpec(memory_space=pl.ANY)],
        out_specs=(blk, pl.BlockSpec((None, cg, cg), lambda g, i: (g, 0, 0)),
                   pl.BlockSpec((1, cg), lambda g, i: (0, g))),
        input_output_aliases={5: 0}, compiler_params=_params("parallel", "arbitrary"),
    )(dya, dya, pooled, w_pool, pool_scale, dproj)
    return du, gw, gs


def _qkv_prepare(proj, q_norm_w, k_norm_w, width, deps=()):
    s = proj.shape[0]
    tm = _tile(s, 256)
    heads = width // HEAD_DIM

    def body(q_ref, k_ref, v_ref, qw_ref, kw_ref, qn_ref, kn_ref, vb_ref):
        for h in range(heads):
            cols = slice(h * HEAD_DIM, (h + 1) * HEAD_DIM)
            for src, w_ref, dst in ((q_ref, qw_ref, qn_ref), (k_ref, kw_ref, kn_ref)):
                v = src[:, cols]
                r = lax.rsqrt(jnp.mean(v * v, axis=-1, keepdims=True) + NORM_EPS)
                dst[:, cols] = (v * r * w_ref[...]).astype(BF16)
        vb_ref[...] = v_ref[...].astype(BF16)

    vec = pl.BlockSpec((1, HEAD_DIM), lambda i: (0, 0))
    out_spec = pl.BlockSpec((tm, width), lambda i: (i, 0))
    return pl.pallas_call(
        _with_deps(body, 5, deps), name="qkv_prep", grid=(s // tm,),
        out_shape=(_sds((s, width), BF16),) * 3,
        in_specs=[pl.BlockSpec((tm, width), lambda i: (i, 1)), pl.BlockSpec((tm, width), lambda i: (i, 2)),
                  pl.BlockSpec((tm, width), lambda i: (i, 3)), vec, vec] + [_ANY] * len(deps),
        out_specs=(out_spec,) * 3, compiler_params=_params("parallel"),
    )(proj, proj, proj, q_norm_w, k_norm_w, *deps)


def _qk_norm_backward(name, dn, proj, col_block, norm_w, dproj, width, deps=()):
    s = proj.shape[0]
    tm = _tile(s, 256)
    heads = width // HEAD_DIM

    def body(dn_ref, q_ref, w_ref, dproj_in, dq_ref, gw_ref):
        del dproj_in

        @pl.when(pl.program_id(0) == 0)
        def _():
            gw_ref[...] = jnp.zeros_like(gw_ref)

        wv = w_ref[...]
        gw = jnp.zeros((1, HEAD_DIM), F32)
        for h in range(heads):
            cols = slice(h * HEAD_DIM, (h + 1) * HEAD_DIM)
            v = q_ref[:, cols]
            g = dn_ref[:, cols]
            r = lax.rsqrt(jnp.mean(v * v, axis=-1, keepdims=True) + NORM_EPS)
            n = v * r
            gw = gw + jnp.sum(g * n, axis=0, keepdims=True)
            gn = g * wv
            dq_ref[:, cols] = (r * (gn - n * jnp.mean(gn * n, axis=-1, keepdims=True))).astype(BF16)
        gw_ref[...] += gw

    blk = pl.BlockSpec((tm, width), lambda i: (i, col_block))
    return pl.pallas_call(
        _with_deps(body, 4, deps), name=name, grid=(s // tm,),
        out_shape=(_sds(dproj.shape, BF16), _sds((1, HEAD_DIM), F32)),
        in_specs=[pl.BlockSpec((tm, width), lambda i: (i, 0)), blk,
                  pl.BlockSpec((1, HEAD_DIM), lambda i: (0, 0)), pl.BlockSpec(memory_space=pl.ANY)]
        + [_ANY] * len(deps),
        out_specs=(blk, pl.BlockSpec((1, HEAD_DIM), lambda i: (0, 0))),
        input_output_aliases={3: 0}, compiler_params=_params("arbitrary"),
    )(dn, proj, norm_w, dproj, *deps)


def _strict_upper(n):
    row = lax.broadcasted_iota(jnp.int32, (n, n), 0)
    col = lax.broadcasted_iota(jnp.int32, (n, n), 1)
    return (row > col).astype(BF16)


def _strict_lower(n):
    row = lax.broadcasted_iota(jnp.int32, (n, n), 0)
    col = lax.broadcasted_iota(jnp.int32, (n, n), 1)
    return (row < col).astype(BF16)


def _cumulate(v, tri):
    hi, lo = _split_bf16(v)
    return _dot(hi, tri) + _dot(lo, tri)


def _log_sigmoid(z):
    return jnp.minimum(z, 0.0) - jnp.log(1.0 + jnp.exp(-jnp.abs(z)))


def _attention_forward(qn, kn, vb):
    s, width = qn.shape
    heads = width // HEAD_DIM
    tq, tk = Q_TILE, K_TILE
    hp = HEADS_PER_STEP
    assert tq == tk and s % tq == 0 and heads % hp == 0
    scale = 1.0 / math.sqrt(HEAD_DIM)

    def body(q_ref, k_ref, v_ref, o_ref, a_scr):
        qi = pl.program_id(1)
        upper = _strict_upper(tk)
        causal = lax.broadcasted_iota(jnp.int32, (tq, tk), 1) < lax.broadcasted_iota(jnp.int32, (tq, tk), 0)
        head_cols = [slice(u * HEAD_DIM, (u + 1) * HEAD_DIM) for u in range(hp)]

        def weights(kb, carry, masked):
            rows = pl.ds(pl.multiple_of(kb * tk, tk), tk)
            out = []
            for u, cols in enumerate(head_cols):
                later = carry[u]
                z = _dot(q_ref[:, cols], k_ref[rows, cols], "nt") * scale
                log_beta = _log_sigmoid(z)
                l = log_beta - z
                if masked:
                    l = jnp.where(causal, l, 0.0)
                a = jnp.exp(log_beta + _cumulate(l, upper) + later)
                if masked:
                    a = jnp.where(causal, a, 0.0)
                a_scr[u, :, rows] = a.astype(BF16)
                out.append(later + jnp.sum(l, axis=1, keepdims=True))
            return tuple(out)

        later = weights(qi, tuple(jnp.zeros((tq, 1), F32) for _ in range(hp)), True)
        lax.fori_loop(0, qi, lambda i, c: weights(qi - 1 - i, c, False), later)

        def mix(kb, accs):
            rows = pl.ds(pl.multiple_of(kb * tk, tk), tk)
            return tuple(acc + _dot(a_scr[u, :, rows], v_ref[rows, cols])
                         for u, (acc, cols) in enumerate(zip(accs, head_cols)))

        accs = lax.fori_loop(0, qi + 1, mix, tuple(jnp.zeros((tq, HEAD_DIM), F32) for _ in range(hp)))
        for acc, cols in zip(accs, head_cols):
            o_ref[:, cols] = acc.astype(BF16)

    full = pl.BlockSpec((s, hp * HEAD_DIM), lambda h, i: (0, h))
    blk = pl.BlockSpec((tq, hp * HEAD_DIM), lambda h, i: (i, h))
    return pl.pallas_call(
        body, name="attn_fwd", grid=(heads // hp, s // tq), out_shape=_sds((s, width), BF16),
        in_specs=[blk, full, full], out_specs=blk, scratch_shapes=[pltpu.VMEM((hp, tq, s), BF16)],
        compiler_params=_params("parallel", "parallel"),
    )(qn, kn, vb)


def _attention_backward(qn, kn, vb, dout, dproj, v_col_block, deps=()):
    s, width = qn.shape
    heads = width // HEAD_DIM
    tq, tk = Q_TILE, K_TILE
    hp = HEADS_PER_STEP_BWD
    nq = s // tq
    scale = 1.0 / math.sqrt(HEAD_DIM)
    v_block0 = v_col_block * (heads // hp)

    def body(q_ref, k_ref, v_ref, do_ref, dproj_in, dq_ref, dk_ref, dv_ref,
             a_scr, lb_scr, dz_scr, dkt_scr, dvt_scr):
        del dproj_in
        qi = pl.program_id(1)

        @pl.when(qi == 0)
        def _():
            dkt_scr[...] = jnp.zeros_like(dkt_scr)
            dvt_scr[...] = jnp.zeros_like(dvt_scr)

        upper = _strict_upper(tk)
        lower = _strict_lower(tk)
        causal = lax.broadcasted_iota(jnp.int32, (tq, tk), 1) < lax.broadcasted_iota(jnp.int32, (tq, tk), 0)
        head_cols = [slice(u * HEAD_DIM, (u + 1) * HEAD_DIM) for u in range(hp)]

        def weights(kb, carry, masked):
            rows = pl.ds(pl.multiple_of(kb * tk, tk), tk)
            out = []
            for u, cols in enumerate(head_cols):
                later = carry[u]
                z = _dot(q_ref[:, cols], k_ref[rows, cols], "nt") * scale
                log_beta = _log_sigmoid(z)
                l = log_beta - z
                if masked:
                    l = jnp.where(causal, l, 0.0)
                a = jnp.exp(log_beta + _cumulate(l, upper) + later)
                if masked:
                    a = jnp.where(causal, a, 0.0)
                a_scr[u, :, rows] = a
                lb_scr[u, :, rows] = log_beta
                out.append(later + jnp.sum(l, axis=1, keepdims=True))
            return tuple(out)

        zeros = tuple(jnp.zeros((tq, 1), F32) for _ in range(hp))
        later = weights(qi, zeros, True)
        lax.fori_loop(0, qi, lambda i, c: weights(qi - 1 - i, c, False), later)

        q_t = [jnp.transpose(q_ref[:, cols].astype(F32)).astype(BF16) for cols in head_cols]
        do_t = [jnp.transpose(do_ref[:, cols].astype(F32)).astype(BF16) for cols in head_cols]

        def scores(kb, carry, masked):
            rows = pl.ds(pl.multiple_of(kb * tk, tk), tk)
            out = []
            for u, cols in enumerate(head_cols):
                before = carry[u]
                beta = jnp.exp(lb_scr[u, :, rows])
                g = a_scr[u, :, rows] * _dot(do_ref[:, cols], v_ref[rows, cols], "nt")
                p = _cumulate(g, lower) + before
                dz = g - (g + p) * beta
                if masked:
                    dz = jnp.where(causal, dz, 0.0)
                dz_scr[u, :, rows] = (dz * scale).astype(BF16)
                out.append(before + jnp.sum(g, axis=1, keepdims=True))
            return tuple(out)

        before = lax.fori_loop(0, qi, lambda i, c: scores(i, c, False), zeros)
        scores(qi, before, True)

        def products(kb, dqs):
            rows = pl.ds(pl.multiple_of(kb * tk, tk), tk)
            out = []
            for u, cols in enumerate(head_cols):
                dz = dz_scr[u, :, rows]
                dkt_scr[cols, rows] += _dot(q_t[u], dz)
                dvt_scr[cols, rows] += _dot(do_t[u], a_scr[u, :, rows].astype(BF16))
                out.append(dqs[u] + _dot(dz, k_ref[rows, cols]))
            return tuple(out)

        dqs = lax.fori_loop(0, qi + 1, products, tuple(jnp.zeros((tq, HEAD_DIM), F32) for _ in range(hp)))
        for u, cols in enumerate(head_cols):
            dq_ref[:, cols] = dqs[u]

        @pl.when(qi == nq - 1)
        def _():
            dk_ref[...] = jnp.transpose(dkt_scr[...])
            dv_ref[...] = jnp.transpose(dvt_scr[...]).astype(BF16)

    wide = hp * HEAD_DIM
    full = pl.BlockSpec((s, wide), lambda h, i: (0, h))
    blk = pl.BlockSpec((tq, wide), lambda h, i: (i, h))
    return pl.pallas_call(
        _with_deps(body, 5, deps), name="attn_bwd", grid=(heads // hp, nq),
        out_shape=(_sds((s, width), F32), _sds((s, width), F32), _sds(dproj.shape, BF16)),
        in_specs=[blk, full, full, blk, pl.BlockSpec(memory_space=pl.ANY)] + [_ANY] * len(deps),
        out_specs=(blk, full, pl.BlockSpec((s, wide), lambda h, i: (0, v_block0 + h))),
        scratch_shapes=[pltpu.VMEM((hp, tq, s), F32), pltpu.VMEM((hp, tq, s), F32), pltpu.VMEM((hp, tq, s), BF16),
                        pltpu.VMEM((wide, s), F32), pltpu.VMEM((wide, s), F32)],
        input_output_aliases={4: 2}, compiler_params=_params("parallel", "arbitrary"),
    )(qn, kn, vb, dout, dproj, *deps)


def _place_columns(name, src, dst, col_block):
    s, w = src.shape
    tm = _tile(s, 512)

    def body(src_ref, dst_in, out_ref):
        del dst_in
        out_ref[...] = src_ref[...]

    return pl.pallas_call(
        body, name=name, grid=(s // tm,), out_shape=_sds(dst.shape, dst.dtype),
        in_specs=[pl.BlockSpec((tm, w), lambda i: (i, 0)), pl.BlockSpec(memory_space=pl.ANY)],
        out_specs=pl.BlockSpec((tm, w), lambda i: (i, col_block)),
        input_output_aliases={1: 0}, compiler_params=_params("parallel"),
    )(src, dst)


def _cast_into_slot(name, x, slot):
    r, c = x.shape
    tr = _tile(r, max(SUBLANES * 2, (1 << 20) // c), SUBLANES * 2)

    def body(slot_ref, x_ref, o_ref):
        del slot_ref
        o_ref[...] = x_ref[...].astype(BF16)

    grid_spec = pltpu.PrefetchScalarGridSpec(
        num_scalar_prefetch=1, grid=(r // tr,),
        in_specs=[pl.BlockSpec((tr, c), lambda i, slot_ref: (i, 0))],
        out_specs=pl.BlockSpec((None, tr, c), lambda i, slot_ref: (slot_ref[0], i, 0)))
    return pl.pallas_call(
        body, name=name, grid_spec=grid_spec, out_shape=_sds((N_DEV, r, c), BF16),
        compiler_params=_params("parallel"),
    )(slot, x)


def _adamw_update(gv, w_ref, m_ref, v_ref, d_ref, nm_ref, nv_ref):
    c1 = 1.0 - ADAM_B1 ** ADAM_STEP
    c2 = 1.0 - ADAM_B2 ** ADAM_STEP
    nm = ADAM_B1 * m_ref[...] + (1.0 - ADAM_B1) * gv
    nv = ADAM_B2 * v_ref[...] + (1.0 - ADAM_B2) * (gv * gv)
    d_ref[...] = -ADAM_LR * ((nm / c1) / (jnp.sqrt(nv / c2) + ADAM_EPS) + ADAM_WD * w_ref[...])
    nm_ref[...] = nm
    nv_ref[...] = nv


def _adamw(name, w, g, m, v):
    r, c = w.shape
    tr = _tile(r, max(SUBLANES, (1 << 19) // c))

    def body(w_ref, g_ref, m_ref, v_ref, d_ref, nm_ref, nv_ref):
        _adamw_update(g_ref[...], w_ref, m_ref, v_ref, d_ref, nm_ref, nv_ref)

    blk = pl.BlockSpec((tr, c), lambda i: (i, 0))
    return pl.pallas_call(
        body, name=name, grid=(r // tr,), out_shape=(_sds((r, c), F32),) * 3,
        in_specs=[blk] * 4, out_specs=(blk,) * 3, compiler_params=_params("parallel"),
    )(w, g, m, v)


def _adamw_summed(name, w, own, received, m, v):
    r, c = w.shape
    nj = received.shape[0]
    tr = _tile(r, max(2 * SUBLANES, (1 << 19) // c), 2 * SUBLANES)

    def body(w_ref, own_ref, rec_ref, m_ref, v_ref, g_ref, d_ref, nm_ref, nv_ref):
        gv = own_ref[...].astype(F32)
        for j in range(nj):
            gv = gv + rec_ref[j].astype(F32)
        g_ref[...] = gv
        _adamw_update(gv, w_ref, m_ref, v_ref, d_ref, nm_ref, nv_ref)

    blk = pl.BlockSpec((tr, c), lambda i: (i, 0))
    return pl.pallas_call(
        body, name=name, grid=(r // tr,), out_shape=(_sds((r, c), F32),) * 4,
        in_specs=[blk, pl.BlockSpec((None, tr, c), lambda i: (0, i, 0)),
                  pl.BlockSpec((nj, tr, c), lambda i: (0, i, 0)), blk, blk],
        out_specs=(blk,) * 4, compiler_params=_params("parallel"),
    )(w, own, received, m, v)


def _rows_of_lanes(v):
    rows = v.shape[1] // LANES
    out = v.reshape(rows, LANES)
    pad = (-rows) % SUBLANES
    if pad:
        out = jnp.pad(out, ((0, pad), (0, 0)))
    return out


def kernel(x, c, w_ada, b_ada, norm1_w, w_in, q_norm_w, k_norm_w, w_pool, pool_scale, w_a_up, w_b_up, w_o, norm2_w, w_ff1, w_ff2, loss_target, m_w_ada, m_b_ada, m_norm1_w, m_w_in, m_q_norm_w, m_k_norm_w, m_w_pool, m_pool_scale, m_w_a_up, m_w_b_up, m_w_o, m_norm2_w, m_w_ff1, m_w_ff2, v_w_ada, v_b_ada, v_norm1_w, v_w_in, v_q_norm_w, v_k_norm_w, v_w_pool, v_pool_scale, v_w_a_up, v_w_b_up, v_w_o, v_norm2_w, v_w_ff1, v_w_ff2):
    _, s, d = x.shape
    half = d // 2
    d8 = d // N_DEV
    n_groups = len(POOL_WINDOWS)
    cg = half // n_groups
    me = _group_index(MESH_AXES)

    x2 = x[0]
    target = loss_target[0]

    my_slot = jnp.reshape(me, (1,)).astype(jnp.int32)

    def cast(i, t):
        return _cast_into_slot("cast_w%d" % i, t, my_slot)

    def gather_start(tag, bufs, phase):
        if phase < 2:
            return _launch("ag%s_ici%d" % (tag, phase), bufs, _plan_gather_ici(phase), (2, 4)[phase] * len(bufs))
        return _launch("ag%s_d2d" % tag, bufs, _plan_gather_d2d, len(_CHIP_MASKS) * len(bufs))

    def gather_plans(tag, n_bufs, phase):
        if phase < 2:
            return ("ag%s_ici%d" % (tag, phase), _plan_gather_ici(phase), (2, 4)[phase] * n_bufs)
        return ("ag%s_d2d" % tag, _plan_gather_d2d, len(_CHIP_MASKS) * n_bufs)

    bx, by, bc = _AXIS_BIT["x"], _AXIS_BIT["y"], _AXIS_BIT["c"]
    buf_a = [cast(0, w_in[0])]

    c_all = _all_gather_2d("ag_c", c.reshape(d // LANES, LANES), deps=tuple(buf_a)).reshape(N_DEV, d)
    wa = w_ada.shape[2]
    b_shard = lax.dynamic_slice_in_dim(b_ada, me * wa, wa, axis=1)
    mod_part = _ada_forward(c_all, w_ada[0], b_shard)
    mod_all = _all_gather_2d("ag_mod", mod_part.reshape(N_DEV * wa // LANES, LANES))
    mod_all = mod_all.reshape(N_DEV, N_DEV, wa)
    mod = lax.dynamic_slice_in_dim(mod_all, me, 1, axis=1).reshape(1, N_MOD * d)
    shift1, scale1, gate1, shift2, scale2, gate2 = [mod[:, i * d:(i + 1) * d] for i in range(N_MOD)]
    fl_a, tok = _launch("agA_near", buf_a, _plan_neighbours, 2, deps=(mod_all,))
    buf_b = [cast(1, w_pool[0].reshape(-1, cg)), cast(2, w_a_up[0]), cast(3, w_b_up[0]), cast(4, w_o[0])]
    buf_c = [cast(5, w_ff1[0])]
    buf_e = [cast(6, w_ff2[0])]

    h = _norm_forward("norm1_fwd", x2, norm1_w, scale1, shift1, deps=(tok,))
    buf_a = _land(fl_a, [h] + buf_b + buf_c + buf_e)
    near = (0, bx, by)
    (fl_far, fl_near), tok = _launch_groups(
        "agA_far", [(buf_a, [("agA_far", _plan_diagonal, 2), ("agA_near_d2d", _plan_d2d(near), len(near))])])
    (fl_b, fl_c, fl_e), tok = _launch_groups(
        "agBCE_ici0", [(buf_b, [gather_plans("B", len(buf_b), 0)]), (buf_c, [gather_plans("C", 1, 0)]),
                       (buf_e, [gather_plans("E", 1, 0)])])

    tm = _tile(s, 1024)
    tk = _tile(d, 2048)
    te = _tile(d, 512)

    own_slots = jnp.stack([me ^ m for m in near]).astype(jnp.int32)
    far_slots = jnp.stack([me ^ bx ^ by ^ f for f in (0, bc)]).astype(jnp.int32)
    buf_a = fl_near[-1]
    proj = _project_slots("proj_own", h, buf_a[0], own_slots, 4 * d, deps=(tok,))
    buf_a = _land(fl_near, proj, bufs=buf_a)
    proj = _project_slots("proj_sibling", h, buf_a[0], own_slots ^ bc, 4 * d, proj_in=proj)
    buf_a = _land(fl_far, proj, bufs=buf_a)
    fl_a, tok = _launch("agA_far_d2d", buf_a, _plan_d2d((bx | by,)), 1)
    w_in_f, = _land(fl_a, tok)
    proj = _project_slots("proj_far", h, w_in_f, far_slots, 4 * d, proj_in=proj)
    buf_b = _land(fl_b, proj)
    buf_c = _land(fl_c, proj)
    (fl_b, fl_c), tok = _launch_groups(
        "agBC_ici1", [(buf_b, [gather_plans("B", len(buf_b), 1)]), (buf_c, [gather_plans("C", 1, 1)])])

    qn, kn, vb = _qkv_prepare(proj, q_norm_w, k_norm_w, half, deps=(tok,))
    attn = _attention_forward(qn, kn, vb)
    buf_b = _land(fl_b, attn)
    buf_e = _land(fl_e, attn)
    (fl_b, fl_e), tok = _launch_groups(
        "agB_d2d_E_ici1", [(buf_b, [gather_plans("B", len(buf_b), 2)]), (buf_e, [gather_plans("E", 1, 1)])])
    w_pool_f, w_a_f, w_b_f, w_o_f = _land(fl_b, tok)
    rows_pool = cg // N_DEV
    w_pool_f = w_pool_f.reshape(N_DEV, n_groups, rows_pool, cg).transpose(1, 0, 2, 3).reshape(n_groups, cg, cg)
    w_o_f = w_o_f.reshape(d, d)
    pooled, ya_in = _pool_forward(proj, w_pool_f, pool_scale)

    def merge_epilogue(ga_ref, gb_ref, ya, yb, out_refs):
        merged_ref, ya_ref, yb_ref = out_refs
        merged = jax.nn.sigmoid(ga_ref[...]) * ya + jax.nn.sigmoid(gb_ref[...]) * yb
        merged_ref[...] = merged.astype(BF16)
        ya_ref[...] = ya.astype(BF16)
        yb_ref[...] = yb.astype(BF16)

    def up_body(a1_ref, b1_ref, a2_ref, b2_ref, ga_ref, gb_ref, *out_refs):
        merge_epilogue(ga_ref, gb_ref, _dot(a1_ref[...], b1_ref[...]), _dot(a2_ref[...], b2_ref[...]), out_refs)

    ga_blk0 = 2 * d // d8
    gb_blk0 = 3 * d // d8
    a_spec = pl.BlockSpec((tm, half), lambda i, j: (i, 0))
    wup_spec = pl.BlockSpec((None, half, d8), lambda i, j: (j, 0, 0))
    o_blk = pl.BlockSpec((tm, d8), lambda i, j: (i, j))
    merged, y_a, y_b = pl.pallas_call(
        up_body, name="up_merge", grid=(s // tm, N_DEV), out_shape=(_sds((s, d), BF16),) * 3,
        in_specs=[a_spec, wup_spec, a_spec, wup_spec,
                  pl.BlockSpec((tm, d8), lambda i, j: (i, ga_blk0 + j)),
                  pl.BlockSpec((tm, d8), lambda i, j: (i, gb_blk0 + j))],
        out_specs=(o_blk,) * 3, compiler_params=_params("parallel", "parallel"),
    )(ya_in, w_a_f, attn, w_b_f, proj, proj)
    buf_c = _land(fl_c, merged)
    fl_c, tok_c = gather_start("C", buf_c, 2)

    tn = _tile(d, 1024)

    def oproj_epilogue(acc, extra_refs, out_refs):
        x_ref, g_ref = extra_refs
        x1_ref, o_ref = out_refs
        x1_ref[...] = x_ref[...] + g_ref[...] * acc
        o_ref[...] = acc.astype(BF16)

    mn_blk = pl.BlockSpec((tm, tn), lambda i, j, k: (i, j))
    e_blk = pl.BlockSpec((tm, te), lambda i, j, k: (i, j))
    e_vec = pl.BlockSpec((1, te), lambda i, j, k: (0, j))
    x1, o_act = _matmul(
        "oproj", "nn", (s // tm, d // te, d // tk), merged, pl.BlockSpec((tm, tk), lambda i, j, k: (i, k)),
        w_o_f, pl.BlockSpec((tk, te), lambda i, j, k: (k, j)),
        [_sds((s, d), F32), _sds((s, d), BF16)], [e_blk, e_blk], (tm, te),
        epilogue=oproj_epilogue, extras=(x2, gate1), extra_specs=(e_blk, e_vec), deps=(tok_c,))

    h2 = _norm_forward("norm2_fwd", x1, norm2_w, scale2, shift2)
    buf_e = _land(fl_e, h2)
    fl_e, tok_e = gather_start("E", buf_e, 2)
    w_ff1_f, = _land(fl_c, [h2, tok_e])

    def ff1_epilogue(acc, extra_refs, out_refs):
        r = jnp.maximum(acc, 0.0)
        out_refs[0][...] = r.astype(BF16)
        out_refs[1][...] = (r * r).astype(BF16)

    ff_blk = pl.BlockSpec((tm, half), lambda i, j, k: (i, j))
    relu, act = _matmul(
        "ff1", "nn", (s // tm, N_DEV, d // tk), h2, pl.BlockSpec((tm, tk), lambda i, j, k: (i, k)),
        w_ff1_f, pl.BlockSpec((None, tk, half), lambda i, j, k: (j, k, 0)),
        [_sds((s, 4 * d), BF16)] * 2, [ff_blk, ff_blk], (tm, half), epilogue=ff1_epilogue)
    w_ff2_f, = _land(fl_e, act)
    w_ff2_f = w_ff2_f.reshape(4 * d, d)

    def ff2_epilogue(acc, extra_refs, out_refs):
        x1_ref, g_ref, t_ref = extra_refs
        df_ref, dy_ref, sq_ref, dgate_ref = out_refs
        gate = g_ref[...]
        err = x1_ref[...] + gate * acc - t_ref[...]
        dyv = err * (1.0 / d)
        dy_ref[...] = dyv
        df_ref[...] = (dyv * gate).astype(BF16)
        sq_ref[...] = jnp.full(sq_ref.shape, jnp.sum(err * err), F32)
        dgate_ref[...] = jnp.broadcast_to(jnp.sum(dyv * acc, axis=0, keepdims=True), dgate_ref.shape)

    df, dy, sq, dgate2_parts = _matmul(
        "ff2", "nn", (s // tm, d // te, 2 * d // tk), act, pl.BlockSpec((tm, 2 * tk), lambda i, j, k: (i, k)),
        w_ff2_f, pl.BlockSpec((2 * tk, te), lambda i, j, k: (k, j)),
        [_sds((s, d), BF16), _sds((s, d), F32), _sds((s // tm * SUBLANES, d // te * LANES), F32),
         _sds((s // tm * SUBLANES, d), F32)],
        [e_blk, e_blk, pl.BlockSpec((SUBLANES, LANES), lambda i, j, k: (i, j)),
         pl.BlockSpec((SUBLANES, te), lambda i, j, k: (i, j))], (tm, te),
        epilogue=ff2_epilogue, extras=(x1, gate2, target), extra_specs=(e_blk, e_vec, e_blk))
    loss_local = (0.5 / d) * jnp.sum(sq[::SUBLANES, ::LANES])
    dgate2 = jnp.sum(dgate2_parts[::SUBLANES], axis=0, keepdims=True)

    tok_k = _tile(s, 2048)
    tw = _tile(d, 1024)
    g_ff2 = _matmul(
        "g_ff2", "tn", (4 * d // tw, d // tn, s // tok_k), act, pl.BlockSpec((tok_k, tw), lambda i, j, k: (k, i)),
        df, pl.BlockSpec((tok_k, tn), lambda i, j, k: (k, j)),
        [_sds((4 * d, d), BF16)], [pl.BlockSpec((tw, tn), lambda i, j, k: (i, j))], (tw, tn))[0]

    def da_epilogue(acc, extra_refs, out_refs):
        out_refs[0][...] = (acc * (2.0 * extra_refs[0][...].astype(F32))).astype(BF16)

    big_blk = pl.BlockSpec((tm, tn), lambda i, j, k: (i, j))
    fl_f2, tok = _reduce_scatter_start("F2", [g_ff2.reshape(N_DEV, half, d)])
    df1 = _matmul(
        "da_ff", "nt", (s // tm, 4 * d // tn, d // tk), df, pl.BlockSpec((tm, tk), lambda i, j, k: (i, k)),
        w_ff2_f, pl.BlockSpec((tn, tk), lambda i, j, k: (j, k)),
        [_sds((s, 4 * d), BF16)], [big_blk], (tm, tn),
        epilogue=da_epilogue, extras=(relu,), extra_specs=(big_blk,), deps=(tok,))[0]

    fl_f2, tok = _reduce_scatter_middle("F2", fl_f2, df1, me)
    g_ff1 = _matmul(
        "g_ff1", "tn", (d // tw, N_DEV, s // tok_k), h2, pl.BlockSpec((tok_k, tw), lambda i, j, k: (k, i)),
        df1, pl.BlockSpec((tok_k, half), lambda i, j, k: (k, j)),
        [_sds((N_DEV, d, half), BF16)], [pl.BlockSpec((None, tw, half), lambda i, j, k: (j, i, 0))], (tw, half),
        deps=(tok,))[0]

    fl_f1, tok = _reduce_scatter_start("F1", [g_ff1])
    dh2 = _matmul(
        "dh2", "nt", (s // tm, d // tn, N_DEV // 2), df1, pl.BlockSpec((tm, 2 * half), lambda i, j, k: (i, k)),
        w_ff1_f, pl.BlockSpec((2, tn, half), lambda i, j, k: (k, j, 0)),
        [_sds((s, d), F32)], [mn_blk], (tm, tn), deps=(tok,))[0]

    sum_ff2, = _reduce_scatter_finish(fl_f2, dh2)
    fl_f1, tok = _reduce_scatter_middle("F1", fl_f1, dh2, me)
    dx1, dshift2, dscale2, g_norm2, do, dgate1 = _norm_backward(
        "norm2_bwd", dh2, x1, norm2_w, scale2, dy, gated=(gate1, o_act), deps=(tok,))

    g_o = _matmul(
        "g_o", "tn", (d // tw, d // tn, s // tok_k), merged, pl.BlockSpec((tok_k, tw), lambda i, j, k: (k, i)),
        do, pl.BlockSpec((tok_k, tn), lambda i, j, k: (k, j)),
        [_sds((d, d), BF16)], [pl.BlockSpec((tw, tn), lambda i, j, k: (i, j))], (tw, tn))[0]

    def merge_bwd_epilogue(acc, extra_refs, out_refs):
        ga_ref, gb_ref, ya_ref, yb_ref = extra_refs
        dya_ref, dyb_ref, dga_ref, dgb_ref = out_refs
        sa = jax.nn.sigmoid(ga_ref[...])
        sb = jax.nn.sigmoid(gb_ref[...])
        dya_ref[...] = (acc * sa).astype(BF16)
        dyb_ref[...] = (acc * sb).astype(BF16)
        dga_ref[...] = (acc * ya_ref[...].astype(F32) * (sa * (1.0 - sa))).astype(BF16)
        dgb_ref[...] = (acc * yb_ref[...].astype(F32) * (sb * (1.0 - sb))).astype(BF16)

    nb = d // te
    dy_a, dy_b, dproj, dg_b = _matmul(
        "dmerged", "nt", (s // tm, nb, d // tk), do, pl.BlockSpec((tm, tk), lambda i, j, k: (i, k)),
        w_o_f, pl.BlockSpec((te, tk), lambda i, j, k: (j, k)),
        [_sds((s, d), BF16), _sds((s, d), BF16), _sds((s, 4 * d), BF16), _sds((s, d), BF16)],
        [e_blk, e_blk, pl.BlockSpec((tm, te), lambda i, j, k: (i, 2 * nb + j)), e_blk], (tm, te),
        epilogue=merge_bwd_epilogue, extras=(proj, proj, y_a, y_b),
        extra_specs=(pl.BlockSpec((tm, te), lambda i, j, k: (i, 2 * nb + j)),
                     pl.BlockSpec((tm, te), lambda i, j, k: (i, 3 * nb + j)), e_blk, e_blk))
    dproj = _place_columns("place_dgb", dg_b, dproj, 3)

    up_a = pl.BlockSpec((tok_k, half), lambda i, j, k: (k, 0))
    up_b = pl.BlockSpec((tok_k, d8), lambda i, j, k: (k, j))
    up_o = pl.BlockSpec((None, half, d8), lambda i, j, k: (j, 0, 0))
    g_a_up = _matmul("g_a_up", "tn", (1, N_DEV, s // tok_k), ya_in, up_a, dy_a, up_b,
                     [_sds((N_DEV, half, d8), BF16)], [up_o], (half, d8))[0]
    g_b_up = _matmul("g_b_up", "tn", (1, N_DEV, s // tok_k), attn, up_a, dy_b, up_b,
                     [_sds((N_DEV, half, d8), BF16)], [up_o], (half, d8))[0]
    dn_a = pl.BlockSpec((tm, d8), lambda i, j, k: (i, k))
    dn_b = pl.BlockSpec((None, half, d8), lambda i, j, k: (k, 0, 0))
    dn_o = pl.BlockSpec((tm, half), lambda i, j, k: (i, 0))
    dya_in = _matmul("d_ya_in", "nt", (s // tm, 1, N_DEV), dy_a, dn_a, w_a_f, dn_b,
                     [_sds((s, half), BF16)], [dn_o], (tm, half))[0]
    dattn = _matmul("d_attn", "nt", (s // tm, 1, N_DEV), dy_b, dn_a, w_b_f, dn_b,
                    [_sds((s, half), BF16)], [dn_o], (tm, half))[0]

    dproj, g_pool, g_pool_scale = _pool_backward(dya_in, pooled, w_pool_f, pool_scale, dproj)
    sum_ff1, = _reduce_scatter_finish(fl_f1, g_pool)
    g_pool_send = g_pool.astype(BF16).reshape(n_groups, N_DEV, rows_pool, cg).transpose(1, 0, 2, 3)
    g_pool_send = g_pool_send.reshape(N_DEV, n_groups * rows_pool, cg)
    fl_b, tok = _reduce_scatter_start("B", [g_pool_send, g_a_up, g_b_up, g_o.reshape(N_DEV, d8, d)])
    dqn, dkn, dproj = _attention_backward(qn, kn, vb, dattn, dproj, 3, deps=(tok,))
    fl_b, tok = _reduce_scatter_middle("B", fl_b, dqn, me)
    dproj, g_qnorm = _qk_norm_backward("qnorm_bwd", dqn, proj, 1, q_norm_w, dproj, half, deps=(tok,))
    dproj, g_knorm = _qk_norm_backward("knorm_bwd", dkn, proj, 2, k_norm_w, dproj, half)

    g_in = _matmul(
        "g_in", "tn", (d // tw, N_DEV, s // tok_k), h, pl.BlockSpec((tok_k, tw), lambda i, j, k: (k, i)),
        dproj, pl.BlockSpec((tok_k, half), lambda i, j, k: (k, j)),
        [_sds((N_DEV, d, half), BF16)], [pl.BlockSpec((None, tw, half), lambda i, j, k: (j, i, 0))], (tw, half))[0]
    fl_in, tok = _reduce_scatter_start("I", [g_in])
    dh = _matmul(
        "dh", "nt", (s // tm, d // tn, N_DEV // 2), dproj, pl.BlockSpec((tm, 2 * half), lambda i, j, k: (i, k)),
        w_in_f, pl.BlockSpec((2, tn, half), lambda i, j, k: (k, j, 0)),
        [_sds((s, d), F32)], [mn_blk], (tm, tn), deps=(tok,))[0]
    sum_pool, sum_a_up, sum_b_up, sum_o = _reduce_scatter_finish(fl_b, dh)
    grad_x, dshift1, dscale1, g_norm1 = _norm_backward("norm1_bwd", dh, x2, norm1_w, scale1, dx1)

    dmod = jnp.concatenate([dshift1, dscale1, dgate1, dshift2, dscale2, dgate2], axis=1)
    pieces = [dmod, g_norm1, g_norm2, g_pool_scale, g_qnorm, g_knorm, jnp.full((1, LANES), loss_local, F32)]
    packed_rows = [_rows_of_lanes(p) for p in pieces]
    offsets = [0]
    for p in packed_rows:
        offsets.append(offsets[-1] + p.shape[0])
    packed = jnp.concatenate(packed_rows, axis=0)
    small_all = _all_gather_2d("ag_small", packed)
    fl_in, tok = _reduce_scatter_middle("I", fl_in, small_all, me)
    small_sum = _sum_slots("small_sum", small_all[None], F32)[0]

    def unpack(i, width):
        return small_sum[offsets[i]:offsets[i] + width // LANES].reshape(1, width)

    g_b_ada = unpack(0, N_MOD * d)
    g_norm1_w = unpack(1, d)
    g_norm2_w = unpack(2, d)
    g_pool_scale_w = unpack(3, half)
    g_q_norm_w = unpack(4, HEAD_DIM)
    g_k_norm_w = unpack(5, HEAD_DIM)
    loss = unpack(6, LANES)[0, 0]
    dmod_all = small_all[:, :N_MOD * d // LANES].reshape(N_DEV, N_MOD * d)
    dmod_cols = lax.dynamic_slice_in_dim(dmod_all, me * wa, wa, axis=1)
    g_w_ada = _ada_weight_grad(c_all, dmod_cols, deps=(tok,))[None]


    grads = {
        "w_ada": g_w_ada, "b_ada": g_b_ada, "norm1_w": g_norm1_w,
        "q_norm_w": g_q_norm_w, "k_norm_w": g_k_norm_w,
        "pool_scale": g_pool_scale_w, "norm2_w": g_norm2_w,
    }
    sums = {"w_pool": sum_pool, "w_a_up": sum_a_up, "w_b_up": sum_b_up, "w_o": sum_o,
            "w_ff1": sum_ff1, "w_ff2": sum_ff2}
    weights = {"w_ada": (w_ada, m_w_ada, v_w_ada), "b_ada": (b_ada, m_b_ada, v_b_ada),
               "norm1_w": (norm1_w, m_norm1_w, v_norm1_w), "w_in": (w_in, m_w_in, v_w_in),
               "q_norm_w": (q_norm_w, m_q_norm_w, v_q_norm_w), "k_norm_w": (k_norm_w, m_k_norm_w, v_k_norm_w),
               "w_pool": (w_pool, m_w_pool, v_w_pool), "pool_scale": (pool_scale, m_pool_scale, v_pool_scale),
               "w_a_up": (w_a_up, m_w_a_up, v_w_a_up), "w_b_up": (w_b_up, m_w_b_up, v_w_b_up),
               "w_o": (w_o, m_w_o, v_w_o), "norm2_w": (norm2_w, m_norm2_w, v_norm2_w),
               "w_ff1": (w_ff1, m_w_ff1, v_w_ff1), "w_ff2": (w_ff2, m_w_ff2, v_w_ff2)}
    order = list(weights)
    deltas, new_m, new_v = {}, {}, {}
    def adam(name):
        wt, mt, vt = weights[name]
        shape = wt.shape
        flat = (-1, shape[-1])
        if name in sums:
            own, received = sums[name]
            g, dl, nm, nv = _adamw_summed("adamw_" + name, wt.reshape(flat), own, received,
                                          mt.reshape(flat), vt.reshape(flat))
            grads[name] = g.reshape(shape)
        else:
            dl, nm, nv = _adamw("adamw_" + name, wt.reshape(flat), grads[name].reshape(flat),
                                mt.reshape(flat), vt.reshape(flat))
        deltas[name], new_m[name], new_v[name] = dl.reshape(shape), nm.reshape(shape), nv.reshape(shape)

    others = [n for n in order if n != "w_in"]
    for name in others:
        adam(name)
    sums["w_in"], = _reduce_scatter_finish(fl_in, [deltas[n] for n in others])
    adam("w_in")

    return (loss, grad_x[None], *[grads[n] for n in order], *[deltas[n] for n in order],
            *[new_m[n] for n in order], *[new_v[n] for n in order])
```

```python
import math

import jax
import jax.numpy as jnp
from jax import lax
from jax.experimental import pallas as pl
from jax.experimental.pallas import tpu as pltpu

F32 = jnp.float32
BF16 = jnp.bfloat16
MESH_AXES = ("x", "y", "c")
N_DEV = 8
HEAD_DIM = 128
POOL_WINDOWS = (2, 4, 8, 16)
N_MOD = 6
NORM_EPS = 1e-6
LANES = 128
SUBLANES = 8
VMEM_LIMIT_BYTES = 56 * 1024 * 1024
Q_TILE = 256
K_TILE = 256
POOL_TILE = 512
HEADS_PER_STEP = 8
HEADS_PER_STEP_BWD = 4

ADAM_LR = 0.001
ADAM_B1 = 0.9
ADAM_B2 = 0.999
ADAM_EPS = 1e-08
ADAM_WD = 0.01
ADAM_STEP = 10

_NN = (((1,), (0,)), ((), ()))
_NT = (((1,), (1,)), ((), ()))
_TN = (((0,), (0,)), ((), ()))
_DIMS = {"nn": _NN, "nt": _NT, "tn": _TN}


def _dot(a, b, mode="nn"):
    return lax.dot_general(a, b, _DIMS[mode], preferred_element_type=F32)


def _params(*sem):
    return pltpu.CompilerParams(dimension_semantics=sem, vmem_limit_bytes=VMEM_LIMIT_BYTES)


def _tile(dim, pref, align=SUBLANES):
    for t in range(min(dim, pref), 0, -1):
        if dim % t == 0 and t % align == 0:
            return t
    return dim


def _group_index(axes):
    idx = 0
    for a in axes:
        idx = idx * 2 + lax.axis_index(a)
    return idx


def _peer_device(axes, k):
    coords = {a: lax.axis_index(a) for a in MESH_AXES}
    for pos, a in enumerate(axes):
        if (k >> (len(axes) - 1 - pos)) & 1:
            coords[a] = 1 - coords[a]
    return tuple(coords[a] for a in MESH_AXES)


_AXIS_BIT = {"x": 4, "y": 2, "c": 1}
_ANY = pl.BlockSpec(memory_space=pl.ANY)


def _device_xor(mask):
    return tuple(1 - lax.axis_index(a) if mask & _AXIS_BIT[a] else lax.axis_index(a) for a in MESH_AXES)


def _remote(src, dst, send_sem, recv_sem, mask):
    return pltpu.make_async_remote_copy(src_ref=src, dst_ref=dst, send_sem=send_sem, recv_sem=recv_sem,
                                        device_id=_device_xor(mask), device_id_type=pl.DeviceIdType.MESH)


_CHIP_MASKS = (0, _AXIS_BIT["y"], _AXIS_BIT["x"], _AXIS_BIT["x"] | _AXIS_BIT["y"])


def _add_received(name, own, own_slots, received, out_dtype):
    nj, r, c = received.shape
    tr = _tile(r, max(2 * SUBLANES, (1 << 20) // c), 2 * SUBLANES)

    def body(slots_ref, own_ref, rec_ref, o_ref):
        del slots_ref
        o_ref[...] = (own_ref[...].astype(F32) + rec_ref[...].astype(F32)).astype(o_ref.dtype)

    grid_spec = pltpu.PrefetchScalarGridSpec(
        num_scalar_prefetch=1, grid=(nj, r // tr),
        in_specs=[pl.BlockSpec((None, tr, c), lambda j, i, slots: (slots[j], i, 0)),
                  pl.BlockSpec((None, tr, c), lambda j, i, slots: (j, i, 0))],
        out_specs=pl.BlockSpec((None, tr, c), lambda j, i, slots: (j, i, 0)))
    return pl.pallas_call(
        body, name=name, grid_spec=grid_spec, out_shape=jax.ShapeDtypeStruct((nj, r, c), out_dtype),
        compiler_params=_params("parallel", "parallel"),
    )(own_slots, own, received)


_HBM = pl.BlockSpec(memory_space=pltpu.HBM)
_SEM = pl.BlockSpec(memory_space=pltpu.SEMAPHORE)
_DATAFLOW = pltpu.SideEffectType.DATAFLOW_SIDE_EFFECTING


def _launch_groups(name, groups, deps=()):
    bufs = [b for g_bufs, _ in groups for b in g_bufs]
    specs = [(len(g_bufs), spec) for g_bufs, g_plans in groups for spec in g_plans]
    nb, ns = len(bufs), len(specs)

    def body(*refs):
        sems = refs[nb + len(deps):nb + len(deps) + 2 * ns]
        me = _group_index(MESH_AXES)
        first, which = 0, 0
        for g_bufs, g_plans in groups:
            ins = refs[first:first + len(g_bufs)]
            for _, plan, n_copies in g_plans:
                copies = plan(ins, me)
                assert len(copies) == n_copies
                for n, (src, dst, mask) in enumerate(copies):
                    _remote(src, dst, sems[2 * which].at[n], sems[2 * which + 1].at[n], mask).start()
                which += 1
            first += len(g_bufs)
        refs[-1][...] = jnp.zeros_like(refs[-1])

    sem_shapes = [pltpu.SemaphoreType.DMA((n,)) for _, (_, _, n) in specs for _ in range(2)]
    outs = pl.pallas_call(
        body, name=name,
        out_shape=(*sem_shapes, *[pltpu.HBM(b.shape, b.dtype) for b in bufs],
                   jax.ShapeDtypeStruct((SUBLANES, LANES), F32)),
        in_specs=[_HBM] * nb + [_ANY] * len(deps),
        out_specs=(*[_SEM] * (2 * ns), *[_HBM] * nb, pl.BlockSpec(memory_space=pltpu.VMEM)),
        input_output_aliases={i: 2 * ns + i for i in range(nb)},
        compiler_params=pltpu.CompilerParams(has_side_effects=_DATAFLOW),
    )(*[pltpu.with_memory_space_constraint(b, pltpu.HBM) for b in bufs], *deps)
    flights, first, which = [], 0, 0
    for g_bufs, g_plans in groups:
        through = list(outs[2 * ns + first:2 * ns + first + len(g_bufs)])
        for land_name, plan, n_copies in g_plans:
            flights.append((land_name, plan, n_copies, outs[2 * which], outs[2 * which + 1], through))
            which += 1
        first += len(g_bufs)
    return flights, outs[-1]


def _launch(name, bufs, plan, n_copies, deps=()):
    (flight,), token = _launch_groups(name, [(bufs, [(name, plan, n_copies)])], deps)
    return flight, token


def _land(flight, after, bufs=None):
    name, plan, n_copies, send_sems, recv_sems, launched = flight
    bufs = launched if bufs is None else bufs
    nb = len(bufs)
    after = list(after) if isinstance(after, (list, tuple)) else [after]

    def body(*refs):
        ins = refs[:nb]
        s_sems, r_sems = refs[nb], refs[nb + 1]
        for n, (src, dst, mask) in enumerate(plan(ins, _group_index(MESH_AXES))):
            cp = _remote(src, dst, s_sems.at[n], r_sems.at[n], mask)
            cp.wait_send()
            cp.wait_recv()

    outs = pl.pallas_call(
        body, name=name + "_land",
        out_shape=tuple(pltpu.HBM(b.shape, b.dtype) for b in bufs),
        in_specs=[_HBM] * nb + [_SEM, _SEM] + [_ANY] * len(after), out_specs=tuple([_HBM] * nb),
        input_output_aliases={i: i for i in range(nb)},
        compiler_params=pltpu.CompilerParams(has_side_effects=_DATAFLOW),
    )(*bufs, send_sems, recv_sems, *after)
    return list(outs)


def _plan_gather_ici(phase):
    bx, by = _AXIS_BIT["x"], _AXIS_BIT["y"]

    def plan(refs, me):
        copies = []
        for ref in refs:
            half = ref.shape[1] // 2

            def piece(slot, color, mask, ref=ref, half=half):
                p = ref.at[slot, pl.ds(color * half, half)]
                return (p, p, mask)

            if phase == 0:
                copies += [piece(me, 0, bx), piece(me, 1, by)]
            else:
                copies += [piece(me, 0, by), piece(me ^ bx, 0, by), piece(me, 1, bx), piece(me ^ by, 1, bx)]
        return copies

    return plan


def _plan_d2d(masks):
    def plan(refs, me):
        return [(ref.at[me ^ m], ref.at[me ^ m], _AXIS_BIT["c"]) for ref in refs for m in masks]

    return plan


def _plan_gather_d2d(refs, me):
    return _plan_d2d(_CHIP_MASKS)(refs, me)


def _plan_neighbours(refs, me):
    return [(ref.at[me], ref.at[me], _AXIS_BIT[a]) for ref in refs for a in ("x", "y")]


def _plan_diagonal(refs, me):
    bx, by = _AXIS_BIT["x"], _AXIS_BIT["y"]
    copies = []
    for ref in refs:
        half = ref.shape[1] // 2
        lo = ref.at[me ^ bx, pl.ds(0, half)]
        hi = ref.at[me ^ by, pl.ds(half, half)]
        copies += [(lo, lo, by), (hi, hi, bx)]
    return copies


def _plan_scatter_d2d(refs, me):
    na = len(refs) // 2
    copies = []
    for a in range(na):
        for j, m in enumerate(_CHIP_MASKS):
            copies.append((refs[a].at[me ^ _AXIS_BIT["c"] ^ m], refs[na + a].at[j], _AXIS_BIT["c"]))
    return copies


def _plan_scatter_ici(refs, me):
    del me
    na = len(refs) // 2
    copies = []
    for a in range(na):
        for n, m in enumerate(_CHIP_MASKS[1:]):
            copies.append((refs[a].at[n + 1], refs[na + a].at[n], m))
    return copies


def _with_deps(body, n_in, deps):
    if not deps:
        return body

    def wrapped(*refs):
        return body(*refs[:n_in], *refs[n_in + len(deps):])

    return wrapped


def _reduce_scatter_start(tag, grads):
    lands = [lax.empty((len(_CHIP_MASKS),) + g.shape[1:], g.dtype) for g in grads]
    return _launch("rs%s_d2d" % tag, list(grads) + lands, _plan_scatter_d2d, len(_CHIP_MASKS) * len(grads))


def _reduce_scatter_middle(tag, flight, after, me):
    bufs = _land(flight, after)
    na = len(bufs) // 2
    own_slots = jnp.stack([me ^ m for m in _CHIP_MASKS]).astype(jnp.int32)
    sums = [_add_received("rs%s_add_d2d_%d" % (tag, a), bufs[a], own_slots, bufs[na + a], BF16) for a in range(na)]
    lands = [lax.empty((len(_CHIP_MASKS) - 1,) + h.shape[1:], h.dtype) for h in sums]
    return _launch("rs%s_ici" % tag, sums + lands, _plan_scatter_ici, (len(_CHIP_MASKS) - 1) * na)


def _reduce_scatter_finish(flight, after):
    bufs = _land(flight, after)
    na = len(bufs) // 2
    return [(bufs[a], bufs[na + a]) for a in range(na)]


def _all_gather_2d(name, x, deps=()):
    r, c = x.shape

    def body(x_ref, out_ref, send_sems, recv_sems):
        me = _group_index(MESH_AXES)
        out_ref[me] = x_ref[...]
        copies = []
        for k in range(1, N_DEV):
            cp = pltpu.make_async_remote_copy(
                src_ref=x_ref, dst_ref=out_ref.at[me],
                send_sem=send_sems.at[k - 1], recv_sem=recv_sems.at[k - 1],
                device_id=_peer_device(MESH_AXES, k), device_id_type=pl.DeviceIdType.MESH)
            cp.start()
            copies.append(cp)
        for cp in copies:
            cp.wait()

    vmem = pl.BlockSpec(memory_space=pltpu.VMEM)
    return pl.pallas_call(
        _with_deps(body, 1, deps), name=name, out_shape=jax.ShapeDtypeStruct((N_DEV, r, c), x.dtype),
        in_specs=[vmem] + [_ANY] * len(deps), out_specs=vmem,
        scratch_shapes=[pltpu.SemaphoreType.DMA((N_DEV - 1,)), pltpu.SemaphoreType.DMA((N_DEV - 1,))],
    )(x, *deps)


def _sum_slots(name, buf, out_dtype):
    pre, n, r, c = buf.shape
    tr = _tile(r, max(SUBLANES * 2, (1 << 20) // c))

    def body(b_ref, o_ref):
        acc = b_ref[0].astype(F32)
        for q in range(1, n):
            acc = acc + b_ref[q].astype(F32)
        o_ref[...] = acc.astype(o_ref.dtype)

    return pl.pallas_call(
        body, name=name, grid=(pre, r // tr),
        out_shape=jax.ShapeDtypeStruct((pre, r, c), out_dtype),
        in_specs=[pl.BlockSpec((None, n, tr, c), lambda i, j: (i, 0, j, 0))],
        out_specs=pl.BlockSpec((None, tr, c), lambda i, j: (i, j, 0)),
        compiler_params=_params("parallel", "parallel"),
    )(buf)


def _matmul(name, mode, grid, a, a_spec, b, b_spec, out_shapes, out_specs, acc_shape,
            epilogue=None, extras=(), extra_specs=(), aliases=None, deps=()):
    nk = grid[2]
    n_extra = len(extras)
    n_out = len(out_shapes)

    def finish(acc, extra_refs, out_refs):
        if epilogue is None:
            out_refs[0][...] = acc.astype(out_refs[0].dtype)
        else:
            epilogue(acc, extra_refs, out_refs)

    def product(a_ref, b_ref):
        if len(b_ref.shape) == 2:
            return _dot(a_ref[...], b_ref[...], mode)
        width = a_ref.shape[1] // b_ref.shape[0]
        total = None
        for i in range(b_ref.shape[0]):
            part = _dot(a_ref[:, i * width:(i + 1) * width], b_ref[i], mode)
            total = part if total is None else total + part
        return total

    def body(*refs):
        a_ref, b_ref = refs[0], refs[1]
        extra_refs = refs[2:2 + n_extra]
        out_refs = refs[2 + n_extra:2 + n_extra + n_out]
        if nk == 1:
            finish(product(a_ref, b_ref), extra_refs, out_refs)
            return
        acc_ref = refs[-1]
        k = pl.program_id(2)

        @pl.when(k == 0)
        def _():
            acc_ref[...] = product(a_ref, b_ref)

        @pl.when((k > 0) & (k < nk - 1))
        def _():
            acc_ref[...] += product(a_ref, b_ref)

        @pl.when(k == nk - 1)
        def _():
            finish(acc_ref[...] + product(a_ref, b_ref), extra_refs, out_refs)

    scratch = [] if nk == 1 else [pltpu.VMEM(acc_shape, F32)]
    return pl.pallas_call(
        _with_deps(body, 2 + n_extra, deps), name=name, grid=grid, out_shape=tuple(out_shapes),
        in_specs=[a_spec, b_spec] + list(extra_specs) + [_ANY] * len(deps), out_specs=tuple(out_specs),
        scratch_shapes=scratch, input_output_aliases=aliases or {},
        compiler_params=_params("parallel", "parallel", "arbitrary"),
    )(a, b, *extras, *deps)


def _sds(shape, dtype):
    return jax.ShapeDtypeStruct(tuple(shape), dtype)


def _project_slots(name, h, w_full, slots, out_cols, proj_in=None, deps=()):
    s, d = h.shape
    _, _, wide = w_full.shape
    tm = _tile(s, 1024)
    n_in = 4 if proj_in is not None else 3

    def body(*refs):
        refs[-1][...] = _dot(refs[1][...], refs[2][...])

    grid_spec = pltpu.PrefetchScalarGridSpec(
        num_scalar_prefetch=1, grid=(s // tm, slots.shape[0]),
        in_specs=[pl.BlockSpec((tm, d), lambda i, j, sl: (i, 0)),
                  pl.BlockSpec((None, d, wide), lambda i, j, sl: (sl[j], 0, 0))]
        + [_ANY] * (n_in - 3 + len(deps)),
        out_specs=pl.BlockSpec((tm, wide), lambda i, j, sl: (i, sl[j])))
    extra = ([proj_in] if proj_in is not None else []) + list(deps)
    return pl.pallas_call(
        body, name=name, grid_spec=grid_spec, out_shape=_sds((s, out_cols), F32),
        input_output_aliases={3: 0} if proj_in is not None else {},
        compiler_params=_params("parallel", "arbitrary"),
    )(slots, h, w_full, *extra)


def _ada_forward(c_all, w_ada, b_shard):
    nb, d = c_all.shape
    w = w_ada.shape[1]
    tn = _tile(w, 512)

    def body(c_ref, w_ref, b_ref, o_ref):
        cv = c_ref[...]
        sc = cv * jax.nn.sigmoid(cv)
        o_ref[...] = jnp.dot(sc, w_ref[...], precision=lax.Precision.HIGHEST,
                             preferred_element_type=F32) + b_ref[...]

    return pl.pallas_call(
        body, name="ada_fwd", grid=(w // tn,), out_shape=_sds((nb, w), F32),
        in_specs=[pl.BlockSpec((nb, d), lambda j: (0, 0)), pl.BlockSpec((d, tn), lambda j: (0, j)),
                  pl.BlockSpec((1, tn), lambda j: (0, j))],
        out_specs=pl.BlockSpec((nb, tn), lambda j: (0, j)),
        compiler_params=_params("parallel"),
    )(c_all, w_ada, b_shard)


def _ada_weight_grad(c_all, dmod_cols, deps=()):
    nb, d = c_all.shape
    w = dmod_cols.shape[1]
    tn = _tile(w, 512)

    def body(c_ref, g_ref, o_ref):
        cv = c_ref[...]
        sc = cv * jax.nn.sigmoid(cv)
        o_ref[...] = lax.dot_general(sc, g_ref[...], _TN, precision=lax.Precision.HIGHEST,
                                     preferred_element_type=F32)

    return pl.pallas_call(
        _with_deps(body, 2, deps), name="ada_wgrad", grid=(w // tn,), out_shape=_sds((d, w), F32),
        in_specs=[pl.BlockSpec((nb, d), lambda j: (0, 0)), pl.BlockSpec((nb, tn), lambda j: (0, j))]
        + [_ANY] * len(deps),
        out_specs=pl.BlockSpec((d, tn), lambda j: (0, j)),
        compiler_params=_params("parallel"),
    )(c_all, dmod_cols, *deps)


def _norm_forward(name, x, norm_w, scale, shift, deps=()):
    s, d = x.shape
    tm = _tile(s, 256)

    def body(x_ref, w_ref, sc_ref, sh_ref, h_ref):
        xv = x_ref[...]
        r = lax.rsqrt(jnp.mean(xv * xv, axis=-1, keepdims=True) + NORM_EPS)
        h = (xv * r * w_ref[...]) * (1.0 + sc_ref[...]) + sh_ref[...]
        h_ref[...] = h.astype(BF16)

    vec = pl.BlockSpec((1, d), lambda i: (0, 0))
    row = pl.BlockSpec((tm, d), lambda i: (i, 0))
    return pl.pallas_call(
        _with_deps(body, 4, deps), name=name, grid=(s // tm,), out_shape=_sds((s, d), BF16),
        in_specs=[row, vec, vec, vec] + [_ANY] * len(deps), out_specs=row, compiler_params=_params("parallel"),
    )(x, norm_w, scale, shift, *deps)


def _norm_backward(name, dh, x, norm_w, scale, dres, gated=None, deps=()):
    s, d = x.shape
    tm = _tile(s, 256)
    n_in = 7 if gated else 5

    def body(*refs):
        dh_ref, x_ref, w_ref, sc_ref, dres_ref = refs[:5]
        dx_ref, dshift_ref, dscale_ref, dw_ref = refs[n_in:n_in + 4]
        sums = (dshift_ref, dscale_ref, dw_ref) + ((refs[n_in + 5],) if gated else ())

        @pl.when(pl.program_id(0) == 0)
        def _():
            for ref in sums:
                ref[...] = jnp.zeros_like(ref)

        xv = x_ref[...]
        g = dh_ref[...]
        r = lax.rsqrt(jnp.mean(xv * xv, axis=-1, keepdims=True) + NORM_EPS)
        n = xv * r
        gain = 1.0 + sc_ref[...]
        gn = g * n
        dshift_ref[...] += jnp.sum(g, axis=0, keepdims=True)
        dscale_ref[...] += jnp.sum(gn, axis=0, keepdims=True) * w_ref[...]
        dw_ref[...] += jnp.sum(gn, axis=0, keepdims=True) * gain
        dn = g * (w_ref[...] * gain)
        dx = dres_ref[...] + r * (dn - n * jnp.mean(dn * n, axis=-1, keepdims=True))
        dx_ref[...] = dx
        if gated:
            gate_ref, other_ref = refs[5:7]
            refs[n_in + 4][...] = (dx * gate_ref[...]).astype(BF16)
            refs[n_in + 5][...] += jnp.sum(dx * other_ref[...].astype(F32), axis=0, keepdims=True)

    vec = pl.BlockSpec((1, d), lambda i: (0, 0))
    row = pl.BlockSpec((tm, d), lambda i: (i, 0))
    vec_out = _sds((1, d), F32)
    return pl.pallas_call(
        _with_deps(body, n_in, deps), name=name, grid=(s // tm,),
        out_shape=(_sds((s, d), F32), vec_out, vec_out, vec_out) + ((_sds((s, d), BF16), vec_out) if gated else ()),
        in_specs=[row, row, vec, vec, row] + ([vec, row] if gated else []) + [_ANY] * len(deps),
        out_specs=(row, vec, vec, vec) + ((row, vec) if gated else ()),
        compiler_params=_params("arbitrary"),
    )(dh, x, norm_w, scale, dres, *(gated or ()), *deps)


def _split_bf16(v):
    hi = v.astype(BF16)
    lo = (v - hi.astype(F32)).astype(BF16)
    return hi, lo


def _pool_forward(proj, w_pool, pool_scale, deps=()):
    s = proj.shape[0]
    g_n, cg, _ = w_pool.shape
    t = POOL_TILE
    nt = s // t

    def body(cur_ref, prev_ref, wp_ref, sc_ref, pooled_ref, ya_ref):
        g = pl.program_id(0)
        ti = pl.program_id(1)
        win = jnp.left_shift(2, g)
        row = lax.broadcasted_iota(jnp.int32, (t, t), 0)
        col = lax.broadcasted_iota(jnp.int32, (t, t), 1)
        lag = row - col
        band_cur = ((lag >= 0) & (lag < win)).astype(BF16)
        band_prev = ((lag + t < win) & (ti > 0)).astype(BF16)
        u = cur_ref[...]
        u_hi, u_lo = _split_bf16(u)
        p_hi, p_lo = _split_bf16(prev_ref[...])
        wsum = (_dot(band_cur, u_hi) + _dot(band_cur, u_lo)
                + _dot(band_prev, p_hi) + _dot(band_prev, p_lo))
        tok = ti * t + lax.broadcasted_iota(jnp.int32, (t, 1), 0)
        count = jnp.minimum(tok + 1, win).astype(F32)
        pooled = (wsum / count - u).astype(BF16)
        pooled_ref[...] = pooled
        ya_ref[...] = (_dot(pooled, wp_ref[...]) * sc_ref[...]).astype(BF16)

    blk = pl.BlockSpec((t, cg), lambda g, i: (i, g))
    return pl.pallas_call(
        _with_deps(body, 4, deps), name="pool_fwd", grid=(g_n, nt),
        out_shape=(_sds((s, g_n * cg), BF16), _sds((s, g_n * cg), BF16)),
        in_specs=[blk, pl.BlockSpec((t, cg), lambda g, i: (jnp.maximum(i - 1, 0), g)),
                  pl.BlockSpec((None, cg, cg), lambda g, i: (g, 0, 0)),
                  pl.BlockSpec((1, cg), lambda g, i: (0, g))] + [_ANY] * len(deps),
        out_specs=(blk, blk), compiler_params=_params("parallel", "parallel"),
    )(proj, proj, w_pool, pool_scale, *deps)


def _pool_backward(dya, pooled, w_pool, pool_scale, dproj):
    s = dya.shape[0]
    g_n, cg, _ = w_pool.shape
    t = POOL_TILE
    nt = s // t

    def body(dya_ref, dya_next_ref, pooled_ref, wp_ref, sc_ref, dproj_in, du_ref, gw_ref, gs_ref):
        del dproj_in
        g = pl.program_id(0)
        ti = pl.program_id(1)

        @pl.when(ti == 0)
        def _():
            gw_ref[...] = jnp.zeros_like(gw_ref)
            gs_ref[...] = jnp.zeros_like(gs_ref)

        win = jnp.left_shift(2, g)
        wp = wp_ref[...]
        sc = sc_ref[...]
        pooled_v = pooled_ref[...]
        dya_v = dya_ref[...].astype(F32)
        mixed = _dot(pooled_v, wp)
        gs_ref[...] += jnp.sum(dya_v * mixed, axis=0, keepdims=True)
        dmixed = (dya_v * sc).astype(BF16)
        gw_ref[...] += _dot(pooled_v, dmixed, "tn")
        dpooled = _dot(dmixed, wp, "nt")
        dmixed_next = (dya_next_ref[...].astype(F32) * sc).astype(BF16)
        dpooled_next = _dot(dmixed_next, wp, "nt")
        tok = ti * t + lax.broadcasted_iota(jnp.int32, (t, 1), 0)
        e_cur = dpooled / jnp.minimum(tok + 1, win).astype(F32)
        e_next = dpooled_next / jnp.minimum(tok + t + 1, win).astype(F32)
        row = lax.broadcasted_iota(jnp.int32, (t, t), 0)
        col = lax.broadcasted_iota(jnp.int32, (t, t), 1)
        lead = col - row
        band_cur = ((lead >= 0) & (lead < win)).astype(BF16)
        band_next = ((lead + t < win) & (ti < nt - 1)).astype(BF16)
        c_hi, c_lo = _split_bf16(e_cur)
        n_hi, n_lo = _split_bf16(e_next)
        du = (_dot(band_cur, c_hi) + _dot(band_cur, c_lo)
              + _dot(band_next, n_hi) + _dot(band_next, n_lo)) - dpooled
        du_ref[...] = du.astype(BF16)

    blk = pl.BlockSpec((t, cg), lambda g, i: (i, g))
    du, gw, gs = pl.pallas_call(
        body, name="pool_bwd", grid=(g_n, nt),
        out_shape=(_sds(dproj.shape, BF16), _sds((g_n, cg, cg), F32), _sds((1, g_n * cg), F32)),
        in_specs=[blk, pl.BlockSpec((t, cg), lambda g, i: (jnp.minimum(i + 1, nt - 1), g)), blk,
                  pl.BlockSpec((None, cg, cg), lambda g, i: (g, 0, 0)),
                  pl.BlockSpec((1, cg), lambda g, i: (0, g)),
                  pl.BlockSpec(memory_space=pl.ANY)],
        out_specs=(blk, pl.BlockSpec((None, cg, cg), lambda g, i: (g, 0, 0)),
                   pl.BlockSpec((1, cg), lambda g, i: (0, g))),
        input_output_aliases={5: 0}, compiler_params=_params("parallel", "arbitrary"),
    )(dya, dya, pooled, w_pool, pool_scale, dproj)
    return du, gw, gs


def _qkv_prepare(proj, q_norm_w, k_norm_w, width, deps=()):
    s = proj.shape[0]
    tm = _tile(s, 256)
    heads = width // HEAD_DIM

    def body(q_ref, k_ref, v_ref, qw_ref, kw_ref, qn_ref, kn_ref, vb_ref):
        for h in range(heads):
            cols = slice(h * HEAD_DIM, (h + 1) * HEAD_DIM)
            for src, w_ref, dst in ((q_ref, qw_ref, qn_ref), (k_ref, kw_ref, kn_ref)):
                v = src[:, cols]
                r = lax.rsqrt(jnp.mean(v * v, axis=-1, keepdims=True) + NORM_EPS)
                dst[:, cols] = (v * r * w_ref[...]).astype(BF16)
        vb_ref[...] = v_ref[...].astype(BF16)

    vec = pl.BlockSpec((1, HEAD_DIM), lambda i: (0, 0))
    out_spec = pl.BlockSpec((tm, width), lambda i: (i, 0))
    return pl.pallas_call(
        _with_deps(body, 5, deps), name="qkv_prep", grid=(s // tm,),
        out_shape=(_sds((s, width), BF16),) * 3,
        in_specs=[pl.BlockSpec((tm, width), lambda i: (i, 1)), pl.BlockSpec((tm, width), lambda i: (i, 2)),
                  pl.BlockSpec((tm, width), lambda i: (i, 3)), vec, vec] + [_ANY] * len(deps),
        out_specs=(out_spec,) * 3, compiler_params=_params("parallel"),
    )(proj, proj, proj, q_norm_w, k_norm_w, *deps)


def _qk_norm_backward(name, dn, proj, col_block, norm_w, dproj, width, deps=()):
    s = proj.shape[0]
    tm = _tile(s, 256)
    heads = width // HEAD_DIM

    def body(dn_ref, q_ref, w_ref, dproj_in, dq_ref, gw_ref):
        del dproj_in

        @pl.when(pl.program_id(0) == 0)
        def _():
            gw_ref[...] = jnp.zeros_like(gw_ref)

        wv = w_ref[...]
        gw = jnp.zeros((1, HEAD_DIM), F32)
        for h in range(heads):
            cols = slice(h * HEAD_DIM, (h + 1) * HEAD_DIM)
            v = q_ref[:, cols]
            g = dn_ref[:, cols]
            r = lax.rsqrt(jnp.mean(v * v, axis=-1, keepdims=True) + NORM_EPS)
            n = v * r
            gw = gw + jnp.sum(g * n, axis=0, keepdims=True)
            gn = g * wv
            dq_ref[:, cols] = (r * (gn - n * jnp.mean(gn * n, axis=-1, keepdims=True))).astype(BF16)
        gw_ref[...] += gw

    blk = pl.BlockSpec((tm, width), lambda i: (i, col_block))
    return pl.pallas_call(
        _with_deps(body, 4, deps), name=name, grid=(s // tm,),
        out_shape=(_sds(dproj.shape, BF16), _sds((1, HEAD_DIM), F32)),
        in_specs=[pl.BlockSpec((tm, width), lambda i: (i, 0)), blk,
                  pl.BlockSpec((1, HEAD_DIM), lambda i: (0, 0)), pl.BlockSpec(memory_space=pl.ANY)]
        + [_ANY] * len(deps),
        out_specs=(blk, pl.BlockSpec((1, HEAD_DIM), lambda i: (0, 0))),
        input_output_aliases={3: 0}, compiler_params=_params("arbitrary"),
    )(dn, proj, norm_w, dproj, *deps)


def _strict_upper(n):
    row = lax.broadcasted_iota(jnp.int32, (n, n), 0)
    col = lax.broadcasted_iota(jnp.int32, (n, n), 1)
    return (row > col).astype(BF16)


def _strict_lower(n):
    row = lax.broadcasted_iota(jnp.int32, (n, n), 0)
    col = lax.broadcasted_iota(jnp.int32, (n, n), 1)
    return (row < col).astype(BF16)


def _cumulate(v, tri):
    hi, lo = _split_bf16(v)
    return _dot(hi, tri) + _dot(lo, tri)


def _log_sigmoid(z):
    return jnp.minimum(z, 0.0) - jnp.log(1.0 + jnp.exp(-jnp.abs(z)))


def _attention_forward(qn, kn, vb):
    s, width = qn.shape
    heads = width // HEAD_DIM
    tq, tk = Q_TILE, K_TILE
    hp = min(HEADS_PER_STEP, heads)
    assert tq == tk and s % tq == 0 and heads % hp == 0
    scale = 1.0 / math.sqrt(HEAD_DIM)

    def body(q_ref, k_ref, v_ref, o_ref, a_scr):
        qi = pl.program_id(1)
        upper = _strict_upper(tk)
        causal = lax.broadcasted_iota(jnp.int32, (tq, tk), 1) < lax.broadcasted_iota(jnp.int32, (tq, tk), 0)
        head_cols = [slice(u * HEAD_DIM, (u + 1) * HEAD_DIM) for u in range(hp)]

        def weights(kb, carry, masked):
            rows = pl.ds(pl.multiple_of(kb * tk, tk), tk)
            out = []
            for u, cols in enumerate(head_cols):
                later = carry[u]
                z = _dot(q_ref[:, cols], k_ref[rows, cols], "nt") * scale
                log_beta = _log_sigmoid(z)
                l = log_beta - z
                if masked:
                    l = jnp.where(causal, l, 0.0)
                a = jnp.exp(log_beta + _cumulate(l, upper) + later)
                if masked:
                    a = jnp.where(causal, a, 0.0)
                a_scr[u, :, rows] = a.astype(BF16)
                out.append(later + jnp.sum(l, axis=1, keepdims=True))
            return tuple(out)

        later = weights(qi, tuple(jnp.zeros((tq, 1), F32) for _ in range(hp)), True)
        lax.fori_loop(0, qi, lambda i, c: weights(qi - 1 - i, c, False), later)

        def mix(kb, accs):
            rows = pl.ds(pl.multiple_of(kb * tk, tk), tk)
            return tuple(acc + _dot(a_scr[u, :, rows], v_ref[rows, cols])
                         for u, (acc, cols) in enumerate(zip(accs, head_cols)))

        accs = lax.fori_loop(0, qi + 1, mix, tuple(jnp.zeros((tq, HEAD_DIM), F32) for _ in range(hp)))
        for acc, cols in zip(accs, head_cols):
            o_ref[:, cols] = acc.astype(BF16)

    full = pl.BlockSpec((s, hp * HEAD_DIM), lambda h, i: (0, h))
    blk = pl.BlockSpec((tq, hp * HEAD_DIM), lambda h, i: (i, h))
    return pl.pallas_call(
        body, name="attn_fwd", grid=(heads // hp, s // tq), out_shape=_sds((s, width), BF16),
        in_specs=[blk, full, full], out_specs=blk, scratch_shapes=[pltpu.VMEM((hp, tq, s), BF16)],
        compiler_params=_params("parallel", "parallel"),
    )(qn, kn, vb)


def _attention_backward(qn, kn, vb, dout, dproj, v_col_block, deps=()):
    s, width = qn.shape
    heads = width // HEAD_DIM
    tq, tk = Q_TILE, K_TILE
    hp = min(HEADS_PER_STEP_BWD, heads)
    nq = s // tq
    scale = 1.0 / math.sqrt(HEAD_DIM)
    v_block0 = v_col_block * (heads // hp)

    def body(q_ref, k_ref, v_ref, do_ref, dproj_in, dq_ref, dk_ref, dv_ref,
             a_scr, lb_scr, dz_scr, dkt_scr, dvt_scr):
        del dproj_in
        qi = pl.program_id(1)

        @pl.when(qi == 0)
        def _():
            dkt_scr[...] = jnp.zeros_like(dkt_scr)
            dvt_scr[...] = jnp.zeros_like(dvt_scr)

        upper = _strict_upper(tk)
        lower = _strict_lower(tk)
        causal = lax.broadcasted_iota(jnp.int32, (tq, tk), 1) < lax.broadcasted_iota(jnp.int32, (tq, tk), 0)
        head_cols = [slice(u * HEAD_DIM, (u + 1) * HEAD_DIM) for u in range(hp)]

        def weights(kb, carry, masked):
            rows = pl.ds(pl.multiple_of(kb * tk, tk), tk)
            out = []
            for u, cols in enumerate(head_cols):
                later = carry[u]
                z = _dot(q_ref[:, cols], k_ref[rows, cols], "nt") * scale
                log_beta = _log_sigmoid(z)
                l = log_beta - z
                if masked:
                    l = jnp.where(causal, l, 0.0)
                a = jnp.exp(log_beta + _cumulate(l, upper) + later)
                if masked:
                    a = jnp.where(causal, a, 0.0)
                a_scr[u, :, rows] = a
                lb_scr[u, :, rows] = log_beta
                out.append(later + jnp.sum(l, axis=1, keepdims=True))
            return tuple(out)

        zeros = tuple(jnp.zeros((tq, 1), F32) for _ in range(hp))
        later = weights(qi, zeros, True)
        lax.fori_loop(0, qi, lambda i, c: weights(qi - 1 - i, c, False), later)

        q_t = [jnp.transpose(q_ref[:, cols].astype(F32)).astype(BF16) for cols in head_cols]
        do_t = [jnp.transpose(do_ref[:, cols].astype(F32)).astype(BF16) for cols in head_cols]

        def scores(kb, carry, masked):
            rows = pl.ds(pl.multiple_of(kb * tk, tk), tk)
            out = []
            for u, cols in enumerate(head_cols):
                before = carry[u]
                beta = jnp.exp(lb_scr[u, :, rows])
                g = a_scr[u, :, rows] * _dot(do_ref[:, cols], v_ref[rows, cols], "nt")
                p = _cumulate(g, lower) + before
                dz = g - (g + p) * beta
                if masked:
                    dz = jnp.where(causal, dz, 0.0)
                dz_scr[u, :, rows] = (dz * scale).astype(BF16)
                out.append(before + jnp.sum(g, axis=1, keepdims=True))
            return tuple(out)

        before = lax.fori_loop(0, qi, lambda i, c: scores(i, c, False), zeros)
        scores(qi, before, True)

        def products(kb, dqs):
            rows = pl.ds(pl.multiple_of(kb * tk, tk), tk)
            out = []
            for u, cols in enumerate(head_cols):
                dz = dz_scr[u, :, rows]
                dkt_scr[cols, rows] += _dot(q_t[u], dz)
                dvt_scr[cols, rows] += _dot(do_t[u], a_scr[u, :, rows].astype(BF16))
                out.append(dqs[u] + _dot(dz, k_ref[rows, cols]))
            return tuple(out)

        dqs = lax.fori_loop(0, qi + 1, products, tuple(jnp.zeros((tq, HEAD_DIM), F32) for _ in range(hp)))
        for u, cols in enumerate(head_cols):
            dq_ref[:, cols] = dqs[u]

        @pl.when(qi == nq - 1)
        def _():
            dk_ref[...] = jnp.transpose(dkt_scr[...])
            dv_ref[...] = jnp.transpose(dvt_scr[...]).astype(BF16)

    wide = hp * HEAD_DIM
    full = pl.BlockSpec((s, wide), lambda h, i: (0, h))
    blk = pl.BlockSpec((tq, wide), lambda h, i: (i, h))
    return pl.pallas_call(
        _with_deps(body, 5, deps), name="attn_bwd", grid=(heads // hp, nq),
        out_shape=(_sds((s, width), F32), _sds((s, width), F32), _sds(dproj.shape, BF16)),
        in_specs=[blk, full, full, blk, pl.BlockSpec(memory_space=pl.ANY)] + [_ANY] * len(deps),
        out_specs=(blk, full, pl.BlockSpec((s, wide), lambda h, i: (0, v_block0 + h))),
        scratch_shapes=[pltpu.VMEM((hp, tq, s), F32), pltpu.VMEM((hp, tq, s), F32), pltpu.VMEM((hp, tq, s), BF16),
                        pltpu.VMEM((wide, s), F32), pltpu.VMEM((wide, s), F32)],
        input_output_aliases={4: 2}, compiler_params=_params("parallel", "arbitrary"),
    )(qn, kn, vb, dout, dproj, *deps)


def _place_columns(name, src, dst, col_block):
    s, w = src.shape
    tm = _tile(s, 512)

    def body(src_ref, dst_in, out_ref):
        del dst_in
        out_ref[...] = src_ref[...]

    return pl.pallas_call(
        body, name=name, grid=(s // tm,), out_shape=_sds(dst.shape, dst.dtype),
        in_specs=[pl.BlockSpec((tm, w), lambda i: (i, 0)), pl.BlockSpec(memory_space=pl.ANY)],
        out_specs=pl.BlockSpec((tm, w), lambda i: (i, col_block)),
        input_output_aliases={1: 0}, compiler_params=_params("parallel"),
    )(src, dst)


def _cast_into_slot(name, x, slot):
    r, c = x.shape
    tr = _tile(r, max(SUBLANES * 2, (1 << 20) // c), SUBLANES * 2)

    def body(slot_ref, x_ref, o_ref):
        del slot_ref
        o_ref[...] = x_ref[...].astype(BF16)

    grid_spec = pltpu.PrefetchScalarGridSpec(
        num_scalar_prefetch=1, grid=(r // tr,),
        in_specs=[pl.BlockSpec((tr, c), lambda i, slot_ref: (i, 0))],
        out_specs=pl.BlockSpec((None, tr, c), lambda i, slot_ref: (slot_ref[0], i, 0)))
    return pl.pallas_call(
        body, name=name, grid_spec=grid_spec, out_shape=_sds((N_DEV, r, c), BF16),
        compiler_params=_params("parallel"),
    )(slot, x)


def _adamw_update(gv, w_ref, m_ref, v_ref, d_ref, nm_ref, nv_ref):
    c1 = 1.0 - ADAM_B1 ** ADAM_STEP
    c2 = 1.0 - ADAM_B2 ** ADAM_STEP
    nm = ADAM_B1 * m_ref[...] + (1.0 - ADAM_B1) * gv
    nv = ADAM_B2 * v_ref[...] + (1.0 - ADAM_B2) * (gv * gv)
    d_ref[...] = -ADAM_LR * ((nm / c1) / (jnp.sqrt(nv / c2) + ADAM_EPS) + ADAM_WD * w_ref[...])
    nm_ref[...] = nm
    nv_ref[...] = nv


def _adamw(name, w, g, m, v):
    r, c = w.shape
    tr = _tile(r, max(SUBLANES, (1 << 19) // c))

    def body(w_ref, g_ref, m_ref, v_ref, d_ref, nm_ref, nv_ref):
        _adamw_update(g_ref[...], w_ref, m_ref, v_ref, d_ref, nm_ref, nv_ref)

    blk = pl.BlockSpec((tr, c), lambda i: (i, 0))
    return pl.pallas_call(
        body, name=name, grid=(r // tr,), out_shape=(_sds((r, c), F32),) * 3,
        in_specs=[blk] * 4, out_specs=(blk,) * 3, compiler_params=_params("parallel"),
    )(w, g, m, v)


def _adamw_summed(name, w, own, received, m, v):
    r, c = w.shape
    nj = received.shape[0]
    tr = _tile(r, max(2 * SUBLANES, (1 << 19) // c), 2 * SUBLANES)

    def body(w_ref, own_ref, rec_ref, m_ref, v_ref, g_ref, d_ref, nm_ref, nv_ref):
        gv = own_ref[...].astype(F32)
        for j in range(nj):
            gv = gv + rec_ref[j].astype(F32)
        g_ref[...] = gv
        _adamw_update(gv, w_ref, m_ref, v_ref, d_ref, nm_ref, nv_ref)

    blk = pl.BlockSpec((tr, c), lambda i: (i, 0))
    return pl.pallas_call(
        body, name=name, grid=(r // tr,), out_shape=(_sds((r, c), F32),) * 4,
        in_specs=[blk, pl.BlockSpec((None, tr, c), lambda i: (0, i, 0)),
                  pl.BlockSpec((nj, tr, c), lambda i: (0, i, 0)), blk, blk],
        out_specs=(blk,) * 4, compiler_params=_params("parallel"),
    )(w, own, received, m, v)


def _rows_of_lanes(v):
    rows = v.shape[1] // LANES
    out = v.reshape(rows, LANES)
    pad = (-rows) % SUBLANES
    if pad:
        out = jnp.pad(out, ((0, pad), (0, 0)))
    return out


def kernel(x, c, w_ada, b_ada, norm1_w, w_in, q_norm_w, k_norm_w, w_pool, pool_scale, w_a_up, w_b_up, w_o, norm2_w, w_ff1, w_ff2, loss_target, m_w_ada, m_b_ada, m_norm1_w, m_w_in, m_q_norm_w, m_k_norm_w, m_w_pool, m_pool_scale, m_w_a_up, m_w_b_up, m_w_o, m_norm2_w, m_w_ff1, m_w_ff2, v_w_ada, v_b_ada, v_norm1_w, v_w_in, v_q_norm_w, v_k_norm_w, v_w_pool, v_pool_scale, v_w_a_up, v_w_b_up, v_w_o, v_norm2_w, v_w_ff1, v_w_ff2):
    _, s, d = x.shape
    half = d // 2
    d8 = d // N_DEV
    n_groups = len(POOL_WINDOWS)
    cg = half // n_groups
    me = _group_index(MESH_AXES)

    x2 = x[0]
    target = loss_target[0]

    my_slot = jnp.reshape(me, (1,)).astype(jnp.int32)

    def cast(i, t):
        return _cast_into_slot("cast_w%d" % i, t, my_slot)

    def gather_start(tag, bufs, phase):
        if phase < 2:
            return _launch("ag%s_ici%d" % (tag, phase), bufs, _plan_gather_ici(phase), (2, 4)[phase] * len(bufs))
        return _launch("ag%s_d2d" % tag, bufs, _plan_gather_d2d, len(_CHIP_MASKS) * len(bufs))

    def gather_plans(tag, n_bufs, phase):
        if phase < 2:
            return ("ag%s_ici%d" % (tag, phase), _plan_gather_ici(phase), (2, 4)[phase] * n_bufs)
        return ("ag%s_d2d" % tag, _plan_gather_d2d, len(_CHIP_MASKS) * n_bufs)

    bx, by, bc = _AXIS_BIT["x"], _AXIS_BIT["y"], _AXIS_BIT["c"]
    buf_a = [cast(0, w_in[0])]

    c_all = _all_gather_2d("ag_c", c.reshape(d // LANES, LANES), deps=tuple(buf_a)).reshape(N_DEV, d)
    wa = w_ada.shape[2]
    b_shard = lax.dynamic_slice_in_dim(b_ada, me * wa, wa, axis=1)
    mod_part = _ada_forward(c_all, w_ada[0], b_shard)
    mod_all = _all_gather_2d("ag_mod", mod_part.reshape(N_DEV * wa // LANES, LANES))
    mod_all = mod_all.reshape(N_DEV, N_DEV, wa)
    mod = lax.dynamic_slice_in_dim(mod_all, me, 1, axis=1).reshape(1, N_MOD * d)
    shift1, scale1, gate1, shift2, scale2, gate2 = [mod[:, i * d:(i + 1) * d] for i in range(N_MOD)]
    fl_a, tok = _launch("agA_near", buf_a, _plan_neighbours, 2, deps=(mod_all,))
    buf_b = [cast(1, w_pool[0].reshape(-1, cg)), cast(2, w_a_up[0]), cast(3, w_b_up[0]), cast(4, w_o[0])]
    buf_c = [cast(5, w_ff1[0])]
    buf_e = [cast(6, w_ff2[0])]

    h = _norm_forward("norm1_fwd", x2, norm1_w, scale1, shift1, deps=(tok,))
    buf_a = _land(fl_a, [h] + buf_b + buf_c + buf_e)
    near = (0, bx, by)
    (fl_far, fl_near), tok = _launch_groups(
        "agA_far", [(buf_a, [("agA_far", _plan_diagonal, 2), ("agA_near_d2d", _plan_d2d(near), len(near))])])
    (fl_b, fl_c, fl_e), tok = _launch_groups(
        "agBCE_ici0", [(buf_b, [gather_plans("B", len(buf_b), 0)]), (buf_c, [gather_plans("C", 1, 0)]),
                       (buf_e, [gather_plans("E", 1, 0)])])

    tm = _tile(s, 1024)
    tk = _tile(d, 2048)
    te = _tile(d, 512)

    own_slots = jnp.stack([me ^ m for m in near]).astype(jnp.int32)
    far_slots = jnp.stack([me ^ bx ^ by ^ f for f in (0, bc)]).astype(jnp.int32)
    buf_a = fl_near[-1]
    proj = _project_slots("proj_own", h, buf_a[0], own_slots, 4 * d, deps=(tok,))
    buf_a = _land(fl_near, proj, bufs=buf_a)
    proj = _project_slots("proj_sibling", h, buf_a[0], own_slots ^ bc, 4 * d, proj_in=proj)
    buf_a = _land(fl_far, proj, bufs=buf_a)
    fl_a, tok = _launch("agA_far_d2d", buf_a, _plan_d2d((bx | by,)), 1)
    w_in_f, = _land(fl_a, tok)
    proj = _project_slots("proj_far", h, w_in_f, far_slots, 4 * d, proj_in=proj)
    buf_b = _land(fl_b, proj)
    buf_c = _land(fl_c, proj)
    (fl_b, fl_c), tok = _launch_groups(
        "agBC_ici1", [(buf_b, [gather_plans("B", len(buf_b), 1)]), (buf_c, [gather_plans("C", 1, 1)])])

    qn, kn, vb = _qkv_prepare(proj, q_norm_w, k_norm_w, half, deps=(tok,))
    attn = _attention_forward(qn, kn, vb)
    buf_b = _land(fl_b, attn)
    buf_e = _land(fl_e, attn)
    (fl_b, fl_e), tok = _launch_groups(
        "agB_d2d_E_ici1", [(buf_b, [gather_plans("B", len(buf_b), 2)]), (buf_e, [gather_plans("E", 1, 1)])])
    w_pool_f, w_a_f, w_b_f, w_o_f = _land(fl_b, tok)
    rows_pool = cg // N_DEV
    w_pool_f = w_pool_f.reshape(N_DEV, n_groups, rows_pool, cg).transpose(1, 0, 2, 3).reshape(n_groups, cg, cg)
    w_o_f = w_o_f.reshape(d, d)
    pooled, ya_in = _pool_forward(proj, w_pool_f, pool_scale)

    def merge_epilogue(ga_ref, gb_ref, ya, yb, out_refs):
        merged_ref, ya_ref, yb_ref = out_refs
        merged = jax.nn.sigmoid(ga_ref[...]) * ya + jax.nn.sigmoid(gb_ref[...]) * yb
        merged_ref[...] = merged.astype(BF16)
        ya_ref[...] = ya.astype(BF16)
        yb_ref[...] = yb.astype(BF16)

    def up_body(a1_ref, b1_ref, a2_ref, b2_ref, ga_ref, gb_ref, *out_refs):
        merge_epilogue(ga_ref, gb_ref, _dot(a1_ref[...], b1_ref[...]), _dot(a2_ref[...], b2_ref[...]), out_refs)

    ga_blk0 = 2 * d // d8
    gb_blk0 = 3 * d // d8
    tu = s
    a_spec = pl.BlockSpec((tu, half), lambda i, j: (i, 0))
    wup_spec = pl.BlockSpec((None, half, d8), lambda i, j: (j, 0, 0))
    o_blk = pl.BlockSpec((tu, d8), lambda i, j: (i, j))
    merged, y_a, y_b = pl.pallas_call(
        up_body, name="up_merge", grid=(s // tu, N_DEV), out_shape=(_sds((s, d), BF16),) * 3,
        in_specs=[a_spec, wup_spec, a_spec, wup_spec,
                  pl.BlockSpec((tu, d8), lambda i, j: (i, ga_blk0 + j)),
                  pl.BlockSpec((tu, d8), lambda i, j: (i, gb_blk0 + j))],
        out_specs=(o_blk,) * 3, compiler_params=_params("parallel", "parallel"),
    )(ya_in, w_a_f, attn, w_b_f, proj, proj)
    buf_c = _land(fl_c, merged)
    fl_c, tok_c = gather_start("C", buf_c, 2)

    tn = _tile(d, 1024)

    def oproj_epilogue(acc, extra_refs, out_refs):
        x_ref, g_ref = extra_refs
        x1_ref, o_ref = out_refs
        x1_ref[...] = x_ref[...] + g_ref[...] * acc
        o_ref[...] = acc.astype(BF16)

    mn_blk = pl.BlockSpec((tm, tn), lambda i, j, k: (i, j))
    e_blk = pl.BlockSpec((tm, te), lambda i, j, k: (i, j))
    e_vec = pl.BlockSpec((1, te), lambda i, j, k: (0, j))
    s_blk = pl.BlockSpec((s, te), lambda i, j, k: (i, j))
    x1, o_act = _matmul(
        "oproj", "nn", (1, d // te, d // tk), merged, pl.BlockSpec((s, tk), lambda i, j, k: (i, k)),
        w_o_f, pl.BlockSpec((tk, te), lambda i, j, k: (k, j)),
        [_sds((s, d), F32), _sds((s, d), BF16)], [s_blk, s_blk], (s, te),
        epilogue=oproj_epilogue, extras=(x2, gate1), extra_specs=(s_blk, e_vec), deps=(tok_c,))

    h2 = _norm_forward("norm2_fwd", x1, norm2_w, scale2, shift2)
    buf_e = _land(fl_e, h2)
    fl_e, tok_e = gather_start("E", buf_e, 2)
    w_ff1_f, = _land(fl_c, [h2, tok_e])

    def ff1_epilogue(acc, extra_refs, out_refs):
        r = jnp.maximum(acc, 0.0)
        out_refs[0][...] = r.astype(BF16)
        out_refs[1][...] = (r * r).astype(BF16)

    ff_blk = pl.BlockSpec((tm, half), lambda i, j, k: (i, j))
    relu, act = _matmul(
        "ff1", "nn", (s // tm, N_DEV, d // tk), h2, pl.BlockSpec((tm, tk), lambda i, j, k: (i, k)),
        w_ff1_f, pl.BlockSpec((None, tk, half), lambda i, j, k: (j, k, 0)),
        [_sds((s, 4 * d), BF16)] * 2, [ff_blk, ff_blk], (tm, half), epilogue=ff1_epilogue)
    w_ff2_f, = _land(fl_e, act)
    w_ff2_f = w_ff2_f.reshape(4 * d, d)

    def ff2_epilogue(acc, extra_refs, out_refs):
        x1_ref, g_ref, t_ref = extra_refs
        df_ref, dy_ref, sq_ref, dgate_ref = out_refs
        gate = g_ref[...]
        err = x1_ref[...] + gate * acc - t_ref[...]
        dyv = err * (1.0 / d)
        dy_ref[...] = dyv
        df_ref[...] = (dyv * gate).astype(BF16)
        sq_ref[...] = jnp.full(sq_ref.shape, jnp.sum(err * err), F32)
        dgate_ref[...] = jnp.broadcast_to(jnp.sum(dyv * acc, axis=0, keepdims=True), dgate_ref.shape)

    df, dy, sq, dgate2_parts = _matmul(
        "ff2", "nn", (s // tm, d // te, 2 * d // tk), act, pl.BlockSpec((tm, 2 * tk), lambda i, j, k: (i, k)),
        w_ff2_f, pl.BlockSpec((2 * tk, te), lambda i, j, k: (k, j)),
        [_sds((s, d), BF16), _sds((s, d), F32), _sds((s // tm * SUBLANES, d // te * LANES), F32),
         _sds((s // tm * SUBLANES, d), F32)],
        [e_blk, e_blk, pl.BlockSpec((SUBLANES, LANES), lambda i, j, k: (i, j)),
         pl.BlockSpec((SUBLANES, te), lambda i, j, k: (i, j))], (tm, te),
        epilogue=ff2_epilogue, extras=(x1, gate2, target), extra_specs=(e_blk, e_vec, e_blk))
    loss_local = (0.5 / d) * jnp.sum(sq[::SUBLANES, ::LANES])
    dgate2 = jnp.sum(dgate2_parts[::SUBLANES], axis=0, keepdims=True)

    tok_k = _tile(s, 2048)
    tw = _tile(d, 1024)
    g_ff2 = _matmul(
        "g_ff2", "tn", (4 * d // tw, d // tn, s // tok_k), act, pl.BlockSpec((tok_k, tw), lambda i, j, k: (k, i)),
        df, pl.BlockSpec((tok_k, tn), lambda i, j, k: (k, j)),
        [_sds((4 * d, d), BF16)], [pl.BlockSpec((tw, tn), lambda i, j, k: (i, j))], (tw, tn))[0]

    def da_epilogue(acc, extra_refs, out_refs):
        out_refs[0][...] = (acc * (2.0 * extra_refs[0][...].astype(F32))).astype(BF16)

    big_blk = pl.BlockSpec((tm, tn), lambda i, j, k: (i, j))
    fl_f2, tok = _reduce_scatter_start("F2", [g_ff2.reshape(N_DEV, half, d)])
    df1 = _matmul(
        "da_ff", "nt", (s // tm, 4 * d // tn, d // tk), df, pl.BlockSpec((tm, tk), lambda i, j, k: (i, k)),
        w_ff2_f, pl.BlockSpec((tn, tk), lambda i, j, k: (j, k)),
        [_sds((s, 4 * d), BF16)], [big_blk], (tm, tn),
        epilogue=da_epilogue, extras=(relu,), extra_specs=(big_blk,), deps=(tok,))[0]

    fl_f2, tok = _reduce_scatter_middle("F2", fl_f2, df1, me)
    g_ff1 = _matmul(
        "g_ff1", "tn", (d // tw, N_DEV, s // tok_k), h2, pl.BlockSpec((tok_k, tw), lambda i, j, k: (k, i)),
        df1, pl.BlockSpec((tok_k, half), lambda i, j, k: (k, j)),
        [_sds((N_DEV, d, half), BF16)], [pl.BlockSpec((None, tw, half), lambda i, j, k: (j, i, 0))], (tw, half),
        deps=(tok,))[0]

    fl_f1, tok = _reduce_scatter_start("F1", [g_ff1])
    dh2 = _matmul(
        "dh2", "nt", (s // tm, d // tn, N_DEV // 2), df1, pl.BlockSpec((tm, 2 * half), lambda i, j, k: (i, k)),
        w_ff1_f, pl.BlockSpec((2, tn, half), lambda i, j, k: (k, j, 0)),
        [_sds((s, d), F32)], [mn_blk], (tm, tn), deps=(tok,))[0]

    sum_ff2, = _reduce_scatter_finish(fl_f2, dh2)
    fl_f1, tok = _reduce_scatter_middle("F1", fl_f1, dh2, me)
    dx1, dshift2, dscale2, g_norm2, do, dgate1 = _norm_backward(
        "norm2_bwd", dh2, x1, norm2_w, scale2, dy, gated=(gate1, o_act), deps=(tok,))

    g_o = _matmul(
        "g_o", "tn", (d // tw, d // tn, s // tok_k), merged, pl.BlockSpec((tok_k, tw), lambda i, j, k: (k, i)),
        do, pl.BlockSpec((tok_k, tn), lambda i, j, k: (k, j)),
        [_sds((d, d), BF16)], [pl.BlockSpec((tw, tn), lambda i, j, k: (i, j))], (tw, tn))[0]

    def merge_bwd_epilogue(acc, extra_refs, out_refs):
        ga_ref, gb_ref, ya_ref, yb_ref = extra_refs
        dya_ref, dyb_ref, dga_ref, dgb_ref = out_refs
        sa = jax.nn.sigmoid(ga_ref[...])
        sb = jax.nn.sigmoid(gb_ref[...])
        dya_ref[...] = (acc * sa).astype(BF16)
        dyb_ref[...] = (acc * sb).astype(BF16)
        dga_ref[...] = (acc * ya_ref[...].astype(F32) * (sa * (1.0 - sa))).astype(BF16)
        dgb_ref[...] = (acc * yb_ref[...].astype(F32) * (sb * (1.0 - sb))).astype(BF16)

    td = _tile(d, 256)
    nb = d // td
    d_blk = pl.BlockSpec((s, td), lambda i, j, k: (i, j))
    dy_a, dy_b, dproj, dg_b = _matmul(
        "dmerged", "nt", (1, nb, d // tk), do, pl.BlockSpec((s, tk), lambda i, j, k: (i, k)),
        w_o_f, pl.BlockSpec((td, tk), lambda i, j, k: (j, k)),
        [_sds((s, d), BF16), _sds((s, d), BF16), _sds((s, 4 * d), BF16), _sds((s, d), BF16)],
        [d_blk, d_blk, pl.BlockSpec((s, td), lambda i, j, k: (i, 2 * nb + j)), d_blk], (s, td),
        epilogue=merge_bwd_epilogue, extras=(proj, proj, y_a, y_b),
        extra_specs=(pl.BlockSpec((s, td), lambda i, j, k: (i, 2 * nb + j)),
                     pl.BlockSpec((s, td), lambda i, j, k: (i, 3 * nb + j)), d_blk, d_blk))
    dproj = _place_columns("place_dgb", dg_b, dproj, 3)

    up_a = pl.BlockSpec((tok_k, half), lambda i, j, k: (k, 0))
    up_b = pl.BlockSpec((tok_k, d8), lambda i, j, k: (k, j))
    up_o = pl.BlockSpec((None, half, d8), lambda i, j, k: (j, 0, 0))
    g_a_up = _matmul("g_a_up", "tn", (1, N_DEV, s // tok_k), ya_in, up_a, dy_a, up_b,
                     [_sds((N_DEV, half, d8), BF16)], [up_o], (half, d8))[0]
    g_b_up = _matmul("g_b_up", "tn", (1, N_DEV, s // tok_k), attn, up_a, dy_b, up_b,
                     [_sds((N_DEV, half, d8), BF16)], [up_o], (half, d8))[0]
    slabs = 4
    dn_a = pl.BlockSpec((tm, slabs * d8), lambda i, j, k: (i, k))
    dn_b = pl.BlockSpec((slabs, half, d8), lambda i, j, k: (k, 0, 0))
    dn_o = pl.BlockSpec((tm, half), lambda i, j, k: (i, 0))
    dya_in = _matmul("d_ya_in", "nt", (s // tm, 1, N_DEV // slabs), dy_a, dn_a, w_a_f, dn_b,
                     [_sds((s, half), BF16)], [dn_o], (tm, half))[0]
    dattn = _matmul("d_attn", "nt", (s // tm, 1, N_DEV // slabs), dy_b, dn_a, w_b_f, dn_b,
                    [_sds((s, half), BF16)], [dn_o], (tm, half))[0]

    dproj, g_pool, g_pool_scale = _pool_backward(dya_in, pooled, w_pool_f, pool_scale, dproj)
    sum_ff1, = _reduce_scatter_finish(fl_f1, g_pool)
    g_pool_send = g_pool.astype(BF16).reshape(n_groups, N_DEV, rows_pool, cg).transpose(1, 0, 2, 3)
    g_pool_send = g_pool_send.reshape(N_DEV, n_groups * rows_pool, cg)
    fl_b, tok = _reduce_scatter_start("B", [g_pool_send, g_a_up, g_b_up, g_o.reshape(N_DEV, d8, d)])
    dqn, dkn, dproj = _attention_backward(qn, kn, vb, dattn, dproj, 3, deps=(tok,))
    fl_b, tok = _reduce_scatter_middle("B", fl_b, dqn, me)
    dproj, g_qnorm = _qk_norm_backward("qnorm_bwd", dqn, proj, 1, q_norm_w, dproj, half, deps=(tok,))
    dproj, g_knorm = _qk_norm_backward("knorm_bwd", dkn, proj, 2, k_norm_w, dproj, half)

    g_in = _matmul(
        "g_in", "tn", (d // tw, N_DEV, s // tok_k), h, pl.BlockSpec((tok_k, tw), lambda i, j, k: (k, i)),
        dproj, pl.BlockSpec((tok_k, half), lambda i, j, k: (k, j)),
        [_sds((N_DEV, d, half), BF16)], [pl.BlockSpec((None, tw, half), lambda i, j, k: (j, i, 0))], (tw, half))[0]
    fl_in, tok = _reduce_scatter_start("I", [g_in])
    dh = _matmul(
        "dh", "nt", (s // tm, d // tn, N_DEV // 2), dproj, pl.BlockSpec((tm, 2 * half), lambda i, j, k: (i, k)),
        w_in_f, pl.BlockSpec((2, tn, half), lambda i, j, k: (k, j, 0)),
        [_sds((s, d), F32)], [mn_blk], (tm, tn), deps=(tok,))[0]
    sum_pool, sum_a_up, sum_b_up, sum_o = _reduce_scatter_finish(fl_b, dh)
    grad_x, dshift1, dscale1, g_norm1 = _norm_backward("norm1_bwd", dh, x2, norm1_w, scale1, dx1)

    dmod = jnp.concatenate([dshift1, dscale1, dgate1, dshift2, dscale2, dgate2], axis=1)
    pieces = [dmod, g_norm1, g_norm2, g_pool_scale, g_qnorm, g_knorm, jnp.full((1, LANES), loss_local, F32)]
    packed_rows = [_rows_of_lanes(p) for p in pieces]
    offsets = [0]
    for p in packed_rows:
        offsets.append(offsets[-1] + p.shape[0])
    packed = jnp.concatenate(packed_rows, axis=0)
    small_all = _all_gather_2d("ag_small", packed)
    fl_in, tok = _reduce_scatter_middle("I", fl_in, small_all, me)
    small_sum = _sum_slots("small_sum", small_all[None], F32)[0]

    def unpack(i, width):
        return small_sum[offsets[i]:offsets[i] + width // LANES].reshape(1, width)

    g_b_ada = unpack(0, N_MOD * d)
    g_norm1_w = unpack(1, d)
    g_norm2_w = unpack(2, d)
    g_pool_scale_w = unpack(3, half)
    g_q_norm_w = unpack(4, HEAD_DIM)
    g_k_norm_w = unpack(5, HEAD_DIM)
    loss = unpack(6, LANES)[0, 0]
    dmod_all = small_all[:, :N_MOD * d // LANES].reshape(N_DEV, N_MOD * d)
    dmod_cols = lax.dynamic_slice_in_dim(dmod_all, me * wa, wa, axis=1)
    g_w_ada = _ada_weight_grad(c_all, dmod_cols, deps=(tok,))[None]


    grads = {
        "w_ada": g_w_ada, "b_ada": g_b_ada, "norm1_w": g_norm1_w,
        "q_norm_w": g_q_norm_w, "k_norm_w": g_k_norm_w,
        "pool_scale": g_pool_scale_w, "norm2_w": g_norm2_w,
    }
    sums = {"w_pool": sum_pool, "w_a_up": sum_a_up, "w_b_up": sum_b_up, "w_o": sum_o,
            "w_ff1": sum_ff1, "w_ff2": sum_ff2}
    weights = {"w_ada": (w_ada, m_w_ada, v_w_ada), "b_ada": (b_ada, m_b_ada, v_b_ada),
               "norm1_w": (norm1_w, m_norm1_w, v_norm1_w), "w_in": (w_in, m_w_in, v_w_in),
               "q_norm_w": (q_norm_w, m_q_norm_w, v_q_norm_w), "k_norm_w": (k_norm_w, m_k_norm_w, v_k_norm_w),
               "w_pool": (w_pool, m_w_pool, v_w_pool), "pool_scale": (pool_scale, m_pool_scale, v_pool_scale),
               "w_a_up": (w_a_up, m_w_a_up, v_w_a_up), "w_b_up": (w_b_up, m_w_b_up, v_w_b_up),
               "w_o": (w_o, m_w_o, v_w_o), "norm2_w": (norm2_w, m_norm2_w, v_norm2_w),
               "w_ff1": (w_ff1, m_w_ff1, v_w_ff1), "w_ff2": (w_ff2, m_w_ff2, v_w_ff2)}
    order = list(weights)
    deltas, new_m, new_v = {}, {}, {}
    def adam(name):
        wt, mt, vt = weights[name]
        shape = wt.shape
        flat = (-1, shape[-1])
        if name in sums:
            own, received = sums[name]
            g, dl, nm, nv = _adamw_summed("adamw_" + name, wt.reshape(flat), own, received,
                                          mt.reshape(flat), vt.reshape(flat))
            grads[name] = g.reshape(shape)
        else:
            dl, nm, nv = _adamw("adamw_" + name, wt.reshape(flat), grads[name].reshape(flat),
                                mt.reshape(flat), vt.reshape(flat))
        deltas[name], new_m[name], new_v[name] = dl.reshape(shape), nm.reshape(shape), nv.reshape(shape)

    others = [n for n in order if n != "w_in"]
    for name in others:
        adam(name)
    sums["w_in"], = _reduce_scatter_finish(fl_in, [deltas[n] for n in others])
    adam("w_in")

    return (loss, grad_x[None], *[grads[n] for n in order], *[deltas[n] for n in order],
            *[new_m[n] for n in order], *[new_v[n] for n in order])
```

```python
import math

import jax
import jax.numpy as jnp
from jax import lax
from jax.experimental import pallas as pl
from jax.experimental.pallas import tpu as pltpu

F32 = jnp.float32
BF16 = jnp.bfloat16
MESH_AXES = ("x", "y", "c")
N_DEV = 8
HEAD_DIM = 128
POOL_WINDOWS = (2, 4, 8, 16)
N_MOD = 6
NORM_EPS = 1e-6
LANES = 128
SUBLANES = 8
VMEM_LIMIT_BYTES = 56 * 1024 * 1024
Q_TILE = 256
K_TILE = 256
POOL_TILE = 512
HEADS_PER_STEP = 8
HEADS_PER_STEP_BWD = 4

ADAM_LR = 0.001
ADAM_B1 = 0.9
ADAM_B2 = 0.999
ADAM_EPS = 1e-08
ADAM_WD = 0.01
ADAM_STEP = 10

_NN = (((1,), (0,)), ((), ()))
_NT = (((1,), (1,)), ((), ()))
_TN = (((0,), (0,)), ((), ()))
_DIMS = {"nn": _NN, "nt": _NT, "tn": _TN}


def _dot(a, b, mode="nn"):
    return lax.dot_general(a, b, _DIMS[mode], preferred_element_type=F32)


def _params(*sem):
    return pltpu.CompilerParams(dimension_semantics=sem, vmem_limit_bytes=VMEM_LIMIT_BYTES)


def _tile(dim, pref, align=SUBLANES):
    for t in range(min(dim, pref), 0, -1):
        if dim % t == 0 and t % align == 0:
            return t
    return dim


def _group_index(axes):
    idx = 0
    for a in axes:
        idx = idx * 2 + lax.axis_index(a)
    return idx


def _peer_device(axes, k):
    coords = {a: lax.axis_index(a) for a in MESH_AXES}
    for pos, a in enumerate(axes):
        if (k >> (len(axes) - 1 - pos)) & 1:
            coords[a] = 1 - coords[a]
    return tuple(coords[a] for a in MESH_AXES)


_AXIS_BIT = {"x": 4, "y": 2, "c": 1}
_ANY = pl.BlockSpec(memory_space=pl.ANY)


def _device_xor(mask):
    return tuple(1 - lax.axis_index(a) if mask & _AXIS_BIT[a] else lax.axis_index(a) for a in MESH_AXES)


def _remote(src, dst, send_sem, recv_sem, mask):
    return pltpu.make_async_remote_copy(src_ref=src, dst_ref=dst, send_sem=send_sem, recv_sem=recv_sem,
                                        device_id=_device_xor(mask), device_id_type=pl.DeviceIdType.MESH)


_CHIP_MASKS = (0, _AXIS_BIT["y"], _AXIS_BIT["x"], _AXIS_BIT["x"] | _AXIS_BIT["y"])


def _add_received(name, own, own_slots, received, out_dtype):
    nj, r, c = received.shape
    tr = _tile(r, max(2 * SUBLANES, (1 << 20) // c), 2 * SUBLANES)

    def body(slots_ref, own_ref, rec_ref, o_ref):
        del slots_ref
        o_ref[...] = (own_ref[...].astype(F32) + rec_ref[...].astype(F32)).astype(o_ref.dtype)

    grid_spec = pltpu.PrefetchScalarGridSpec(
        num_scalar_prefetch=1, grid=(nj, r // tr),
        in_specs=[pl.BlockSpec((None, tr, c), lambda j, i, slots: (slots[j], i, 0)),
                  pl.BlockSpec((None, tr, c), lambda j, i, slots: (j, i, 0))],
        out_specs=pl.BlockSpec((None, tr, c), lambda j, i, slots: (j, i, 0)))
    return pl.pallas_call(
        body, name=name, grid_spec=grid_spec, out_shape=jax.ShapeDtypeStruct((nj, r, c), out_dtype),
        compiler_params=_params("parallel", "parallel"),
    )(own_slots, own, received)


_HBM = pl.BlockSpec(memory_space=pltpu.HBM)
_SEM = pl.BlockSpec(memory_space=pltpu.SEMAPHORE)
_DATAFLOW = pltpu.SideEffectType.DATAFLOW_SIDE_EFFECTING


def _launch_groups(name, groups, deps=()):
    bufs = [b for g_bufs, _ in groups for b in g_bufs]
    specs = [(len(g_bufs), spec) for g_bufs, g_plans in groups for spec in g_plans]
    nb, ns = len(bufs), len(specs)

    def body(*refs):
        sems = refs[nb + len(deps):nb + len(deps) + 2 * ns]
        me = _group_index(MESH_AXES)
        first, which = 0, 0
        for g_bufs, g_plans in groups:
            ins = refs[first:first + len(g_bufs)]
            for _, plan, n_copies in g_plans:
                copies = plan(ins, me)
                assert len(copies) == n_copies
                for n, (src, dst, mask) in enumerate(copies):
                    _remote(src, dst, sems[2 * which].at[n], sems[2 * which + 1].at[n], mask).start()
                which += 1
            first += len(g_bufs)
        refs[-1][...] = jnp.zeros_like(refs[-1])

    sem_shapes = [pltpu.SemaphoreType.DMA((n,)) for _, (_, _, n) in specs for _ in range(2)]
    outs = pl.pallas_call(
        body, name=name,
        out_shape=(*sem_shapes, *[pltpu.HBM(b.shape, b.dtype) for b in bufs],
                   jax.ShapeDtypeStruct((SUBLANES, LANES), F32)),
        in_specs=[_HBM] * nb + [_ANY] * len(deps),
        out_specs=(*[_SEM] * (2 * ns), *[_HBM] * nb, pl.BlockSpec(memory_space=pltpu.VMEM)),
        input_output_aliases={i: 2 * ns + i for i in range(nb)},
        compiler_params=pltpu.CompilerParams(has_side_effects=_DATAFLOW),
    )(*[pltpu.with_memory_space_constraint(b, pltpu.HBM) for b in bufs], *deps)
    flights, first, which = [], 0, 0
    for g_bufs, g_plans in groups:
        through = list(outs[2 * ns + first:2 * ns + first + len(g_bufs)])
        for land_name, plan, n_copies in g_plans:
            flights.append((land_name, plan, n_copies, outs[2 * which], outs[2 * which + 1], through))
            which += 1
        first += len(g_bufs)
    return flights, outs[-1]


def _launch(name, bufs, plan, n_copies, deps=()):
    (flight,), token = _launch_groups(name, [(bufs, [(name, plan, n_copies)])], deps)
    return flight, token


def _land(flight, after, bufs=None):
    name, plan, n_copies, send_sems, recv_sems, launched = flight
    bufs = launched if bufs is None else bufs
    nb = len(bufs)
    after = list(after) if isinstance(after, (list, tuple)) else [after]

    def body(*refs):
        ins = refs[:nb]
        s_sems, r_sems = refs[nb], refs[nb + 1]
        for n, (src, dst, mask) in enumerate(plan(ins, _group_index(MESH_AXES))):
            cp = _remote(src, dst, s_sems.at[n], r_sems.at[n], mask)
            cp.wait_send()
            cp.wait_recv()

    outs = pl.pallas_call(
        body, name=name + "_land",
        out_shape=tuple(pltpu.HBM(b.shape, b.dtype) for b in bufs),
        in_specs=[_HBM] * nb + [_SEM, _SEM] + [_ANY] * len(after), out_specs=tuple([_HBM] * nb),
        input_output_aliases={i: i for i in range(nb)},
        compiler_params=pltpu.CompilerParams(has_side_effects=_DATAFLOW),
    )(*bufs, send_sems, recv_sems, *after)
    return list(outs)


def _plan_gather_ici(phase):
    bx, by = _AXIS_BIT["x"], _AXIS_BIT["y"]

    def plan(refs, me):
        copies = []
        for ref in refs:
            half = ref.shape[1] // 2

            def piece(slot, color, mask, ref=ref, half=half):
                p = ref.at[slot, pl.ds(color * half, half)]
                return (p, p, mask)

            if phase == 0:
                copies += [piece(me, 0, bx), piece(me, 1, by)]
            else:
                copies += [piece(me, 0, by), piece(me ^ bx, 0, by), piece(me, 1, bx), piece(me ^ by, 1, bx)]
        return copies

    return plan


def _plan_d2d(masks):
    def plan(refs, me):
        return [(ref.at[me ^ m], ref.at[me ^ m], _AXIS_BIT["c"]) for ref in refs for m in masks]

    return plan


def _plan_gather_d2d(refs, me):
    return _plan_d2d(_CHIP_MASKS)(refs, me)


def _plan_neighbours(refs, me):
    return [(ref.at[me], ref.at[me], _AXIS_BIT[a]) for ref in refs for a in ("x", "y")]


def _plan_diagonal(refs, me):
    bx, by = _AXIS_BIT["x"], _AXIS_BIT["y"]
    copies = []
    for ref in refs:
        half = ref.shape[1] // 2
        lo = ref.at[me ^ bx, pl.ds(0, half)]
        hi = ref.at[me ^ by, pl.ds(half, half)]
        copies += [(lo, lo, by), (hi, hi, bx)]
    return copies


def _plan_scatter_d2d(refs, me):
    na = len(refs) // 2
    copies = []
    for a in range(na):
        for j, m in enumerate(_CHIP_MASKS):
            copies.append((refs[a].at[me ^ _AXIS_BIT["c"] ^ m], refs[na + a].at[j], _AXIS_BIT["c"]))
    return copies


def _plan_scatter_ici(refs, me):
    del me
    na = len(refs) // 2
    copies = []
    for a in range(na):
        for n, m in enumerate(_CHIP_MASKS[1:]):
            copies.append((refs[a].at[n + 1], refs[na + a].at[n], m))
    return copies


def _with_deps(body, n_in, deps):
    if not deps:
        return body

    def wrapped(*refs):
        return body(*refs[:n_in], *refs[n_in + len(deps):])

    return wrapped


def _reduce_scatter_start(tag, grads):
    lands = [lax.empty((len(_CHIP_MASKS),) + g.shape[1:], g.dtype) for g in grads]
    return _launch("rs%s_d2d" % tag, list(grads) + lands, _plan_scatter_d2d, len(_CHIP_MASKS) * len(grads))


def _reduce_scatter_middle(tag, flight, after, me):
    bufs = _land(flight, after)
    na = len(bufs) // 2
    own_slots = jnp.stack([me ^ m for m in _CHIP_MASKS]).astype(jnp.int32)
    sums = [_add_received("rs%s_add_d2d_%d" % (tag, a), bufs[a], own_slots, bufs[na + a], BF16) for a in range(na)]
    lands = [lax.empty((len(_CHIP_MASKS) - 1,) + h.shape[1:], h.dtype) for h in sums]
    return _launch("rs%s_ici" % tag, sums + lands, _plan_scatter_ici, (len(_CHIP_MASKS) - 1) * na)


def _reduce_scatter_finish(flight, after):
    bufs = _land(flight, after)
    na = len(bufs) // 2
    return [(bufs[a], bufs[na + a]) for a in range(na)]


def _all_gather_2d(name, x, deps=()):
    r, c = x.shape

    def body(x_ref, out_ref, send_sems, recv_sems):
        me = _group_index(MESH_AXES)
        out_ref[me] = x_ref[...]
        copies = []
        for k in range(1, N_DEV):
            cp = pltpu.make_async_remote_copy(
                src_ref=x_ref, dst_ref=out_ref.at[me],
                send_sem=send_sems.at[k - 1], recv_sem=recv_sems.at[k - 1],
                device_id=_peer_device(MESH_AXES, k), device_id_type=pl.DeviceIdType.MESH)
            cp.start()
            copies.append(cp)
        for cp in copies:
            cp.wait()

    vmem = pl.BlockSpec(memory_space=pltpu.VMEM)
    return pl.pallas_call(
        _with_deps(body, 1, deps), name=name, out_shape=jax.ShapeDtypeStruct((N_DEV, r, c), x.dtype),
        in_specs=[vmem] + [_ANY] * len(deps), out_specs=vmem,
        scratch_shapes=[pltpu.SemaphoreType.DMA((N_DEV - 1,)), pltpu.SemaphoreType.DMA((N_DEV - 1,))],
    )(x, *deps)


def _sum_slots(name, buf, out_dtype):
    pre, n, r, c = buf.shape
    tr = _tile(r, max(SUBLANES * 2, (1 << 20) // c))

    def body(b_ref, o_ref):
        acc = b_ref[0].astype(F32)
        for q in range(1, n):
            acc = acc + b_ref[q].astype(F32)
        o_ref[...] = acc.astype(o_ref.dtype)

    return pl.pallas_call(
        body, name=name, grid=(pre, r // tr),
        out_shape=jax.ShapeDtypeStruct((pre, r, c), out_dtype),
        in_specs=[pl.BlockSpec((None, n, tr, c), lambda i, j: (i, 0, j, 0))],
        out_specs=pl.BlockSpec((None, tr, c), lambda i, j: (i, j, 0)),
        compiler_params=_params("parallel", "parallel"),
    )(buf)


def _matmul(name, mode, grid, a, a_spec, b, b_spec, out_shapes, out_specs, acc_shape,
            epilogue=None, extras=(), extra_specs=(), aliases=None, deps=()):
    nk = grid[2]
    n_extra = len(extras)
    n_out = len(out_shapes)

    def finish(acc, extra_refs, out_refs):
        if epilogue is None:
            out_refs[0][...] = acc.astype(out_refs[0].dtype)
        else:
            epilogue(acc, extra_refs, out_refs)

    def product(a_ref, b_ref):
        if len(b_ref.shape) == 2:
            return _dot(a_ref[...], b_ref[...], mode)
        width = a_ref.shape[1] // b_ref.shape[0]
        total = None
        for i in range(b_ref.shape[0]):
            part = _dot(a_ref[:, i * width:(i + 1) * width], b_ref[i], mode)
            total = part if total is None else total + part
        return total

    def body(*refs):
        a_ref, b_ref = refs[0], refs[1]
        extra_refs = refs[2:2 + n_extra]
        out_refs = refs[2 + n_extra:2 + n_extra + n_out]
        if nk == 1:
            finish(product(a_ref, b_ref), extra_refs, out_refs)
            return
        acc_ref = refs[-1]
        k = pl.program_id(2)

        @pl.when(k == 0)
        def _():
            acc_ref[...] = product(a_ref, b_ref)

        @pl.when((k > 0) & (k < nk - 1))
        def _():
            acc_ref[...] += product(a_ref, b_ref)

        @pl.when(k == nk - 1)
        def _():
            finish(acc_ref[...] + product(a_ref, b_ref), extra_refs, out_refs)

    scratch = [] if nk == 1 else [pltpu.VMEM(acc_shape, F32)]
    return pl.pallas_call(
        _with_deps(body, 2 + n_extra, deps), name=name, grid=grid, out_shape=tuple(out_shapes),
        in_specs=[a_spec, b_spec] + list(extra_specs) + [_ANY] * len(deps), out_specs=tuple(out_specs),
        scratch_shapes=scratch, input_output_aliases=aliases or {},
        compiler_params=_params("parallel", "parallel", "arbitrary"),
    )(a, b, *extras, *deps)


def _sds(shape, dtype):
    return jax.ShapeDtypeStruct(tuple(shape), dtype)


def _project_slots(name, h, w_full, slots, out_cols, proj_in=None, deps=()):
    s, d = h.shape
    _, _, wide = w_full.shape
    tm = _tile(s, 1024)
    n_in = 4 if proj_in is not None else 3

    def body(*refs):
        refs[-1][...] = _dot(refs[1][...], refs[2][...])

    grid_spec = pltpu.PrefetchScalarGridSpec(
        num_scalar_prefetch=1, grid=(s // tm, slots.shape[0]),
        in_specs=[pl.BlockSpec((tm, d), lambda i, j, sl: (i, 0)),
                  pl.BlockSpec((None, d, wide), lambda i, j, sl: (sl[j], 0, 0))]
        + [_ANY] * (n_in - 3 + len(deps)),
        out_specs=pl.BlockSpec((tm, wide), lambda i, j, sl: (i, sl[j])))
    extra = ([proj_in] if proj_in is not None else []) + list(deps)
    return pl.pallas_call(
        body, name=name, grid_spec=grid_spec, out_shape=_sds((s, out_cols), F32),
        input_output_aliases={3: 0} if proj_in is not None else {},
        compiler_params=_params("parallel", "arbitrary"),
    )(slots, h, w_full, *extra)


def _ada_forward(c_all, w_ada, b_shard):
    nb, d = c_all.shape
    w = w_ada.shape[1]
    tn = _tile(w, 512)

    def body(c_ref, w_ref, b_ref, o_ref):
        cv = c_ref[...]
        sc = cv * jax.nn.sigmoid(cv)
        o_ref[...] = jnp.dot(sc, w_ref[...], precision=lax.Precision.HIGHEST,
                             preferred_element_type=F32) + b_ref[...]

    return pl.pallas_call(
        body, name="ada_fwd", grid=(w // tn,), out_shape=_sds((nb, w), F32),
        in_specs=[pl.BlockSpec((nb, d), lambda j: (0, 0)), pl.BlockSpec((d, tn), lambda j: (0, j)),
                  pl.BlockSpec((1, tn), lambda j: (0, j))],
        out_specs=pl.BlockSpec((nb, tn), lambda j: (0, j)),
        compiler_params=_params("parallel"),
    )(c_all, w_ada, b_shard)


def _ada_weight_grad(c_all, dmod_cols, deps=()):
    nb, d = c_all.shape
    w = dmod_cols.shape[1]
    tn = _tile(w, 512)

    def body(c_ref, g_ref, o_ref):
        cv = c_ref[...]
        sc = cv * jax.nn.sigmoid(cv)
        o_ref[...] = lax.dot_general(sc, g_ref[...], _TN, precision=lax.Precision.HIGHEST,
                                     preferred_element_type=F32)

    return pl.pallas_call(
        _with_deps(body, 2, deps), name="ada_wgrad", grid=(w // tn,), out_shape=_sds((d, w), F32),
        in_specs=[pl.BlockSpec((nb, d), lambda j: (0, 0)), pl.BlockSpec((nb, tn), lambda j: (0, j))]
        + [_ANY] * len(deps),
        out_specs=pl.BlockSpec((d, tn), lambda j: (0, j)),
        compiler_params=_params("parallel"),
    )(c_all, dmod_cols, *deps)


def _norm_forward(name, x, norm_w, scale, shift, deps=()):
    s, d = x.shape
    tm = _tile(s, 256)

    def body(x_ref, w_ref, sc_ref, sh_ref, h_ref):
        xv = x_ref[...]
        r = lax.rsqrt(jnp.mean(xv * xv, axis=-1, keepdims=True) + NORM_EPS)
        h = (xv * r * w_ref[...]) * (1.0 + sc_ref[...]) + sh_ref[...]
        h_ref[...] = h.astype(BF16)

    vec = pl.BlockSpec((1, d), lambda i: (0, 0))
    row = pl.BlockSpec((tm, d), lambda i: (i, 0))
    return pl.pallas_call(
        _with_deps(body, 4, deps), name=name, grid=(s // tm,), out_shape=_sds((s, d), BF16),
        in_specs=[row, vec, vec, vec] + [_ANY] * len(deps), out_specs=row, compiler_params=_params("parallel"),
    )(x, norm_w, scale, shift, *deps)


def _norm_backward(name, dh, x, norm_w, scale, dres, gated=None, deps=()):
    s, d = x.shape
    tm = _tile(s, 256)
    n_in = 7 if gated else 5

    def body(*refs):
        dh_ref, x_ref, w_ref, sc_ref, dres_ref = refs[:5]
        dx_ref, dshift_ref, dscale_ref, dw_ref = refs[n_in:n_in + 4]
        sums = (dshift_ref, dscale_ref, dw_ref) + ((refs[n_in + 5],) if gated else ())

        @pl.when(pl.program_id(0) == 0)
        def _():
            for ref in sums:
                ref[...] = jnp.zeros_like(ref)

        xv = x_ref[...]
        g = dh_ref[...]
        r = lax.rsqrt(jnp.mean(xv * xv, axis=-1, keepdims=True) + NORM_EPS)
        n = xv * r
        gain = 1.0 + sc_ref[...]
        gn = g * n
        dshift_ref[...] += jnp.sum(g, axis=0, keepdims=True)
        dscale_ref[...] += jnp.sum(gn, axis=0, keepdims=True) * w_ref[...]
        dw_ref[...] += jnp.sum(gn, axis=0, keepdims=True) * gain
        dn = g * (w_ref[...] * gain)
        dx = dres_ref[...] + r * (dn - n * jnp.mean(dn * n, axis=-1, keepdims=True))
        dx_ref[...] = dx
        if gated:
            gate_ref, other_ref = refs[5:7]
            refs[n_in + 4][...] = (dx * gate_ref[...]).astype(BF16)
            refs[n_in + 5][...] += jnp.sum(dx * other_ref[...].astype(F32), axis=0, keepdims=True)

    vec = pl.BlockSpec((1, d), lambda i: (0, 0))
    row = pl.BlockSpec((tm, d), lambda i: (i, 0))
    vec_out = _sds((1, d), F32)
    return pl.pallas_call(
        _with_deps(body, n_in, deps), name=name, grid=(s // tm,),
        out_shape=(_sds((s, d), F32), vec_out, vec_out, vec_out) + ((_sds((s, d), BF16), vec_out) if gated else ()),
        in_specs=[row, row, vec, vec, row] + ([vec, row] if gated else []) + [_ANY] * len(deps),
        out_specs=(row, vec, vec, vec) + ((row, vec) if gated else ()),
        compiler_params=_params("arbitrary"),
    )(dh, x, norm_w, scale, dres, *(gated or ()), *deps)


def _split_bf16(v):
    hi = v.astype(BF16)
    lo = (v - hi.astype(F32)).astype(BF16)
    return hi, lo


def _pool_forward(proj, w_pool, pool_scale, deps=()):
    s = proj.shape[0]
    g_n, cg, _ = w_pool.shape
    t = POOL_TILE
    nt = s // t

    def body(cur_ref, prev_ref, wp_ref, sc_ref, pooled_ref, ya_ref):
        g = pl.program_id(0)
        ti = pl.program_id(1)
        win = jnp.left_shift(2, g)
        row = lax.broadcasted_iota(jnp.int32, (t, t), 0)
        col = lax.broadcasted_iota(jnp.int32, (t, t), 1)
        lag = row - col
        band_cur = ((lag >= 0) & (lag < win)).astype(BF16)
        band_prev = ((lag + t < win) & (ti > 0)).astype(BF16)
        u = cur_ref[...]
        u_hi, u_lo = _split_bf16(u)
        p_hi, p_lo = _split_bf16(prev_ref[...])
        wsum = (_dot(band_cur, u_hi) + _dot(band_cur, u_lo)
                + _dot(band_prev, p_hi) + _dot(band_prev, p_lo))
        tok = ti * t + lax.broadcasted_iota(jnp.int32, (t, 1), 0)
        count = jnp.minimum(tok + 1, win).astype(F32)
        pooled = (wsum / count - u).astype(BF16)
        pooled_ref[...] = pooled
        ya_ref[...] = (_dot(pooled, wp_ref[...]) * sc_ref[...]).astype(BF16)

    blk = pl.BlockSpec((t, cg), lambda g, i: (i, g))
    return pl.pallas_call(
        _with_deps(body, 4, deps), name="pool_fwd", grid=(g_n, nt),
        out_shape=(_sds((s, g_n * cg), BF16), _sds((s, g_n * cg), BF16)),
        in_specs=[blk, pl.BlockSpec((t, cg), lambda g, i: (jnp.maximum(i - 1, 0), g)),
                  pl.BlockSpec((None, cg, cg), lambda g, i: (g, 0, 0)),
                  pl.BlockSpec((1, cg), lambda g, i: (0, g))] + [_ANY] * len(deps),
        out_specs=(blk, blk), compiler_params=_params("parallel", "parallel"),
    )(proj, proj, w_pool, pool_scale, *deps)


def _pool_backward(dya, pooled, w_pool, pool_scale, dproj):
    s = dya.shape[0]
    g_n, cg, _ = w_pool.shape
    t = POOL_TILE
    nt = s // t

    def body(dya_ref, dya_next_ref, pooled_ref, wp_ref, sc_ref, dproj_in, du_ref, gw_ref, gs_ref):
        del dproj_in
        g = pl.program_id(0)
        ti = pl.program_id(1)

        @pl.when(ti == 0)
        def _():
            gw_ref[...] = jnp.zeros_like(gw_ref)
            gs_ref[...] = jnp.zeros_like(gs_ref)

        win = jnp.left_shift(2, g)
        wp = wp_ref[...]
        sc = sc_ref[...]
        pooled_v = pooled_ref[...]
        dya_v = dya_ref[...].astype(F32)
        mixed = _dot(pooled_v, wp)
        gs_ref[...] += jnp.sum(dya_v * mixed, axis=0, keepdims=True)
        dmixed = (dya_v * sc).astype(BF16)
        gw_ref[...] += _dot(pooled_v, dmixed, "tn")
        dpooled = _dot(dmixed, wp, "nt")
        dmixed_next = (dya_next_ref[...].astype(F32) * sc).astype(BF16)
        dpooled_next = _dot(dmixed_next, wp, "nt")
        tok = ti * t + lax.broadcasted_iota(jnp.int32, (t, 1), 0)
        e_cur = dpooled / jnp.minimum(tok + 1, win).astype(F32)
        e_next = dpooled_next / jnp.minimum(tok + t + 1, win).astype(F32)
        row = lax.broadcasted_iota(jnp.int32, (t, t), 0)
        col = lax.broadcasted_iota(jnp.int32, (t, t), 1)
        lead = col - row
        band_cur = ((lead >= 0) & (lead < win)).astype(BF16)
        band_next = ((lead + t < win) & (ti < nt - 1)).astype(BF16)
        c_hi, c_lo = _split_bf16(e_cur)
        n_hi, n_lo = _split_bf16(e_next)
        du = (_dot(band_cur, c_hi) + _dot(band_cur, c_lo)
              + _dot(band_next, n_hi) + _dot(band_next, n_lo)) - dpooled
        du_ref[...] = du.astype(BF16)

    blk = pl.BlockSpec((t, cg), lambda g, i: (i, g))
    du, gw, gs = pl.pallas_call(
        body, name="pool_bwd", grid=(g_n, nt),
        out_shape=(_sds(dproj.shape, BF16), _sds((g_n, cg, cg), F32), _sds((1, g_n * cg), F32)),
        in_specs=[blk, pl.BlockSpec((t, cg), lambda g, i: (jnp.minimum(i + 1, nt - 1), g)), blk,
                  pl.BlockSpec((None, cg, cg), lambda g, i: (g, 0, 0)),
                  pl.BlockSpec((1, cg), lambda g, i: (0, g)),
                  pl.BlockSpec(memory_space=pl.ANY)],
        out_specs=(blk, pl.BlockSpec((None, cg, cg), lambda g, i: (g, 0, 0)),
                   pl.BlockSpec((1, cg), lambda g, i: (0, g))),
        input_output_aliases={5: 0}, compiler_params=_params("parallel", "arbitrary"),
    )(dya, dya, pooled, w_pool, pool_scale, dproj)
    return du, gw, gs


def _qkv_prepare(proj, q_norm_w, k_norm_w, width, deps=()):
    s = proj.shape[0]
    tm = _tile(s, 256)
    heads = width // HEAD_DIM

    def body(q_ref, k_ref, v_ref, qw_ref, kw_ref, qn_ref, kn_ref, vb_ref):
        for h in range(heads):
            cols = slice(h * HEAD_DIM, (h + 1) * HEAD_DIM)
            for src, w_ref, dst in ((q_ref, qw_ref, qn_ref), (k_ref, kw_ref, kn_ref)):
                v = src[:, cols]
                r = lax.rsqrt(jnp.mean(v * v, axis=-1, keepdims=True) + NORM_EPS)
                dst[:, cols] = (v * r * w_ref[...]).astype(BF16)
        vb_ref[...] = v_ref[...].astype(BF16)

    vec = pl.BlockSpec((1, HEAD_DIM), lambda i: (0, 0))
    out_spec = pl.BlockSpec((tm, width), lambda i: (i, 0))
    return pl.pallas_call(
        _with_deps(body, 5, deps), name="qkv_prep", grid=(s // tm,),
        out_shape=(_sds((s, width), BF16),) * 3,
        in_specs=[pl.BlockSpec((tm, width), lambda i: (i, 1)), pl.BlockSpec((tm, width), lambda i: (i, 2)),
                  pl.BlockSpec((tm, width), lambda i: (i, 3)), vec, vec] + [_ANY] * len(deps),
        out_specs=(out_spec,) * 3, compiler_params=_params("parallel"),
    )(proj, proj, proj, q_norm_w, k_norm_w, *deps)


def _qk_norm_backward(name, dn, proj, col_block, norm_w, dproj, width, deps=()):
    s = proj.shape[0]
    tm = _tile(s, 256)
    heads = width // HEAD_DIM

    def body(dn_ref, q_ref, w_ref, dproj_in, dq_ref, gw_ref):
        del dproj_in

        @pl.when(pl.program_id(0) == 0)
        def _():
            gw_ref[...] = jnp.zeros_like(gw_ref)

        wv = w_ref[...]
        gw = jnp.zeros((1, HEAD_DIM), F32)
        for h in range(heads):
            cols = slice(h * HEAD_DIM, (h + 1) * HEAD_DIM)
            v = q_ref[:, cols]
            g = dn_ref[:, cols]
            r = lax.rsqrt(jnp.mean(v * v, axis=-1, keepdims=True) + NORM_EPS)
            n = v * r
            gw = gw + jnp.sum(g * n, axis=0, keepdims=True)
            gn = g * wv
            dq_ref[:, cols] = (r * (gn - n * jnp.mean(gn * n, axis=-1, keepdims=True))).astype(BF16)
        gw_ref[...] += gw

    blk = pl.BlockSpec((tm, width), lambda i: (i, col_block))
    return pl.pallas_call(
        _with_deps(body, 4, deps), name=name, grid=(s // tm,),
        out_shape=(_sds(dproj.shape, BF16), _sds((1, HEAD_DIM), F32)),
        in_specs=[pl.BlockSpec((tm, width), lambda i: (i, 0)), blk,
                  pl.BlockSpec((1, HEAD_DIM), lambda i: (0, 0)), pl.BlockSpec(memory_space=pl.ANY)]
        + [_ANY] * len(deps),
        out_specs=(blk, pl.BlockSpec((1, HEAD_DIM), lambda i: (0, 0))),
        input_output_aliases={3: 0}, compiler_params=_params("arbitrary"),
    )(dn, proj, norm_w, dproj, *deps)


def _strict_upper(n):
    row = lax.broadcasted_iota(jnp.int32, (n, n), 0)
    col = lax.broadcasted_iota(jnp.int32, (n, n), 1)
    return (row > col).astype(BF16)


def _strict_lower(n):
    row = lax.broadcasted_iota(jnp.int32, (n, n), 0)
    col = lax.broadcasted_iota(jnp.int32, (n, n), 1)
    return (row < col).astype(BF16)


def _cumulate(v, tri):
    return _dot(v.astype(BF16), tri)


def _log_sigmoid(z):
    return jnp.minimum(z, 0.0) - jnp.log(1.0 + jnp.exp(-jnp.abs(z)))


def _attention_forward(qn, kn, vb):
    s, width = qn.shape
    heads = width // HEAD_DIM
    tq, tk = Q_TILE, K_TILE
    hp = min(HEADS_PER_STEP, heads)
    assert tq == tk and s % tq == 0 and heads % hp == 0
    scale = 1.0 / math.sqrt(HEAD_DIM)

    def body(q_ref, k_ref, v_ref, o_ref, a_scr):
        qi = pl.program_id(1)
        upper = _strict_upper(tk)
        causal = lax.broadcasted_iota(jnp.int32, (tq, tk), 1) < lax.broadcasted_iota(jnp.int32, (tq, tk), 0)
        head_cols = [slice(u * HEAD_DIM, (u + 1) * HEAD_DIM) for u in range(hp)]

        def weights(kb, carry, masked):
            rows = pl.ds(pl.multiple_of(kb * tk, tk), tk)
            out = []
            for u, cols in enumerate(head_cols):
                later = carry[u]
                z = _dot(q_ref[:, cols], k_ref[rows, cols], "nt") * scale
                log_beta = _log_sigmoid(z)
                l = log_beta - z
                if masked:
                    l = jnp.where(causal, l, 0.0)
                a = jnp.exp(log_beta + _cumulate(l, upper) + later)
                if masked:
                    a = jnp.where(causal, a, 0.0)
                a_scr[u, :, rows] = a.astype(BF16)
                out.append(later + jnp.sum(l, axis=1, keepdims=True))
            return tuple(out)

        later = weights(qi, tuple(jnp.zeros((tq, 1), F32) for _ in range(hp)), True)
        lax.fori_loop(0, qi, lambda i, c: weights(qi - 1 - i, c, False), later)

        def mix(kb, accs):
            rows = pl.ds(pl.multiple_of(kb * tk, tk), tk)
            return tuple(acc + _dot(a_scr[u, :, rows], v_ref[rows, cols])
                         for u, (acc, cols) in enumerate(zip(accs, head_cols)))

        accs = lax.fori_loop(0, qi + 1, mix, tuple(jnp.zeros((tq, HEAD_DIM), F32) for _ in range(hp)))
        for acc, cols in zip(accs, head_cols):
            o_ref[:, cols] = acc.astype(BF16)

    full = pl.BlockSpec((s, hp * HEAD_DIM), lambda h, i: (0, h))
    blk = pl.BlockSpec((tq, hp * HEAD_DIM), lambda h, i: (i, h))
    return pl.pallas_call(
        body, name="attn_fwd", grid=(heads // hp, s // tq), out_shape=_sds((s, width), BF16),
        in_specs=[blk, full, full], out_specs=blk, scratch_shapes=[pltpu.VMEM((hp, tq, s), BF16)],
        compiler_params=_params("parallel", "parallel"),
    )(qn, kn, vb)


def _attention_backward(qn, kn, vb, dout, dproj, v_col_block, deps=()):
    s, width = qn.shape
    heads = width // HEAD_DIM
    tq, tk = Q_TILE, K_TILE
    hp = min(HEADS_PER_STEP_BWD, heads)
    nq = s // tq
    scale = 1.0 / math.sqrt(HEAD_DIM)
    v_block0 = v_col_block * (heads // hp)

    def body(q_ref, k_ref, v_ref, do_ref, dproj_in, dq_ref, dk_ref, dv_ref,
             a_scr, lb_scr, dz_scr, dkt_scr, dvt_scr):
        del dproj_in
        qi = pl.program_id(1)

        @pl.when(qi == 0)
        def _():
            dkt_scr[...] = jnp.zeros_like(dkt_scr)
            dvt_scr[...] = jnp.zeros_like(dvt_scr)

        upper = _strict_upper(tk)
        lower = _strict_lower(tk)
        causal = lax.broadcasted_iota(jnp.int32, (tq, tk), 1) < lax.broadcasted_iota(jnp.int32, (tq, tk), 0)
        head_cols = [slice(u * HEAD_DIM, (u + 1) * HEAD_DIM) for u in range(hp)]

        def weights(kb, carry, masked):
            rows = pl.ds(pl.multiple_of(kb * tk, tk), tk)
            out = []
            for u, cols in enumerate(head_cols):
                later = carry[u]
                z = _dot(q_ref[:, cols], k_ref[rows, cols], "nt") * scale
                log_beta = _log_sigmoid(z)
                l = log_beta - z
                if masked:
                    l = jnp.where(causal, l, 0.0)
                a = jnp.exp(log_beta + _cumulate(l, upper) + later)
                if masked:
                    a = jnp.where(causal, a, 0.0)
                a_scr[u, :, rows] = a
                lb_scr[u, :, rows] = log_beta
                out.append(later + jnp.sum(l, axis=1, keepdims=True))
            return tuple(out)

        zeros = tuple(jnp.zeros((tq, 1), F32) for _ in range(hp))
        later = weights(qi, zeros, True)
        lax.fori_loop(0, qi, lambda i, c: weights(qi - 1 - i, c, False), later)

        q_t = [jnp.transpose(q_ref[:, cols].astype(F32)).astype(BF16) for cols in head_cols]
        do_t = [jnp.transpose(do_ref[:, cols].astype(F32)).astype(BF16) for cols in head_cols]

        def scores(kb, carry, masked):
            rows = pl.ds(pl.multiple_of(kb * tk, tk), tk)
            out = []
            for u, cols in enumerate(head_cols):
                before = carry[u]
                beta = jnp.exp(lb_scr[u, :, rows])
                g = a_scr[u, :, rows] * _dot(do_ref[:, cols], v_ref[rows, cols], "nt")
                p = _cumulate(g, lower) + before
                dz = g - (g + p) * beta
                if masked:
                    dz = jnp.where(causal, dz, 0.0)
                dz_scr[u, :, rows] = (dz * scale).astype(BF16)
                out.append(before + jnp.sum(g, axis=1, keepdims=True))
            return tuple(out)

        before = lax.fori_loop(0, qi, lambda i, c: scores(i, c, False), zeros)
        scores(qi, before, True)

        def products(kb, dqs):
            rows = pl.ds(pl.multiple_of(kb * tk, tk), tk)
            out = []
            for u, cols in enumerate(head_cols):
                dz = dz_scr[u, :, rows]
                dkt_scr[cols, rows] += _dot(q_t[u], dz)
                dvt_scr[cols, rows] += _dot(do_t[u], a_scr[u, :, rows].astype(BF16))
                out.append(dqs[u] + _dot(dz, k_ref[rows, cols]))
            return tuple(out)

        dqs = lax.fori_loop(0, qi + 1, products, tuple(jnp.zeros((tq, HEAD_DIM), F32) for _ in range(hp)))
        for u, cols in enumerate(head_cols):
            dq_ref[:, cols] = dqs[u]

        @pl.when(qi == nq - 1)
        def _():
            dk_ref[...] = jnp.transpose(dkt_scr[...])
            dv_ref[...] = jnp.transpose(dvt_scr[...]).astype(BF16)

    wide = hp * HEAD_DIM
    full = pl.BlockSpec((s, wide), lambda h, i: (0, h))
    blk = pl.BlockSpec((tq, wide), lambda h, i: (i, h))
    return pl.pallas_call(
        _with_deps(body, 5, deps), name="attn_bwd", grid=(heads // hp, nq),
        out_shape=(_sds((s, width), F32), _sds((s, width), F32), _sds(dproj.shape, BF16)),
        in_specs=[blk, full, full, blk, pl.BlockSpec(memory_space=pl.ANY)] + [_ANY] * len(deps),
        out_specs=(blk, full, pl.BlockSpec((s, wide), lambda h, i: (0, v_block0 + h))),
        scratch_shapes=[pltpu.VMEM((hp, tq, s), F32), pltpu.VMEM((hp, tq, s), F32), pltpu.VMEM((hp, tq, s), BF16),
                        pltpu.VMEM((wide, s), F32), pltpu.VMEM((wide, s), F32)],
        input_output_aliases={4: 2}, compiler_params=_params("parallel", "arbitrary"),
    )(qn, kn, vb, dout, dproj, *deps)


def _place_columns(name, src, dst, col_block):
    s, w = src.shape
    tm = _tile(s, 512)

    def body(src_ref, dst_in, out_ref):
        del dst_in
        out_ref[...] = src_ref[...]

    return pl.pallas_call(
        body, name=name, grid=(s // tm,), out_shape=_sds(dst.shape, dst.dtype),
        in_specs=[pl.BlockSpec((tm, w), lambda i: (i, 0)), pl.BlockSpec(memory_space=pl.ANY)],
        out_specs=pl.BlockSpec((tm, w), lambda i: (i, col_block)),
        input_output_aliases={1: 0}, compiler_params=_params("parallel"),
    )(src, dst)


def _cast_into_slot(name, x, slot):
    r, c = x.shape
    tr = _tile(r, max(SUBLANES * 2, (1 << 20) // c), SUBLANES * 2)

    def body(slot_ref, x_ref, o_ref):
        del slot_ref
        o_ref[...] = x_ref[...].astype(BF16)

    grid_spec = pltpu.PrefetchScalarGridSpec(
        num_scalar_prefetch=1, grid=(r // tr,),
        in_specs=[pl.BlockSpec((tr, c), lambda i, slot_ref: (i, 0))],
        out_specs=pl.BlockSpec((None, tr, c), lambda i, slot_ref: (slot_ref[0], i, 0)))
    return pl.pallas_call(
        body, name=name, grid_spec=grid_spec, out_shape=_sds((N_DEV, r, c), BF16),
        compiler_params=_params("parallel"),
    )(slot, x)


def _adamw_update(gv, w_ref, m_ref, v_ref, d_ref, nm_ref, nv_ref):
    c1 = 1.0 - ADAM_B1 ** ADAM_STEP
    c2 = 1.0 - ADAM_B2 ** ADAM_STEP
    nm = ADAM_B1 * m_ref[...] + (1.0 - ADAM_B1) * gv
    nv = ADAM_B2 * v_ref[...] + (1.0 - ADAM_B2) * (gv * gv)
    d_ref[...] = -ADAM_LR * ((nm / c1) / (jnp.sqrt(nv / c2) + ADAM_EPS) + ADAM_WD * w_ref[...])
    nm_ref[...] = nm
    nv_ref[...] = nv


def _adamw(name, w, g, m, v):
    r, c = w.shape
    tr = _tile(r, max(SUBLANES, (1 << 19) // c))

    def body(w_ref, g_ref, m_ref, v_ref, d_ref, nm_ref, nv_ref):
        _adamw_update(g_ref[...], w_ref, m_ref, v_ref, d_ref, nm_ref, nv_ref)

    blk = pl.BlockSpec((tr, c), lambda i: (i, 0))
    return pl.pallas_call(
        body, name=name, grid=(r // tr,), out_shape=(_sds((r, c), F32),) * 3,
        in_specs=[blk] * 4, out_specs=(blk,) * 3, compiler_params=_params("parallel"),
    )(w, g, m, v)


def _adamw_summed(name, w, own, received, m, v):
    r, c = w.shape
    nj = received.shape[0]
    tr = _tile(r, max(2 * SUBLANES, (1 << 19) // c), 2 * SUBLANES)

    def body(w_ref, own_ref, rec_ref, m_ref, v_ref, g_ref, d_ref, nm_ref, nv_ref):
        gv = own_ref[...].astype(F32)
        for j in range(nj):
            gv = gv + rec_ref[j].astype(F32)
        g_ref[...] = gv
        _adamw_update(gv, w_ref, m_ref, v_ref, d_ref, nm_ref, nv_ref)

    blk = pl.BlockSpec((tr, c), lambda i: (i, 0))
    return pl.pallas_call(
        body, name=name, grid=(r // tr,), out_shape=(_sds((r, c), F32),) * 4,
        in_specs=[blk, pl.BlockSpec((None, tr, c), lambda i: (0, i, 0)),
                  pl.BlockSpec((nj, tr, c), lambda i: (0, i, 0)), blk, blk],
        out_specs=(blk,) * 4, compiler_params=_params("parallel"),
    )(w, own, received, m, v)


def _rows_of_lanes(v):
    rows = v.shape[1] // LANES
    out = v.reshape(rows, LANES)
    pad = (-rows) % SUBLANES
    if pad:
        out = jnp.pad(out, ((0, pad), (0, 0)))
    return out


def kernel(x, c, w_ada, b_ada, norm1_w, w_in, q_norm_w, k_norm_w, w_pool, pool_scale, w_a_up, w_b_up, w_o, norm2_w, w_ff1, w_ff2, loss_target, m_w_ada, m_b_ada, m_norm1_w, m_w_in, m_q_norm_w, m_k_norm_w, m_w_pool, m_pool_scale, m_w_a_up, m_w_b_up, m_w_o, m_norm2_w, m_w_ff1, m_w_ff2, v_w_ada, v_b_ada, v_norm1_w, v_w_in, v_q_norm_w, v_k_norm_w, v_w_pool, v_pool_scale, v_w_a_up, v_w_b_up, v_w_o, v_norm2_w, v_w_ff1, v_w_ff2):
    _, s, d = x.shape
    half = d // 2
    d8 = d // N_DEV
    n_groups = len(POOL_WINDOWS)
    cg = half // n_groups
    me = _group_index(MESH_AXES)

    x2 = x[0]
    target = loss_target[0]

    my_slot = jnp.reshape(me, (1,)).astype(jnp.int32)

    def cast(i, t):
        return _cast_into_slot("cast_w%d" % i, t, my_slot)

    def gather_start(tag, bufs, phase):
        if phase < 2:
            return _launch("ag%s_ici%d" % (tag, phase), bufs, _plan_gather_ici(phase), (2, 4)[phase] * len(bufs))
        return _launch("ag%s_d2d" % tag, bufs, _plan_gather_d2d, len(_CHIP_MASKS) * len(bufs))

    def gather_plans(tag, n_bufs, phase):
        if phase < 2:
            return ("ag%s_ici%d" % (tag, phase), _plan_gather_ici(phase), (2, 4)[phase] * n_bufs)
        return ("ag%s_d2d" % tag, _plan_gather_d2d, len(_CHIP_MASKS) * n_bufs)

    bx, by, bc = _AXIS_BIT["x"], _AXIS_BIT["y"], _AXIS_BIT["c"]
    buf_a = [cast(0, w_in[0])]

    c_all = _all_gather_2d("ag_c", c.reshape(d // LANES, LANES), deps=tuple(buf_a)).reshape(N_DEV, d)
    wa = w_ada.shape[2]
    b_shard = lax.dynamic_slice_in_dim(b_ada, me * wa, wa, axis=1)
    mod_part = _ada_forward(c_all, w_ada[0], b_shard)
    mod_all = _all_gather_2d("ag_mod", mod_part.reshape(N_DEV * wa // LANES, LANES))
    mod_all = mod_all.reshape(N_DEV, N_DEV, wa)
    mod = lax.dynamic_slice_in_dim(mod_all, me, 1, axis=1).reshape(1, N_MOD * d)
    shift1, scale1, gate1, shift2, scale2, gate2 = [mod[:, i * d:(i + 1) * d] for i in range(N_MOD)]
    fl_a, tok = _launch("agA_near", buf_a, _plan_neighbours, 2, deps=(mod_all,))
    buf_b = [cast(1, w_pool[0].reshape(-1, cg)), cast(2, w_a_up[0]), cast(3, w_b_up[0]), cast(4, w_o[0])]
    buf_c = [cast(5, w_ff1[0])]
    buf_e = [cast(6, w_ff2[0])]

    h = _norm_forward("norm1_fwd", x2, norm1_w, scale1, shift1, deps=(tok,))
    buf_a = _land(fl_a, [h] + buf_b + buf_c + buf_e)
    near = (0, bx, by)
    (fl_far, fl_near), tok = _launch_groups(
        "agA_far", [(buf_a, [("agA_far", _plan_diagonal, 2), ("agA_near_d2d", _plan_d2d(near), len(near))])])
    (fl_b, fl_c, fl_e), tok = _launch_groups(
        "agBCE_ici0", [(buf_b, [gather_plans("B", len(buf_b), 0)]), (buf_c, [gather_plans("C", 1, 0)]),
                       (buf_e, [gather_plans("E", 1, 0)])])

    tm = _tile(s, 1024)
    tk = _tile(d, 2048)
    te = _tile(d, 512)

    own_slots = jnp.stack([me ^ m for m in near]).astype(jnp.int32)
    far_slots = jnp.stack([me ^ bx ^ by ^ f for f in (0, bc)]).astype(jnp.int32)
    buf_a = fl_near[-1]
    proj = _project_slots("proj_own", h, buf_a[0], own_slots, 4 * d, deps=(tok,))
    buf_a = _land(fl_near, proj, bufs=buf_a)
    proj = _project_slots("proj_sibling", h, buf_a[0], own_slots ^ bc, 4 * d, proj_in=proj)
    buf_a = _land(fl_far, proj, bufs=buf_a)
    fl_a, tok = _launch("agA_far_d2d", buf_a, _plan_d2d((bx | by,)), 1)
    w_in_f, = _land(fl_a, tok)
    proj = _project_slots("proj_far", h, w_in_f, far_slots, 4 * d, proj_in=proj)
    buf_b = _land(fl_b, proj)
    buf_c = _land(fl_c, proj)
    (fl_b, fl_c), tok = _launch_groups(
        "agBC_ici1", [(buf_b, [gather_plans("B", len(buf_b), 1)]), (buf_c, [gather_plans("C", 1, 1)])])

    qn, kn, vb = _qkv_prepare(proj, q_norm_w, k_norm_w, half, deps=(tok,))
    attn = _attention_forward(qn, kn, vb)
    buf_b = _land(fl_b, attn)
    buf_e = _land(fl_e, attn)
    (fl_b, fl_e), tok = _launch_groups(
        "agB_d2d_E_ici1", [(buf_b, [gather_plans("B", len(buf_b), 2)]), (buf_e, [gather_plans("E", 1, 1)])])
    w_pool_f, w_a_f, w_b_f, w_o_f = _land(fl_b, tok)
    rows_pool = cg // N_DEV
    w_pool_f = w_pool_f.reshape(N_DEV, n_groups, rows_pool, cg).transpose(1, 0, 2, 3).reshape(n_groups, cg, cg)
    w_o_f = w_o_f.reshape(d, d)
    pooled, ya_in = _pool_forward(proj, w_pool_f, pool_scale)

    def merge_epilogue(ga_ref, gb_ref, ya, yb, out_refs):
        merged_ref, ya_ref, yb_ref = out_refs
        merged = jax.nn.sigmoid(ga_ref[...]) * ya + jax.nn.sigmoid(gb_ref[...]) * yb
        merged_ref[...] = merged.astype(BF16)
        ya_ref[...] = ya.astype(BF16)
        yb_ref[...] = yb.astype(BF16)

    def up_body(a1_ref, b1_ref, a2_ref, b2_ref, ga_ref, gb_ref, *out_refs):
        merge_epilogue(ga_ref, gb_ref, _dot(a1_ref[...], b1_ref[...]), _dot(a2_ref[...], b2_ref[...]), out_refs)

    ga_blk0 = 2 * d // d8
    gb_blk0 = 3 * d // d8
    tu = s
    a_spec = pl.BlockSpec((tu, half), lambda i, j: (i, 0))
    wup_spec = pl.BlockSpec((None, half, d8), lambda i, j: (j, 0, 0))
    o_blk = pl.BlockSpec((tu, d8), lambda i, j: (i, j))
    merged, y_a, y_b = pl.pallas_call(
        up_body, name="up_merge", grid=(s // tu, N_DEV), out_shape=(_sds((s, d), BF16),) * 3,
        in_specs=[a_spec, wup_spec, a_spec, wup_spec,
                  pl.BlockSpec((tu, d8), lambda i, j: (i, ga_blk0 + j)),
                  pl.BlockSpec((tu, d8), lambda i, j: (i, gb_blk0 + j))],
        out_specs=(o_blk,) * 3, compiler_params=_params("parallel", "parallel"),
    )(ya_in, w_a_f, attn, w_b_f, proj, proj)
    buf_c = _land(fl_c, merged)
    fl_c, tok_c = gather_start("C", buf_c, 2)

    tn = _tile(d, 1024)

    def oproj_epilogue(acc, extra_refs, out_refs):
        x_ref, g_ref = extra_refs
        x1_ref, o_ref = out_refs
        x1_ref[...] = x_ref[...] + g_ref[...] * acc
        o_ref[...] = acc.astype(BF16)

    mn_blk = pl.BlockSpec((tm, tn), lambda i, j, k: (i, j))
    e_blk = pl.BlockSpec((tm, te), lambda i, j, k: (i, j))
    e_vec = pl.BlockSpec((1, te), lambda i, j, k: (0, j))
    s_blk = pl.BlockSpec((s, te), lambda i, j, k: (i, j))
    x1, o_act = _matmul(
        "oproj", "nn", (1, d // te, d // tk), merged, pl.BlockSpec((s, tk), lambda i, j, k: (i, k)),
        w_o_f, pl.BlockSpec((tk, te), lambda i, j, k: (k, j)),
        [_sds((s, d), F32), _sds((s, d), BF16)], [s_blk, s_blk], (s, te),
        epilogue=oproj_epilogue, extras=(x2, gate1), extra_specs=(s_blk, e_vec), deps=(tok_c,))

    h2 = _norm_forward("norm2_fwd", x1, norm2_w, scale2, shift2)
    buf_e = _land(fl_e, h2)
    fl_e, tok_e = gather_start("E", buf_e, 2)
    w_ff1_f, = _land(fl_c, [h2, tok_e])

    def ff1_epilogue(acc, extra_refs, out_refs):
        r = jnp.maximum(acc, 0.0)
        out_refs[0][...] = r.astype(BF16)
        out_refs[1][...] = (r * r).astype(BF16)

    ff_blk = pl.BlockSpec((tm, half), lambda i, j, k: (i, j))
    relu, act = _matmul(
        "ff1", "nn", (s // tm, N_DEV, d // tk), h2, pl.BlockSpec((tm, tk), lambda i, j, k: (i, k)),
        w_ff1_f, pl.BlockSpec((None, tk, half), lambda i, j, k: (j, k, 0)),
        [_sds((s, 4 * d), BF16)] * 2, [ff_blk, ff_blk], (tm, half), epilogue=ff1_epilogue)
    w_ff2_f, = _land(fl_e, act)
    w_ff2_f = w_ff2_f.reshape(4 * d, d)

    def ff2_epilogue(acc, extra_refs, out_refs):
        x1_ref, g_ref, t_ref = extra_refs
        df_ref, dy_ref, sq_ref, dgate_ref = out_refs
        gate = g_ref[...]
        err = x1_ref[...] + gate * acc - t_ref[...]
        dyv = err * (1.0 / d)
        dy_ref[...] = dyv
        df_ref[...] = (dyv * gate).astype(BF16)
        sq_ref[...] = jnp.full(sq_ref.shape, jnp.sum(err * err), F32)
        dgate_ref[...] = jnp.broadcast_to(jnp.sum(dyv * acc, axis=0, keepdims=True), dgate_ref.shape)

    df, dy, sq, dgate2_parts = _matmul(
        "ff2", "nn", (s // tm, d // te, 2 * d // tk), act, pl.BlockSpec((tm, 2 * tk), lambda i, j, k: (i, k)),
        w_ff2_f, pl.BlockSpec((2 * tk, te), lambda i, j, k: (k, j)),
        [_sds((s, d), BF16), _sds((s, d), F32), _sds((s // tm * SUBLANES, d // te * LANES), F32),
         _sds((s // tm * SUBLANES, d), F32)],
        [e_blk, e_blk, pl.BlockSpec((SUBLANES, LANES), lambda i, j, k: (i, j)),
         pl.BlockSpec((SUBLANES, te), lambda i, j, k: (i, j))], (tm, te),
        epilogue=ff2_epilogue, extras=(x1, gate2, target), extra_specs=(e_blk, e_vec, e_blk))
    loss_local = (0.5 / d) * jnp.sum(sq[::SUBLANES, ::LANES])
    dgate2 = jnp.sum(dgate2_parts[::SUBLANES], axis=0, keepdims=True)

    tok_k = _tile(s, 2048)
    tw = _tile(d, 1024)
    g_ff2 = _matmul(
        "g_ff2", "tn", (4 * d // tw, d // tn, s // tok_k), act, pl.BlockSpec((tok_k, tw), lambda i, j, k: (k, i)),
        df, pl.BlockSpec((tok_k, tn), lambda i, j, k: (k, j)),
        [_sds((4 * d, d), BF16)], [pl.BlockSpec((tw, tn), lambda i, j, k: (i, j))], (tw, tn))[0]

    def da_epilogue(acc, extra_refs, out_refs):
        out_refs[0][...] = (acc * (2.0 * extra_refs[0][...].astype(F32))).astype(BF16)

    big_blk = pl.BlockSpec((tm, tn), lambda i, j, k: (i, j))
    fl_f2, tok = _reduce_scatter_start("F2", [g_ff2.reshape(N_DEV, half, d)])
    df1 = _matmul(
        "da_ff", "nt", (s // tm, 4 * d // tn, d // tk), df, pl.BlockSpec((tm, tk), lambda i, j, k: (i, k)),
        w_ff2_f, pl.BlockSpec((tn, tk), lambda i, j, k: (j, k)),
        [_sds((s, 4 * d), BF16)], [big_blk], (tm, tn),
        epilogue=da_epilogue, extras=(relu,), extra_specs=(big_blk,), deps=(tok,))[0]

    fl_f2, tok = _reduce_scatter_middle("F2", fl_f2, df1, me)
    g_ff1 = _matmul(
        "g_ff1", "tn", (d // tw, N_DEV, s // tok_k), h2, pl.BlockSpec((tok_k, tw), lambda i, j, k: (k, i)),
        df1, pl.BlockSpec((tok_k, half), lambda i, j, k: (k, j)),
        [_sds((N_DEV, d, half), BF16)], [pl.BlockSpec((None, tw, half), lambda i, j, k: (j, i, 0))], (tw, half),
        deps=(tok,))[0]

    fl_f1, tok = _reduce_scatter_start("F1", [g_ff1])
    dh2 = _matmul(
        "dh2", "nt", (s // tm, d // tn, N_DEV // 2), df1, pl.BlockSpec((tm, 2 * half), lambda i, j, k: (i, k)),
        w_ff1_f, pl.BlockSpec((2, tn, half), lambda i, j, k: (k, j, 0)),
        [_sds((s, d), F32)], [mn_blk], (tm, tn), deps=(tok,))[0]

    sum_ff2, = _reduce_scatter_finish(fl_f2, dh2)
    fl_f1, tok = _reduce_scatter_middle("F1", fl_f1, dh2, me)
    dx1, dshift2, dscale2, g_norm2, do, dgate1 = _norm_backward(
        "norm2_bwd", dh2, x1, norm2_w, scale2, dy, gated=(gate1, o_act), deps=(tok,))

    g_o = _matmul(
        "g_o", "tn", (d // tw, d // tn, s // tok_k), merged, pl.BlockSpec((tok_k, tw), lambda i, j, k: (k, i)),
        do, pl.BlockSpec((tok_k, tn), lambda i, j, k: (k, j)),
        [_sds((d, d), BF16)], [pl.BlockSpec((tw, tn), lambda i, j, k: (i, j))], (tw, tn))[0]

    def merge_bwd_epilogue(acc, extra_refs, out_refs):
        ga_ref, gb_ref, ya_ref, yb_ref = extra_refs
        dya_ref, dyb_ref, dga_ref, dgb_ref = out_refs
        sa = jax.nn.sigmoid(ga_ref[...])
        sb = jax.nn.sigmoid(gb_ref[...])
        dya_ref[...] = (acc * sa).astype(BF16)
        dyb_ref[...] = (acc * sb).astype(BF16)
        dga_ref[...] = (acc * ya_ref[...].astype(F32) * (sa * (1.0 - sa))).astype(BF16)
        dgb_ref[...] = (acc * yb_ref[...].astype(F32) * (sb * (1.0 - sb))).astype(BF16)

    td = _tile(d, 256)
    nb = d // td
    d_blk = pl.BlockSpec((s, td), lambda i, j, k: (i, j))
    dy_a, dy_b, dproj, dg_b = _matmul(
        "dmerged", "nt", (1, nb, d // tk), do, pl.BlockSpec((s, tk), lambda i, j, k: (i, k)),
        w_o_f, pl.BlockSpec((td, tk), lambda i, j, k: (j, k)),
        [_sds((s, d), BF16), _sds((s, d), BF16), _sds((s, 4 * d), BF16), _sds((s, d), BF16)],
        [d_blk, d_blk, pl.BlockSpec((s, td), lambda i, j, k: (i, 2 * nb + j)), d_blk], (s, td),
        epilogue=merge_bwd_epilogue, extras=(proj, proj, y_a, y_b),
        extra_specs=(pl.BlockSpec((s, td), lambda i, j, k: (i, 2 * nb + j)),
                     pl.BlockSpec((s, td), lambda i, j, k: (i, 3 * nb + j)), d_blk, d_blk))
    dproj = _place_columns("place_dgb", dg_b, dproj, 3)

    up_a = pl.BlockSpec((tok_k, half), lambda i, j, k: (k, 0))
    up_b = pl.BlockSpec((tok_k, d8), lambda i, j, k: (k, j))
    up_o = pl.BlockSpec((None, half, d8), lambda i, j, k: (j, 0, 0))
    g_a_up = _matmul("g_a_up", "tn", (1, N_DEV, s // tok_k), ya_in, up_a, dy_a, up_b,
                     [_sds((N_DEV, half, d8), BF16)], [up_o], (half, d8))[0]
    g_b_up = _matmul("g_b_up", "tn", (1, N_DEV, s // tok_k), attn, up_a, dy_b, up_b,
                     [_sds((N_DEV, half, d8), BF16)], [up_o], (half, d8))[0]
    slabs = 4
    dn_a = pl.BlockSpec((tm, slabs * d8), lambda i, j, k: (i, k))
    dn_b = pl.BlockSpec((slabs, half, d8), lambda i, j, k: (k, 0, 0))
    dn_o = pl.BlockSpec((tm, half), lambda i, j, k: (i, 0))
    dya_in = _matmul("d_ya_in", "nt", (s // tm, 1, N_DEV // slabs), dy_a, dn_a, w_a_f, dn_b,
                     [_sds((s, half), BF16)], [dn_o], (tm, half))[0]
    dattn = _matmul("d_attn", "nt", (s // tm, 1, N_DEV // slabs), dy_b, dn_a, w_b_f, dn_b,
                    [_sds((s, half), BF16)], [dn_o], (tm, half))[0]

    dproj, g_pool, g_pool_scale = _pool_backward(dya_in, pooled, w_pool_f, pool_scale, dproj)
    sum_ff1, = _reduce_scatter_finish(fl_f1, g_pool)
    g_pool_send = g_pool.astype(BF16).reshape(n_groups, N_DEV, rows_pool, cg).transpose(1, 0, 2, 3)
    g_pool_send = g_pool_send.reshape(N_DEV, n_groups * rows_pool, cg)
    fl_b, tok = _reduce_scatter_start("B", [g_pool_send, g_a_up, g_b_up, g_o.reshape(N_DEV, d8, d)])
    dqn, dkn, dproj = _attention_backward(qn, kn, vb, dattn, dproj, 3, deps=(tok,))
    fl_b, tok = _reduce_scatter_middle("B", fl_b, dqn, me)
    dproj, g_qnorm = _qk_norm_backward("qnorm_bwd", dqn, proj, 1, q_norm_w, dproj, half, deps=(tok,))
    dproj, g_knorm = _qk_norm_backward("knorm_bwd", dkn, proj, 2, k_norm_w, dproj, half)

    g_in = _matmul(
        "g_in", "tn", (d // tw, N_DEV, s // tok_k), h, pl.BlockSpec((tok_k, tw), lambda i, j, k: (k, i)),
        dproj, pl.BlockSpec((tok_k, half), lambda i, j, k: (k, j)),
        [_sds((N_DEV, d, half), BF16)], [pl.BlockSpec((None, tw, half), lambda i, j, k: (j, i, 0))], (tw, half))[0]
    fl_in, tok = _reduce_scatter_start("I", [g_in])
    dh = _matmul(
        "dh", "nt", (s // tm, d // tn, N_DEV // 2), dproj, pl.BlockSpec((tm, 2 * half), lambda i, j, k: (i, k)),
        w_in_f, pl.BlockSpec((2, tn, half), lambda i, j, k: (k, j, 0)),
        [_sds((s, d), F32)], [mn_blk], (tm, tn), deps=(tok,))[0]
    sum_pool, sum_a_up, sum_b_up, sum_o = _reduce_scatter_finish(fl_b, dh)
    grad_x, dshift1, dscale1, g_norm1 = _norm_backward("norm1_bwd", dh, x2, norm1_w, scale1, dx1)

    dmod = jnp.concatenate([dshift1, dscale1, dgate1, dshift2, dscale2, dgate2], axis=1)
    pieces = [dmod, g_norm1, g_norm2, g_pool_scale, g_qnorm, g_knorm, jnp.full((1, LANES), loss_local, F32)]
    packed_rows = [_rows_of_lanes(p) for p in pieces]
    offsets = [0]
    for p in packed_rows:
        offsets.append(offsets[-1] + p.shape[0])
    packed = jnp.concatenate(packed_rows, axis=0)
    small_all = _all_gather_2d("ag_small", packed)
    fl_in, tok = _reduce_scatter_middle("I", fl_in, small_all, me)
    small_sum = _sum_slots("small_sum", small_all[None], F32)[0]

    def unpack(i, width):
        return small_sum[offsets[i]:offsets[i] + width // LANES].reshape(1, width)

    g_b_ada = unpack(0, N_MOD * d)
    g_norm1_w = unpack(1, d)
    g_norm2_w = unpack(2, d)
    g_pool_scale_w = unpack(3, half)
    g_q_norm_w = unpack(4, HEAD_DIM)
    g_k_norm_w = unpack(5, HEAD_DIM)
    loss = unpack(6, LANES)[0, 0]
    dmod_all = small_all[:, :N_MOD * d // LANES].reshape(N_DEV, N_MOD * d)
    dmod_cols = lax.dynamic_slice_in_dim(dmod_all, me * wa, wa, axis=1)
    g_w_ada = _ada_weight_grad(c_all, dmod_cols, deps=(tok,))[None]


    grads = {
        "w_ada": g_w_ada, "b_ada": g_b_ada, "norm1_w": g_norm1_w,
        "q_norm_w": g_q_norm_w, "k_norm_w": g_k_norm_w,
        "pool_scale": g_pool_scale_w, "norm2_w": g_norm2_w,
    }
    sums = {"w_pool": sum_pool, "w_a_up": sum_a_up, "w_b_up": sum_b_up, "w_o": sum_o,
            "w_ff1": sum_ff1, "w_ff2": sum_ff2}
    weights = {"w_ada": (w_ada, m_w_ada, v_w_ada), "b_ada": (b_ada, m_b_ada, v_b_ada),
               "norm1_w": (norm1_w, m_norm1_w, v_norm1_w), "w_in": (w_in, m_w_in, v_w_in),
               "q_norm_w": (q_norm_w, m_q_norm_w, v_q_norm_w), "k_norm_w": (k_norm_w, m_k_norm_w, v_k_norm_w),
               "w_pool": (w_pool, m_w_pool, v_w_pool), "pool_scale": (pool_scale, m_pool_scale, v_pool_scale),
               "w_a_up": (w_a_up, m_w_a_up, v_w_a_up), "w_b_up": (w_b_up, m_w_b_up, v_w_b_up),
               "w_o": (w_o, m_w_o, v_w_o), "norm2_w": (norm2_w, m_norm2_w, v_norm2_w),
               "w_ff1": (w_ff1, m_w_ff1, v_w_ff1), "w_ff2": (w_ff2, m_w_ff2, v_w_ff2)}
    order = list(weights)
    deltas, new_m, new_v = {}, {}, {}
    def adam(name):
        wt, mt, vt = weights[name]
        shape = wt.shape
        flat = (-1, shape[-1])
        if name in sums:
            own, received = sums[name]
            g, dl, nm, nv = _adamw_summed("adamw_" + name, wt.reshape(flat), own, received,
                                          mt.reshape(flat), vt.reshape(flat))
            grads[name] = g.reshape(shape)
        else:
            dl, nm, nv = _adamw("adamw_" + name, wt.reshape(flat), grads[name].reshape(flat),
                                mt.reshape(flat), vt.reshape(flat))
        deltas[name], new_m[name], new_v[name] = dl.reshape(shape), nm.reshape(shape), nv.reshape(shape)

    others = [n for n in order if n != "w_in"]
    for name in others:
        adam(name)
    sums["w_in"], = _reduce_scatter_finish(fl_in, [deltas[n] for n in others])
    adam("w_in")

    return (loss, grad_x[None], *[grads[n] for n in order], *[deltas[n] for n in order],
            *[new_m[n] for n in order], *[new_v[n] for n in order])
```

```python
import math

import jax
import jax.numpy as jnp
from jax import lax
from jax.experimental import pallas as pl
from jax.experimental.pallas import tpu as pltpu

F32 = jnp.float32
BF16 = jnp.bfloat16
MESH_AXES = ("x", "y", "c")
N_DEV = 8
HEAD_DIM = 128
POOL_WINDOWS = (2, 4, 8, 16)
N_MOD = 6
NORM_EPS = 1e-6
LANES = 128
SUBLANES = 8
VMEM_LIMIT_BYTES = 56 * 1024 * 1024
Q_TILE = 256
K_TILE = 256
POOL_TILE = 512
HEADS_PER_STEP = 8
HEADS_PER_STEP_BWD = 4

ADAM_LR = 0.001
ADAM_B1 = 0.9
ADAM_B2 = 0.999
ADAM_EPS = 1e-08
ADAM_WD = 0.01
ADAM_STEP = 10

_NN = (((1,), (0,)), ((), ()))
_NT = (((1,), (1,)), ((), ()))
_TN = (((0,), (0,)), ((), ()))
_DIMS = {"nn": _NN, "nt": _NT, "tn": _TN}


def _dot(a, b, mode="nn"):
    return lax.dot_general(a, b, _DIMS[mode], preferred_element_type=F32)


def _params(*sem):
    return pltpu.CompilerParams(dimension_semantics=sem, vmem_limit_bytes=VMEM_LIMIT_BYTES)


def _tile(dim, pref, align=SUBLANES):
    for t in range(min(dim, pref), 0, -1):
        if dim % t == 0 and t % align == 0:
            return t
    return dim


def _group_index(axes):
    idx = 0
    for a in axes:
        idx = idx * 2 + lax.axis_index(a)
    return idx


def _peer_device(axes, k):
    coords = {a: lax.axis_index(a) for a in MESH_AXES}
    for pos, a in enumerate(axes):
        if (k >> (len(axes) - 1 - pos)) & 1:
            coords[a] = 1 - coords[a]
    return tuple(coords[a] for a in MESH_AXES)


_AXIS_BIT = {"x": 4, "y": 2, "c": 1}
_ANY = pl.BlockSpec(memory_space=pl.ANY)


def _device_xor(mask):
    return tuple(1 - lax.axis_index(a) if mask & _AXIS_BIT[a] else lax.axis_index(a) for a in MESH_AXES)


def _remote(src, dst, send_sem, recv_sem, mask):
    return pltpu.make_async_remote_copy(src_ref=src, dst_ref=dst, send_sem=send_sem, recv_sem=recv_sem,
                                        device_id=_device_xor(mask), device_id_type=pl.DeviceIdType.MESH)


_CHIP_MASKS = (0, _AXIS_BIT["y"], _AXIS_BIT["x"], _AXIS_BIT["x"] | _AXIS_BIT["y"])


def _add_received(name, own, own_slots, received, out_dtype):
    nj, r, c = received.shape
    tr = _tile(r, max(2 * SUBLANES, (1 << 20) // c), 2 * SUBLANES)

    def body(slots_ref, own_ref, rec_ref, o_ref):
        del slots_ref
        o_ref[...] = (own_ref[...].astype(F32) + rec_ref[...].astype(F32)).astype(o_ref.dtype)

    grid_spec = pltpu.PrefetchScalarGridSpec(
        num_scalar_prefetch=1, grid=(nj, r // tr),
        in_specs=[pl.BlockSpec((None, tr, c), lambda j, i, slots: (slots[j], i, 0)),
                  pl.BlockSpec((None, tr, c), lambda j, i, slots: (j, i, 0))],
        out_specs=pl.BlockSpec((None, tr, c), lambda j, i, slots: (j, i, 0)))
    return pl.pallas_call(
        body, name=name, grid_spec=grid_spec, out_shape=jax.ShapeDtypeStruct((nj, r, c), out_dtype),
        compiler_params=_params("parallel", "parallel"),
    )(own_slots, own, received)


_HBM = pl.BlockSpec(memory_space=pltpu.HBM)
_SEM = pl.BlockSpec(memory_space=pltpu.SEMAPHORE)
_DATAFLOW = pltpu.SideEffectType.DATAFLOW_SIDE_EFFECTING


def _launch_groups(name, groups, deps=()):
    bufs = [b for g_bufs, _ in groups for b in g_bufs]
    specs = [(len(g_bufs), spec) for g_bufs, g_plans in groups for spec in g_plans]
    nb, ns = len(bufs), len(specs)

    def body(*refs):
        sems = refs[nb + len(deps):nb + len(deps) + 2 * ns]
        me = _group_index(MESH_AXES)
        first, which = 0, 0
        for g_bufs, g_plans in groups:
            ins = refs[first:first + len(g_bufs)]
            for _, plan, n_copies in g_plans:
                copies = plan(ins, me)
                assert len(copies) == n_copies
                for n, (src, dst, mask) in enumerate(copies):
                    _remote(src, dst, sems[2 * which].at[n], sems[2 * which + 1].at[n], mask).start()
                which += 1
            first += len(g_bufs)
        refs[-1][...] = jnp.zeros_like(refs[-1])

    sem_shapes = [pltpu.SemaphoreType.DMA((n,)) for _, (_, _, n) in specs for _ in range(2)]
    outs = pl.pallas_call(
        body, name=name,
        out_shape=(*sem_shapes, *[pltpu.HBM(b.shape, b.dtype) for b in bufs],
                   jax.ShapeDtypeStruct((SUBLANES, LANES), F32)),
        in_specs=[_HBM] * nb + [_ANY] * len(deps),
        out_specs=(*[_SEM] * (2 * ns), *[_HBM] * nb, pl.BlockSpec(memory_space=pltpu.VMEM)),
        input_output_aliases={i: 2 * ns + i for i in range(nb)},
        compiler_params=pltpu.CompilerParams(has_side_effects=_DATAFLOW),
    )(*[pltpu.with_memory_space_constraint(b, pltpu.HBM) for b in bufs], *deps)
    flights, first, which = [], 0, 0
    for g_bufs, g_plans in groups:
        through = list(outs[2 * ns + first:2 * ns + first + len(g_bufs)])
        for land_name, plan, n_copies in g_plans:
            flights.append((land_name, plan, n_copies, outs[2 * which], outs[2 * which + 1], through))
            which += 1
        first += len(g_bufs)
    return flights, outs[-1]


def _launch(name, bufs, plan, n_copies, deps=()):
    (flight,), token = _launch_groups(name, [(bufs, [(name, plan, n_copies)])], deps)
    return flight, token


def _land(flight, after, bufs=None):
    name, plan, n_copies, send_sems, recv_sems, launched = flight
    bufs = launched if bufs is None else bufs
    nb = len(bufs)
    after = list(after) if isinstance(after, (list, tuple)) else [after]

    def body(*refs):
        ins = refs[:nb]
        s_sems, r_sems = refs[nb], refs[nb + 1]
        for n, (src, dst, mask) in enumerate(plan(ins, _group_index(MESH_AXES))):
            cp = _remote(src, dst, s_sems.at[n], r_sems.at[n], mask)
            cp.wait_send()
            cp.wait_recv()

    outs = pl.pallas_call(
        body, name=name + "_land",
        out_shape=tuple(pltpu.HBM(b.shape, b.dtype) for b in bufs),
        in_specs=[_HBM] * nb + [_SEM, _SEM] + [_ANY] * len(after), out_specs=tuple([_HBM] * nb),
        input_output_aliases={i: i for i in range(nb)},
        compiler_params=pltpu.CompilerParams(has_side_effects=_DATAFLOW),
    )(*bufs, send_sems, recv_sems, *after)
    return list(outs)


def _plan_gather_ici(phase):
    bx, by = _AXIS_BIT["x"], _AXIS_BIT["y"]

    def plan(refs, me):
        copies = []
        for ref in refs:
            half = ref.shape[1] // 2

            def piece(slot, color, mask, ref=ref, half=half):
                p = ref.at[slot, pl.ds(color * half, half)]
                return (p, p, mask)

            if phase == 0:
                copies += [piece(me, 0, bx), piece(me, 1, by)]
            else:
                copies += [piece(me, 0, by), piece(me ^ bx, 0, by), piece(me, 1, bx), piece(me ^ by, 1, bx)]
        return copies

    return plan


def _plan_d2d(masks):
    def plan(refs, me):
        return [(ref.at[me ^ m], ref.at[me ^ m], _AXIS_BIT["c"]) for ref in refs for m in masks]

    return plan


def _plan_gather_d2d(refs, me):
    return _plan_d2d(_CHIP_MASKS)(refs, me)


def _plan_neighbours(refs, me):
    return [(ref.at[me], ref.at[me], _AXIS_BIT[a]) for ref in refs for a in ("x", "y")]


def _plan_diagonal(refs, me):
    bx, by = _AXIS_BIT["x"], _AXIS_BIT["y"]
    copies = []
    for ref in refs:
        half = ref.shape[1] // 2
        lo = ref.at[me ^ bx, pl.ds(0, half)]
        hi = ref.at[me ^ by, pl.ds(half, half)]
        copies += [(lo, lo, by), (hi, hi, bx)]
    return copies


def _plan_scatter_d2d(refs, me):
    na = len(refs) // 2
    copies = []
    for a in range(na):
        for j, m in enumerate(_CHIP_MASKS):
            copies.append((refs[a].at[me ^ _AXIS_BIT["c"] ^ m], refs[na + a].at[j], _AXIS_BIT["c"]))
    return copies


def _plan_scatter_ici(refs, me):
    del me
    na = len(refs) // 2
    copies = []
    for a in range(na):
        for n, m in enumerate(_CHIP_MASKS[1:]):
            copies.append((refs[a].at[n + 1], refs[na + a].at[n], m))
    return copies


def _with_deps(body, n_in, deps):
    if not deps:
        return body

    def wrapped(*refs):
        return body(*refs[:n_in], *refs[n_in + len(deps):])

    return wrapped


def _reduce_scatter_start(tag, grads):
    lands = [lax.empty((len(_CHIP_MASKS),) + g.shape[1:], g.dtype) for g in grads]
    return _launch("rs%s_d2d" % tag, list(grads) + lands, _plan_scatter_d2d, len(_CHIP_MASKS) * len(grads))


def _reduce_scatter_add(tag, flight, after, me):
    bufs = _land(flight, after)
    na = len(bufs) // 2
    own_slots = jnp.stack([me ^ m for m in _CHIP_MASKS]).astype(jnp.int32)
    sums = [_add_received("rs%s_add_d2d_%d" % (tag, a), bufs[a], own_slots, bufs[na + a], BF16) for a in range(na)]
    lands = [lax.empty((len(_CHIP_MASKS) - 1,) + h.shape[1:], h.dtype) for h in sums]
    return sums + lands


def _reduce_scatter_plan(tag, bufs):
    return ("rs%s_ici" % tag, _plan_scatter_ici, (len(_CHIP_MASKS) - 1) * (len(bufs) // 2))


def _reduce_scatter_middle(tag, flight, after, me):
    bufs = _reduce_scatter_add(tag, flight, after, me)
    name, plan, n_copies = _reduce_scatter_plan(tag, bufs)
    return _launch(name, bufs, plan, n_copies)


def _plan_everyone(refs, me):
    return [(ref.at[me], ref.at[me], m) for ref in refs for m in range(1, N_DEV)]


def _reduce_scatter_finish(flight, after):
    bufs = _land(flight, after)
    na = len(bufs) // 2
    return [(bufs[a], bufs[na + a]) for a in range(na)]


def _all_gather_2d(name, x, deps=()):
    r, c = x.shape

    def body(x_ref, out_ref, send_sems, recv_sems):
        me = _group_index(MESH_AXES)
        out_ref[me] = x_ref[...]
        copies = []
        for k in range(1, N_DEV):
            cp = pltpu.make_async_remote_copy(
                src_ref=x_ref, dst_ref=out_ref.at[me],
                send_sem=send_sems.at[k - 1], recv_sem=recv_sems.at[k - 1],
                device_id=_peer_device(MESH_AXES, k), device_id_type=pl.DeviceIdType.MESH)
            cp.start()
            copies.append(cp)
        for cp in copies:
            cp.wait()

    vmem = pl.BlockSpec(memory_space=pltpu.VMEM)
    return pl.pallas_call(
        _with_deps(body, 1, deps), name=name, out_shape=jax.ShapeDtypeStruct((N_DEV, r, c), x.dtype),
        in_specs=[vmem] + [_ANY] * len(deps), out_specs=vmem,
        scratch_shapes=[pltpu.SemaphoreType.DMA((N_DEV - 1,)), pltpu.SemaphoreType.DMA((N_DEV - 1,))],
    )(x, *deps)


def _sum_slots(name, buf, out_dtype):
    pre, n, r, c = buf.shape
    tr = _tile(r, max(SUBLANES * 2, (1 << 20) // c))

    def body(b_ref, o_ref):
        acc = b_ref[0].astype(F32)
        for q in range(1, n):
            acc = acc + b_ref[q].astype(F32)
        o_ref[...] = acc.astype(o_ref.dtype)

    return pl.pallas_call(
        body, name=name, grid=(pre, r // tr),
        out_shape=jax.ShapeDtypeStruct((pre, r, c), out_dtype),
        in_specs=[pl.BlockSpec((None, n, tr, c), lambda i, j: (i, 0, j, 0))],
        out_specs=pl.BlockSpec((None, tr, c), lambda i, j: (i, j, 0)),
        compiler_params=_params("parallel", "parallel"),
    )(buf)


def _matmul(name, mode, grid, a, a_spec, b, b_spec, out_shapes, out_specs, acc_shape,
            epilogue=None, extras=(), extra_specs=(), aliases=None, deps=()):
    nk = grid[2]
    n_extra = len(extras)
    n_out = len(out_shapes)

    def finish(acc, extra_refs, out_refs):
        if epilogue is None:
            out_refs[0][...] = acc.astype(out_refs[0].dtype)
        else:
            epilogue(acc, extra_refs, out_refs)

    def product(a_ref, b_ref):
        if len(b_ref.shape) == 2:
            return _dot(a_ref[...], b_ref[...], mode)
        width = a_ref.shape[1] // b_ref.shape[0]
        total = None
        for i in range(b_ref.shape[0]):
            part = _dot(a_ref[:, i * width:(i + 1) * width], b_ref[i], mode)
            total = part if total is None else total + part
        return total

    def body(*refs):
        a_ref, b_ref = refs[0], refs[1]
        extra_refs = refs[2:2 + n_extra]
        out_refs = refs[2 + n_extra:2 + n_extra + n_out]
        if nk == 1:
            finish(product(a_ref, b_ref), extra_refs, out_refs)
            return
        acc_ref = refs[-1]
        k = pl.program_id(2)

        @pl.when(k == 0)
        def _():
            acc_ref[...] = product(a_ref, b_ref)

        @pl.when((k > 0) & (k < nk - 1))
        def _():
            acc_ref[...] += product(a_ref, b_ref)

        @pl.when(k == nk - 1)
        def _():
            finish(acc_ref[...] + product(a_ref, b_ref), extra_refs, out_refs)

    scratch = [] if nk == 1 else [pltpu.VMEM(acc_shape, F32)]
    return pl.pallas_call(
        _with_deps(body, 2 + n_extra, deps), name=name, grid=grid, out_shape=tuple(out_shapes),
        in_specs=[a_spec, b_spec] + list(extra_specs) + [_ANY] * len(deps), out_specs=tuple(out_specs),
        scratch_shapes=scratch, input_output_aliases=aliases or {},
        compiler_params=_params("parallel", "parallel", "arbitrary"),
    )(a, b, *extras, *deps)


def _sds(shape, dtype):
    return jax.ShapeDtypeStruct(tuple(shape), dtype)


def _project_slots(name, h, w_full, slots, out_cols, proj_in=None, deps=()):
    s, d = h.shape
    _, _, wide = w_full.shape
    tm = _tile(s, 1024)
    n_in = 4 if proj_in is not None else 3

    def body(*refs):
        refs[-1][...] = _dot(refs[1][...], refs[2][...])

    grid_spec = pltpu.PrefetchScalarGridSpec(
        num_scalar_prefetch=1, grid=(s // tm, slots.shape[0]),
        in_specs=[pl.BlockSpec((tm, d), lambda i, j, sl: (i, 0)),
                  pl.BlockSpec((None, d, wide), lambda i, j, sl: (sl[j], 0, 0))]
        + [_ANY] * (n_in - 3 + len(deps)),
        out_specs=pl.BlockSpec((tm, wide), lambda i, j, sl: (i, sl[j])))
    extra = ([proj_in] if proj_in is not None else []) + list(deps)
    return pl.pallas_call(
        body, name=name, grid_spec=grid_spec, out_shape=_sds((s, out_cols), F32),
        input_output_aliases={3: 0} if proj_in is not None else {},
        compiler_params=_params("parallel", "arbitrary"),
    )(slots, h, w_full, *extra)


def _ada_forward(c_all, w_ada, b_shard):
    nb, d = c_all.shape
    w = w_ada.shape[1]
    tn = _tile(w, 512)

    def body(c_ref, w_ref, b_ref, o_ref):
        cv = c_ref[...]
        sc = cv * jax.nn.sigmoid(cv)
        o_ref[...] = jnp.dot(sc, w_ref[...], precision=lax.Precision.HIGHEST,
                             preferred_element_type=F32) + b_ref[...]

    return pl.pallas_call(
        body, name="ada_fwd", grid=(w // tn,), out_shape=_sds((nb, w), F32),
        in_specs=[pl.BlockSpec((nb, d), lambda j: (0, 0)), pl.BlockSpec((d, tn), lambda j: (0, j)),
                  pl.BlockSpec((1, tn), lambda j: (0, j))],
        out_specs=pl.BlockSpec((nb, tn), lambda j: (0, j)),
        compiler_params=_params("parallel"),
    )(c_all, w_ada, b_shard)


def _ada_weight_grad(c_all, dmod_cols, deps=()):
    nb, d = c_all.shape
    w = dmod_cols.shape[1]
    tn = _tile(w, 512)

    def body(c_ref, g_ref, o_ref):
        cv = c_ref[...]
        sc = cv * jax.nn.sigmoid(cv)
        o_ref[...] = lax.dot_general(sc, g_ref[...], _TN, precision=lax.Precision.HIGHEST,
                                     preferred_element_type=F32)

    return pl.pallas_call(
        _with_deps(body, 2, deps), name="ada_wgrad", grid=(w // tn,), out_shape=_sds((d, w), F32),
        in_specs=[pl.BlockSpec((nb, d), lambda j: (0, 0)), pl.BlockSpec((nb, tn), lambda j: (0, j))]
        + [_ANY] * len(deps),
        out_specs=pl.BlockSpec((d, tn), lambda j: (0, j)),
        compiler_params=_params("parallel"),
    )(c_all, dmod_cols, *deps)


def _norm_forward(name, x, norm_w, scale, shift, deps=()):
    s, d = x.shape
    tm = _tile(s, 256)

    def body(x_ref, w_ref, sc_ref, sh_ref, h_ref):
        xv = x_ref[...]
        r = lax.rsqrt(jnp.mean(xv * xv, axis=-1, keepdims=True) + NORM_EPS)
        h = (xv * r * w_ref[...]) * (1.0 + sc_ref[...]) + sh_ref[...]
        h_ref[...] = h.astype(BF16)

    vec = pl.BlockSpec((1, d), lambda i: (0, 0))
    row = pl.BlockSpec((tm, d), lambda i: (i, 0))
    return pl.pallas_call(
        _with_deps(body, 4, deps), name=name, grid=(s // tm,), out_shape=_sds((s, d), BF16),
        in_specs=[row, vec, vec, vec] + [_ANY] * len(deps), out_specs=row, compiler_params=_params("parallel"),
    )(x, norm_w, scale, shift, *deps)


def _norm_backward(name, dh, x, norm_w, scale, dres, gated=None, deps=()):
    s, d = x.shape
    tm = _tile(s, 256)
    n_in = 7 if gated else 5

    def body(*refs):
        dh_ref, x_ref, w_ref, sc_ref, dres_ref = refs[:5]
        dx_ref, dshift_ref, dscale_ref, dw_ref = refs[n_in:n_in + 4]
        sums = (dshift_ref, dscale_ref, dw_ref) + ((refs[n_in + 5],) if gated else ())

        @pl.when(pl.program_id(0) == 0)
        def _():
            for ref in sums:
                ref[...] = jnp.zeros_like(ref)

        xv = x_ref[...]
        g = dh_ref[...]
        r = lax.rsqrt(jnp.mean(xv * xv, axis=-1, keepdims=True) + NORM_EPS)
        n = xv * r
        gain = 1.0 + sc_ref[...]
        gn = g * n
        dshift_ref[...] += jnp.sum(g, axis=0, keepdims=True)
        dscale_ref[...] += jnp.sum(gn, axis=0, keepdims=True) * w_ref[...]
        dw_ref[...] += jnp.sum(gn, axis=0, keepdims=True) * gain
        dn = g * (w_ref[...] * gain)
        dx = dres_ref[...] + r * (dn - n * jnp.mean(dn * n, axis=-1, keepdims=True))
        dx_ref[...] = dx
        if gated:
            gate_ref, other_ref = refs[5:7]
            refs[n_in + 4][...] = (dx * gate_ref[...]).astype(BF16)
            refs[n_in + 5][...] += jnp.sum(dx * other_ref[...].astype(F32), axis=0, keepdims=True)

    vec = pl.BlockSpec((1, d), lambda i: (0, 0))
    row = pl.BlockSpec((tm, d), lambda i: (i, 0))
    vec_out = _sds((1, d), F32)
    return pl.pallas_call(
        _with_deps(body, n_in, deps), name=name, grid=(s // tm,),
        out_shape=(_sds((s, d), F32), vec_out, vec_out, vec_out) + ((_sds((s, d), BF16), vec_out) if gated else ()),
        in_specs=[row, row, vec, vec, row] + ([vec, row] if gated else []) + [_ANY] * len(deps),
        out_specs=(row, vec, vec, vec) + ((row, vec) if gated else ()),
        compiler_params=_params("arbitrary"),
    )(dh, x, norm_w, scale, dres, *(gated or ()), *deps)


def _split_bf16(v):
    hi = v.astype(BF16)
    lo = (v - hi.astype(F32)).astype(BF16)
    return hi, lo


def _pool_forward(proj, w_pool, pool_scale, deps=()):
    s = proj.shape[0]
    g_n, cg, _ = w_pool.shape
    t = POOL_TILE
    nt = s // t

    def body(cur_ref, prev_ref, wp_ref, sc_ref, pooled_ref, ya_ref):
        g = pl.program_id(0)
        ti = pl.program_id(1)
        win = jnp.left_shift(2, g)
        row = lax.broadcasted_iota(jnp.int32, (t, t), 0)
        col = lax.broadcasted_iota(jnp.int32, (t, t), 1)
        lag = row - col
        band_cur = ((lag >= 0) & (lag < win)).astype(BF16)
        band_prev = ((lag + t < win) & (ti > 0)).astype(BF16)
        u = cur_ref[...]
        u_hi, u_lo = _split_bf16(u)
        p_hi, p_lo = _split_bf16(prev_ref[...])
        wsum = (_dot(band_cur, u_hi) + _dot(band_cur, u_lo)
                + _dot(band_prev, p_hi) + _dot(band_prev, p_lo))
        tok = ti * t + lax.broadcasted_iota(jnp.int32, (t, 1), 0)
        count = jnp.minimum(tok + 1, win).astype(F32)
        pooled = (wsum / count - u).astype(BF16)
        pooled_ref[...] = pooled
        ya_ref[...] = (_dot(pooled, wp_ref[...]) * sc_ref[...]).astype(BF16)

    blk = pl.BlockSpec((t, cg), lambda g, i: (i, g))
    return pl.pallas_call(
        _with_deps(body, 4, deps), name="pool_fwd", grid=(g_n, nt),
        out_shape=(_sds((s, g_n * cg), BF16), _sds((s, g_n * cg), BF16)),
        in_specs=[blk, pl.BlockSpec((t, cg), lambda g, i: (jnp.maximum(i - 1, 0), g)),
                  pl.BlockSpec((None, cg, cg), lambda g, i: (g, 0, 0)),
                  pl.BlockSpec((1, cg), lambda g, i: (0, g))] + [_ANY] * len(deps),
        out_specs=(blk, blk), compiler_params=_params("parallel", "parallel"),
    )(proj, proj, w_pool, pool_scale, *deps)


def _pool_backward(dya, pooled, w_pool, pool_scale, dproj):
    s = dya.shape[0]
    g_n, cg, _ = w_pool.shape
    t = POOL_TILE
    nt = s // t

    def body(dya_ref, dya_next_ref, pooled_ref, wp_ref, sc_ref, dproj_in, du_ref, gw_ref, gs_ref):
        del dproj_in
        g = pl.program_id(0)
        ti = pl.program_id(1)

        @pl.when(ti == 0)
        def _():
            gw_ref[...] = jnp.zeros_like(gw_ref)
            gs_ref[...] = jnp.zeros_like(gs_ref)

        win = jnp.left_shift(2, g)
        wp = wp_ref[...]
        sc = sc_ref[...]
        pooled_v = pooled_ref[...]
        dya_v = dya_ref[...].astype(F32)
        mixed = _dot(pooled_v, wp)
        gs_ref[...] += jnp.sum(dya_v * mixed, axis=0, keepdims=True)
        dmixed = (dya_v * sc).astype(BF16)
        gw_ref[...] += _dot(pooled_v, dmixed, "tn")
        dpooled = _dot(dmixed, wp, "nt")
        dmixed_next = (dya_next_ref[...].astype(F32) * sc).astype(BF16)
        dpooled_next = _dot(dmixed_next, wp, "nt")
        tok = ti * t + lax.broadcasted_iota(jnp.int32, (t, 1), 0)
        e_cur = dpooled / jnp.minimum(tok + 1, win).astype(F32)
        e_next = dpooled_next / jnp.minimum(tok + t + 1, win).astype(F32)
        row = lax.broadcasted_iota(jnp.int32, (t, t), 0)
        col = lax.broadcasted_iota(jnp.int32, (t, t), 1)
        lead = col - row
        band_cur = ((lead >= 0) & (lead < win)).astype(BF16)
        band_next = ((lead + t < win) & (ti < nt - 1)).astype(BF16)
        c_hi, c_lo = _split_bf16(e_cur)
        n_hi, n_lo = _split_bf16(e_next)
        du = (_dot(band_cur, c_hi) + _dot(band_cur, c_lo)
              + _dot(band_next, n_hi) + _dot(band_next, n_lo)) - dpooled
        du_ref[...] = du.astype(BF16)

    blk = pl.BlockSpec((t, cg), lambda g, i: (i, g))
    du, gw, gs = pl.pallas_call(
        body, name="pool_bwd", grid=(g_n, nt),
        out_shape=(_sds(dproj.shape, BF16), _sds((g_n, cg, cg), F32), _sds((1, g_n * cg), F32)),
        in_specs=[blk, pl.BlockSpec((t, cg), lambda g, i: (jnp.minimum(i + 1, nt - 1), g)), blk,
                  pl.BlockSpec((None, cg, cg), lambda g, i: (g, 0, 0)),
                  pl.BlockSpec((1, cg), lambda g, i: (0, g)),
                  pl.BlockSpec(memory_space=pl.ANY)],
        out_specs=(blk, pl.BlockSpec((None, cg, cg), lambda g, i: (g, 0, 0)),
                   pl.BlockSpec((1, cg), lambda g, i: (0, g))),
        input_output_aliases={5: 0}, compiler_params=_params("parallel", "arbitrary"),
    )(dya, dya, pooled, w_pool, pool_scale, dproj)
    return du, gw, gs


def _qkv_prepare(proj, q_norm_w, k_norm_w, width, deps=()):
    s = proj.shape[0]
    tm = _tile(s, 256)
    heads = width // HEAD_DIM

    def body(q_ref, k_ref, v_ref, qw_ref, kw_ref, qn_ref, kn_ref, vb_ref):
        for h in range(heads):
            cols = slice(h * HEAD_DIM, (h + 1) * HEAD_DIM)
            for src, w_ref, dst in ((q_ref, qw_ref, qn_ref), (k_ref, kw_ref, kn_ref)):
                v = src[:, cols]
                r = lax.rsqrt(jnp.mean(v * v, axis=-1, keepdims=True) + NORM_EPS)
                dst[:, cols] = (v * r * w_ref[...]).astype(BF16)
        vb_ref[...] = v_ref[...].astype(BF16)

    vec = pl.BlockSpec((1, HEAD_DIM), lambda i: (0, 0))
    out_spec = pl.BlockSpec((tm, width), lambda i: (i, 0))
    return pl.pallas_call(
        _with_deps(body, 5, deps), name="qkv_prep", grid=(s // tm,),
        out_shape=(_sds((s, width), BF16),) * 3,
        in_specs=[pl.BlockSpec((tm, width), lambda i: (i, 1)), pl.BlockSpec((tm, width), lambda i: (i, 2)),
                  pl.BlockSpec((tm, width), lambda i: (i, 3)), vec, vec] + [_ANY] * len(deps),
        out_specs=(out_spec,) * 3, compiler_params=_params("parallel"),
    )(proj, proj, proj, q_norm_w, k_norm_w, *deps)


def _qk_norm_backward(name, dn, proj, col_block, norm_w, dproj, width, deps=()):
    s = proj.shape[0]
    tm = _tile(s, 256)
    heads = width // HEAD_DIM

    def body(dn_ref, q_ref, w_ref, dproj_in, dq_ref, gw_ref):
        del dproj_in

        @pl.when(pl.program_id(0) == 0)
        def _():
            gw_ref[...] = jnp.zeros_like(gw_ref)

        wv = w_ref[...]
        gw = jnp.zeros((1, HEAD_DIM), F32)
        for h in range(heads):
            cols = slice(h * HEAD_DIM, (h + 1) * HEAD_DIM)
            v = q_ref[:, cols]
            g = dn_ref[:, cols]
            r = lax.rsqrt(jnp.mean(v * v, axis=-1, keepdims=True) + NORM_EPS)
            n = v * r
            gw = gw + jnp.sum(g * n, axis=0, keepdims=True)
            gn = g * wv
            dq_ref[:, cols] = (r * (gn - n * jnp.mean(gn * n, axis=-1, keepdims=True))).astype(BF16)
        gw_ref[...] += gw

    blk = pl.BlockSpec((tm, width), lambda i: (i, col_block))
    return pl.pallas_call(
        _with_deps(body, 4, deps), name=name, grid=(s // tm,),
        out_shape=(_sds(dproj.shape, BF16), _sds((1, HEAD_DIM), F32)),
        in_specs=[pl.BlockSpec((tm, width), lambda i: (i, 0)), blk,
                  pl.BlockSpec((1, HEAD_DIM), lambda i: (0, 0)), pl.BlockSpec(memory_space=pl.ANY)]
        + [_ANY] * len(deps),
        out_specs=(blk, pl.BlockSpec((1, HEAD_DIM), lambda i: (0, 0))),
        input_output_aliases={3: 0}, compiler_params=_params("arbitrary"),
    )(dn, proj, norm_w, dproj, *deps)


def _strict_upper(n):
    row = lax.broadcasted_iota(jnp.int32, (n, n), 0)
    col = lax.broadcasted_iota(jnp.int32, (n, n), 1)
    return (row > col).astype(BF16)


def _strict_lower(n):
    row = lax.broadcasted_iota(jnp.int32, (n, n), 0)
    col = lax.broadcasted_iota(jnp.int32, (n, n), 1)
    return (row < col).astype(BF16)


def _cumulate(v, tri):
    return _dot(v.astype(BF16), tri)


def _log_sigmoid(z):
    return jnp.minimum(z, 0.0) - jnp.log(1.0 + jnp.exp(-jnp.abs(z)))


def _attention_forward(qn, kn, vb):
    s, width = qn.shape
    heads = width // HEAD_DIM
    tq, tk = Q_TILE, K_TILE
    hp = min(HEADS_PER_STEP, heads)
    assert tq == tk and s % tq == 0 and heads % hp == 0
    scale = 1.0 / math.sqrt(HEAD_DIM)

    def body(q_ref, k_ref, v_ref, o_ref, a_scr):
        qi = pl.program_id(1)
        upper = _strict_upper(tk)
        causal = lax.broadcasted_iota(jnp.int32, (tq, tk), 1) < lax.broadcasted_iota(jnp.int32, (tq, tk), 0)
        head_cols = [slice(u * HEAD_DIM, (u + 1) * HEAD_DIM) for u in range(hp)]

        def weights(kb, carry, masked):
            rows = pl.ds(pl.multiple_of(kb * tk, tk), tk)
            out = []
            for u, cols in enumerate(head_cols):
                later = carry[u]
                z = _dot(q_ref[:, cols], k_ref[rows, cols], "nt") * scale
                log_beta = _log_sigmoid(z)
                l = log_beta - z
                if masked:
                    l = jnp.where(causal, l, 0.0)
                a = jnp.exp(log_beta + _cumulate(l, upper) + later)
                if masked:
                    a = jnp.where(causal, a, 0.0)
                a_scr[u, :, rows] = a.astype(BF16)
                out.append(later + jnp.sum(l, axis=1, keepdims=True))
            return tuple(out)

        later = weights(qi, tuple(jnp.zeros((tq, 1), F32) for _ in range(hp)), True)
        lax.fori_loop(0, qi, lambda i, c: weights(qi - 1 - i, c, False), later)

        def mix(kb, accs):
            rows = pl.ds(pl.multiple_of(kb * tk, tk), tk)
            return tuple(acc + _dot(a_scr[u, :, rows], v_ref[rows, cols])
                         for u, (acc, cols) in enumerate(zip(accs, head_cols)))

        accs = lax.fori_loop(0, qi + 1, mix, tuple(jnp.zeros((tq, HEAD_DIM), F32) for _ in range(hp)))
        for acc, cols in zip(accs, head_cols):
            o_ref[:, cols] = acc.astype(BF16)

    full = pl.BlockSpec((s, hp * HEAD_DIM), lambda h, i: (0, h))
    blk = pl.BlockSpec((tq, hp * HEAD_DIM), lambda h, i: (i, h))
    return pl.pallas_call(
        body, name="attn_fwd", grid=(heads // hp, s // tq), out_shape=_sds((s, width), BF16),
        in_specs=[blk, full, full], out_specs=blk, scratch_shapes=[pltpu.VMEM((hp, tq, s), BF16)],
        compiler_params=_params("parallel", "parallel"),
    )(qn, kn, vb)


def _attention_backward(qn, kn, vb, dout, dproj, v_col_block, deps=()):
    s, width = qn.shape
    heads = width // HEAD_DIM
    tq, tk = Q_TILE, K_TILE
    hp = min(HEADS_PER_STEP_BWD, heads)
    nq = s // tq
    scale = 1.0 / math.sqrt(HEAD_DIM)
    v_block0 = v_col_block * (heads // hp)

    def body(q_ref, k_ref, v_ref, do_ref, dproj_in, dq_ref, dk_ref, dv_ref,
             a_scr, lb_scr, dz_scr, dkt_scr, dvt_scr):
        del dproj_in
        qi = pl.program_id(1)

        @pl.when(qi == 0)
        def _():
            dkt_scr[...] = jnp.zeros_like(dkt_scr)
            dvt_scr[...] = jnp.zeros_like(dvt_scr)

        upper = _strict_upper(tk)
        lower = _strict_lower(tk)
        causal = lax.broadcasted_iota(jnp.int32, (tq, tk), 1) < lax.broadcasted_iota(jnp.int32, (tq, tk), 0)
        head_cols = [slice(u * HEAD_DIM, (u + 1) * HEAD_DIM) for u in range(hp)]

        def weights(kb, carry, masked):
            rows = pl.ds(pl.multiple_of(kb * tk, tk), tk)
            out = []
            for u, cols in enumerate(head_cols):
                later = carry[u]
                z = _dot(q_ref[:, cols], k_ref[rows, cols], "nt") * scale
                log_beta = _log_sigmoid(z)
                l = log_beta - z
                if masked:
                    l = jnp.where(causal, l, 0.0)
                a = jnp.exp(log_beta + _cumulate(l, upper) + later)
                if masked:
                    a = jnp.where(causal, a, 0.0)
                a_scr[u, :, rows] = a
                lb_scr[u, :, rows] = log_beta
                out.append(later + jnp.sum(l, axis=1, keepdims=True))
            return tuple(out)

        zeros = tuple(jnp.zeros((tq, 1), F32) for _ in range(hp))
        later = weights(qi, zeros, True)
        lax.fori_loop(0, qi, lambda i, c: weights(qi - 1 - i, c, False), later)

        q_t = [jnp.transpose(q_ref[:, cols].astype(F32)).astype(BF16) for cols in head_cols]
        do_t = [jnp.transpose(do_ref[:, cols].astype(F32)).astype(BF16) for cols in head_cols]

        def scores(kb, carry, masked):
            rows = pl.ds(pl.multiple_of(kb * tk, tk), tk)
            out = []
            for u, cols in enumerate(head_cols):
                before = carry[u]
                beta = jnp.exp(lb_scr[u, :, rows])
                g = a_scr[u, :, rows] * _dot(do_ref[:, cols], v_ref[rows, cols], "nt")
                p = _cumulate(g, lower) + before
                dz = g - (g + p) * beta
                if masked:
                    dz = jnp.where(causal, dz, 0.0)
                dz_scr[u, :, rows] = (dz * scale).astype(BF16)
                out.append(before + jnp.sum(g, axis=1, keepdims=True))
            return tuple(out)

        before = lax.fori_loop(0, qi, lambda i, c: scores(i, c, False), zeros)
        scores(qi, before, True)

        def products(kb, dqs):
            rows = pl.ds(pl.multiple_of(kb * tk, tk), tk)
            out = []
            for u, cols in enumerate(head_cols):
                dz = dz_scr[u, :, rows]
                dkt_scr[cols, rows] += _dot(q_t[u], dz)
                dvt_scr[cols, rows] += _dot(do_t[u], a_scr[u, :, rows].astype(BF16))
                out.append(dqs[u] + _dot(dz, k_ref[rows, cols]))
            return tuple(out)

        dqs = lax.fori_loop(0, qi + 1, products, tuple(jnp.zeros((tq, HEAD_DIM), F32) for _ in range(hp)))
        for u, cols in enumerate(head_cols):
            dq_ref[:, cols] = dqs[u]

        @pl.when(qi == nq - 1)
        def _():
            dk_ref[...] = jnp.transpose(dkt_scr[...])
            dv_ref[...] = jnp.transpose(dvt_scr[...]).astype(BF16)

    wide = hp * HEAD_DIM
    full = pl.BlockSpec((s, wide), lambda h, i: (0, h))
    blk = pl.BlockSpec((tq, wide), lambda h, i: (i, h))
    return pl.pallas_call(
        _with_deps(body, 5, deps), name="attn_bwd", grid=(heads // hp, nq),
        out_shape=(_sds((s, width), F32), _sds((s, width), F32), _sds(dproj.shape, BF16)),
        in_specs=[blk, full, full, blk, pl.BlockSpec(memory_space=pl.ANY)] + [_ANY] * len(deps),
        out_specs=(blk, full, pl.BlockSpec((s, wide), lambda h, i: (0, v_block0 + h))),
        scratch_shapes=[pltpu.VMEM((hp, tq, s), F32), pltpu.VMEM((hp, tq, s), F32), pltpu.VMEM((hp, tq, s), BF16),
                        pltpu.VMEM((wide, s), F32), pltpu.VMEM((wide, s), F32)],
        input_output_aliases={4: 2}, compiler_params=_params("parallel", "arbitrary"),
    )(qn, kn, vb, dout, dproj, *deps)


def _place_columns(name, src, dst, col_block):
    s, w = src.shape
    tm = _tile(s, 512)

    def body(src_ref, dst_in, out_ref):
        del dst_in
        out_ref[...] = src_ref[...]

    return pl.pallas_call(
        body, name=name, grid=(s // tm,), out_shape=_sds(dst.shape, dst.dtype),
        in_specs=[pl.BlockSpec((tm, w), lambda i: (i, 0)), pl.BlockSpec(memory_space=pl.ANY)],
        out_specs=pl.BlockSpec((tm, w), lambda i: (i, col_block)),
        input_output_aliases={1: 0}, compiler_params=_params("parallel"),
    )(src, dst)


def _cast_into_slot(name, x, slot):
    r, c = x.shape
    tr = _tile(r, max(SUBLANES * 2, (1 << 20) // c), SUBLANES * 2)

    def body(slot_ref, x_ref, o_ref):
        del slot_ref
        o_ref[...] = x_ref[...].astype(BF16)

    grid_spec = pltpu.PrefetchScalarGridSpec(
        num_scalar_prefetch=1, grid=(r // tr,),
        in_specs=[pl.BlockSpec((tr, c), lambda i, slot_ref: (i, 0))],
        out_specs=pl.BlockSpec((None, tr, c), lambda i, slot_ref: (slot_ref[0], i, 0)))
    return pl.pallas_call(
        body, name=name, grid_spec=grid_spec, out_shape=_sds((N_DEV, r, c), BF16),
        compiler_params=_params("parallel"),
    )(slot, x)


def _adamw_update(gv, w_ref, m_ref, v_ref, d_ref, nm_ref, nv_ref):
    c1 = 1.0 - ADAM_B1 ** ADAM_STEP
    c2 = 1.0 - ADAM_B2 ** ADAM_STEP
    nm = ADAM_B1 * m_ref[...] + (1.0 - ADAM_B1) * gv
    nv = ADAM_B2 * v_ref[...] + (1.0 - ADAM_B2) * (gv * gv)
    d_ref[...] = -ADAM_LR * ((nm / c1) / (jnp.sqrt(nv / c2) + ADAM_EPS) + ADAM_WD * w_ref[...])
    nm_ref[...] = nm
    nv_ref[...] = nv


def _adamw(name, w, g, m, v):
    r, c = w.shape
    tr = _tile(r, max(SUBLANES, (1 << 19) // c))

    def body(w_ref, g_ref, m_ref, v_ref, d_ref, nm_ref, nv_ref):
        _adamw_update(g_ref[...], w_ref, m_ref, v_ref, d_ref, nm_ref, nv_ref)

    blk = pl.BlockSpec((tr, c), lambda i: (i, 0))
    return pl.pallas_call(
        body, name=name, grid=(r // tr,), out_shape=(_sds((r, c), F32),) * 3,
        in_specs=[blk] * 4, out_specs=(blk,) * 3, compiler_params=_params("parallel"),
    )(w, g, m, v)


def _adamw_summed(name, w, own, received, m, v, deps=()):
    r, c = w.shape
    nj = received.shape[0]
    tr = _tile(r, max(2 * SUBLANES, (1 << 19) // c), 2 * SUBLANES)

    def body(w_ref, own_ref, rec_ref, m_ref, v_ref, g_ref, d_ref, nm_ref, nv_ref):
        gv = own_ref[...].astype(F32)
        for j in range(nj):
            gv = gv + rec_ref[j].astype(F32)
        g_ref[...] = gv
        _adamw_update(gv, w_ref, m_ref, v_ref, d_ref, nm_ref, nv_ref)

    blk = pl.BlockSpec((tr, c), lambda i: (i, 0))
    return pl.pallas_call(
        _with_deps(body, 5, deps), name=name, grid=(r // tr,), out_shape=(_sds((r, c), F32),) * 4,
        in_specs=[blk, pl.BlockSpec((None, tr, c), lambda i: (0, i, 0)),
                  pl.BlockSpec((nj, tr, c), lambda i: (0, i, 0)), blk, blk] + [_ANY] * len(deps),
        out_specs=(blk,) * 4, compiler_params=_params("parallel"),
    )(w, own, received, m, v, *deps)


def _rows_of_lanes(v):
    rows = v.shape[1] // LANES
    out = v.reshape(rows, LANES)
    pad = (-rows) % SUBLANES
    if pad:
        out = jnp.pad(out, ((0, pad), (0, 0)))
    return out


def kernel(x, c, w_ada, b_ada, norm1_w, w_in, q_norm_w, k_norm_w, w_pool, pool_scale, w_a_up, w_b_up, w_o, norm2_w, w_ff1, w_ff2, loss_target, m_w_ada, m_b_ada, m_norm1_w, m_w_in, m_q_norm_w, m_k_norm_w, m_w_pool, m_pool_scale, m_w_a_up, m_w_b_up, m_w_o, m_norm2_w, m_w_ff1, m_w_ff2, v_w_ada, v_b_ada, v_norm1_w, v_w_in, v_q_norm_w, v_k_norm_w, v_w_pool, v_pool_scale, v_w_a_up, v_w_b_up, v_w_o, v_norm2_w, v_w_ff1, v_w_ff2):
    _, s, d = x.shape
    half = d // 2
    d8 = d // N_DEV
    n_groups = len(POOL_WINDOWS)
    cg = half // n_groups
    me = _group_index(MESH_AXES)

    x2 = x[0]
    target = loss_target[0]

    my_slot = jnp.reshape(me, (1,)).astype(jnp.int32)

    def cast(i, t):
        return _cast_into_slot("cast_w%d" % i, t, my_slot)

    def gather_start(tag, bufs, phase):
        if phase < 2:
            return _launch("ag%s_ici%d" % (tag, phase), bufs, _plan_gather_ici(phase), (2, 4)[phase] * len(bufs))
        return _launch("ag%s_d2d" % tag, bufs, _plan_gather_d2d, len(_CHIP_MASKS) * len(bufs))

    def gather_plans(tag, n_bufs, phase):
        if phase < 2:
            return ("ag%s_ici%d" % (tag, phase), _plan_gather_ici(phase), (2, 4)[phase] * n_bufs)
        return ("ag%s_d2d" % tag, _plan_gather_d2d, len(_CHIP_MASKS) * n_bufs)

    bx, by, bc = _AXIS_BIT["x"], _AXIS_BIT["y"], _AXIS_BIT["c"]
    buf_a = [cast(0, w_in[0])]

    c_all = _all_gather_2d("ag_c", c.reshape(d // LANES, LANES), deps=tuple(buf_a)).reshape(N_DEV, d)
    fl_a, tok = _launch("agA_near", buf_a, _plan_neighbours, 2, deps=(c_all,))
    wa = w_ada.shape[2]
    b_shard = lax.dynamic_slice_in_dim(b_ada, me * wa, wa, axis=1)
    mod_part = _ada_forward(c_all, w_ada[0], b_shard)
    buf_b = [cast(1, w_pool[0].reshape(-1, cg)), cast(2, w_a_up[0]), cast(3, w_b_up[0]), cast(4, w_o[0])]
    buf_c = [cast(5, w_ff1[0])]
    buf_e = [cast(6, w_ff2[0])]
    mod_all = _all_gather_2d("ag_mod", mod_part.reshape(N_DEV * wa // LANES, LANES),
                             deps=tuple([tok] + buf_b + buf_c + buf_e))
    mod_all = mod_all.reshape(N_DEV, N_DEV, wa)
    mod = lax.dynamic_slice_in_dim(mod_all, me, 1, axis=1).reshape(1, N_MOD * d)
    shift1, scale1, gate1, shift2, scale2, gate2 = [mod[:, i * d:(i + 1) * d] for i in range(N_MOD)]

    h = _norm_forward("norm1_fwd", x2, norm1_w, scale1, shift1)
    buf_a = _land(fl_a, h)
    near = (0, bx, by)
    (fl_far, fl_near), tok = _launch_groups(
        "agA_far", [(buf_a, [("agA_far", _plan_diagonal, 2), ("agA_near_d2d", _plan_d2d(near), len(near))])])
    (fl_b, fl_c, fl_e), tok = _launch_groups(
        "agBCE_ici0", [(buf_b, [gather_plans("B", len(buf_b), 0)]), (buf_c, [gather_plans("C", 1, 0)]),
                       (buf_e, [gather_plans("E", 1, 0)])])

    tm = _tile(s, 1024)
    tk = _tile(d, 2048)
    te = _tile(d, 512)

    own_slots = jnp.stack([me ^ m for m in near]).astype(jnp.int32)
    far_slots = jnp.stack([me ^ bx ^ by ^ f for f in (0, bc)]).astype(jnp.int32)
    buf_a = fl_near[-1]
    proj = _project_slots("proj_own", h, buf_a[0], own_slots, 4 * d, deps=(tok,))
    buf_a = _land(fl_near, proj, bufs=buf_a)
    proj = _project_slots("proj_sibling", h, buf_a[0], own_slots ^ bc, 4 * d, proj_in=proj)
    buf_a = _land(fl_far, proj, bufs=buf_a)
    fl_a, tok = _launch("agA_far_d2d", buf_a, _plan_d2d((bx | by,)), 1)
    w_in_f, = _land(fl_a, tok)
    proj = _project_slots("proj_far", h, w_in_f, far_slots, 4 * d, proj_in=proj)
    buf_b = _land(fl_b, proj)
    buf_c = _land(fl_c, proj)
    (fl_b, fl_c), tok = _launch_groups(
        "agBC_ici1", [(buf_b, [gather_plans("B", len(buf_b), 1)]), (buf_c, [gather_plans("C", 1, 1)])])

    qn, kn, vb = _qkv_prepare(proj, q_norm_w, k_norm_w, half, deps=(tok,))
    attn = _attention_forward(qn, kn, vb)
    buf_b = _land(fl_b, attn)
    buf_e = _land(fl_e, attn)
    (fl_b, fl_e), tok = _launch_groups(
        "agB_d2d_E_ici1", [(buf_b, [gather_plans("B", len(buf_b), 2)]), (buf_e, [gather_plans("E", 1, 1)])])
    w_pool_f, w_a_f, w_b_f, w_o_f = _land(fl_b, tok)
    rows_pool = cg // N_DEV
    w_pool_f = w_pool_f.reshape(N_DEV, n_groups, rows_pool, cg).transpose(1, 0, 2, 3).reshape(n_groups, cg, cg)
    w_o_f = w_o_f.reshape(d, d)
    pooled, ya_in = _pool_forward(proj, w_pool_f, pool_scale)

    def merge_epilogue(ga_ref, gb_ref, ya, yb, out_refs):
        merged_ref, ya_ref, yb_ref = out_refs
        merged = jax.nn.sigmoid(ga_ref[...]) * ya + jax.nn.sigmoid(gb_ref[...]) * yb
        merged_ref[...] = merged.astype(BF16)
        ya_ref[...] = ya.astype(BF16)
        yb_ref[...] = yb.astype(BF16)

    def up_body(a1_ref, b1_ref, a2_ref, b2_ref, ga_ref, gb_ref, *out_refs):
        merge_epilogue(ga_ref, gb_ref, _dot(a1_ref[...], b1_ref[...]), _dot(a2_ref[...], b2_ref[...]), out_refs)

    ga_blk0 = 2 * d // d8
    gb_blk0 = 3 * d // d8
    tu = s
    a_spec = pl.BlockSpec((tu, half), lambda i, j: (i, 0))
    wup_spec = pl.BlockSpec((None, half, d8), lambda i, j: (j, 0, 0))
    o_blk = pl.BlockSpec((tu, d8), lambda i, j: (i, j))
    merged, y_a, y_b = pl.pallas_call(
        up_body, name="up_merge", grid=(s // tu, N_DEV), out_shape=(_sds((s, d), BF16),) * 3,
        in_specs=[a_spec, wup_spec, a_spec, wup_spec,
                  pl.BlockSpec((tu, d8), lambda i, j: (i, ga_blk0 + j)),
                  pl.BlockSpec((tu, d8), lambda i, j: (i, gb_blk0 + j))],
        out_specs=(o_blk,) * 3, compiler_params=_params("parallel", "parallel"),
    )(ya_in, w_a_f, attn, w_b_f, proj, proj)
    buf_c = _land(fl_c, merged)
    fl_c, tok_c = gather_start("C", buf_c, 2)

    tn = _tile(d, 1024)

    def oproj_epilogue(acc, extra_refs, out_refs):
        x_ref, g_ref = extra_refs
        x1_ref, o_ref = out_refs
        x1_ref[...] = x_ref[...] + g_ref[...] * acc
        o_ref[...] = acc.astype(BF16)

    mn_blk = pl.BlockSpec((tm, tn), lambda i, j, k: (i, j))
    e_blk = pl.BlockSpec((tm, te), lambda i, j, k: (i, j))
    e_vec = pl.BlockSpec((1, te), lambda i, j, k: (0, j))
    s_blk = pl.BlockSpec((s, te), lambda i, j, k: (i, j))
    x1, o_act = _matmul(
        "oproj", "nn", (1, d // te, d // tk), merged, pl.BlockSpec((s, tk), lambda i, j, k: (i, k)),
        w_o_f, pl.BlockSpec((tk, te), lambda i, j, k: (k, j)),
        [_sds((s, d), F32), _sds((s, d), BF16)], [s_blk, s_blk], (s, te),
        epilogue=oproj_epilogue, extras=(x2, gate1), extra_specs=(s_blk, e_vec), deps=(tok_c,))

    h2 = _norm_forward("norm2_fwd", x1, norm2_w, scale2, shift2)
    w_ff1_f, = _land(fl_c, h2)

    def ff1_epilogue(acc, extra_refs, out_refs):
        r = jnp.maximum(acc, 0.0)
        out_refs[0][...] = r.astype(BF16)
        out_refs[1][...] = (r * r).astype(BF16)

    n_rows = s // tm
    first = max(n_rows // 2, 1)

    def ff1_rows(name, row0, rows, prior, deps):
        blk = pl.BlockSpec((tm, half), lambda i, j, k: (i + row0, j))
        return _matmul(
            name, "nn", (rows, N_DEV, d // tk), h2, pl.BlockSpec((tm, tk), lambda i, j, k: (i + row0, k)),
            w_ff1_f, pl.BlockSpec((None, tk, half), lambda i, j, k: (j, k, 0)),
            [_sds((s, 4 * d), BF16)] * 2, [blk, blk], (tm, half), epilogue=ff1_epilogue,
            extras=prior, extra_specs=[_ANY] * len(prior), aliases={2 + n: n for n in range(len(prior))}, deps=deps)

    relu, act = ff1_rows("ff1_a", 0, first, (), ())
    if n_rows > first:
        buf_e = _land(fl_e, act)
        fl_e, tok_e = gather_start("E", buf_e, 2)
        relu, act = ff1_rows("ff1_b", first, n_rows - first, (relu, act), (tok_e,))
    else:
        fl_e, tok_e = gather_start("E", _land(fl_e, act), 2)
    w_ff2_f, = _land(fl_e, act if n_rows > first else tok_e)
    w_ff2_f = w_ff2_f.reshape(4 * d, d)

    def ff2_epilogue(acc, extra_refs, out_refs):
        x1_ref, g_ref, t_ref = extra_refs
        df_ref, dy_ref, sq_ref, dgate_ref = out_refs
        gate = g_ref[...]
        err = x1_ref[...] + gate * acc - t_ref[...]
        dyv = err * (1.0 / d)
        dy_ref[...] = dyv
        df_ref[...] = (dyv * gate).astype(BF16)
        sq_ref[...] = jnp.full(sq_ref.shape, jnp.sum(err * err), F32)
        dgate_ref[...] = jnp.broadcast_to(jnp.sum(dyv * acc, axis=0, keepdims=True), dgate_ref.shape)

    df, dy, sq, dgate2_parts = _matmul(
        "ff2", "nn", (s // tm, d // te, 2 * d // tk), act, pl.BlockSpec((tm, 2 * tk), lambda i, j, k: (i, k)),
        w_ff2_f, pl.BlockSpec((2 * tk, te), lambda i, j, k: (k, j)),
        [_sds((s, d), BF16), _sds((s, d), F32), _sds((s // tm * SUBLANES, d // te * LANES), F32),
         _sds((s // tm * SUBLANES, d), F32)],
        [e_blk, e_blk, pl.BlockSpec((SUBLANES, LANES), lambda i, j, k: (i, j)),
         pl.BlockSpec((SUBLANES, te), lambda i, j, k: (i, j))], (tm, te),
        epilogue=ff2_epilogue, extras=(x1, gate2, target), extra_specs=(e_blk, e_vec, e_blk))
    loss_local = (0.5 / d) * jnp.sum(sq[::SUBLANES, ::LANES])
    dgate2 = jnp.sum(dgate2_parts[::SUBLANES], axis=0, keepdims=True)

    tok_k = _tile(s, 2048)
    tw = _tile(d, 1024)
    g_ff2 = _matmul(
        "g_ff2", "tn", (4 * d // tw, d // tn, s // tok_k), act, pl.BlockSpec((tok_k, tw), lambda i, j, k: (k, i)),
        df, pl.BlockSpec((tok_k, tn), lambda i, j, k: (k, j)),
        [_sds((4 * d, d), BF16)], [pl.BlockSpec((tw, tn), lambda i, j, k: (i, j))], (tw, tn))[0]

    def da_epilogue(acc, extra_refs, out_refs):
        out_refs[0][...] = (acc * (2.0 * extra_refs[0][...].astype(F32))).astype(BF16)

    big_blk = pl.BlockSpec((tm, tn), lambda i, j, k: (i, j))
    fl_f2, tok = _reduce_scatter_start("F2", [g_ff2.reshape(N_DEV, half, d)])
    df1 = _matmul(
        "da_ff", "nt", (s // tm, 4 * d // tn, d // tk), df, pl.BlockSpec((tm, tk), lambda i, j, k: (i, k)),
        w_ff2_f, pl.BlockSpec((tn, tk), lambda i, j, k: (j, k)),
        [_sds((s, 4 * d), BF16)], [big_blk], (tm, tn),
        epilogue=da_epilogue, extras=(relu,), extra_specs=(big_blk,), deps=(tok,))[0]

    fl_f2, tok = _reduce_scatter_middle("F2", fl_f2, df1, me)
    g_ff1 = _matmul(
        "g_ff1", "tn", (d // tw, N_DEV, s // tok_k), h2, pl.BlockSpec((tok_k, tw), lambda i, j, k: (k, i)),
        df1, pl.BlockSpec((tok_k, half), lambda i, j, k: (k, j)),
        [_sds((N_DEV, d, half), BF16)], [pl.BlockSpec((None, tw, half), lambda i, j, k: (j, i, 0))], (tw, half),
        deps=(tok,))[0]

    fl_f1, tok = _reduce_scatter_start("F1", [g_ff1])
    dh2 = _matmul(
        "dh2", "nt", (s // tm, d // tn, N_DEV // 2), df1, pl.BlockSpec((tm, 2 * half), lambda i, j, k: (i, k)),
        w_ff1_f, pl.BlockSpec((2, tn, half), lambda i, j, k: (k, j, 0)),
        [_sds((s, d), F32)], [mn_blk], (tm, tn), deps=(tok,))[0]

    sum_ff2, = _reduce_scatter_finish(fl_f2, dh2)
    fl_f1, tok = _reduce_scatter_middle("F1", fl_f1, dh2, me)
    dx1, dshift2, dscale2, g_norm2, do, dgate1 = _norm_backward(
        "norm2_bwd", dh2, x1, norm2_w, scale2, dy, gated=(gate1, o_act), deps=(tok,))

    g_o = _matmul(
        "g_o", "tn", (d // tw, d // tn, s // tok_k), merged, pl.BlockSpec((tok_k, tw), lambda i, j, k: (k, i)),
        do, pl.BlockSpec((tok_k, tn), lambda i, j, k: (k, j)),
        [_sds((d, d), BF16)], [pl.BlockSpec((tw, tn), lambda i, j, k: (i, j))], (tw, tn))[0]

    def merge_bwd_epilogue(acc, extra_refs, out_refs):
        ga_ref, gb_ref, ya_ref, yb_ref = extra_refs
        dya_ref, dyb_ref, dga_ref, dgb_ref = out_refs
        sa = jax.nn.sigmoid(ga_ref[...])
        sb = jax.nn.sigmoid(gb_ref[...])
        dya_ref[...] = (acc * sa).astype(BF16)
        dyb_ref[...] = (acc * sb).astype(BF16)
        dga_ref[...] = (acc * ya_ref[...].astype(F32) * (sa * (1.0 - sa))).astype(BF16)
        dgb_ref[...] = (acc * yb_ref[...].astype(F32) * (sb * (1.0 - sb))).astype(BF16)

    td = _tile(d, 256)
    nb = d // td
    d_blk = pl.BlockSpec((s, td), lambda i, j, k: (i, j))
    dy_a, dy_b, dproj, dg_b = _matmul(
        "dmerged", "nt", (1, nb, d // tk), do, pl.BlockSpec((s, tk), lambda i, j, k: (i, k)),
        w_o_f, pl.BlockSpec((td, tk), lambda i, j, k: (j, k)),
        [_sds((s, d), BF16), _sds((s, d), BF16), _sds((s, 4 * d), BF16), _sds((s, d), BF16)],
        [d_blk, d_blk, pl.BlockSpec((s, td), lambda i, j, k: (i, 2 * nb + j)), d_blk], (s, td),
        epilogue=merge_bwd_epilogue, extras=(proj, proj, y_a, y_b),
        extra_specs=(pl.BlockSpec((s, td), lambda i, j, k: (i, 2 * nb + j)),
                     pl.BlockSpec((s, td), lambda i, j, k: (i, 3 * nb + j)), d_blk, d_blk))
    dproj = _place_columns("place_dgb", dg_b, dproj, 3)

    up_a = pl.BlockSpec((tok_k, half), lambda i, j, k: (k, 0))
    up_b = pl.BlockSpec((tok_k, d8), lambda i, j, k: (k, j))
    up_o = pl.BlockSpec((None, half, d8), lambda i, j, k: (j, 0, 0))
    g_a_up = _matmul("g_a_up", "tn", (1, N_DEV, s // tok_k), ya_in, up_a, dy_a, up_b,
                     [_sds((N_DEV, half, d8), BF16)], [up_o], (half, d8))[0]
    g_b_up = _matmul("g_b_up", "tn", (1, N_DEV, s // tok_k), attn, up_a, dy_b, up_b,
                     [_sds((N_DEV, half, d8), BF16)], [up_o], (half, d8))[0]
    slabs = 4
    dn_a = pl.BlockSpec((tm, slabs * d8), lambda i, j, k: (i, k))
    dn_b = pl.BlockSpec((slabs, half, d8), lambda i, j, k: (k, 0, 0))
    dn_o = pl.BlockSpec((tm, half), lambda i, j, k: (i, 0))
    dya_in = _matmul("d_ya_in", "nt", (s // tm, 1, N_DEV // slabs), dy_a, dn_a, w_a_f, dn_b,
                     [_sds((s, half), BF16)], [dn_o], (tm, half))[0]
    dattn = _matmul("d_attn", "nt", (s // tm, 1, N_DEV // slabs), dy_b, dn_a, w_b_f, dn_b,
                    [_sds((s, half), BF16)], [dn_o], (tm, half))[0]

    dproj, g_pool, g_pool_scale = _pool_backward(dya_in, pooled, w_pool_f, pool_scale, dproj)
    sum_ff1, = _reduce_scatter_finish(fl_f1, g_pool)
    g_pool_send = g_pool.astype(BF16).reshape(n_groups, N_DEV, rows_pool, cg).transpose(1, 0, 2, 3)
    g_pool_send = g_pool_send.reshape(N_DEV, n_groups * rows_pool, cg)
    fl_b, tok = _reduce_scatter_start("B", [g_pool_send, g_a_up, g_b_up, g_o.reshape(N_DEV, d8, d)])
    dqn, dkn, dproj = _attention_backward(qn, kn, vb, dattn, dproj, 3, deps=(tok,))
    fl_b, tok = _reduce_scatter_middle("B", fl_b, dqn, me)
    dproj, g_qnorm = _qk_norm_backward("qnorm_bwd", dqn, proj, 1, q_norm_w, dproj, half, deps=(tok,))
    dproj, g_knorm = _qk_norm_backward("knorm_bwd", dkn, proj, 2, k_norm_w, dproj, half)

    g_in = _matmul(
        "g_in", "tn", (d // tw, N_DEV, s // tok_k), h, pl.BlockSpec((tok_k, tw), lambda i, j, k: (k, i)),
        dproj, pl.BlockSpec((tok_k, half), lambda i, j, k: (k, j)),
        [_sds((N_DEV, d, half), BF16)], [pl.BlockSpec((None, tw, half), lambda i, j, k: (j, i, 0))], (tw, half))[0]
    fl_in, tok = _reduce_scatter_start("I", [g_in])
    dh = _matmul(
        "dh", "nt", (s // tm, d // tn, N_DEV // 2), dproj, pl.BlockSpec((tm, 2 * half), lambda i, j, k: (i, k)),
        w_in_f, pl.BlockSpec((2, tn, half), lambda i, j, k: (k, j, 0)),
        [_sds((s, d), F32)], [mn_blk], (tm, tn), deps=(tok,))[0]
    sum_pool, sum_a_up, sum_b_up, sum_o = _reduce_scatter_finish(fl_b, dh)
    buf_in = _reduce_scatter_add("I", fl_in, dh, me)
    grad_x, dshift1, dscale1, g_norm1 = _norm_backward("norm1_bwd", dh, x2, norm1_w, scale1, dx1, deps=tuple(buf_in[:1]))

    dmod = jnp.concatenate([dshift1, dscale1, dgate1, dshift2, dscale2, dgate2], axis=1)
    pieces = [dmod, g_norm1, g_norm2, g_pool_scale, g_qnorm, g_knorm, jnp.full((1, LANES), loss_local, F32)]
    packed_rows = [_rows_of_lanes(p) for p in pieces]
    offsets = [0]
    for p in packed_rows:
        offsets.append(offsets[-1] + p.shape[0])
    packed = jnp.concatenate(packed_rows, axis=0)
    small_buf = lax.dynamic_update_slice(jnp.zeros((N_DEV,) + packed.shape, F32), packed[None], (me, 0, 0))
    (fl_small, fl_in), tok = _launch_groups(
        "rsI_ici_small", [([small_buf], [("ag_small", _plan_everyone, N_DEV - 1)]),
                          (buf_in, [_reduce_scatter_plan("I", buf_in)])])
    small_all, = _land(fl_small, tok)
    small_sum = _sum_slots("small_sum", small_all[None], F32)[0]

    def unpack(i, width):
        return small_sum[offsets[i]:offsets[i] + width // LANES].reshape(1, width)

    g_b_ada = unpack(0, N_MOD * d)
    g_norm1_w = unpack(1, d)
    g_norm2_w = unpack(2, d)
    g_pool_scale_w = unpack(3, half)
    g_q_norm_w = unpack(4, HEAD_DIM)
    g_k_norm_w = unpack(5, HEAD_DIM)
    loss = unpack(6, LANES)[0, 0]
    dmod_all = small_all[:, :N_MOD * d // LANES].reshape(N_DEV, N_MOD * d)
    dmod_cols = lax.dynamic_slice_in_dim(dmod_all, me * wa, wa, axis=1)
    g_w_ada = _ada_weight_grad(c_all, dmod_cols, deps=(tok,))[None]


    grads = {
        "w_ada": g_w_ada, "b_ada": g_b_ada, "norm1_w": g_norm1_w,
        "q_norm_w": g_q_norm_w, "k_norm_w": g_k_norm_w,
        "pool_scale": g_pool_scale_w, "norm2_w": g_norm2_w,
    }
    sums = {"w_pool": sum_pool, "w_a_up": sum_a_up, "w_b_up": sum_b_up, "w_o": sum_o,
            "w_ff1": sum_ff1, "w_ff2": sum_ff2}
    weights = {"w_ada": (w_ada, m_w_ada, v_w_ada), "b_ada": (b_ada, m_b_ada, v_b_ada),
               "norm1_w": (norm1_w, m_norm1_w, v_norm1_w), "w_in": (w_in, m_w_in, v_w_in),
               "q_norm_w": (q_norm_w, m_q_norm_w, v_q_norm_w), "k_norm_w": (k_norm_w, m_k_norm_w, v_k_norm_w),
               "w_pool": (w_pool, m_w_pool, v_w_pool), "pool_scale": (pool_scale, m_pool_scale, v_pool_scale),
               "w_a_up": (w_a_up, m_w_a_up, v_w_a_up), "w_b_up": (w_b_up, m_w_b_up, v_w_b_up),
               "w_o": (w_o, m_w_o, v_w_o), "norm2_w": (norm2_w, m_norm2_w, v_norm2_w),
               "w_ff1": (w_ff1, m_w_ff1, v_w_ff1), "w_ff2": (w_ff2, m_w_ff2, v_w_ff2)}
    order = list(weights)
    deltas, new_m, new_v = {}, {}, {}
    def adam(name):
        wt, mt, vt = weights[name]
        shape = wt.shape
        flat = (-1, shape[-1])
        if name in sums:
            own, received = sums[name]
            g, dl, nm, nv = _adamw_summed("adamw_" + name, wt.reshape(flat), own, received,
                                          mt.reshape(flat), vt.reshape(flat), deps=(tok,))
            grads[name] = g.reshape(shape)
        else:
            dl, nm, nv = _adamw("adamw_" + name, wt.reshape(flat), grads[name].reshape(flat),
                                mt.reshape(flat), vt.reshape(flat))
        deltas[name], new_m[name], new_v[name] = dl.reshape(shape), nm.reshape(shape), nv.reshape(shape)

    others = [n for n in order if n != "w_in"]
    for name in others:
        adam(name)
    sums["w_in"], = _reduce_scatter_finish(fl_in, [deltas[n] for n in others])
    adam("w_in")

    return (loss, grad_x[None], *[grads[n] for n in order], *[deltas[n] for n in order],
            *[new_m[n] for n in order], *[new_v[n] for n in order])
```

```python
import math

import jax
import jax.numpy as jnp
from jax import lax
from jax.experimental import pallas as pl
from jax.experimental.pallas import tpu as pltpu

F32 = jnp.float32
BF16 = jnp.bfloat16
MESH_AXES = ("x", "y", "c")
N_DEV = 8
HEAD_DIM = 128
POOL_WINDOWS = (2, 4, 8, 16)
N_MOD = 6
NORM_EPS = 1e-6
LANES = 128
SUBLANES = 8
VMEM_LIMIT_BYTES = 56 * 1024 * 1024
Q_TILE = 256
K_TILE = 256
POOL_TILE = 512
HEADS_PER_STEP = 8
HEADS_PER_STEP_BWD = 4

ADAM_LR = 0.001
ADAM_B1 = 0.9
ADAM_B2 = 0.999
ADAM_EPS = 1e-08
ADAM_WD = 0.01
ADAM_STEP = 10

_NN = (((1,), (0,)), ((), ()))
_NT = (((1,), (1,)), ((), ()))
_TN = (((0,), (0,)), ((), ()))
_DIMS = {"nn": _NN, "nt": _NT, "tn": _TN}


def _dot(a, b, mode="nn"):
    return lax.dot_general(a, b, _DIMS[mode], preferred_element_type=F32)


def _params(*sem):
    return pltpu.CompilerParams(dimension_semantics=sem, vmem_limit_bytes=VMEM_LIMIT_BYTES)


def _tile(dim, pref, align=SUBLANES):
    for t in range(min(dim, pref), 0, -1):
        if dim % t == 0 and t % align == 0:
            return t
    return dim


def _group_index(axes):
    idx = 0
    for a in axes:
        idx = idx * 2 + lax.axis_index(a)
    return idx


def _peer_device(axes, k):
    coords = {a: lax.axis_index(a) for a in MESH_AXES}
    for pos, a in enumerate(axes):
        if (k >> (len(axes) - 1 - pos)) & 1:
            coords[a] = 1 - coords[a]
    return tuple(coords[a] for a in MESH_AXES)


_AXIS_BIT = {"x": 4, "y": 2, "c": 1}
_ANY = pl.BlockSpec(memory_space=pl.ANY)


def _device_xor(mask):
    return tuple(1 - lax.axis_index(a) if mask & _AXIS_BIT[a] else lax.axis_index(a) for a in MESH_AXES)


def _remote(src, dst, send_sem, recv_sem, mask):
    return pltpu.make_async_remote_copy(src_ref=src, dst_ref=dst, send_sem=send_sem, recv_sem=recv_sem,
                                        device_id=_device_xor(mask), device_id_type=pl.DeviceIdType.MESH)


_CHIP_MASKS = (0, _AXIS_BIT["y"], _AXIS_BIT["x"], _AXIS_BIT["x"] | _AXIS_BIT["y"])


def _add_received(name, own, own_slots, received, out_dtype):
    nj, r, c = received.shape
    tr = _tile(r, max(2 * SUBLANES, (1 << 20) // c), 2 * SUBLANES)

    def body(slots_ref, own_ref, rec_ref, o_ref):
        del slots_ref
        o_ref[...] = (own_ref[...].astype(F32) + rec_ref[...].astype(F32)).astype(o_ref.dtype)

    grid_spec = pltpu.PrefetchScalarGridSpec(
        num_scalar_prefetch=1, grid=(nj, r // tr),
        in_specs=[pl.BlockSpec((None, tr, c), lambda j, i, slots: (slots[j], i, 0)),
                  pl.BlockSpec((None, tr, c), lambda j, i, slots: (j, i, 0))],
        out_specs=pl.BlockSpec((None, tr, c), lambda j, i, slots: (j, i, 0)))
    return pl.pallas_call(
        body, name=name, grid_spec=grid_spec, out_shape=jax.ShapeDtypeStruct((nj, r, c), out_dtype),
        compiler_params=_params("parallel", "parallel"),
    )(own_slots, own, received)


_HBM = pl.BlockSpec(memory_space=pltpu.HBM)
_SEM = pl.BlockSpec(memory_space=pltpu.SEMAPHORE)
_DATAFLOW = pltpu.SideEffectType.DATAFLOW_SIDE_EFFECTING


def _launch_groups(name, groups, deps=()):
    bufs = [b for g_bufs, _ in groups for b in g_bufs]
    specs = [(len(g_bufs), spec) for g_bufs, g_plans in groups for spec in g_plans]
    nb, ns = len(bufs), len(specs)

    def body(*refs):
        sems = refs[nb + len(deps):nb + len(deps) + 2 * ns]
        me = _group_index(MESH_AXES)
        first, which = 0, 0
        for g_bufs, g_plans in groups:
            ins = refs[first:first + len(g_bufs)]
            for _, plan, n_copies in g_plans:
                copies = plan(ins, me)
                assert len(copies) == n_copies
                for n, (src, dst, mask) in enumerate(copies):
                    _remote(src, dst, sems[2 * which].at[n], sems[2 * which + 1].at[n], mask).start()
                which += 1
            first += len(g_bufs)
        refs[-1][...] = jnp.zeros_like(refs[-1])

    sem_shapes = [pltpu.SemaphoreType.DMA((n,)) for _, (_, _, n) in specs for _ in range(2)]
    outs = pl.pallas_call(
        body, name=name,
        out_shape=(*sem_shapes, *[pltpu.HBM(b.shape, b.dtype) for b in bufs],
                   jax.ShapeDtypeStruct((SUBLANES, LANES), F32)),
        in_specs=[_HBM] * nb + [_ANY] * len(deps),
        out_specs=(*[_SEM] * (2 * ns), *[_HBM] * nb, pl.BlockSpec(memory_space=pltpu.VMEM)),
        input_output_aliases={i: 2 * ns + i for i in range(nb)},
        compiler_params=pltpu.CompilerParams(has_side_effects=_DATAFLOW),
    )(*[pltpu.with_memory_space_constraint(b, pltpu.HBM) for b in bufs], *deps)
    flights, first, which = [], 0, 0
    for g_bufs, g_plans in groups:
        through = list(outs[2 * ns + first:2 * ns + first + len(g_bufs)])
        for land_name, plan, n_copies in g_plans:
            flights.append((land_name, plan, n_copies, outs[2 * which], outs[2 * which + 1], through))
            which += 1
        first += len(g_bufs)
    return flights, outs[-1]


def _launch(name, bufs, plan, n_copies, deps=()):
    (flight,), token = _launch_groups(name, [(bufs, [(name, plan, n_copies)])], deps)
    return flight, token


def _land(flight, after, bufs=None):
    name, plan, n_copies, send_sems, recv_sems, launched = flight
    bufs = launched if bufs is None else bufs
    nb = len(bufs)
    after = list(after) if isinstance(after, (list, tuple)) else [after]

    def body(*refs):
        ins = refs[:nb]
        s_sems, r_sems = refs[nb], refs[nb + 1]
        for n, (src, dst, mask) in enumerate(plan(ins, _group_index(MESH_AXES))):
            cp = _remote(src, dst, s_sems.at[n], r_sems.at[n], mask)
            cp.wait_send()
            cp.wait_recv()

    outs = pl.pallas_call(
        body, name=name + "_land",
        out_shape=tuple(pltpu.HBM(b.shape, b.dtype) for b in bufs),
        in_specs=[_HBM] * nb + [_SEM, _SEM] + [_ANY] * len(after), out_specs=tuple([_HBM] * nb),
        input_output_aliases={i: i for i in range(nb)},
        compiler_params=pltpu.CompilerParams(has_side_effects=_DATAFLOW),
    )(*bufs, send_sems, recv_sems, *after)
    return list(outs)


def _plan_gather_ici(phase):
    bx, by = _AXIS_BIT["x"], _AXIS_BIT["y"]

    def plan(refs, me):
        copies = []
        for ref in refs:
            half = ref.shape[1] // 2

            def piece(slot, color, mask, ref=ref, half=half):
                p = ref.at[slot, pl.ds(color * half, half)]
                return (p, p, mask)

            if phase == 0:
                copies += [piece(me, 0, bx), piece(me, 1, by)]
            else:
                copies += [piece(me, 0, by), piece(me ^ bx, 0, by), piece(me, 1, bx), piece(me ^ by, 1, bx)]
        return copies

    return plan


def _plan_d2d(masks):
    def plan(refs, me):
        return [(ref.at[me ^ m], ref.at[me ^ m], _AXIS_BIT["c"]) for ref in refs for m in masks]

    return plan


def _plan_gather_d2d(refs, me):
    return _plan_d2d(_CHIP_MASKS)(refs, me)


def _plan_neighbours(refs, me):
    return [(ref.at[me], ref.at[me], _AXIS_BIT[a]) for ref in refs for a in ("x", "y")]


def _plan_diagonal(refs, me):
    bx, by = _AXIS_BIT["x"], _AXIS_BIT["y"]
    copies = []
    for ref in refs:
        half = ref.shape[1] // 2
        lo = ref.at[me ^ bx, pl.ds(0, half)]
        hi = ref.at[me ^ by, pl.ds(half, half)]
        copies += [(lo, lo, by), (hi, hi, bx)]
    return copies


def _plan_scatter_d2d(refs, me):
    na = len(refs) // 2
    copies = []
    for a in range(na):
        for j, m in enumerate(_CHIP_MASKS):
            copies.append((refs[a].at[me ^ _AXIS_BIT["c"] ^ m], refs[na + a].at[j], _AXIS_BIT["c"]))
    return copies


def _plan_scatter_ici(refs, me):
    del me
    na = len(refs) // 2
    copies = []
    for a in range(na):
        for n, m in enumerate(_CHIP_MASKS[1:]):
            copies.append((refs[a].at[n + 1], refs[na + a].at[n], m))
    return copies


def _with_deps(body, n_in, deps):
    if not deps:
        return body

    def wrapped(*refs):
        return body(*refs[:n_in], *refs[n_in + len(deps):])

    return wrapped


def _reduce_scatter_start(tag, grads):
    lands = [lax.empty((len(_CHIP_MASKS),) + g.shape[1:], g.dtype) for g in grads]
    return _launch("rs%s_d2d" % tag, list(grads) + lands, _plan_scatter_d2d, len(_CHIP_MASKS) * len(grads))


def _reduce_scatter_add(tag, flight, after, me, n_land=len(_CHIP_MASKS) - 1):
    bufs = _land(flight, after)
    na = len(bufs) // 2
    own_slots = jnp.stack([me ^ m for m in _CHIP_MASKS]).astype(jnp.int32)
    sums = [_add_received("rs%s_add_d2d_%d" % (tag, a), bufs[a], own_slots, bufs[na + a], BF16) for a in range(na)]
    lands = [lax.empty((n_land,) + h.shape[1:], h.dtype) for h in sums]
    return sums + lands


def _plan_scatter_relay_first(refs, me):
    del me
    bx, by = _AXIS_BIT["x"], _AXIS_BIT["y"]
    na = len(refs) // 2
    copies = []
    for a in range(na):
        h, land = refs[a], refs[na + a]
        half = h.shape[1] // 2
        lo, hi = pl.ds(0, half), pl.ds(half, half)
        copies += [(h.at[1, lo], land.at[0, lo], by), (h.at[3, lo], land.at[1, lo], by),
                   (h.at[2, hi], land.at[0, hi], bx), (h.at[3, hi], land.at[1, hi], bx)]
    return copies


def _plan_scatter_relay_second(refs, me):
    del me
    na = len(refs) // 2
    copies = []
    for a in range(na):
        f, land = refs[a], refs[na + a]
        half = f.shape[1] // 2
        lo, hi = pl.ds(0, half), pl.ds(half, half)
        copies += [(f.at[1, lo], land.at[0, lo], _AXIS_BIT["x"]), (f.at[1, hi], land.at[0, hi], _AXIS_BIT["y"])]
    return copies


def _add_relayed(name, sums, received):
    _, r, c = sums.shape
    tr = _tile(r // 2, max(2 * SUBLANES, (1 << 20) // c), 2 * SUBLANES)
    n_half = (r // 2) // tr

    def body(own_ref, rec_ref, o_ref):
        o_ref[...] = (own_ref[...].astype(F32) + rec_ref[...].astype(F32)).astype(o_ref.dtype)

    def own_slot(j, i):
        return jnp.where(j == 0, 0, jnp.where(i < n_half, 2, 1))

    return pl.pallas_call(
        body, name=name, grid=(2, 2 * n_half), out_shape=jax.ShapeDtypeStruct((2, r, c), sums.dtype),
        in_specs=[pl.BlockSpec((None, tr, c), lambda j, i: (own_slot(j, i), i, 0)),
                  pl.BlockSpec((None, tr, c), lambda j, i: (j, i, 0))],
        out_specs=pl.BlockSpec((None, tr, c), lambda j, i: (j, i, 0)),
        compiler_params=_params("parallel", "parallel"),
    )(sums, received)


def _reduce_scatter_plan(tag, bufs):
    return ("rs%s_ici" % tag, _plan_scatter_ici, (len(_CHIP_MASKS) - 1) * (len(bufs) // 2))


def _reduce_scatter_middle(tag, flight, after, me):
    bufs = _reduce_scatter_add(tag, flight, after, me)
    name, plan, n_copies = _reduce_scatter_plan(tag, bufs)
    return _launch(name, bufs, plan, n_copies)


def _plan_everyone(refs, me):
    return [(ref.at[me], ref.at[me], m) for ref in refs for m in range(1, N_DEV)]


def _reduce_scatter_finish(flight, after):
    bufs = _land(flight, after)
    na = len(bufs) // 2
    return [(bufs[a], bufs[na + a]) for a in range(na)]


def _all_gather_2d(name, x, deps=()):
    r, c = x.shape

    def body(x_ref, out_ref, send_sems, recv_sems):
        me = _group_index(MESH_AXES)
        out_ref[me] = x_ref[...]
        copies = []
        for k in range(1, N_DEV):
            cp = pltpu.make_async_remote_copy(
                src_ref=x_ref, dst_ref=out_ref.at[me],
                send_sem=send_sems.at[k - 1], recv_sem=recv_sems.at[k - 1],
                device_id=_peer_device(MESH_AXES, k), device_id_type=pl.DeviceIdType.MESH)
            cp.start()
            copies.append(cp)
        for cp in copies:
            cp.wait()

    vmem = pl.BlockSpec(memory_space=pltpu.VMEM)
    return pl.pallas_call(
        _with_deps(body, 1, deps), name=name, out_shape=jax.ShapeDtypeStruct((N_DEV, r, c), x.dtype),
        in_specs=[vmem] + [_ANY] * len(deps), out_specs=vmem,
        scratch_shapes=[pltpu.SemaphoreType.DMA((N_DEV - 1,)), pltpu.SemaphoreType.DMA((N_DEV - 1,))],
    )(x, *deps)


def _sum_slots(name, buf, out_dtype):
    pre, n, r, c = buf.shape
    tr = _tile(r, max(SUBLANES * 2, (1 << 20) // c))

    def body(b_ref, o_ref):
        acc = b_ref[0].astype(F32)
        for q in range(1, n):
            acc = acc + b_ref[q].astype(F32)
        o_ref[...] = acc.astype(o_ref.dtype)

    return pl.pallas_call(
        body, name=name, grid=(pre, r // tr),
        out_shape=jax.ShapeDtypeStruct((pre, r, c), out_dtype),
        in_specs=[pl.BlockSpec((None, n, tr, c), lambda i, j: (i, 0, j, 0))],
        out_specs=pl.BlockSpec((None, tr, c), lambda i, j: (i, j, 0)),
        compiler_params=_params("parallel", "parallel"),
    )(buf)


def _matmul(name, mode, grid, a, a_spec, b, b_spec, out_shapes, out_specs, acc_shape,
            epilogue=None, extras=(), extra_specs=(), aliases=None, deps=()):
    nk = grid[2]
    n_extra = len(extras)
    n_out = len(out_shapes)

    def finish(acc, extra_refs, out_refs):
        if epilogue is None:
            out_refs[0][...] = acc.astype(out_refs[0].dtype)
        else:
            epilogue(acc, extra_refs, out_refs)

    def product(a_ref, b_ref):
        if len(b_ref.shape) == 2:
            return _dot(a_ref[...], b_ref[...], mode)
        width = a_ref.shape[1] // b_ref.shape[0]
        total = None
        for i in range(b_ref.shape[0]):
            part = _dot(a_ref[:, i * width:(i + 1) * width], b_ref[i], mode)
            total = part if total is None else total + part
        return total

    def body(*refs):
        a_ref, b_ref = refs[0], refs[1]
        extra_refs = refs[2:2 + n_extra]
        out_refs = refs[2 + n_extra:2 + n_extra + n_out]
        if nk == 1:
            finish(product(a_ref, b_ref), extra_refs, out_refs)
            return
        acc_ref = refs[-1]
        k = pl.program_id(2)

        @pl.when(k == 0)
        def _():
            acc_ref[...] = product(a_ref, b_ref)

        @pl.when((k > 0) & (k < nk - 1))
        def _():
            acc_ref[...] += product(a_ref, b_ref)

        @pl.when(k == nk - 1)
        def _():
            finish(acc_ref[...] + product(a_ref, b_ref), extra_refs, out_refs)

    scratch = [] if nk == 1 else [pltpu.VMEM(acc_shape, F32)]
    return pl.pallas_call(
        _with_deps(body, 2 + n_extra, deps), name=name, grid=grid, out_shape=tuple(out_shapes),
        in_specs=[a_spec, b_spec] + list(extra_specs) + [_ANY] * len(deps), out_specs=tuple(out_specs),
        scratch_shapes=scratch, input_output_aliases=aliases or {},
        compiler_params=_params("parallel", "parallel", "arbitrary"),
    )(a, b, *extras, *deps)


def _sds(shape, dtype):
    return jax.ShapeDtypeStruct(tuple(shape), dtype)


def _project_slots(name, h, w_full, slots, out_cols, proj_in=None, deps=()):
    s, d = h.shape
    _, _, wide = w_full.shape
    tm = _tile(s, 1024)
    n_in = 4 if proj_in is not None else 3

    def body(*refs):
        refs[-1][...] = _dot(refs[1][...], refs[2][...])

    grid_spec = pltpu.PrefetchScalarGridSpec(
        num_scalar_prefetch=1, grid=(s // tm, slots.shape[0]),
        in_specs=[pl.BlockSpec((tm, d), lambda i, j, sl: (i, 0)),
                  pl.BlockSpec((None, d, wide), lambda i, j, sl: (sl[j], 0, 0))]
        + [_ANY] * (n_in - 3 + len(deps)),
        out_specs=pl.BlockSpec((tm, wide), lambda i, j, sl: (i, sl[j])))
    extra = ([proj_in] if proj_in is not None else []) + list(deps)
    return pl.pallas_call(
        body, name=name, grid_spec=grid_spec, out_shape=_sds((s, out_cols), F32),
        input_output_aliases={3: 0} if proj_in is not None else {},
        compiler_params=_params("parallel", "arbitrary"),
    )(slots, h, w_full, *extra)


def _ada_forward(c_all, w_ada, b_shard):
    nb, d = c_all.shape
    w = w_ada.shape[1]
    tn = _tile(w, 512)

    def body(c_ref, w_ref, b_ref, o_ref):
        cv = c_ref[...]
        sc = cv * jax.nn.sigmoid(cv)
        o_ref[...] = jnp.dot(sc, w_ref[...], precision=lax.Precision.HIGHEST,
                             preferred_element_type=F32) + b_ref[...]

    return pl.pallas_call(
        body, name="ada_fwd", grid=(w // tn,), out_shape=_sds((nb, w), F32),
        in_specs=[pl.BlockSpec((nb, d), lambda j: (0, 0)), pl.BlockSpec((d, tn), lambda j: (0, j)),
                  pl.BlockSpec((1, tn), lambda j: (0, j))],
        out_specs=pl.BlockSpec((nb, tn), lambda j: (0, j)),
        compiler_params=_params("parallel"),
    )(c_all, w_ada, b_shard)


def _ada_weight_grad(c_all, dmod_cols, deps=()):
    nb, d = c_all.shape
    w = dmod_cols.shape[1]
    tn = _tile(w, 512)

    def body(c_ref, g_ref, o_ref):
        cv = c_ref[...]
        sc = cv * jax.nn.sigmoid(cv)
        o_ref[...] = lax.dot_general(sc, g_ref[...], _TN, precision=lax.Precision.HIGHEST,
                                     preferred_element_type=F32)

    return pl.pallas_call(
        _with_deps(body, 2, deps), name="ada_wgrad", grid=(w // tn,), out_shape=_sds((d, w), F32),
        in_specs=[pl.BlockSpec((nb, d), lambda j: (0, 0)), pl.BlockSpec((nb, tn), lambda j: (0, j))]
        + [_ANY] * len(deps),
        out_specs=pl.BlockSpec((d, tn), lambda j: (0, j)),
        compiler_params=_params("parallel"),
    )(c_all, dmod_cols, *deps)


def _norm_forward(name, x, norm_w, scale, shift, deps=()):
    s, d = x.shape
    tm = _tile(s, 256)

    def body(x_ref, w_ref, sc_ref, sh_ref, h_ref):
        xv = x_ref[...]
        r = lax.rsqrt(jnp.mean(xv * xv, axis=-1, keepdims=True) + NORM_EPS)
        h = (xv * r * w_ref[...]) * (1.0 + sc_ref[...]) + sh_ref[...]
        h_ref[...] = h.astype(BF16)

    vec = pl.BlockSpec((1, d), lambda i: (0, 0))
    row = pl.BlockSpec((tm, d), lambda i: (i, 0))
    return pl.pallas_call(
        _with_deps(body, 4, deps), name=name, grid=(s // tm,), out_shape=_sds((s, d), BF16),
        in_specs=[row, vec, vec, vec] + [_ANY] * len(deps), out_specs=row, compiler_params=_params("parallel"),
    )(x, norm_w, scale, shift, *deps)


def _norm_backward(name, dh, x, norm_w, scale, dres, gated=None, deps=()):
    s, d = x.shape
    tm = _tile(s, 256)
    n_in = 7 if gated else 5

    def body(*refs):
        dh_ref, x_ref, w_ref, sc_ref, dres_ref = refs[:5]
        dx_ref, dshift_ref, dscale_ref, dw_ref = refs[n_in:n_in + 4]
        sums = (dshift_ref, dscale_ref, dw_ref) + ((refs[n_in + 5],) if gated else ())

        @pl.when(pl.program_id(0) == 0)
        def _():
            for ref in sums:
                ref[...] = jnp.zeros_like(ref)

        xv = x_ref[...]
        g = dh_ref[...]
        r = lax.rsqrt(jnp.mean(xv * xv, axis=-1, keepdims=True) + NORM_EPS)
        n = xv * r
        gain = 1.0 + sc_ref[...]
        gn = g * n
        dshift_ref[...] += jnp.sum(g, axis=0, keepdims=True)
        dscale_ref[...] += jnp.sum(gn, axis=0, keepdims=True) * w_ref[...]
        dw_ref[...] += jnp.sum(gn, axis=0, keepdims=True) * gain
        dn = g * (w_ref[...] * gain)
        dx = dres_ref[...] + r * (dn - n * jnp.mean(dn * n, axis=-1, keepdims=True))
        dx_ref[...] = dx
        if gated:
            gate_ref, other_ref = refs[5:7]
            refs[n_in + 4][...] = (dx * gate_ref[...]).astype(BF16)
            refs[n_in + 5][...] += jnp.sum(dx * other_ref[...].astype(F32), axis=0, keepdims=True)

    vec = pl.BlockSpec((1, d), lambda i: (0, 0))
    row = pl.BlockSpec((tm, d), lambda i: (i, 0))
    vec_out = _sds((1, d), F32)
    return pl.pallas_call(
        _with_deps(body, n_in, deps), name=name, grid=(s // tm,),
        out_shape=(_sds((s, d), F32), vec_out, vec_out, vec_out) + ((_sds((s, d), BF16), vec_out) if gated else ()),
        in_specs=[row, row, vec, vec, row] + ([vec, row] if gated else []) + [_ANY] * len(deps),
        out_specs=(row, vec, vec, vec) + ((row, vec) if gated else ()),
        compiler_params=_params("arbitrary"),
    )(dh, x, norm_w, scale, dres, *(gated or ()), *deps)


def _split_bf16(v):
    hi = v.astype(BF16)
    lo = (v - hi.astype(F32)).astype(BF16)
    return hi, lo


def _pool_forward(proj, w_pool, pool_scale, deps=()):
    s = proj.shape[0]
    g_n, cg, _ = w_pool.shape
    t = POOL_TILE
    nt = s // t

    def body(cur_ref, prev_ref, wp_ref, sc_ref, pooled_ref, ya_ref):
        g = pl.program_id(0)
        ti = pl.program_id(1)
        win = jnp.left_shift(2, g)
        row = lax.broadcasted_iota(jnp.int32, (t, t), 0)
        col = lax.broadcasted_iota(jnp.int32, (t, t), 1)
        lag = row - col
        band_cur = ((lag >= 0) & (lag < win)).astype(BF16)
        band_prev = ((lag + t < win) & (ti > 0)).astype(BF16)
        u = cur_ref[...]
        u_hi, u_lo = _split_bf16(u)
        p_hi, p_lo = _split_bf16(prev_ref[...])
        wsum = (_dot(band_cur, u_hi) + _dot(band_cur, u_lo)
                + _dot(band_prev, p_hi) + _dot(band_prev, p_lo))
        tok = ti * t + lax.broadcasted_iota(jnp.int32, (t, 1), 0)
        count = jnp.minimum(tok + 1, win).astype(F32)
        pooled = (wsum / count - u).astype(BF16)
        pooled_ref[...] = pooled
        ya_ref[...] = (_dot(pooled, wp_ref[...]) * sc_ref[...]).astype(BF16)

    blk = pl.BlockSpec((t, cg), lambda g, i: (i, g))
    return pl.pallas_call(
        _with_deps(body, 4, deps), name="pool_fwd", grid=(g_n, nt),
        out_shape=(_sds((s, g_n * cg), BF16), _sds((s, g_n * cg), BF16)),
        in_specs=[blk, pl.BlockSpec((t, cg), lambda g, i: (jnp.maximum(i - 1, 0), g)),
                  pl.BlockSpec((None, cg, cg), lambda g, i: (g, 0, 0)),
                  pl.BlockSpec((1, cg), lambda g, i: (0, g))] + [_ANY] * len(deps),
        out_specs=(blk, blk), compiler_params=_params("parallel", "parallel"),
    )(proj, proj, w_pool, pool_scale, *deps)


def _pool_backward(dya, pooled, w_pool, pool_scale, dproj):
    s = dya.shape[0]
    g_n, cg, _ = w_pool.shape
    t = POOL_TILE
    nt = s // t

    def body(dya_ref, dya_next_ref, pooled_ref, wp_ref, sc_ref, dproj_in, du_ref, gw_ref, gs_ref):
        del dproj_in
        g = pl.program_id(0)
        ti = pl.program_id(1)

        @pl.when(ti == 0)
        def _():
            gw_ref[...] = jnp.zeros_like(gw_ref)
            gs_ref[...] = jnp.zeros_like(gs_ref)

        win = jnp.left_shift(2, g)
        wp = wp_ref[...]
        sc = sc_ref[...]
        pooled_v = pooled_ref[...]
        dya_v = dya_ref[...].astype(F32)
        mixed = _dot(pooled_v, wp)
        gs_ref[...] += jnp.sum(dya_v * mixed, axis=0, keepdims=True)
        dmixed = (dya_v * sc).astype(BF16)
        gw_ref[...] += _dot(pooled_v, dmixed, "tn")
        dpooled = _dot(dmixed, wp, "nt")
        dmixed_next = (dya_next_ref[...].astype(F32) * sc).astype(BF16)
        dpooled_next = _dot(dmixed_next, wp, "nt")
        tok = ti * t + lax.broadcasted_iota(jnp.int32, (t, 1), 0)
        e_cur = dpooled / jnp.minimum(tok + 1, win).astype(F32)
        e_next = dpooled_next / jnp.minimum(tok + t + 1, win).astype(F32)
        row = lax.broadcasted_iota(jnp.int32, (t, t), 0)
        col = lax.broadcasted_iota(jnp.int32, (t, t), 1)
        lead = col - row
        band_cur = ((lead >= 0) & (lead < win)).astype(BF16)
        band_next = ((lead + t < win) & (ti < nt - 1)).astype(BF16)
        c_hi, c_lo = _split_bf16(e_cur)
        n_hi, n_lo = _split_bf16(e_next)
        du = (_dot(band_cur, c_hi) + _dot(band_cur, c_lo)
              + _dot(band_next, n_hi) + _dot(band_next, n_lo)) - dpooled
        du_ref[...] = du.astype(BF16)

    blk = pl.BlockSpec((t, cg), lambda g, i: (i, g))
    du, gw, gs = pl.pallas_call(
        body, name="pool_bwd", grid=(g_n, nt),
        out_shape=(_sds(dproj.shape, BF16), _sds((g_n, cg, cg), F32), _sds((1, g_n * cg), F32)),
        in_specs=[blk, pl.BlockSpec((t, cg), lambda g, i: (jnp.minimum(i + 1, nt - 1), g)), blk,
                  pl.BlockSpec((None, cg, cg), lambda g, i: (g, 0, 0)),
                  pl.BlockSpec((1, cg), lambda g, i: (0, g)),
                  pl.BlockSpec(memory_space=pl.ANY)],
        out_specs=(blk, pl.BlockSpec((None, cg, cg), lambda g, i: (g, 0, 0)),
                   pl.BlockSpec((1, cg), lambda g, i: (0, g))),
        input_output_aliases={5: 0}, compiler_params=_params("parallel", "arbitrary"),
    )(dya, dya, pooled, w_pool, pool_scale, dproj)
    return du, gw, gs


def _qkv_prepare(proj, q_norm_w, k_norm_w, width, deps=()):
    s = proj.shape[0]
    tm = _tile(s, 256)
    heads = width // HEAD_DIM

    def body(q_ref, k_ref, v_ref, qw_ref, kw_ref, qn_ref, kn_ref, vb_ref):
        for h in range(heads):
            cols = slice(h * HEAD_DIM, (h + 1) * HEAD_DIM)
            for src, w_ref, dst in ((q_ref, qw_ref, qn_ref), (k_ref, kw_ref, kn_ref)):
                v = src[:, cols]
                r = lax.rsqrt(jnp.mean(v * v, axis=-1, keepdims=True) + NORM_EPS)
                dst[:, cols] = (v * r * w_ref[...]).astype(BF16)
        vb_ref[...] = v_ref[...].astype(BF16)

    vec = pl.BlockSpec((1, HEAD_DIM), lambda i: (0, 0))
    out_spec = pl.BlockSpec((tm, width), lambda i: (i, 0))
    return pl.pallas_call(
        _with_deps(body, 5, deps), name="qkv_prep", grid=(s // tm,),
        out_shape=(_sds((s, width), BF16),) * 3,
        in_specs=[pl.BlockSpec((tm, width), lambda i: (i, 1)), pl.BlockSpec((tm, width), lambda i: (i, 2)),
                  pl.BlockSpec((tm, width), lambda i: (i, 3)), vec, vec] + [_ANY] * len(deps),
        out_specs=(out_spec,) * 3, compiler_params=_params("parallel"),
    )(proj, proj, proj, q_norm_w, k_norm_w, *deps)


def _qk_norm_backward(name, dn, proj, col_block, norm_w, dproj, width, deps=()):
    s = proj.shape[0]
    tm = _tile(s, 256)
    heads = width // HEAD_DIM

    def body(dn_ref, q_ref, w_ref, dproj_in, dq_ref, gw_ref):
        del dproj_in

        @pl.when(pl.program_id(0) == 0)
        def _():
            gw_ref[...] = jnp.zeros_like(gw_ref)

        wv = w_ref[...]
        gw = jnp.zeros((1, HEAD_DIM), F32)
        for h in range(heads):
            cols = slice(h * HEAD_DIM, (h + 1) * HEAD_DIM)
            v = q_ref[:, cols]
            g = dn_ref[:, cols]
            r = lax.rsqrt(jnp.mean(v * v, axis=-1, keepdims=True) + NORM_EPS)
            n = v * r
            gw = gw + jnp.sum(g * n, axis=0, keepdims=True)
            gn = g * wv
            dq_ref[:, cols] = (r * (gn - n * jnp.mean(gn * n, axis=-1, keepdims=True))).astype(BF16)
        gw_ref[...] += gw

    blk = pl.BlockSpec((tm, width), lambda i: (i, col_block))
    return pl.pallas_call(
        _with_deps(body, 4, deps), name=name, grid=(s // tm,),
        out_shape=(_sds(dproj.shape, BF16), _sds((1, HEAD_DIM), F32)),
        in_specs=[pl.BlockSpec((tm, width), lambda i: (i, 0)), blk,
                  pl.BlockSpec((1, HEAD_DIM), lambda i: (0, 0)), pl.BlockSpec(memory_space=pl.ANY)]
        + [_ANY] * len(deps),
        out_specs=(blk, pl.BlockSpec((1, HEAD_DIM), lambda i: (0, 0))),
        input_output_aliases={3: 0}, compiler_params=_params("arbitrary"),
    )(dn, proj, norm_w, dproj, *deps)


def _strict_upper(n):
    row = lax.broadcasted_iota(jnp.int32, (n, n), 0)
    col = lax.broadcasted_iota(jnp.int32, (n, n), 1)
    return (row > col).astype(BF16)


def _strict_lower(n):
    row = lax.broadcasted_iota(jnp.int32, (n, n), 0)
    col = lax.broadcasted_iota(jnp.int32, (n, n), 1)
    return (row < col).astype(BF16)


def _cumulate(v, tri):
    return _dot(v.astype(BF16), tri)


def _log_sigmoid(z):
    return jnp.minimum(z, 0.0) - jnp.log(1.0 + jnp.exp(-jnp.abs(z)))


def _attention_forward(qn, kn, vb):
    s, width = qn.shape
    heads = width // HEAD_DIM
    tq, tk = Q_TILE, K_TILE
    hp = min(HEADS_PER_STEP, heads)
    assert tq == tk and s % tq == 0 and heads % hp == 0
    scale = 1.0 / math.sqrt(HEAD_DIM)

    def body(q_ref, k_ref, v_ref, o_ref, a_scr):
        qi = pl.program_id(1)
        upper = _strict_upper(tk)
        causal = lax.broadcasted_iota(jnp.int32, (tq, tk), 1) < lax.broadcasted_iota(jnp.int32, (tq, tk), 0)
        head_cols = [slice(u * HEAD_DIM, (u + 1) * HEAD_DIM) for u in range(hp)]

        def weights(kb, carry, masked):
            rows = pl.ds(pl.multiple_of(kb * tk, tk), tk)
            out = []
            for u, cols in enumerate(head_cols):
                later = carry[u]
                z = _dot(q_ref[:, cols], k_ref[rows, cols], "nt") * scale
                log_beta = _log_sigmoid(z)
                l = log_beta - z
                if masked:
                    l = jnp.where(causal, l, 0.0)
                a = jnp.exp(log_beta + _cumulate(l, upper) + later)
                if masked:
                    a = jnp.where(causal, a, 0.0)
                a_scr[u, :, rows] = a.astype(BF16)
                out.append(later + jnp.sum(l, axis=1, keepdims=True))
            return tuple(out)

        later = weights(qi, tuple(jnp.zeros((tq, 1), F32) for _ in range(hp)), True)
        lax.fori_loop(0, qi, lambda i, c: weights(qi - 1 - i, c, False), later)

        def mix(kb, accs):
            rows = pl.ds(pl.multiple_of(kb * tk, tk), tk)
            return tuple(acc + _dot(a_scr[u, :, rows], v_ref[rows, cols])
                         for u, (acc, cols) in enumerate(zip(accs, head_cols)))

        accs = lax.fori_loop(0, qi + 1, mix, tuple(jnp.zeros((tq, HEAD_DIM), F32) for _ in range(hp)))
        for acc, cols in zip(accs, head_cols):
            o_ref[:, cols] = acc.astype(BF16)

    full = pl.BlockSpec((s, hp * HEAD_DIM), lambda h, i: (0, h))
    blk = pl.BlockSpec((tq, hp * HEAD_DIM), lambda h, i: (i, h))
    return pl.pallas_call(
        body, name="attn_fwd", grid=(heads // hp, s // tq), out_shape=_sds((s, width), BF16),
        in_specs=[blk, full, full], out_specs=blk, scratch_shapes=[pltpu.VMEM((hp, tq, s), BF16)],
        compiler_params=_params("parallel", "parallel"),
    )(qn, kn, vb)


def _attention_backward(qn, kn, vb, dout, dproj, v_col_block, deps=()):
    s, width = qn.shape
    heads = width // HEAD_DIM
    tq, tk = Q_TILE, K_TILE
    hp = min(HEADS_PER_STEP_BWD, heads)
    nq = s // tq
    scale = 1.0 / math.sqrt(HEAD_DIM)
    v_block0 = v_col_block * (heads // hp)

    def body(q_ref, k_ref, v_ref, do_ref, dproj_in, dq_ref, dk_ref, dv_ref,
             a_scr, lb_scr, dz_scr, dkt_scr, dvt_scr):
        del dproj_in
        qi = pl.program_id(1)

        @pl.when(qi == 0)
        def _():
            dkt_scr[...] = jnp.zeros_like(dkt_scr)
            dvt_scr[...] = jnp.zeros_like(dvt_scr)

        upper = _strict_upper(tk)
        lower = _strict_lower(tk)
        causal = lax.broadcasted_iota(jnp.int32, (tq, tk), 1) < lax.broadcasted_iota(jnp.int32, (tq, tk), 0)
        head_cols = [slice(u * HEAD_DIM, (u + 1) * HEAD_DIM) for u in range(hp)]

        def weights(kb, carry, masked):
            rows = pl.ds(pl.multiple_of(kb * tk, tk), tk)
            out = []
            for u, cols in enumerate(head_cols):
                later = carry[u]
                z = _dot(q_ref[:, cols], k_ref[rows, cols], "nt") * scale
                log_beta = _log_sigmoid(z)
                l = log_beta - z
                if masked:
                    l = jnp.where(causal, l, 0.0)
                a = jnp.exp(log_beta + _cumulate(l, upper) + later)
                if masked:
                    a = jnp.where(causal, a, 0.0)
                a_scr[u, :, rows] = a
                lb_scr[u, :, rows] = log_beta
                out.append(later + jnp.sum(l, axis=1, keepdims=True))
            return tuple(out)

        zeros = tuple(jnp.zeros((tq, 1), F32) for _ in range(hp))
        later = weights(qi, zeros, True)
        lax.fori_loop(0, qi, lambda i, c: weights(qi - 1 - i, c, False), later)

        q_t = [jnp.transpose(q_ref[:, cols].astype(F32)).astype(BF16) for cols in head_cols]
        do_t = [jnp.transpose(do_ref[:, cols].astype(F32)).astype(BF16) for cols in head_cols]

        def scores(kb, carry, masked):
            rows = pl.ds(pl.multiple_of(kb * tk, tk), tk)
            out = []
            for u, cols in enumerate(head_cols):
                before = carry[u]
                beta = jnp.exp(lb_scr[u, :, rows])
                g = a_scr[u, :, rows] * _dot(do_ref[:, cols], v_ref[rows, cols], "nt")
                p = _cumulate(g, lower) + before
                dz = g - (g + p) * beta
                if masked:
                    dz = jnp.where(causal, dz, 0.0)
                dz_scr[u, :, rows] = (dz * scale).astype(BF16)
                out.append(before + jnp.sum(g, axis=1, keepdims=True))
            return tuple(out)

        before = lax.fori_loop(0, qi, lambda i, c: scores(i, c, False), zeros)
        scores(qi, before, True)

        def products(kb, dqs):
            rows = pl.ds(pl.multiple_of(kb * tk, tk), tk)
            out = []
            for u, cols in enumerate(head_cols):
                dz = dz_scr[u, :, rows]
                dkt_scr[cols, rows] += _dot(q_t[u], dz)
                dvt_scr[cols, rows] += _dot(do_t[u], a_scr[u, :, rows].astype(BF16))
                out.append(dqs[u] + _dot(dz, k_ref[rows, cols]))
            return tuple(out)

        dqs = lax.fori_loop(0, qi + 1, products, tuple(jnp.zeros((tq, HEAD_DIM), F32) for _ in range(hp)))
        for u, cols in enumerate(head_cols):
            dq_ref[:, cols] = dqs[u]

        @pl.when(qi == nq - 1)
        def _():
            dk_ref[...] = jnp.transpose(dkt_scr[...])
            dv_ref[...] = jnp.transpose(dvt_scr[...]).astype(BF16)

    wide = hp * HEAD_DIM
    full = pl.BlockSpec((s, wide), lambda h, i: (0, h))
    blk = pl.BlockSpec((tq, wide), lambda h, i: (i, h))
    return pl.pallas_call(
        _with_deps(body, 5, deps), name="attn_bwd", grid=(heads // hp, nq),
        out_shape=(_sds((s, width), F32), _sds((s, width), F32), _sds(dproj.shape, BF16)),
        in_specs=[blk, full, full, blk, pl.BlockSpec(memory_space=pl.ANY)] + [_ANY] * len(deps),
        out_specs=(blk, full, pl.BlockSpec((s, wide), lambda h, i: (0, v_block0 + h))),
        scratch_shapes=[pltpu.VMEM((hp, tq, s), F32), pltpu.VMEM((hp, tq, s), F32), pltpu.VMEM((hp, tq, s), BF16),
                        pltpu.VMEM((wide, s), F32), pltpu.VMEM((wide, s), F32)],
        input_output_aliases={4: 2}, compiler_params=_params("parallel", "arbitrary"),
    )(qn, kn, vb, dout, dproj, *deps)


def _place_columns(name, src, dst, col_block):
    s, w = src.shape
    tm = _tile(s, 512)

    def body(src_ref, dst_in, out_ref):
        del dst_in
        out_ref[...] = src_ref[...]

    return pl.pallas_call(
        body, name=name, grid=(s // tm,), out_shape=_sds(dst.shape, dst.dtype),
        in_specs=[pl.BlockSpec((tm, w), lambda i: (i, 0)), pl.BlockSpec(memory_space=pl.ANY)],
        out_specs=pl.BlockSpec((tm, w), lambda i: (i, col_block)),
        input_output_aliases={1: 0}, compiler_params=_params("parallel"),
    )(src, dst)


def _cast_into_slot(name, x, slot):
    r, c = x.shape
    tr = _tile(r, max(SUBLANES * 2, (1 << 20) // c), SUBLANES * 2)

    def body(slot_ref, x_ref, o_ref):
        del slot_ref
        o_ref[...] = x_ref[...].astype(BF16)

    grid_spec = pltpu.PrefetchScalarGridSpec(
        num_scalar_prefetch=1, grid=(r // tr,),
        in_specs=[pl.BlockSpec((tr, c), lambda i, slot_ref: (i, 0))],
        out_specs=pl.BlockSpec((None, tr, c), lambda i, slot_ref: (slot_ref[0], i, 0)))
    return pl.pallas_call(
        body, name=name, grid_spec=grid_spec, out_shape=_sds((N_DEV, r, c), BF16),
        compiler_params=_params("parallel"),
    )(slot, x)


def _adamw_update(gv, w_ref, m_ref, v_ref, d_ref, nm_ref, nv_ref):
    c1 = 1.0 - ADAM_B1 ** ADAM_STEP
    c2 = 1.0 - ADAM_B2 ** ADAM_STEP
    nm = ADAM_B1 * m_ref[...] + (1.0 - ADAM_B1) * gv
    nv = ADAM_B2 * v_ref[...] + (1.0 - ADAM_B2) * (gv * gv)
    d_ref[...] = -ADAM_LR * ((nm / c1) / (jnp.sqrt(nv / c2) + ADAM_EPS) + ADAM_WD * w_ref[...])
    nm_ref[...] = nm
    nv_ref[...] = nv


def _adamw(name, w, g, m, v, deps=()):
    r, c = w.shape
    tr = _tile(r, max(SUBLANES, (1 << 19) // c))

    def body(w_ref, g_ref, m_ref, v_ref, d_ref, nm_ref, nv_ref):
        _adamw_update(g_ref[...], w_ref, m_ref, v_ref, d_ref, nm_ref, nv_ref)

    blk = pl.BlockSpec((tr, c), lambda i: (i, 0))
    return pl.pallas_call(
        _with_deps(body, 4, deps), name=name, grid=(r // tr,), out_shape=(_sds((r, c), F32),) * 3,
        in_specs=[blk] * 4 + [_ANY] * len(deps), out_specs=(blk,) * 3, compiler_params=_params("parallel"),
    )(w, g, m, v, *deps)


def _adamw_summed(name, w, own, received, m, v, deps=()):
    r, c = w.shape
    nj = received.shape[0]
    tr = _tile(r, max(2 * SUBLANES, (1 << 19) // c), 2 * SUBLANES)

    def body(w_ref, own_ref, rec_ref, m_ref, v_ref, g_ref, d_ref, nm_ref, nv_ref):
        gv = own_ref[...].astype(F32)
        for j in range(nj):
            gv = gv + rec_ref[j].astype(F32)
        g_ref[...] = gv
        _adamw_update(gv, w_ref, m_ref, v_ref, d_ref, nm_ref, nv_ref)

    blk = pl.BlockSpec((tr, c), lambda i: (i, 0))
    return pl.pallas_call(
        _with_deps(body, 5, deps), name=name, grid=(r // tr,), out_shape=(_sds((r, c), F32),) * 4,
        in_specs=[blk, pl.BlockSpec((None, tr, c), lambda i: (0, i, 0)),
                  pl.BlockSpec((nj, tr, c), lambda i: (0, i, 0)), blk, blk] + [_ANY] * len(deps),
        out_specs=(blk,) * 4, compiler_params=_params("parallel"),
    )(w, own, received, m, v, *deps)


def _rows_of_lanes(v):
    rows = v.shape[1] // LANES
    out = v.reshape(rows, LANES)
    pad = (-rows) % SUBLANES
    if pad:
        out = jnp.pad(out, ((0, pad), (0, 0)))
    return out


def kernel(x, c, w_ada, b_ada, norm1_w, w_in, q_norm_w, k_norm_w, w_pool, pool_scale, w_a_up, w_b_up, w_o, norm2_w, w_ff1, w_ff2, loss_target, m_w_ada, m_b_ada, m_norm1_w, m_w_in, m_q_norm_w, m_k_norm_w, m_w_pool, m_pool_scale, m_w_a_up, m_w_b_up, m_w_o, m_norm2_w, m_w_ff1, m_w_ff2, v_w_ada, v_b_ada, v_norm1_w, v_w_in, v_q_norm_w, v_k_norm_w, v_w_pool, v_pool_scale, v_w_a_up, v_w_b_up, v_w_o, v_norm2_w, v_w_ff1, v_w_ff2):
    _, s, d = x.shape
    half = d // 2
    d8 = d // N_DEV
    n_groups = len(POOL_WINDOWS)
    cg = half // n_groups
    me = _group_index(MESH_AXES)

    x2 = x[0]
    target = loss_target[0]

    my_slot = jnp.reshape(me, (1,)).astype(jnp.int32)

    def cast(i, t):
        return _cast_into_slot("cast_w%d" % i, t, my_slot)

    def gather_start(tag, bufs, phase):
        if phase < 2:
            return _launch("ag%s_ici%d" % (tag, phase), bufs, _plan_gather_ici(phase), (2, 4)[phase] * len(bufs))
        return _launch("ag%s_d2d" % tag, bufs, _plan_gather_d2d, len(_CHIP_MASKS) * len(bufs))

    def gather_plans(tag, n_bufs, phase):
        if phase < 2:
            return ("ag%s_ici%d" % (tag, phase), _plan_gather_ici(phase), (2, 4)[phase] * n_bufs)
        return ("ag%s_d2d" % tag, _plan_gather_d2d, len(_CHIP_MASKS) * n_bufs)

    bx, by, bc = _AXIS_BIT["x"], _AXIS_BIT["y"], _AXIS_BIT["c"]
    buf_a = [cast(0, w_in[0])]

    c_all = _all_gather_2d("ag_c", c.reshape(d // LANES, LANES), deps=tuple(buf_a)).reshape(N_DEV, d)
    fl_a, tok = _launch("agA_near", buf_a, _plan_neighbours, 2, deps=(c_all,))
    wa = w_ada.shape[2]
    b_shard = lax.dynamic_slice_in_dim(b_ada, me * wa, wa, axis=1)
    mod_part = _ada_forward(c_all, w_ada[0], b_shard)
    buf_b = [cast(1, w_pool[0].reshape(-1, cg)), cast(2, w_a_up[0]), cast(3, w_b_up[0]), cast(4, w_o[0])]
    buf_c = [cast(5, w_ff1[0])]
    buf_e = [cast(6, w_ff2[0])]
    mod_all = _all_gather_2d("ag_mod", mod_part.reshape(N_DEV * wa // LANES, LANES),
                             deps=tuple([tok] + buf_b + buf_c + buf_e))
    mod_all = mod_all.reshape(N_DEV, N_DEV, wa)
    mod = lax.dynamic_slice_in_dim(mod_all, me, 1, axis=1).reshape(1, N_MOD * d)
    shift1, scale1, gate1, shift2, scale2, gate2 = [mod[:, i * d:(i + 1) * d] for i in range(N_MOD)]

    h = _norm_forward("norm1_fwd", x2, norm1_w, scale1, shift1)
    buf_a = _land(fl_a, h)
    near = (0, bx, by)
    (fl_far, fl_near), tok = _launch_groups(
        "agA_far", [(buf_a, [("agA_far", _plan_diagonal, 2), ("agA_near_d2d", _plan_d2d(near), len(near))])])
    (fl_b, fl_c, fl_e), tok = _launch_groups(
        "agBCE_ici0", [(buf_b, [gather_plans("B", len(buf_b), 0)]), (buf_c, [gather_plans("C", 1, 0)]),
                       (buf_e, [gather_plans("E", 1, 0)])])

    tm = _tile(s, 1024)
    tk = _tile(d, 2048)
    te = _tile(d, 512)

    own_slots = jnp.stack([me ^ m for m in near]).astype(jnp.int32)
    far_slots = jnp.stack([me ^ bx ^ by ^ f for f in (0, bc)]).astype(jnp.int32)
    buf_a = fl_near[-1]
    proj = _project_slots("proj_own", h, buf_a[0], own_slots, 4 * d, deps=(tok,))
    buf_a = _land(fl_near, proj, bufs=buf_a)
    proj = _project_slots("proj_sibling", h, buf_a[0], own_slots ^ bc, 4 * d, proj_in=proj)
    buf_a = _land(fl_far, proj, bufs=buf_a)
    fl_a, tok = _launch("agA_far_d2d", buf_a, _plan_d2d((bx | by,)), 1)
    w_in_f, = _land(fl_a, tok)
    proj = _project_slots("proj_far", h, w_in_f, far_slots, 4 * d, proj_in=proj)
    buf_b = _land(fl_b, proj)
    buf_c = _land(fl_c, proj)
    (fl_b, fl_c), tok = _launch_groups(
        "agBC_ici1", [(buf_b, [gather_plans("B", len(buf_b), 1)]), (buf_c, [gather_plans("C", 1, 1)])])

    qn, kn, vb = _qkv_prepare(proj, q_norm_w, k_norm_w, half, deps=(tok,))
    attn = _attention_forward(qn, kn, vb)
    buf_b = _land(fl_b, attn)
    buf_e = _land(fl_e, attn)
    (fl_b, fl_e), tok = _launch_groups(
        "agB_d2d_E_ici1", [(buf_b, [gather_plans("B", len(buf_b), 2)]), (buf_e, [gather_plans("E", 1, 1)])])
    w_pool_f, w_a_f, w_b_f, w_o_f = _land(fl_b, tok)
    rows_pool = cg // N_DEV
    w_pool_f = w_pool_f.reshape(N_DEV, n_groups, rows_pool, cg).transpose(1, 0, 2, 3).reshape(n_groups, cg, cg)
    w_o_f = w_o_f.reshape(d, d)
    pooled, ya_in = _pool_forward(proj, w_pool_f, pool_scale)

    def merge_epilogue(ga_ref, gb_ref, ya, yb, out_refs):
        merged_ref, ya_ref, yb_ref = out_refs
        merged = jax.nn.sigmoid(ga_ref[...]) * ya + jax.nn.sigmoid(gb_ref[...]) * yb
        merged_ref[...] = merged.astype(BF16)
        ya_ref[...] = ya.astype(BF16)
        yb_ref[...] = yb.astype(BF16)

    def up_body(a1_ref, b1_ref, a2_ref, b2_ref, ga_ref, gb_ref, *out_refs):
        merge_epilogue(ga_ref, gb_ref, _dot(a1_ref[...], b1_ref[...]), _dot(a2_ref[...], b2_ref[...]), out_refs)

    ga_blk0 = 2 * d // d8
    gb_blk0 = 3 * d // d8
    tu = s
    a_spec = pl.BlockSpec((tu, half), lambda i, j: (i, 0))
    wup_spec = pl.BlockSpec((None, half, d8), lambda i, j: (j, 0, 0))
    o_blk = pl.BlockSpec((tu, d8), lambda i, j: (i, j))
    merged, y_a, y_b = pl.pallas_call(
        up_body, name="up_merge", grid=(s // tu, N_DEV), out_shape=(_sds((s, d), BF16),) * 3,
        in_specs=[a_spec, wup_spec, a_spec, wup_spec,
                  pl.BlockSpec((tu, d8), lambda i, j: (i, ga_blk0 + j)),
                  pl.BlockSpec((tu, d8), lambda i, j: (i, gb_blk0 + j))],
        out_specs=(o_blk,) * 3, compiler_params=_params("parallel", "parallel"),
    )(ya_in, w_a_f, attn, w_b_f, proj, proj)
    buf_c = _land(fl_c, merged)
    fl_c, tok_c = gather_start("C", buf_c, 2)

    tn = _tile(d, 1024)

    def oproj_epilogue(acc, extra_refs, out_refs):
        x_ref, g_ref = extra_refs
        x1_ref, o_ref = out_refs
        x1_ref[...] = x_ref[...] + g_ref[...] * acc
        o_ref[...] = acc.astype(BF16)

    mn_blk = pl.BlockSpec((tm, tn), lambda i, j, k: (i, j))
    e_blk = pl.BlockSpec((tm, te), lambda i, j, k: (i, j))
    e_vec = pl.BlockSpec((1, te), lambda i, j, k: (0, j))
    s_blk = pl.BlockSpec((s, te), lambda i, j, k: (i, j))
    x1, o_act = _matmul(
        "oproj", "nn", (1, d // te, d // tk), merged, pl.BlockSpec((s, tk), lambda i, j, k: (i, k)),
        w_o_f, pl.BlockSpec((tk, te), lambda i, j, k: (k, j)),
        [_sds((s, d), F32), _sds((s, d), BF16)], [s_blk, s_blk], (s, te),
        epilogue=oproj_epilogue, extras=(x2, gate1), extra_specs=(s_blk, e_vec), deps=(tok_c,))

    h2 = _norm_forward("norm2_fwd", x1, norm2_w, scale2, shift2)
    w_ff1_f, = _land(fl_c, h2)

    def ff1_epilogue(acc, extra_refs, out_refs):
        r = jnp.maximum(acc, 0.0)
        out_refs[0][...] = r.astype(BF16)
        out_refs[1][...] = (r * r).astype(BF16)

    n_rows = s // tm
    first = max(n_rows // 2, 1)

    def ff1_rows(name, row0, rows, prior, deps):
        blk = pl.BlockSpec((tm, half), lambda i, j, k: (i + row0, j))
        return _matmul(
            name, "nn", (rows, N_DEV, d // tk), h2, pl.BlockSpec((tm, tk), lambda i, j, k: (i + row0, k)),
            w_ff1_f, pl.BlockSpec((None, tk, half), lambda i, j, k: (j, k, 0)),
            [_sds((s, 4 * d), BF16)] * 2, [blk, blk], (tm, half), epilogue=ff1_epilogue,
            extras=prior, extra_specs=[_ANY] * len(prior), aliases={2 + n: n for n in range(len(prior))}, deps=deps)

    relu, act = ff1_rows("ff1_a", 0, first, (), ())
    if n_rows > first:
        buf_e = _land(fl_e, act)
        fl_e, tok_e = gather_start("E", buf_e, 2)
        relu, act = ff1_rows("ff1_b", first, n_rows - first, (relu, act), (tok_e,))
    else:
        fl_e, tok_e = gather_start("E", _land(fl_e, act), 2)
    w_ff2_f, = _land(fl_e, act if n_rows > first else tok_e)
    w_ff2_f = w_ff2_f.reshape(4 * d, d)

    def ff2_epilogue(acc, extra_refs, out_refs):
        x1_ref, g_ref, t_ref = extra_refs
        df_ref, dy_ref, sq_ref, dgate_ref = out_refs
        gate = g_ref[...]
        err = x1_ref[...] + gate * acc - t_ref[...]
        dyv = err * (1.0 / d)
        dy_ref[...] = dyv
        df_ref[...] = (dyv * gate).astype(BF16)
        sq_ref[...] = jnp.full(sq_ref.shape, jnp.sum(err * err), F32)
        dgate_ref[...] = jnp.broadcast_to(jnp.sum(dyv * acc, axis=0, keepdims=True), dgate_ref.shape)

    df, dy, sq, dgate2_parts = _matmul(
        "ff2", "nn", (s // tm, d // te, 2 * d // tk), act, pl.BlockSpec((tm, 2 * tk), lambda i, j, k: (i, k)),
        w_ff2_f, pl.BlockSpec((2 * tk, te), lambda i, j, k: (k, j)),
        [_sds((s, d), BF16), _sds((s, d), F32), _sds((s // tm * SUBLANES, d // te * LANES), F32),
         _sds((s // tm * SUBLANES, d), F32)],
        [e_blk, e_blk, pl.BlockSpec((SUBLANES, LANES), lambda i, j, k: (i, j)),
         pl.BlockSpec((SUBLANES, te), lambda i, j, k: (i, j))], (tm, te),
        epilogue=ff2_epilogue, extras=(x1, gate2, target), extra_specs=(e_blk, e_vec, e_blk))
    loss_local = (0.5 / d) * jnp.sum(sq[::SUBLANES, ::LANES])
    dgate2 = jnp.sum(dgate2_parts[::SUBLANES], axis=0, keepdims=True)

    tok_k = _tile(s, 2048)
    tw = _tile(d, 1024)
    g_ff2 = _matmul(
        "g_ff2", "tn", (4 * d // tw, d // tn, s // tok_k), act, pl.BlockSpec((tok_k, tw), lambda i, j, k: (k, i)),
        df, pl.BlockSpec((tok_k, tn), lambda i, j, k: (k, j)),
        [_sds((4 * d, d), BF16)], [pl.BlockSpec((tw, tn), lambda i, j, k: (i, j))], (tw, tn))[0]

    def da_epilogue(acc, extra_refs, out_refs):
        out_refs[0][...] = (acc * (2.0 * extra_refs[0][...].astype(F32))).astype(BF16)

    big_blk = pl.BlockSpec((tm, tn), lambda i, j, k: (i, j))
    fl_f2, tok = _reduce_scatter_start("F2", [g_ff2.reshape(N_DEV, half, d)])
    df1 = _matmul(
        "da_ff", "nt", (s // tm, 4 * d // tn, d // tk), df, pl.BlockSpec((tm, tk), lambda i, j, k: (i, k)),
        w_ff2_f, pl.BlockSpec((tn, tk), lambda i, j, k: (j, k)),
        [_sds((s, 4 * d), BF16)], [big_blk], (tm, tn),
        epilogue=da_epilogue, extras=(relu,), extra_specs=(big_blk,), deps=(tok,))[0]

    fl_f2, tok = _reduce_scatter_middle("F2", fl_f2, df1, me)
    g_ff1 = _matmul(
        "g_ff1", "tn", (d // tw, N_DEV, s // tok_k), h2, pl.BlockSpec((tok_k, tw), lambda i, j, k: (k, i)),
        df1, pl.BlockSpec((tok_k, half), lambda i, j, k: (k, j)),
        [_sds((N_DEV, d, half), BF16)], [pl.BlockSpec((None, tw, half), lambda i, j, k: (j, i, 0))], (tw, half),
        deps=(tok,))[0]

    fl_f1, tok = _reduce_scatter_start("F1", [g_ff1])
    dh2 = _matmul(
        "dh2", "nt", (s // tm, d // tn, N_DEV // 2), df1, pl.BlockSpec((tm, 2 * half), lambda i, j, k: (i, k)),
        w_ff1_f, pl.BlockSpec((2, tn, half), lambda i, j, k: (k, j, 0)),
        [_sds((s, d), F32)], [mn_blk], (tm, tn), deps=(tok,))[0]

    sum_ff2, = _reduce_scatter_finish(fl_f2, dh2)
    fl_f1, tok = _reduce_scatter_middle("F1", fl_f1, dh2, me)
    dx1, dshift2, dscale2, g_norm2, do, dgate1 = _norm_backward(
        "norm2_bwd", dh2, x1, norm2_w, scale2, dy, gated=(gate1, o_act), deps=(tok,))

    g_o = _matmul(
        "g_o", "tn", (d // tw, d // tn, s // tok_k), merged, pl.BlockSpec((tok_k, tw), lambda i, j, k: (k, i)),
        do, pl.BlockSpec((tok_k, tn), lambda i, j, k: (k, j)),
        [_sds((d, d), BF16)], [pl.BlockSpec((tw, tn), lambda i, j, k: (i, j))], (tw, tn))[0]

    def merge_bwd_epilogue(acc, extra_refs, out_refs):
        ga_ref, gb_ref, ya_ref, yb_ref = extra_refs
        dya_ref, dyb_ref, dga_ref, dgb_ref = out_refs
        sa = jax.nn.sigmoid(ga_ref[...])
        sb = jax.nn.sigmoid(gb_ref[...])
        dya_ref[...] = (acc * sa).astype(BF16)
        dyb_ref[...] = (acc * sb).astype(BF16)
        dga_ref[...] = (acc * ya_ref[...].astype(F32) * (sa * (1.0 - sa))).astype(BF16)
        dgb_ref[...] = (acc * yb_ref[...].astype(F32) * (sb * (1.0 - sb))).astype(BF16)

    td = _tile(d, 256)
    nb = d // td
    d_blk = pl.BlockSpec((s, td), lambda i, j, k: (i, j))
    dy_a, dy_b, dproj, dg_b = _matmul(
        "dmerged", "nt", (1, nb, d // tk), do, pl.BlockSpec((s, tk), lambda i, j, k: (i, k)),
        w_o_f, pl.BlockSpec((td, tk), lambda i, j, k: (j, k)),
        [_sds((s, d), BF16), _sds((s, d), BF16), _sds((s, 4 * d), BF16), _sds((s, d), BF16)],
        [d_blk, d_blk, pl.BlockSpec((s, td), lambda i, j, k: (i, 2 * nb + j)), d_blk], (s, td),
        epilogue=merge_bwd_epilogue, extras=(proj, proj, y_a, y_b),
        extra_specs=(pl.BlockSpec((s, td), lambda i, j, k: (i, 2 * nb + j)),
                     pl.BlockSpec((s, td), lambda i, j, k: (i, 3 * nb + j)), d_blk, d_blk))
    dproj = _place_columns("place_dgb", dg_b, dproj, 3)

    up_a = pl.BlockSpec((tok_k, half), lambda i, j, k: (k, 0))
    up_b = pl.BlockSpec((tok_k, d8), lambda i, j, k: (k, j))
    up_o = pl.BlockSpec((None, half, d8), lambda i, j, k: (j, 0, 0))
    g_a_up = _matmul("g_a_up", "tn", (1, N_DEV, s // tok_k), ya_in, up_a, dy_a, up_b,
                     [_sds((N_DEV, half, d8), BF16)], [up_o], (half, d8))[0]
    g_b_up = _matmul("g_b_up", "tn", (1, N_DEV, s // tok_k), attn, up_a, dy_b, up_b,
                     [_sds((N_DEV, half, d8), BF16)], [up_o], (half, d8))[0]
    slabs = 4
    dn_a = pl.BlockSpec((tm, slabs * d8), lambda i, j, k: (i, k))
    dn_b = pl.BlockSpec((slabs, half, d8), lambda i, j, k: (k, 0, 0))
    dn_o = pl.BlockSpec((tm, half), lambda i, j, k: (i, 0))
    dya_in = _matmul("d_ya_in", "nt", (s // tm, 1, N_DEV // slabs), dy_a, dn_a, w_a_f, dn_b,
                     [_sds((s, half), BF16)], [dn_o], (tm, half))[0]
    dattn = _matmul("d_attn", "nt", (s // tm, 1, N_DEV // slabs), dy_b, dn_a, w_b_f, dn_b,
                    [_sds((s, half), BF16)], [dn_o], (tm, half))[0]

    dproj, g_pool, g_pool_scale = _pool_backward(dya_in, pooled, w_pool_f, pool_scale, dproj)
    sum_ff1, = _reduce_scatter_finish(fl_f1, g_pool)
    g_pool_send = g_pool.astype(BF16).reshape(n_groups, N_DEV, rows_pool, cg).transpose(1, 0, 2, 3)
    g_pool_send = g_pool_send.reshape(N_DEV, n_groups * rows_pool, cg)
    fl_b, tok = _reduce_scatter_start("B", [g_pool_send, g_a_up, g_b_up, g_o.reshape(N_DEV, d8, d)])
    dqn, dkn, dproj = _attention_backward(qn, kn, vb, dattn, dproj, 3, deps=(tok,))
    fl_b, tok = _reduce_scatter_middle("B", fl_b, dqn, me)
    dproj, g_qnorm = _qk_norm_backward("qnorm_bwd", dqn, proj, 1, q_norm_w, dproj, half, deps=(tok,))
    dproj, g_knorm = _qk_norm_backward("knorm_bwd", dkn, proj, 2, k_norm_w, dproj, half)

    g_in = _matmul(
        "g_in", "tn", (d // tw, N_DEV, s // tok_k), h, pl.BlockSpec((tok_k, tw), lambda i, j, k: (k, i)),
        dproj, pl.BlockSpec((tok_k, half), lambda i, j, k: (k, j)),
        [_sds((N_DEV, d, half), BF16)], [pl.BlockSpec((None, tw, half), lambda i, j, k: (j, i, 0))], (tw, half))[0]
    fl_in, tok = _reduce_scatter_start("I", [g_in])
    dh = _matmul(
        "dh", "nt", (s // tm, d // tn, N_DEV // 2), dproj, pl.BlockSpec((tm, 2 * half), lambda i, j, k: (i, k)),
        w_in_f, pl.BlockSpec((2, tn, half), lambda i, j, k: (k, j, 0)),
        [_sds((s, d), F32)], [mn_blk], (tm, tn), deps=(tok,))[0]
    sum_pool, sum_a_up, sum_b_up, sum_o = _reduce_scatter_finish(fl_b, dh)
    buf_in = _reduce_scatter_add("I", fl_in, dh, me, n_land=2)
    grad_x, dshift1, dscale1, g_norm1 = _norm_backward("norm1_bwd", dh, x2, norm1_w, scale1, dx1, deps=tuple(buf_in[:1]))

    dmod = jnp.concatenate([dshift1, dscale1, dgate1, dshift2, dscale2, dgate2], axis=1)
    pieces = [dmod, g_norm1, g_norm2, g_pool_scale, g_qnorm, g_knorm, jnp.full((1, LANES), loss_local, F32)]
    packed_rows = [_rows_of_lanes(p) for p in pieces]
    offsets = [0]
    for p in packed_rows:
        offsets.append(offsets[-1] + p.shape[0])
    packed = jnp.concatenate(packed_rows, axis=0)
    small_buf = lax.dynamic_update_slice(jnp.zeros((N_DEV,) + packed.shape, F32), packed[None], (me, 0, 0))
    (fl_small, fl_in), tok = _launch_groups(
        "rsI_relay0_small", [([small_buf], [("ag_small", _plan_everyone, N_DEV - 1)]),
                             (buf_in, [("rsI_relay0", _plan_scatter_relay_first, 4)])])
    small_all, = _land(fl_small, tok)
    small_sum = _sum_slots("small_sum", small_all[None], F32)[0]

    def unpack(i, width):
        return small_sum[offsets[i]:offsets[i] + width // LANES].reshape(1, width)

    g_b_ada = unpack(0, N_MOD * d)
    g_norm1_w = unpack(1, d)
    g_norm2_w = unpack(2, d)
    g_pool_scale_w = unpack(3, half)
    g_q_norm_w = unpack(4, HEAD_DIM)
    g_k_norm_w = unpack(5, HEAD_DIM)
    loss = unpack(6, LANES)[0, 0]
    dmod_all = small_all[:, :N_MOD * d // LANES].reshape(N_DEV, N_MOD * d)
    dmod_cols = lax.dynamic_slice_in_dim(dmod_all, me * wa, wa, axis=1)

    grads = {
        "b_ada": g_b_ada, "norm1_w": g_norm1_w,
        "q_norm_w": g_q_norm_w, "k_norm_w": g_k_norm_w,
        "pool_scale": g_pool_scale_w, "norm2_w": g_norm2_w,
    }
    sums = {"w_pool": sum_pool, "w_a_up": sum_a_up, "w_b_up": sum_b_up, "w_o": sum_o,
            "w_ff1": sum_ff1, "w_ff2": sum_ff2}
    weights = {"w_ada": (w_ada, m_w_ada, v_w_ada), "b_ada": (b_ada, m_b_ada, v_b_ada),
               "norm1_w": (norm1_w, m_norm1_w, v_norm1_w), "w_in": (w_in, m_w_in, v_w_in),
               "q_norm_w": (q_norm_w, m_q_norm_w, v_q_norm_w), "k_norm_w": (k_norm_w, m_k_norm_w, v_k_norm_w),
               "w_pool": (w_pool, m_w_pool, v_w_pool), "pool_scale": (pool_scale, m_pool_scale, v_pool_scale),
               "w_a_up": (w_a_up, m_w_a_up, v_w_a_up), "w_b_up": (w_b_up, m_w_b_up, v_w_b_up),
               "w_o": (w_o, m_w_o, v_w_o), "norm2_w": (norm2_w, m_norm2_w, v_norm2_w),
               "w_ff1": (w_ff1, m_w_ff1, v_w_ff1), "w_ff2": (w_ff2, m_w_ff2, v_w_ff2)}
    order = list(weights)
    deltas, new_m, new_v = {}, {}, {}
    def adam(name, deps=()):
        wt, mt, vt = weights[name]
        shape = wt.shape
        flat = (-1, shape[-1])
        if name in sums:
            own, received = sums[name]
            g, dl, nm, nv = _adamw_summed("adamw_" + name, wt.reshape(flat), own, received,
                                          mt.reshape(flat), vt.reshape(flat), deps=deps)
            grads[name] = g.reshape(shape)
        else:
            dl, nm, nv = _adamw("adamw_" + name, wt.reshape(flat), grads[name].reshape(flat),
                                mt.reshape(flat), vt.reshape(flat), deps=deps)
        deltas[name], new_m[name], new_v[name] = dl.reshape(shape), nm.reshape(shape), nv.reshape(shape)

    behind_first = list(sums)
    for name in behind_first:
        adam(name, deps=(tok,))
    grads["w_ada"] = _ada_weight_grad(c_all, dmod_cols, deps=(tok,))[None]
    chip_sums, relayed = _land(fl_in, [deltas[n] for n in behind_first] + [grads["w_ada"]])
    passed_on = _add_relayed("rsI_add_relay", chip_sums, relayed)
    fl_in, tok = _launch("rsI_relay1", [passed_on, lax.empty((1,) + passed_on.shape[1:], passed_on.dtype)],
                         _plan_scatter_relay_second, 2)
    behind_second = [n for n in order if n not in sums and n != "w_in"]
    for name in behind_second:
        adam(name, deps=(tok,))
    sums["w_in"] = tuple(_land(fl_in, [deltas[n] for n in behind_second]))
    adam("w_in")

    return (loss, grad_x[None], *[grads[n] for n in order], *[deltas[n] for n in order],
            *[new_m[n] for n in order], *[new_v[n] for n in order])
```

```python
import math

import jax
import jax.numpy as jnp
from jax import lax
from jax.experimental import pallas as pl
from jax.experimental.pallas import tpu as pltpu

F32 = jnp.float32
BF16 = jnp.bfloat16
MESH_AXES = ("x", "y", "c")
N_DEV = 8
HEAD_DIM = 128
POOL_WINDOWS = (2, 4, 8, 16)
N_MOD = 6
NORM_EPS = 1e-6
LANES = 128
SUBLANES = 8
VMEM_LIMIT_BYTES = 56 * 1024 * 1024
Q_TILE = 256
K_TILE = 256
POOL_TILE = 512
HEADS_PER_STEP = 8
HEADS_PER_STEP_BWD = 4

ADAM_LR = 0.001
ADAM_B1 = 0.9
ADAM_B2 = 0.999
ADAM_EPS = 1e-08
ADAM_WD = 0.01
ADAM_STEP = 10

_NN = (((1,), (0,)), ((), ()))
_NT = (((1,), (1,)), ((), ()))
_TN = (((0,), (0,)), ((), ()))
_DIMS = {"nn": _NN, "nt": _NT, "tn": _TN}


def _dot(a, b, mode="nn"):
    return lax.dot_general(a, b, _DIMS[mode], preferred_element_type=F32)


def _params(*sem):
    return pltpu.CompilerParams(dimension_semantics=sem, vmem_limit_bytes=VMEM_LIMIT_BYTES)


def _tile(dim, pref, align=SUBLANES):
    for t in range(min(dim, pref), 0, -1):
        if dim % t == 0 and t % align == 0:
            return t
    return dim


def _group_index(axes):
    idx = 0
    for a in axes:
        idx = idx * 2 + lax.axis_index(a)
    return idx


def _peer_device(axes, k):
    coords = {a: lax.axis_index(a) for a in MESH_AXES}
    for pos, a in enumerate(axes):
        if (k >> (len(axes) - 1 - pos)) & 1:
            coords[a] = 1 - coords[a]
    return tuple(coords[a] for a in MESH_AXES)


_AXIS_BIT = {"x": 4, "y": 2, "c": 1}
_ANY = pl.BlockSpec(memory_space=pl.ANY)


def _device_xor(mask):
    return tuple(1 - lax.axis_index(a) if mask & _AXIS_BIT[a] else lax.axis_index(a) for a in MESH_AXES)


def _remote(src, dst, send_sem, recv_sem, mask):
    return pltpu.make_async_remote_copy(src_ref=src, dst_ref=dst, send_sem=send_sem, recv_sem=recv_sem,
                                        device_id=_device_xor(mask), device_id_type=pl.DeviceIdType.MESH)


_CHIP_MASKS = (0, _AXIS_BIT["y"], _AXIS_BIT["x"], _AXIS_BIT["x"] | _AXIS_BIT["y"])


def _add_received(name, own, own_slots, received, out_dtype):
    nj, r, c = received.shape
    tr = _tile(r, max(2 * SUBLANES, (1 << 20) // c), 2 * SUBLANES)

    def body(slots_ref, own_ref, rec_ref, o_ref):
        del slots_ref
        o_ref[...] = (own_ref[...].astype(F32) + rec_ref[...].astype(F32)).astype(o_ref.dtype)

    grid_spec = pltpu.PrefetchScalarGridSpec(
        num_scalar_prefetch=1, grid=(nj, r // tr),
        in_specs=[pl.BlockSpec((None, tr, c), lambda j, i, slots: (slots[j], i, 0)),
                  pl.BlockSpec((None, tr, c), lambda j, i, slots: (j, i, 0))],
        out_specs=pl.BlockSpec((None, tr, c), lambda j, i, slots: (j, i, 0)))
    return pl.pallas_call(
        body, name=name, grid_spec=grid_spec, out_shape=jax.ShapeDtypeStruct((nj, r, c), out_dtype),
        compiler_params=_params("parallel", "parallel"),
    )(own_slots, own, received)


_HBM = pl.BlockSpec(memory_space=pltpu.HBM)
_SEM = pl.BlockSpec(memory_space=pltpu.SEMAPHORE)
_DATAFLOW = pltpu.SideEffectType.DATAFLOW_SIDE_EFFECTING


def _launch_groups(name, groups, deps=()):
    bufs = [b for g_bufs, _ in groups for b in g_bufs]
    specs = [(len(g_bufs), spec) for g_bufs, g_plans in groups for spec in g_plans]
    nb, ns = len(bufs), len(specs)

    def body(*refs):
        sems = refs[nb + len(deps):nb + len(deps) + 2 * ns]
        me = _group_index(MESH_AXES)
        first, which = 0, 0
        for g_bufs, g_plans in groups:
            ins = refs[first:first + len(g_bufs)]
            for _, plan, n_copies in g_plans:
                copies = plan(ins, me)
                assert len(copies) == n_copies
                for n, (src, dst, mask) in enumerate(copies):
                    _remote(src, dst, sems[2 * which].at[n], sems[2 * which + 1].at[n], mask).start()
                which += 1
            first += len(g_bufs)
        refs[-1][...] = jnp.zeros_like(refs[-1])

    sem_shapes = [pltpu.SemaphoreType.DMA((n,)) for _, (_, _, n) in specs for _ in range(2)]
    outs = pl.pallas_call(
        body, name=name,
        out_shape=(*sem_shapes, *[pltpu.HBM(b.shape, b.dtype) for b in bufs],
                   jax.ShapeDtypeStruct((SUBLANES, LANES), F32)),
        in_specs=[_HBM] * nb + [_ANY] * len(deps),
        out_specs=(*[_SEM] * (2 * ns), *[_HBM] * nb, pl.BlockSpec(memory_space=pltpu.VMEM)),
        input_output_aliases={i: 2 * ns + i for i in range(nb)},
        compiler_params=pltpu.CompilerParams(has_side_effects=_DATAFLOW),
    )(*[pltpu.with_memory_space_constraint(b, pltpu.HBM) for b in bufs], *deps)
    flights, first, which = [], 0, 0
    for g_bufs, g_plans in groups:
        through = list(outs[2 * ns + first:2 * ns + first + len(g_bufs)])
        for land_name, plan, n_copies in g_plans:
            flights.append((land_name, plan, n_copies, outs[2 * which], outs[2 * which + 1], through))
            which += 1
        first += len(g_bufs)
    return flights, outs[-1]


def _launch(name, bufs, plan, n_copies, deps=()):
    (flight,), token = _launch_groups(name, [(bufs, [(name, plan, n_copies)])], deps)
    return flight, token


def _land(flight, after, bufs=None):
    name, plan, n_copies, send_sems, recv_sems, launched = flight
    bufs = launched if bufs is None else bufs
    nb = len(bufs)
    after = list(after) if isinstance(after, (list, tuple)) else [after]

    def body(*refs):
        ins = refs[:nb]
        s_sems, r_sems = refs[nb], refs[nb + 1]
        for n, (src, dst, mask) in enumerate(plan(ins, _group_index(MESH_AXES))):
            cp = _remote(src, dst, s_sems.at[n], r_sems.at[n], mask)
            cp.wait_send()
            cp.wait_recv()

    outs = pl.pallas_call(
        body, name=name + "_land",
        out_shape=tuple(pltpu.HBM(b.shape, b.dtype) for b in bufs),
        in_specs=[_HBM] * nb + [_SEM, _SEM] + [_ANY] * len(after), out_specs=tuple([_HBM] * nb),
        input_output_aliases={i: i for i in range(nb)},
        compiler_params=pltpu.CompilerParams(has_side_effects=_DATAFLOW),
    )(*bufs, send_sems, recv_sems, *after)
    return list(outs)


def _plan_gather_ici(phase):
    bx, by = _AXIS_BIT["x"], _AXIS_BIT["y"]

    def plan(refs, me):
        copies = []
        for ref in refs:
            half = ref.shape[1] // 2

            def piece(slot, color, mask, ref=ref, half=half):
                p = ref.at[slot, pl.ds(color * half, half)]
                return (p, p, mask)

            if phase == 0:
                copies += [piece(me, 0, bx), piece(me, 1, by)]
            else:
                copies += [piece(me, 0, by), piece(me ^ bx, 0, by), piece(me, 1, bx), piece(me ^ by, 1, bx)]
        return copies

    return plan


def _plan_d2d(masks):
    def plan(refs, me):
        return [(ref.at[me ^ m], ref.at[me ^ m], _AXIS_BIT["c"]) for ref in refs for m in masks]

    return plan


def _plan_gather_d2d(refs, me):
    return _plan_d2d(_CHIP_MASKS)(refs, me)


def _plan_neighbours(refs, me):
    return [(ref.at[me], ref.at[me], _AXIS_BIT[a]) for ref in refs for a in ("x", "y")]


def _plan_diagonal(refs, me):
    bx, by = _AXIS_BIT["x"], _AXIS_BIT["y"]
    copies = []
    for ref in refs:
        half = ref.shape[1] // 2
        lo = ref.at[me ^ bx, pl.ds(0, half)]
        hi = ref.at[me ^ by, pl.ds(half, half)]
        copies += [(lo, lo, by), (hi, hi, bx)]
    return copies


def _plan_scatter_d2d(refs, me):
    na = len(refs) // 2
    copies = []
    for a in range(na):
        for j, m in enumerate(_CHIP_MASKS):
            copies.append((refs[a].at[me ^ _AXIS_BIT["c"] ^ m], refs[na + a].at[j], _AXIS_BIT["c"]))
    return copies


def _plan_scatter_ici(refs, me):
    del me
    na = len(refs) // 2
    copies = []
    for a in range(na):
        for n, m in enumerate(_CHIP_MASKS[1:]):
            copies.append((refs[a].at[n + 1], refs[na + a].at[n], m))
    return copies


def _with_deps(body, n_in, deps):
    if not deps:
        return body

    def wrapped(*refs):
        return body(*refs[:n_in], *refs[n_in + len(deps):])

    return wrapped


def _reduce_scatter_start(tag, grads):
    lands = [lax.empty((len(_CHIP_MASKS),) + g.shape[1:], g.dtype) for g in grads]
    return _launch("rs%s_d2d" % tag, list(grads) + lands, _plan_scatter_d2d, len(_CHIP_MASKS) * len(grads))


def _reduce_scatter_add(tag, flight, after, me, n_land=len(_CHIP_MASKS) - 1):
    bufs = _land(flight, after)
    na = len(bufs) // 2
    own_slots = jnp.stack([me ^ m for m in _CHIP_MASKS]).astype(jnp.int32)
    sums = [_add_received("rs%s_add_d2d_%d" % (tag, a), bufs[a], own_slots, bufs[na + a], BF16) for a in range(na)]
    lands = [lax.empty((n_land,) + h.shape[1:], h.dtype) for h in sums]
    return sums + lands


def _plan_scatter_relay_first(refs, me):
    del me
    bx, by = _AXIS_BIT["x"], _AXIS_BIT["y"]
    na = len(refs) // 2
    copies = []
    for a in range(na):
        h, land = refs[a], refs[na + a]
        half = h.shape[1] // 2
        lo, hi = pl.ds(0, half), pl.ds(half, half)
        copies += [(h.at[1, lo], land.at[0, lo], by), (h.at[3, lo], land.at[1, lo], by),
                   (h.at[2, hi], land.at[0, hi], bx), (h.at[3, hi], land.at[1, hi], bx)]
    return copies


def _plan_scatter_relay_second(refs, me):
    del me
    na = len(refs) // 2
    copies = []
    for a in range(na):
        f, land = refs[a], refs[na + a]
        half = f.shape[1] // 2
        lo, hi = pl.ds(0, half), pl.ds(half, half)
        copies += [(f.at[1, lo], land.at[0, lo], _AXIS_BIT["x"]), (f.at[1, hi], land.at[0, hi], _AXIS_BIT["y"])]
    return copies


def _add_relayed(name, sums, received):
    _, r, c = sums.shape
    tr = _tile(r // 2, max(2 * SUBLANES, (1 << 20) // c), 2 * SUBLANES)
    n_half = (r // 2) // tr

    def body(own_ref, rec_ref, o_ref):
        o_ref[...] = (own_ref[...].astype(F32) + rec_ref[...].astype(F32)).astype(o_ref.dtype)

    def own_slot(j, i):
        return jnp.where(j == 0, 0, jnp.where(i < n_half, 2, 1))

    return pl.pallas_call(
        body, name=name, grid=(2, 2 * n_half), out_shape=jax.ShapeDtypeStruct((2, r, c), sums.dtype),
        in_specs=[pl.BlockSpec((None, tr, c), lambda j, i: (own_slot(j, i), i, 0)),
                  pl.BlockSpec((None, tr, c), lambda j, i: (j, i, 0))],
        out_specs=pl.BlockSpec((None, tr, c), lambda j, i: (j, i, 0)),
        compiler_params=_params("parallel", "parallel"),
    )(sums, received)


def _reduce_scatter_plan(tag, bufs):
    return ("rs%s_ici" % tag, _plan_scatter_ici, (len(_CHIP_MASKS) - 1) * (len(bufs) // 2))


def _reduce_scatter_middle(tag, flight, after, me):
    bufs = _reduce_scatter_add(tag, flight, after, me)
    name, plan, n_copies = _reduce_scatter_plan(tag, bufs)
    return _launch(name, bufs, plan, n_copies)


def _plan_everyone(refs, me):
    return [(ref.at[me], ref.at[me], m) for ref in refs for m in range(1, N_DEV)]


def _reduce_scatter_finish(flight, after):
    bufs = _land(flight, after)
    na = len(bufs) // 2
    return [(bufs[a], bufs[na + a]) for a in range(na)]


def _all_gather_2d(name, x, deps=()):
    r, c = x.shape

    def body(x_ref, out_ref, send_sems, recv_sems):
        me = _group_index(MESH_AXES)
        out_ref[me] = x_ref[...]
        copies = []
        for k in range(1, N_DEV):
            cp = pltpu.make_async_remote_copy(
                src_ref=x_ref, dst_ref=out_ref.at[me],
                send_sem=send_sems.at[k - 1], recv_sem=recv_sems.at[k - 1],
                device_id=_peer_device(MESH_AXES, k), device_id_type=pl.DeviceIdType.MESH)
            cp.start()
            copies.append(cp)
        for cp in copies:
            cp.wait()

    vmem = pl.BlockSpec(memory_space=pltpu.VMEM)
    return pl.pallas_call(
        _with_deps(body, 1, deps), name=name, out_shape=jax.ShapeDtypeStruct((N_DEV, r, c), x.dtype),
        in_specs=[vmem] + [_ANY] * len(deps), out_specs=vmem,
        scratch_shapes=[pltpu.SemaphoreType.DMA((N_DEV - 1,)), pltpu.SemaphoreType.DMA((N_DEV - 1,))],
    )(x, *deps)


def _sum_slots(name, buf, out_dtype):
    pre, n, r, c = buf.shape
    tr = _tile(r, max(SUBLANES * 2, (1 << 20) // c))

    def body(b_ref, o_ref):
        acc = b_ref[0].astype(F32)
        for q in range(1, n):
            acc = acc + b_ref[q].astype(F32)
        o_ref[...] = acc.astype(o_ref.dtype)

    return pl.pallas_call(
        body, name=name, grid=(pre, r // tr),
        out_shape=jax.ShapeDtypeStruct((pre, r, c), out_dtype),
        in_specs=[pl.BlockSpec((None, n, tr, c), lambda i, j: (i, 0, j, 0))],
        out_specs=pl.BlockSpec((None, tr, c), lambda i, j: (i, j, 0)),
        compiler_params=_params("parallel", "parallel"),
    )(buf)


def _matmul(name, mode, grid, a, a_spec, b, b_spec, out_shapes, out_specs, acc_shape,
            epilogue=None, extras=(), extra_specs=(), aliases=None, deps=()):
    nk = grid[2]
    n_extra = len(extras)
    n_out = len(out_shapes)

    def finish(acc, extra_refs, out_refs):
        if epilogue is None:
            out_refs[0][...] = acc.astype(out_refs[0].dtype)
        else:
            epilogue(acc, extra_refs, out_refs)

    def product(a_ref, b_ref):
        if len(b_ref.shape) == 2:
            return _dot(a_ref[...], b_ref[...], mode)
        width = a_ref.shape[1] // b_ref.shape[0]
        total = None
        for i in range(b_ref.shape[0]):
            part = _dot(a_ref[:, i * width:(i + 1) * width], b_ref[i], mode)
            total = part if total is None else total + part
        return total

    def body(*refs):
        a_ref, b_ref = refs[0], refs[1]
        extra_refs = refs[2:2 + n_extra]
        out_refs = refs[2 + n_extra:2 + n_extra + n_out]
        if nk == 1:
            finish(product(a_ref, b_ref), extra_refs, out_refs)
            return
        acc_ref = refs[-1]
        k = pl.program_id(2)

        @pl.when(k == 0)
        def _():
            acc_ref[...] = product(a_ref, b_ref)

        @pl.when((k > 0) & (k < nk - 1))
        def _():
            acc_ref[...] += product(a_ref, b_ref)

        @pl.when(k == nk - 1)
        def _():
            finish(acc_ref[...] + product(a_ref, b_ref), extra_refs, out_refs)

    scratch = [] if nk == 1 else [pltpu.VMEM(acc_shape, F32)]
    return pl.pallas_call(
        _with_deps(body, 2 + n_extra, deps), name=name, grid=grid, out_shape=tuple(out_shapes),
        in_specs=[a_spec, b_spec] + list(extra_specs) + [_ANY] * len(deps), out_specs=tuple(out_specs),
        scratch_shapes=scratch, input_output_aliases=aliases or {},
        compiler_params=_params("parallel", "parallel", "arbitrary"),
    )(a, b, *extras, *deps)


def _sds(shape, dtype):
    return jax.ShapeDtypeStruct(tuple(shape), dtype)


def _project_slots(name, h, w_full, slots, out_cols, proj_in=None, deps=()):
    s, d = h.shape
    _, _, wide = w_full.shape
    tm = _tile(s, 1024)
    n_in = 4 if proj_in is not None else 3

    def body(*refs):
        refs[-1][...] = _dot(refs[1][...], refs[2][...])

    grid_spec = pltpu.PrefetchScalarGridSpec(
        num_scalar_prefetch=1, grid=(s // tm, slots.shape[0]),
        in_specs=[pl.BlockSpec((tm, d), lambda i, j, sl: (i, 0)),
                  pl.BlockSpec((None, d, wide), lambda i, j, sl: (sl[j], 0, 0))]
        + [_ANY] * (n_in - 3 + len(deps)),
        out_specs=pl.BlockSpec((tm, wide), lambda i, j, sl: (i, sl[j])))
    extra = ([proj_in] if proj_in is not None else []) + list(deps)
    return pl.pallas_call(
        body, name=name, grid_spec=grid_spec, out_shape=_sds((s, out_cols), F32),
        input_output_aliases={3: 0} if proj_in is not None else {},
        compiler_params=_params("parallel", "arbitrary"),
    )(slots, h, w_full, *extra)


def _ada_forward(c_all, w_ada, b_shard):
    nb, d = c_all.shape
    w = w_ada.shape[1]
    tn = _tile(w, 512)

    def body(c_ref, w_ref, b_ref, o_ref):
        cv = c_ref[...]
        sc = cv * jax.nn.sigmoid(cv)
        o_ref[...] = jnp.dot(sc, w_ref[...], precision=lax.Precision.HIGHEST,
                             preferred_element_type=F32) + b_ref[...]

    return pl.pallas_call(
        body, name="ada_fwd", grid=(w // tn,), out_shape=_sds((nb, w), F32),
        in_specs=[pl.BlockSpec((nb, d), lambda j: (0, 0)), pl.BlockSpec((d, tn), lambda j: (0, j)),
                  pl.BlockSpec((1, tn), lambda j: (0, j))],
        out_specs=pl.BlockSpec((nb, tn), lambda j: (0, j)),
        compiler_params=_params("parallel"),
    )(c_all, w_ada, b_shard)


def _ada_weight_grad(c_all, dmod_cols, deps=()):
    nb, d = c_all.shape
    w = dmod_cols.shape[1]
    tn = _tile(w, 512)

    def body(c_ref, g_ref, o_ref):
        cv = c_ref[...]
        sc = cv * jax.nn.sigmoid(cv)
        o_ref[...] = lax.dot_general(sc, g_ref[...], _TN, precision=lax.Precision.HIGHEST,
                                     preferred_element_type=F32)

    return pl.pallas_call(
        _with_deps(body, 2, deps), name="ada_wgrad", grid=(w // tn,), out_shape=_sds((d, w), F32),
        in_specs=[pl.BlockSpec((nb, d), lambda j: (0, 0)), pl.BlockSpec((nb, tn), lambda j: (0, j))]
        + [_ANY] * len(deps),
        out_specs=pl.BlockSpec((d, tn), lambda j: (0, j)),
        compiler_params=_params("parallel"),
    )(c_all, dmod_cols, *deps)


def _norm_forward(name, x, norm_w, scale, shift, deps=()):
    s, d = x.shape
    tm = _tile(s, 256)

    def body(x_ref, w_ref, sc_ref, sh_ref, h_ref):
        xv = x_ref[...]
        r = lax.rsqrt(jnp.mean(xv * xv, axis=-1, keepdims=True) + NORM_EPS)
        h = (xv * r * w_ref[...]) * (1.0 + sc_ref[...]) + sh_ref[...]
        h_ref[...] = h.astype(BF16)

    vec = pl.BlockSpec((1, d), lambda i: (0, 0))
    row = pl.BlockSpec((tm, d), lambda i: (i, 0))
    return pl.pallas_call(
        _with_deps(body, 4, deps), name=name, grid=(s // tm,), out_shape=_sds((s, d), BF16),
        in_specs=[row, vec, vec, vec] + [_ANY] * len(deps), out_specs=row, compiler_params=_params("parallel"),
    )(x, norm_w, scale, shift, *deps)


def _norm_backward(name, dh, x, norm_w, scale, dres, gated=None, deps=()):
    s, d = x.shape
    tm = _tile(s, 256)
    n_in = 7 if gated else 5

    def body(*refs):
        dh_ref, x_ref, w_ref, sc_ref, dres_ref = refs[:5]
        dx_ref, dshift_ref, dscale_ref, dw_ref = refs[n_in:n_in + 4]
        sums = (dshift_ref, dscale_ref, dw_ref) + ((refs[n_in + 5],) if gated else ())

        @pl.when(pl.program_id(0) == 0)
        def _():
            for ref in sums:
                ref[...] = jnp.zeros_like(ref)

        xv = x_ref[...]
        g = dh_ref[...]
        r = lax.rsqrt(jnp.mean(xv * xv, axis=-1, keepdims=True) + NORM_EPS)
        n = xv * r
        gain = 1.0 + sc_ref[...]
        gn = g * n
        dshift_ref[...] += jnp.sum(g, axis=0, keepdims=True)
        dscale_ref[...] += jnp.sum(gn, axis=0, keepdims=True) * w_ref[...]
        dw_ref[...] += jnp.sum(gn, axis=0, keepdims=True) * gain
        dn = g * (w_ref[...] * gain)
        dx = dres_ref[...] + r * (dn - n * jnp.mean(dn * n, axis=-1, keepdims=True))
        dx_ref[...] = dx
        if gated:
            gate_ref, other_ref = refs[5:7]
            refs[n_in + 4][...] = (dx * gate_ref[...]).astype(BF16)
            refs[n_in + 5][...] += jnp.sum(dx * other_ref[...].astype(F32), axis=0, keepdims=True)

    vec = pl.BlockSpec((1, d), lambda i: (0, 0))
    row = pl.BlockSpec((tm, d), lambda i: (i, 0))
    vec_out = _sds((1, d), F32)
    return pl.pallas_call(
        _with_deps(body, n_in, deps), name=name, grid=(s // tm,),
        out_shape=(_sds((s, d), F32), vec_out, vec_out, vec_out) + ((_sds((s, d), BF16), vec_out) if gated else ()),
        in_specs=[row, row, vec, vec, row] + ([vec, row] if gated else []) + [_ANY] * len(deps),
        out_specs=(row, vec, vec, vec) + ((row, vec) if gated else ()),
        compiler_params=_params("arbitrary"),
    )(dh, x, norm_w, scale, dres, *(gated or ()), *deps)


def _split_bf16(v):
    hi = v.astype(BF16)
    lo = (v - hi.astype(F32)).astype(BF16)
    return hi, lo


def _pool_forward(proj, w_pool, pool_scale, deps=()):
    s = proj.shape[0]
    g_n, cg, _ = w_pool.shape
    t = POOL_TILE
    nt = s // t

    def body(cur_ref, prev_ref, wp_ref, sc_ref, pooled_ref, ya_ref):
        g = pl.program_id(0)
        ti = pl.program_id(1)
        win = jnp.left_shift(2, g)
        row = lax.broadcasted_iota(jnp.int32, (t, t), 0)
        col = lax.broadcasted_iota(jnp.int32, (t, t), 1)
        lag = row - col
        band_cur = ((lag >= 0) & (lag < win)).astype(BF16)
        band_prev = ((lag + t < win) & (ti > 0)).astype(BF16)
        u = cur_ref[...]
        u_hi, u_lo = _split_bf16(u)
        p_hi, p_lo = _split_bf16(prev_ref[...])
        wsum = (_dot(band_cur, u_hi) + _dot(band_cur, u_lo)
                + _dot(band_prev, p_hi) + _dot(band_prev, p_lo))
        tok = ti * t + lax.broadcasted_iota(jnp.int32, (t, 1), 0)
        count = jnp.minimum(tok + 1, win).astype(F32)
        pooled = (wsum / count - u).astype(BF16)
        pooled_ref[...] = pooled
        ya_ref[...] = (_dot(pooled, wp_ref[...]) * sc_ref[...]).astype(BF16)

    blk = pl.BlockSpec((t, cg), lambda g, i: (i, g))
    return pl.pallas_call(
        _with_deps(body, 4, deps), name="pool_fwd", grid=(g_n, nt),
        out_shape=(_sds((s, g_n * cg), BF16), _sds((s, g_n * cg), BF16)),
        in_specs=[blk, pl.BlockSpec((t, cg), lambda g, i: (jnp.maximum(i - 1, 0), g)),
                  pl.BlockSpec((None, cg, cg), lambda g, i: (g, 0, 0)),
                  pl.BlockSpec((1, cg), lambda g, i: (0, g))] + [_ANY] * len(deps),
        out_specs=(blk, blk), compiler_params=_params("parallel", "parallel"),
    )(proj, proj, w_pool, pool_scale, *deps)


def _pool_backward(dya, pooled, w_pool, pool_scale, dproj):
    s = dya.shape[0]
    g_n, cg, _ = w_pool.shape
    t = POOL_TILE
    nt = s // t

    def body(dya_ref, dya_next_ref, pooled_ref, wp_ref, sc_ref, dproj_in, du_ref, gw_ref, gs_ref):
        del dproj_in
        g = pl.program_id(0)
        ti = pl.program_id(1)

        @pl.when(ti == 0)
        def _():
            gw_ref[...] = jnp.zeros_like(gw_ref)
            gs_ref[...] = jnp.zeros_like(gs_ref)

        win = jnp.left_shift(2, g)
        wp = wp_ref[...]
        sc = sc_ref[...]
        pooled_v = pooled_ref[...]
        dya_v = dya_ref[...].astype(F32)
        mixed = _dot(pooled_v, wp)
        gs_ref[...] += jnp.sum(dya_v * mixed, axis=0, keepdims=True)
        dmixed = (dya_v * sc).astype(BF16)
        gw_ref[...] += _dot(pooled_v, dmixed, "tn")
        dpooled = _dot(dmixed, wp, "nt")
        dmixed_next = (dya_next_ref[...].astype(F32) * sc).astype(BF16)
        dpooled_next = _dot(dmixed_next, wp, "nt")
        tok = ti * t + lax.broadcasted_iota(jnp.int32, (t, 1), 0)
        e_cur = dpooled / jnp.minimum(tok + 1, win).astype(F32)
        e_next = dpooled_next / jnp.minimum(tok + t + 1, win).astype(F32)
        row = lax.broadcasted_iota(jnp.int32, (t, t), 0)
        col = lax.broadcasted_iota(jnp.int32, (t, t), 1)
        lead = col - row
        band_cur = ((lead >= 0) & (lead < win)).astype(BF16)
        band_next = ((lead + t < win) & (ti < nt - 1)).astype(BF16)
        c_hi, c_lo = _split_bf16(e_cur)
        n_hi, n_lo = _split_bf16(e_next)
        du = (_dot(band_cur, c_hi) + _dot(band_cur, c_lo)
              + _dot(band_next, n_hi) + _dot(band_next, n_lo)) - dpooled
        du_ref[...] = du.astype(BF16)

    blk = pl.BlockSpec((t, cg), lambda g, i: (i, g))
    du, gw, gs = pl.pallas_call(
        body, name="pool_bwd", grid=(g_n, nt),
        out_shape=(_sds(dproj.shape, BF16), _sds((g_n, cg, cg), F32), _sds((1, g_n * cg), F32)),
        in_specs=[blk, pl.BlockSpec((t, cg), lambda g, i: (jnp.minimum(i + 1, nt - 1), g)), blk,
                  pl.BlockSpec((None, cg, cg), lambda g, i: (g, 0, 0)),
                  pl.BlockSpec((1, cg), lambda g, i: (0, g)),
                  pl.BlockSpec(memory_space=pl.ANY)],
        out_specs=(blk, pl.BlockSpec((None, cg, cg), lambda g, i: (g, 0, 0)),
                   pl.BlockSpec((1, cg), lambda g, i: (0, g))),
        input_output_aliases={5: 0}, compiler_params=_params("parallel", "arbitrary"),
    )(dya, dya, pooled, w_pool, pool_scale, dproj)
    return du, gw, gs


def _qkv_prepare(proj, q_norm_w, k_norm_w, width, deps=()):
    s = proj.shape[0]
    tm = _tile(s, 256)
    heads = width // HEAD_DIM

    def body(q_ref, k_ref, v_ref, qw_ref, kw_ref, qn_ref, kn_ref, vb_ref):
        for h in range(heads):
            cols = slice(h * HEAD_DIM, (h + 1) * HEAD_DIM)
            for src, w_ref, dst in ((q_ref, qw_ref, qn_ref), (k_ref, kw_ref, kn_ref)):
                v = src[:, cols]
                r = lax.rsqrt(jnp.mean(v * v, axis=-1, keepdims=True) + NORM_EPS)
                dst[:, cols] = (v * r * w_ref[...]).astype(BF16)
        vb_ref[...] = v_ref[...].astype(BF16)

    vec = pl.BlockSpec((1, HEAD_DIM), lambda i: (0, 0))
    out_spec = pl.BlockSpec((tm, width), lambda i: (i, 0))
    return pl.pallas_call(
        _with_deps(body, 5, deps), name="qkv_prep", grid=(s // tm,),
        out_shape=(_sds((s, width), BF16),) * 3,
        in_specs=[pl.BlockSpec((tm, width), lambda i: (i, 1)), pl.BlockSpec((tm, width), lambda i: (i, 2)),
                  pl.BlockSpec((tm, width), lambda i: (i, 3)), vec, vec] + [_ANY] * len(deps),
        out_specs=(out_spec,) * 3, compiler_params=_params("parallel"),
    )(proj, proj, proj, q_norm_w, k_norm_w, *deps)


def _qk_norm_backward(name, dn, proj, col_block, norm_w, dproj, width, deps=()):
    s = proj.shape[0]
    tm = _tile(s, 256)
    heads = width // HEAD_DIM

    def body(dn_ref, q_ref, w_ref, dproj_in, dq_ref, gw_ref):
        del dproj_in

        @pl.when(pl.program_id(0) == 0)
        def _():
            gw_ref[...] = jnp.zeros_like(gw_ref)

        wv = w_ref[...]
        gw = jnp.zeros((1, HEAD_DIM), F32)
        for h in range(heads):
            cols = slice(h * HEAD_DIM, (h + 1) * HEAD_DIM)
            v = q_ref[:, cols]
            g = dn_ref[:, cols]
            r = lax.rsqrt(jnp.mean(v * v, axis=-1, keepdims=True) + NORM_EPS)
            n = v * r
            gw = gw + jnp.sum(g * n, axis=0, keepdims=True)
            gn = g * wv
            dq_ref[:, cols] = (r * (gn - n * jnp.mean(gn * n, axis=-1, keepdims=True))).astype(BF16)
        gw_ref[...] += gw

    blk = pl.BlockSpec((tm, width), lambda i: (i, col_block))
    return pl.pallas_call(
        _with_deps(body, 4, deps), name=name, grid=(s // tm,),
        out_shape=(_sds(dproj.shape, BF16), _sds((1, HEAD_DIM), F32)),
        in_specs=[pl.BlockSpec((tm, width), lambda i: (i, 0)), blk,
                  pl.BlockSpec((1, HEAD_DIM), lambda i: (0, 0)), pl.BlockSpec(memory_space=pl.ANY)]
        + [_ANY] * len(deps),
        out_specs=(blk, pl.BlockSpec((1, HEAD_DIM), lambda i: (0, 0))),
        input_output_aliases={3: 0}, compiler_params=_params("arbitrary"),
    )(dn, proj, norm_w, dproj, *deps)


def _strict_upper(n):
    row = lax.broadcasted_iota(jnp.int32, (n, n), 0)
    col = lax.broadcasted_iota(jnp.int32, (n, n), 1)
    return (row > col).astype(BF16)


def _strict_lower(n):
    row = lax.broadcasted_iota(jnp.int32, (n, n), 0)
    col = lax.broadcasted_iota(jnp.int32, (n, n), 1)
    return (row < col).astype(BF16)


def _cumulate(v, tri):
    return _dot(v.astype(BF16), tri)


def _log_sigmoid(z):
    return jnp.minimum(z, 0.0) - jnp.log(1.0 + jnp.exp(-jnp.abs(z)))


def _attention_forward(qn, kn, vb, deps=()):
    s, width = qn.shape
    heads = width // HEAD_DIM
    tq, tk = Q_TILE, K_TILE
    hp = min(HEADS_PER_STEP, heads)
    assert tq == tk and s % tq == 0 and heads % hp == 0
    scale = 1.0 / math.sqrt(HEAD_DIM)

    def body(q_ref, k_ref, v_ref, o_ref, a_scr):
        qi = pl.program_id(1)
        upper = _strict_upper(tk)
        causal = lax.broadcasted_iota(jnp.int32, (tq, tk), 1) < lax.broadcasted_iota(jnp.int32, (tq, tk), 0)
        head_cols = [slice(u * HEAD_DIM, (u + 1) * HEAD_DIM) for u in range(hp)]

        def weights(kb, carry, masked):
            rows = pl.ds(pl.multiple_of(kb * tk, tk), tk)
            out = []
            for u, cols in enumerate(head_cols):
                later = carry[u]
                z = _dot(q_ref[:, cols], k_ref[rows, cols], "nt") * scale
                log_beta = _log_sigmoid(z)
                l = log_beta - z
                if masked:
                    l = jnp.where(causal, l, 0.0)
                a = jnp.exp(log_beta + _cumulate(l, upper) + later)
                if masked:
                    a = jnp.where(causal, a, 0.0)
                a_scr[u, :, rows] = a.astype(BF16)
                out.append(later + jnp.sum(l, axis=1, keepdims=True))
            return tuple(out)

        later = weights(qi, tuple(jnp.zeros((tq, 1), F32) for _ in range(hp)), True)
        lax.fori_loop(0, qi, lambda i, c: weights(qi - 1 - i, c, False), later)

        def mix(kb, accs):
            rows = pl.ds(pl.multiple_of(kb * tk, tk), tk)
            return tuple(acc + _dot(a_scr[u, :, rows], v_ref[rows, cols])
                         for u, (acc, cols) in enumerate(zip(accs, head_cols)))

        accs = lax.fori_loop(0, qi + 1, mix, tuple(jnp.zeros((tq, HEAD_DIM), F32) for _ in range(hp)))
        for acc, cols in zip(accs, head_cols):
            o_ref[:, cols] = acc.astype(BF16)

    full = pl.BlockSpec((s, hp * HEAD_DIM), lambda h, i: (0, h))
    blk = pl.BlockSpec((tq, hp * HEAD_DIM), lambda h, i: (i, h))
    return pl.pallas_call(
        _with_deps(body, 3, deps), name="attn_fwd", grid=(heads // hp, s // tq), out_shape=_sds((s, width), BF16),
        in_specs=[blk, full, full] + [_ANY] * len(deps), out_specs=blk,
        scratch_shapes=[pltpu.VMEM((hp, tq, s), BF16)],
        compiler_params=_params("parallel", "parallel"),
    )(qn, kn, vb, *deps)


def _attention_backward(qn, kn, vb, dout, dproj, v_col_block, deps=()):
    s, width = qn.shape
    heads = width // HEAD_DIM
    tq, tk = Q_TILE, K_TILE
    hp = min(HEADS_PER_STEP_BWD, heads)
    nq = s // tq
    scale = 1.0 / math.sqrt(HEAD_DIM)
    v_block0 = v_col_block * (heads // hp)

    def body(q_ref, k_ref, v_ref, do_ref, dproj_in, dq_ref, dk_ref, dv_ref,
             a_scr, lb_scr, dz_scr, dkt_scr, dvt_scr):
        del dproj_in
        qi = pl.program_id(1)

        @pl.when(qi == 0)
        def _():
            dkt_scr[...] = jnp.zeros_like(dkt_scr)
            dvt_scr[...] = jnp.zeros_like(dvt_scr)

        upper = _strict_upper(tk)
        lower = _strict_lower(tk)
        causal = lax.broadcasted_iota(jnp.int32, (tq, tk), 1) < lax.broadcasted_iota(jnp.int32, (tq, tk), 0)
        head_cols = [slice(u * HEAD_DIM, (u + 1) * HEAD_DIM) for u in range(hp)]

        def weights(kb, carry, masked):
            rows = pl.ds(pl.multiple_of(kb * tk, tk), tk)
            out = []
            for u, cols in enumerate(head_cols):
                later = carry[u]
                z = _dot(q_ref[:, cols], k_ref[rows, cols], "nt") * scale
                log_beta = _log_sigmoid(z)
                l = log_beta - z
                if masked:
                    l = jnp.where(causal, l, 0.0)
                a = jnp.exp(log_beta + _cumulate(l, upper) + later)
                if masked:
                    a = jnp.where(causal, a, 0.0)
                a_scr[u, :, rows] = a
                lb_scr[u, :, rows] = log_beta
                out.append(later + jnp.sum(l, axis=1, keepdims=True))
            return tuple(out)

        zeros = tuple(jnp.zeros((tq, 1), F32) for _ in range(hp))
        later = weights(qi, zeros, True)
        lax.fori_loop(0, qi, lambda i, c: weights(qi - 1 - i, c, False), later)

        q_t = [jnp.transpose(q_ref[:, cols].astype(F32)).astype(BF16) for cols in head_cols]
        do_t = [jnp.transpose(do_ref[:, cols].astype(F32)).astype(BF16) for cols in head_cols]

        def scores(kb, carry, masked):
            rows = pl.ds(pl.multiple_of(kb * tk, tk), tk)
            out = []
            for u, cols in enumerate(head_cols):
                before = carry[u]
                beta = jnp.exp(lb_scr[u, :, rows])
                g = a_scr[u, :, rows] * _dot(do_ref[:, cols], v_ref[rows, cols], "nt")
                p = _cumulate(g, lower) + before
                dz = g - (g + p) * beta
                if masked:
                    dz = jnp.where(causal, dz, 0.0)
                dz_scr[u, :, rows] = (dz * scale).astype(BF16)
                out.append(before + jnp.sum(g, axis=1, keepdims=True))
            return tuple(out)

        before = lax.fori_loop(0, qi, lambda i, c: scores(i, c, False), zeros)
        scores(qi, before, True)

        def products(kb, dqs):
            rows = pl.ds(pl.multiple_of(kb * tk, tk), tk)
            out = []
            for u, cols in enumerate(head_cols):
                dz = dz_scr[u, :, rows]
                dkt_scr[cols, rows] += _dot(q_t[u], dz)
                dvt_scr[cols, rows] += _dot(do_t[u], a_scr[u, :, rows].astype(BF16))
                out.append(dqs[u] + _dot(dz, k_ref[rows, cols]))
            return tuple(out)

        dqs = lax.fori_loop(0, qi + 1, products, tuple(jnp.zeros((tq, HEAD_DIM), F32) for _ in range(hp)))
        for u, cols in enumerate(head_cols):
            dq_ref[:, cols] = dqs[u]

        @pl.when(qi == nq - 1)
        def _():
            dk_ref[...] = jnp.transpose(dkt_scr[...])
            dv_ref[...] = jnp.transpose(dvt_scr[...]).astype(BF16)

    wide = hp * HEAD_DIM
    full = pl.BlockSpec((s, wide), lambda h, i: (0, h))
    blk = pl.BlockSpec((tq, wide), lambda h, i: (i, h))
    return pl.pallas_call(
        _with_deps(body, 5, deps), name="attn_bwd", grid=(heads // hp, nq),
        out_shape=(_sds((s, width), F32), _sds((s, width), F32), _sds(dproj.shape, BF16)),
        in_specs=[blk, full, full, blk, pl.BlockSpec(memory_space=pl.ANY)] + [_ANY] * len(deps),
        out_specs=(blk, full, pl.BlockSpec((s, wide), lambda h, i: (0, v_block0 + h))),
        scratch_shapes=[pltpu.VMEM((hp, tq, s), F32), pltpu.VMEM((hp, tq, s), F32), pltpu.VMEM((hp, tq, s), BF16),
                        pltpu.VMEM((wide, s), F32), pltpu.VMEM((wide, s), F32)],
        input_output_aliases={4: 2}, compiler_params=_params("parallel", "arbitrary"),
    )(qn, kn, vb, dout, dproj, *deps)


def _place_columns(name, src, dst, col_block):
    s, w = src.shape
    tm = _tile(s, 512)

    def body(src_ref, dst_in, out_ref):
        del dst_in
        out_ref[...] = src_ref[...]

    return pl.pallas_call(
        body, name=name, grid=(s // tm,), out_shape=_sds(dst.shape, dst.dtype),
        in_specs=[pl.BlockSpec((tm, w), lambda i: (i, 0)), pl.BlockSpec(memory_space=pl.ANY)],
        out_specs=pl.BlockSpec((tm, w), lambda i: (i, col_block)),
        input_output_aliases={1: 0}, compiler_params=_params("parallel"),
    )(src, dst)


def _cast_into_slot(name, x, slot):
    r, c = x.shape
    tr = _tile(r, max(SUBLANES * 2, (1 << 20) // c), SUBLANES * 2)

    def body(slot_ref, x_ref, o_ref):
        del slot_ref
        o_ref[...] = x_ref[...].astype(BF16)

    grid_spec = pltpu.PrefetchScalarGridSpec(
        num_scalar_prefetch=1, grid=(r // tr,),
        in_specs=[pl.BlockSpec((tr, c), lambda i, slot_ref: (i, 0))],
        out_specs=pl.BlockSpec((None, tr, c), lambda i, slot_ref: (slot_ref[0], i, 0)))
    return pl.pallas_call(
        body, name=name, grid_spec=grid_spec, out_shape=_sds((N_DEV, r, c), BF16),
        compiler_params=_params("parallel"),
    )(slot, x)


def _adamw_update(gv, w_ref, m_ref, v_ref, d_ref, nm_ref, nv_ref):
    c1 = 1.0 - ADAM_B1 ** ADAM_STEP
    c2 = 1.0 - ADAM_B2 ** ADAM_STEP
    nm = ADAM_B1 * m_ref[...] + (1.0 - ADAM_B1) * gv
    nv = ADAM_B2 * v_ref[...] + (1.0 - ADAM_B2) * (gv * gv)
    d_ref[...] = -ADAM_LR * ((nm / c1) / (jnp.sqrt(nv / c2) + ADAM_EPS) + ADAM_WD * w_ref[...])
    nm_ref[...] = nm
    nv_ref[...] = nv


def _adamw(name, w, g, m, v, deps=()):
    r, c = w.shape
    tr = _tile(r, max(SUBLANES, (1 << 19) // c))

    def body(w_ref, g_ref, m_ref, v_ref, d_ref, nm_ref, nv_ref):
        _adamw_update(g_ref[...], w_ref, m_ref, v_ref, d_ref, nm_ref, nv_ref)

    blk = pl.BlockSpec((tr, c), lambda i: (i, 0))
    return pl.pallas_call(
        _with_deps(body, 4, deps), name=name, grid=(r // tr,), out_shape=(_sds((r, c), F32),) * 3,
        in_specs=[blk] * 4 + [_ANY] * len(deps), out_specs=(blk,) * 3, compiler_params=_params("parallel"),
    )(w, g, m, v, *deps)


def _adamw_summed(name, w, own, received, m, v, deps=()):
    r, c = w.shape
    nj = received.shape[0]
    tr = _tile(r, max(2 * SUBLANES, (1 << 19) // c), 2 * SUBLANES)

    def body(w_ref, own_ref, rec_ref, m_ref, v_ref, g_ref, d_ref, nm_ref, nv_ref):
        gv = own_ref[...].astype(F32)
        for j in range(nj):
            gv = gv + rec_ref[j].astype(F32)
        g_ref[...] = gv
        _adamw_update(gv, w_ref, m_ref, v_ref, d_ref, nm_ref, nv_ref)

    blk = pl.BlockSpec((tr, c), lambda i: (i, 0))
    return pl.pallas_call(
        _with_deps(body, 5, deps), name=name, grid=(r // tr,), out_shape=(_sds((r, c), F32),) * 4,
        in_specs=[blk, pl.BlockSpec((None, tr, c), lambda i: (0, i, 0)),
                  pl.BlockSpec((nj, tr, c), lambda i: (0, i, 0)), blk, blk] + [_ANY] * len(deps),
        out_specs=(blk,) * 4, compiler_params=_params("parallel"),
    )(w, own, received, m, v, *deps)


def _rows_of_lanes(v):
    rows = v.shape[1] // LANES
    out = v.reshape(rows, LANES)
    pad = (-rows) % SUBLANES
    if pad:
        out = jnp.pad(out, ((0, pad), (0, 0)))
    return out


def kernel(x, c, w_ada, b_ada, norm1_w, w_in, q_norm_w, k_norm_w, w_pool, pool_scale, w_a_up, w_b_up, w_o, norm2_w, w_ff1, w_ff2, loss_target, m_w_ada, m_b_ada, m_norm1_w, m_w_in, m_q_norm_w, m_k_norm_w, m_w_pool, m_pool_scale, m_w_a_up, m_w_b_up, m_w_o, m_norm2_w, m_w_ff1, m_w_ff2, v_w_ada, v_b_ada, v_norm1_w, v_w_in, v_q_norm_w, v_k_norm_w, v_w_pool, v_pool_scale, v_w_a_up, v_w_b_up, v_w_o, v_norm2_w, v_w_ff1, v_w_ff2):
    _, s, d = x.shape
    half = d // 2
    d8 = d // N_DEV
    n_groups = len(POOL_WINDOWS)
    cg = half // n_groups
    me = _group_index(MESH_AXES)

    x2 = x[0]
    target = loss_target[0]

    my_slot = jnp.reshape(me, (1,)).astype(jnp.int32)

    def cast(i, t):
        return _cast_into_slot("cast_w%d" % i, t, my_slot)

    def gather_start(tag, bufs, phase):
        if phase < 2:
            return _launch("ag%s_ici%d" % (tag, phase), bufs, _plan_gather_ici(phase), (2, 4)[phase] * len(bufs))
        return _launch("ag%s_d2d" % tag, bufs, _plan_gather_d2d, len(_CHIP_MASKS) * len(bufs))

    def gather_plans(tag, n_bufs, phase):
        if phase < 2:
            return ("ag%s_ici%d" % (tag, phase), _plan_gather_ici(phase), (2, 4)[phase] * n_bufs)
        return ("ag%s_d2d" % tag, _plan_gather_d2d, len(_CHIP_MASKS) * n_bufs)

    bx, by, bc = _AXIS_BIT["x"], _AXIS_BIT["y"], _AXIS_BIT["c"]
    buf_a = [cast(0, w_in[0])]

    c_all = _all_gather_2d("ag_c", c.reshape(d // LANES, LANES), deps=tuple(buf_a)).reshape(N_DEV, d)
    wa = w_ada.shape[2]
    b_shard = lax.dynamic_slice_in_dim(b_ada, me * wa, wa, axis=1)
    mod_part = _ada_forward(c_all, w_ada[0], b_shard)
    mod_rows = mod_part.reshape(N_DEV * wa // LANES, LANES)
    mod_buf = lax.dynamic_update_slice(jnp.zeros((N_DEV,) + mod_rows.shape, F32), mod_rows[None], (me, 0, 0))
    (fl_mod, fl_a, fl_pair), tok = _launch_groups(
        "agA_near", [([mod_buf], [("ag_mod", _plan_everyone, N_DEV - 1)]),
                     (buf_a, [("agA_near", _plan_neighbours, 2), ("agA_pair_d2d", _plan_d2d((0,)), 1)])])
    mod_all, = _land(fl_mod, tok)
    buf_b = [cast(1, w_pool[0].reshape(-1, cg)), cast(2, w_a_up[0]), cast(3, w_b_up[0]), cast(4, w_o[0])]
    buf_c = [cast(5, w_ff1[0])]
    buf_e = [cast(6, w_ff2[0])]
    mod_all = mod_all.reshape(N_DEV, N_DEV, wa)
    mod = lax.dynamic_slice_in_dim(mod_all, me, 1, axis=1).reshape(1, N_MOD * d)
    shift1, scale1, gate1, shift2, scale2, gate2 = [mod[:, i * d:(i + 1) * d] for i in range(N_MOD)]

    h = _norm_forward("norm1_fwd", x2, norm1_w, scale1, shift1)
    buf_a = _land(fl_pair, h, bufs=fl_a[-1])
    pair_slots = jnp.stack([me, me ^ bc]).astype(jnp.int32)
    proj = _project_slots("proj_pair", h, buf_a[0], pair_slots, 4 * d)
    buf_a = _land(fl_a, [proj] + buf_b + buf_c + buf_e, bufs=buf_a)
    near = (bx, by)
    (fl_far, fl_near), tok = _launch_groups(
        "agA_far", [(buf_a, [("agA_far", _plan_diagonal, 2), ("agA_near_d2d", _plan_d2d(near), len(near))])])
    (fl_b, fl_c, fl_e), tok = _launch_groups(
        "agBCE_ici0", [(buf_b, [gather_plans("B", len(buf_b), 0)]), (buf_c, [gather_plans("C", 1, 0)]),
                       (buf_e, [gather_plans("E", 1, 0)])])

    tm = _tile(s, 1024)
    tk = _tile(d, 2048)
    te = _tile(d, 512)

    own_slots = jnp.stack([me ^ m for m in near]).astype(jnp.int32)
    far_slots = jnp.stack([me ^ bx ^ by ^ f for f in (0, bc)]).astype(jnp.int32)
    buf_a = fl_near[-1]
    proj = _project_slots("proj_own", h, buf_a[0], own_slots, 4 * d, proj_in=proj, deps=(tok,))
    buf_a = _land(fl_near, proj, bufs=buf_a)
    proj = _project_slots("proj_sibling", h, buf_a[0], own_slots ^ bc, 4 * d, proj_in=proj)
    buf_a = _land(fl_far, proj, bufs=buf_a)
    fl_a, tok = _launch("agA_far_d2d", buf_a, _plan_d2d((bx | by,)), 1)
    w_in_f, = _land(fl_a, tok)
    proj = _project_slots("proj_far", h, w_in_f, far_slots, 4 * d, proj_in=proj)
    qn, kn, vb = _qkv_prepare(proj, q_norm_w, k_norm_w, half)
    buf_b = _land(fl_b, qn)
    buf_c = _land(fl_c, qn)
    (fl_b, fl_c), tok = _launch_groups(
        "agBC_ici1", [(buf_b, [gather_plans("B", len(buf_b), 1)]), (buf_c, [gather_plans("C", 1, 1)])])
    attn = _attention_forward(qn, kn, vb, deps=(tok,))
    buf_b = _land(fl_b, attn)
    buf_e = _land(fl_e, attn)
    (fl_b, fl_e), tok = _launch_groups(
        "agB_d2d_E_ici1", [(buf_b, [gather_plans("B", len(buf_b), 2)]), (buf_e, [gather_plans("E", 1, 1)])])
    w_pool_f, w_a_f, w_b_f, w_o_f = _land(fl_b, tok)
    rows_pool = cg // N_DEV
    w_pool_f = w_pool_f.reshape(N_DEV, n_groups, rows_pool, cg).transpose(1, 0, 2, 3).reshape(n_groups, cg, cg)
    w_o_f = w_o_f.reshape(d, d)
    pooled, ya_in = _pool_forward(proj, w_pool_f, pool_scale)

    def merge_epilogue(ga_ref, gb_ref, ya, yb, out_refs):
        merged_ref, ya_ref, yb_ref = out_refs
        merged = jax.nn.sigmoid(ga_ref[...]) * ya + jax.nn.sigmoid(gb_ref[...]) * yb
        merged_ref[...] = merged.astype(BF16)
        ya_ref[...] = ya.astype(BF16)
        yb_ref[...] = yb.astype(BF16)

    def up_body(a1_ref, b1_ref, a2_ref, b2_ref, ga_ref, gb_ref, *out_refs):
        merge_epilogue(ga_ref, gb_ref, _dot(a1_ref[...], b1_ref[...]), _dot(a2_ref[...], b2_ref[...]), out_refs)

    ga_blk0 = 2 * d // d8
    gb_blk0 = 3 * d // d8
    tu = s
    a_spec = pl.BlockSpec((tu, half), lambda i, j: (i, 0))
    wup_spec = pl.BlockSpec((None, half, d8), lambda i, j: (j, 0, 0))
    o_blk = pl.BlockSpec((tu, d8), lambda i, j: (i, j))
    merged, y_a, y_b = pl.pallas_call(
        up_body, name="up_merge", grid=(s // tu, N_DEV), out_shape=(_sds((s, d), BF16),) * 3,
        in_specs=[a_spec, wup_spec, a_spec, wup_spec,
                  pl.BlockSpec((tu, d8), lambda i, j: (i, ga_blk0 + j)),
                  pl.BlockSpec((tu, d8), lambda i, j: (i, gb_blk0 + j))],
        out_specs=(o_blk,) * 3, compiler_params=_params("parallel", "parallel"),
    )(ya_in, w_a_f, attn, w_b_f, proj, proj)
    buf_c = _land(fl_c, merged)
    fl_c, tok_c = gather_start("C", buf_c, 2)

    tn = _tile(d, 1024)

    def oproj_epilogue(acc, extra_refs, out_refs):
        x_ref, g_ref = extra_refs
        x1_ref, o_ref = out_refs
        x1_ref[...] = x_ref[...] + g_ref[...] * acc
        o_ref[...] = acc.astype(BF16)

    mn_blk = pl.BlockSpec((tm, tn), lambda i, j, k: (i, j))
    e_blk = pl.BlockSpec((tm, te), lambda i, j, k: (i, j))
    e_vec = pl.BlockSpec((1, te), lambda i, j, k: (0, j))
    s_blk = pl.BlockSpec((s, te), lambda i, j, k: (i, j))
    x1, o_act = _matmul(
        "oproj", "nn", (1, d // te, d // tk), merged, pl.BlockSpec((s, tk), lambda i, j, k: (i, k)),
        w_o_f, pl.BlockSpec((tk, te), lambda i, j, k: (k, j)),
        [_sds((s, d), F32), _sds((s, d), BF16)], [s_blk, s_blk], (s, te),
        epilogue=oproj_epilogue, extras=(x2, gate1), extra_specs=(s_blk, e_vec), deps=(tok_c,))

    h2 = _norm_forward("norm2_fwd", x1, norm2_w, scale2, shift2)
    w_ff1_f, = _land(fl_c, h2)

    def ff1_epilogue(acc, extra_refs, out_refs):
        r = jnp.maximum(acc, 0.0)
        out_refs[0][...] = r.astype(BF16)
        out_refs[1][...] = (r * r).astype(BF16)

    n_rows = s // tm
    first = max(n_rows // 2, 1)

    def ff1_rows(name, row0, rows, prior, deps):
        blk = pl.BlockSpec((tm, half), lambda i, j, k: (i + row0, j))
        return _matmul(
            name, "nn", (rows, N_DEV, d // tk), h2, pl.BlockSpec((tm, tk), lambda i, j, k: (i + row0, k)),
            w_ff1_f, pl.BlockSpec((None, tk, half), lambda i, j, k: (j, k, 0)),
            [_sds((s, 4 * d), BF16)] * 2, [blk, blk], (tm, half), epilogue=ff1_epilogue,
            extras=prior, extra_specs=[_ANY] * len(prior), aliases={2 + n: n for n in range(len(prior))}, deps=deps)

    relu, act = ff1_rows("ff1_a", 0, first, (), ())
    if n_rows > first:
        buf_e = _land(fl_e, act)
        fl_e, tok_e = gather_start("E", buf_e, 2)
        relu, act = ff1_rows("ff1_b", first, n_rows - first, (relu, act), (tok_e,))
    else:
        fl_e, tok_e = gather_start("E", _land(fl_e, act), 2)
    w_ff2_f, = _land(fl_e, act if n_rows > first else tok_e)
    w_ff2_f = w_ff2_f.reshape(4 * d, d)

    def ff2_epilogue(acc, extra_refs, out_refs):
        x1_ref, g_ref, t_ref = extra_refs
        df_ref, dy_ref, sq_ref, dgate_ref = out_refs
        gate = g_ref[...]
        err = x1_ref[...] + gate * acc - t_ref[...]
        dyv = err * (1.0 / d)
        dy_ref[...] = dyv
        df_ref[...] = (dyv * gate).astype(BF16)
        sq_ref[...] = jnp.full(sq_ref.shape, jnp.sum(err * err), F32)
        dgate_ref[...] = jnp.broadcast_to(jnp.sum(dyv * acc, axis=0, keepdims=True), dgate_ref.shape)

    df, dy, sq, dgate2_parts = _matmul(
        "ff2", "nn", (s // tm, d // te, 2 * d // tk), act, pl.BlockSpec((tm, 2 * tk), lambda i, j, k: (i, k)),
        w_ff2_f, pl.BlockSpec((2 * tk, te), lambda i, j, k: (k, j)),
        [_sds((s, d), BF16), _sds((s, d), F32), _sds((s // tm * SUBLANES, d // te * LANES), F32),
         _sds((s // tm * SUBLANES, d), F32)],
        [e_blk, e_blk, pl.BlockSpec((SUBLANES, LANES), lambda i, j, k: (i, j)),
         pl.BlockSpec((SUBLANES, te), lambda i, j, k: (i, j))], (tm, te),
        epilogue=ff2_epilogue, extras=(x1, gate2, target), extra_specs=(e_blk, e_vec, e_blk))
    loss_local = (0.5 / d) * jnp.sum(sq[::SUBLANES, ::LANES])
    dgate2 = jnp.sum(dgate2_parts[::SUBLANES], axis=0, keepdims=True)

    tok_k = _tile(s, 2048)
    tw = _tile(d, 1024)
    g_ff2 = _matmul(
        "g_ff2", "tn", (4 * d // tw, d // tn, s // tok_k), act, pl.BlockSpec((tok_k, tw), lambda i, j, k: (k, i)),
        df, pl.BlockSpec((tok_k, tn), lambda i, j, k: (k, j)),
        [_sds((4 * d, d), BF16)], [pl.BlockSpec((tw, tn), lambda i, j, k: (i, j))], (tw, tn))[0]

    def da_epilogue(acc, extra_refs, out_refs):
        out_refs[0][...] = (acc * (2.0 * extra_refs[0][...].astype(F32))).astype(BF16)

    big_blk = pl.BlockSpec((tm, tn), lambda i, j, k: (i, j))
    fl_f2, tok = _reduce_scatter_start("F2", [g_ff2.reshape(N_DEV, half, d)])
    df1 = _matmul(
        "da_ff", "nt", (s // tm, 4 * d // tn, d // tk), df, pl.BlockSpec((tm, tk), lambda i, j, k: (i, k)),
        w_ff2_f, pl.BlockSpec((tn, tk), lambda i, j, k: (j, k)),
        [_sds((s, 4 * d), BF16)], [big_blk], (tm, tn),
        epilogue=da_epilogue, extras=(relu,), extra_specs=(big_blk,), deps=(tok,))[0]

    fl_f2, tok = _reduce_scatter_middle("F2", fl_f2, df1, me)
    g_ff1 = _matmul(
        "g_ff1", "tn", (d // tw, N_DEV, s // tok_k), h2, pl.BlockSpec((tok_k, tw), lambda i, j, k: (k, i)),
        df1, pl.BlockSpec((tok_k, half), lambda i, j, k: (k, j)),
        [_sds((N_DEV, d, half), BF16)], [pl.BlockSpec((None, tw, half), lambda i, j, k: (j, i, 0))], (tw, half),
        deps=(tok,))[0]

    fl_f1, tok = _reduce_scatter_start("F1", [g_ff1])
    dh2 = _matmul(
        "dh2", "nt", (s // tm, d // tn, N_DEV // 2), df1, pl.BlockSpec((tm, 2 * half), lambda i, j, k: (i, k)),
        w_ff1_f, pl.BlockSpec((2, tn, half), lambda i, j, k: (k, j, 0)),
        [_sds((s, d), F32)], [mn_blk], (tm, tn), deps=(tok,))[0]

    fl_f1, tok = _reduce_scatter_middle("F1", fl_f1, dh2, me)
    dx1, dshift2, dscale2, g_norm2, do, dgate1 = _norm_backward(
        "norm2_bwd", dh2, x1, norm2_w, scale2, dy, gated=(gate1, o_act), deps=(tok,))
    sum_ff2, = _reduce_scatter_finish(fl_f2, do)

    g_o = _matmul(
        "g_o", "tn", (d // tw, d // tn, s // tok_k), merged, pl.BlockSpec((tok_k, tw), lambda i, j, k: (k, i)),
        do, pl.BlockSpec((tok_k, tn), lambda i, j, k: (k, j)),
        [_sds((d, d), BF16)], [pl.BlockSpec((tw, tn), lambda i, j, k: (i, j))], (tw, tn))[0]

    def merge_bwd_epilogue(acc, extra_refs, out_refs):
        ga_ref, gb_ref, ya_ref, yb_ref = extra_refs
        dya_ref, dyb_ref, dga_ref, dgb_ref = out_refs
        sa = jax.nn.sigmoid(ga_ref[...])
        sb = jax.nn.sigmoid(gb_ref[...])
        dya_ref[...] = (acc * sa).astype(BF16)
        dyb_ref[...] = (acc * sb).astype(BF16)
        dga_ref[...] = (acc * ya_ref[...].astype(F32) * (sa * (1.0 - sa))).astype(BF16)
        dgb_ref[...] = (acc * yb_ref[...].astype(F32) * (sb * (1.0 - sb))).astype(BF16)

    td = _tile(d, 256)
    nb = d // td
    d_blk = pl.BlockSpec((s, td), lambda i, j, k: (i, j))
    dy_a, dy_b, dproj, dg_b = _matmul(
        "dmerged", "nt", (1, nb, d // tk), do, pl.BlockSpec((s, tk), lambda i, j, k: (i, k)),
        w_o_f, pl.BlockSpec((td, tk), lambda i, j, k: (j, k)),
        [_sds((s, d), BF16), _sds((s, d), BF16), _sds((s, 4 * d), BF16), _sds((s, d), BF16)],
        [d_blk, d_blk, pl.BlockSpec((s, td), lambda i, j, k: (i, 2 * nb + j)), d_blk], (s, td),
        epilogue=merge_bwd_epilogue, extras=(proj, proj, y_a, y_b),
        extra_specs=(pl.BlockSpec((s, td), lambda i, j, k: (i, 2 * nb + j)),
                     pl.BlockSpec((s, td), lambda i, j, k: (i, 3 * nb + j)), d_blk, d_blk))
    dproj = _place_columns("place_dgb", dg_b, dproj, 3)

    up_a = pl.BlockSpec((tok_k, half), lambda i, j, k: (k, 0))
    up_b = pl.BlockSpec((tok_k, d8), lambda i, j, k: (k, j))
    up_o = pl.BlockSpec((None, half, d8), lambda i, j, k: (j, 0, 0))
    g_a_up = _matmul("g_a_up", "tn", (1, N_DEV, s // tok_k), ya_in, up_a, dy_a, up_b,
                     [_sds((N_DEV, half, d8), BF16)], [up_o], (half, d8))[0]
    g_b_up = _matmul("g_b_up", "tn", (1, N_DEV, s // tok_k), attn, up_a, dy_b, up_b,
                     [_sds((N_DEV, half, d8), BF16)], [up_o], (half, d8))[0]
    slabs = 4
    dn_a = pl.BlockSpec((tm, slabs * d8), lambda i, j, k: (i, k))
    dn_b = pl.BlockSpec((slabs, half, d8), lambda i, j, k: (k, 0, 0))
    dn_o = pl.BlockSpec((tm, half), lambda i, j, k: (i, 0))
    dya_in = _matmul("d_ya_in", "nt", (s // tm, 1, N_DEV // slabs), dy_a, dn_a, w_a_f, dn_b,
                     [_sds((s, half), BF16)], [dn_o], (tm, half))[0]
    dattn = _matmul("d_attn", "nt", (s // tm, 1, N_DEV // slabs), dy_b, dn_a, w_b_f, dn_b,
                    [_sds((s, half), BF16)], [dn_o], (tm, half))[0]

    dproj, g_pool, g_pool_scale = _pool_backward(dya_in, pooled, w_pool_f, pool_scale, dproj)
    sum_ff1, = _reduce_scatter_finish(fl_f1, g_pool)
    g_pool_send = g_pool.astype(BF16).reshape(n_groups, N_DEV, rows_pool, cg).transpose(1, 0, 2, 3)
    g_pool_send = g_pool_send.reshape(N_DEV, n_groups * rows_pool, cg)
    fl_b, tok = _reduce_scatter_start("B", [g_pool_send, g_a_up, g_b_up, g_o.reshape(N_DEV, d8, d)])
    dqn, dkn, dproj = _attention_backward(qn, kn, vb, dattn, dproj, 3, deps=(tok,))
    fl_b, tok = _reduce_scatter_middle("B", fl_b, dqn, me)
    dproj, g_qnorm = _qk_norm_backward("qnorm_bwd", dqn, proj, 1, q_norm_w, dproj, half, deps=(tok,))
    dproj, g_knorm = _qk_norm_backward("knorm_bwd", dkn, proj, 2, k_norm_w, dproj, half)

    g_in = _matmul(
        "g_in", "tn", (d // tw, N_DEV, s // tok_k), h, pl.BlockSpec((tok_k, tw), lambda i, j, k: (k, i)),
        dproj, pl.BlockSpec((tok_k, half), lambda i, j, k: (k, j)),
        [_sds((N_DEV, d, half), BF16)], [pl.BlockSpec((None, tw, half), lambda i, j, k: (j, i, 0))], (tw, half))[0]
    fl_in, tok = _reduce_scatter_start("I", [g_in])
    dh = _matmul(
        "dh", "nt", (s // tm, d // tn, N_DEV // 2), dproj, pl.BlockSpec((tm, 2 * half), lambda i, j, k: (i, k)),
        w_in_f, pl.BlockSpec((2, tn, half), lambda i, j, k: (k, j, 0)),
        [_sds((s, d), F32)], [mn_blk], (tm, tn), deps=(tok,))[0]
    sum_pool, sum_a_up, sum_b_up, sum_o = _reduce_scatter_finish(fl_b, dh)
    buf_in = _reduce_scatter_add("I", fl_in, dh, me, n_land=2)
    grad_x, dshift1, dscale1, g_norm1 = _norm_backward("norm1_bwd", dh, x2, norm1_w, scale1, dx1, deps=tuple(buf_in[:1]))

    dmod = jnp.concatenate([dshift1, dscale1, dgate1, dshift2, dscale2, dgate2], axis=1)
    pieces = [dmod, g_norm1, g_norm2, g_pool_scale, g_qnorm, g_knorm, jnp.full((1, LANES), loss_local, F32)]
    packed_rows = [_rows_of_lanes(p) for p in pieces]
    offsets = [0]
    for p in packed_rows:
        offsets.append(offsets[-1] + p.shape[0])
    packed = jnp.concatenate(packed_rows, axis=0)
    small_buf = lax.dynamic_update_slice(jnp.zeros((N_DEV,) + packed.shape, F32), packed[None], (me, 0, 0))
    (fl_small, fl_in), tok = _launch_groups(
        "rsI_relay0_small", [([small_buf], [("ag_small", _plan_everyone, N_DEV - 1)]),
                             (buf_in, [("rsI_relay0", _plan_scatter_relay_first, 4)])])
    small_all, = _land(fl_small, tok)
    small_sum = _sum_slots("small_sum", small_all[None], F32)[0]

    def unpack(i, width):
        return small_sum[offsets[i]:offsets[i] + width // LANES].reshape(1, width)

    g_b_ada = unpack(0, N_MOD * d)
    g_norm1_w = unpack(1, d)
    g_norm2_w = unpack(2, d)
    g_pool_scale_w = unpack(3, half)
    g_q_norm_w = unpack(4, HEAD_DIM)
    g_k_norm_w = unpack(5, HEAD_DIM)
    loss = unpack(6, LANES)[0, 0]
    dmod_all = small_all[:, :N_MOD * d // LANES].reshape(N_DEV, N_MOD * d)
    dmod_cols = lax.dynamic_slice_in_dim(dmod_all, me * wa, wa, axis=1)

    grads = {
        "b_ada": g_b_ada, "norm1_w": g_norm1_w,
        "q_norm_w": g_q_norm_w, "k_norm_w": g_k_norm_w,
        "pool_scale": g_pool_scale_w, "norm2_w": g_norm2_w,
    }
    sums = {"w_pool": sum_pool, "w_a_up": sum_a_up, "w_b_up": sum_b_up, "w_o": sum_o,
            "w_ff1": sum_ff1, "w_ff2": sum_ff2}
    weights = {"w_ada": (w_ada, m_w_ada, v_w_ada), "b_ada": (b_ada, m_b_ada, v_b_ada),
               "norm1_w": (norm1_w, m_norm1_w, v_norm1_w), "w_in": (w_in, m_w_in, v_w_in),
               "q_norm_w": (q_norm_w, m_q_norm_w, v_q_norm_w), "k_norm_w": (k_norm_w, m_k_norm_w, v_k_norm_w),
               "w_pool": (w_pool, m_w_pool, v_w_pool), "pool_scale": (pool_scale, m_pool_scale, v_pool_scale),
               "w_a_up": (w_a_up, m_w_a_up, v_w_a_up), "w_b_up": (w_b_up, m_w_b_up, v_w_b_up),
               "w_o": (w_o, m_w_o, v_w_o), "norm2_w": (norm2_w, m_norm2_w, v_norm2_w),
               "w_ff1": (w_ff1, m_w_ff1, v_w_ff1), "w_ff2": (w_ff2, m_w_ff2, v_w_ff2)}
    order = list(weights)
    deltas, new_m, new_v = {}, {}, {}
    def adam(name, deps=()):
        wt, mt, vt = weights[name]
        shape = wt.shape
        flat = (-1, shape[-1])
        if name in sums:
            own, received = sums[name]
            g, dl, nm, nv = _adamw_summed("adamw_" + name, wt.reshape(flat), own, received,
                                          mt.reshape(flat), vt.reshape(flat), deps=deps)
            grads[name] = g.reshape(shape)
        else:
            dl, nm, nv = _adamw("adamw_" + name, wt.reshape(flat), grads[name].reshape(flat),
                                mt.reshape(flat), vt.reshape(flat), deps=deps)
        deltas[name], new_m[name], new_v[name] = dl.reshape(shape), nm.reshape(shape), nv.reshape(shape)

    behind_first = list(sums)
    for name in behind_first:
        adam(name, deps=(tok,))
    grads["w_ada"] = _ada_weight_grad(c_all, dmod_cols, deps=(tok,))[None]
    chip_sums, relayed = _land(fl_in, [deltas[n] for n in behind_first] + [grads["w_ada"]])
    passed_on = _add_relayed("rsI_add_relay", chip_sums, relayed)
    fl_in, tok = _launch("rsI_relay1", [passed_on, lax.empty((1,) + passed_on.shape[1:], passed_on.dtype)],
                         _plan_scatter_relay_second, 2)
    behind_second = [n for n in order if n not in sums and n != "w_in"]
    for name in behind_second:
        adam(name, deps=(tok,))
    sums["w_in"] = tuple(_land(fl_in, [deltas[n] for n in behind_second]))
    adam("w_in")

    return (loss, grad_x[None], *[grads[n] for n in order], *[deltas[n] for n in order],
            *[new_m[n] for n in order], *[new_v[n] for n in order])
```

```python
import math

import jax
import jax.numpy as jnp
from jax import lax
from jax.experimental import pallas as pl
from jax.experimental.pallas import tpu as pltpu

F32 = jnp.float32
BF16 = jnp.bfloat16
MESH_AXES = ("x", "y", "c")
N_DEV = 8
HEAD_DIM = 128
ATTN_SCALE = 1.0 / math.sqrt(HEAD_DIM)
POOL_WINDOWS = (2, 4, 8, 16)
N_MOD = 6
NORM_EPS = 1e-6
LANES = 128
SUBLANES = 8
VMEM_LIMIT_BYTES = 56 * 1024 * 1024
Q_TILE = 256
K_TILE = 256
POOL_TILE = 512
HEADS_PER_STEP = 8
HEADS_PER_STEP_BWD = 4

ADAM_LR = 0.001
ADAM_B1 = 0.9
ADAM_B2 = 0.999
ADAM_EPS = 1e-08
ADAM_WD = 0.01
ADAM_STEP = 10

_NN = (((1,), (0,)), ((), ()))
_NT = (((1,), (1,)), ((), ()))
_TN = (((0,), (0,)), ((), ()))
_DIMS = {"nn": _NN, "nt": _NT, "tn": _TN}


def _dot(a, b, mode="nn"):
    return lax.dot_general(a, b, _DIMS[mode], preferred_element_type=F32)


def _params(*sem):
    return pltpu.CompilerParams(dimension_semantics=sem, vmem_limit_bytes=VMEM_LIMIT_BYTES)


def _tile(dim, pref, align=SUBLANES):
    for t in range(min(dim, pref), 0, -1):
        if dim % t == 0 and t % align == 0:
            return t
    return dim


def _group_index(axes):
    idx = 0
    for a in axes:
        idx = idx * 2 + lax.axis_index(a)
    return idx


def _peer_device(axes, k):
    coords = {a: lax.axis_index(a) for a in MESH_AXES}
    for pos, a in enumerate(axes):
        if (k >> (len(axes) - 1 - pos)) & 1:
            coords[a] = 1 - coords[a]
    return tuple(coords[a] for a in MESH_AXES)


_AXIS_BIT = {"x": 4, "y": 2, "c": 1}
_ANY = pl.BlockSpec(memory_space=pl.ANY)


def _device_xor(mask):
    return tuple(1 - lax.axis_index(a) if mask & _AXIS_BIT[a] else lax.axis_index(a) for a in MESH_AXES)


def _remote(src, dst, send_sem, recv_sem, mask):
    return pltpu.make_async_remote_copy(src_ref=src, dst_ref=dst, send_sem=send_sem, recv_sem=recv_sem,
                                        device_id=_device_xor(mask), device_id_type=pl.DeviceIdType.MESH)


_CHIP_MASKS = (0, _AXIS_BIT["y"], _AXIS_BIT["x"], _AXIS_BIT["x"] | _AXIS_BIT["y"])


def _add_received(name, own, own_slots, received, out_dtype):
    nj, r, c = received.shape
    tr = _tile(r, max(2 * SUBLANES, (1 << 20) // c), 2 * SUBLANES)

    def body(slots_ref, own_ref, rec_ref, o_ref):
        del slots_ref
        o_ref[...] = (own_ref[...].astype(F32) + rec_ref[...].astype(F32)).astype(o_ref.dtype)

    grid_spec = pltpu.PrefetchScalarGridSpec(
        num_scalar_prefetch=1, grid=(nj, r // tr),
        in_specs=[pl.BlockSpec((None, tr, c), lambda j, i, slots: (slots[j], i, 0)),
                  pl.BlockSpec((None, tr, c), lambda j, i, slots: (j, i, 0))],
        out_specs=pl.BlockSpec((None, tr, c), lambda j, i, slots: (j, i, 0)))
    return pl.pallas_call(
        body, name=name, grid_spec=grid_spec, out_shape=jax.ShapeDtypeStruct((nj, r, c), out_dtype),
        compiler_params=_params("parallel", "parallel"),
    )(own_slots, own, received)


_HBM = pl.BlockSpec(memory_space=pltpu.HBM)
_SEM = pl.BlockSpec(memory_space=pltpu.SEMAPHORE)
_DATAFLOW = pltpu.SideEffectType.DATAFLOW_SIDE_EFFECTING


def _launch_groups(name, groups, deps=()):
    bufs = [b for g_bufs, _ in groups for b in g_bufs]
    specs = [(len(g_bufs), spec) for g_bufs, g_plans in groups for spec in g_plans]
    nb, ns = len(bufs), len(specs)

    def body(*refs):
        sems = refs[nb + len(deps):nb + len(deps) + 2 * ns]
        me = _group_index(MESH_AXES)
        first, which = 0, 0
        for g_bufs, g_plans in groups:
            ins = refs[first:first + len(g_bufs)]
            for _, plan, n_copies in g_plans:
                copies = plan(ins, me)
                assert len(copies) == n_copies
                for n, (src, dst, mask) in enumerate(copies):
                    _remote(src, dst, sems[2 * which].at[n], sems[2 * which + 1].at[n], mask).start()
                which += 1
            first += len(g_bufs)
        refs[-1][...] = jnp.zeros_like(refs[-1])

    sem_shapes = [pltpu.SemaphoreType.DMA((n,)) for _, (_, _, n) in specs for _ in range(2)]
    outs = pl.pallas_call(
        body, name=name,
        out_shape=(*sem_shapes, *[pltpu.HBM(b.shape, b.dtype) for b in bufs],
                   jax.ShapeDtypeStruct((SUBLANES, LANES), F32)),
        in_specs=[_HBM] * nb + [_ANY] * len(deps),
        out_specs=(*[_SEM] * (2 * ns), *[_HBM] * nb, pl.BlockSpec(memory_space=pltpu.VMEM)),
        input_output_aliases={i: 2 * ns + i for i in range(nb)},
        compiler_params=pltpu.CompilerParams(has_side_effects=_DATAFLOW),
    )(*[pltpu.with_memory_space_constraint(b, pltpu.HBM) for b in bufs], *deps)
    flights, first, which = [], 0, 0
    for g_bufs, g_plans in groups:
        through = list(outs[2 * ns + first:2 * ns + first + len(g_bufs)])
        for land_name, plan, n_copies in g_plans:
            flights.append((land_name, plan, n_copies, outs[2 * which], outs[2 * which + 1], through))
            which += 1
        first += len(g_bufs)
    return flights, outs[-1]


def _launch(name, bufs, plan, n_copies, deps=()):
    (flight,), token = _launch_groups(name, [(bufs, [(name, plan, n_copies)])], deps)
    return flight, token


def _land(flight, after, bufs=None):
    name, plan, n_copies, send_sems, recv_sems, launched = flight
    bufs = launched if bufs is None else bufs
    nb = len(bufs)
    after = list(after) if isinstance(after, (list, tuple)) else [after]

    def body(*refs):
        ins = refs[:nb]
        s_sems, r_sems = refs[nb], refs[nb + 1]
        for n, (src, dst, mask) in enumerate(plan(ins, _group_index(MESH_AXES))):
            cp = _remote(src, dst, s_sems.at[n], r_sems.at[n], mask)
            cp.wait_send()
            cp.wait_recv()

    outs = pl.pallas_call(
        body, name=name + "_land",
        out_shape=tuple(pltpu.HBM(b.shape, b.dtype) for b in bufs),
        in_specs=[_HBM] * nb + [_SEM, _SEM] + [_ANY] * len(after), out_specs=tuple([_HBM] * nb),
        input_output_aliases={i: i for i in range(nb)},
        compiler_params=pltpu.CompilerParams(has_side_effects=_DATAFLOW),
    )(*bufs, send_sems, recv_sems, *after)
    return list(outs)


def _plan_gather_ici(phase):
    bx, by = _AXIS_BIT["x"], _AXIS_BIT["y"]

    def plan(refs, me):
        copies = []
        for ref in refs:
            half = ref.shape[1] // 2

            def piece(slot, color, mask, ref=ref, half=half):
                p = ref.at[slot, pl.ds(color * half, half)]
                return (p, p, mask)

            if phase == 0:
                copies += [piece(me, 0, bx), piece(me, 1, by)]
            else:
                copies += [piece(me, 0, by), piece(me ^ bx, 0, by), piece(me, 1, bx), piece(me ^ by, 1, bx)]
        return copies

    return plan


def _plan_d2d(masks):
    def plan(refs, me):
        return [(ref.at[me ^ m], ref.at[me ^ m], _AXIS_BIT["c"]) for ref in refs for m in masks]

    return plan


def _plan_gather_d2d(refs, me):
    return _plan_d2d(_CHIP_MASKS)(refs, me)


def _plan_neighbours(refs, me):
    return [(ref.at[me], ref.at[me], _AXIS_BIT[a]) for ref in refs for a in ("x", "y")]


def _plan_diagonal(refs, me):
    bx, by = _AXIS_BIT["x"], _AXIS_BIT["y"]
    copies = []
    for ref in refs:
        half = ref.shape[1] // 2
        lo = ref.at[me ^ bx, pl.ds(0, half)]
        hi = ref.at[me ^ by, pl.ds(half, half)]
        copies += [(lo, lo, by), (hi, hi, bx)]
    return copies


def _plan_scatter_d2d(refs, me):
    na = len(refs) // 2
    copies = []
    for a in range(na):
        for j, m in enumerate(_CHIP_MASKS):
            copies.append((refs[a].at[me ^ _AXIS_BIT["c"] ^ m], refs[na + a].at[j], _AXIS_BIT["c"]))
    return copies


def _plan_scatter_ici(refs, me):
    del me
    na = len(refs) // 2
    copies = []
    for a in range(na):
        for n, m in enumerate(_CHIP_MASKS[1:]):
            copies.append((refs[a].at[n + 1], refs[na + a].at[n], m))
    return copies


def _with_deps(body, n_in, deps):
    if not deps:
        return body

    def wrapped(*refs):
        return body(*refs[:n_in], *refs[n_in + len(deps):])

    return wrapped


def _reduce_scatter_start(tag, grads):
    lands = [lax.empty((len(_CHIP_MASKS),) + g.shape[1:], g.dtype) for g in grads]
    return _launch("rs%s_d2d" % tag, list(grads) + lands, _plan_scatter_d2d, len(_CHIP_MASKS) * len(grads))


def _reduce_scatter_add(tag, flight, after, me, n_land=len(_CHIP_MASKS) - 1):
    bufs = _land(flight, after)
    na = len(bufs) // 2
    own_slots = jnp.stack([me ^ m for m in _CHIP_MASKS]).astype(jnp.int32)
    sums = [_add_received("rs%s_add_d2d_%d" % (tag, a), bufs[a], own_slots, bufs[na + a], BF16) for a in range(na)]
    lands = [lax.empty((n_land,) + h.shape[1:], h.dtype) for h in sums]
    return sums + lands


def _plan_scatter_relay_first(refs, me):
    del me
    bx, by = _AXIS_BIT["x"], _AXIS_BIT["y"]
    na = len(refs) // 2
    copies = []
    for a in range(na):
        h, land = refs[a], refs[na + a]
        half = h.shape[1] // 2
        lo, hi = pl.ds(0, half), pl.ds(half, half)
        copies += [(h.at[1, lo], land.at[0, lo], by), (h.at[3, lo], land.at[1, lo], by),
                   (h.at[2, hi], land.at[0, hi], bx), (h.at[3, hi], land.at[1, hi], bx)]
    return copies


def _plan_scatter_relay_second(refs, me):
    del me
    na = len(refs) // 2
    copies = []
    for a in range(na):
        f, land = refs[a], refs[na + a]
        half = f.shape[1] // 2
        lo, hi = pl.ds(0, half), pl.ds(half, half)
        copies += [(f.at[1, lo], land.at[0, lo], _AXIS_BIT["x"]), (f.at[1, hi], land.at[0, hi], _AXIS_BIT["y"])]
    return copies


def _add_relayed(name, sums, received):
    _, r, c = sums.shape
    tr = _tile(r // 2, max(2 * SUBLANES, (1 << 20) // c), 2 * SUBLANES)
    n_half = (r // 2) // tr

    def body(own_ref, rec_ref, o_ref):
        o_ref[...] = (own_ref[...].astype(F32) + rec_ref[...].astype(F32)).astype(o_ref.dtype)

    def own_slot(j, i):
        return jnp.where(j == 0, 0, jnp.where(i < n_half, 2, 1))

    return pl.pallas_call(
        body, name=name, grid=(2, 2 * n_half), out_shape=jax.ShapeDtypeStruct((2, r, c), sums.dtype),
        in_specs=[pl.BlockSpec((None, tr, c), lambda j, i: (own_slot(j, i), i, 0)),
                  pl.BlockSpec((None, tr, c), lambda j, i: (j, i, 0))],
        out_specs=pl.BlockSpec((None, tr, c), lambda j, i: (j, i, 0)),
        compiler_params=_params("parallel", "parallel"),
    )(sums, received)


def _reduce_scatter_plan(tag, bufs):
    return ("rs%s_ici" % tag, _plan_scatter_ici, (len(_CHIP_MASKS) - 1) * (len(bufs) // 2))


def _reduce_scatter_middle(tag, flight, after, me):
    bufs = _reduce_scatter_add(tag, flight, after, me)
    name, plan, n_copies = _reduce_scatter_plan(tag, bufs)
    return _launch(name, bufs, plan, n_copies)


def _plan_everyone(refs, me):
    return [(ref.at[me], ref.at[me], m) for ref in refs for m in range(1, N_DEV)]


def _reduce_scatter_finish(flight, after):
    bufs = _land(flight, after)
    na = len(bufs) // 2
    return [(bufs[a], bufs[na + a]) for a in range(na)]


def _all_gather_2d(name, x, deps=()):
    r, c = x.shape

    def body(x_ref, out_ref, send_sems, recv_sems):
        me = _group_index(MESH_AXES)
        out_ref[me] = x_ref[...]
        copies = []
        for k in range(1, N_DEV):
            cp = pltpu.make_async_remote_copy(
                src_ref=x_ref, dst_ref=out_ref.at[me],
                send_sem=send_sems.at[k - 1], recv_sem=recv_sems.at[k - 1],
                device_id=_peer_device(MESH_AXES, k), device_id_type=pl.DeviceIdType.MESH)
            cp.start()
            copies.append(cp)
        for cp in copies:
            cp.wait()

    vmem = pl.BlockSpec(memory_space=pltpu.VMEM)
    return pl.pallas_call(
        _with_deps(body, 1, deps), name=name, out_shape=jax.ShapeDtypeStruct((N_DEV, r, c), x.dtype),
        in_specs=[vmem] + [_ANY] * len(deps), out_specs=vmem,
        scratch_shapes=[pltpu.SemaphoreType.DMA((N_DEV - 1,)), pltpu.SemaphoreType.DMA((N_DEV - 1,))],
    )(x, *deps)


def _sum_slots(name, buf, out_dtype):
    pre, n, r, c = buf.shape
    tr = _tile(r, max(SUBLANES * 2, (1 << 20) // c))

    def body(b_ref, o_ref):
        acc = b_ref[0].astype(F32)
        for q in range(1, n):
            acc = acc + b_ref[q].astype(F32)
        o_ref[...] = acc.astype(o_ref.dtype)

    return pl.pallas_call(
        body, name=name, grid=(pre, r // tr),
        out_shape=jax.ShapeDtypeStruct((pre, r, c), out_dtype),
        in_specs=[pl.BlockSpec((None, n, tr, c), lambda i, j: (i, 0, j, 0))],
        out_specs=pl.BlockSpec((None, tr, c), lambda i, j: (i, j, 0)),
        compiler_params=_params("parallel", "parallel"),
    )(buf)


def _matmul(name, mode, grid, a, a_spec, b, b_spec, out_shapes, out_specs, acc_shape,
            epilogue=None, extras=(), extra_specs=(), aliases=None, deps=()):
    nk = grid[2]
    n_extra = len(extras)
    n_out = len(out_shapes)

    def finish(acc, extra_refs, out_refs):
        if epilogue is None:
            out_refs[0][...] = acc.astype(out_refs[0].dtype)
        else:
            epilogue(acc, extra_refs, out_refs)

    def product(a_ref, b_ref):
        if len(b_ref.shape) == 2:
            return _dot(a_ref[...], b_ref[...], mode)
        width = a_ref.shape[1] // b_ref.shape[0]
        total = None
        for i in range(b_ref.shape[0]):
            part = _dot(a_ref[:, i * width:(i + 1) * width], b_ref[i], mode)
            total = part if total is None else total + part
        return total

    def body(*refs):
        a_ref, b_ref = refs[0], refs[1]
        extra_refs = refs[2:2 + n_extra]
        out_refs = refs[2 + n_extra:2 + n_extra + n_out]
        if nk == 1:
            finish(product(a_ref, b_ref), extra_refs, out_refs)
            return
        acc_ref = refs[-1]
        k = pl.program_id(2)

        @pl.when(k == 0)
        def _():
            acc_ref[...] = product(a_ref, b_ref)

        @pl.when((k > 0) & (k < nk - 1))
        def _():
            acc_ref[...] += product(a_ref, b_ref)

        @pl.when(k == nk - 1)
        def _():
            finish(acc_ref[...] + product(a_ref, b_ref), extra_refs, out_refs)

    scratch = [] if nk == 1 else [pltpu.VMEM(acc_shape, F32)]
    return pl.pallas_call(
        _with_deps(body, 2 + n_extra, deps), name=name, grid=grid, out_shape=tuple(out_shapes),
        in_specs=[a_spec, b_spec] + list(extra_specs) + [_ANY] * len(deps), out_specs=tuple(out_specs),
        scratch_shapes=scratch, input_output_aliases=aliases or {},
        compiler_params=_params("parallel", "parallel", "arbitrary"),
    )(a, b, *extras, *deps)


def _sds(shape, dtype):
    return jax.ShapeDtypeStruct(tuple(shape), dtype)


def _project_slots(name, h, w_full, slots, out_cols, proj_in=None, deps=()):
    s, d = h.shape
    _, _, wide = w_full.shape
    tm = _tile(s, 1024)
    n_in = 4 if proj_in is not None else 3

    def body(*refs):
        refs[-1][...] = _dot(refs[1][...], refs[2][...])

    grid_spec = pltpu.PrefetchScalarGridSpec(
        num_scalar_prefetch=1, grid=(s // tm, slots.shape[0]),
        in_specs=[pl.BlockSpec((tm, d), lambda i, j, sl: (i, 0)),
                  pl.BlockSpec((None, d, wide), lambda i, j, sl: (sl[j], 0, 0))]
        + [_ANY] * (n_in - 3 + len(deps)),
        out_specs=pl.BlockSpec((tm, wide), lambda i, j, sl: (i, sl[j])))
    extra = ([proj_in] if proj_in is not None else []) + list(deps)
    return pl.pallas_call(
        body, name=name, grid_spec=grid_spec, out_shape=_sds((s, out_cols), F32),
        input_output_aliases={3: 0} if proj_in is not None else {},
        compiler_params=_params("parallel", "arbitrary"),
    )(slots, h, w_full, *extra)


def _ada_forward(c_all, w_ada, b_shard):
    nb, d = c_all.shape
    w = w_ada.shape[1]
    tn = _tile(w, 512)

    def body(c_ref, w_ref, b_ref, o_ref):
        cv = c_ref[...]
        sc = cv * jax.nn.sigmoid(cv)
        o_ref[...] = jnp.dot(sc, w_ref[...], precision=lax.Precision.HIGHEST,
                             preferred_element_type=F32) + b_ref[...]

    return pl.pallas_call(
        body, name="ada_fwd", grid=(w // tn,), out_shape=_sds((nb, w), F32),
        in_specs=[pl.BlockSpec((nb, d), lambda j: (0, 0)), pl.BlockSpec((d, tn), lambda j: (0, j)),
                  pl.BlockSpec((1, tn), lambda j: (0, j))],
        out_specs=pl.BlockSpec((nb, tn), lambda j: (0, j)),
        compiler_params=_params("parallel"),
    )(c_all, w_ada, b_shard)


def _ada_weight_grad(c_all, dmod_cols, deps=()):
    nb, d = c_all.shape
    w = dmod_cols.shape[1]
    tn = _tile(w, 512)

    def body(c_ref, g_ref, o_ref):
        cv = c_ref[...]
        sc = cv * jax.nn.sigmoid(cv)
        o_ref[...] = lax.dot_general(sc, g_ref[...], _TN, precision=lax.Precision.HIGHEST,
                                     preferred_element_type=F32)

    return pl.pallas_call(
        _with_deps(body, 2, deps), name="ada_wgrad", grid=(w // tn,), out_shape=_sds((d, w), F32),
        in_specs=[pl.BlockSpec((nb, d), lambda j: (0, 0)), pl.BlockSpec((nb, tn), lambda j: (0, j))]
        + [_ANY] * len(deps),
        out_specs=pl.BlockSpec((d, tn), lambda j: (0, j)),
        compiler_params=_params("parallel"),
    )(c_all, dmod_cols, *deps)


def _norm_forward(name, x, norm_w, scale, shift, deps=()):
    s, d = x.shape
    tm = _tile(s, 256)

    def body(x_ref, w_ref, sc_ref, sh_ref, h_ref):
        xv = x_ref[...]
        r = lax.rsqrt(jnp.mean(xv * xv, axis=-1, keepdims=True) + NORM_EPS)
        h = (xv * r * w_ref[...]) * (1.0 + sc_ref[...]) + sh_ref[...]
        h_ref[...] = h.astype(BF16)

    vec = pl.BlockSpec((1, d), lambda i: (0, 0))
    row = pl.BlockSpec((tm, d), lambda i: (i, 0))
    return pl.pallas_call(
        _with_deps(body, 4, deps), name=name, grid=(s // tm,), out_shape=_sds((s, d), BF16),
        in_specs=[row, vec, vec, vec] + [_ANY] * len(deps), out_specs=row, compiler_params=_params("parallel"),
    )(x, norm_w, scale, shift, *deps)


def _norm_backward(name, dh, x, norm_w, scale, dres, gated=None, deps=()):
    s, d = x.shape
    tm = _tile(s, 256)
    n_in = 7 if gated else 5

    def body(*refs):
        dh_ref, x_ref, w_ref, sc_ref, dres_ref = refs[:5]
        dx_ref, dshift_ref, dscale_ref, dw_ref = refs[n_in:n_in + 4]
        sums = (dshift_ref, dscale_ref, dw_ref) + ((refs[n_in + 5],) if gated else ())

        @pl.when(pl.program_id(0) == 0)
        def _():
            for ref in sums:
                ref[...] = jnp.zeros_like(ref)

        xv = x_ref[...]
        g = dh_ref[...]
        r = lax.rsqrt(jnp.mean(xv * xv, axis=-1, keepdims=True) + NORM_EPS)
        n = xv * r
        gain = 1.0 + sc_ref[...]
        gn = g * n
        dshift_ref[...] += jnp.sum(g, axis=0, keepdims=True)
        dscale_ref[...] += jnp.sum(gn, axis=0, keepdims=True) * w_ref[...]
        dw_ref[...] += jnp.sum(gn, axis=0, keepdims=True) * gain
        dn = g * (w_ref[...] * gain)
        dx = dres_ref[...] + r * (dn - n * jnp.mean(dn * n, axis=-1, keepdims=True))
        dx_ref[...] = dx
        if gated:
            gate_ref, other_ref = refs[5:7]
            refs[n_in + 4][...] = (dx * gate_ref[...]).astype(BF16)
            refs[n_in + 5][...] += jnp.sum(dx * other_ref[...].astype(F32), axis=0, keepdims=True)

    vec = pl.BlockSpec((1, d), lambda i: (0, 0))
    row = pl.BlockSpec((tm, d), lambda i: (i, 0))
    vec_out = _sds((1, d), F32)
    return pl.pallas_call(
        _with_deps(body, n_in, deps), name=name, grid=(s // tm,),
        out_shape=(_sds((s, d), F32), vec_out, vec_out, vec_out) + ((_sds((s, d), BF16), vec_out) if gated else ()),
        in_specs=[row, row, vec, vec, row] + ([vec, row] if gated else []) + [_ANY] * len(deps),
        out_specs=(row, vec, vec, vec) + ((row, vec) if gated else ()),
        compiler_params=_params("arbitrary"),
    )(dh, x, norm_w, scale, dres, *(gated or ()), *deps)


def _split_bf16(v):
    hi = v.astype(BF16)
    lo = (v - hi.astype(F32)).astype(BF16)
    return hi, lo


def _pool_forward(proj, w_pool, pool_scale, deps=()):
    s = proj.shape[0]
    g_n, cg, _ = w_pool.shape
    t = POOL_TILE
    nt = s // t

    def body(cur_ref, prev_ref, wp_ref, sc_ref, pooled_ref, ya_ref):
        g = pl.program_id(0)
        ti = pl.program_id(1)
        win = jnp.left_shift(2, g)
        row = lax.broadcasted_iota(jnp.int32, (t, t), 0)
        col = lax.broadcasted_iota(jnp.int32, (t, t), 1)
        lag = row - col
        band_cur = ((lag >= 0) & (lag < win)).astype(BF16)
        band_prev = ((lag + t < win) & (ti > 0)).astype(BF16)
        u = cur_ref[...]
        u_hi, u_lo = _split_bf16(u)
        p_hi, p_lo = _split_bf16(prev_ref[...])
        wsum = (_dot(band_cur, u_hi) + _dot(band_cur, u_lo)
                + _dot(band_prev, p_hi) + _dot(band_prev, p_lo))
        tok = ti * t + lax.broadcasted_iota(jnp.int32, (t, 1), 0)
        count = jnp.minimum(tok + 1, win).astype(F32)
        pooled = (wsum / count - u).astype(BF16)
        pooled_ref[...] = pooled
        ya_ref[...] = (_dot(pooled, wp_ref[...]) * sc_ref[...]).astype(BF16)

    blk = pl.BlockSpec((t, cg), lambda g, i: (i, g))
    return pl.pallas_call(
        _with_deps(body, 4, deps), name="pool_fwd", grid=(g_n, nt),
        out_shape=(_sds((s, g_n * cg), BF16), _sds((s, g_n * cg), BF16)),
        in_specs=[blk, pl.BlockSpec((t, cg), lambda g, i: (jnp.maximum(i - 1, 0), g)),
                  pl.BlockSpec((None, cg, cg), lambda g, i: (g, 0, 0)),
                  pl.BlockSpec((1, cg), lambda g, i: (0, g))] + [_ANY] * len(deps),
        out_specs=(blk, blk), compiler_params=_params("parallel", "parallel"),
    )(proj, proj, w_pool, pool_scale, *deps)


def _pool_backward(dya, pooled, w_pool, pool_scale, dproj):
    s = dya.shape[0]
    g_n, cg, _ = w_pool.shape
    t = POOL_TILE
    nt = s // t

    def body(dya_ref, dya_next_ref, pooled_ref, wp_ref, sc_ref, dproj_in, du_ref, gw_ref, gs_ref):
        del dproj_in
        g = pl.program_id(0)
        ti = pl.program_id(1)

        @pl.when(ti == 0)
        def _():
            gw_ref[...] = jnp.zeros_like(gw_ref)
            gs_ref[...] = jnp.zeros_like(gs_ref)

        win = jnp.left_shift(2, g)
        wp = wp_ref[...]
        sc = sc_ref[...]
        pooled_v = pooled_ref[...]
        dya_v = dya_ref[...].astype(F32)
        mixed = _dot(pooled_v, wp)
        gs_ref[...] += jnp.sum(dya_v * mixed, axis=0, keepdims=True)
        dmixed = (dya_v * sc).astype(BF16)
        gw_ref[...] += _dot(pooled_v, dmixed, "tn")
        dpooled = _dot(dmixed, wp, "nt")
        dmixed_next = (dya_next_ref[...].astype(F32) * sc).astype(BF16)
        dpooled_next = _dot(dmixed_next, wp, "nt")
        tok = ti * t + lax.broadcasted_iota(jnp.int32, (t, 1), 0)
        e_cur = dpooled / jnp.minimum(tok + 1, win).astype(F32)
        e_next = dpooled_next / jnp.minimum(tok + t + 1, win).astype(F32)
        row = lax.broadcasted_iota(jnp.int32, (t, t), 0)
        col = lax.broadcasted_iota(jnp.int32, (t, t), 1)
        lead = col - row
        band_cur = ((lead >= 0) & (lead < win)).astype(BF16)
        band_next = ((lead + t < win) & (ti < nt - 1)).astype(BF16)
        c_hi, c_lo = _split_bf16(e_cur)
        n_hi, n_lo = _split_bf16(e_next)
        du = (_dot(band_cur, c_hi) + _dot(band_cur, c_lo)
              + _dot(band_next, n_hi) + _dot(band_next, n_lo)) - dpooled
        du_ref[...] = du.astype(BF16)

    blk = pl.BlockSpec((t, cg), lambda g, i: (i, g))
    du, gw, gs = pl.pallas_call(
        body, name="pool_bwd", grid=(g_n, nt),
        out_shape=(_sds(dproj.shape, BF16), _sds((g_n, cg, cg), F32), _sds((1, g_n * cg), F32)),
        in_specs=[blk, pl.BlockSpec((t, cg), lambda g, i: (jnp.minimum(i + 1, nt - 1), g)), blk,
                  pl.BlockSpec((None, cg, cg), lambda g, i: (g, 0, 0)),
                  pl.BlockSpec((1, cg), lambda g, i: (0, g)),
                  pl.BlockSpec(memory_space=pl.ANY)],
        out_specs=(blk, pl.BlockSpec((None, cg, cg), lambda g, i: (g, 0, 0)),
                   pl.BlockSpec((1, cg), lambda g, i: (0, g))),
        input_output_aliases={5: 0}, compiler_params=_params("parallel", "arbitrary"),
    )(dya, dya, pooled, w_pool, pool_scale, dproj)
    return du, gw, gs


def _qkv_prepare(proj, q_norm_w, k_norm_w, width, deps=()):
    s = proj.shape[0]
    tm = _tile(s, 256)
    heads = width // HEAD_DIM

    def body(q_ref, k_ref, v_ref, qw_ref, kw_ref, qn_ref, kn_ref, vb_ref):
        for h in range(heads):
            cols = slice(h * HEAD_DIM, (h + 1) * HEAD_DIM)
            for src, w_ref, dst, gain in ((q_ref, qw_ref, qn_ref, ATTN_SCALE), (k_ref, kw_ref, kn_ref, 1.0)):
                v = src[:, cols]
                r = lax.rsqrt(jnp.mean(v * v, axis=-1, keepdims=True) + NORM_EPS)
                dst[:, cols] = (v * (r * gain) * w_ref[...]).astype(BF16)
        vb_ref[...] = v_ref[...].astype(BF16)

    vec = pl.BlockSpec((1, HEAD_DIM), lambda i: (0, 0))
    out_spec = pl.BlockSpec((tm, width), lambda i: (i, 0))
    return pl.pallas_call(
        _with_deps(body, 5, deps), name="qkv_prep", grid=(s // tm,),
        out_shape=(_sds((s, width), BF16),) * 3,
        in_specs=[pl.BlockSpec((tm, width), lambda i: (i, 1)), pl.BlockSpec((tm, width), lambda i: (i, 2)),
                  pl.BlockSpec((tm, width), lambda i: (i, 3)), vec, vec] + [_ANY] * len(deps),
        out_specs=(out_spec,) * 3, compiler_params=_params("parallel"),
    )(proj, proj, proj, q_norm_w, k_norm_w, *deps)


def _qk_norm_backward(name, dn, proj, col_block, norm_w, dproj, width, deps=()):
    s = proj.shape[0]
    tm = _tile(s, 256)
    heads = width // HEAD_DIM

    def body(dn_ref, q_ref, w_ref, dproj_in, dq_ref, gw_ref):
        del dproj_in

        @pl.when(pl.program_id(0) == 0)
        def _():
            gw_ref[...] = jnp.zeros_like(gw_ref)

        wv = w_ref[...]
        gw = jnp.zeros((1, HEAD_DIM), F32)
        for h in range(heads):
            cols = slice(h * HEAD_DIM, (h + 1) * HEAD_DIM)
            v = q_ref[:, cols]
            g = dn_ref[:, cols]
            r = lax.rsqrt(jnp.mean(v * v, axis=-1, keepdims=True) + NORM_EPS)
            n = v * r
            gw = gw + jnp.sum(g * n, axis=0, keepdims=True)
            gn = g * wv
            dq_ref[:, cols] = (r * (gn - n * jnp.mean(gn * n, axis=-1, keepdims=True))).astype(BF16)
        gw_ref[...] += gw

    blk = pl.BlockSpec((tm, width), lambda i: (i, col_block))
    return pl.pallas_call(
        _with_deps(body, 4, deps), name=name, grid=(s // tm,),
        out_shape=(_sds(dproj.shape, BF16), _sds((1, HEAD_DIM), F32)),
        in_specs=[pl.BlockSpec((tm, width), lambda i: (i, 0)), blk,
                  pl.BlockSpec((1, HEAD_DIM), lambda i: (0, 0)), pl.BlockSpec(memory_space=pl.ANY)]
        + [_ANY] * len(deps),
        out_specs=(blk, pl.BlockSpec((1, HEAD_DIM), lambda i: (0, 0))),
        input_output_aliases={3: 0}, compiler_params=_params("arbitrary"),
    )(dn, proj, norm_w, dproj, *deps)


def _strict_upper(n):
    row = lax.broadcasted_iota(jnp.int32, (n, n), 0)
    col = lax.broadcasted_iota(jnp.int32, (n, n), 1)
    return (row > col).astype(BF16)


def _strict_lower(n):
    row = lax.broadcasted_iota(jnp.int32, (n, n), 0)
    col = lax.broadcasted_iota(jnp.int32, (n, n), 1)
    return (row < col).astype(BF16)


def _cumulate(v, tri):
    return _dot(v.astype(BF16), tri)


def _log_sigmoid(z):
    return jnp.minimum(z, 0.0) - jnp.log(1.0 + jnp.exp(-jnp.abs(z)))


def _attention_forward(qn, kn, vb, deps=()):
    s, width = qn.shape
    heads = width // HEAD_DIM
    tq, tk = Q_TILE, K_TILE
    hp = min(HEADS_PER_STEP, heads)
    assert tq == tk and s % tq == 0 and heads % hp == 0

    def body(q_ref, k_ref, v_ref, o_ref, a_scr):
        qi = pl.program_id(1)
        upper = _strict_upper(tk)
        causal = lax.broadcasted_iota(jnp.int32, (tq, tk), 1) < lax.broadcasted_iota(jnp.int32, (tq, tk), 0)
        head_cols = [slice(u * HEAD_DIM, (u + 1) * HEAD_DIM) for u in range(hp)]

        def weights(kb, carry, masked):
            rows = pl.ds(pl.multiple_of(kb * tk, tk), tk)
            out = []
            for u, cols in enumerate(head_cols):
                later = carry[u]
                z = _dot(q_ref[:, cols], k_ref[rows, cols], "nt")
                log_beta = _log_sigmoid(z)
                l = log_beta - z
                if masked:
                    l = jnp.where(causal, l, 0.0)
                a = jnp.exp(log_beta + _cumulate(l, upper) + later)
                if masked:
                    a = jnp.where(causal, a, 0.0)
                a_scr[u, :, rows] = a.astype(BF16)
                out.append(later + jnp.sum(l, axis=1, keepdims=True))
            return tuple(out)

        later = weights(qi, tuple(jnp.zeros((tq, 1), F32) for _ in range(hp)), True)
        lax.fori_loop(0, qi, lambda i, c: weights(qi - 1 - i, c, False), later)

        def mix(kb, accs):
            rows = pl.ds(pl.multiple_of(kb * tk, tk), tk)
            return tuple(acc + _dot(a_scr[u, :, rows], v_ref[rows, cols])
                         for u, (acc, cols) in enumerate(zip(accs, head_cols)))

        accs = lax.fori_loop(0, qi + 1, mix, tuple(jnp.zeros((tq, HEAD_DIM), F32) for _ in range(hp)))
        for acc, cols in zip(accs, head_cols):
            o_ref[:, cols] = acc.astype(BF16)

    full = pl.BlockSpec((s, hp * HEAD_DIM), lambda h, i: (0, h))
    blk = pl.BlockSpec((tq, hp * HEAD_DIM), lambda h, i: (i, h))
    return pl.pallas_call(
        _with_deps(body, 3, deps), name="attn_fwd", grid=(heads // hp, s // tq), out_shape=_sds((s, width), BF16),
        in_specs=[blk, full, full] + [_ANY] * len(deps), out_specs=blk,
        scratch_shapes=[pltpu.VMEM((hp, tq, s), BF16)],
        compiler_params=_params("parallel", "parallel"),
    )(qn, kn, vb, *deps)


def _attention_backward(qn, kn, vb, dout, dproj, v_col_block, deps=()):
    s, width = qn.shape
    heads = width // HEAD_DIM
    tq, tk = Q_TILE, K_TILE
    hp = min(HEADS_PER_STEP_BWD, heads)
    nq = s // tq
    v_block0 = v_col_block * (heads // hp)

    def body(q_ref, k_ref, v_ref, do_ref, dproj_in, dq_ref, dk_ref, dv_ref,
             a_scr, lb_scr, dz_scr, dkt_scr, dvt_scr):
        del dproj_in
        qi = pl.program_id(1)

        @pl.when(qi == 0)
        def _():
            dkt_scr[...] = jnp.zeros_like(dkt_scr)
            dvt_scr[...] = jnp.zeros_like(dvt_scr)

        upper = _strict_upper(tk)
        lower = _strict_lower(tk)
        causal = lax.broadcasted_iota(jnp.int32, (tq, tk), 1) < lax.broadcasted_iota(jnp.int32, (tq, tk), 0)
        head_cols = [slice(u * HEAD_DIM, (u + 1) * HEAD_DIM) for u in range(hp)]

        def weights(kb, carry, masked):
            rows = pl.ds(pl.multiple_of(kb * tk, tk), tk)
            out = []
            for u, cols in enumerate(head_cols):
                later = carry[u]
                z = _dot(q_ref[:, cols], k_ref[rows, cols], "nt")
                log_beta = _log_sigmoid(z)
                l = log_beta - z
                if masked:
                    l = jnp.where(causal, l, 0.0)
                a = jnp.exp(log_beta + _cumulate(l, upper) + later)
                if masked:
                    a = jnp.where(causal, a, 0.0)
                a_scr[u, :, rows] = a
                lb_scr[u, :, rows] = log_beta
                out.append(later + jnp.sum(l, axis=1, keepdims=True))
            return tuple(out)

        zeros = tuple(jnp.zeros((tq, 1), F32) for _ in range(hp))
        later = weights(qi, zeros, True)
        lax.fori_loop(0, qi, lambda i, c: weights(qi - 1 - i, c, False), later)

        q_t = [jnp.transpose(q_ref[:, cols].astype(F32)).astype(BF16) for cols in head_cols]
        do_t = [jnp.transpose(do_ref[:, cols].astype(F32)).astype(BF16) for cols in head_cols]

        def scores(kb, carry, masked):
            rows = pl.ds(pl.multiple_of(kb * tk, tk), tk)
            out = []
            for u, cols in enumerate(head_cols):
                before = carry[u]
                beta = jnp.exp(lb_scr[u, :, rows])
                g = a_scr[u, :, rows] * _dot(do_ref[:, cols], v_ref[rows, cols], "nt")
                p = _cumulate(g, lower) + before
                dz = g - (g + p) * beta
                if masked:
                    dz = jnp.where(causal, dz, 0.0)
                dz_scr[u, :, rows] = dz.astype(BF16)
                out.append(before + jnp.sum(g, axis=1, keepdims=True))
            return tuple(out)

        before = lax.fori_loop(0, qi, lambda i, c: scores(i, c, False), zeros)
        scores(qi, before, True)

        def products(kb, dqs):
            rows = pl.ds(pl.multiple_of(kb * tk, tk), tk)
            out = []
            for u, cols in enumerate(head_cols):
                dz = dz_scr[u, :, rows]
                dkt_scr[cols, rows] += _dot(q_t[u], dz)
                dvt_scr[cols, rows] += _dot(do_t[u], a_scr[u, :, rows].astype(BF16))
                out.append(dqs[u] + _dot(dz, k_ref[rows, cols]))
            return tuple(out)

        dqs = lax.fori_loop(0, qi + 1, products, tuple(jnp.zeros((tq, HEAD_DIM), F32) for _ in range(hp)))
        for u, cols in enumerate(head_cols):
            dq_ref[:, cols] = dqs[u] * ATTN_SCALE

        @pl.when(qi == nq - 1)
        def _():
            dk_ref[...] = jnp.transpose(dkt_scr[...])
            dv_ref[...] = jnp.transpose(dvt_scr[...]).astype(BF16)

    wide = hp * HEAD_DIM
    full = pl.BlockSpec((s, wide), lambda h, i: (0, h))
    blk = pl.BlockSpec((tq, wide), lambda h, i: (i, h))
    return pl.pallas_call(
        _with_deps(body, 5, deps), name="attn_bwd", grid=(heads // hp, nq),
        out_shape=(_sds((s, width), F32), _sds((s, width), F32), _sds(dproj.shape, BF16)),
        in_specs=[blk, full, full, blk, pl.BlockSpec(memory_space=pl.ANY)] + [_ANY] * len(deps),
        out_specs=(blk, full, pl.BlockSpec((s, wide), lambda h, i: (0, v_block0 + h))),
        scratch_shapes=[pltpu.VMEM((hp, tq, s), F32), pltpu.VMEM((hp, tq, s), F32), pltpu.VMEM((hp, tq, s), BF16),
                        pltpu.VMEM((wide, s), F32), pltpu.VMEM((wide, s), F32)],
        input_output_aliases={4: 2}, compiler_params=_params("parallel", "arbitrary"),
    )(qn, kn, vb, dout, dproj, *deps)


def _place_columns(name, src, dst, col_block):
    s, w = src.shape
    tm = _tile(s, 512)

    def body(src_ref, dst_in, out_ref):
        del dst_in
        out_ref[...] = src_ref[...]

    return pl.pallas_call(
        body, name=name, grid=(s // tm,), out_shape=_sds(dst.shape, dst.dtype),
        in_specs=[pl.BlockSpec((tm, w), lambda i: (i, 0)), pl.BlockSpec(memory_space=pl.ANY)],
        out_specs=pl.BlockSpec((tm, w), lambda i: (i, col_block)),
        input_output_aliases={1: 0}, compiler_params=_params("parallel"),
    )(src, dst)


def _cast_into_slot(name, x, slot):
    r, c = x.shape
    tr = _tile(r, max(SUBLANES * 2, (1 << 20) // c), SUBLANES * 2)

    def body(slot_ref, x_ref, o_ref):
        del slot_ref
        o_ref[...] = x_ref[...].astype(BF16)

    grid_spec = pltpu.PrefetchScalarGridSpec(
        num_scalar_prefetch=1, grid=(r // tr,),
        in_specs=[pl.BlockSpec((tr, c), lambda i, slot_ref: (i, 0))],
        out_specs=pl.BlockSpec((None, tr, c), lambda i, slot_ref: (slot_ref[0], i, 0)))
    return pl.pallas_call(
        body, name=name, grid_spec=grid_spec, out_shape=_sds((N_DEV, r, c), BF16),
        compiler_params=_params("parallel"),
    )(slot, x)


def _adamw_update(gv, w_ref, m_ref, v_ref, d_ref, nm_ref, nv_ref):
    c1 = 1.0 - ADAM_B1 ** ADAM_STEP
    c2 = 1.0 - ADAM_B2 ** ADAM_STEP
    nm = ADAM_B1 * m_ref[...] + (1.0 - ADAM_B1) * gv
    nv = ADAM_B2 * v_ref[...] + (1.0 - ADAM_B2) * (gv * gv)
    d_ref[...] = -ADAM_LR * ((nm / c1) / (jnp.sqrt(nv / c2) + ADAM_EPS) + ADAM_WD * w_ref[...])
    nm_ref[...] = nm
    nv_ref[...] = nv


def _adamw(name, w, g, m, v, deps=()):
    r, c = w.shape
    tr = _tile(r, max(SUBLANES, (1 << 19) // c))

    def body(w_ref, g_ref, m_ref, v_ref, d_ref, nm_ref, nv_ref):
        _adamw_update(g_ref[...], w_ref, m_ref, v_ref, d_ref, nm_ref, nv_ref)

    blk = pl.BlockSpec((tr, c), lambda i: (i, 0))
    return pl.pallas_call(
        _with_deps(body, 4, deps), name=name, grid=(r // tr,), out_shape=(_sds((r, c), F32),) * 3,
        in_specs=[blk] * 4 + [_ANY] * len(deps), out_specs=(blk,) * 3, compiler_params=_params("parallel"),
    )(w, g, m, v, *deps)


def _adamw_summed(name, w, own, received, m, v, deps=()):
    r, c = w.shape
    nj = received.shape[0]
    tr = _tile(r, max(2 * SUBLANES, (1 << 19) // c), 2 * SUBLANES)

    def body(w_ref, own_ref, rec_ref, m_ref, v_ref, g_ref, d_ref, nm_ref, nv_ref):
        gv = own_ref[...].astype(F32)
        for j in range(nj):
            gv = gv + rec_ref[j].astype(F32)
        g_ref[...] = gv
        _adamw_update(gv, w_ref, m_ref, v_ref, d_ref, nm_ref, nv_ref)

    blk = pl.BlockSpec((tr, c), lambda i: (i, 0))
    return pl.pallas_call(
        _with_deps(body, 5, deps), name=name, grid=(r // tr,), out_shape=(_sds((r, c), F32),) * 4,
        in_specs=[blk, pl.BlockSpec((None, tr, c), lambda i: (0, i, 0)),
                  pl.BlockSpec((nj, tr, c), lambda i: (0, i, 0)), blk, blk] + [_ANY] * len(deps),
        out_specs=(blk,) * 4, compiler_params=_params("parallel"),
    )(w, own, received, m, v, *deps)


def _rows_of_lanes(v):
    rows = v.shape[1] // LANES
    out = v.reshape(rows, LANES)
    pad = (-rows) % SUBLANES
    if pad:
        out = jnp.pad(out, ((0, pad), (0, 0)))
    return out


def kernel(x, c, w_ada, b_ada, norm1_w, w_in, q_norm_w, k_norm_w, w_pool, pool_scale, w_a_up, w_b_up, w_o, norm2_w, w_ff1, w_ff2, loss_target, m_w_ada, m_b_ada, m_norm1_w, m_w_in, m_q_norm_w, m_k_norm_w, m_w_pool, m_pool_scale, m_w_a_up, m_w_b_up, m_w_o, m_norm2_w, m_w_ff1, m_w_ff2, v_w_ada, v_b_ada, v_norm1_w, v_w_in, v_q_norm_w, v_k_norm_w, v_w_pool, v_pool_scale, v_w_a_up, v_w_b_up, v_w_o, v_norm2_w, v_w_ff1, v_w_ff2):
    _, s, d = x.shape
    half = d // 2
    d8 = d // N_DEV
    n_groups = len(POOL_WINDOWS)
    cg = half // n_groups
    me = _group_index(MESH_AXES)

    x2 = x[0]
    target = loss_target[0]

    my_slot = jnp.reshape(me, (1,)).astype(jnp.int32)

    def cast(i, t):
        return _cast_into_slot("cast_w%d" % i, t, my_slot)

    def gather_start(tag, bufs, phase):
        if phase < 2:
            return _launch("ag%s_ici%d" % (tag, phase), bufs, _plan_gather_ici(phase), (2, 4)[phase] * len(bufs))
        return _launch("ag%s_d2d" % tag, bufs, _plan_gather_d2d, len(_CHIP_MASKS) * len(bufs))

    def gather_plans(tag, n_bufs, phase):
        if phase < 2:
            return ("ag%s_ici%d" % (tag, phase), _plan_gather_ici(phase), (2, 4)[phase] * n_bufs)
        return ("ag%s_d2d" % tag, _plan_gather_d2d, len(_CHIP_MASKS) * n_bufs)

    bx, by, bc = _AXIS_BIT["x"], _AXIS_BIT["y"], _AXIS_BIT["c"]
    buf_a = [cast(0, w_in[0])]

    c_all = _all_gather_2d("ag_c", c.reshape(d // LANES, LANES), deps=tuple(buf_a)).reshape(N_DEV, d)
    wa = w_ada.shape[2]
    b_shard = lax.dynamic_slice_in_dim(b_ada, me * wa, wa, axis=1)
    mod_part = _ada_forward(c_all, w_ada[0], b_shard)
    mod_rows = mod_part.reshape(N_DEV * wa // LANES, LANES)
    mod_buf = lax.dynamic_update_slice(jnp.zeros((N_DEV,) + mod_rows.shape, F32), mod_rows[None], (me, 0, 0))
    (fl_mod, fl_a, fl_pair), tok = _launch_groups(
        "agA_near", [([mod_buf], [("ag_mod", _plan_everyone, N_DEV - 1)]),
                     (buf_a, [("agA_near", _plan_neighbours, 2), ("agA_pair_d2d", _plan_d2d((0,)), 1)])])
    mod_all, = _land(fl_mod, tok)
    buf_b = [cast(1, w_pool[0].reshape(-1, cg)), cast(2, w_a_up[0]), cast(3, w_b_up[0]), cast(4, w_o[0])]
    buf_c = [cast(5, w_ff1[0])]
    buf_e = [cast(6, w_ff2[0])]
    mod_all = mod_all.reshape(N_DEV, N_DEV, wa)
    mod = lax.dynamic_slice_in_dim(mod_all, me, 1, axis=1).reshape(1, N_MOD * d)
    shift1, scale1, gate1, shift2, scale2, gate2 = [mod[:, i * d:(i + 1) * d] for i in range(N_MOD)]

    h = _norm_forward("norm1_fwd", x2, norm1_w, scale1, shift1)
    buf_a = _land(fl_pair, h, bufs=fl_a[-1])
    pair_slots = jnp.stack([me, me ^ bc]).astype(jnp.int32)
    proj = _project_slots("proj_pair", h, buf_a[0], pair_slots, 4 * d)
    buf_a = _land(fl_a, [proj] + buf_b + buf_c + buf_e, bufs=buf_a)
    near = (bx, by)
    (fl_far, fl_near), tok = _launch_groups(
        "agA_far", [(buf_a, [("agA_far", _plan_diagonal, 2), ("agA_near_d2d", _plan_d2d(near), len(near))])])
    (fl_b, fl_c, fl_e), tok = _launch_groups(
        "agBCE_ici0", [(buf_b, [gather_plans("B", len(buf_b), 0)]), (buf_c, [gather_plans("C", 1, 0)]),
                       (buf_e, [gather_plans("E", 1, 0)])])

    tm = _tile(s, 1024)
    tk = _tile(d, 2048)
    te = _tile(d, 512)

    own_slots = jnp.stack([me ^ m for m in near]).astype(jnp.int32)
    far_slots = jnp.stack([me ^ bx ^ by ^ f for f in (0, bc)]).astype(jnp.int32)
    buf_a = fl_near[-1]
    proj = _project_slots("proj_own", h, buf_a[0], own_slots, 4 * d, proj_in=proj, deps=(tok,))
    buf_a = _land(fl_near, proj, bufs=buf_a)
    proj = _project_slots("proj_sibling", h, buf_a[0], own_slots ^ bc, 4 * d, proj_in=proj)
    buf_a = _land(fl_far, proj, bufs=buf_a)
    fl_a, tok = _launch("agA_far_d2d", buf_a, _plan_d2d((bx | by,)), 1)
    w_in_f, = _land(fl_a, tok)
    proj = _project_slots("proj_far", h, w_in_f, far_slots, 4 * d, proj_in=proj)
    qn, kn, vb = _qkv_prepare(proj, q_norm_w, k_norm_w, half)
    buf_b = _land(fl_b, qn)
    buf_c = _land(fl_c, qn)
    (fl_b, fl_c), tok = _launch_groups(
        "agBC_ici1", [(buf_b, [gather_plans("B", len(buf_b), 1)]), (buf_c, [gather_plans("C", 1, 1)])])
    attn = _attention_forward(qn, kn, vb, deps=(tok,))
    buf_b = _land(fl_b, attn)
    buf_e = _land(fl_e, attn)
    (fl_b, fl_e), tok = _launch_groups(
        "agB_d2d_E_ici1", [(buf_b, [gather_plans("B", len(buf_b), 2)]), (buf_e, [gather_plans("E", 1, 1)])])
    w_pool_f, w_a_f, w_b_f, w_o_f = _land(fl_b, tok)
    rows_pool = cg // N_DEV
    w_pool_f = w_pool_f.reshape(N_DEV, n_groups, rows_pool, cg).transpose(1, 0, 2, 3).reshape(n_groups, cg, cg)
    w_o_f = w_o_f.reshape(d, d)
    pooled, ya_in = _pool_forward(proj, w_pool_f, pool_scale)

    def merge_epilogue(ga_ref, gb_ref, ya, yb, out_refs):
        merged_ref, ya_ref, yb_ref = out_refs
        merged = jax.nn.sigmoid(ga_ref[...]) * ya + jax.nn.sigmoid(gb_ref[...]) * yb
        merged_ref[...] = merged.astype(BF16)
        ya_ref[...] = ya.astype(BF16)
        yb_ref[...] = yb.astype(BF16)

    def up_body(a1_ref, b1_ref, a2_ref, b2_ref, ga_ref, gb_ref, *out_refs):
        merge_epilogue(ga_ref, gb_ref, _dot(a1_ref[...], b1_ref[...]), _dot(a2_ref[...], b2_ref[...]), out_refs)

    ga_blk0 = 2 * d // d8
    gb_blk0 = 3 * d // d8
    tu = s
    a_spec = pl.BlockSpec((tu, half), lambda i, j: (i, 0))
    wup_spec = pl.BlockSpec((None, half, d8), lambda i, j: (j, 0, 0))
    o_blk = pl.BlockSpec((tu, d8), lambda i, j: (i, j))
    merged, y_a, y_b = pl.pallas_call(
        up_body, name="up_merge", grid=(s // tu, N_DEV), out_shape=(_sds((s, d), BF16),) * 3,
        in_specs=[a_spec, wup_spec, a_spec, wup_spec,
                  pl.BlockSpec((tu, d8), lambda i, j: (i, ga_blk0 + j)),
                  pl.BlockSpec((tu, d8), lambda i, j: (i, gb_blk0 + j))],
        out_specs=(o_blk,) * 3, compiler_params=_params("parallel", "parallel"),
    )(ya_in, w_a_f, attn, w_b_f, proj, proj)
    buf_c = _land(fl_c, merged)
    fl_c, tok_c = gather_start("C", buf_c, 2)

    tn = _tile(d, 1024)

    def oproj_epilogue(acc, extra_refs, out_refs):
        x_ref, g_ref = extra_refs
        x1_ref, o_ref = out_refs
        x1_ref[...] = x_ref[...] + g_ref[...] * acc
        o_ref[...] = acc.astype(BF16)

    mn_blk = pl.BlockSpec((tm, tn), lambda i, j, k: (i, j))
    e_blk = pl.BlockSpec((tm, te), lambda i, j, k: (i, j))
    e_vec = pl.BlockSpec((1, te), lambda i, j, k: (0, j))
    s_blk = pl.BlockSpec((s, te), lambda i, j, k: (i, j))
    x1, o_act = _matmul(
        "oproj", "nn", (1, d // te, d // tk), merged, pl.BlockSpec((s, tk), lambda i, j, k: (i, k)),
        w_o_f, pl.BlockSpec((tk, te), lambda i, j, k: (k, j)),
        [_sds((s, d), F32), _sds((s, d), BF16)], [s_blk, s_blk], (s, te),
        epilogue=oproj_epilogue, extras=(x2, gate1), extra_specs=(s_blk, e_vec), deps=(tok_c,))

    h2 = _norm_forward("norm2_fwd", x1, norm2_w, scale2, shift2)
    w_ff1_f, = _land(fl_c, h2)

    def ff1_epilogue(acc, extra_refs, out_refs):
        r = jnp.maximum(acc, 0.0)
        out_refs[0][...] = r.astype(BF16)
        out_refs[1][...] = (r * r).astype(BF16)

    n_rows = s // tm
    first = max(n_rows // 2, 1)

    def ff1_rows(name, row0, rows, prior, deps):
        blk = pl.BlockSpec((tm, half), lambda i, j, k: (i + row0, j))
        return _matmul(
            name, "nn", (rows, N_DEV, d // tk), h2, pl.BlockSpec((tm, tk), lambda i, j, k: (i + row0, k)),
            w_ff1_f, pl.BlockSpec((None, tk, half), lambda i, j, k: (j, k, 0)),
            [_sds((s, 4 * d), BF16)] * 2, [blk, blk], (tm, half), epilogue=ff1_epilogue,
            extras=prior, extra_specs=[_ANY] * len(prior), aliases={2 + n: n for n in range(len(prior))}, deps=deps)

    relu, act = ff1_rows("ff1_a", 0, first, (), ())
    if n_rows > first:
        buf_e = _land(fl_e, act)
        fl_e, tok_e = gather_start("E", buf_e, 2)
        relu, act = ff1_rows("ff1_b", first, n_rows - first, (relu, act), (tok_e,))
    else:
        fl_e, tok_e = gather_start("E", _land(fl_e, act), 2)
    w_ff2_f, = _land(fl_e, act if n_rows > first else tok_e)
    w_ff2_f = w_ff2_f.reshape(4 * d, d)

    def ff2_epilogue(acc, extra_refs, out_refs):
        x1_ref, g_ref, t_ref = extra_refs
        df_ref, dy_ref, sq_ref, dgate_ref = out_refs
        gate = g_ref[...]
        err = x1_ref[...] + gate * acc - t_ref[...]
        dyv = err * (1.0 / d)
        dy_ref[...] = dyv
        df_ref[...] = (dyv * gate).astype(BF16)
        sq_ref[...] = jnp.full(sq_ref.shape, jnp.sum(err * err), F32)
        dgate_ref[...] = jnp.broadcast_to(jnp.sum(dyv * acc, axis=0, keepdims=True), dgate_ref.shape)

    df, dy, sq, dgate2_parts = _matmul(
        "ff2", "nn", (s // tm, d // te, 2 * d // tk), act, pl.BlockSpec((tm, 2 * tk), lambda i, j, k: (i, k)),
        w_ff2_f, pl.BlockSpec((2 * tk, te), lambda i, j, k: (k, j)),
        [_sds((s, d), BF16), _sds((s, d), F32), _sds((s // tm * SUBLANES, d // te * LANES), F32),
         _sds((s // tm * SUBLANES, d), F32)],
        [e_blk, e_blk, pl.BlockSpec((SUBLANES, LANES), lambda i, j, k: (i, j)),
         pl.BlockSpec((SUBLANES, te), lambda i, j, k: (i, j))], (tm, te),
        epilogue=ff2_epilogue, extras=(x1, gate2, target), extra_specs=(e_blk, e_vec, e_blk))
    loss_local = (0.5 / d) * jnp.sum(sq[::SUBLANES, ::LANES])
    dgate2 = jnp.sum(dgate2_parts[::SUBLANES], axis=0, keepdims=True)

    tok_k = _tile(s, 2048)
    tw = _tile(d, 1024)
    g_ff2 = _matmul(
        "g_ff2", "tn", (4 * d // tw, d // tn, s // tok_k), act, pl.BlockSpec((tok_k, tw), lambda i, j, k: (k, i)),
        df, pl.BlockSpec((tok_k, tn), lambda i, j, k: (k, j)),
        [_sds((4 * d, d), BF16)], [pl.BlockSpec((tw, tn), lambda i, j, k: (i, j))], (tw, tn))[0]

    def da_epilogue(acc, extra_refs, out_refs):
        out_refs[0][...] = (acc * (2.0 * extra_refs[0][...].astype(F32))).astype(BF16)

    big_blk = pl.BlockSpec((tm, tn), lambda i, j, k: (i, j))
    fl_f2, tok = _reduce_scatter_start("F2", [g_ff2.reshape(N_DEV, half, d)])
    df1 = _matmul(
        "da_ff", "nt", (s // tm, 4 * d // tn, d // tk), df, pl.BlockSpec((tm, tk), lambda i, j, k: (i, k)),
        w_ff2_f, pl.BlockSpec((tn, tk), lambda i, j, k: (j, k)),
        [_sds((s, 4 * d), BF16)], [big_blk], (tm, tn),
        epilogue=da_epilogue, extras=(relu,), extra_specs=(big_blk,), deps=(tok,))[0]

    fl_f2, tok = _reduce_scatter_middle("F2", fl_f2, df1, me)
    g_ff1 = _matmul(
        "g_ff1", "tn", (d // tw, N_DEV, s // tok_k), h2, pl.BlockSpec((tok_k, tw), lambda i, j, k: (k, i)),
        df1, pl.BlockSpec((tok_k, half), lambda i, j, k: (k, j)),
        [_sds((N_DEV, d, half), BF16)], [pl.BlockSpec((None, tw, half), lambda i, j, k: (j, i, 0))], (tw, half),
        deps=(tok,))[0]

    fl_f1, tok = _reduce_scatter_start("F1", [g_ff1])
    dh2 = _matmul(
        "dh2", "nt", (s // tm, d // tn, N_DEV // 2), df1, pl.BlockSpec((tm, 2 * half), lambda i, j, k: (i, k)),
        w_ff1_f, pl.BlockSpec((2, tn, half), lambda i, j, k: (k, j, 0)),
        [_sds((s, d), F32)], [mn_blk], (tm, tn), deps=(tok,))[0]

    fl_f1, tok = _reduce_scatter_middle("F1", fl_f1, dh2, me)
    dx1, dshift2, dscale2, g_norm2, do, dgate1 = _norm_backward(
        "norm2_bwd", dh2, x1, norm2_w, scale2, dy, gated=(gate1, o_act), deps=(tok,))
    sum_ff2, = _reduce_scatter_finish(fl_f2, do)

    g_o = _matmul(
        "g_o", "tn", (d // tw, d // tn, s // tok_k), merged, pl.BlockSpec((tok_k, tw), lambda i, j, k: (k, i)),
        do, pl.BlockSpec((tok_k, tn), lambda i, j, k: (k, j)),
        [_sds((d, d), BF16)], [pl.BlockSpec((tw, tn), lambda i, j, k: (i, j))], (tw, tn))[0]

    def merge_bwd_epilogue(acc, extra_refs, out_refs):
        ga_ref, gb_ref, ya_ref, yb_ref = extra_refs
        dya_ref, dyb_ref, dga_ref, dgb_ref = out_refs
        sa = jax.nn.sigmoid(ga_ref[...])
        sb = jax.nn.sigmoid(gb_ref[...])
        dya_ref[...] = (acc * sa).astype(BF16)
        dyb_ref[...] = (acc * sb).astype(BF16)
        dga_ref[...] = (acc * ya_ref[...].astype(F32) * (sa * (1.0 - sa))).astype(BF16)
        dgb_ref[...] = (acc * yb_ref[...].astype(F32) * (sb * (1.0 - sb))).astype(BF16)

    td = _tile(d, 256)
    nb = d // td
    d_blk = pl.BlockSpec((s, td), lambda i, j, k: (i, j))
    dy_a, dy_b, dproj, dg_b = _matmul(
        "dmerged", "nt", (1, nb, d // tk), do, pl.BlockSpec((s, tk), lambda i, j, k: (i, k)),
        w_o_f, pl.BlockSpec((td, tk), lambda i, j, k: (j, k)),
        [_sds((s, d), BF16), _sds((s, d), BF16), _sds((s, 4 * d), BF16), _sds((s, d), BF16)],
        [d_blk, d_blk, pl.BlockSpec((s, td), lambda i, j, k: (i, 2 * nb + j)), d_blk], (s, td),
        epilogue=merge_bwd_epilogue, extras=(proj, proj, y_a, y_b),
        extra_specs=(pl.BlockSpec((s, td), lambda i, j, k: (i, 2 * nb + j)),
                     pl.BlockSpec((s, td), lambda i, j, k: (i, 3 * nb + j)), d_blk, d_blk))
    dproj = _place_columns("place_dgb", dg_b, dproj, 3)

    up_a = pl.BlockSpec((tok_k, half), lambda i, j, k: (k, 0))
    up_b = pl.BlockSpec((tok_k, d8), lambda i, j, k: (k, j))
    up_o = pl.BlockSpec((None, half, d8), lambda i, j, k: (j, 0, 0))
    g_a_up = _matmul("g_a_up", "tn", (1, N_DEV, s // tok_k), ya_in, up_a, dy_a, up_b,
                     [_sds((N_DEV, half, d8), BF16)], [up_o], (half, d8))[0]
    g_b_up = _matmul("g_b_up", "tn", (1, N_DEV, s // tok_k), attn, up_a, dy_b, up_b,
                     [_sds((N_DEV, half, d8), BF16)], [up_o], (half, d8))[0]
    slabs = 4
    dn_a = pl.BlockSpec((tm, slabs * d8), lambda i, j, k: (i, k))
    dn_b = pl.BlockSpec((slabs, half, d8), lambda i, j, k: (k, 0, 0))
    dn_o = pl.BlockSpec((tm, half), lambda i, j, k: (i, 0))
    dya_in = _matmul("d_ya_in", "nt", (s // tm, 1, N_DEV // slabs), dy_a, dn_a, w_a_f, dn_b,
                     [_sds((s, half), BF16)], [dn_o], (tm, half))[0]
    dattn = _matmul("d_attn", "nt", (s // tm, 1, N_DEV // slabs), dy_b, dn_a, w_b_f, dn_b,
                    [_sds((s, half), BF16)], [dn_o], (tm, half))[0]

    dproj, g_pool, g_pool_scale = _pool_backward(dya_in, pooled, w_pool_f, pool_scale, dproj)
    sum_ff1, = _reduce_scatter_finish(fl_f1, g_pool)
    g_pool_send = g_pool.astype(BF16).reshape(n_groups, N_DEV, rows_pool, cg).transpose(1, 0, 2, 3)
    g_pool_send = g_pool_send.reshape(N_DEV, n_groups * rows_pool, cg)
    fl_b, tok = _reduce_scatter_start("B", [g_pool_send, g_a_up, g_b_up, g_o.reshape(N_DEV, d8, d)])
    dqn, dkn, dproj = _attention_backward(qn, kn, vb, dattn, dproj, 3, deps=(tok,))
    fl_b, tok = _reduce_scatter_middle("B", fl_b, dqn, me)
    dproj, g_qnorm = _qk_norm_backward("qnorm_bwd", dqn, proj, 1, q_norm_w, dproj, half, deps=(tok,))
    dproj, g_knorm = _qk_norm_backward("knorm_bwd", dkn, proj, 2, k_norm_w, dproj, half)

    g_in = _matmul(
        "g_in", "tn", (d // tw, N_DEV, s // tok_k), h, pl.BlockSpec((tok_k, tw), lambda i, j, k: (k, i)),
        dproj, pl.BlockSpec((tok_k, half), lambda i, j, k: (k, j)),
        [_sds((N_DEV, d, half), BF16)], [pl.BlockSpec((None, tw, half), lambda i, j, k: (j, i, 0))], (tw, half))[0]
    fl_in, tok = _reduce_scatter_start("I", [g_in])
    dh = _matmul(
        "dh", "nt", (s // tm, d // tn, N_DEV // 2), dproj, pl.BlockSpec((tm, 2 * half), lambda i, j, k: (i, k)),
        w_in_f, pl.BlockSpec((2, tn, half), lambda i, j, k: (k, j, 0)),
        [_sds((s, d), F32)], [mn_blk], (tm, tn), deps=(tok,))[0]
    sum_pool, sum_a_up, sum_b_up, sum_o = _reduce_scatter_finish(fl_b, dh)
    buf_in = _reduce_scatter_add("I", fl_in, dh, me, n_land=2)
    grad_x, dshift1, dscale1, g_norm1 = _norm_backward("norm1_bwd", dh, x2, norm1_w, scale1, dx1, deps=tuple(buf_in[:1]))

    dmod = jnp.concatenate([dshift1, dscale1, dgate1, dshift2, dscale2, dgate2], axis=1)
    pieces = [dmod, g_norm1, g_norm2, g_pool_scale, g_qnorm, g_knorm, jnp.full((1, LANES), loss_local, F32)]
    packed_rows = [_rows_of_lanes(p) for p in pieces]
    offsets = [0]
    for p in packed_rows:
        offsets.append(offsets[-1] + p.shape[0])
    packed = jnp.concatenate(packed_rows, axis=0)
    small_buf = lax.dynamic_update_slice(jnp.zeros((N_DEV,) + packed.shape, F32), packed[None], (me, 0, 0))
    (fl_small, fl_in), tok = _launch_groups(
        "rsI_relay0_small", [([small_buf], [("ag_small", _plan_everyone, N_DEV - 1)]),
                             (buf_in, [("rsI_relay0", _plan_scatter_relay_first, 4)])])
    small_all, = _land(fl_small, tok)
    small_sum = _sum_slots("small_sum", small_all[None], F32)[0]

    def unpack(i, width):
        return small_sum[offsets[i]:offsets[i] + width // LANES].reshape(1, width)

    g_b_ada = unpack(0, N_MOD * d)
    g_norm1_w = unpack(1, d)
    g_norm2_w = unpack(2, d)
    g_pool_scale_w = unpack(3, half)
    g_q_norm_w = unpack(4, HEAD_DIM)
    g_k_norm_w = unpack(5, HEAD_DIM)
    loss = unpack(6, LANES)[0, 0]
    dmod_all = small_all[:, :N_MOD * d // LANES].reshape(N_DEV, N_MOD * d)
    dmod_cols = lax.dynamic_slice_in_dim(dmod_all, me * wa, wa, axis=1)

    grads = {
        "b_ada": g_b_ada, "norm1_w": g_norm1_w,
        "q_norm_w": g_q_norm_w, "k_norm_w": g_k_norm_w,
        "pool_scale": g_pool_scale_w, "norm2_w": g_norm2_w,
    }
    sums = {"w_pool": sum_pool, "w_a_up": sum_a_up, "w_b_up": sum_b_up, "w_o": sum_o,
            "w_ff1": sum_ff1, "w_ff2": sum_ff2}
    weights = {"w_ada": (w_ada, m_w_ada, v_w_ada), "b_ada": (b_ada, m_b_ada, v_b_ada),
               "norm1_w": (norm1_w, m_norm1_w, v_norm1_w), "w_in": (w_in, m_w_in, v_w_in),
               "q_norm_w": (q_norm_w, m_q_norm_w, v_q_norm_w), "k_norm_w": (k_norm_w, m_k_norm_w, v_k_norm_w),
               "w_pool": (w_pool, m_w_pool, v_w_pool), "pool_scale": (pool_scale, m_pool_scale, v_pool_scale),
               "w_a_up": (w_a_up, m_w_a_up, v_w_a_up), "w_b_up": (w_b_up, m_w_b_up, v_w_b_up),
               "w_o": (w_o, m_w_o, v_w_o), "norm2_w": (norm2_w, m_norm2_w, v_norm2_w),
               "w_ff1": (w_ff1, m_w_ff1, v_w_ff1), "w_ff2": (w_ff2, m_w_ff2, v_w_ff2)}
    order = list(weights)
    deltas, new_m, new_v = {}, {}, {}
    def adam(name, deps=()):
        wt, mt, vt = weights[name]
        shape = wt.shape
        flat = (-1, shape[-1])
        if name in sums:
            own, received = sums[name]
            g, dl, nm, nv = _adamw_summed("adamw_" + name, wt.reshape(flat), own, received,
                                          mt.reshape(flat), vt.reshape(flat), deps=deps)
            grads[name] = g.reshape(shape)
        else:
            dl, nm, nv = _adamw("adamw_" + name, wt.reshape(flat), grads[name].reshape(flat),
                                mt.reshape(flat), vt.reshape(flat), deps=deps)
        deltas[name], new_m[name], new_v[name] = dl.reshape(shape), nm.reshape(shape), nv.reshape(shape)

    behind_first = list(sums)
    for name in behind_first:
        adam(name, deps=(tok,))
    grads["w_ada"] = _ada_weight_grad(c_all, dmod_cols, deps=(tok,))[None]
    chip_sums, relayed = _land(fl_in, [deltas[n] for n in behind_first] + [grads["w_ada"]])
    passed_on = _add_relayed("rsI_add_relay", chip_sums, relayed)
    fl_in, tok = _launch("rsI_relay1", [passed_on, lax.empty((1,) + passed_on.shape[1:], passed_on.dtype)],
                         _plan_scatter_relay_second, 2)
    behind_second = [n for n in order if n not in sums and n != "w_in"]
    for name in behind_second:
        adam(name, deps=(tok,))
    sums["w_in"] = tuple(_land(fl_in, [deltas[n] for n in behind_second]))
    adam("w_in")

    return (loss, grad_x[None], *[grads[n] for n in order], *[deltas[n] for n in order],
            *[new_m[n] for n in order], *[new_v[n] for n in order])
```

```python
import math

import jax
import jax.numpy as jnp
from jax import lax
from jax.experimental import pallas as pl
from jax.experimental.pallas import tpu as pltpu

F32 = jnp.float32
BF16 = jnp.bfloat16
MESH_AXES = ("x", "y", "c")
N_DEV = 8
HEAD_DIM = 128
ATTN_SCALE = 1.0 / math.sqrt(HEAD_DIM)
POOL_WINDOWS = (2, 4, 8, 16)
N_MOD = 6
NORM_EPS = 1e-6
LANES = 128
SUBLANES = 8
VMEM_LIMIT_BYTES = 56 * 1024 * 1024
Q_TILE = 256
K_TILE = 256
POOL_TILE = 512
HEADS_PER_STEP = 8
HEADS_PER_STEP_BWD = 4

ADAM_LR = 0.001
ADAM_B1 = 0.9
ADAM_B2 = 0.999
ADAM_EPS = 1e-08
ADAM_WD = 0.01
ADAM_STEP = 10

_NN = (((1,), (0,)), ((), ()))
_NT = (((1,), (1,)), ((), ()))
_TN = (((0,), (0,)), ((), ()))
_DIMS = {"nn": _NN, "nt": _NT, "tn": _TN}


def _dot(a, b, mode="nn"):
    return lax.dot_general(a, b, _DIMS[mode], preferred_element_type=F32)


def _params(*sem):
    return pltpu.CompilerParams(dimension_semantics=sem, vmem_limit_bytes=VMEM_LIMIT_BYTES)


def _tile(dim, pref, align=SUBLANES):
    for t in range(min(dim, pref), 0, -1):
        if dim % t == 0 and t % align == 0:
            return t
    return dim


def _group_index(axes):
    idx = 0
    for a in axes:
        idx = idx * 2 + lax.axis_index(a)
    return idx


def _peer_device(axes, k):
    coords = {a: lax.axis_index(a) for a in MESH_AXES}
    for pos, a in enumerate(axes):
        if (k >> (len(axes) - 1 - pos)) & 1:
            coords[a] = 1 - coords[a]
    return tuple(coords[a] for a in MESH_AXES)


_AXIS_BIT = {"x": 4, "y": 2, "c": 1}
_ANY = pl.BlockSpec(memory_space=pl.ANY)


def _device_xor(mask):
    return tuple(1 - lax.axis_index(a) if mask & _AXIS_BIT[a] else lax.axis_index(a) for a in MESH_AXES)


def _remote(src, dst, send_sem, recv_sem, mask):
    return pltpu.make_async_remote_copy(src_ref=src, dst_ref=dst, send_sem=send_sem, recv_sem=recv_sem,
                                        device_id=_device_xor(mask), device_id_type=pl.DeviceIdType.MESH)


_CHIP_MASKS = (0, _AXIS_BIT["y"], _AXIS_BIT["x"], _AXIS_BIT["x"] | _AXIS_BIT["y"])


def _add_received(name, own, own_slots, received, out_dtype):
    nj, r, c = received.shape
    tr = _tile(r, max(2 * SUBLANES, (1 << 20) // c), 2 * SUBLANES)

    def body(slots_ref, own_ref, rec_ref, o_ref):
        del slots_ref
        o_ref[...] = (own_ref[...].astype(F32) + rec_ref[...].astype(F32)).astype(o_ref.dtype)

    grid_spec = pltpu.PrefetchScalarGridSpec(
        num_scalar_prefetch=1, grid=(nj, r // tr),
        in_specs=[pl.BlockSpec((None, tr, c), lambda j, i, slots: (slots[j], i, 0)),
                  pl.BlockSpec((None, tr, c), lambda j, i, slots: (j, i, 0))],
        out_specs=pl.BlockSpec((None, tr, c), lambda j, i, slots: (j, i, 0)))
    return pl.pallas_call(
        body, name=name, grid_spec=grid_spec, out_shape=jax.ShapeDtypeStruct((nj, r, c), out_dtype),
        compiler_params=_params("parallel", "parallel"),
    )(own_slots, own, received)


_HBM = pl.BlockSpec(memory_space=pltpu.HBM)
_SEM = pl.BlockSpec(memory_space=pltpu.SEMAPHORE)
_DATAFLOW = pltpu.SideEffectType.DATAFLOW_SIDE_EFFECTING


def _launch_groups(name, groups, deps=()):
    bufs = [b for g_bufs, _ in groups for b in g_bufs]
    specs = [(len(g_bufs), spec) for g_bufs, g_plans in groups for spec in g_plans]
    nb, ns = len(bufs), len(specs)

    def body(*refs):
        sems = refs[nb + len(deps):nb + len(deps) + 2 * ns]
        me = _group_index(MESH_AXES)
        first, which = 0, 0
        for g_bufs, g_plans in groups:
            ins = refs[first:first + len(g_bufs)]
            for _, plan, n_copies in g_plans:
                copies = plan(ins, me)
                assert len(copies) == n_copies
                for n, (src, dst, mask) in enumerate(copies):
                    _remote(src, dst, sems[2 * which].at[n], sems[2 * which + 1].at[n], mask).start()
                which += 1
            first += len(g_bufs)
        refs[-1][...] = jnp.zeros_like(refs[-1])

    sem_shapes = [pltpu.SemaphoreType.DMA((n,)) for _, (_, _, n) in specs for _ in range(2)]
    outs = pl.pallas_call(
        body, name=name,
        out_shape=(*sem_shapes, *[pltpu.HBM(b.shape, b.dtype) for b in bufs],
                   jax.ShapeDtypeStruct((SUBLANES, LANES), F32)),
        in_specs=[_HBM] * nb + [_ANY] * len(deps),
        out_specs=(*[_SEM] * (2 * ns), *[_HBM] * nb, pl.BlockSpec(memory_space=pltpu.VMEM)),
        input_output_aliases={i: 2 * ns + i for i in range(nb)},
        compiler_params=pltpu.CompilerParams(has_side_effects=_DATAFLOW),
    )(*[pltpu.with_memory_space_constraint(b, pltpu.HBM) for b in bufs], *deps)
    flights, first, which = [], 0, 0
    for g_bufs, g_plans in groups:
        through = list(outs[2 * ns + first:2 * ns + first + len(g_bufs)])
        for land_name, plan, n_copies in g_plans:
            flights.append((land_name, plan, n_copies, outs[2 * which], outs[2 * which + 1], through))
            which += 1
        first += len(g_bufs)
    return flights, outs[-1]


def _launch(name, bufs, plan, n_copies, deps=()):
    (flight,), token = _launch_groups(name, [(bufs, [(name, plan, n_copies)])], deps)
    return flight, token


def _land(flight, after, bufs=None):
    name, plan, n_copies, send_sems, recv_sems, launched = flight
    bufs = launched if bufs is None else bufs
    nb = len(bufs)
    after = list(after) if isinstance(after, (list, tuple)) else [after]

    def body(*refs):
        ins = refs[:nb]
        s_sems, r_sems = refs[nb], refs[nb + 1]
        for n, (src, dst, mask) in enumerate(plan(ins, _group_index(MESH_AXES))):
            cp = _remote(src, dst, s_sems.at[n], r_sems.at[n], mask)
            cp.wait_send()
            cp.wait_recv()

    outs = pl.pallas_call(
        body, name=name + "_land",
        out_shape=tuple(pltpu.HBM(b.shape, b.dtype) for b in bufs),
        in_specs=[_HBM] * nb + [_SEM, _SEM] + [_ANY] * len(after), out_specs=tuple([_HBM] * nb),
        input_output_aliases={i: i for i in range(nb)},
        compiler_params=pltpu.CompilerParams(has_side_effects=_DATAFLOW),
    )(*bufs, send_sems, recv_sems, *after)
    return list(outs)


def _plan_gather_ici(phase):
    bx, by = _AXIS_BIT["x"], _AXIS_BIT["y"]

    def plan(refs, me):
        copies = []
        for ref in refs:
            half = ref.shape[1] // 2

            def piece(slot, color, mask, ref=ref, half=half):
                p = ref.at[slot, pl.ds(color * half, half)]
                return (p, p, mask)

            if phase == 0:
                copies += [piece(me, 0, bx), piece(me, 1, by)]
            else:
                copies += [piece(me, 0, by), piece(me ^ bx, 0, by), piece(me, 1, bx), piece(me ^ by, 1, bx)]
        return copies

    return plan


def _plan_d2d(masks):
    def plan(refs, me):
        return [(ref.at[me ^ m], ref.at[me ^ m], _AXIS_BIT["c"]) for ref in refs for m in masks]

    return plan


def _plan_gather_d2d(refs, me):
    return _plan_d2d(_CHIP_MASKS)(refs, me)


def _plan_neighbours(refs, me):
    return [(ref.at[me], ref.at[me], _AXIS_BIT[a]) for ref in refs for a in ("x", "y")]


def _plan_diagonal(refs, me):
    bx, by = _AXIS_BIT["x"], _AXIS_BIT["y"]
    copies = []
    for ref in refs:
        half = ref.shape[1] // 2
        lo = ref.at[me ^ bx, pl.ds(0, half)]
        hi = ref.at[me ^ by, pl.ds(half, half)]
        copies += [(lo, lo, by), (hi, hi, bx)]
    return copies


def _plan_scatter_d2d(refs, me):
    na = len(refs) // 2
    copies = []
    for a in range(na):
        for j, m in enumerate(_CHIP_MASKS):
            copies.append((refs[a].at[me ^ _AXIS_BIT["c"] ^ m], refs[na + a].at[j], _AXIS_BIT["c"]))
    return copies


def _plan_scatter_ici(refs, me):
    del me
    na = len(refs) // 2
    copies = []
    for a in range(na):
        for n, m in enumerate(_CHIP_MASKS[1:]):
            copies.append((refs[a].at[n + 1], refs[na + a].at[n], m))
    return copies


def _with_deps(body, n_in, deps):
    if not deps:
        return body

    def wrapped(*refs):
        return body(*refs[:n_in], *refs[n_in + len(deps):])

    return wrapped


def _reduce_scatter_start(tag, grads):
    lands = [lax.empty((len(_CHIP_MASKS),) + g.shape[1:], g.dtype) for g in grads]
    return _launch("rs%s_d2d" % tag, list(grads) + lands, _plan_scatter_d2d, len(_CHIP_MASKS) * len(grads))


def _reduce_scatter_handover(tag_done, bufs_done, tag_new, grads_new):
    lands = [lax.empty((len(_CHIP_MASKS),) + g.shape[1:], g.dtype) for g in grads_new]
    (fl_ici, fl_d2d), token = _launch_groups(
        "rs%s_ici_%s_d2d" % (tag_done, tag_new),
        [(bufs_done, [_reduce_scatter_plan(tag_done, bufs_done)]),
         (list(grads_new) + lands, [("rs%s_d2d" % tag_new, _plan_scatter_d2d, len(_CHIP_MASKS) * len(grads_new))])])
    return fl_ici, fl_d2d, token


def _reduce_scatter_add(tag, flight, after, me, n_land=len(_CHIP_MASKS) - 1):
    bufs = _land(flight, after)
    na = len(bufs) // 2
    own_slots = jnp.stack([me ^ m for m in _CHIP_MASKS]).astype(jnp.int32)
    sums = [_add_received("rs%s_add_d2d_%d" % (tag, a), bufs[a], own_slots, bufs[na + a], BF16) for a in range(na)]
    lands = [lax.empty((n_land,) + h.shape[1:], h.dtype) for h in sums]
    return sums + lands


def _plan_scatter_relay_first(refs, me):
    del me
    bx, by = _AXIS_BIT["x"], _AXIS_BIT["y"]
    na = len(refs) // 2
    copies = []
    for a in range(na):
        h, land = refs[a], refs[na + a]
        half = h.shape[1] // 2
        lo, hi = pl.ds(0, half), pl.ds(half, half)
        copies += [(h.at[1, lo], land.at[0, lo], by), (h.at[3, lo], land.at[1, lo], by),
                   (h.at[2, hi], land.at[0, hi], bx), (h.at[3, hi], land.at[1, hi], bx)]
    return copies


def _plan_scatter_relay_second(refs, me):
    del me
    na = len(refs) // 2
    copies = []
    for a in range(na):
        f, land = refs[a], refs[na + a]
        half = f.shape[1] // 2
        lo, hi = pl.ds(0, half), pl.ds(half, half)
        copies += [(f.at[1, lo], land.at[0, lo], _AXIS_BIT["x"]), (f.at[1, hi], land.at[0, hi], _AXIS_BIT["y"])]
    return copies


def _add_relayed(name, sums, received):
    _, r, c = sums.shape
    tr = _tile(r // 2, max(2 * SUBLANES, (1 << 20) // c), 2 * SUBLANES)
    n_half = (r // 2) // tr

    def body(own_ref, rec_ref, o_ref):
        o_ref[...] = (own_ref[...].astype(F32) + rec_ref[...].astype(F32)).astype(o_ref.dtype)

    def own_slot(j, i):
        return jnp.where(j == 0, 0, jnp.where(i < n_half, 2, 1))

    return pl.pallas_call(
        body, name=name, grid=(2, 2 * n_half), out_shape=jax.ShapeDtypeStruct((2, r, c), sums.dtype),
        in_specs=[pl.BlockSpec((None, tr, c), lambda j, i: (own_slot(j, i), i, 0)),
                  pl.BlockSpec((None, tr, c), lambda j, i: (j, i, 0))],
        out_specs=pl.BlockSpec((None, tr, c), lambda j, i: (j, i, 0)),
        compiler_params=_params("parallel", "parallel"),
    )(sums, received)


def _reduce_scatter_plan(tag, bufs):
    return ("rs%s_ici" % tag, _plan_scatter_ici, (len(_CHIP_MASKS) - 1) * (len(bufs) // 2))


def _reduce_scatter_middle(tag, flight, after, me):
    bufs = _reduce_scatter_add(tag, flight, after, me)
    name, plan, n_copies = _reduce_scatter_plan(tag, bufs)
    return _launch(name, bufs, plan, n_copies)


def _plan_everyone(refs, me):
    return [(ref.at[me], ref.at[me], m) for ref in refs for m in range(1, N_DEV)]


def _reduce_scatter_finish(flight, after):
    bufs = _land(flight, after)
    na = len(bufs) // 2
    return [(bufs[a], bufs[na + a]) for a in range(na)]


def _all_gather_2d(name, x, deps=()):
    r, c = x.shape

    def body(x_ref, out_ref, send_sems, recv_sems):
        me = _group_index(MESH_AXES)
        out_ref[me] = x_ref[...]
        copies = []
        for k in range(1, N_DEV):
            cp = pltpu.make_async_remote_copy(
                src_ref=x_ref, dst_ref=out_ref.at[me],
                send_sem=send_sems.at[k - 1], recv_sem=recv_sems.at[k - 1],
                device_id=_peer_device(MESH_AXES, k), device_id_type=pl.DeviceIdType.MESH)
            cp.start()
            copies.append(cp)
        for cp in copies:
            cp.wait()

    vmem = pl.BlockSpec(memory_space=pltpu.VMEM)
    return pl.pallas_call(
        _with_deps(body, 1, deps), name=name, out_shape=jax.ShapeDtypeStruct((N_DEV, r, c), x.dtype),
        in_specs=[vmem] + [_ANY] * len(deps), out_specs=vmem,
        scratch_shapes=[pltpu.SemaphoreType.DMA((N_DEV - 1,)), pltpu.SemaphoreType.DMA((N_DEV - 1,))],
    )(x, *deps)


def _sum_slots(name, buf, out_dtype):
    pre, n, r, c = buf.shape
    tr = _tile(r, max(SUBLANES * 2, (1 << 20) // c))

    def body(b_ref, o_ref):
        acc = b_ref[0].astype(F32)
        for q in range(1, n):
            acc = acc + b_ref[q].astype(F32)
        o_ref[...] = acc.astype(o_ref.dtype)

    return pl.pallas_call(
        body, name=name, grid=(pre, r // tr),
        out_shape=jax.ShapeDtypeStruct((pre, r, c), out_dtype),
        in_specs=[pl.BlockSpec((None, n, tr, c), lambda i, j: (i, 0, j, 0))],
        out_specs=pl.BlockSpec((None, tr, c), lambda i, j: (i, j, 0)),
        compiler_params=_params("parallel", "parallel"),
    )(buf)


def _matmul(name, mode, grid, a, a_spec, b, b_spec, out_shapes, out_specs, acc_shape,
            epilogue=None, extras=(), extra_specs=(), aliases=None, deps=()):
    nk = grid[2]
    n_extra = len(extras)
    n_out = len(out_shapes)

    def finish(acc, extra_refs, out_refs):
        if epilogue is None:
            out_refs[0][...] = acc.astype(out_refs[0].dtype)
        else:
            epilogue(acc, extra_refs, out_refs)

    def product(a_ref, b_ref):
        if len(b_ref.shape) == 2:
            return _dot(a_ref[...], b_ref[...], mode)
        width = a_ref.shape[1] // b_ref.shape[0]
        total = None
        for i in range(b_ref.shape[0]):
            part = _dot(a_ref[:, i * width:(i + 1) * width], b_ref[i], mode)
            total = part if total is None else total + part
        return total

    def body(*refs):
        a_ref, b_ref = refs[0], refs[1]
        extra_refs = refs[2:2 + n_extra]
        out_refs = refs[2 + n_extra:2 + n_extra + n_out]
        if nk == 1:
            finish(product(a_ref, b_ref), extra_refs, out_refs)
            return
        acc_ref = refs[-1]
        k = pl.program_id(2)

        @pl.when(k == 0)
        def _():
            acc_ref[...] = product(a_ref, b_ref)

        @pl.when((k > 0) & (k < nk - 1))
        def _():
            acc_ref[...] += product(a_ref, b_ref)

        @pl.when(k == nk - 1)
        def _():
            finish(acc_ref[...] + product(a_ref, b_ref), extra_refs, out_refs)

    scratch = [] if nk == 1 else [pltpu.VMEM(acc_shape, F32)]
    return pl.pallas_call(
        _with_deps(body, 2 + n_extra, deps), name=name, grid=grid, out_shape=tuple(out_shapes),
        in_specs=[a_spec, b_spec] + list(extra_specs) + [_ANY] * len(deps), out_specs=tuple(out_specs),
        scratch_shapes=scratch, input_output_aliases=aliases or {},
        compiler_params=_params("parallel", "parallel", "arbitrary"),
    )(a, b, *extras, *deps)


def _sds(shape, dtype):
    return jax.ShapeDtypeStruct(tuple(shape), dtype)


def _project_slots(name, h, w_full, slots, out_cols, proj_in=None, deps=()):
    s, d = h.shape
    _, _, wide = w_full.shape
    tm = _tile(s, 1024)
    n_in = 4 if proj_in is not None else 3

    def body(*refs):
        refs[-1][...] = _dot(refs[1][...], refs[2][...])

    grid_spec = pltpu.PrefetchScalarGridSpec(
        num_scalar_prefetch=1, grid=(s // tm, slots.shape[0]),
        in_specs=[pl.BlockSpec((tm, d), lambda i, j, sl: (i, 0)),
                  pl.BlockSpec((None, d, wide), lambda i, j, sl: (sl[j], 0, 0))]
        + [_ANY] * (n_in - 3 + len(deps)),
        out_specs=pl.BlockSpec((tm, wide), lambda i, j, sl: (i, sl[j])))
    extra = ([proj_in] if proj_in is not None else []) + list(deps)
    return pl.pallas_call(
        body, name=name, grid_spec=grid_spec, out_shape=_sds((s, out_cols), F32),
        input_output_aliases={3: 0} if proj_in is not None else {},
        compiler_params=_params("parallel", "arbitrary"),
    )(slots, h, w_full, *extra)


def _ada_forward(c_all, w_ada, b_shard):
    nb, d = c_all.shape
    w = w_ada.shape[1]
    tn = _tile(w, 512)

    def body(c_ref, w_ref, b_ref, o_ref):
        cv = c_ref[...]
        sc = cv * jax.nn.sigmoid(cv)
        o_ref[...] = jnp.dot(sc, w_ref[...], precision=lax.Precision.HIGHEST,
                             preferred_element_type=F32) + b_ref[...]

    return pl.pallas_call(
        body, name="ada_fwd", grid=(w // tn,), out_shape=_sds((nb, w), F32),
        in_specs=[pl.BlockSpec((nb, d), lambda j: (0, 0)), pl.BlockSpec((d, tn), lambda j: (0, j)),
                  pl.BlockSpec((1, tn), lambda j: (0, j))],
        out_specs=pl.BlockSpec((nb, tn), lambda j: (0, j)),
        compiler_params=_params("parallel"),
    )(c_all, w_ada, b_shard)


def _ada_weight_grad(c_all, dmod_cols, deps=()):
    nb, d = c_all.shape
    w = dmod_cols.shape[1]
    tn = _tile(w, 512)

    def body(c_ref, g_ref, o_ref):
        cv = c_ref[...]
        sc = cv * jax.nn.sigmoid(cv)
        o_ref[...] = lax.dot_general(sc, g_ref[...], _TN, precision=lax.Precision.HIGHEST,
                                     preferred_element_type=F32)

    return pl.pallas_call(
        _with_deps(body, 2, deps), name="ada_wgrad", grid=(w // tn,), out_shape=_sds((d, w), F32),
        in_specs=[pl.BlockSpec((nb, d), lambda j: (0, 0)), pl.BlockSpec((nb, tn), lambda j: (0, j))]
        + [_ANY] * len(deps),
        out_specs=pl.BlockSpec((d, tn), lambda j: (0, j)),
        compiler_params=_params("parallel"),
    )(c_all, dmod_cols, *deps)


def _norm_forward(name, x, norm_w, scale, shift, deps=()):
    s, d = x.shape
    tm = _tile(s, 256)

    def body(x_ref, w_ref, sc_ref, sh_ref, h_ref):
        xv = x_ref[...]
        r = lax.rsqrt(jnp.mean(xv * xv, axis=-1, keepdims=True) + NORM_EPS)
        h = (xv * r * w_ref[...]) * (1.0 + sc_ref[...]) + sh_ref[...]
        h_ref[...] = h.astype(BF16)

    vec = pl.BlockSpec((1, d), lambda i: (0, 0))
    row = pl.BlockSpec((tm, d), lambda i: (i, 0))
    return pl.pallas_call(
        _with_deps(body, 4, deps), name=name, grid=(s // tm,), out_shape=_sds((s, d), BF16),
        in_specs=[row, vec, vec, vec] + [_ANY] * len(deps), out_specs=row, compiler_params=_params("parallel"),
    )(x, norm_w, scale, shift, *deps)


def _norm_backward(name, dh, x, norm_w, scale, dres, gated=None, deps=()):
    s, d = x.shape
    tm = _tile(s, 256)
    n_in = 7 if gated else 5

    def body(*refs):
        dh_ref, x_ref, w_ref, sc_ref, dres_ref = refs[:5]
        dx_ref, dshift_ref, dscale_ref, dw_ref = refs[n_in:n_in + 4]
        sums = (dshift_ref, dscale_ref, dw_ref) + ((refs[n_in + 5],) if gated else ())

        @pl.when(pl.program_id(0) == 0)
        def _():
            for ref in sums:
                ref[...] = jnp.zeros_like(ref)

        xv = x_ref[...]
        g = dh_ref[...]
        r = lax.rsqrt(jnp.mean(xv * xv, axis=-1, keepdims=True) + NORM_EPS)
        n = xv * r
        gain = 1.0 + sc_ref[...]
        gn = g * n
        dshift_ref[...] += jnp.sum(g, axis=0, keepdims=True)
        dscale_ref[...] += jnp.sum(gn, axis=0, keepdims=True) * w_ref[...]
        dw_ref[...] += jnp.sum(gn, axis=0, keepdims=True) * gain
        dn = g * (w_ref[...] * gain)
        dx = dres_ref[...] + r * (dn - n * jnp.mean(dn * n, axis=-1, keepdims=True))
        dx_ref[...] = dx
        if gated:
            gate_ref, other_ref = refs[5:7]
            refs[n_in + 4][...] = (dx * gate_ref[...]).astype(BF16)
            refs[n_in + 5][...] += jnp.sum(dx * other_ref[...].astype(F32), axis=0, keepdims=True)

    vec = pl.BlockSpec((1, d), lambda i: (0, 0))
    row = pl.BlockSpec((tm, d), lambda i: (i, 0))
    vec_out = _sds((1, d), F32)
    return pl.pallas_call(
        _with_deps(body, n_in, deps), name=name, grid=(s // tm,),
        out_shape=(_sds((s, d), F32), vec_out, vec_out, vec_out) + ((_sds((s, d), BF16), vec_out) if gated else ()),
        in_specs=[row, row, vec, vec, row] + ([vec, row] if gated else []) + [_ANY] * len(deps),
        out_specs=(row, vec, vec, vec) + ((row, vec) if gated else ()),
        compiler_params=_params("arbitrary"),
    )(dh, x, norm_w, scale, dres, *(gated or ()), *deps)


def _split_bf16(v):
    hi = v.astype(BF16)
    lo = (v - hi.astype(F32)).astype(BF16)
    return hi, lo


def _pool_forward(proj, w_pool, pool_scale, deps=()):
    s = proj.shape[0]
    g_n, cg, _ = w_pool.shape
    t = POOL_TILE
    nt = s // t

    def body(cur_ref, prev_ref, wp_ref, sc_ref, pooled_ref, ya_ref):
        g = pl.program_id(0)
        ti = pl.program_id(1)
        win = jnp.left_shift(2, g)
        row = lax.broadcasted_iota(jnp.int32, (t, t), 0)
        col = lax.broadcasted_iota(jnp.int32, (t, t), 1)
        lag = row - col
        band_cur = ((lag >= 0) & (lag < win)).astype(BF16)
        band_prev = ((lag + t < win) & (ti > 0)).astype(BF16)
        u = cur_ref[...]
        u_hi, u_lo = _split_bf16(u)
        p_hi, p_lo = _split_bf16(prev_ref[...])
        wsum = (_dot(band_cur, u_hi) + _dot(band_cur, u_lo)
                + _dot(band_prev, p_hi) + _dot(band_prev, p_lo))
        tok = ti * t + lax.broadcasted_iota(jnp.int32, (t, 1), 0)
        count = jnp.minimum(tok + 1, win).astype(F32)
        pooled = (wsum / count - u).astype(BF16)
        pooled_ref[...] = pooled
        ya_ref[...] = (_dot(pooled, wp_ref[...]) * sc_ref[...]).astype(BF16)

    blk = pl.BlockSpec((t, cg), lambda g, i: (i, g))
    return pl.pallas_call(
        _with_deps(body, 4, deps), name="pool_fwd", grid=(g_n, nt),
        out_shape=(_sds((s, g_n * cg), BF16), _sds((s, g_n * cg), BF16)),
        in_specs=[blk, pl.BlockSpec((t, cg), lambda g, i: (jnp.maximum(i - 1, 0), g)),
                  pl.BlockSpec((None, cg, cg), lambda g, i: (g, 0, 0)),
                  pl.BlockSpec((1, cg), lambda g, i: (0, g))] + [_ANY] * len(deps),
        out_specs=(blk, blk), compiler_params=_params("parallel", "parallel"),
    )(proj, proj, w_pool, pool_scale, *deps)


def _pool_backward(dya, pooled, w_pool, pool_scale, dproj):
    s = dya.shape[0]
    g_n, cg, _ = w_pool.shape
    t = POOL_TILE
    nt = s // t

    def body(dya_ref, dya_next_ref, pooled_ref, wp_ref, sc_ref, dproj_in, du_ref, gw_ref, gs_ref):
        del dproj_in
        g = pl.program_id(0)
        ti = pl.program_id(1)

        @pl.when(ti == 0)
        def _():
            gw_ref[...] = jnp.zeros_like(gw_ref)
            gs_ref[...] = jnp.zeros_like(gs_ref)

        win = jnp.left_shift(2, g)
        wp = wp_ref[...]
        sc = sc_ref[...]
        pooled_v = pooled_ref[...]
        dya_v = dya_ref[...].astype(F32)
        mixed = _dot(pooled_v, wp)
        gs_ref[...] += jnp.sum(dya_v * mixed, axis=0, keepdims=True)
        dmixed = (dya_v * sc).astype(BF16)
        gw_ref[...] += _dot(pooled_v, dmixed, "tn")
        dpooled = _dot(dmixed, wp, "nt")
        dmixed_next = (dya_next_ref[...].astype(F32) * sc).astype(BF16)
        dpooled_next = _dot(dmixed_next, wp, "nt")
        tok = ti * t + lax.broadcasted_iota(jnp.int32, (t, 1), 0)
        e_cur = dpooled / jnp.minimum(tok + 1, win).astype(F32)
        e_next = dpooled_next / jnp.minimum(tok + t + 1, win).astype(F32)
        row = lax.broadcasted_iota(jnp.int32, (t, t), 0)
        col = lax.broadcasted_iota(jnp.int32, (t, t), 1)
        lead = col - row
        band_cur = ((lead >= 0) & (lead < win)).astype(BF16)
        band_next = ((lead + t < win) & (ti < nt - 1)).astype(BF16)
        c_hi, c_lo = _split_bf16(e_cur)
        n_hi, n_lo = _split_bf16(e_next)
        du = (_dot(band_cur, c_hi) + _dot(band_cur, c_lo)
              + _dot(band_next, n_hi) + _dot(band_next, n_lo)) - dpooled
        du_ref[...] = du.astype(BF16)

    blk = pl.BlockSpec((t, cg), lambda g, i: (i, g))
    du, gw, gs = pl.pallas_call(
        body, name="pool_bwd", grid=(g_n, nt),
        out_shape=(_sds(dproj.shape, BF16), _sds((g_n, cg, cg), F32), _sds((1, g_n * cg), F32)),
        in_specs=[blk, pl.BlockSpec((t, cg), lambda g, i: (jnp.minimum(i + 1, nt - 1), g)), blk,
                  pl.BlockSpec((None, cg, cg), lambda g, i: (g, 0, 0)),
                  pl.BlockSpec((1, cg), lambda g, i: (0, g)),
                  pl.BlockSpec(memory_space=pl.ANY)],
        out_specs=(blk, pl.BlockSpec((None, cg, cg), lambda g, i: (g, 0, 0)),
                   pl.BlockSpec((1, cg), lambda g, i: (0, g))),
        input_output_aliases={5: 0}, compiler_params=_params("parallel", "arbitrary"),
    )(dya, dya, pooled, w_pool, pool_scale, dproj)
    return du, gw, gs


def _qkv_prepare(proj, q_norm_w, k_norm_w, width, deps=()):
    s = proj.shape[0]
    tm = _tile(s, 256)
    heads = width // HEAD_DIM

    def body(q_ref, k_ref, v_ref, qw_ref, kw_ref, qn_ref, kn_ref, vb_ref):
        for h in range(heads):
            cols = slice(h * HEAD_DIM, (h + 1) * HEAD_DIM)
            for src, w_ref, dst, gain in ((q_ref, qw_ref, qn_ref, ATTN_SCALE), (k_ref, kw_ref, kn_ref, 1.0)):
                v = src[:, cols]
                r = lax.rsqrt(jnp.mean(v * v, axis=-1, keepdims=True) + NORM_EPS)
                dst[:, cols] = (v * (r * gain) * w_ref[...]).astype(BF16)
        vb_ref[...] = v_ref[...].astype(BF16)

    vec = pl.BlockSpec((1, HEAD_DIM), lambda i: (0, 0))
    out_spec = pl.BlockSpec((tm, width), lambda i: (i, 0))
    return pl.pallas_call(
        _with_deps(body, 5, deps), name="qkv_prep", grid=(s // tm,),
        out_shape=(_sds((s, width), BF16),) * 3,
        in_specs=[pl.BlockSpec((tm, width), lambda i: (i, 1)), pl.BlockSpec((tm, width), lambda i: (i, 2)),
                  pl.BlockSpec((tm, width), lambda i: (i, 3)), vec, vec] + [_ANY] * len(deps),
        out_specs=(out_spec,) * 3, compiler_params=_params("parallel"),
    )(proj, proj, proj, q_norm_w, k_norm_w, *deps)


def _qk_norm_backward(name, dn, proj, col_block, norm_w, dproj, width, deps=()):
    s = proj.shape[0]
    tm = _tile(s, 256)
    heads = width // HEAD_DIM

    def body(dn_ref, q_ref, w_ref, dproj_in, dq_ref, gw_ref):
        del dproj_in

        @pl.when(pl.program_id(0) == 0)
        def _():
            gw_ref[...] = jnp.zeros_like(gw_ref)

        wv = w_ref[...]
        gw = jnp.zeros((1, HEAD_DIM), F32)
        for h in range(heads):
            cols = slice(h * HEAD_DIM, (h + 1) * HEAD_DIM)
            v = q_ref[:, cols]
            g = dn_ref[:, cols]
            r = lax.rsqrt(jnp.mean(v * v, axis=-1, keepdims=True) + NORM_EPS)
            n = v * r
            gw = gw + jnp.sum(g * n, axis=0, keepdims=True)
            gn = g * wv
            dq_ref[:, cols] = (r * (gn - n * jnp.mean(gn * n, axis=-1, keepdims=True))).astype(BF16)
        gw_ref[...] += gw

    blk = pl.BlockSpec((tm, width), lambda i: (i, col_block))
    return pl.pallas_call(
        _with_deps(body, 4, deps), name=name, grid=(s // tm,),
        out_shape=(_sds(dproj.shape, BF16), _sds((1, HEAD_DIM), F32)),
        in_specs=[pl.BlockSpec((tm, width), lambda i: (i, 0)), blk,
                  pl.BlockSpec((1, HEAD_DIM), lambda i: (0, 0)), pl.BlockSpec(memory_space=pl.ANY)]
        + [_ANY] * len(deps),
        out_specs=(blk, pl.BlockSpec((1, HEAD_DIM), lambda i: (0, 0))),
        input_output_aliases={3: 0}, compiler_params=_params("arbitrary"),
    )(dn, proj, norm_w, dproj, *deps)


def _strict_upper(n):
    row = lax.broadcasted_iota(jnp.int32, (n, n), 0)
    col = lax.broadcasted_iota(jnp.int32, (n, n), 1)
    return (row > col).astype(BF16)


def _strict_lower(n):
    row = lax.broadcasted_iota(jnp.int32, (n, n), 0)
    col = lax.broadcasted_iota(jnp.int32, (n, n), 1)
    return (row < col).astype(BF16)


def _cumulate(v, tri):
    return _dot(v.astype(BF16), tri)


def _log_sigmoid(z):
    return jnp.minimum(z, 0.0) - jnp.log(1.0 + jnp.exp(-jnp.abs(z)))


def _attention_forward(qn, kn, vb, deps=()):
    s, width = qn.shape
    heads = width // HEAD_DIM
    tq, tk = Q_TILE, K_TILE
    hp = min(HEADS_PER_STEP, heads)
    assert tq == tk and s % tq == 0 and heads % hp == 0

    def body(q_ref, k_ref, v_ref, o_ref, a_scr):
        qi = pl.program_id(1)
        upper = _strict_upper(tk)
        causal = lax.broadcasted_iota(jnp.int32, (tq, tk), 1) < lax.broadcasted_iota(jnp.int32, (tq, tk), 0)
        head_cols = [slice(u * HEAD_DIM, (u + 1) * HEAD_DIM) for u in range(hp)]

        def weights(kb, carry, masked):
            rows = pl.ds(pl.multiple_of(kb * tk, tk), tk)
            out = []
            for u, cols in enumerate(head_cols):
                later = carry[u]
                z = _dot(q_ref[:, cols], k_ref[rows, cols], "nt")
                log_beta = _log_sigmoid(z)
                l = log_beta - z
                if masked:
                    l = jnp.where(causal, l, 0.0)
                a = jnp.exp(log_beta + _cumulate(l, upper) + later)
                if masked:
                    a = jnp.where(causal, a, 0.0)
                a_scr[u, :, rows] = a.astype(BF16)
                out.append(later + jnp.sum(l, axis=1, keepdims=True))
            return tuple(out)

        later = weights(qi, tuple(jnp.zeros((tq, 1), F32) for _ in range(hp)), True)
        lax.fori_loop(0, qi, lambda i, c: weights(qi - 1 - i, c, False), later)

        def mix(kb, accs):
            rows = pl.ds(pl.multiple_of(kb * tk, tk), tk)
            return tuple(acc + _dot(a_scr[u, :, rows], v_ref[rows, cols])
                         for u, (acc, cols) in enumerate(zip(accs, head_cols)))

        accs = lax.fori_loop(0, qi + 1, mix, tuple(jnp.zeros((tq, HEAD_DIM), F32) for _ in range(hp)))
        for acc, cols in zip(accs, head_cols):
            o_ref[:, cols] = acc.astype(BF16)

    full = pl.BlockSpec((s, hp * HEAD_DIM), lambda h, i: (0, h))
    blk = pl.BlockSpec((tq, hp * HEAD_DIM), lambda h, i: (i, h))
    return pl.pallas_call(
        _with_deps(body, 3, deps), name="attn_fwd", grid=(heads // hp, s // tq), out_shape=_sds((s, width), BF16),
        in_specs=[blk, full, full] + [_ANY] * len(deps), out_specs=blk,
        scratch_shapes=[pltpu.VMEM((hp, tq, s), BF16)],
        compiler_params=_params("parallel", "parallel"),
    )(qn, kn, vb, *deps)


def _attention_backward(qn, kn, vb, dout, dproj, v_col_block, deps=()):
    s, width = qn.shape
    heads = width // HEAD_DIM
    tq, tk = Q_TILE, K_TILE
    hp = min(HEADS_PER_STEP_BWD, heads)
    nq = s // tq
    v_block0 = v_col_block * (heads // hp)

    def body(q_ref, k_ref, v_ref, do_ref, dproj_in, dq_ref, dk_ref, dv_ref,
             a_scr, lb_scr, dz_scr, dkt_scr, dvt_scr):
        del dproj_in
        qi = pl.program_id(1)

        @pl.when(qi == 0)
        def _():
            dkt_scr[...] = jnp.zeros_like(dkt_scr)
            dvt_scr[...] = jnp.zeros_like(dvt_scr)

        upper = _strict_upper(tk)
        lower = _strict_lower(tk)
        causal = lax.broadcasted_iota(jnp.int32, (tq, tk), 1) < lax.broadcasted_iota(jnp.int32, (tq, tk), 0)
        head_cols = [slice(u * HEAD_DIM, (u + 1) * HEAD_DIM) for u in range(hp)]

        def weights(kb, carry, masked):
            rows = pl.ds(pl.multiple_of(kb * tk, tk), tk)
            out = []
            for u, cols in enumerate(head_cols):
                later = carry[u]
                z = _dot(q_ref[:, cols], k_ref[rows, cols], "nt")
                log_beta = _log_sigmoid(z)
                l = log_beta - z
                if masked:
                    l = jnp.where(causal, l, 0.0)
                a = jnp.exp(log_beta + _cumulate(l, upper) + later)
                if masked:
                    a = jnp.where(causal, a, 0.0)
                a_scr[u, :, rows] = a
                lb_scr[u, :, rows] = log_beta
                out.append(later + jnp.sum(l, axis=1, keepdims=True))
            return tuple(out)

        zeros = tuple(jnp.zeros((tq, 1), F32) for _ in range(hp))
        later = weights(qi, zeros, True)
        lax.fori_loop(0, qi, lambda i, c: weights(qi - 1 - i, c, False), later)

        q_t = [jnp.transpose(q_ref[:, cols].astype(F32)).astype(BF16) for cols in head_cols]
        do_t = [jnp.transpose(do_ref[:, cols].astype(F32)).astype(BF16) for cols in head_cols]

        def scores(kb, carry, masked):
            rows = pl.ds(pl.multiple_of(kb * tk, tk), tk)
            out = []
            for u, cols in enumerate(head_cols):
                before = carry[u]
                beta = jnp.exp(lb_scr[u, :, rows])
                g = a_scr[u, :, rows] * _dot(do_ref[:, cols], v_ref[rows, cols], "nt")
                p = _cumulate(g, lower) + before
                dz = g - (g + p) * beta
                if masked:
                    dz = jnp.where(causal, dz, 0.0)
                dz_scr[u, :, rows] = dz.astype(BF16)
                out.append(before + jnp.sum(g, axis=1, keepdims=True))
            return tuple(out)

        before = lax.fori_loop(0, qi, lambda i, c: scores(i, c, False), zeros)
        scores(qi, before, True)

        def products(kb, dqs):
            rows = pl.ds(pl.multiple_of(kb * tk, tk), tk)
            out = []
            for u, cols in enumerate(head_cols):
                dz = dz_scr[u, :, rows]
                dkt_scr[cols, rows] += _dot(q_t[u], dz)
                dvt_scr[cols, rows] += _dot(do_t[u], a_scr[u, :, rows].astype(BF16))
                out.append(dqs[u] + _dot(dz, k_ref[rows, cols]))
            return tuple(out)

        dqs = lax.fori_loop(0, qi + 1, products, tuple(jnp.zeros((tq, HEAD_DIM), F32) for _ in range(hp)))
        for u, cols in enumerate(head_cols):
            dq_ref[:, cols] = dqs[u] * ATTN_SCALE

        @pl.when(qi == nq - 1)
        def _():
            dk_ref[...] = jnp.transpose(dkt_scr[...])
            dv_ref[...] = jnp.transpose(dvt_scr[...]).astype(BF16)

    wide = hp * HEAD_DIM
    full = pl.BlockSpec((s, wide), lambda h, i: (0, h))
    blk = pl.BlockSpec((tq, wide), lambda h, i: (i, h))
    return pl.pallas_call(
        _with_deps(body, 5, deps), name="attn_bwd", grid=(heads // hp, nq),
        out_shape=(_sds((s, width), F32), _sds((s, width), F32), _sds(dproj.shape, BF16)),
        in_specs=[blk, full, full, blk, pl.BlockSpec(memory_space=pl.ANY)] + [_ANY] * len(deps),
        out_specs=(blk, full, pl.BlockSpec((s, wide), lambda h, i: (0, v_block0 + h))),
        scratch_shapes=[pltpu.VMEM((hp, tq, s), F32), pltpu.VMEM((hp, tq, s), F32), pltpu.VMEM((hp, tq, s), BF16),
                        pltpu.VMEM((wide, s), F32), pltpu.VMEM((wide, s), F32)],
        input_output_aliases={4: 2}, compiler_params=_params("parallel", "arbitrary"),
    )(qn, kn, vb, dout, dproj, *deps)


def _place_columns(name, src, dst, col_block):
    s, w = src.shape
    tm = _tile(s, 512)

    def body(src_ref, dst_in, out_ref):
        del dst_in
        out_ref[...] = src_ref[...]

    return pl.pallas_call(
        body, name=name, grid=(s // tm,), out_shape=_sds(dst.shape, dst.dtype),
        in_specs=[pl.BlockSpec((tm, w), lambda i: (i, 0)), pl.BlockSpec(memory_space=pl.ANY)],
        out_specs=pl.BlockSpec((tm, w), lambda i: (i, col_block)),
        input_output_aliases={1: 0}, compiler_params=_params("parallel"),
    )(src, dst)


def _cast_into_slot(name, x, slot):
    r, c = x.shape
    tr = _tile(r, max(SUBLANES * 2, (1 << 20) // c), SUBLANES * 2)

    def body(slot_ref, x_ref, o_ref):
        del slot_ref
        o_ref[...] = x_ref[...].astype(BF16)

    grid_spec = pltpu.PrefetchScalarGridSpec(
        num_scalar_prefetch=1, grid=(r // tr,),
        in_specs=[pl.BlockSpec((tr, c), lambda i, slot_ref: (i, 0))],
        out_specs=pl.BlockSpec((None, tr, c), lambda i, slot_ref: (slot_ref[0], i, 0)))
    return pl.pallas_call(
        body, name=name, grid_spec=grid_spec, out_shape=_sds((N_DEV, r, c), BF16),
        compiler_params=_params("parallel"),
    )(slot, x)


def _adamw_update(gv, w_ref, m_ref, v_ref, d_ref, nm_ref, nv_ref):
    c1 = 1.0 - ADAM_B1 ** ADAM_STEP
    c2 = 1.0 - ADAM_B2 ** ADAM_STEP
    nm = ADAM_B1 * m_ref[...] + (1.0 - ADAM_B1) * gv
    nv = ADAM_B2 * v_ref[...] + (1.0 - ADAM_B2) * (gv * gv)
    d_ref[...] = -ADAM_LR * ((nm / c1) / (jnp.sqrt(nv / c2) + ADAM_EPS) + ADAM_WD * w_ref[...])
    nm_ref[...] = nm
    nv_ref[...] = nv


def _adamw(name, w, g, m, v, deps=()):
    r, c = w.shape
    tr = _tile(r, max(SUBLANES, (1 << 19) // c))

    def body(w_ref, g_ref, m_ref, v_ref, d_ref, nm_ref, nv_ref):
        _adamw_update(g_ref[...], w_ref, m_ref, v_ref, d_ref, nm_ref, nv_ref)

    blk = pl.BlockSpec((tr, c), lambda i: (i, 0))
    return pl.pallas_call(
        _with_deps(body, 4, deps), name=name, grid=(r // tr,), out_shape=(_sds((r, c), F32),) * 3,
        in_specs=[blk] * 4 + [_ANY] * len(deps), out_specs=(blk,) * 3, compiler_params=_params("parallel"),
    )(w, g, m, v, *deps)


def _adamw_summed(name, w, own, received, m, v, deps=()):
    r, c = w.shape
    nj = received.shape[0]
    tr = _tile(r, max(2 * SUBLANES, (1 << 19) // c), 2 * SUBLANES)

    def body(w_ref, own_ref, rec_ref, m_ref, v_ref, g_ref, d_ref, nm_ref, nv_ref):
        gv = own_ref[...].astype(F32)
        for j in range(nj):
            gv = gv + rec_ref[j].astype(F32)
        g_ref[...] = gv
        _adamw_update(gv, w_ref, m_ref, v_ref, d_ref, nm_ref, nv_ref)

    blk = pl.BlockSpec((tr, c), lambda i: (i, 0))
    return pl.pallas_call(
        _with_deps(body, 5, deps), name=name, grid=(r // tr,), out_shape=(_sds((r, c), F32),) * 4,
        in_specs=[blk, pl.BlockSpec((None, tr, c), lambda i: (0, i, 0)),
                  pl.BlockSpec((nj, tr, c), lambda i: (0, i, 0)), blk, blk] + [_ANY] * len(deps),
        out_specs=(blk,) * 4, compiler_params=_params("parallel"),
    )(w, own, received, m, v, *deps)


def _rows_of_lanes(v):
    rows = v.shape[1] // LANES
    out = v.reshape(rows, LANES)
    pad = (-rows) % SUBLANES
    if pad:
        out = jnp.pad(out, ((0, pad), (0, 0)))
    return out


def kernel(x, c, w_ada, b_ada, norm1_w, w_in, q_norm_w, k_norm_w, w_pool, pool_scale, w_a_up, w_b_up, w_o, norm2_w, w_ff1, w_ff2, loss_target, m_w_ada, m_b_ada, m_norm1_w, m_w_in, m_q_norm_w, m_k_norm_w, m_w_pool, m_pool_scale, m_w_a_up, m_w_b_up, m_w_o, m_norm2_w, m_w_ff1, m_w_ff2, v_w_ada, v_b_ada, v_norm1_w, v_w_in, v_q_norm_w, v_k_norm_w, v_w_pool, v_pool_scale, v_w_a_up, v_w_b_up, v_w_o, v_norm2_w, v_w_ff1, v_w_ff2):
    _, s, d = x.shape
    half = d // 2
    d8 = d // N_DEV
    n_groups = len(POOL_WINDOWS)
    cg = half // n_groups
    me = _group_index(MESH_AXES)

    x2 = x[0]
    target = loss_target[0]

    my_slot = jnp.reshape(me, (1,)).astype(jnp.int32)

    def cast(i, t):
        return _cast_into_slot("cast_w%d" % i, t, my_slot)

    def gather_start(tag, bufs, phase):
        if phase < 2:
            return _launch("ag%s_ici%d" % (tag, phase), bufs, _plan_gather_ici(phase), (2, 4)[phase] * len(bufs))
        return _launch("ag%s_d2d" % tag, bufs, _plan_gather_d2d, len(_CHIP_MASKS) * len(bufs))

    def gather_plans(tag, n_bufs, phase):
        if phase < 2:
            return ("ag%s_ici%d" % (tag, phase), _plan_gather_ici(phase), (2, 4)[phase] * n_bufs)
        return ("ag%s_d2d" % tag, _plan_gather_d2d, len(_CHIP_MASKS) * n_bufs)

    bx, by, bc = _AXIS_BIT["x"], _AXIS_BIT["y"], _AXIS_BIT["c"]
    buf_a = [cast(0, w_in[0])]

    c_all = _all_gather_2d("ag_c", c.reshape(d // LANES, LANES), deps=tuple(buf_a)).reshape(N_DEV, d)
    wa = w_ada.shape[2]
    b_shard = lax.dynamic_slice_in_dim(b_ada, me * wa, wa, axis=1)
    mod_part = _ada_forward(c_all, w_ada[0], b_shard)
    mod_rows = mod_part.reshape(N_DEV * wa // LANES, LANES)
    mod_buf = lax.dynamic_update_slice(jnp.zeros((N_DEV,) + mod_rows.shape, F32), mod_rows[None], (me, 0, 0))
    (fl_mod, fl_a, fl_pair), tok = _launch_groups(
        "agA_near", [([mod_buf], [("ag_mod", _plan_everyone, N_DEV - 1)]),
                     (buf_a, [("agA_near", _plan_neighbours, 2), ("agA_pair_d2d", _plan_d2d((0,)), 1)])])
    mod_all, = _land(fl_mod, tok)
    buf_b = [cast(1, w_pool[0].reshape(-1, cg)), cast(2, w_a_up[0]), cast(3, w_b_up[0]), cast(4, w_o[0])]
    buf_c = [cast(5, w_ff1[0])]
    buf_e = [cast(6, w_ff2[0])]
    mod_all = mod_all.reshape(N_DEV, N_DEV, wa)
    mod = lax.dynamic_slice_in_dim(mod_all, me, 1, axis=1).reshape(1, N_MOD * d)
    shift1, scale1, gate1, shift2, scale2, gate2 = [mod[:, i * d:(i + 1) * d] for i in range(N_MOD)]

    h = _norm_forward("norm1_fwd", x2, norm1_w, scale1, shift1)
    buf_a = _land(fl_pair, h, bufs=fl_a[-1])
    pair_slots = jnp.stack([me, me ^ bc]).astype(jnp.int32)
    proj = _project_slots("proj_pair", h, buf_a[0], pair_slots, 4 * d)
    buf_a = _land(fl_a, [proj] + buf_b + buf_c + buf_e, bufs=buf_a)
    near = (bx, by)
    (fl_far, fl_near), tok = _launch_groups(
        "agA_far", [(buf_a, [("agA_far", _plan_diagonal, 2), ("agA_near_d2d", _plan_d2d(near), len(near))])])
    (fl_b, fl_c, fl_e), tok = _launch_groups(
        "agBCE_ici0", [(buf_b, [gather_plans("B", len(buf_b), 0)]), (buf_c, [gather_plans("C", 1, 0)]),
                       (buf_e, [gather_plans("E", 1, 0)])])

    tm = _tile(s, 1024)
    tk = _tile(d, 2048)
    te = _tile(d, 512)

    own_slots = jnp.stack([me ^ m for m in near]).astype(jnp.int32)
    far_slots = jnp.stack([me ^ bx ^ by ^ f for f in (0, bc)]).astype(jnp.int32)
    buf_a = fl_near[-1]
    proj = _project_slots("proj_own", h, buf_a[0], own_slots, 4 * d, proj_in=proj, deps=(tok,))
    buf_a = _land(fl_near, proj, bufs=buf_a)
    proj = _project_slots("proj_sibling", h, buf_a[0], own_slots ^ bc, 4 * d, proj_in=proj)
    buf_a = _land(fl_far, proj, bufs=buf_a)
    fl_a, tok = _launch("agA_far_d2d", buf_a, _plan_d2d((bx | by,)), 1)
    w_in_f, = _land(fl_a, tok)
    proj = _project_slots("proj_far", h, w_in_f, far_slots, 4 * d, proj_in=proj)
    qn, kn, vb = _qkv_prepare(proj, q_norm_w, k_norm_w, half)
    buf_b = _land(fl_b, qn)
    buf_c = _land(fl_c, qn)
    (fl_b, fl_c), tok = _launch_groups(
        "agBC_ici1", [(buf_b, [gather_plans("B", len(buf_b), 1)]), (buf_c, [gather_plans("C", 1, 1)])])
    attn = _attention_forward(qn, kn, vb, deps=(tok,))
    buf_b = _land(fl_b, attn)
    buf_e = _land(fl_e, attn)
    (fl_b, fl_e), tok = _launch_groups(
        "agB_d2d_E_ici1", [(buf_b, [gather_plans("B", len(buf_b), 2)]), (buf_e, [gather_plans("E", 1, 1)])])
    w_pool_f, w_a_f, w_b_f, w_o_f = _land(fl_b, tok)
    rows_pool = cg // N_DEV
    w_pool_f = w_pool_f.reshape(N_DEV, n_groups, rows_pool, cg).transpose(1, 0, 2, 3).reshape(n_groups, cg, cg)
    w_o_f = w_o_f.reshape(d, d)
    pooled, ya_in = _pool_forward(proj, w_pool_f, pool_scale)

    def merge_epilogue(ga_ref, gb_ref, ya, yb, out_refs):
        merged_ref, ya_ref, yb_ref = out_refs
        merged = jax.nn.sigmoid(ga_ref[...]) * ya + jax.nn.sigmoid(gb_ref[...]) * yb
        merged_ref[...] = merged.astype(BF16)
        ya_ref[...] = ya.astype(BF16)
        yb_ref[...] = yb.astype(BF16)

    def up_body(a1_ref, b1_ref, a2_ref, b2_ref, ga_ref, gb_ref, *out_refs):
        merge_epilogue(ga_ref, gb_ref, _dot(a1_ref[...], b1_ref[...]), _dot(a2_ref[...], b2_ref[...]), out_refs)

    ga_blk0 = 2 * d // d8
    gb_blk0 = 3 * d // d8
    tu = s
    a_spec = pl.BlockSpec((tu, half), lambda i, j: (i, 0))
    wup_spec = pl.BlockSpec((None, half, d8), lambda i, j: (j, 0, 0))
    o_blk = pl.BlockSpec((tu, d8), lambda i, j: (i, j))
    merged, y_a, y_b = pl.pallas_call(
        up_body, name="up_merge", grid=(s // tu, N_DEV), out_shape=(_sds((s, d), BF16),) * 3,
        in_specs=[a_spec, wup_spec, a_spec, wup_spec,
                  pl.BlockSpec((tu, d8), lambda i, j: (i, ga_blk0 + j)),
                  pl.BlockSpec((tu, d8), lambda i, j: (i, gb_blk0 + j))],
        out_specs=(o_blk,) * 3, compiler_params=_params("parallel", "parallel"),
    )(ya_in, w_a_f, attn, w_b_f, proj, proj)
    buf_c = _land(fl_c, merged)
    fl_c, tok_c = gather_start("C", buf_c, 2)

    tn = _tile(d, 1024)

    def oproj_epilogue(acc, extra_refs, out_refs):
        x_ref, g_ref = extra_refs
        x1_ref, o_ref = out_refs
        x1_ref[...] = x_ref[...] + g_ref[...] * acc
        o_ref[...] = acc.astype(BF16)

    mn_blk = pl.BlockSpec((tm, tn), lambda i, j, k: (i, j))
    e_blk = pl.BlockSpec((tm, te), lambda i, j, k: (i, j))
    e_vec = pl.BlockSpec((1, te), lambda i, j, k: (0, j))
    s_blk = pl.BlockSpec((s, te), lambda i, j, k: (i, j))
    x1, o_act = _matmul(
        "oproj", "nn", (1, d // te, d // tk), merged, pl.BlockSpec((s, tk), lambda i, j, k: (i, k)),
        w_o_f, pl.BlockSpec((tk, te), lambda i, j, k: (k, j)),
        [_sds((s, d), F32), _sds((s, d), BF16)], [s_blk, s_blk], (s, te),
        epilogue=oproj_epilogue, extras=(x2, gate1), extra_specs=(s_blk, e_vec), deps=(tok_c,))

    h2 = _norm_forward("norm2_fwd", x1, norm2_w, scale2, shift2)
    w_ff1_f, = _land(fl_c, h2)

    def ff1_epilogue(acc, extra_refs, out_refs):
        r = jnp.maximum(acc, 0.0)
        out_refs[0][...] = r.astype(BF16)
        out_refs[1][...] = (r * r).astype(BF16)

    n_rows = s // tm
    first = max(n_rows // 2, 1)

    def ff1_rows(name, row0, rows, prior, deps):
        blk = pl.BlockSpec((tm, half), lambda i, j, k: (i + row0, j))
        return _matmul(
            name, "nn", (rows, N_DEV, d // tk), h2, pl.BlockSpec((tm, tk), lambda i, j, k: (i + row0, k)),
            w_ff1_f, pl.BlockSpec((None, tk, half), lambda i, j, k: (j, k, 0)),
            [_sds((s, 4 * d), BF16)] * 2, [blk, blk], (tm, half), epilogue=ff1_epilogue,
            extras=prior, extra_specs=[_ANY] * len(prior), aliases={2 + n: n for n in range(len(prior))}, deps=deps)

    relu, act = ff1_rows("ff1_a", 0, first, (), ())
    if n_rows > first:
        buf_e = _land(fl_e, act)
        fl_e, tok_e = gather_start("E", buf_e, 2)
        relu, act = ff1_rows("ff1_b", first, n_rows - first, (relu, act), (tok_e,))
    else:
        fl_e, tok_e = gather_start("E", _land(fl_e, act), 2)
    w_ff2_f, = _land(fl_e, act if n_rows > first else tok_e)
    w_ff2_f = w_ff2_f.reshape(4 * d, d)

    def ff2_epilogue(acc, extra_refs, out_refs):
        x1_ref, g_ref, t_ref = extra_refs
        df_ref, dy_ref, sq_ref, dgate_ref = out_refs
        gate = g_ref[...]
        err = x1_ref[...] + gate * acc - t_ref[...]
        dyv = err * (1.0 / d)
        dy_ref[...] = dyv
        df_ref[...] = (dyv * gate).astype(BF16)
        sq_ref[...] = jnp.full(sq_ref.shape, jnp.sum(err * err), F32)
        dgate_ref[...] = jnp.broadcast_to(jnp.sum(dyv * acc, axis=0, keepdims=True), dgate_ref.shape)

    df, dy, sq, dgate2_parts = _matmul(
        "ff2", "nn", (s // tm, d // te, 2 * d // tk), act, pl.BlockSpec((tm, 2 * tk), lambda i, j, k: (i, k)),
        w_ff2_f, pl.BlockSpec((2 * tk, te), lambda i, j, k: (k, j)),
        [_sds((s, d), BF16), _sds((s, d), F32), _sds((s // tm * SUBLANES, d // te * LANES), F32),
         _sds((s // tm * SUBLANES, d), F32)],
        [e_blk, e_blk, pl.BlockSpec((SUBLANES, LANES), lambda i, j, k: (i, j)),
         pl.BlockSpec((SUBLANES, te), lambda i, j, k: (i, j))], (tm, te),
        epilogue=ff2_epilogue, extras=(x1, gate2, target), extra_specs=(e_blk, e_vec, e_blk))
    loss_local = (0.5 / d) * jnp.sum(sq[::SUBLANES, ::LANES])
    dgate2 = jnp.sum(dgate2_parts[::SUBLANES], axis=0, keepdims=True)

    tok_k = _tile(s, 2048)
    tw = _tile(d, 1024)
    g_ff2 = _matmul(
        "g_ff2", "tn", (4 * d // tw, d // tn, s // tok_k), act, pl.BlockSpec((tok_k, tw), lambda i, j, k: (k, i)),
        df, pl.BlockSpec((tok_k, tn), lambda i, j, k: (k, j)),
        [_sds((4 * d, d), BF16)], [pl.BlockSpec((tw, tn), lambda i, j, k: (i, j))], (tw, tn))[0]

    def da_epilogue(acc, extra_refs, out_refs):
        out_refs[0][...] = (acc * (2.0 * extra_refs[0][...].astype(F32))).astype(BF16)

    big_blk = pl.BlockSpec((tm, tn), lambda i, j, k: (i, j))
    fl_f2, tok = _reduce_scatter_start("F2", [g_ff2.reshape(N_DEV, half, d)])
    df1 = _matmul(
        "da_ff", "nt", (s // tm, 4 * d // tn, d // tk), df, pl.BlockSpec((tm, tk), lambda i, j, k: (i, k)),
        w_ff2_f, pl.BlockSpec((tn, tk), lambda i, j, k: (j, k)),
        [_sds((s, 4 * d), BF16)], [big_blk], (tm, tn),
        epilogue=da_epilogue, extras=(relu,), extra_specs=(big_blk,), deps=(tok,))[0]

    buf_f2 = _reduce_scatter_add("F2", fl_f2, df1, me)
    g_ff1 = _matmul(
        "g_ff1", "tn", (d // tw, N_DEV, s // tok_k), h2, pl.BlockSpec((tok_k, tw), lambda i, j, k: (k, i)),
        df1, pl.BlockSpec((tok_k, half), lambda i, j, k: (k, j)),
        [_sds((N_DEV, d, half), BF16)], [pl.BlockSpec((None, tw, half), lambda i, j, k: (j, i, 0))], (tw, half))[0]

    fl_f2, fl_f1, tok = _reduce_scatter_handover("F2", buf_f2, "F1", [g_ff1])
    dh2 = _matmul(
        "dh2", "nt", (s // tm, d // tn, N_DEV // 2), df1, pl.BlockSpec((tm, 2 * half), lambda i, j, k: (i, k)),
        w_ff1_f, pl.BlockSpec((2, tn, half), lambda i, j, k: (k, j, 0)),
        [_sds((s, d), F32)], [mn_blk], (tm, tn), deps=(tok,))[0]

    buf_f1 = _reduce_scatter_add("F1", fl_f1, dh2, me)
    dx1, dshift2, dscale2, g_norm2, do, dgate1 = _norm_backward(
        "norm2_bwd", dh2, x1, norm2_w, scale2, dy, gated=(gate1, o_act))

    g_o = _matmul(
        "g_o", "tn", (d // tw, d // tn, s // tok_k), merged, pl.BlockSpec((tok_k, tw), lambda i, j, k: (k, i)),
        do, pl.BlockSpec((tok_k, tn), lambda i, j, k: (k, j)),
        [_sds((d, d), BF16)], [pl.BlockSpec((tw, tn), lambda i, j, k: (i, j))], (tw, tn))[0]

    def merge_bwd_epilogue(acc, extra_refs, out_refs):
        ga_ref, gb_ref, ya_ref, yb_ref = extra_refs
        dya_ref, dyb_ref, dga_ref, dgb_ref = out_refs
        sa = jax.nn.sigmoid(ga_ref[...])
        sb = jax.nn.sigmoid(gb_ref[...])
        dya_ref[...] = (acc * sa).astype(BF16)
        dyb_ref[...] = (acc * sb).astype(BF16)
        dga_ref[...] = (acc * ya_ref[...].astype(F32) * (sa * (1.0 - sa))).astype(BF16)
        dgb_ref[...] = (acc * yb_ref[...].astype(F32) * (sb * (1.0 - sb))).astype(BF16)

    td = _tile(d, 256)
    nb = d // td
    d_blk = pl.BlockSpec((s, td), lambda i, j, k: (i, j))
    dy_a, dy_b, dproj, dg_b = _matmul(
        "dmerged", "nt", (1, nb, d // tk), do, pl.BlockSpec((s, tk), lambda i, j, k: (i, k)),
        w_o_f, pl.BlockSpec((td, tk), lambda i, j, k: (j, k)),
        [_sds((s, d), BF16), _sds((s, d), BF16), _sds((s, 4 * d), BF16), _sds((s, d), BF16)],
        [d_blk, d_blk, pl.BlockSpec((s, td), lambda i, j, k: (i, 2 * nb + j)), d_blk], (s, td),
        epilogue=merge_bwd_epilogue, extras=(proj, proj, y_a, y_b),
        extra_specs=(pl.BlockSpec((s, td), lambda i, j, k: (i, 2 * nb + j)),
                     pl.BlockSpec((s, td), lambda i, j, k: (i, 3 * nb + j)), d_blk, d_blk))
    dproj = _place_columns("place_dgb", dg_b, dproj, 3)

    up_a = pl.BlockSpec((tok_k, half), lambda i, j, k: (k, 0))
    up_b = pl.BlockSpec((tok_k, d8), lambda i, j, k: (k, j))
    up_o = pl.BlockSpec((None, half, d8), lambda i, j, k: (j, 0, 0))
    g_a_up = _matmul("g_a_up", "tn", (1, N_DEV, s // tok_k), ya_in, up_a, dy_a, up_b,
                     [_sds((N_DEV, half, d8), BF16)], [up_o], (half, d8))[0]
    g_b_up = _matmul("g_b_up", "tn", (1, N_DEV, s // tok_k), attn, up_a, dy_b, up_b,
                     [_sds((N_DEV, half, d8), BF16)], [up_o], (half, d8))[0]
    slabs = 4
    dn_a = pl.BlockSpec((tm, slabs * d8), lambda i, j, k: (i, k))
    dn_b = pl.BlockSpec((slabs, half, d8), lambda i, j, k: (k, 0, 0))
    dn_o = pl.BlockSpec((tm, half), lambda i, j, k: (i, 0))
    dya_in = _matmul("d_ya_in", "nt", (s // tm, 1, N_DEV // slabs), dy_a, dn_a, w_a_f, dn_b,
                     [_sds((s, half), BF16)], [dn_o], (tm, half))[0]
    dattn = _matmul("d_attn", "nt", (s // tm, 1, N_DEV // slabs), dy_b, dn_a, w_b_f, dn_b,
                    [_sds((s, half), BF16)], [dn_o], (tm, half))[0]

    dproj, g_pool, g_pool_scale = _pool_backward(dya_in, pooled, w_pool_f, pool_scale, dproj)
    sum_ff2, = _reduce_scatter_finish(fl_f2, g_pool)
    g_pool_send = g_pool.astype(BF16).reshape(n_groups, N_DEV, rows_pool, cg).transpose(1, 0, 2, 3)
    g_pool_send = g_pool_send.reshape(N_DEV, n_groups * rows_pool, cg)
    fl_f1, fl_b, tok = _reduce_scatter_handover(
        "F1", buf_f1, "B", [g_pool_send, g_a_up, g_b_up, g_o.reshape(N_DEV, d8, d)])
    dqn, dkn, dproj = _attention_backward(qn, kn, vb, dattn, dproj, 3, deps=(tok,))
    sum_ff1, = _reduce_scatter_finish(fl_f1, dqn)
    buf_b = _reduce_scatter_add("B", fl_b, dqn, me)
    dproj, g_qnorm = _qk_norm_backward("qnorm_bwd", dqn, proj, 1, q_norm_w, dproj, half)
    dproj, g_knorm = _qk_norm_backward("knorm_bwd", dkn, proj, 2, k_norm_w, dproj, half)

    g_in = _matmul(
        "g_in", "tn", (d // tw, N_DEV, s // tok_k), h, pl.BlockSpec((tok_k, tw), lambda i, j, k: (k, i)),
        dproj, pl.BlockSpec((tok_k, half), lambda i, j, k: (k, j)),
        [_sds((N_DEV, d, half), BF16)], [pl.BlockSpec((None, tw, half), lambda i, j, k: (j, i, 0))], (tw, half))[0]
    fl_b, fl_in, tok = _reduce_scatter_handover("B", buf_b, "I", [g_in])
    dh = _matmul(
        "dh", "nt", (s // tm, d // tn, N_DEV // 2), dproj, pl.BlockSpec((tm, 2 * half), lambda i, j, k: (i, k)),
        w_in_f, pl.BlockSpec((2, tn, half), lambda i, j, k: (k, j, 0)),
        [_sds((s, d), F32)], [mn_blk], (tm, tn), deps=(tok,))[0]
    buf_in = _reduce_scatter_add("I", fl_in, dh, me, n_land=2)
    grad_x, dshift1, dscale1, g_norm1 = _norm_backward("norm1_bwd", dh, x2, norm1_w, scale1, dx1, deps=tuple(buf_in[:1]))
    sum_pool, sum_a_up, sum_b_up, sum_o = _reduce_scatter_finish(fl_b, grad_x)

    dmod = jnp.concatenate([dshift1, dscale1, dgate1, dshift2, dscale2, dgate2], axis=1)
    pieces = [dmod, g_norm1, g_norm2, g_pool_scale, g_qnorm, g_knorm, jnp.full((1, LANES), loss_local, F32)]
    packed_rows = [_rows_of_lanes(p) for p in pieces]
    offsets = [0]
    for p in packed_rows:
        offsets.append(offsets[-1] + p.shape[0])
    packed = jnp.concatenate(packed_rows, axis=0)
    small_buf = lax.dynamic_update_slice(jnp.zeros((N_DEV,) + packed.shape, F32), packed[None], (me, 0, 0))
    (fl_small, fl_in), tok = _launch_groups(
        "rsI_relay0_small", [([small_buf], [("ag_small", _plan_everyone, N_DEV - 1)]),
                             (buf_in, [("rsI_relay0", _plan_scatter_relay_first, 4)])])
    small_all, = _land(fl_small, tok)
    small_sum = _sum_slots("small_sum", small_all[None], F32)[0]

    def unpack(i, width):
        return small_sum[offsets[i]:offsets[i] + width // LANES].reshape(1, width)

    g_b_ada = unpack(0, N_MOD * d)
    g_norm1_w = unpack(1, d)
    g_norm2_w = unpack(2, d)
    g_pool_scale_w = unpack(3, half)
    g_q_norm_w = unpack(4, HEAD_DIM)
    g_k_norm_w = unpack(5, HEAD_DIM)
    loss = unpack(6, LANES)[0, 0]
    dmod_all = small_all[:, :N_MOD * d // LANES].reshape(N_DEV, N_MOD * d)
    dmod_cols = lax.dynamic_slice_in_dim(dmod_all, me * wa, wa, axis=1)

    grads = {
        "b_ada": g_b_ada, "norm1_w": g_norm1_w,
        "q_norm_w": g_q_norm_w, "k_norm_w": g_k_norm_w,
        "pool_scale": g_pool_scale_w, "norm2_w": g_norm2_w,
    }
    sums = {"w_pool": sum_pool, "w_a_up": sum_a_up, "w_b_up": sum_b_up, "w_o": sum_o,
            "w_ff1": sum_ff1, "w_ff2": sum_ff2}
    weights = {"w_ada": (w_ada, m_w_ada, v_w_ada), "b_ada": (b_ada, m_b_ada, v_b_ada),
               "norm1_w": (norm1_w, m_norm1_w, v_norm1_w), "w_in": (w_in, m_w_in, v_w_in),
               "q_norm_w": (q_norm_w, m_q_norm_w, v_q_norm_w), "k_norm_w": (k_norm_w, m_k_norm_w, v_k_norm_w),
               "w_pool": (w_pool, m_w_pool, v_w_pool), "pool_scale": (pool_scale, m_pool_scale, v_pool_scale),
               "w_a_up": (w_a_up, m_w_a_up, v_w_a_up), "w_b_up": (w_b_up, m_w_b_up, v_w_b_up),
               "w_o": (w_o, m_w_o, v_w_o), "norm2_w": (norm2_w, m_norm2_w, v_norm2_w),
               "w_ff1": (w_ff1, m_w_ff1, v_w_ff1), "w_ff2": (w_ff2, m_w_ff2, v_w_ff2)}
    order = list(weights)
    deltas, new_m, new_v = {}, {}, {}
    def adam(name, deps=()):
        wt, mt, vt = weights[name]
        shape = wt.shape
        flat = (-1, shape[-1])
        if name in sums:
            own, received = sums[name]
            g, dl, nm, nv = _adamw_summed("adamw_" + name, wt.reshape(flat), own, received,
                                          mt.reshape(flat), vt.reshape(flat), deps=deps)
            grads[name] = g.reshape(shape)
        else:
            dl, nm, nv = _adamw("adamw_" + name, wt.reshape(flat), grads[name].reshape(flat),
                                mt.reshape(flat), vt.reshape(flat), deps=deps)
        deltas[name], new_m[name], new_v[name] = dl.reshape(shape), nm.reshape(shape), nv.reshape(shape)

    behind_first = list(sums)
    for name in behind_first:
        adam(name, deps=(tok,))
    grads["w_ada"] = _ada_weight_grad(c_all, dmod_cols, deps=(tok,))[None]
    chip_sums, relayed = _land(fl_in, [deltas[n] for n in behind_first] + [grads["w_ada"]])
    passed_on = _add_relayed("rsI_add_relay", chip_sums, relayed)
    fl_in, tok = _launch("rsI_relay1", [passed_on, lax.empty((1,) + passed_on.shape[1:], passed_on.dtype)],
                         _plan_scatter_relay_second, 2)
    behind_second = [n for n in order if n not in sums and n != "w_in"]
    for name in behind_second:
        adam(name, deps=(tok,))
    sums["w_in"] = tuple(_land(fl_in, [deltas[n] for n in behind_second]))
    adam("w_in")

    return (loss, grad_x[None], *[grads[n] for n in order], *[deltas[n] for n in order],
            *[new_m[n] for n in order], *[new_v[n] for n in order])
```

```python
import math

import jax
import jax.numpy as jnp
from jax import lax
from jax.experimental import pallas as pl
from jax.experimental.pallas import tpu as pltpu

F32 = jnp.float32
BF16 = jnp.bfloat16
MESH_AXES = ("x", "y", "c")
N_DEV = 8
HEAD_DIM = 128
ATTN_SCALE = 1.0 / math.sqrt(HEAD_DIM)
POOL_WINDOWS = (2, 4, 8, 16)
N_MOD = 6
NORM_EPS = 1e-6
LANES = 128
SUBLANES = 8
VMEM_LIMIT_BYTES = 56 * 1024 * 1024
Q_TILE = 256
K_TILE = 256
POOL_TILE = 512
HEADS_PER_STEP = 8
HEADS_PER_STEP_BWD = 4

ADAM_LR = 0.001
ADAM_B1 = 0.9
ADAM_B2 = 0.999
ADAM_EPS = 1e-08
ADAM_WD = 0.01
ADAM_STEP = 10

_NN = (((1,), (0,)), ((), ()))
_NT = (((1,), (1,)), ((), ()))
_TN = (((0,), (0,)), ((), ()))
_DIMS = {"nn": _NN, "nt": _NT, "tn": _TN}


def _dot(a, b, mode="nn"):
    return lax.dot_general(a, b, _DIMS[mode], preferred_element_type=F32)


def _params(*sem):
    return pltpu.CompilerParams(dimension_semantics=sem, vmem_limit_bytes=VMEM_LIMIT_BYTES)


def _tile(dim, pref, align=SUBLANES):
    for t in range(min(dim, pref), 0, -1):
        if dim % t == 0 and t % align == 0:
            return t
    return dim


def _group_index(axes):
    idx = 0
    for a in axes:
        idx = idx * 2 + lax.axis_index(a)
    return idx


def _peer_device(axes, k):
    coords = {a: lax.axis_index(a) for a in MESH_AXES}
    for pos, a in enumerate(axes):
        if (k >> (len(axes) - 1 - pos)) & 1:
            coords[a] = 1 - coords[a]
    return tuple(coords[a] for a in MESH_AXES)


_AXIS_BIT = {"x": 4, "y": 2, "c": 1}
_ANY = pl.BlockSpec(memory_space=pl.ANY)


def _device_xor(mask):
    return tuple(1 - lax.axis_index(a) if mask & _AXIS_BIT[a] else lax.axis_index(a) for a in MESH_AXES)


def _remote(src, dst, send_sem, recv_sem, mask):
    return pltpu.make_async_remote_copy(src_ref=src, dst_ref=dst, send_sem=send_sem, recv_sem=recv_sem,
                                        device_id=_device_xor(mask), device_id_type=pl.DeviceIdType.MESH)


_CHIP_MASKS = (0, _AXIS_BIT["y"], _AXIS_BIT["x"], _AXIS_BIT["x"] | _AXIS_BIT["y"])


def _add_received(name, own, own_slots, received, out_dtype):
    nj, r, c = received.shape
    tr = _tile(r, max(2 * SUBLANES, (1 << 20) // c), 2 * SUBLANES)

    def body(slots_ref, own_ref, rec_ref, o_ref):
        del slots_ref
        o_ref[...] = (own_ref[...].astype(F32) + rec_ref[...].astype(F32)).astype(o_ref.dtype)

    grid_spec = pltpu.PrefetchScalarGridSpec(
        num_scalar_prefetch=1, grid=(nj, r // tr),
        in_specs=[pl.BlockSpec((None, tr, c), lambda j, i, slots: (slots[j], i, 0)),
                  pl.BlockSpec((None, tr, c), lambda j, i, slots: (j, i, 0))],
        out_specs=pl.BlockSpec((None, tr, c), lambda j, i, slots: (j, i, 0)))
    return pl.pallas_call(
        body, name=name, grid_spec=grid_spec, out_shape=jax.ShapeDtypeStruct((nj, r, c), out_dtype),
        compiler_params=_params("parallel", "parallel"),
    )(own_slots, own, received)


_HBM = pl.BlockSpec(memory_space=pltpu.HBM)
_SEM = pl.BlockSpec(memory_space=pltpu.SEMAPHORE)
_DATAFLOW = pltpu.SideEffectType.DATAFLOW_SIDE_EFFECTING


def _launch_groups(name, groups, deps=()):
    bufs = [b for g_bufs, _ in groups for b in g_bufs]
    specs = [(len(g_bufs), spec) for g_bufs, g_plans in groups for spec in g_plans]
    nb, ns = len(bufs), len(specs)

    def body(*refs):
        sems = refs[nb + len(deps):nb + len(deps) + 2 * ns]
        me = _group_index(MESH_AXES)
        first, which = 0, 0
        for g_bufs, g_plans in groups:
            ins = refs[first:first + len(g_bufs)]
            for _, plan, n_copies in g_plans:
                copies = plan(ins, me)
                assert len(copies) == n_copies
                for n, (src, dst, mask) in enumerate(copies):
                    _remote(src, dst, sems[2 * which].at[n], sems[2 * which + 1].at[n], mask).start()
                which += 1
            first += len(g_bufs)
        refs[-1][...] = jnp.zeros_like(refs[-1])

    sem_shapes = [pltpu.SemaphoreType.DMA((n,)) for _, (_, _, n) in specs for _ in range(2)]
    outs = pl.pallas_call(
        body, name=name,
        out_shape=(*sem_shapes, *[pltpu.HBM(b.shape, b.dtype) for b in bufs],
                   jax.ShapeDtypeStruct((SUBLANES, LANES), F32)),
        in_specs=[_HBM] * nb + [_ANY] * len(deps),
        out_specs=(*[_SEM] * (2 * ns), *[_HBM] * nb, pl.BlockSpec(memory_space=pltpu.VMEM)),
        input_output_aliases={i: 2 * ns + i for i in range(nb)},
        compiler_params=pltpu.CompilerParams(has_side_effects=_DATAFLOW),
    )(*[pltpu.with_memory_space_constraint(b, pltpu.HBM) for b in bufs], *deps)
    flights, first, which = [], 0, 0
    for g_bufs, g_plans in groups:
        through = list(outs[2 * ns + first:2 * ns + first + len(g_bufs)])
        for land_name, plan, n_copies in g_plans:
            flights.append((land_name, plan, n_copies, outs[2 * which], outs[2 * which + 1], through))
            which += 1
        first += len(g_bufs)
    return flights, outs[-1]


def _launch(name, bufs, plan, n_copies, deps=()):
    (flight,), token = _launch_groups(name, [(bufs, [(name, plan, n_copies)])], deps)
    return flight, token


def _land(flight, after, bufs=None):
    name, plan, n_copies, send_sems, recv_sems, launched = flight
    bufs = launched if bufs is None else bufs
    nb = len(bufs)
    after = list(after) if isinstance(after, (list, tuple)) else [after]

    def body(*refs):
        ins = refs[:nb]
        s_sems, r_sems = refs[nb], refs[nb + 1]
        for n, (src, dst, mask) in enumerate(plan(ins, _group_index(MESH_AXES))):
            cp = _remote(src, dst, s_sems.at[n], r_sems.at[n], mask)
            cp.wait_send()
            cp.wait_recv()

    outs = pl.pallas_call(
        body, name=name + "_land",
        out_shape=tuple(pltpu.HBM(b.shape, b.dtype) for b in bufs),
        in_specs=[_HBM] * nb + [_SEM, _SEM] + [_ANY] * len(after), out_specs=tuple([_HBM] * nb),
        input_output_aliases={i: i for i in range(nb)},
        compiler_params=pltpu.CompilerParams(has_side_effects=_DATAFLOW),
    )(*bufs, send_sems, recv_sems, *after)
    return list(outs)


def _plan_gather_ici(phase):
    bx, by = _AXIS_BIT["x"], _AXIS_BIT["y"]

    def plan(refs, me):
        copies = []
        for ref in refs:
            half = ref.shape[1] // 2

            def piece(slot, color, mask, ref=ref, half=half):
                p = ref.at[slot, pl.ds(color * half, half)]
                return (p, p, mask)

            if phase == 0:
                copies += [piece(me, 0, bx), piece(me, 1, by)]
            else:
                copies += [piece(me, 0, by), piece(me ^ bx, 0, by), piece(me, 1, bx), piece(me ^ by, 1, bx)]
        return copies

    return plan


def _plan_d2d(masks):
    def plan(refs, me):
        return [(ref.at[me ^ m], ref.at[me ^ m], _AXIS_BIT["c"]) for ref in refs for m in masks]

    return plan


def _plan_gather_d2d(refs, me):
    return _plan_d2d(_CHIP_MASKS)(refs, me)


def _plan_neighbours(refs, me):
    return [(ref.at[me], ref.at[me], _AXIS_BIT[a]) for ref in refs for a in ("x", "y")]


def _plan_diagonal(refs, me):
    bx, by = _AXIS_BIT["x"], _AXIS_BIT["y"]
    copies = []
    for ref in refs:
        half = ref.shape[1] // 2
        lo = ref.at[me ^ bx, pl.ds(0, half)]
        hi = ref.at[me ^ by, pl.ds(half, half)]
        copies += [(lo, lo, by), (hi, hi, bx)]
    return copies


def _plan_scatter_d2d(refs, me):
    na = len(refs) // 2
    copies = []
    for a in range(na):
        for j, m in enumerate(_CHIP_MASKS):
            copies.append((refs[a].at[me ^ _AXIS_BIT["c"] ^ m], refs[na + a].at[j], _AXIS_BIT["c"]))
    return copies


def _plan_scatter_ici(refs, me):
    del me
    na = len(refs) // 2
    copies = []
    for a in range(na):
        for n, m in enumerate(_CHIP_MASKS[1:]):
            copies.append((refs[a].at[n + 1], refs[na + a].at[n], m))
    return copies


def _with_deps(body, n_in, deps):
    if not deps:
        return body

    def wrapped(*refs):
        return body(*refs[:n_in], *refs[n_in + len(deps):])

    return wrapped


def _reduce_scatter_start(tag, grads):
    lands = [lax.empty((len(_CHIP_MASKS),) + g.shape[1:], g.dtype) for g in grads]
    return _launch("rs%s_d2d" % tag, list(grads) + lands, _plan_scatter_d2d, len(_CHIP_MASKS) * len(grads))


def _reduce_scatter_handover(tag_done, bufs_done, tag_new, grads_new):
    lands = [lax.empty((len(_CHIP_MASKS),) + g.shape[1:], g.dtype) for g in grads_new]
    (fl_ici, fl_d2d), token = _launch_groups(
        "rs%s_ici_%s_d2d" % (tag_done, tag_new),
        [(bufs_done, [_reduce_scatter_plan(tag_done, bufs_done)]),
         (list(grads_new) + lands, [("rs%s_d2d" % tag_new, _plan_scatter_d2d, len(_CHIP_MASKS) * len(grads_new))])])
    return fl_ici, fl_d2d, token


def _reduce_scatter_add(tag, flight, after, me, n_land=len(_CHIP_MASKS) - 1):
    bufs = _land(flight, after)
    na = len(bufs) // 2
    own_slots = jnp.stack([me ^ m for m in _CHIP_MASKS]).astype(jnp.int32)
    sums = [_add_received("rs%s_add_d2d_%d" % (tag, a), bufs[a], own_slots, bufs[na + a], BF16) for a in range(na)]
    lands = [lax.empty((n_land,) + h.shape[1:], h.dtype) for h in sums]
    return sums + lands


def _plan_scatter_relay_first(refs, me):
    del me
    bx, by = _AXIS_BIT["x"], _AXIS_BIT["y"]
    na = len(refs) // 2
    copies = []
    for a in range(na):
        h, land = refs[a], refs[na + a]
        half = h.shape[1] // 2
        lo, hi = pl.ds(0, half), pl.ds(half, half)
        copies += [(h.at[1, lo], land.at[0, lo], by), (h.at[3, lo], land.at[1, lo], by),
                   (h.at[2, hi], land.at[0, hi], bx), (h.at[3, hi], land.at[1, hi], bx)]
    return copies


def _plan_scatter_relay_second(refs, me):
    del me
    na = len(refs) // 2
    copies = []
    for a in range(na):
        f, land = refs[a], refs[na + a]
        half = f.shape[1] // 2
        lo, hi = pl.ds(0, half), pl.ds(half, half)
        copies += [(f.at[1, lo], land.at[0, lo], _AXIS_BIT["x"]), (f.at[1, hi], land.at[0, hi], _AXIS_BIT["y"])]
    return copies


def _add_relayed(name, sums, received):
    _, r, c = sums.shape
    tr = _tile(r // 2, max(2 * SUBLANES, (1 << 20) // c), 2 * SUBLANES)
    n_half = (r // 2) // tr

    def body(own_ref, rec_ref, o_ref):
        o_ref[...] = (own_ref[...].astype(F32) + rec_ref[...].astype(F32)).astype(o_ref.dtype)

    def own_slot(j, i):
        return jnp.where(j == 0, 0, jnp.where(i < n_half, 2, 1))

    return pl.pallas_call(
        body, name=name, grid=(2, 2 * n_half), out_shape=jax.ShapeDtypeStruct((2, r, c), sums.dtype),
        in_specs=[pl.BlockSpec((None, tr, c), lambda j, i: (own_slot(j, i), i, 0)),
                  pl.BlockSpec((None, tr, c), lambda j, i: (j, i, 0))],
        out_specs=pl.BlockSpec((None, tr, c), lambda j, i: (j, i, 0)),
        compiler_params=_params("parallel", "parallel"),
    )(sums, received)


def _reduce_scatter_plan(tag, bufs):
    return ("rs%s_ici" % tag, _plan_scatter_ici, (len(_CHIP_MASKS) - 1) * (len(bufs) // 2))


def _reduce_scatter_middle(tag, flight, after, me):
    bufs = _reduce_scatter_add(tag, flight, after, me)
    name, plan, n_copies = _reduce_scatter_plan(tag, bufs)
    return _launch(name, bufs, plan, n_copies)


def _plan_everyone(refs, me):
    return [(ref.at[me], ref.at[me], m) for ref in refs for m in range(1, N_DEV)]


def _reduce_scatter_finish(flight, after):
    bufs = _land(flight, after)
    na = len(bufs) // 2
    return [(bufs[a], bufs[na + a]) for a in range(na)]


def _all_gather_2d(name, x, deps=()):
    r, c = x.shape

    def body(x_ref, out_ref, send_sems, recv_sems):
        me = _group_index(MESH_AXES)
        out_ref[me] = x_ref[...]
        copies = []
        for k in range(1, N_DEV):
            cp = pltpu.make_async_remote_copy(
                src_ref=x_ref, dst_ref=out_ref.at[me],
                send_sem=send_sems.at[k - 1], recv_sem=recv_sems.at[k - 1],
                device_id=_peer_device(MESH_AXES, k), device_id_type=pl.DeviceIdType.MESH)
            cp.start()
            copies.append(cp)
        for cp in copies:
            cp.wait()

    vmem = pl.BlockSpec(memory_space=pltpu.VMEM)
    return pl.pallas_call(
        _with_deps(body, 1, deps), name=name, out_shape=jax.ShapeDtypeStruct((N_DEV, r, c), x.dtype),
        in_specs=[vmem] + [_ANY] * len(deps), out_specs=vmem,
        scratch_shapes=[pltpu.SemaphoreType.DMA((N_DEV - 1,)), pltpu.SemaphoreType.DMA((N_DEV - 1,))],
    )(x, *deps)


def _sum_slots(name, buf, out_dtype):
    pre, n, r, c = buf.shape
    tr = _tile(r, max(SUBLANES * 2, (1 << 20) // c))

    def body(b_ref, o_ref):
        acc = b_ref[0].astype(F32)
        for q in range(1, n):
            acc = acc + b_ref[q].astype(F32)
        o_ref[...] = acc.astype(o_ref.dtype)

    return pl.pallas_call(
        body, name=name, grid=(pre, r // tr),
        out_shape=jax.ShapeDtypeStruct((pre, r, c), out_dtype),
        in_specs=[pl.BlockSpec((None, n, tr, c), lambda i, j: (i, 0, j, 0))],
        out_specs=pl.BlockSpec((None, tr, c), lambda i, j: (i, j, 0)),
        compiler_params=_params("parallel", "parallel"),
    )(buf)


def _matmul(name, mode, grid, a, a_spec, b, b_spec, out_shapes, out_specs, acc_shape,
            epilogue=None, extras=(), extra_specs=(), aliases=None, deps=()):
    nk = grid[2]
    n_extra = len(extras)
    n_out = len(out_shapes)

    def finish(acc, extra_refs, out_refs):
        if epilogue is None:
            out_refs[0][...] = acc.astype(out_refs[0].dtype)
        else:
            epilogue(acc, extra_refs, out_refs)

    def product(a_ref, b_ref):
        if len(b_ref.shape) == 2:
            return _dot(a_ref[...], b_ref[...], mode)
        width = a_ref.shape[1] // b_ref.shape[0]
        total = None
        for i in range(b_ref.shape[0]):
            part = _dot(a_ref[:, i * width:(i + 1) * width], b_ref[i], mode)
            total = part if total is None else total + part
        return total

    def body(*refs):
        a_ref, b_ref = refs[0], refs[1]
        extra_refs = refs[2:2 + n_extra]
        out_refs = refs[2 + n_extra:2 + n_extra + n_out]
        if nk == 1:
            finish(product(a_ref, b_ref), extra_refs, out_refs)
            return
        acc_ref = refs[-1]
        k = pl.program_id(2)

        @pl.when(k == 0)
        def _():
            acc_ref[...] = product(a_ref, b_ref)

        @pl.when((k > 0) & (k < nk - 1))
        def _():
            acc_ref[...] += product(a_ref, b_ref)

        @pl.when(k == nk - 1)
        def _():
            finish(acc_ref[...] + product(a_ref, b_ref), extra_refs, out_refs)

    scratch = [] if nk == 1 else [pltpu.VMEM(acc_shape, F32)]
    return pl.pallas_call(
        _with_deps(body, 2 + n_extra, deps), name=name, grid=grid, out_shape=tuple(out_shapes),
        in_specs=[a_spec, b_spec] + list(extra_specs) + [_ANY] * len(deps), out_specs=tuple(out_specs),
        scratch_shapes=scratch, input_output_aliases=aliases or {},
        compiler_params=_params("parallel", "parallel", "arbitrary"),
    )(a, b, *extras, *deps)


def _sds(shape, dtype):
    return jax.ShapeDtypeStruct(tuple(shape), dtype)


def _project_slots(name, h, w_full, slots, out_cols, proj_in=None, deps=()):
    s, d = h.shape
    _, _, wide = w_full.shape
    tm = _tile(s, 1024)
    n_in = 4 if proj_in is not None else 3

    def body(*refs):
        refs[-1][...] = _dot(refs[1][...], refs[2][...])

    grid_spec = pltpu.PrefetchScalarGridSpec(
        num_scalar_prefetch=1, grid=(s // tm, slots.shape[0]),
        in_specs=[pl.BlockSpec((tm, d), lambda i, j, sl: (i, 0)),
                  pl.BlockSpec((None, d, wide), lambda i, j, sl: (sl[j], 0, 0))]
        + [_ANY] * (n_in - 3 + len(deps)),
        out_specs=pl.BlockSpec((tm, wide), lambda i, j, sl: (i, sl[j])))
    extra = ([proj_in] if proj_in is not None else []) + list(deps)
    return pl.pallas_call(
        body, name=name, grid_spec=grid_spec, out_shape=_sds((s, out_cols), F32),
        input_output_aliases={3: 0} if proj_in is not None else {},
        compiler_params=_params("parallel", "arbitrary"),
    )(slots, h, w_full, *extra)


def _ada_forward(c_all, w_ada, b_shard):
    nb, d = c_all.shape
    w = w_ada.shape[1]
    tn = _tile(w, 512)

    def body(c_ref, w_ref, b_ref, o_ref):
        cv = c_ref[...]
        sc = cv * jax.nn.sigmoid(cv)
        o_ref[...] = jnp.dot(sc, w_ref[...], precision=lax.Precision.HIGHEST,
                             preferred_element_type=F32) + b_ref[...]

    return pl.pallas_call(
        body, name="ada_fwd", grid=(w // tn,), out_shape=_sds((nb, w), F32),
        in_specs=[pl.BlockSpec((nb, d), lambda j: (0, 0)), pl.BlockSpec((d, tn), lambda j: (0, j)),
                  pl.BlockSpec((1, tn), lambda j: (0, j))],
        out_specs=pl.BlockSpec((nb, tn), lambda j: (0, j)),
        compiler_params=_params("parallel"),
    )(c_all, w_ada, b_shard)


def _ada_weight_grad(c_all, dmod_cols, deps=()):
    nb, d = c_all.shape
    w = dmod_cols.shape[1]
    tn = _tile(w, 512)

    def body(c_ref, g_ref, o_ref):
        cv = c_ref[...]
        sc = cv * jax.nn.sigmoid(cv)
        o_ref[...] = lax.dot_general(sc, g_ref[...], _TN, precision=lax.Precision.HIGHEST,
                                     preferred_element_type=F32)

    return pl.pallas_call(
        _with_deps(body, 2, deps), name="ada_wgrad", grid=(w // tn,), out_shape=_sds((d, w), F32),
        in_specs=[pl.BlockSpec((nb, d), lambda j: (0, 0)), pl.BlockSpec((nb, tn), lambda j: (0, j))]
        + [_ANY] * len(deps),
        out_specs=pl.BlockSpec((d, tn), lambda j: (0, j)),
        compiler_params=_params("parallel"),
    )(c_all, dmod_cols, *deps)


def _norm_forward(name, x, norm_w, scale, shift, deps=()):
    s, d = x.shape
    tm = _tile(s, 256)

    def body(x_ref, w_ref, sc_ref, sh_ref, h_ref):
        xv = x_ref[...]
        r = lax.rsqrt(jnp.mean(xv * xv, axis=-1, keepdims=True) + NORM_EPS)
        h = (xv * r * w_ref[...]) * (1.0 + sc_ref[...]) + sh_ref[...]
        h_ref[...] = h.astype(BF16)

    vec = pl.BlockSpec((1, d), lambda i: (0, 0))
    row = pl.BlockSpec((tm, d), lambda i: (i, 0))
    return pl.pallas_call(
        _with_deps(body, 4, deps), name=name, grid=(s // tm,), out_shape=_sds((s, d), BF16),
        in_specs=[row, vec, vec, vec] + [_ANY] * len(deps), out_specs=row, compiler_params=_params("parallel"),
    )(x, norm_w, scale, shift, *deps)


def _norm_backward(name, dh, x, norm_w, scale, dres, gated=None, deps=()):
    s, d = x.shape
    tm = _tile(s, 256)
    n_in = 7 if gated else 5

    def body(*refs):
        dh_ref, x_ref, w_ref, sc_ref, dres_ref = refs[:5]
        dx_ref, dshift_ref, dscale_ref, dw_ref = refs[n_in:n_in + 4]
        sums = (dshift_ref, dscale_ref, dw_ref) + ((refs[n_in + 5],) if gated else ())

        @pl.when(pl.program_id(0) == 0)
        def _():
            for ref in sums:
                ref[...] = jnp.zeros_like(ref)

        xv = x_ref[...]
        g = dh_ref[...]
        r = lax.rsqrt(jnp.mean(xv * xv, axis=-1, keepdims=True) + NORM_EPS)
        n = xv * r
        gain = 1.0 + sc_ref[...]
        gn = g * n
        dshift_ref[...] += jnp.sum(g, axis=0, keepdims=True)
        dscale_ref[...] += jnp.sum(gn, axis=0, keepdims=True) * w_ref[...]
        dw_ref[...] += jnp.sum(gn, axis=0, keepdims=True) * gain
        dn = g * (w_ref[...] * gain)
        dx = dres_ref[...] + r * (dn - n * jnp.mean(dn * n, axis=-1, keepdims=True))
        dx_ref[...] = dx
        if gated:
            gate_ref, other_ref = refs[5:7]
            refs[n_in + 4][...] = (dx * gate_ref[...]).astype(BF16)
            refs[n_in + 5][...] += jnp.sum(dx * other_ref[...].astype(F32), axis=0, keepdims=True)

    vec = pl.BlockSpec((1, d), lambda i: (0, 0))
    row = pl.BlockSpec((tm, d), lambda i: (i, 0))
    vec_out = _sds((1, d), F32)
    return pl.pallas_call(
        _with_deps(body, n_in, deps), name=name, grid=(s // tm,),
        out_shape=(_sds((s, d), F32), vec_out, vec_out, vec_out) + ((_sds((s, d), BF16), vec_out) if gated else ()),
        in_specs=[row, row, vec, vec, row] + ([vec, row] if gated else []) + [_ANY] * len(deps),
        out_specs=(row, vec, vec, vec) + ((row, vec) if gated else ()),
        compiler_params=_params("arbitrary"),
    )(dh, x, norm_w, scale, dres, *(gated or ()), *deps)


def _split_bf16(v):
    hi = v.astype(BF16)
    lo = (v - hi.astype(F32)).astype(BF16)
    return hi, lo


def _pool_forward(proj, w_pool, pool_scale, deps=()):
    s = proj.shape[0]
    g_n, cg, _ = w_pool.shape
    t = POOL_TILE
    nt = s // t

    def body(cur_ref, prev_ref, wp_ref, sc_ref, pooled_ref, ya_ref):
        g = pl.program_id(0)
        ti = pl.program_id(1)
        win = jnp.left_shift(2, g)
        row = lax.broadcasted_iota(jnp.int32, (t, t), 0)
        col = lax.broadcasted_iota(jnp.int32, (t, t), 1)
        lag = row - col
        band_cur = ((lag >= 0) & (lag < win)).astype(BF16)
        band_prev = ((lag + t < win) & (ti > 0)).astype(BF16)
        u = cur_ref[...]
        u_hi, u_lo = _split_bf16(u)
        p_hi, p_lo = _split_bf16(prev_ref[...])
        wsum = (_dot(band_cur, u_hi) + _dot(band_cur, u_lo)
                + _dot(band_prev, p_hi) + _dot(band_prev, p_lo))
        tok = ti * t + lax.broadcasted_iota(jnp.int32, (t, 1), 0)
        count = jnp.minimum(tok + 1, win).astype(F32)
        pooled = (wsum / count - u).astype(BF16)
        pooled_ref[...] = pooled
        ya_ref[...] = (_dot(pooled, wp_ref[...]) * sc_ref[...]).astype(BF16)

    blk = pl.BlockSpec((t, cg), lambda g, i: (i, g))
    return pl.pallas_call(
        _with_deps(body, 4, deps), name="pool_fwd", grid=(g_n, nt),
        out_shape=(_sds((s, g_n * cg), BF16), _sds((s, g_n * cg), BF16)),
        in_specs=[blk, pl.BlockSpec((t, cg), lambda g, i: (jnp.maximum(i - 1, 0), g)),
                  pl.BlockSpec((None, cg, cg), lambda g, i: (g, 0, 0)),
                  pl.BlockSpec((1, cg), lambda g, i: (0, g))] + [_ANY] * len(deps),
        out_specs=(blk, blk), compiler_params=_params("parallel", "parallel"),
    )(proj, proj, w_pool, pool_scale, *deps)


def _pool_backward(dya, pooled, w_pool, pool_scale, dproj):
    s = dya.shape[0]
    g_n, cg, _ = w_pool.shape
    t = POOL_TILE
    nt = s // t

    def body(dya_ref, dya_next_ref, pooled_ref, wp_ref, sc_ref, dproj_in, du_ref, gw_ref, gs_ref):
        del dproj_in
        g = pl.program_id(0)
        ti = pl.program_id(1)

        @pl.when(ti == 0)
        def _():
            gw_ref[...] = jnp.zeros_like(gw_ref)
            gs_ref[...] = jnp.zeros_like(gs_ref)

        win = jnp.left_shift(2, g)
        wp = wp_ref[...]
        sc = sc_ref[...]
        pooled_v = pooled_ref[...]
        dya_v = dya_ref[...].astype(F32)
        mixed = _dot(pooled_v, wp)
        gs_ref[...] += jnp.sum(dya_v * mixed, axis=0, keepdims=True)
        dmixed = (dya_v * sc).astype(BF16)
        gw_ref[...] += _dot(pooled_v, dmixed, "tn")
        dpooled = _dot(dmixed, wp, "nt")
        dmixed_next = (dya_next_ref[...].astype(F32) * sc).astype(BF16)
        dpooled_next = _dot(dmixed_next, wp, "nt")
        tok = ti * t + lax.broadcasted_iota(jnp.int32, (t, 1), 0)
        e_cur = dpooled / jnp.minimum(tok + 1, win).astype(F32)
        e_next = dpooled_next / jnp.minimum(tok + t + 1, win).astype(F32)
        row = lax.broadcasted_iota(jnp.int32, (t, t), 0)
        col = lax.broadcasted_iota(jnp.int32, (t, t), 1)
        lead = col - row
        band_cur = ((lead >= 0) & (lead < win)).astype(BF16)
        band_next = ((lead + t < win) & (ti < nt - 1)).astype(BF16)
        c_hi, c_lo = _split_bf16(e_cur)
        n_hi, n_lo = _split_bf16(e_next)
        du = (_dot(band_cur, c_hi) + _dot(band_cur, c_lo)
              + _dot(band_next, n_hi) + _dot(band_next, n_lo)) - dpooled
        du_ref[...] = du.astype(BF16)

    blk = pl.BlockSpec((t, cg), lambda g, i: (i, g))
    du, gw, gs = pl.pallas_call(
        body, name="pool_bwd", grid=(g_n, nt),
        out_shape=(_sds(dproj.shape, BF16), _sds((g_n, cg, cg), F32), _sds((1, g_n * cg), F32)),
        in_specs=[blk, pl.BlockSpec((t, cg), lambda g, i: (jnp.minimum(i + 1, nt - 1), g)), blk,
                  pl.BlockSpec((None, cg, cg), lambda g, i: (g, 0, 0)),
                  pl.BlockSpec((1, cg), lambda g, i: (0, g)),
                  pl.BlockSpec(memory_space=pl.ANY)],
        out_specs=(blk, pl.BlockSpec((None, cg, cg), lambda g, i: (g, 0, 0)),
                   pl.BlockSpec((1, cg), lambda g, i: (0, g))),
        input_output_aliases={5: 0}, compiler_params=_params("parallel", "arbitrary"),
    )(dya, dya, pooled, w_pool, pool_scale, dproj)
    return du, gw, gs


def _qkv_prepare(proj, q_norm_w, k_norm_w, width, deps=()):
    s = proj.shape[0]
    tm = _tile(s, 256)
    heads = width // HEAD_DIM

    def body(q_ref, k_ref, v_ref, qw_ref, kw_ref, qn_ref, kn_ref, vb_ref):
        for h in range(heads):
            cols = slice(h * HEAD_DIM, (h + 1) * HEAD_DIM)
            for src, w_ref, dst, gain in ((q_ref, qw_ref, qn_ref, ATTN_SCALE), (k_ref, kw_ref, kn_ref, 1.0)):
                v = src[:, cols]
                r = lax.rsqrt(jnp.mean(v * v, axis=-1, keepdims=True) + NORM_EPS)
                dst[:, cols] = (v * (r * gain) * w_ref[...]).astype(BF16)
        vb_ref[...] = v_ref[...].astype(BF16)

    vec = pl.BlockSpec((1, HEAD_DIM), lambda i: (0, 0))
    out_spec = pl.BlockSpec((tm, width), lambda i: (i, 0))
    return pl.pallas_call(
        _with_deps(body, 5, deps), name="qkv_prep", grid=(s // tm,),
        out_shape=(_sds((s, width), BF16),) * 3,
        in_specs=[pl.BlockSpec((tm, width), lambda i: (i, 1)), pl.BlockSpec((tm, width), lambda i: (i, 2)),
                  pl.BlockSpec((tm, width), lambda i: (i, 3)), vec, vec] + [_ANY] * len(deps),
        out_specs=(out_spec,) * 3, compiler_params=_params("parallel"),
    )(proj, proj, proj, q_norm_w, k_norm_w, *deps)


def _qk_norm_backward(name, dn, proj, col_block, norm_w, dproj, width, deps=()):
    s = proj.shape[0]
    tm = _tile(s, 256)
    heads = width // HEAD_DIM

    def body(dn_ref, q_ref, w_ref, dproj_in, dq_ref, gw_ref):
        del dproj_in

        @pl.when(pl.program_id(0) == 0)
        def _():
            gw_ref[...] = jnp.zeros_like(gw_ref)

        wv = w_ref[...]
        gw = jnp.zeros((1, HEAD_DIM), F32)
        for h in range(heads):
            cols = slice(h * HEAD_DIM, (h + 1) * HEAD_DIM)
            v = q_ref[:, cols]
            g = dn_ref[:, cols]
            r = lax.rsqrt(jnp.mean(v * v, axis=-1, keepdims=True) + NORM_EPS)
            n = v * r
            gw = gw + jnp.sum(g * n, axis=0, keepdims=True)
            gn = g * wv
            dq_ref[:, cols] = (r * (gn - n * jnp.mean(gn * n, axis=-1, keepdims=True))).astype(BF16)
        gw_ref[...] += gw

    blk = pl.BlockSpec((tm, width), lambda i: (i, col_block))
    return pl.pallas_call(
        _with_deps(body, 4, deps), name=name, grid=(s // tm,),
        out_shape=(_sds(dproj.shape, BF16), _sds((1, HEAD_DIM), F32)),
        in_specs=[pl.BlockSpec((tm, width), lambda i: (i, 0)), blk,
                  pl.BlockSpec((1, HEAD_DIM), lambda i: (0, 0)), pl.BlockSpec(memory_space=pl.ANY)]
        + [_ANY] * len(deps),
        out_specs=(blk, pl.BlockSpec((1, HEAD_DIM), lambda i: (0, 0))),
        input_output_aliases={3: 0}, compiler_params=_params("arbitrary"),
    )(dn, proj, norm_w, dproj, *deps)


def _strict_upper(n):
    row = lax.broadcasted_iota(jnp.int32, (n, n), 0)
    col = lax.broadcasted_iota(jnp.int32, (n, n), 1)
    return (row > col).astype(BF16)


def _strict_lower(n):
    row = lax.broadcasted_iota(jnp.int32, (n, n), 0)
    col = lax.broadcasted_iota(jnp.int32, (n, n), 1)
    return (row < col).astype(BF16)


def _cumulate(v, tri):
    return _dot(v.astype(BF16), tri)


def _log_sigmoid(z):
    return jnp.minimum(z, 0.0) - jnp.log(1.0 + jnp.exp(-jnp.abs(z)))


def _attention_forward(qn, kn, vb, deps=()):
    s, width = qn.shape
    heads = width // HEAD_DIM
    tq, tk = Q_TILE, K_TILE
    hp = min(HEADS_PER_STEP, heads)
    assert tq == tk and s % tq == 0 and heads % hp == 0

    def body(q_ref, k_ref, v_ref, o_ref, a_scr):
        qi = pl.program_id(1)
        upper = _strict_upper(tk)
        causal = lax.broadcasted_iota(jnp.int32, (tq, tk), 1) < lax.broadcasted_iota(jnp.int32, (tq, tk), 0)
        head_cols = [slice(u * HEAD_DIM, (u + 1) * HEAD_DIM) for u in range(hp)]

        def weights(kb, carry, masked):
            rows = pl.ds(pl.multiple_of(kb * tk, tk), tk)
            out = []
            for u, cols in enumerate(head_cols):
                later = carry[u]
                z = _dot(q_ref[:, cols], k_ref[rows, cols], "nt")
                log_beta = _log_sigmoid(z)
                l = log_beta - z
                if masked:
                    l = jnp.where(causal, l, 0.0)
                a = jnp.exp(log_beta + _cumulate(l, upper) + later)
                if masked:
                    a = jnp.where(causal, a, 0.0)
                a_scr[u, :, rows] = a.astype(BF16)
                out.append(later + jnp.sum(l, axis=1, keepdims=True))
            return tuple(out)

        later = weights(qi, tuple(jnp.zeros((tq, 1), F32) for _ in range(hp)), True)
        lax.fori_loop(0, qi, lambda i, c: weights(qi - 1 - i, c, False), later)

        def mix(kb, accs):
            rows = pl.ds(pl.multiple_of(kb * tk, tk), tk)
            return tuple(acc + _dot(a_scr[u, :, rows], v_ref[rows, cols])
                         for u, (acc, cols) in enumerate(zip(accs, head_cols)))

        accs = lax.fori_loop(0, qi + 1, mix, tuple(jnp.zeros((tq, HEAD_DIM), F32) for _ in range(hp)))
        for acc, cols in zip(accs, head_cols):
            o_ref[:, cols] = acc.astype(BF16)

    full = pl.BlockSpec((s, hp * HEAD_DIM), lambda h, i: (0, h))
    blk = pl.BlockSpec((tq, hp * HEAD_DIM), lambda h, i: (i, h))
    return pl.pallas_call(
        _with_deps(body, 3, deps), name="attn_fwd", grid=(heads // hp, s // tq), out_shape=_sds((s, width), BF16),
        in_specs=[blk, full, full] + [_ANY] * len(deps), out_specs=blk,
        scratch_shapes=[pltpu.VMEM((hp, tq, s), BF16)],
        compiler_params=_params("parallel", "parallel"),
    )(qn, kn, vb, *deps)


def _attention_backward(qn, kn, vb, dout, dproj, v_col_block, deps=()):
    s, width = qn.shape
    heads = width // HEAD_DIM
    tq, tk = Q_TILE, K_TILE
    hp = min(HEADS_PER_STEP_BWD, heads)
    nq = s // tq
    v_block0 = v_col_block * (heads // hp)

    def body(q_ref, k_ref, v_ref, do_ref, dproj_in, dq_ref, dk_ref, dv_ref,
             a_scr, lb_scr, dz_scr, dkt_scr, dvt_scr):
        del dproj_in
        qi = pl.program_id(1)

        @pl.when(qi == 0)
        def _():
            dkt_scr[...] = jnp.zeros_like(dkt_scr)
            dvt_scr[...] = jnp.zeros_like(dvt_scr)

        upper = _strict_upper(tk)
        lower = _strict_lower(tk)
        causal = lax.broadcasted_iota(jnp.int32, (tq, tk), 1) < lax.broadcasted_iota(jnp.int32, (tq, tk), 0)
        head_cols = [slice(u * HEAD_DIM, (u + 1) * HEAD_DIM) for u in range(hp)]

        def weights(kb, carry, masked):
            rows = pl.ds(pl.multiple_of(kb * tk, tk), tk)
            out = []
            for u, cols in enumerate(head_cols):
                later = carry[u]
                z = _dot(q_ref[:, cols], k_ref[rows, cols], "nt")
                log_beta = _log_sigmoid(z)
                l = log_beta - z
                if masked:
                    l = jnp.where(causal, l, 0.0)
                a = jnp.exp(log_beta + _cumulate(l, upper) + later)
                if masked:
                    a = jnp.where(causal, a, 0.0)
                a_scr[u, :, rows] = a
                lb_scr[u, :, rows] = log_beta
                out.append(later + jnp.sum(l, axis=1, keepdims=True))
            return tuple(out)

        zeros = tuple(jnp.zeros((tq, 1), F32) for _ in range(hp))
        later = weights(qi, zeros, True)
        lax.fori_loop(0, qi, lambda i, c: weights(qi - 1 - i, c, False), later)

        q_t = [jnp.transpose(q_ref[:, cols].astype(F32)).astype(BF16) for cols in head_cols]
        do_t = [jnp.transpose(do_ref[:, cols].astype(F32)).astype(BF16) for cols in head_cols]

        def scores(kb, carry, masked):
            rows = pl.ds(pl.multiple_of(kb * tk, tk), tk)
            out = []
            for u, cols in enumerate(head_cols):
                before = carry[u]
                beta = jnp.exp(lb_scr[u, :, rows])
                g = a_scr[u, :, rows] * _dot(do_ref[:, cols], v_ref[rows, cols], "nt")
                p = _cumulate(g, lower) + before
                dz = g - (g + p) * beta
                if masked:
                    dz = jnp.where(causal, dz, 0.0)
                dz_scr[u, :, rows] = dz.astype(BF16)
                out.append(before + jnp.sum(g, axis=1, keepdims=True))
            return tuple(out)

        before = lax.fori_loop(0, qi, lambda i, c: scores(i, c, False), zeros)
        scores(qi, before, True)

        def products(kb, dqs):
            rows = pl.ds(pl.multiple_of(kb * tk, tk), tk)
            out = []
            for u, cols in enumerate(head_cols):
                dz = dz_scr[u, :, rows]
                dkt_scr[cols, rows] += _dot(q_t[u], dz)
                dvt_scr[cols, rows] += _dot(do_t[u], a_scr[u, :, rows].astype(BF16))
                out.append(dqs[u] + _dot(dz, k_ref[rows, cols]))
            return tuple(out)

        dqs = lax.fori_loop(0, qi + 1, products, tuple(jnp.zeros((tq, HEAD_DIM), F32) for _ in range(hp)))
        for u, cols in enumerate(head_cols):
            dq_ref[:, cols] = dqs[u] * ATTN_SCALE

        @pl.when(qi == nq - 1)
        def _():
            dk_ref[...] = jnp.transpose(dkt_scr[...])
            dv_ref[...] = jnp.transpose(dvt_scr[...]).astype(BF16)

    wide = hp * HEAD_DIM
    full = pl.BlockSpec((s, wide), lambda h, i: (0, h))
    blk = pl.BlockSpec((tq, wide), lambda h, i: (i, h))
    return pl.pallas_call(
        _with_deps(body, 5, deps), name="attn_bwd", grid=(heads // hp, nq),
        out_shape=(_sds((s, width), F32), _sds((s, width), F32), _sds(dproj.shape, BF16)),
        in_specs=[blk, full, full, blk, pl.BlockSpec(memory_space=pl.ANY)] + [_ANY] * len(deps),
        out_specs=(blk, full, pl.BlockSpec((s, wide), lambda h, i: (0, v_block0 + h))),
        scratch_shapes=[pltpu.VMEM((hp, tq, s), F32), pltpu.VMEM((hp, tq, s), F32), pltpu.VMEM((hp, tq, s), BF16),
                        pltpu.VMEM((wide, s), F32), pltpu.VMEM((wide, s), F32)],
        input_output_aliases={4: 2}, compiler_params=_params("parallel", "arbitrary"),
    )(qn, kn, vb, dout, dproj, *deps)


def _place_columns(name, src, dst, col_block):
    s, w = src.shape
    tm = _tile(s, 512)

    def body(src_ref, dst_in, out_ref):
        del dst_in
        out_ref[...] = src_ref[...]

    return pl.pallas_call(
        body, name=name, grid=(s // tm,), out_shape=_sds(dst.shape, dst.dtype),
        in_specs=[pl.BlockSpec((tm, w), lambda i: (i, 0)), pl.BlockSpec(memory_space=pl.ANY)],
        out_specs=pl.BlockSpec((tm, w), lambda i: (i, col_block)),
        input_output_aliases={1: 0}, compiler_params=_params("parallel"),
    )(src, dst)


def _cast_into_slot(name, x, slot):
    r, c = x.shape
    tr = _tile(r, max(SUBLANES * 2, (1 << 20) // c), SUBLANES * 2)

    def body(slot_ref, x_ref, o_ref):
        del slot_ref
        o_ref[...] = x_ref[...].astype(BF16)

    grid_spec = pltpu.PrefetchScalarGridSpec(
        num_scalar_prefetch=1, grid=(r // tr,),
        in_specs=[pl.BlockSpec((tr, c), lambda i, slot_ref: (i, 0))],
        out_specs=pl.BlockSpec((None, tr, c), lambda i, slot_ref: (slot_ref[0], i, 0)))
    return pl.pallas_call(
        body, name=name, grid_spec=grid_spec, out_shape=_sds((N_DEV, r, c), BF16),
        compiler_params=_params("parallel"),
    )(slot, x)


def _adamw_update(gv, w_ref, m_ref, v_ref, d_ref, nm_ref, nv_ref):
    c1 = 1.0 - ADAM_B1 ** ADAM_STEP
    c2 = 1.0 - ADAM_B2 ** ADAM_STEP
    nm = ADAM_B1 * m_ref[...] + (1.0 - ADAM_B1) * gv
    nv = ADAM_B2 * v_ref[...] + (1.0 - ADAM_B2) * (gv * gv)
    d_ref[...] = -ADAM_LR * ((nm / c1) / (jnp.sqrt(nv / c2) + ADAM_EPS) + ADAM_WD * w_ref[...])
    nm_ref[...] = nm
    nv_ref[...] = nv


def _adamw(name, w, g, m, v, deps=()):
    r, c = w.shape
    tr = _tile(r, max(SUBLANES, (1 << 19) // c))

    def body(w_ref, g_ref, m_ref, v_ref, d_ref, nm_ref, nv_ref):
        _adamw_update(g_ref[...], w_ref, m_ref, v_ref, d_ref, nm_ref, nv_ref)

    blk = pl.BlockSpec((tr, c), lambda i: (i, 0))
    return pl.pallas_call(
        _with_deps(body, 4, deps), name=name, grid=(r // tr,), out_shape=(_sds((r, c), F32),) * 3,
        in_specs=[blk] * 4 + [_ANY] * len(deps), out_specs=(blk,) * 3, compiler_params=_params("parallel"),
    )(w, g, m, v, *deps)


def _adamw_streamed(name, w, g, m, v, deps=()):
    r, c = w.shape
    tr = _tile(r, max(SUBLANES, (1 << 19) // (2 * c)))
    n = r // tr
    n_in, n_out, depth = 4, 3, 3

    def body(*refs):
        ins, outs = refs[:n_in], refs[n_in + len(deps):n_in + len(deps) + n_out]
        in_buf, out_buf, in_sem, out_sem = refs[n_in + len(deps) + n_out:]

        def fetch(i):
            return [pltpu.make_async_copy(ins[a].at[pl.ds(i * tr, tr)], in_buf.at[i % depth, a], in_sem.at[i % depth, a])
                    for a in range(n_in)]

        def store(i):
            return [pltpu.make_async_copy(out_buf.at[i % 2, a], outs[a].at[pl.ds(i * tr, tr)], out_sem.at[i % 2, a])
                    for a in range(n_out)]

        for i in range(min(depth - 1, n)):
            for cp in fetch(i):
                cp.start()
        for i in range(n):
            if i + depth - 1 < n:
                for cp in fetch(i + depth - 1):
                    cp.start()
            for cp in fetch(i):
                cp.wait()
            if i >= 2:
                for cp in store(i - 2):
                    cp.wait()
            slot, oslot = i % depth, i % 2
            _adamw_update(in_buf[slot, 1], in_buf.at[slot, 0], in_buf.at[slot, 2], in_buf.at[slot, 3],
                          out_buf.at[oslot, 0], out_buf.at[oslot, 1], out_buf.at[oslot, 2])
            for cp in store(i):
                cp.start()
        for i in range(max(n - 2, 0), n):
            for cp in store(i):
                cp.wait()

    return pl.pallas_call(
        body, name=name, out_shape=(_sds((r, c), F32),) * 3,
        in_specs=[_ANY] * (n_in + len(deps)), out_specs=(_ANY,) * n_out,
        scratch_shapes=[pltpu.VMEM((depth, n_in, tr, c), F32), pltpu.VMEM((2, n_out, tr, c), F32),
                        pltpu.SemaphoreType.DMA((depth, n_in)), pltpu.SemaphoreType.DMA((2, n_out))],
        compiler_params=pltpu.CompilerParams(vmem_limit_bytes=VMEM_LIMIT_BYTES),
    )(w, g, m, v, *deps)


def _adamw_summed(name, w, own, received, m, v, deps=()):
    r, c = w.shape
    nj = received.shape[0]
    tr = _tile(r, max(2 * SUBLANES, (1 << 19) // c), 2 * SUBLANES)

    def body(w_ref, own_ref, rec_ref, m_ref, v_ref, g_ref, d_ref, nm_ref, nv_ref):
        gv = own_ref[...].astype(F32)
        for j in range(nj):
            gv = gv + rec_ref[j].astype(F32)
        g_ref[...] = gv
        _adamw_update(gv, w_ref, m_ref, v_ref, d_ref, nm_ref, nv_ref)

    blk = pl.BlockSpec((tr, c), lambda i: (i, 0))
    return pl.pallas_call(
        _with_deps(body, 5, deps), name=name, grid=(r // tr,), out_shape=(_sds((r, c), F32),) * 4,
        in_specs=[blk, pl.BlockSpec((None, tr, c), lambda i: (0, i, 0)),
                  pl.BlockSpec((nj, tr, c), lambda i: (0, i, 0)), blk, blk] + [_ANY] * len(deps),
        out_specs=(blk,) * 4, compiler_params=_params("parallel"),
    )(w, own, received, m, v, *deps)


def _rows_of_lanes(v):
    rows = v.shape[1] // LANES
    out = v.reshape(rows, LANES)
    pad = (-rows) % SUBLANES
    if pad:
        out = jnp.pad(out, ((0, pad), (0, 0)))
    return out


def kernel(x, c, w_ada, b_ada, norm1_w, w_in, q_norm_w, k_norm_w, w_pool, pool_scale, w_a_up, w_b_up, w_o, norm2_w, w_ff1, w_ff2, loss_target, m_w_ada, m_b_ada, m_norm1_w, m_w_in, m_q_norm_w, m_k_norm_w, m_w_pool, m_pool_scale, m_w_a_up, m_w_b_up, m_w_o, m_norm2_w, m_w_ff1, m_w_ff2, v_w_ada, v_b_ada, v_norm1_w, v_w_in, v_q_norm_w, v_k_norm_w, v_w_pool, v_pool_scale, v_w_a_up, v_w_b_up, v_w_o, v_norm2_w, v_w_ff1, v_w_ff2):
    _, s, d = x.shape
    half = d // 2
    d8 = d // N_DEV
    n_groups = len(POOL_WINDOWS)
    cg = half // n_groups
    me = _group_index(MESH_AXES)

    x2 = x[0]
    target = loss_target[0]

    my_slot = jnp.reshape(me, (1,)).astype(jnp.int32)

    def cast(i, t):
        return _cast_into_slot("cast_w%d" % i, t, my_slot)

    def gather_start(tag, bufs, phase):
        if phase < 2:
            return _launch("ag%s_ici%d" % (tag, phase), bufs, _plan_gather_ici(phase), (2, 4)[phase] * len(bufs))
        return _launch("ag%s_d2d" % tag, bufs, _plan_gather_d2d, len(_CHIP_MASKS) * len(bufs))

    def gather_plans(tag, n_bufs, phase):
        if phase < 2:
            return ("ag%s_ici%d" % (tag, phase), _plan_gather_ici(phase), (2, 4)[phase] * n_bufs)
        return ("ag%s_d2d" % tag, _plan_gather_d2d, len(_CHIP_MASKS) * n_bufs)

    bx, by, bc = _AXIS_BIT["x"], _AXIS_BIT["y"], _AXIS_BIT["c"]
    buf_a = [cast(0, w_in[0])]

    c_all = _all_gather_2d("ag_c", c.reshape(d // LANES, LANES), deps=tuple(buf_a)).reshape(N_DEV, d)
    wa = w_ada.shape[2]
    b_shard = lax.dynamic_slice_in_dim(b_ada, me * wa, wa, axis=1)
    mod_part = _ada_forward(c_all, w_ada[0], b_shard)
    mod_rows = mod_part.reshape(N_DEV * wa // LANES, LANES)
    mod_buf = lax.dynamic_update_slice(jnp.zeros((N_DEV,) + mod_rows.shape, F32), mod_rows[None], (me, 0, 0))
    (fl_mod, fl_a, fl_pair), tok = _launch_groups(
        "agA_near", [([mod_buf], [("ag_mod", _plan_everyone, N_DEV - 1)]),
                     (buf_a, [("agA_near", _plan_neighbours, 2), ("agA_pair_d2d", _plan_d2d((0,)), 1)])])
    mod_all, = _land(fl_mod, tok)
    buf_b = [cast(1, w_pool[0].reshape(-1, cg)), cast(2, w_a_up[0]), cast(3, w_b_up[0]), cast(4, w_o[0])]
    buf_c = [cast(5, w_ff1[0])]
    buf_e = [cast(6, w_ff2[0])]
    mod_all = mod_all.reshape(N_DEV, N_DEV, wa)
    mod = lax.dynamic_slice_in_dim(mod_all, me, 1, axis=1).reshape(1, N_MOD * d)
    shift1, scale1, gate1, shift2, scale2, gate2 = [mod[:, i * d:(i + 1) * d] for i in range(N_MOD)]

    h = _norm_forward("norm1_fwd", x2, norm1_w, scale1, shift1)
    buf_a = _land(fl_pair, h, bufs=fl_a[-1])
    pair_slots = jnp.stack([me, me ^ bc]).astype(jnp.int32)
    proj = _project_slots("proj_pair", h, buf_a[0], pair_slots, 4 * d)
    buf_a = _land(fl_a, [proj] + buf_b + buf_c + buf_e, bufs=buf_a)
    near = (bx, by)
    (fl_far, fl_near), tok = _launch_groups(
        "agA_far", [(buf_a, [("agA_far", _plan_diagonal, 2), ("agA_near_d2d", _plan_d2d(near), len(near))])])
    (fl_b, fl_c, fl_e), tok = _launch_groups(
        "agBCE_ici0", [(buf_b, [gather_plans("B", len(buf_b), 0)]), (buf_c, [gather_plans("C", 1, 0)]),
                       (buf_e, [gather_plans("E", 1, 0)])])

    tm = _tile(s, 1024)
    tk = _tile(d, 2048)
    te = _tile(d, 512)

    own_slots = jnp.stack([me ^ m for m in near]).astype(jnp.int32)
    far_slots = jnp.stack([me ^ bx ^ by ^ f for f in (0, bc)]).astype(jnp.int32)
    buf_a = fl_near[-1]
    proj = _project_slots("proj_own", h, buf_a[0], own_slots, 4 * d, proj_in=proj, deps=(tok,))
    buf_a = _land(fl_near, proj, bufs=buf_a)
    proj = _project_slots("proj_sibling", h, buf_a[0], own_slots ^ bc, 4 * d, proj_in=proj)
    buf_a = _land(fl_far, proj, bufs=buf_a)
    fl_a, tok = _launch("agA_far_d2d", buf_a, _plan_d2d((bx | by,)), 1)
    w_in_f, = _land(fl_a, tok)
    proj = _project_slots("proj_far", h, w_in_f, far_slots, 4 * d, proj_in=proj)
    qn, kn, vb = _qkv_prepare(proj, q_norm_w, k_norm_w, half)
    buf_b = _land(fl_b, qn)
    buf_c = _land(fl_c, qn)
    (fl_b, fl_c), tok = _launch_groups(
        "agBC_ici1", [(buf_b, [gather_plans("B", len(buf_b), 1)]), (buf_c, [gather_plans("C", 1, 1)])])
    attn = _attention_forward(qn, kn, vb, deps=(tok,))
    buf_b = _land(fl_b, attn)
    buf_e = _land(fl_e, attn)
    (fl_b, fl_e), tok = _launch_groups(
        "agB_d2d_E_ici1", [(buf_b, [gather_plans("B", len(buf_b), 2)]), (buf_e, [gather_plans("E", 1, 1)])])
    w_pool_f, w_a_f, w_b_f, w_o_f = _land(fl_b, tok)
    rows_pool = cg // N_DEV
    w_pool_f = w_pool_f.reshape(N_DEV, n_groups, rows_pool, cg).transpose(1, 0, 2, 3).reshape(n_groups, cg, cg)
    w_o_f = w_o_f.reshape(d, d)
    pooled, ya_in = _pool_forward(proj, w_pool_f, pool_scale)

    def merge_epilogue(ga_ref, gb_ref, ya, yb, out_refs):
        merged_ref, ya_ref, yb_ref = out_refs
        merged = jax.nn.sigmoid(ga_ref[...]) * ya + jax.nn.sigmoid(gb_ref[...]) * yb
        merged_ref[...] = merged.astype(BF16)
        ya_ref[...] = ya.astype(BF16)
        yb_ref[...] = yb.astype(BF16)

    def up_body(a1_ref, b1_ref, a2_ref, b2_ref, ga_ref, gb_ref, *out_refs):
        merge_epilogue(ga_ref, gb_ref, _dot(a1_ref[...], b1_ref[...]), _dot(a2_ref[...], b2_ref[...]), out_refs)

    ga_blk0 = 2 * d // d8
    gb_blk0 = 3 * d // d8
    tu = s
    a_spec = pl.BlockSpec((tu, half), lambda i, j: (i, 0))
    wup_spec = pl.BlockSpec((None, half, d8), lambda i, j: (j, 0, 0))
    o_blk = pl.BlockSpec((tu, d8), lambda i, j: (i, j))
    merged, y_a, y_b = pl.pallas_call(
        up_body, name="up_merge", grid=(s // tu, N_DEV), out_shape=(_sds((s, d), BF16),) * 3,
        in_specs=[a_spec, wup_spec, a_spec, wup_spec,
                  pl.BlockSpec((tu, d8), lambda i, j: (i, ga_blk0 + j)),
                  pl.BlockSpec((tu, d8), lambda i, j: (i, gb_blk0 + j))],
        out_specs=(o_blk,) * 3, compiler_params=_params("parallel", "parallel"),
    )(ya_in, w_a_f, attn, w_b_f, proj, proj)
    buf_c = _land(fl_c, merged)
    fl_c, tok_c = gather_start("C", buf_c, 2)

    tn = _tile(d, 1024)

    def oproj_epilogue(acc, extra_refs, out_refs):
        x_ref, g_ref = extra_refs
        x1_ref, o_ref = out_refs
        x1_ref[...] = x_ref[...] + g_ref[...] * acc
        o_ref[...] = acc.astype(BF16)

    mn_blk = pl.BlockSpec((tm, tn), lambda i, j, k: (i, j))
    e_blk = pl.BlockSpec((tm, te), lambda i, j, k: (i, j))
    e_vec = pl.BlockSpec((1, te), lambda i, j, k: (0, j))
    s_blk = pl.BlockSpec((s, te), lambda i, j, k: (i, j))
    x1, o_act = _matmul(
        "oproj", "nn", (1, d // te, d // tk), merged, pl.BlockSpec((s, tk), lambda i, j, k: (i, k)),
        w_o_f, pl.BlockSpec((tk, te), lambda i, j, k: (k, j)),
        [_sds((s, d), F32), _sds((s, d), BF16)], [s_blk, s_blk], (s, te),
        epilogue=oproj_epilogue, extras=(x2, gate1), extra_specs=(s_blk, e_vec), deps=(tok_c,))

    h2 = _norm_forward("norm2_fwd", x1, norm2_w, scale2, shift2)
    w_ff1_f, = _land(fl_c, h2)

    def ff1_epilogue(acc, extra_refs, out_refs):
        r = jnp.maximum(acc, 0.0)
        out_refs[0][...] = r.astype(BF16)
        out_refs[1][...] = (r * r).astype(BF16)

    n_rows = s // tm
    first = max(n_rows // 2, 1)

    def ff1_rows(name, row0, rows, prior, deps):
        blk = pl.BlockSpec((tm, half), lambda i, j, k: (i + row0, j))
        return _matmul(
            name, "nn", (rows, N_DEV, d // tk), h2, pl.BlockSpec((tm, tk), lambda i, j, k: (i + row0, k)),
            w_ff1_f, pl.BlockSpec((None, tk, half), lambda i, j, k: (j, k, 0)),
            [_sds((s, 4 * d), BF16)] * 2, [blk, blk], (tm, half), epilogue=ff1_epilogue,
            extras=prior, extra_specs=[_ANY] * len(prior), aliases={2 + n: n for n in range(len(prior))}, deps=deps)

    relu, act = ff1_rows("ff1_a", 0, first, (), ())
    if n_rows > first:
        buf_e = _land(fl_e, act)
        fl_e, tok_e = gather_start("E", buf_e, 2)
        relu, act = ff1_rows("ff1_b", first, n_rows - first, (relu, act), (tok_e,))
    else:
        fl_e, tok_e = gather_start("E", _land(fl_e, act), 2)
    w_ff2_f, = _land(fl_e, act if n_rows > first else tok_e)
    w_ff2_f = w_ff2_f.reshape(4 * d, d)

    def ff2_epilogue(acc, extra_refs, out_refs):
        x1_ref, g_ref, t_ref = extra_refs
        df_ref, dy_ref, sq_ref, dgate_ref = out_refs
        gate = g_ref[...]
        err = x1_ref[...] + gate * acc - t_ref[...]
        dyv = err * (1.0 / d)
        dy_ref[...] = dyv
        df_ref[...] = (dyv * gate).astype(BF16)
        sq_ref[...] = jnp.full(sq_ref.shape, jnp.sum(err * err), F32)
        dgate_ref[...] = jnp.broadcast_to(jnp.sum(dyv * acc, axis=0, keepdims=True), dgate_ref.shape)

    df, dy, sq, dgate2_parts = _matmul(
        "ff2", "nn", (s // tm, d // te, 2 * d // tk), act, pl.BlockSpec((tm, 2 * tk), lambda i, j, k: (i, k)),
        w_ff2_f, pl.BlockSpec((2 * tk, te), lambda i, j, k: (k, j)),
        [_sds((s, d), BF16), _sds((s, d), F32), _sds((s // tm * SUBLANES, d // te * LANES), F32),
         _sds((s // tm * SUBLANES, d), F32)],
        [e_blk, e_blk, pl.BlockSpec((SUBLANES, LANES), lambda i, j, k: (i, j)),
         pl.BlockSpec((SUBLANES, te), lambda i, j, k: (i, j))], (tm, te),
        epilogue=ff2_epilogue, extras=(x1, gate2, target), extra_specs=(e_blk, e_vec, e_blk))
    loss_local = (0.5 / d) * jnp.sum(sq[::SUBLANES, ::LANES])
    dgate2 = jnp.sum(dgate2_parts[::SUBLANES], axis=0, keepdims=True)

    tok_k = _tile(s, 2048)
    tw = _tile(d, 1024)
    g_ff2 = _matmul(
        "g_ff2", "tn", (4 * d // tw, d // tn, s // tok_k), act, pl.BlockSpec((tok_k, tw), lambda i, j, k: (k, i)),
        df, pl.BlockSpec((tok_k, tn), lambda i, j, k: (k, j)),
        [_sds((4 * d, d), BF16)], [pl.BlockSpec((tw, tn), lambda i, j, k: (i, j))], (tw, tn))[0]

    def da_epilogue(acc, extra_refs, out_refs):
        out_refs[0][...] = (acc * (2.0 * extra_refs[0][...].astype(F32))).astype(BF16)

    big_blk = pl.BlockSpec((tm, tn), lambda i, j, k: (i, j))
    fl_f2, tok = _reduce_scatter_start("F2", [g_ff2.reshape(N_DEV, half, d)])
    df1 = _matmul(
        "da_ff", "nt", (s // tm, 4 * d // tn, d // tk), df, pl.BlockSpec((tm, tk), lambda i, j, k: (i, k)),
        w_ff2_f, pl.BlockSpec((tn, tk), lambda i, j, k: (j, k)),
        [_sds((s, 4 * d), BF16)], [big_blk], (tm, tn),
        epilogue=da_epilogue, extras=(relu,), extra_specs=(big_blk,), deps=(tok,))[0]

    buf_f2 = _reduce_scatter_add("F2", fl_f2, df1, me)
    g_ff1 = _matmul(
        "g_ff1", "tn", (d // tw, N_DEV, s // tok_k), h2, pl.BlockSpec((tok_k, tw), lambda i, j, k: (k, i)),
        df1, pl.BlockSpec((tok_k, half), lambda i, j, k: (k, j)),
        [_sds((N_DEV, d, half), BF16)], [pl.BlockSpec((None, tw, half), lambda i, j, k: (j, i, 0))], (tw, half))[0]

    fl_f2, fl_f1, tok = _reduce_scatter_handover("F2", buf_f2, "F1", [g_ff1])
    dh2 = _matmul(
        "dh2", "nt", (s // tm, d // tn, N_DEV // 2), df1, pl.BlockSpec((tm, 2 * half), lambda i, j, k: (i, k)),
        w_ff1_f, pl.BlockSpec((2, tn, half), lambda i, j, k: (k, j, 0)),
        [_sds((s, d), F32)], [mn_blk], (tm, tn), deps=(tok,))[0]

    buf_f1 = _reduce_scatter_add("F1", fl_f1, dh2, me)
    dx1, dshift2, dscale2, g_norm2, do, dgate1 = _norm_backward(
        "norm2_bwd", dh2, x1, norm2_w, scale2, dy, gated=(gate1, o_act))

    g_o = _matmul(
        "g_o", "tn", (d // tw, d // tn, s // tok_k), merged, pl.BlockSpec((tok_k, tw), lambda i, j, k: (k, i)),
        do, pl.BlockSpec((tok_k, tn), lambda i, j, k: (k, j)),
        [_sds((d, d), BF16)], [pl.BlockSpec((tw, tn), lambda i, j, k: (i, j))], (tw, tn))[0]

    def merge_bwd_epilogue(acc, extra_refs, out_refs):
        ga_ref, gb_ref, ya_ref, yb_ref = extra_refs
        dya_ref, dyb_ref, dga_ref, dgb_ref = out_refs
        sa = jax.nn.sigmoid(ga_ref[...])
        sb = jax.nn.sigmoid(gb_ref[...])
        dya_ref[...] = (acc * sa).astype(BF16)
        dyb_ref[...] = (acc * sb).astype(BF16)
        dga_ref[...] = (acc * ya_ref[...].astype(F32) * (sa * (1.0 - sa))).astype(BF16)
        dgb_ref[...] = (acc * yb_ref[...].astype(F32) * (sb * (1.0 - sb))).astype(BF16)

    td = _tile(d, 256)
    nb = d // td
    d_blk = pl.BlockSpec((s, td), lambda i, j, k: (i, j))
    dy_a, dy_b, dproj, dg_b = _matmul(
        "dmerged", "nt", (1, nb, d // tk), do, pl.BlockSpec((s, tk), lambda i, j, k: (i, k)),
        w_o_f, pl.BlockSpec((td, tk), lambda i, j, k: (j, k)),
        [_sds((s, d), BF16), _sds((s, d), BF16), _sds((s, 4 * d), BF16), _sds((s, d), BF16)],
        [d_blk, d_blk, pl.BlockSpec((s, td), lambda i, j, k: (i, 2 * nb + j)), d_blk], (s, td),
        epilogue=merge_bwd_epilogue, extras=(proj, proj, y_a, y_b),
        extra_specs=(pl.BlockSpec((s, td), lambda i, j, k: (i, 2 * nb + j)),
                     pl.BlockSpec((s, td), lambda i, j, k: (i, 3 * nb + j)), d_blk, d_blk))
    dproj = _place_columns("place_dgb", dg_b, dproj, 3)

    up_a = pl.BlockSpec((tok_k, half), lambda i, j, k: (k, 0))
    up_b = pl.BlockSpec((tok_k, d8), lambda i, j, k: (k, j))
    up_o = pl.BlockSpec((None, half, d8), lambda i, j, k: (j, 0, 0))
    g_a_up = _matmul("g_a_up", "tn", (1, N_DEV, s // tok_k), ya_in, up_a, dy_a, up_b,
                     [_sds((N_DEV, half, d8), BF16)], [up_o], (half, d8))[0]
    g_b_up = _matmul("g_b_up", "tn", (1, N_DEV, s // tok_k), attn, up_a, dy_b, up_b,
                     [_sds((N_DEV, half, d8), BF16)], [up_o], (half, d8))[0]
    slabs = 4
    dn_a = pl.BlockSpec((tm, slabs * d8), lambda i, j, k: (i, k))
    dn_b = pl.BlockSpec((slabs, half, d8), lambda i, j, k: (k, 0, 0))
    dn_o = pl.BlockSpec((tm, half), lambda i, j, k: (i, 0))
    dya_in = _matmul("d_ya_in", "nt", (s // tm, 1, N_DEV // slabs), dy_a, dn_a, w_a_f, dn_b,
                     [_sds((s, half), BF16)], [dn_o], (tm, half))[0]
    dattn = _matmul("d_attn", "nt", (s // tm, 1, N_DEV // slabs), dy_b, dn_a, w_b_f, dn_b,
                    [_sds((s, half), BF16)], [dn_o], (tm, half))[0]

    dproj, g_pool, g_pool_scale = _pool_backward(dya_in, pooled, w_pool_f, pool_scale, dproj)
    sum_ff2, = _reduce_scatter_finish(fl_f2, g_pool)
    g_pool_send = g_pool.astype(BF16).reshape(n_groups, N_DEV, rows_pool, cg).transpose(1, 0, 2, 3)
    g_pool_send = g_pool_send.reshape(N_DEV, n_groups * rows_pool, cg)
    fl_f1, fl_b, tok = _reduce_scatter_handover(
        "F1", buf_f1, "B", [g_pool_send, g_a_up, g_b_up, g_o.reshape(N_DEV, d8, d)])
    dqn, dkn, dproj = _attention_backward(qn, kn, vb, dattn, dproj, 3, deps=(tok,))
    sum_ff1, = _reduce_scatter_finish(fl_f1, dqn)
    buf_b = _reduce_scatter_add("B", fl_b, dqn, me)
    dproj, g_qnorm = _qk_norm_backward("qnorm_bwd", dqn, proj, 1, q_norm_w, dproj, half)
    dproj, g_knorm = _qk_norm_backward("knorm_bwd", dkn, proj, 2, k_norm_w, dproj, half)

    g_in = _matmul(
        "g_in", "tn", (d // tw, N_DEV, s // tok_k), h, pl.BlockSpec((tok_k, tw), lambda i, j, k: (k, i)),
        dproj, pl.BlockSpec((tok_k, half), lambda i, j, k: (k, j)),
        [_sds((N_DEV, d, half), BF16)], [pl.BlockSpec((None, tw, half), lambda i, j, k: (j, i, 0))], (tw, half))[0]
    fl_b, fl_in, tok = _reduce_scatter_handover("B", buf_b, "I", [g_in])
    dh = _matmul(
        "dh", "nt", (s // tm, d // tn, N_DEV // 2), dproj, pl.BlockSpec((tm, 2 * half), lambda i, j, k: (i, k)),
        w_in_f, pl.BlockSpec((2, tn, half), lambda i, j, k: (k, j, 0)),
        [_sds((s, d), F32)], [mn_blk], (tm, tn), deps=(tok,))[0]
    buf_in = _reduce_scatter_add("I", fl_in, dh, me, n_land=2)
    grad_x, dshift1, dscale1, g_norm1 = _norm_backward("norm1_bwd", dh, x2, norm1_w, scale1, dx1, deps=tuple(buf_in[:1]))
    sum_pool, sum_a_up, sum_b_up, sum_o = _reduce_scatter_finish(fl_b, grad_x)

    dmod = jnp.concatenate([dshift1, dscale1, dgate1, dshift2, dscale2, dgate2], axis=1)
    pieces = [dmod, g_norm1, g_norm2, g_pool_scale, g_qnorm, g_knorm, jnp.full((1, LANES), loss_local, F32)]
    packed_rows = [_rows_of_lanes(p) for p in pieces]
    offsets = [0]
    for p in packed_rows:
        offsets.append(offsets[-1] + p.shape[0])
    packed = jnp.concatenate(packed_rows, axis=0)
    small_buf = lax.dynamic_update_slice(jnp.zeros((N_DEV,) + packed.shape, F32), packed[None], (me, 0, 0))
    (fl_small, fl_in), tok = _launch_groups(
        "rsI_relay0_small", [([small_buf], [("ag_small", _plan_everyone, N_DEV - 1)]),
                             (buf_in, [("rsI_relay0", _plan_scatter_relay_first, 4)])])
    small_all, = _land(fl_small, tok)
    small_sum = _sum_slots("small_sum", small_all[None], F32)[0]

    def unpack(i, width):
        return small_sum[offsets[i]:offsets[i] + width // LANES].reshape(1, width)

    g_b_ada = unpack(0, N_MOD * d)
    g_norm1_w = unpack(1, d)
    g_norm2_w = unpack(2, d)
    g_pool_scale_w = unpack(3, half)
    g_q_norm_w = unpack(4, HEAD_DIM)
    g_k_norm_w = unpack(5, HEAD_DIM)
    loss = unpack(6, LANES)[0, 0]
    dmod_all = small_all[:, :N_MOD * d // LANES].reshape(N_DEV, N_MOD * d)
    dmod_cols = lax.dynamic_slice_in_dim(dmod_all, me * wa, wa, axis=1)

    grads = {
        "b_ada": g_b_ada, "norm1_w": g_norm1_w,
        "q_norm_w": g_q_norm_w, "k_norm_w": g_k_norm_w,
        "pool_scale": g_pool_scale_w, "norm2_w": g_norm2_w,
    }
    sums = {"w_pool": sum_pool, "w_a_up": sum_a_up, "w_b_up": sum_b_up, "w_o": sum_o,
            "w_ff1": sum_ff1, "w_ff2": sum_ff2}
    weights = {"w_ada": (w_ada, m_w_ada, v_w_ada), "b_ada": (b_ada, m_b_ada, v_b_ada),
               "norm1_w": (norm1_w, m_norm1_w, v_norm1_w), "w_in": (w_in, m_w_in, v_w_in),
               "q_norm_w": (q_norm_w, m_q_norm_w, v_q_norm_w), "k_norm_w": (k_norm_w, m_k_norm_w, v_k_norm_w),
               "w_pool": (w_pool, m_w_pool, v_w_pool), "pool_scale": (pool_scale, m_pool_scale, v_pool_scale),
               "w_a_up": (w_a_up, m_w_a_up, v_w_a_up), "w_b_up": (w_b_up, m_w_b_up, v_w_b_up),
               "w_o": (w_o, m_w_o, v_w_o), "norm2_w": (norm2_w, m_norm2_w, v_norm2_w),
               "w_ff1": (w_ff1, m_w_ff1, v_w_ff1), "w_ff2": (w_ff2, m_w_ff2, v_w_ff2)}
    order = list(weights)
    deltas, new_m, new_v = {}, {}, {}
    def adam(name, deps=()):
        wt, mt, vt = weights[name]
        shape = wt.shape
        flat = (-1, shape[-1])
        if name in sums:
            own, received = sums[name]
            g, dl, nm, nv = _adamw_summed("adamw_" + name, wt.reshape(flat), own, received,
                                          mt.reshape(flat), vt.reshape(flat), deps=deps)
            grads[name] = g.reshape(shape)
        else:
            update = _adamw_streamed if wt.size // shape[-1] >= 8 * SUBLANES else _adamw
            dl, nm, nv = update("adamw_" + name, wt.reshape(flat), grads[name].reshape(flat),
                                mt.reshape(flat), vt.reshape(flat), deps=deps)
        deltas[name], new_m[name], new_v[name] = dl.reshape(shape), nm.reshape(shape), nv.reshape(shape)

    behind_first = list(sums)
    for name in behind_first:
        adam(name, deps=(tok,))
    grads["w_ada"] = _ada_weight_grad(c_all, dmod_cols, deps=(tok,))[None]
    chip_sums, relayed = _land(fl_in, [deltas[n] for n in behind_first] + [grads["w_ada"]])
    passed_on = _add_relayed("rsI_add_relay", chip_sums, relayed)
    fl_in, tok = _launch("rsI_relay1", [passed_on, lax.empty((1,) + passed_on.shape[1:], passed_on.dtype)],
                         _plan_scatter_relay_second, 2)
    behind_second = [n for n in order if n not in sums and n != "w_in"]
    for name in behind_second:
        adam(name, deps=(tok,))
    sums["w_in"] = tuple(_land(fl_in, [deltas[n] for n in behind_second]))
    adam("w_in")

    return (loss, grad_x[None], *[grads[n] for n in order], *[deltas[n] for n in order],
            *[new_m[n] for n in order], *[new_v[n] for n in order])
```
